```python
import math
import jax, jax.numpy as jnp
from jax import lax
import numpy as np

D_MODEL = 1024
BATCH = 8
SEQ = 4096
DEPTH = 1

ATTN_HEAD_DIM = 64
ATTN_PATTERNS = ((128, 1), (512, 4), (2048, 16))
HEADS_PER_PATTERN = 8
N_ATTN_HEADS = HEADS_PER_PATTERN * len(ATTN_PATTERNS)
ATTN_QKV = N_ATTN_HEADS * ATTN_HEAD_DIM
ATTN_OUT = HEADS_PER_PATTERN * ATTN_HEAD_DIM
ATTN_BLOCK = 128
ALIBI_MAX_EXP = 8.0
SSD_EXPAND = 2
SSD_INNER = SSD_EXPAND * D_MODEL
SSD_HEAD_DIM = 64
SSD_HEADS = SSD_INNER // SSD_HEAD_DIM
SSD_STATE = 128
SSD_GROUPS = 4
SSD_CONV = 4
SSD_CHUNK = 128
SSD_CONV_DIM = SSD_INNER + 2 * SSD_GROUPS * SSD_STATE
D_FF = 2816
EPS = 1e-6
IN_COLS = 3 * ATTN_QKV + SSD_INNER + SSD_CONV_DIM + SSD_HEADS + 2 * D_MODEL

kernel_name = "hybrid_dilated_attn_ssd_macaron"


def rmsnorm(x, g):
    x32 = x.astype(jnp.float32)
    y = x32 * lax.rsqrt(jnp.mean(x32 * x32, axis=-1, keepdims=True) + EPS)
    return (y * g.astype(jnp.float32)).astype(x.dtype)


def swiglu(h, w_gate, w_up, w_down):
    return (jax.nn.silu(h @ w_gate) * (h @ w_up)) @ w_down


def alibi_slopes(n):
    return jnp.exp2(-ALIBI_MAX_EXP * jnp.arange(1, n + 1, dtype=jnp.float32) / n)


def dilated_window_attention(q, k, v, slopes, window, dilation):
    b, S, H, hd = q.shape
    L = S // dilation
    n_back = window // dilation
    nb = -(-L // ATTN_BLOCK)
    Lp = nb * ATTN_BLOCK

    def to_blocks(t):
        t = t.reshape(b, L, dilation, H, hd).transpose(0, 2, 1, 3, 4)
        t = jnp.pad(t, ((0, 0), (0, 0), (0, Lp - L), (0, 0), (0, 0)))
        return t.reshape(b, dilation, nb, ATTN_BLOCK, H, hd)

    def with_prev(t):
        prev = jnp.pad(t, ((0, 0), (0, 0), (1, 0), (0, 0), (0, 0), (0, 0)))[:, :, :-1]
        return jnp.concatenate([prev, t], axis=3)

    qb = to_blocks(q)
    kk = with_prev(to_blocks(k))
    vv = with_prev(to_blocks(v))
    scale = 1.0 / math.sqrt(hd)
    logits = jnp.einsum('bdnqhe,bdnkhe->bdnhqk', qb, kk).astype(jnp.float32) * scale

    a_idx = jnp.arange(ATTN_BLOCK)[:, None]
    c_idx = jnp.arange(2 * ATTN_BLOCK)[None, :]
    rel = ATTN_BLOCK + a_idx - c_idx
    band = (rel >= 0) & (rel <= n_back)
    key_pos = jnp.arange(nb)[:, None] * ATTN_BLOCK + jnp.arange(2 * ATTN_BLOCK)[None, :] - ATTN_BLOCK
    mask = band[None] & (key_pos >= 0)[:, None, :]
    bias = -slopes[:, None, None] * (rel * dilation).astype(jnp.float32)[None]
    logits = jnp.where(mask[None, None, :, None], logits + bias[None, None, None], -jnp.inf)

    m = jnp.max(logits, axis=-1, keepdims=True)
    p = jnp.exp(logits - m)
    l = jnp.sum(p, axis=-1, keepdims=True)
    o = jnp.einsum('bdnhqk,bdnkhe->bdnqhe', p / l, vv.astype(jnp.float32))
    lse = (m + jnp.log(l))[..., 0]

    o = o.reshape(b, dilation, Lp, H, hd)[:, :, :L].transpose(0, 2, 1, 3, 4).reshape(b, S, H, hd)
    lse = lse.transpose(0, 1, 2, 4, 3).reshape(b, dilation, Lp, H)[:, :, :L]
    lse = lse.transpose(0, 2, 1, 3).reshape(b, S, H)
    return o, lse


def attention_branch(q, k, v, q_gain, k_gain):
    b, S, _ = q.shape
    q = rmsnorm(q.reshape(b, S, N_ATTN_HEADS, ATTN_HEAD_DIM), q_gain)
    k = rmsnorm(k.reshape(b, S, N_ATTN_HEADS, ATTN_HEAD_DIM), k_gain)
    v = v.reshape(b, S, N_ATTN_HEADS, ATTN_HEAD_DIM)
    slopes = alibi_slopes(N_ATTN_HEADS)
    outs, lses = [], []
    for g, (window, dilation) in enumerate(ATTN_PATTERNS):
        hs = slice(g * HEADS_PER_PATTERN, (g + 1) * HEADS_PER_PATTERN)
        o, lse = dilated_window_attention(q[:, :, hs], k[:, :, hs], v[:, :, hs],
                                          slopes[hs], window, dilation)
        outs.append(o)
        lses.append(lse)
    w = jax.nn.softmax(jnp.stack(lses, axis=0), axis=0)
    o = jnp.sum(w[..., None] * jnp.stack(outs, axis=0), axis=0)
    return o.reshape(b, S, ATTN_OUT).astype(q.dtype)


def causal_depthwise_conv(u, w, bias):
    C = u.shape[-1]
    y = lax.conv_general_dilated(u, w[:, None, :], window_strides=(1,),
                                 padding=[(SSD_CONV - 1, 0)],
                                 dimension_numbers=('NWC', 'WIO', 'NWC'),
                                 feature_group_count=C)
    return y + bias


def ssd_chunked(x, a, Bm, Cm):
    b, S, H, P = x.shape
    G, N = Bm.shape[2], Bm.shape[3]
    J = H // G
    nc = S // SSD_CHUNK
    Q = SSD_CHUNK
    x = x.reshape(b, nc, Q, G, J, P)
    a = a.reshape(b, nc, Q, G, J).transpose(0, 1, 3, 4, 2)
    Bm = Bm.reshape(b, nc, Q, G, N)
    Cm = Cm.reshape(b, nc, Q, G, N)
    a_cs = jnp.cumsum(a, axis=-1)
    diff = a_cs[..., :, None] - a_cs[..., None, :]
    causal = jnp.tril(jnp.ones((Q, Q), dtype=bool))
    decay = jnp.exp(jnp.where(causal, diff, -jnp.inf))
    CB = jnp.einsum('bclgn,bcsgn->bcgls', Cm, Bm)
    y_diag = jnp.einsum('bcgjls,bcsgjp->bclgjp', CB[:, :, :, None] * decay, x)
    decay_states = jnp.exp(a_cs[..., -1:] - a_cs)
    states = jnp.einsum('bclgn,bcgjl,bclgjp->bcgjpn', Bm, decay_states, x)
    chunk_decay = jnp.exp(a_cs[..., -1])

    def step(carry, inp):
        st, dec = inp
        return carry * dec[..., None, None] + st, carry

    init = jnp.zeros((b, G, J, P, N), dtype=x.dtype)
    _, prev_states = lax.scan(step, init, (states.transpose(1, 0, 2, 3, 4, 5),
                                           chunk_decay.transpose(1, 0, 2, 3)))
    prev_states = prev_states.transpose(1, 0, 2, 3, 4, 5)
    y_off = jnp.einsum('bclgn,bcgjpn,bcgjl->bclgjp', Cm, prev_states, jnp.exp(a_cs))
    return (y_diag + y_off).reshape(b, S, H, P)


def ssd_branch(z, xBC, dt_raw, conv_w, conv_b, dt_bias, a_log, d_skip, ssd_norm):
    b, S, _ = z.shape
    xBC = jax.nn.silu(causal_depthwise_conv(xBC, conv_w, conv_b))
    xs, Bm, Cm = jnp.split(xBC, [SSD_INNER, SSD_INNER + SSD_GROUPS * SSD_STATE], axis=-1)
    xs = xs.reshape(b, S, SSD_HEADS, SSD_HEAD_DIM).astype(jnp.float32)
    Bm = Bm.reshape(b, S, SSD_GROUPS, SSD_STATE).astype(jnp.float32)
    Cm = Cm.reshape(b, S, SSD_GROUPS, SSD_STATE).astype(jnp.float32)
    dt = jax.nn.softplus(dt_raw.astype(jnp.float32) + dt_bias.astype(jnp.float32))
    A = -jnp.exp(a_log.astype(jnp.float32))
    y = ssd_chunked(xs * dt[..., None], dt * A, Bm, Cm)
    y = y + xs * d_skip.astype(jnp.float32)[:, None]
    y = y.reshape(b, S, SSD_INNER).astype(z.dtype) * jax.nn.silu(z)
    y = rmsnorm(y.reshape(b, S, SSD_GROUPS, SSD_INNER // SSD_GROUPS),
                ssd_norm.reshape(SSD_GROUPS, SSD_INNER // SSD_GROUPS))
    return y.reshape(b, S, SSD_INNER)


def _fwd_setup_inputs(seed: int = 0) -> dict:
    key = jax.random.key(seed)
    ks = jax.random.split(key, 24)
    f32 = jnp.float32

    def normal(k, shape, scale):
        return jax.random.normal(k, shape, f32) * scale

    def gain(k, n):
        return 1.0 + 0.02 * jax.random.normal(k, (DEPTH, n), f32)

    dt = jnp.exp(jax.random.uniform(ks[12], (DEPTH, SSD_HEADS), f32,
                                    minval=math.log(1e-3), maxval=math.log(1e-1)))
    return {
        "x": normal(ks[0], (BATCH, SEQ, D_MODEL), 1.0),
        "ffn1_norm": gain(ks[1], D_MODEL),
        "ffn1_w_gate": normal(ks[2], (DEPTH, D_MODEL, D_FF), D_MODEL ** -0.5),
        "ffn1_w_up": normal(ks[3], (DEPTH, D_MODEL, D_FF), D_MODEL ** -0.5),
        "ffn1_w_down": normal(ks[4], (DEPTH, D_FF, D_MODEL), D_FF ** -0.5),
        "mix_norm": gain(ks[5], D_MODEL),
        "w_in": normal(ks[6], (DEPTH, D_MODEL, IN_COLS), D_MODEL ** -0.5),
        "q_norm": gain(ks[7], ATTN_HEAD_DIM),
        "k_norm": gain(ks[8], ATTN_HEAD_DIM),
        "conv_w": normal(ks[9], (DEPTH, SSD_CONV, SSD_CONV_DIM), SSD_CONV ** -0.5),
        "conv_b": normal(ks[10], (DEPTH, SSD_CONV_DIM), 0.02),
        "dt_bias": dt + jnp.log(-jnp.expm1(-dt)),
        "a_log": jnp.log(jax.random.uniform(ks[13], (DEPTH, SSD_HEADS), f32, minval=1.0, maxval=16.0)),
        "d_skip": 1.0 + 0.1 * jax.random.normal(ks[14], (DEPTH, SSD_HEADS), f32),
        "ssd_norm": gain(ks[15], SSD_INNER),
        "w_attn_branch": normal(ks[16], (DEPTH, ATTN_OUT, D_MODEL), ATTN_OUT ** -0.5),
        "w_ssd_branch": normal(ks[17], (DEPTH, SSD_INNER, D_MODEL), SSD_INNER ** -0.5),
        "w_out": normal(ks[18], (DEPTH, D_MODEL, D_MODEL), D_MODEL ** -0.5),
        "ffn2_norm": gain(ks[19], D_MODEL),
        "ffn2_w_gate": normal(ks[20], (DEPTH, D_MODEL, D_FF), D_MODEL ** -0.5),
        "ffn2_w_up": normal(ks[21], (DEPTH, D_MODEL, D_FF), D_MODEL ** -0.5),
        "ffn2_w_down": normal(ks[22], (DEPTH, D_FF, D_MODEL), D_FF ** -0.5),
    }


def _fwd_reference(x, ffn1_norm, ffn1_w_gate, ffn1_w_up, ffn1_w_down, mix_norm, w_in,
              q_norm, k_norm, conv_w, conv_b, dt_bias, a_log, d_skip, ssd_norm,
              w_attn_branch, w_ssd_branch, w_out, ffn2_norm, ffn2_w_gate, ffn2_w_up,
              ffn2_w_down):
    split_at = np.cumsum([ATTN_QKV, ATTN_QKV, ATTN_QKV, SSD_INNER, SSD_CONV_DIM,
                          SSD_HEADS, D_MODEL]).tolist()
    for l in range(DEPTH):
        x = x + 0.5 * swiglu(rmsnorm(x, ffn1_norm[l]), ffn1_w_gate[l], ffn1_w_up[l], ffn1_w_down[l])
        h = rmsnorm(x, mix_norm[l])
        proj = h @ w_in[l]
        q, k, v, z, xBC, dt_raw, g_attn, g_ssd = jnp.split(proj, split_at, axis=-1)
        a = attention_branch(q, k, v, q_norm[l], k_norm[l]) @ w_attn_branch[l]
        s = ssd_branch(z, xBC, dt_raw, conv_w[l], conv_b[l], dt_bias[l], a_log[l],
                       d_skip[l], ssd_norm[l]) @ w_ssd_branch[l]
        merged = jax.nn.sigmoid(g_attn) * a + jax.nn.sigmoid(g_ssd) * s
        x = x + merged @ w_out[l]
        x = x + 0.5 * swiglu(rmsnorm(x, ffn2_norm[l]), ffn2_w_gate[l], ffn2_w_up[l], ffn2_w_down[l])
    return x


import jax as _jax
import jax.numpy as _jnp

TWIN_FORMAT = 'train_step'
FWD_PARAMS = ['x', 'ffn1_norm', 'ffn1_w_gate', 'ffn1_w_up', 'ffn1_w_down', 'mix_norm', 'w_in', 'q_norm', 'k_norm', 'conv_w', 'conv_b', 'dt_bias', 'a_log', 'd_skip', 'ssd_norm', 'w_attn_branch', 'w_ssd_branch', 'w_out', 'ffn2_norm', 'ffn2_w_gate', 'ffn2_w_up', 'ffn2_w_down']
TWIN_WEIGHTS = ['ffn1_norm', 'ffn1_w_gate', 'ffn1_w_up', 'ffn1_w_down', 'mix_norm', 'w_in', 'q_norm', 'k_norm', 'conv_w', 'conv_b', 'dt_bias', 'a_log', 'd_skip', 'ssd_norm', 'w_attn_branch', 'w_ssd_branch', 'w_out', 'ffn2_norm', 'ffn2_w_gate', 'ffn2_w_up', 'ffn2_w_down']
TWIN_DIFF_INPUT = 'x'
TWIN_INPUTS = ['x', 'ffn1_norm', 'ffn1_w_gate', 'ffn1_w_up', 'ffn1_w_down', 'mix_norm', 'w_in', 'q_norm', 'k_norm', 'conv_w', 'conv_b', 'dt_bias', 'a_log', 'd_skip', 'ssd_norm', 'w_attn_branch', 'w_ssd_branch', 'w_out', 'ffn2_norm', 'ffn2_w_gate', 'ffn2_w_up', 'ffn2_w_down', 'loss_target', 'm_ffn1_norm', 'm_ffn1_w_gate', 'm_ffn1_w_up', 'm_ffn1_w_down', 'm_mix_norm', 'm_w_in', 'm_q_norm', 'm_k_norm', 'm_conv_w', 'm_conv_b', 'm_dt_bias', 'm_a_log', 'm_d_skip', 'm_ssd_norm', 'm_w_attn_branch', 'm_w_ssd_branch', 'm_w_out', 'm_ffn2_norm', 'm_ffn2_w_gate', 'm_ffn2_w_up', 'm_ffn2_w_down', 'v_ffn1_norm', 'v_ffn1_w_gate', 'v_ffn1_w_up', 'v_ffn1_w_down', 'v_mix_norm', 'v_w_in', 'v_q_norm', 'v_k_norm', 'v_conv_w', 'v_conv_b', 'v_dt_bias', 'v_a_log', 'v_d_skip', 'v_ssd_norm', 'v_w_attn_branch', 'v_w_ssd_branch', 'v_w_out', 'v_ffn2_norm', 'v_ffn2_w_gate', 'v_ffn2_w_up', 'v_ffn2_w_down']
TWIN_OUTPUTS = ['loss', 'grad_x', 'grad_ffn1_norm', 'grad_ffn1_w_gate', 'grad_ffn1_w_up', 'grad_ffn1_w_down', 'grad_mix_norm', 'grad_w_in', 'grad_q_norm', 'grad_k_norm', 'grad_conv_w', 'grad_conv_b', 'grad_dt_bias', 'grad_a_log', 'grad_d_skip', 'grad_ssd_norm', 'grad_w_attn_branch', 'grad_w_ssd_branch', 'grad_w_out', 'grad_ffn2_norm', 'grad_ffn2_w_gate', 'grad_ffn2_w_up', 'grad_ffn2_w_down', 'delta_ffn1_norm', 'delta_ffn1_w_gate', 'delta_ffn1_w_up', 'delta_ffn1_w_down', 'delta_mix_norm', 'delta_w_in', 'delta_q_norm', 'delta_k_norm', 'delta_conv_w', 'delta_conv_b', 'delta_dt_bias', 'delta_a_log', 'delta_d_skip', 'delta_ssd_norm', 'delta_w_attn_branch', 'delta_w_ssd_branch', 'delta_w_out', 'delta_ffn2_norm', 'delta_ffn2_w_gate', 'delta_ffn2_w_up', 'delta_ffn2_w_down', 'new_m_ffn1_norm', 'new_m_ffn1_w_gate', 'new_m_ffn1_w_up', 'new_m_ffn1_w_down', 'new_m_mix_norm', 'new_m_w_in', 'new_m_q_norm', 'new_m_k_norm', 'new_m_conv_w', 'new_m_conv_b', 'new_m_dt_bias', 'new_m_a_log', 'new_m_d_skip', 'new_m_ssd_norm', 'new_m_w_attn_branch', 'new_m_w_ssd_branch', 'new_m_w_out', 'new_m_ffn2_norm', 'new_m_ffn2_w_gate', 'new_m_ffn2_w_up', 'new_m_ffn2_w_down', 'new_v_ffn1_norm', 'new_v_ffn1_w_gate', 'new_v_ffn1_w_up', 'new_v_ffn1_w_down', 'new_v_mix_norm', 'new_v_w_in', 'new_v_q_norm', 'new_v_k_norm', 'new_v_conv_w', 'new_v_conv_b', 'new_v_dt_bias', 'new_v_a_log', 'new_v_d_skip', 'new_v_ssd_norm', 'new_v_w_attn_branch', 'new_v_w_ssd_branch', 'new_v_w_out', 'new_v_ffn2_norm', 'new_v_ffn2_w_gate', 'new_v_ffn2_w_up', 'new_v_ffn2_w_down']
TWIN_LEAF_KINDS = {'loss': 'loss', 'grad_x': 'grad_x', 'grad_ffn1_norm': 'grad_w', 'grad_ffn1_w_gate': 'grad_w', 'grad_ffn1_w_up': 'grad_w', 'grad_ffn1_w_down': 'grad_w', 'grad_mix_norm': 'grad_w', 'grad_w_in': 'grad_w', 'grad_q_norm': 'grad_w', 'grad_k_norm': 'grad_w', 'grad_conv_w': 'grad_w', 'grad_conv_b': 'grad_w', 'grad_dt_bias': 'grad_w', 'grad_a_log': 'grad_w', 'grad_d_skip': 'grad_w', 'grad_ssd_norm': 'grad_w', 'grad_w_attn_branch': 'grad_w', 'grad_w_ssd_branch': 'grad_w', 'grad_w_out': 'grad_w', 'grad_ffn2_norm': 'grad_w', 'grad_ffn2_w_gate': 'grad_w', 'grad_ffn2_w_up': 'grad_w', 'grad_ffn2_w_down': 'grad_w', 'delta_ffn1_norm': 'delta_w', 'delta_ffn1_w_gate': 'delta_w', 'delta_ffn1_w_up': 'delta_w', 'delta_ffn1_w_down': 'delta_w', 'delta_mix_norm': 'delta_w', 'delta_w_in': 'delta_w', 'delta_q_norm': 'delta_w', 'delta_k_norm': 'delta_w', 'delta_conv_w': 'delta_w', 'delta_conv_b': 'delta_w', 'delta_dt_bias': 'delta_w', 'delta_a_log': 'delta_w', 'delta_d_skip': 'delta_w', 'delta_ssd_norm': 'delta_w', 'delta_w_attn_branch': 'delta_w', 'delta_w_ssd_branch': 'delta_w', 'delta_w_out': 'delta_w', 'delta_ffn2_norm': 'delta_w', 'delta_ffn2_w_gate': 'delta_w', 'delta_ffn2_w_up': 'delta_w', 'delta_ffn2_w_down': 'delta_w', 'new_m_ffn1_norm': 'new_m', 'new_m_ffn1_w_gate': 'new_m', 'new_m_ffn1_w_up': 'new_m', 'new_m_ffn1_w_down': 'new_m', 'new_m_mix_norm': 'new_m', 'new_m_w_in': 'new_m', 'new_m_q_norm': 'new_m', 'new_m_k_norm': 'new_m', 'new_m_conv_w': 'new_m', 'new_m_conv_b': 'new_m', 'new_m_dt_bias': 'new_m', 'new_m_a_log': 'new_m', 'new_m_d_skip': 'new_m', 'new_m_ssd_norm': 'new_m', 'new_m_w_attn_branch': 'new_m', 'new_m_w_ssd_branch': 'new_m', 'new_m_w_out': 'new_m', 'new_m_ffn2_norm': 'new_m', 'new_m_ffn2_w_gate': 'new_m', 'new_m_ffn2_w_up': 'new_m', 'new_m_ffn2_w_down': 'new_m', 'new_v_ffn1_norm': 'new_v', 'new_v_ffn1_w_gate': 'new_v', 'new_v_ffn1_w_up': 'new_v', 'new_v_ffn1_w_down': 'new_v', 'new_v_mix_norm': 'new_v', 'new_v_w_in': 'new_v', 'new_v_q_norm': 'new_v', 'new_v_k_norm': 'new_v', 'new_v_conv_w': 'new_v', 'new_v_conv_b': 'new_v', 'new_v_dt_bias': 'new_v', 'new_v_a_log': 'new_v', 'new_v_d_skip': 'new_v', 'new_v_ssd_norm': 'new_v', 'new_v_w_attn_branch': 'new_v', 'new_v_w_ssd_branch': 'new_v', 'new_v_w_out': 'new_v', 'new_v_ffn2_norm': 'new_v', 'new_v_ffn2_w_gate': 'new_v', 'new_v_ffn2_w_up': 'new_v', 'new_v_ffn2_w_down': 'new_v'}


def _forward(args):
    return _fwd_reference(*[args[k] for k in FWD_PARAMS])


def _output_shape():
    out = _jax.eval_shape(lambda: _forward(_fwd_setup_inputs(0)))
    return out.shape, out.dtype

N_MICROBATCH = 1
ADAM_LR = 0.001
ADAM_B1 = 0.9
ADAM_B2 = 0.999
ADAM_EPS = 1e-08
ADAM_WD = 0.01
ADAM_STEP = 10
PER_EXAMPLE_BATCH_AXIS = {'x': 0, 'loss_target': 0}
SHARED_INPUTS = []
_WEIGHT_DTYPES = {'ffn1_norm': _jnp.float32, 'ffn1_w_gate': _jnp.float32, 'ffn1_w_up': _jnp.float32, 'ffn1_w_down': _jnp.float32, 'mix_norm': _jnp.float32, 'w_in': _jnp.float32, 'q_norm': _jnp.float32, 'k_norm': _jnp.float32, 'conv_w': _jnp.float32, 'conv_b': _jnp.float32, 'dt_bias': _jnp.float32, 'a_log': _jnp.float32, 'd_skip': _jnp.float32, 'ssd_norm': _jnp.float32, 'w_attn_branch': _jnp.float32, 'w_ssd_branch': _jnp.float32, 'w_out': _jnp.float32, 'ffn2_norm': _jnp.float32, 'ffn2_w_gate': _jnp.float32, 'ffn2_w_up': _jnp.float32, 'ffn2_w_down': _jnp.float32}
MOMENT_SCALE = {'ffn1_norm': 6.223424e+00, 'ffn1_w_gate': 7.950302e-02, 'ffn1_w_up': 8.427303e-02, 'ffn1_w_down': 1.378826e-01, 'mix_norm': 1.429755e+00, 'w_in': 9.398504e-02, 'q_norm': 5.578112e+00, 'k_norm': 5.574350e+00, 'conv_w': 2.019080e-01, 'conv_b': 6.210370e-01, 'dt_bias': 4.892919e-01, 'a_log': 1.760775e+00, 'd_skip': 1.350314e+00, 'ssd_norm': 6.567465e+00, 'w_attn_branch': 6.170556e-02, 'w_ssd_branch': 4.856481e-01, 'w_out': 4.116852e-01, 'ffn2_norm': 6.222079e+00, 'ffn2_w_gate': 7.130249e-02, 'ffn2_w_up': 7.476225e-02, 'ffn2_w_down': 1.208344e-01}


def _to_microbatches(a, axis):
    t = _jnp.moveaxis(a, axis, 0)
    t = t.reshape((N_MICROBATCH, t.shape[0] // N_MICROBATCH) + t.shape[1:])
    return _jnp.moveaxis(t, 1, axis + 1)


def setup_inputs(seed: int = 0) -> dict:
    inp = _fwd_setup_inputs(seed)
    key = _jax.random.fold_in(_jax.random.key(seed), 7919)
    shape, _ = _output_shape()
    out = dict(inp)
    out["loss_target"] = _jax.random.normal(_jax.random.fold_in(key, 0), shape, _jnp.float32)
    for i, name in enumerate(TWIN_WEIGHTS):
        w = inp[name].astype(_jnp.float32)
        if MOMENT_SCALE is None:
            s = _jnp.sqrt(_jnp.mean(_jnp.square(w)) + 1e-30)
        else:
            s = MOMENT_SCALE[name]
        km, kv = _jax.random.split(_jax.random.fold_in(key, i + 1))
        out[name] = w
        out["m_" + name] = s * _jax.random.normal(km, w.shape, _jnp.float32)
        out["v_" + name] = (s * s) * _jax.random.uniform(kv, w.shape, _jnp.float32, 0.5, 1.5)
    if N_MICROBATCH > 1:
        for name, axis in PER_EXAMPLE_BATCH_AXIS.items():
            out[name] = _to_microbatches(out[name], axis)
    return {'x': out['x'], 'ffn1_norm': out['ffn1_norm'], 'ffn1_w_gate': out['ffn1_w_gate'], 'ffn1_w_up': out['ffn1_w_up'], 'ffn1_w_down': out['ffn1_w_down'], 'mix_norm': out['mix_norm'], 'w_in': out['w_in'], 'q_norm': out['q_norm'], 'k_norm': out['k_norm'], 'conv_w': out['conv_w'], 'conv_b': out['conv_b'], 'dt_bias': out['dt_bias'], 'a_log': out['a_log'], 'd_skip': out['d_skip'], 'ssd_norm': out['ssd_norm'], 'w_attn_branch': out['w_attn_branch'], 'w_ssd_branch': out['w_ssd_branch'], 'w_out': out['w_out'], 'ffn2_norm': out['ffn2_norm'], 'ffn2_w_gate': out['ffn2_w_gate'], 'ffn2_w_up': out['ffn2_w_up'], 'ffn2_w_down': out['ffn2_w_down'], 'loss_target': out['loss_target'], 'm_ffn1_norm': out['m_ffn1_norm'], 'm_ffn1_w_gate': out['m_ffn1_w_gate'], 'm_ffn1_w_up': out['m_ffn1_w_up'], 'm_ffn1_w_down': out['m_ffn1_w_down'], 'm_mix_norm': out['m_mix_norm'], 'm_w_in': out['m_w_in'], 'm_q_norm': out['m_q_norm'], 'm_k_norm': out['m_k_norm'], 'm_conv_w': out['m_conv_w'], 'm_conv_b': out['m_conv_b'], 'm_dt_bias': out['m_dt_bias'], 'm_a_log': out['m_a_log'], 'm_d_skip': out['m_d_skip'], 'm_ssd_norm': out['m_ssd_norm'], 'm_w_attn_branch': out['m_w_attn_branch'], 'm_w_ssd_branch': out['m_w_ssd_branch'], 'm_w_out': out['m_w_out'], 'm_ffn2_norm': out['m_ffn2_norm'], 'm_ffn2_w_gate': out['m_ffn2_w_gate'], 'm_ffn2_w_up': out['m_ffn2_w_up'], 'm_ffn2_w_down': out['m_ffn2_w_down'], 'v_ffn1_norm': out['v_ffn1_norm'], 'v_ffn1_w_gate': out['v_ffn1_w_gate'], 'v_ffn1_w_up': out['v_ffn1_w_up'], 'v_ffn1_w_down': out['v_ffn1_w_down'], 'v_mix_norm': out['v_mix_norm'], 'v_w_in': out['v_w_in'], 'v_q_norm': out['v_q_norm'], 'v_k_norm': out['v_k_norm'], 'v_conv_w': out['v_conv_w'], 'v_conv_b': out['v_conv_b'], 'v_dt_bias': out['v_dt_bias'], 'v_a_log': out['v_a_log'], 'v_d_skip': out['v_d_skip'], 'v_ssd_norm': out['v_ssd_norm'], 'v_w_attn_branch': out['v_w_attn_branch'], 'v_w_ssd_branch': out['v_w_ssd_branch'], 'v_w_out': out['v_w_out'], 'v_ffn2_norm': out['v_ffn2_norm'], 'v_ffn2_w_gate': out['v_ffn2_w_gate'], 'v_ffn2_w_up': out['v_ffn2_w_up'], 'v_ffn2_w_down': out['v_ffn2_w_down']}


def _loss(weights, diff, rest, loss_target):
    with _jax.named_scope("forward"):
        args = {**rest, TWIN_DIFF_INPUT: diff, **{k: w.astype(_WEIGHT_DTYPES[k]) for k, w in weights.items()}}
        y = _forward(args)
    with _jax.named_scope("loss_head"):
        err = _jnp.square(y.astype(_jnp.float32) - loss_target)
        return 0.5 * _jnp.sum(_jnp.mean(err, axis=-1)) if err.ndim else 0.5 * err


def _adamw(w, g, m, v):
    m = ADAM_B1 * m + (1.0 - ADAM_B1) * g
    v = ADAM_B2 * v + (1.0 - ADAM_B2) * _jnp.square(g)
    m_hat = m / (1.0 - ADAM_B1 ** ADAM_STEP)
    v_hat = v / (1.0 - ADAM_B2 ** ADAM_STEP)
    delta = -ADAM_LR * (m_hat / (_jnp.sqrt(v_hat) + ADAM_EPS) + ADAM_WD * w)
    return delta, m, v


def reference(x, ffn1_norm, ffn1_w_gate, ffn1_w_up, ffn1_w_down, mix_norm, w_in, q_norm, k_norm, conv_w, conv_b, dt_bias, a_log, d_skip, ssd_norm, w_attn_branch, w_ssd_branch, w_out, ffn2_norm, ffn2_w_gate, ffn2_w_up, ffn2_w_down, loss_target, m_ffn1_norm, m_ffn1_w_gate, m_ffn1_w_up, m_ffn1_w_down, m_mix_norm, m_w_in, m_q_norm, m_k_norm, m_conv_w, m_conv_b, m_dt_bias, m_a_log, m_d_skip, m_ssd_norm, m_w_attn_branch, m_w_ssd_branch, m_w_out, m_ffn2_norm, m_ffn2_w_gate, m_ffn2_w_up, m_ffn2_w_down, v_ffn1_norm, v_ffn1_w_gate, v_ffn1_w_up, v_ffn1_w_down, v_mix_norm, v_w_in, v_q_norm, v_k_norm, v_conv_w, v_conv_b, v_dt_bias, v_a_log, v_d_skip, v_ssd_norm, v_w_attn_branch, v_w_ssd_branch, v_w_out, v_ffn2_norm, v_ffn2_w_gate, v_ffn2_w_up, v_ffn2_w_down):
    given = dict(x=x, ffn1_norm=ffn1_norm, ffn1_w_gate=ffn1_w_gate, ffn1_w_up=ffn1_w_up, ffn1_w_down=ffn1_w_down, mix_norm=mix_norm, w_in=w_in, q_norm=q_norm, k_norm=k_norm, conv_w=conv_w, conv_b=conv_b, dt_bias=dt_bias, a_log=a_log, d_skip=d_skip, ssd_norm=ssd_norm, w_attn_branch=w_attn_branch, w_ssd_branch=w_ssd_branch, w_out=w_out, ffn2_norm=ffn2_norm, ffn2_w_gate=ffn2_w_gate, ffn2_w_up=ffn2_w_up, ffn2_w_down=ffn2_w_down, loss_target=loss_target, m_ffn1_norm=m_ffn1_norm, m_ffn1_w_gate=m_ffn1_w_gate, m_ffn1_w_up=m_ffn1_w_up, m_ffn1_w_down=m_ffn1_w_down, m_mix_norm=m_mix_norm, m_w_in=m_w_in, m_q_norm=m_q_norm, m_k_norm=m_k_norm, m_conv_w=m_conv_w, m_conv_b=m_conv_b, m_dt_bias=m_dt_bias, m_a_log=m_a_log, m_d_skip=m_d_skip, m_ssd_norm=m_ssd_norm, m_w_attn_branch=m_w_attn_branch, m_w_ssd_branch=m_w_ssd_branch, m_w_out=m_w_out, m_ffn2_norm=m_ffn2_norm, m_ffn2_w_gate=m_ffn2_w_gate, m_ffn2_w_up=m_ffn2_w_up, m_ffn2_w_down=m_ffn2_w_down, v_ffn1_norm=v_ffn1_norm, v_ffn1_w_gate=v_ffn1_w_gate, v_ffn1_w_up=v_ffn1_w_up, v_ffn1_w_down=v_ffn1_w_down, v_mix_norm=v_mix_norm, v_w_in=v_w_in, v_q_norm=v_q_norm, v_k_norm=v_k_norm, v_conv_w=v_conv_w, v_conv_b=v_conv_b, v_dt_bias=v_dt_bias, v_a_log=v_a_log, v_d_skip=v_d_skip, v_ssd_norm=v_ssd_norm, v_w_attn_branch=v_w_attn_branch, v_w_ssd_branch=v_w_ssd_branch, v_w_out=v_w_out, v_ffn2_norm=v_ffn2_norm, v_ffn2_w_gate=v_ffn2_w_gate, v_ffn2_w_up=v_ffn2_w_up, v_ffn2_w_down=v_ffn2_w_down)
    weights = {n: given[n] for n in TWIN_WEIGHTS}
    shared = {n: given[n] for n in SHARED_INPUTS}
    per_example = {n: given[n] for n in ['x']}
    grad_fn = _jax.value_and_grad(_loss, argnums=(0, 1))

    def one_microbatch(ex, loss_target):
        ex = dict(ex)
        diff = ex.pop(TWIN_DIFF_INPUT)
        return grad_fn(weights, diff, {**shared, **ex}, loss_target)

    if N_MICROBATCH == 1:
        loss, (grad_w, grad_x) = one_microbatch(per_example, given["loss_target"])
    else:
        def body(carry, xs):
            loss_sum, grad_sum = carry
            l_k, (gw_k, gx_k) = one_microbatch(xs[0], xs[1])
            with _jax.named_scope("update"):
                return (loss_sum + l_k, _jax.tree.map(_jnp.add, grad_sum, gw_k)), gx_k

        init = (_jnp.zeros((), _jnp.float32), _jax.tree.map(_jnp.zeros_like, weights))
        (loss, grad_w), grad_x = _jax.lax.scan(body, init, (per_example, given["loss_target"]))
    with _jax.named_scope("update"):
        delta_w, new_m, new_v = {}, {}, {}
        for n in TWIN_WEIGHTS:
            delta_w[n], new_m[n], new_v[n] = _adamw(weights[n], grad_w[n], given["m_" + n], given["v_" + n])
    return (loss, grad_x, *[grad_w[n] for n in TWIN_WEIGHTS], *[delta_w[n] for n in TWIN_WEIGHTS],
            *[new_m[n] for n in TWIN_WEIGHTS], *[new_v[n] for n in TWIN_WEIGHTS])
```

```python
import functools
import math

import numpy as np
import jax
import jax.numpy as jnp
from jax import lax
from jax.experimental import pallas as pl
from jax.experimental.pallas import tpu as pltpu

F32 = jnp.float32
BF16 = jnp.bfloat16
MXU_DTYPE = BF16
ACT_DTYPE = BF16

NDEV = 8
EPS = 1e-6
HD = 64
QB = 128
PATTERNS = ((128, 1), (512, 4), (2048, 16))
ALIBI_MAX_EXP = 8.0
SSD_P = 64
SSD_N = 128
SSD_G = 4
SSD_Q = 128
SSD_K = 4
NEG = -1e30
LANE = 128
VMEM_LIMIT = 56 * 1024 * 1024

ADAM_LR, ADAM_B1, ADAM_B2, ADAM_EPS, ADAM_WD, ADAM_STEP = 0.001, 0.9, 0.999, 1e-8, 0.01, 10

NN = (((1,), (0,)), ((), ()))
NT = (((1,), (1,)), ((), ()))
TN = (((0,), (0,)), ((), ()))


def _dot(a, b, dims=NN):
    return lax.dot_general(a.astype(MXU_DTYPE), b.astype(MXU_DTYPE), dims,
                           preferred_element_type=F32)


def _dot_exact(a, b, dims=NN):
    return lax.dot_general(a, b, dims, precision=lax.Precision.HIGHEST,
                           preferred_element_type=F32)


def _tile(n, cap):
    if n <= cap:
        return n
    best = None
    for t in range(LANE, cap + 1, LANE):
        if n % t == 0:
            best = t
    assert best is not None, (n, cap)
    return best


def _params(sem):
    return pltpu.CompilerParams(dimension_semantics=sem, vmem_limit_bytes=VMEM_LIMIT)


def _mm(name, a, b, mode, out_dtype=F32, res=None, scale=1.0,
        cap_m=1024, cap_n=1408, cap_k=1024):
    if mode == "nn":
        (m, k), (k2, n) = a.shape, b.shape
    elif mode == "nt":
        (m, k), (n, k2) = a.shape, b.shape
    else:
        (k, m), (k2, n) = a.shape, b.shape
    assert k == k2, (name, a.shape, b.shape, mode)
    tm, tn, tk = _tile(m, cap_m), _tile(n, cap_n), _tile(k, cap_k)
    nk = k // tk
    dims = {"nn": NN, "nt": NT, "tn": TN}[mode]
    a_spec = (pl.BlockSpec((tk, tm), lambda i, j, kk: (kk, i)) if mode == "tn"
              else pl.BlockSpec((tm, tk), lambda i, j, kk: (i, kk)))
    b_spec = (pl.BlockSpec((tn, tk), lambda i, j, kk: (j, kk)) if mode == "nt"
              else pl.BlockSpec((tk, tn), lambda i, j, kk: (kk, j)))
    o_spec = pl.BlockSpec((tm, tn), lambda i, j, kk: (i, j))
    has_res = res is not None

    def body(*refs):
        if has_res:
            a_ref, b_ref, r_ref, o_ref, *scr = refs
        else:
            a_ref, b_ref, o_ref, *scr = refs
        part = _dot(a_ref[...], b_ref[...], dims)

        def finish(acc):
            if scale != 1.0:
                acc = acc * scale
            if has_res:
                acc = r_ref[...].astype(F32) + acc
            o_ref[...] = acc.astype(o_ref.dtype)

        if nk == 1:
            finish(part)
        else:
            acc_ref = scr[0]
            kk = pl.program_id(2)

            @pl.when(kk == 0)
            def _():
                acc_ref[...] = part

            @pl.when(kk > 0)
            def _():
                acc_ref[...] += part

            @pl.when(kk == nk - 1)
            def _():
                finish(acc_ref[...])

    in_specs = [a_spec, b_spec] + ([o_spec] if has_res else [])
    args = (a, b) + ((res,) if has_res else ())
    return pl.pallas_call(
        body, name=name,
        out_shape=jax.ShapeDtypeStruct((m, n), out_dtype),
        grid=(m // tm, n // tn, nk),
        in_specs=in_specs, out_specs=o_spec,
        scratch_shapes=[pltpu.VMEM((tm, tn), F32)] if nk > 1 else [],
        compiler_params=_params(("parallel", "parallel", "arbitrary")),
    )(*args)


def _rw(name, fn, ins, outs, accs=(), tr=256, ncb=1):
    t = next(a.shape[0] for kind, a, _, _ in ins if kind == "row")
    assert t % tr == 0
    n_in = len(ins)
    n_pieces = sum(len(w) for w, _ in outs)

    def spec(kind, arr, width, base):
        if kind == "row":
            return pl.BlockSpec((tr, width), lambda j, i: (i, base + j))
        return pl.BlockSpec((arr.shape[0], width), lambda j, i: (0, base + j))

    in_specs = [spec(*s) for s in ins]
    out_shapes, out_specs = [], []
    for widths, dt in outs:
        w = sum(widths)
        out_shapes.append(jax.ShapeDtypeStruct((t, w * ncb), dt))
        out_specs.append(pl.BlockSpec((tr, w), lambda j, i: (i, j)))
    for rows, width in accs:
        out_shapes.append(jax.ShapeDtypeStruct((rows, width * ncb), F32))
        out_specs.append(pl.BlockSpec((rows, width), lambda j, i: (0, j)))

    def body(*refs):
        vals = [r[...] for r in refs[:n_in]]
        res = fn(*vals)
        o_refs = refs[n_in:n_in + len(outs)]
        a_refs = refs[n_in + len(outs):]
        p = 0
        for (widths, _), o_ref in zip(outs, o_refs):
            off = 0
            for w in widths:
                if len(widths) == 1:
                    o_ref[...] = res[p].astype(o_ref.dtype)
                else:
                    o_ref[:, off:off + w] = res[p].astype(o_ref.dtype)
                off += w
                p += 1
        i = pl.program_id(1)
        for a_ref, v in zip(a_refs, res[n_pieces:]):
            @pl.when(i == 0)
            def _(a_ref=a_ref, v=v):
                a_ref[...] = v

            @pl.when(i > 0)
            def _(a_ref=a_ref, v=v):
                a_ref[...] += v

    result = pl.pallas_call(
        body, name=name, out_shape=out_shapes,
        grid=(ncb, t // tr), in_specs=in_specs, out_specs=out_specs,
        compiler_params=_params(("parallel", "arbitrary")),
    )(*[a for _, a, _, _ in ins])
    return result


def _rms(x, g):
    x = x.astype(F32)
    return x * lax.rsqrt(jnp.mean(x * x, axis=-1, keepdims=True) + EPS) * g


def _silu(x):
    return x * jax.nn.sigmoid(x)


def _colsum(v):
    return jnp.sum(v, axis=0, keepdims=True)


def _attn_unit(q, kc, kp, vc, vp, qg, kg, coef, first):
    qn = _rms(q, qg)
    kcn = _rms(kc, kg)
    kpn = _rms(kp, kg)
    scale = 1.0 / math.sqrt(HD)
    sc = _dot(qn, kcn, NT) * scale
    sp = _dot(qn, kpn, NT) * scale
    a_idx = lax.broadcasted_iota(jnp.int32, (QB, QB), 0)
    c_idx = lax.broadcasted_iota(jnp.int32, (QB, QB), 1)
    rel_c = a_idx - c_idx
    rel_p = rel_c + QB
    lc = jnp.where(rel_c >= 0, sc - coef * rel_c.astype(F32), NEG)
    lp = jnp.where(jnp.logical_and(rel_c <= 0, jnp.logical_not(first)),
                   sp - coef * rel_p.astype(F32), NEG)
    m = lax.stop_gradient(jnp.maximum(jnp.max(lc, axis=-1, keepdims=True),
                                      jnp.max(lp, axis=-1, keepdims=True)))
    pc = jnp.exp(lc - m)
    pp = jnp.exp(lp - m)
    l = jnp.sum(pc, axis=-1, keepdims=True) + jnp.sum(pp, axis=-1, keepdims=True)
    inv = 1.0 / l
    o = _dot(pc * inv, vc) + _dot(pp * inv, vp)
    lse = m + jnp.log(l)
    return o, jnp.broadcast_to(lse, (QB, HD))


def _unit_slices(j):
    jp = jnp.maximum(j - 1, 0)
    cur = pl.ds(pl.multiple_of(j * QB, QB), QB)
    prv = pl.ds(pl.multiple_of(jp * QB, QB), QB)
    return cur, prv


def _attn_fwd(name, q, k, v, qg, kg, coefs, nb):
    h, t, _ = q.shape
    nu = t // QB
    blk = pl.BlockSpec((1, t, HD), lambda i, c: (i, 0, 0))
    gain = pl.BlockSpec((1, HD), lambda i, c: (0, 0))

    def body(coef_ref, q_ref, k_ref, v_ref, qg_ref, kg_ref, o_ref, l_ref):
        coef = coef_ref[pl.program_id(0)]

        def step(j, carry):
            cur, prv = _unit_slices(j)
            o, lb = _attn_unit(
                q_ref[0, cur, :].astype(F32), k_ref[0, cur, :].astype(F32),
                k_ref[0, prv, :].astype(F32), v_ref[0, cur, :].astype(F32),
                v_ref[0, prv, :].astype(F32), qg_ref[...], kg_ref[...], coef, (j % nb) == 0)
            o_ref[0, cur, :] = o
            l_ref[0, cur, :] = lb
            return carry

        lax.fori_loop(0, nu, step, 0)

    return pl.pallas_call(
        body, name=name,
        out_shape=[jax.ShapeDtypeStruct((h, t, HD), F32)] * 2,
        grid_spec=pltpu.PrefetchScalarGridSpec(
            num_scalar_prefetch=1, grid=(h,),
            in_specs=[blk, blk, blk, gain, gain], out_specs=[blk, blk]),
        compiler_params=_params(("arbitrary",)),
    )(coefs, q, k, v, qg, kg)


def _attn_bwd(name, q, k, v, qg, kg, coefs, nb, do, dl):
    h, t, _ = q.shape
    nu = t // QB
    blk = pl.BlockSpec((1, t, HD), lambda i, c: (i, 0, 0))
    gain = pl.BlockSpec((1, HD), lambda i, c: (0, 0))

    def body(coef_ref, q_ref, k_ref, v_ref, qg_ref, kg_ref, do_ref, dl_ref,
             dq_ref, dk_ref, dv_ref, dqg_ref, dkg_ref):
        hh = pl.program_id(0)
        coef = coef_ref[hh]
        dk_ref[...] = jnp.zeros_like(dk_ref)
        dv_ref[...] = jnp.zeros_like(dv_ref)

        def step(j, carry):
            dqg_acc, dkg_acc = carry
            cur, prv = _unit_slices(j)
            first = (j % nb) == 0
            f = lambda a, b, c, d, e, g1, g2: _attn_unit(a, b, c, d, e, g1, g2, coef, first)
            _, vjp = jax.vjp(
                f, q_ref[0, cur, :].astype(F32), k_ref[0, cur, :].astype(F32),
                k_ref[0, prv, :].astype(F32), v_ref[0, cur, :].astype(F32),
                v_ref[0, prv, :].astype(F32), qg_ref[...], kg_ref[...])
            dq, dkc, dkp, dvc, dvp, dqg, dkg = vjp((do_ref[0, cur, :], dl_ref[0, cur, :]))
            dq_ref[0, cur, :] = dq
            dk_ref[0, cur, :] += dkc
            dk_ref[0, prv, :] += dkp
            dv_ref[0, cur, :] += dvc
            dv_ref[0, prv, :] += dvp
            return dqg_acc + dqg, dkg_acc + dkg

        zero = jnp.zeros((1, HD), F32)
        dqg, dkg = lax.fori_loop(0, nu, step, (zero, zero))

        @pl.when(hh == 0)
        def _():
            dqg_ref[...] = dqg
            dkg_ref[...] = dkg

        @pl.when(hh > 0)
        def _():
            dqg_ref[...] += dqg
            dkg_ref[...] += dkg

    big = jax.ShapeDtypeStruct((h, t, HD), F32)
    small = jax.ShapeDtypeStruct((1, HD), F32)
    return pl.pallas_call(
        body, name=name,
        out_shape=[big, big, big, small, small],
        grid_spec=pltpu.PrefetchScalarGridSpec(
            num_scalar_prefetch=1, grid=(h,),
            in_specs=[blk, blk, blk, gain, gain, blk, blk],
            out_specs=[blk, blk, blk, gain, gain]),
        compiler_params=_params(("arbitrary",)),
    )(coefs, q, k, v, qg, kg, do, dl)


def _shift_down(u, s):
    if s == 0:
        return u
    rows = lax.broadcasted_iota(jnp.int32, u.shape, 0)
    return jnp.where(rows >= s, pltpu.roll(u, s, 0), 0.0)


def _shift_up(u, s):
    if s == 0:
        return u
    t = u.shape[0]
    rows = lax.broadcasted_iota(jnp.int32, u.shape, 0)
    return jnp.where(rows < t - s, pltpu.roll(u, t - s, 0), 0.0)


def _conv_pre(u, w, b):
    y = b
    for kk in range(SSD_K):
        y = y + w[kk:kk + 1, :] * _shift_down(u, SSD_K - 1 - kk)
    return y


def _conv_fwd(name, src, base, w, b, cw=128):
    t = src.shape[0]
    c = w.shape[1]

    def body(u_ref, w_ref, b_ref, o_ref):
        y = _conv_pre(u_ref[...].astype(F32), w_ref[...], b_ref[...])
        o_ref[...] = _silu(y).astype(o_ref.dtype)

    return pl.pallas_call(
        body, name=name, out_shape=jax.ShapeDtypeStruct((t, c), ACT_DTYPE),
        grid=(c // cw,),
        in_specs=[pl.BlockSpec((t, cw), lambda j: (0, base + j)),
                  pl.BlockSpec((SSD_K, cw), lambda j: (0, j)),
                  pl.BlockSpec((1, cw), lambda j: (0, j))],
        out_specs=pl.BlockSpec((t, cw), lambda j: (0, j)),
        compiler_params=_params(("parallel",)),
    )(src, w, b)


def _conv_bwd(name, src, base, w, b, dout, cw=128):
    t = src.shape[0]
    c = w.shape[1]

    def body(u_ref, w_ref, b_ref, d_ref, du_ref, dw_ref, db_ref):
        u = u_ref[...].astype(F32)
        wv = w_ref[...]
        y = _conv_pre(u, wv, b_ref[...])
        sg = jax.nn.sigmoid(y)
        dy = d_ref[...].astype(F32) * (sg * (1.0 + y * (1.0 - sg)))
        du = jnp.zeros_like(u)
        for kk in range(SSD_K):
            s = SSD_K - 1 - kk
            du = du + wv[kk:kk + 1, :] * _shift_up(dy, s)
            dw_ref[kk:kk + 1, :] = _colsum(dy * _shift_down(u, s))
        du_ref[...] = du.astype(du_ref.dtype)
        db_ref[...] = _colsum(dy)

    return pl.pallas_call(
        body, name=name,
        out_shape=[jax.ShapeDtypeStruct((t, c), ACT_DTYPE),
                   jax.ShapeDtypeStruct((SSD_K, c), F32),
                   jax.ShapeDtypeStruct((1, c), F32)],
        grid=(c // cw,),
        in_specs=[pl.BlockSpec((t, cw), lambda j: (0, base + j)),
                  pl.BlockSpec((SSD_K, cw), lambda j: (0, j)),
                  pl.BlockSpec((1, cw), lambda j: (0, j)),
                  pl.BlockSpec((t, cw), lambda j: (0, j))],
        out_specs=[pl.BlockSpec((t, cw), lambda j: (0, j)),
                   pl.BlockSpec((SSD_K, cw), lambda j: (0, j)),
                   pl.BlockSpec((1, cw), lambda j: (0, j))],
        compiler_params=_params(("parallel",)),
    )(src, w, b, dout)


def _ssd_group(xs, dtc, acol, arow, bm, cm, sprev):
    nj = len(sprev)
    li = lax.broadcasted_iota(jnp.int32, (SSD_Q, SSD_Q), 0)
    si = lax.broadcasted_iota(jnp.int32, (SSD_Q, SSD_Q), 1)
    causal = li >= si
    lower = causal.astype(F32)
    upper = (li <= si).astype(F32)
    acs_c = _dot_exact(lower, acol)
    acs_r = _dot_exact(arow, upper)
    cb = _dot(cm, bm, NT)
    ys, snext = [], []
    for j in range(nj):
        ac = acs_c[:, j:j + 1]
        ar = acs_r[j:j + 1, :]
        decay = jnp.exp(jnp.where(causal, ac - ar, NEG))
        xj = xs[:, j * SSD_P:(j + 1) * SSD_P] * dtc[:, j:j + 1]
        y_diag = _dot(cb * decay, xj)
        a_last = ac[SSD_Q - 1:SSD_Q, :]
        bd = bm * jnp.exp(a_last - ac)
        y_off = _dot(cm, sprev[j]) * jnp.exp(ac)
        ys.append(y_diag + y_off)
        snext.append(sprev[j] * jnp.exp(a_last) + _dot(bd, xj, TN))
    return ys, snext


def _ssd_specs(nj, rev, nc):
    ch = (lambda c: nc - 1 - c) if rev else (lambda c: c)
    xs = pl.BlockSpec((SSD_Q, nj * SSD_P), lambda g, c: (ch(c), g))
    col = pl.BlockSpec((1, SSD_Q, nj), lambda g, c: (g, ch(c), 0))
    row = pl.BlockSpec((1, nj, SSD_Q), lambda g, c: (g, 0, ch(c)))
    nxs = (SSD_G * nj * SSD_P) // SSD_N
    bsp = pl.BlockSpec((SSD_Q, SSD_N), lambda g, c: (ch(c), nxs + g))
    csp = pl.BlockSpec((SSD_Q, SSD_N), lambda g, c: (ch(c), nxs + SSD_G + g))
    st = pl.BlockSpec((1, 1, nj, SSD_N, SSD_P), lambda g, c: (g, ch(c), 0, 0, 0))
    bc = pl.BlockSpec((SSD_Q, SSD_N), lambda g, c: (ch(c), g))
    return xs, col, row, bsp, csp, st, bc


def _ssd_fwd(name, xbc, dtc, acol, arow):
    t = xbc.shape[0]
    nj = dtc.shape[2]
    nc = t // SSD_Q
    xs_s, col_s, row_s, b_s, c_s, st_s, _ = _ssd_specs(nj, False, nc)

    def body(x_ref, dt_ref, ac_ref, ar_ref, b_ref, c_ref, y_ref, st_ref, s_scr):
        @pl.when(pl.program_id(1) == 0)
        def _():
            s_scr[...] = jnp.zeros_like(s_scr)

        sprev = [s_scr[j] for j in range(nj)]
        ys, snext = _ssd_group(x_ref[...].astype(F32), dt_ref[0], ac_ref[0], ar_ref[0],
                               b_ref[...].astype(F32), c_ref[...].astype(F32), sprev)
        for j in range(nj):
            st_ref[0, 0, j] = sprev[j]
            y_ref[:, j * SSD_P:(j + 1) * SSD_P] = ys[j]
            s_scr[j] = snext[j]

    return pl.pallas_call(
        body, name=name,
        out_shape=[jax.ShapeDtypeStruct((t, SSD_G * nj * SSD_P), F32),
                   jax.ShapeDtypeStruct((SSD_G, nc, nj, SSD_N, SSD_P), F32)],
        grid=(SSD_G, nc),
        in_specs=[xs_s, col_s, col_s, row_s, b_s, c_s],
        out_specs=[xs_s, st_s],
        scratch_shapes=[pltpu.VMEM((nj, SSD_N, SSD_P), F32)],
        compiler_params=_params(("parallel", "arbitrary")),
    )(xbc, dtc, acol, arow, xbc, xbc)


def _ssd_bwd(name, xbc, dtc, acol, arow, states, dy, dxs_extra):
    t = xbc.shape[0]
    nj = dtc.shape[2]
    nc = t // SSD_Q
    xs_s, col_s, row_s, b_s, c_s, st_s, bc_s = _ssd_specs(nj, True, nc)

    def body(x_ref, dt_ref, ac_ref, ar_ref, b_ref, c_ref, st_ref, dy_ref, dx0_ref,
             dx_ref, db_ref, dc_ref, ddt_ref, dac_ref, dar_ref, ds_scr):
        @pl.when(pl.program_id(1) == 0)
        def _():
            ds_scr[...] = jnp.zeros_like(ds_scr)

        sprev = [st_ref[0, 0, j] for j in range(nj)]
        _, vjp = jax.vjp(_ssd_group, x_ref[...].astype(F32), dt_ref[0], ac_ref[0], ar_ref[0],
                         b_ref[...].astype(F32), c_ref[...].astype(F32), sprev)
        dyv = dy_ref[...]
        dys = [dyv[:, j * SSD_P:(j + 1) * SSD_P] for j in range(nj)]
        dsn = [ds_scr[j] for j in range(nj)]
        dx, ddt, dac, dar, db, dc, dsp = vjp((dys, dsn))
        dx_ref[...] = dx0_ref[...].astype(F32) + dx
        db_ref[...] = db
        dc_ref[...] = dc
        ddt_ref[0] = ddt
        dac_ref[0] = dac
        dar_ref[0] = dar
        for j in range(nj):
            ds_scr[j] = dsp[j]

    inner = SSD_G * nj * SSD_P
    return pl.pallas_call(
        body, name=name,
        out_shape=[jax.ShapeDtypeStruct((t, inner), F32),
                   jax.ShapeDtypeStruct((t, SSD_G * SSD_N), F32),
                   jax.ShapeDtypeStruct((t, SSD_G * SSD_N), F32),
                   jax.ShapeDtypeStruct(dtc.shape, F32),
                   jax.ShapeDtypeStruct(dtc.shape, F32),
                   jax.ShapeDtypeStruct(arow.shape, F32)],
        grid=(SSD_G, nc),
        in_specs=[xs_s, col_s, col_s, row_s, b_s, c_s, st_s, xs_s, xs_s],
        out_specs=[xs_s, bc_s, bc_s, col_s, col_s, row_s],
        scratch_shapes=[pltpu.VMEM((nj, SSD_N, SSD_P), F32)],
        compiler_params=_params(("parallel", "arbitrary")),
    )(xbc, dtc, acol, arow, xbc, xbc, states, dy, dxs_extra)


def _act(g, u):
    return _silu(g.astype(F32)) * u.astype(F32)


def _softplus(x):
    return jnp.maximum(x, 0.0) + jnp.log(1.0 + jnp.exp(-jnp.abs(x)))


def _dt_pair(raw, bias, alog):
    dt = _softplus(raw + bias)
    return dt, dt * (-jnp.exp(alog))


def _mix(o0, o1, o2, l0, l1, l2):
    m = lax.stop_gradient(jnp.maximum(jnp.maximum(l0, l1), l2))
    e0, e1, e2 = jnp.exp(l0 - m), jnp.exp(l1 - m), jnp.exp(l2 - m)
    return (e0 * o0 + e1 * o1 + e2 * o2) / (e0 + e1 + e2)


def _gate(y, xs, z, dexp, gain):
    v = (y + xs.astype(F32) * dexp) * _silu(z.astype(F32))
    return _rms(v, gain)


def _merge(ga, gs, ap, sp):
    return jax.nn.sigmoid(ga.astype(F32)) * ap + jax.nn.sigmoid(gs.astype(F32)) * sp


def _dedilate(seg, d):
    t, w = seg.shape
    return seg.reshape(t // d, d, w // HD, HD).transpose(2, 1, 0, 3).reshape(w // HD, t, HD)


def _redilate(a, d):
    hp, t, _ = a.shape
    return a.reshape(hp, d, t // d, HD).transpose(2, 1, 0, 3).reshape(t, hp * HD)


def _alibi_coefs(hp):
    n = hp * len(PATTERNS)
    slopes = np.exp2(-ALIBI_MAX_EXP * np.arange(1, n + 1, dtype=np.float32) / n).astype(np.float32)
    return [jnp.asarray(slopes[g * hp:(g + 1) * hp] * np.float32(d))
            for g, (_, d) in enumerate(PATTERNS)]


def _local_step(x, tgt, w, p):
    t, d = x.shape
    dff = w["d1"].shape[0]
    hp = w["ab"].shape[0] // HD
    qkv = len(PATTERNS) * hp * HD
    inner = w["sb"].shape[0]
    nh = p["dt_bias"].shape[1]
    nj = nh // SSD_G
    gw = inner // SSD_G
    cdim = inner + 2 * SSD_G * SSD_N
    z_off, xbc_off = 3 * qkv, 3 * qkv + inner
    ga_off = xbc_off + cdim
    gs_off = ga_off + d
    hw = d // 2
    assert z_off % gw == 0 and xbc_off % LANE == 0 and ga_off % hw == 0 and gs_off % hw == 0
    assert nj * SSD_P == gw

    row = lambda a, width, base=0: ("row", a, width, base)
    const = lambda a, width, base=0: ("const", a, width, base)

    def rms_fwd(name, xin, g):
        return _rw(name, lambda xv, gv: (_rms(xv, gv),), [row(xin, d), const(g, d)],
                   [((d,), ACT_DTYPE)])[0]

    def rms_bwd(name, xin, g, dh, dres):
        def fn(xv, gv, dhv, drv):
            _, vjp = jax.vjp(_rms, xv, gv)
            dx, dg = vjp(dhv.astype(F32))
            return drv + dx, dg
        return _rw(name, fn, [row(xin, d), const(g, d), row(dh, d), row(dres, d)],
                   [((d,), F32)], accs=[(1, d)])

    def ffn_fwd(tag, xin, g, wgu, wd):
        h = rms_fwd(tag + "_norm", xin, g)
        gu = _mm(tag + "_up", h, wgu, "nn", ACT_DTYPE)
        a = _rw(tag + "_act", lambda gv, uv: (_act(gv, uv),),
                [row(gu, dff, 0), row(gu, dff, 1)], [((dff,), ACT_DTYPE)])[0]
        xo = _mm(tag + "_down", a, wd, "nn", F32, res=xin, scale=0.5)
        return xo, (h, gu, a)

    def ffn_bwd(tag, xin, g, wgu, wd, saved, dxo):
        h, gu, a = saved
        da = _mm(tag + "_da", dxo, wd, "nt", ACT_DTYPE, scale=0.5)
        dwd = _mm(tag + "_dwd", a, dxo, "tn", F32, scale=0.5)

        def fn(gv, uv, dav):
            _, vjp = jax.vjp(_act, gv, uv)
            return vjp(dav.astype(F32))
        dgu = _rw(tag + "_dact", fn, [row(gu, dff, 0), row(gu, dff, 1), row(da, dff)],
                  [((dff, dff), ACT_DTYPE)])[0]
        dwgu = _mm(tag + "_dwgu", h, dgu, "tn", F32)
        dh = _mm(tag + "_dh", dgu, wgu, "nt", F32)
        dx, dg = rms_bwd(tag + "_dnorm", xin, g, dh, dxo)
        return dx, dg, dwgu, dwd

    x1, ffn1_saved = ffn_fwd("ffn1", x, p["ffn1_norm"], w["gu1"], w["d1"])
    h2 = rms_fwd("mix_norm", x1, p["mix_norm"])
    proj = _mm("in_proj", h2, w["main"], "nn", ACT_DTYPE)
    dtraw = _mm("dt_proj", h2, w["dt"], "nn", F32)

    coefs = _alibi_coefs(hp)
    heads = []
    for gi, (_, dil) in enumerate(PATTERNS):
        sl = lambda off: _dedilate(proj[:, off + gi * hp * HD: off + (gi + 1) * hp * HD], dil)
        heads.append((sl(0), sl(qkv), sl(2 * qkv)))
    attn_o, attn_l = [], []
    for gi, (_, dil) in enumerate(PATTERNS):
        qh, kh, vh = heads[gi]
        o, l = _attn_fwd(f"attn_fwd{gi}", qh, kh, vh, p["q_norm"], p["k_norm"], coefs[gi],
                         t // (dil * QB))
        attn_o.append(_redilate(o, dil))
        attn_l.append(_redilate(l, dil))
    aw = hp * HD
    ao = _rw("attn_mix", lambda *v: (_mix(*v),), [row(a, aw) for a in attn_o + attn_l],
             [((aw,), ACT_DTYPE)])[0]

    xbc = _conv_fwd("conv_fwd", proj, xbc_off // LANE, p["conv_w"], p["conv_b"])
    pad = lambda v: jnp.pad(v, ((0, 0), (0, LANE - nh)))
    bias_p, alog_p = pad(p["dt_bias"]), pad(p["a_log"])
    dt, av = _rw("dt_fwd", _dt_pair, [row(dtraw, LANE), const(bias_p, LANE), const(alog_p, LANE)],
                 [((LANE,), F32), ((LANE,), F32)])
    to_col = lambda v: v[:, :nh].reshape(t, SSD_G, nj).transpose(1, 0, 2)
    dtc, acol = to_col(dt), to_col(av)
    arow = acol.transpose(0, 2, 1)
    yssd, states = _ssd_fwd("ssd_fwd", xbc, dtc, acol, arow)
    dexp = jnp.repeat(p["d_skip"], SSD_P, axis=1)
    gate_ins = [row(yssd, gw), row(xbc, gw), row(proj, gw, z_off // gw),
                const(dexp, gw), const(p["ssd_norm"], gw)]
    yn = _rw("ssd_gate", lambda *v: (_gate(*v),), gate_ins, [((gw,), ACT_DTYPE)], ncb=SSD_G)[0]

    ap = _mm("attn_out", ao, w["ab"], "nn", F32)
    sp = _mm("ssd_out", yn, w["sb"], "nn", F32)
    merge_ins = [row(proj, hw, ga_off // hw), row(proj, hw, gs_off // hw), row(ap, hw), row(sp, hw)]
    mg = _rw("merge", lambda *v: (_merge(*v),), merge_ins, [((hw,), ACT_DTYPE)], ncb=2)[0]
    x2 = _mm("mix_out", mg, w["out"], "nn", F32, res=x1)
    x3, ffn2_saved = ffn_fwd("ffn2", x2, p["ffn2_norm"], w["gu2"], w["d2"])

    def loss_fn(yv, tv):
        e = yv - tv
        return e * (1.0 / d), _colsum(e * e)
    dy, loss_vec = _rw("loss", loss_fn, [row(x3, d), row(tgt, d)], [((d,), F32)], accs=[(1, d)])

    gw_, gp = {}, {}
    dx2, gp["ffn2_norm"], gw_["gu2"], gw_["d2"] = ffn_bwd(
        "ffn2", x2, p["ffn2_norm"], w["gu2"], w["d2"], ffn2_saved, dy)
    dmg = _mm("d_merge", dx2, w["out"], "nt", ACT_DTYPE)
    gw_["out"] = _mm("dw_out", mg, dx2, "tn", F32)

    def merge_bwd(gav, gsv, apv, spv, dv):
        _, vjp = jax.vjp(_merge, gav, gsv, apv, spv)
        return vjp(dv.astype(F32))
    dga, dgs, dap, dsp = _rw("d_merge_gate", merge_bwd, merge_ins + [row(dmg, hw)],
                             [((hw,), ACT_DTYPE)] * 4, ncb=2)
    gw_["ab"] = _mm("dw_ab", ao, dap, "tn", F32)
    dao = _mm("d_attn_o", dap, w["ab"], "nt", F32)
    gw_["sb"] = _mm("dw_sb", yn, dsp, "tn", F32)
    dyn = _mm("d_ssd_y", dsp, w["sb"], "nt", F32)

    def gate_bwd(yv, xv, zv, dev, gv, dv):
        _, vjp = jax.vjp(_gate, yv, xv, zv, dev, gv)
        dyv, dxv, dzv, ddv, dgv = vjp(dv)
        return dyv, dxv, dzv, ddv, dgv
    dyssd, dxs_gate, dz, ddexp, gp["ssd_norm"] = _rw(
        "d_ssd_gate", gate_bwd, gate_ins + [row(dyn, gw)],
        [((gw,), F32), ((gw,), F32), ((gw,), ACT_DTYPE)], accs=[(1, gw), (1, gw)], ncb=SSD_G)
    gp["d_skip"] = ddexp.reshape(nh, SSD_P).sum(axis=1).reshape(1, nh)

    dxs, dbm, dcm, ddtc, dacol, darow = _ssd_bwd("ssd_bwd", xbc, dtc, acol, arow, states,
                                                  dyssd, dxs_gate)
    from_col = lambda v: jnp.pad(v.transpose(1, 0, 2).reshape(t, nh), ((0, 0), (0, LANE - nh)))
    ddt = from_col(ddtc)
    dav = from_col(dacol + darow.transpose(0, 2, 1))

    def dt_bwd(rv, bv, av_, d1, d2):
        _, vjp = jax.vjp(_dt_pair, rv, bv, av_)
        dr, db, da = vjp((d1, d2))
        return dr, _colsum(db), _colsum(da)
    ddtraw, dbias, dalog = _rw(
        "dt_bwd", dt_bwd, [row(dtraw, LANE), const(bias_p, LANE), const(alog_p, LANE),
                           row(ddt, LANE), row(dav, LANE)],
        [((LANE,), F32)], accs=[(1, LANE), (1, LANE)])
    gp["dt_bias"], gp["a_log"] = dbias[:, :nh], dalog[:, :nh]

    dxbc = jnp.concatenate([dxs, dbm, dcm], axis=1)
    du, gp["conv_w"], gp["conv_b"] = _conv_bwd("conv_bwd", proj, xbc_off // LANE,
                                               p["conv_w"], p["conv_b"], dxbc)

    def mix_bwd(*v):
        _, vjp = jax.vjp(_mix, *v[:6])
        return vjp(v[6])
    dmix = _rw("d_attn_mix", mix_bwd, [row(a, aw) for a in attn_o + attn_l] + [row(dao, aw)],
               [((aw,), F32)] * 6)
    dq, dk, dv = [], [], []
    gp["q_norm"] = gp["k_norm"] = None
    for gi, (_, dil) in enumerate(PATTERNS):
        qh, kh, vh = heads[gi]
        r = _attn_bwd(f"attn_bwd{gi}", qh, kh, vh, p["q_norm"], p["k_norm"], coefs[gi],
                      t // (dil * QB), _dedilate(dmix[gi], dil), _dedilate(dmix[3 + gi], dil))
        dq.append(_redilate(r[0], dil))
        dk.append(_redilate(r[1], dil))
        dv.append(_redilate(r[2], dil))
        gp["q_norm"] = r[3] if gp["q_norm"] is None else gp["q_norm"] + r[3]
        gp["k_norm"] = r[4] if gp["k_norm"] is None else gp["k_norm"] + r[4]

    dproj = jnp.concatenate([a.astype(ACT_DTYPE) for a in dq + dk + dv + [dz, du, dga, dgs]],
                            axis=1)
    gw_["main"] = _mm("dw_in", h2, dproj, "tn", F32)
    gw_["dt"] = _mm("dw_dt", h2, ddtraw, "tn", F32)
    dh2 = _mm("d_h2_main", dproj, w["main"], "nt", F32)
    dh2 = _mm("d_h2_dt", ddtraw, w["dt"], "nt", F32, res=dh2)
    dx1, gp["mix_norm"] = rms_bwd("d_mix_norm", x1, p["mix_norm"], dh2, dx2)
    dx0, gp["ffn1_norm"], gw_["gu1"], gw_["d1"] = ffn_bwd(
        "ffn1", x, p["ffn1_norm"], w["gu1"], w["d1"], ffn1_saved, dx1)
    return loss_vec, dx0, gw_, gp


MESH = pl.DeviceIdType.MESH
HBM_SPEC = pl.BlockSpec(memory_space=pltpu.HBM)


def _mesh_pos():
    return lax.axis_index("x"), lax.axis_index("y"), lax.axis_index("c")


def _flip(pos, k):
    x, y, c = pos
    return (1 - x if k & 4 else x, 1 - y if k & 2 else y, 1 - c if k & 1 else c)


def _dev_index(pos):
    return 4 * pos[0] + 2 * pos[1] + pos[2]


def _all_gather(name, shard):
    def body(x_ref, out_ref, send_sems, recv_sems, local_sem):
        me = _mesh_pos()
        sibling = _flip(me, 1)
        chips = [_flip(me, 4), _flip(me, 2), _flip(me, 6)]

        def copy(k, block, to, src=None):
            slot = out_ref.at[_dev_index(block)]
            return pltpu.make_async_remote_copy(
                src_ref=slot if src is None else src, dst_ref=slot,
                send_sem=send_sems.at[k], recv_sem=recv_sems.at[k],
                device_id=to, device_id_type=MESH)

        mine = pltpu.make_async_copy(x_ref, out_ref.at[_dev_index(me)], local_sem)
        mine.start()
        first = [copy(0, me, sibling, src=x_ref)]
        first += [copy(1 + j, me, chip, src=x_ref) for j, chip in enumerate(chips)]
        for cp in first:
            cp.start()
        passed = [copy(4 + j, chip, sibling) for j, chip in enumerate(chips)]
        for j, chip in enumerate(chips):
            copy(1 + j, chip, me).wait_recv()
            passed[j].start()
        copy(0, sibling, me).wait_recv()
        for j, chip in enumerate(chips):
            copy(4 + j, _flip(chip, 1), me).wait_recv()
        for cp in first + passed:
            cp.wait_send()
        mine.wait()

    return pl.pallas_call(
        body, name=name,
        out_shape=jax.ShapeDtypeStruct((NDEV,) + shard.shape, shard.dtype),
        in_specs=[HBM_SPEC], out_specs=HBM_SPEC,
        scratch_shapes=[pltpu.SemaphoreType.DMA((7,)), pltpu.SemaphoreType.DMA((7,)),
                        pltpu.SemaphoreType.DMA],
    )(shard)


def _exchange(name, slabs, small):
    def body(s_ref, m_ref, r_ref, g_ref, send_sems, recv_sems, local_sems):
        me = _mesh_pos()
        my = _dev_index(me)
        own = [pltpu.make_async_copy(s_ref.at[my], r_ref.at[my], local_sems.at[0]),
               pltpu.make_async_copy(m_ref, g_ref.at[my], local_sems.at[1])]
        for cp in own:
            cp.start()

        def copies(k):
            peer = _flip(me, k)
            pi = _dev_index(peer)
            big = pltpu.make_async_remote_copy(
                src_ref=s_ref.at[pi], dst_ref=r_ref.at[my],
                send_sem=send_sems.at[k - 1], recv_sem=recv_sems.at[k - 1],
                device_id=peer, device_id_type=MESH)
            sm = pltpu.make_async_remote_copy(
                src_ref=m_ref, dst_ref=g_ref.at[my],
                send_sem=send_sems.at[6 + k], recv_sem=recv_sems.at[6 + k],
                device_id=peer, device_id_type=MESH)
            return big, sm

        def arrivals(k):
            pi = _dev_index(_flip(me, k))
            big = pltpu.make_async_remote_copy(
                src_ref=s_ref.at[pi], dst_ref=r_ref.at[pi],
                send_sem=send_sems.at[k - 1], recv_sem=recv_sems.at[k - 1],
                device_id=_flip(me, k), device_id_type=MESH)
            sm = pltpu.make_async_remote_copy(
                src_ref=m_ref, dst_ref=g_ref.at[pi],
                send_sem=send_sems.at[6 + k], recv_sem=recv_sems.at[6 + k],
                device_id=_flip(me, k), device_id_type=MESH)
            return big, sm

        sent = [cp for k in range(1, NDEV) for cp in copies(k)]
        for cp in sent:
            cp.start()
        for k in range(1, NDEV):
            for cp in arrivals(k):
                cp.wait_recv()
        for cp in sent:
            cp.wait_send()
        for cp in own:
            cp.wait()

    return pl.pallas_call(
        body, name=name,
        out_shape=[jax.ShapeDtypeStruct(slabs.shape, slabs.dtype),
                   jax.ShapeDtypeStruct((NDEV,) + small.shape, small.dtype)],
        in_specs=[HBM_SPEC, HBM_SPEC], out_specs=[HBM_SPEC, HBM_SPEC],
        scratch_shapes=[pltpu.SemaphoreType.DMA((14,)), pltpu.SemaphoreType.DMA((14,)),
                        pltpu.SemaphoreType.DMA((2,))],
    )(slabs, small)


def _sum_slabs(name, a):
    s, r, c = a.shape

    def body(a_ref, o_ref):
        acc = a_ref[0].astype(F32)
        for i in range(1, s):
            acc = acc + a_ref[i].astype(F32)
        o_ref[...] = acc

    return pl.pallas_call(body, name=name, out_shape=jax.ShapeDtypeStruct((r, c), F32))(a)


def _adamw(name, gsrc, w, m, v, tr):
    s, r, c = gsrc.shape
    assert r % tr == 0

    def body(g_ref, w_ref, m_ref, v_ref, go_ref, d_ref, mo_ref, vo_ref):
        g = g_ref[0].astype(F32)
        for i in range(1, s):
            g = g + g_ref[i].astype(F32)
        mn = ADAM_B1 * m_ref[...] + (1.0 - ADAM_B1) * g
        vn = ADAM_B2 * v_ref[...] + (1.0 - ADAM_B2) * (g * g)
        m_hat = mn / (1.0 - ADAM_B1 ** ADAM_STEP)
        v_hat = vn / (1.0 - ADAM_B2 ** ADAM_STEP)
        go_ref[...] = g
        d_ref[...] = -ADAM_LR * (m_hat / (jnp.sqrt(v_hat) + ADAM_EPS) + ADAM_WD * w_ref[...])
        mo_ref[...] = mn
        vo_ref[...] = vn

    blk = pl.BlockSpec((tr, c), lambda i: (i, 0))
    return pl.pallas_call(
        body, name=name, out_shape=[jax.ShapeDtypeStruct((r, c), F32)] * 4,
        grid=(r // tr,),
        in_specs=[pl.BlockSpec((s, tr, c), lambda i: (0, i, 0)), blk, blk, blk],
        out_specs=[blk] * 4,
        compiler_params=_params(("parallel",)),
    )(gsrc, w, m, v)


BIG = ("ffn1_w_gate", "ffn1_w_up", "ffn1_w_down", "w_in", "w_attn_branch", "w_ssd_branch",
       "w_out", "ffn2_w_gate", "ffn2_w_up", "ffn2_w_down")
COL_SHARDED = ("ffn1_w_gate", "ffn1_w_up", "w_in", "w_attn_branch", "ffn2_w_gate", "ffn2_w_up")
REPLICATED = ("ffn1_norm", "mix_norm", "q_norm", "k_norm", "conv_b", "dt_bias", "a_log",
              "d_skip", "ssd_norm", "ffn2_norm")
ALL_WEIGHTS = ("ffn1_norm", "ffn1_w_gate", "ffn1_w_up", "ffn1_w_down", "mix_norm", "w_in",
               "q_norm", "k_norm", "conv_w", "conv_b", "dt_bias", "a_log", "d_skip", "ssd_norm",
               "w_attn_branch", "w_ssd_branch", "w_out", "ffn2_norm", "ffn2_w_gate", "ffn2_w_up",
               "ffn2_w_down")
PACK_COLS = 1024
PACK_ROW_TILE = 256


def _nrows(shape, cols):
    return -(-math.prod(shape) // cols)


def _pack_rows(arrs, cols, row_tile, lead=0):
    parts = []
    for a in arrs:
        head = a.shape[:lead]
        flat = a.reshape(head + (-1,))
        n = flat.shape[-1]
        nr = -(-n // cols)
        flat = jnp.pad(flat, [(0, 0)] * lead + [(0, nr * cols - n)])
        parts.append(flat.reshape(head + (nr, cols)))
    out = jnp.concatenate(parts, axis=lead)
    rows = out.shape[lead]
    pad = -(-rows // row_tile) * row_tile - rows
    return jnp.pad(out, [(0, 0)] * lead + [(0, pad), (0, 0)])


def _unpack_rows(packed, shapes, lead=0):
    cols = packed.shape[-1]
    head = packed.shape[:lead]
    out, r0 = [], 0
    for sh in shapes:
        nr = _nrows(sh, cols)
        part = lax.slice_in_dim(packed, r0, r0 + nr, axis=lead).reshape(head + (-1,))
        out.append(part[..., :math.prod(sh)].reshape(head + tuple(sh)))
        r0 += nr
    return out


def _full_from_shards(g, name):
    if name in COL_SHARDED:
        return g.transpose(1, 0, 2).reshape(g.shape[1], NDEV * g.shape[2])
    return g.reshape(NDEV * g.shape[1], g.shape[2])


def _shards_from_full(full, name):
    if name in COL_SHARDED:
        r, c = full.shape
        return full.reshape(r, NDEV, c // NDEV).transpose(1, 0, 2)
    r, c = full.shape
    return full.reshape(NDEV, r // NDEV, c)


def kernel(x, ffn1_norm, ffn1_w_gate, ffn1_w_up, ffn1_w_down, mix_norm, w_in, q_norm, k_norm, conv_w, conv_b, dt_bias, a_log, d_skip, ssd_norm, w_attn_branch, w_ssd_branch, w_out, ffn2_norm, ffn2_w_gate, ffn2_w_up, ffn2_w_down, loss_target, m_ffn1_norm, m_ffn1_w_gate, m_ffn1_w_up, m_ffn1_w_down, m_mix_norm, m_w_in, m_q_norm, m_k_norm, m_conv_w, m_conv_b, m_dt_bias, m_a_log, m_d_skip, m_ssd_norm, m_w_attn_branch, m_w_ssd_branch, m_w_out, m_ffn2_norm, m_ffn2_w_gate, m_ffn2_w_up, m_ffn2_w_down, v_ffn1_norm, v_ffn1_w_gate, v_ffn1_w_up, v_ffn1_w_down, v_mix_norm, v_w_in, v_q_norm, v_k_norm, v_conv_w, v_conv_b, v_dt_bias, v_a_log, v_d_skip, v_ssd_norm, v_w_attn_branch, v_w_ssd_branch, v_w_out, v_ffn2_norm, v_ffn2_w_gate, v_ffn2_w_up, v_ffn2_w_down):
    given = dict(locals())
    wts = {n: given[n] for n in ALL_WEIGHTS}
    mom = {n: given["m_" + n] for n in ALL_WEIGHTS}
    var = {n: given["v_" + n] for n in ALL_WEIGHTS}
    d = x.shape[-1]
    nh = dt_bias.shape[1]
    my = _dev_index(_mesh_pos())

    shard_shapes = [wts[n].shape[1:] for n in BIG]
    packed_w = _pack_rows([wts[n][0] for n in BIG], PACK_COLS, PACK_ROW_TILE)
    gathered = _all_gather("gather_weights", packed_w.astype(MXU_DTYPE))
    full = {n: _full_from_shards(g, n)
            for n, g in zip(BIG, _unpack_rows(gathered, shard_shapes, lead=1))}
    win = full["w_in"]
    dt_off = win.shape[1] - 2 * d - nh
    w = dict(
        gu1=jnp.concatenate([full["ffn1_w_gate"], full["ffn1_w_up"]], axis=1),
        d1=full["ffn1_w_down"],
        main=jnp.concatenate([win[:, :dt_off], win[:, dt_off + nh:]], axis=1),
        dt=jnp.pad(win[:, dt_off:dt_off + nh], ((0, 0), (0, LANE - nh))),
        ab=full["w_attn_branch"], sb=full["w_ssd_branch"], out=full["w_out"],
        gu2=jnp.concatenate([full["ffn2_w_gate"], full["ffn2_w_up"]], axis=1),
        d2=full["ffn2_w_down"])
    p = {n: wts[n] for n in REPLICATED}
    conv_all = _all_gather("gather_conv_w", _pack_rows([conv_w[0]], LANE, 8))
    conv_all = _unpack_rows(conv_all, [conv_w.shape[1:]], lead=1)[0]
    p["conv_w"] = conv_all.transpose(1, 0, 2).reshape(conv_w.shape[1], NDEV * conv_w.shape[2])

    loss_vec, dx, gw, gp = _local_step(x[0], loss_target[0], w, p)

    dff = full["ffn1_w_down"].shape[0]
    gfull = {
        "ffn1_w_gate": gw["gu1"][:, :dff], "ffn1_w_up": gw["gu1"][:, dff:], "ffn1_w_down": gw["d1"],
        "w_in": jnp.concatenate([gw["main"][:, :dt_off], gw["dt"][:, :nh], gw["main"][:, dt_off:]],
                                axis=1),
        "w_attn_branch": gw["ab"], "w_ssd_branch": gw["sb"], "w_out": gw["out"],
        "ffn2_w_gate": gw["gu2"][:, :dff], "ffn2_w_up": gw["gu2"][:, dff:], "ffn2_w_down": gw["d2"]}
    slabs = _pack_rows([_shards_from_full(gfull[n], n).astype(MXU_DTYPE) for n in BIG],
                       PACK_COLS, PACK_ROW_TILE, lead=1)
    small_names = REPLICATED + ("conv_w",)
    small_shapes = [gp[n].shape for n in small_names]
    small = _pack_rows([gp[n] for n in small_names], LANE, 8)
    recv, small_all = _exchange("exchange_grads", slabs, small)

    packed_m = _pack_rows([mom[n][0] for n in BIG], PACK_COLS, PACK_ROW_TILE)
    packed_v = _pack_rows([var[n][0] for n in BIG], PACK_COLS, PACK_ROW_TILE)
    res_big = _adamw("adamw_big", recv, packed_w, packed_m, packed_v, PACK_ROW_TILE)
    res_big = [_unpack_rows(r, shard_shapes) for r in res_big]

    small_g = _unpack_rows(_sum_slabs("sum_small_grads", small_all), small_shapes)
    small_g = dict(zip(small_names, small_g))
    cs = conv_w.shape[2]
    small_g["conv_w"] = lax.dynamic_slice_in_dim(small_g["conv_w"], my * cs, cs, axis=1)
    small_shard_shapes = [wts[n].shape[-2:] for n in small_names]
    sg = _pack_rows([small_g[n] for n in small_names], LANE, 8)
    sw = _pack_rows([wts[n].reshape(sh) for n, sh in zip(small_names, small_shard_shapes)], LANE, 8)
    sm = _pack_rows([mom[n].reshape(sh) for n, sh in zip(small_names, small_shard_shapes)], LANE, 8)
    sv = _pack_rows([var[n].reshape(sh) for n, sh in zip(small_names, small_shard_shapes)], LANE, 8)
    res_small = _adamw("adamw_small", sg[None], sw, sm, sv, sg.shape[0])
    res_small = [_unpack_rows(r, small_shard_shapes) for r in res_small]

    outs = [{}, {}, {}, {}]
    for k in range(4):
        for i, n in enumerate(BIG):
            outs[k][n] = res_big[k][i].reshape(wts[n].shape)
        for i, n in enumerate(small_names):
            outs[k][n] = res_small[k][i].reshape(wts[n].shape)

    loss = lax.psum(0.5 * jnp.sum(loss_vec) / d, ("x", "y", "c"))
    result = [loss, dx[None]]
    for k in range(4):
        result += [outs[k][n] for n in ALL_WEIGHTS]
    return tuple(result)
```

```python
import functools
import math

import numpy as np
import jax
import jax.numpy as jnp
from jax import lax
from jax.experimental import pallas as pl
from jax.experimental.pallas import tpu as pltpu

F32 = jnp.float32
BF16 = jnp.bfloat16
MXU_DTYPE = BF16
ACT_DTYPE = BF16

NDEV = 8
EPS = 1e-6
HD = 64
QB = 128
PATTERNS = ((128, 1), (512, 4), (2048, 16))
ALIBI_MAX_EXP = 8.0
SSD_P = 64
SSD_N = 128
SSD_G = 4
SSD_Q = 128
SSD_K = 4
NEG = -1e30
LANE = 128
ROW_ALIGN = 16
VMEM_LIMIT = 56 * 1024 * 1024

ADAM_LR, ADAM_B1, ADAM_B2, ADAM_EPS, ADAM_WD, ADAM_STEP = 0.001, 0.9, 0.999, 1e-8, 0.01, 10

NN = (((1,), (0,)), ((), ()))
NT = (((1,), (1,)), ((), ()))
TN = (((0,), (0,)), ((), ()))


def _dot(a, b, dims=NN):
    return lax.dot_general(a.astype(MXU_DTYPE), b.astype(MXU_DTYPE), dims,
                           preferred_element_type=F32)


def _split3(a):
    hi = a.astype(BF16)
    r = a - hi.astype(F32)
    mid = r.astype(BF16)
    lo = (r - mid.astype(F32)).astype(BF16)
    return hi, mid, lo


def _dot3(a, b, dims=NN, split=0):
    if split == 0:
        bb = b.astype(BF16)
        parts = [lax.dot_general(s, bb, dims, preferred_element_type=F32) for s in _split3(a)]
    else:
        aa = a.astype(BF16)
        parts = [lax.dot_general(aa, s, dims, preferred_element_type=F32) for s in _split3(b)]
    return parts[0] + parts[1] + parts[2]


@jax.custom_vjp
def _spread(v, e):
    return _dot3(v, e)


def _spread_fwd(v, e):
    return _dot3(v, e), e


def _spread_bwd(e, g):
    return _dot3(g, e, NT), jnp.zeros_like(e)


_spread.defvjp(_spread_fwd, _spread_bwd)


@jax.custom_vjp
def _running_sum(a, lower):
    return _dot3(lower, a, NN, split=1)


def _running_sum_fwd(a, lower):
    return _dot3(lower, a, NN, split=1), lower


def _running_sum_bwd(lower, g):
    return _dot3(lower, g, TN, split=1), jnp.zeros_like(lower)


_running_sum.defvjp(_running_sum_fwd, _running_sum_bwd)


def _tile(n, cap):
    if n <= cap:
        return n
    best = None
    for t in range(LANE, cap + 1, LANE):
        if n % t == 0:
            best = t
    assert best is not None, (n, cap)
    return best


def _params(sem):
    return pltpu.CompilerParams(dimension_semantics=sem, vmem_limit_bytes=VMEM_LIMIT)


def _round_up(n, m):
    return -(-n // m) * m


def _mm(name, a, b, mode, out_dtype=F32, res=None, scale=1.0,
        cap_m=1024, cap_n=1408, cap_k=1024):
    segs = list(a) if isinstance(a, (list, tuple)) else [a]
    nseg = len(segs)
    if mode == "tn":
        k = segs[0].shape[0]
        widths = [s.shape[1] for s in segs]
        m = sum(widths)
        k2, n = b.shape
        tm = _tile(math.gcd(*widths), cap_m)
        tk = _tile(k, cap_k)
        counts = [wd // tm for wd in widths]
    else:
        m = segs[0].shape[0]
        widths = [s.shape[1] for s in segs]
        k = sum(widths)
        (k2, n) = b.shape if mode == "nn" else b.shape[::-1]
        tm = _tile(m, cap_m)
        tk = _tile(math.gcd(*widths), cap_k)
        counts = [wd // tk for wd in widths]
    assert k == k2, (name, [s.shape for s in segs], b.shape, mode)
    tn = _tile(n, cap_n)
    nk = k // tk
    starts = [sum(counts[:s]) for s in range(nseg)]
    dims = {"nn": NN, "nt": NT, "tn": TN}[mode]

    def a_spec(s):
        lo, cnt = starts[s], counts[s]
        if mode == "tn":
            if nseg == 1:
                return pl.BlockSpec((tk, tm), lambda i, j, kk: (kk, i))
            return pl.BlockSpec(
                (tk, tm), lambda i, j, kk: (jnp.where((i >= lo) & (i < lo + cnt), kk, 0),
                                            jnp.clip(i - lo, 0, cnt - 1)))
        if nseg == 1:
            return pl.BlockSpec((tm, tk), lambda i, j, kk: (i, kk))
        return pl.BlockSpec((tm, tk), lambda i, j, kk: (i, jnp.clip(kk - lo, 0, cnt - 1)))

    b_spec = (pl.BlockSpec((tn, tk), lambda i, j, kk: (j, kk)) if mode == "nt"
              else pl.BlockSpec((tk, tn), lambda i, j, kk: (kk, j)))
    o_spec = pl.BlockSpec((tm, tn), lambda i, j, kk: (i, j))
    has_res = res is not None
    use_acc = nk > 1 or nseg > 1

    def body(*refs):
        a_refs = refs[:nseg]
        b_ref = refs[nseg]
        r_ref = refs[nseg + 1] if has_res else None
        o_ref = refs[nseg + 1 + has_res]
        scr = refs[nseg + 2 + has_res:]

        def finish(acc):
            if scale != 1.0:
                acc = acc * scale
            if has_res:
                acc = r_ref[...].astype(F32) + acc
            o_ref[...] = acc.astype(o_ref.dtype)

        if not use_acc:
            finish(_dot(a_refs[0][...], b_ref[...], dims))
            return
        acc_ref = scr[0]
        kk = pl.program_id(2)
        sel = pl.program_id(0) if mode == "tn" else kk

        @pl.when(kk == 0)
        def _():
            acc_ref[...] = jnp.zeros_like(acc_ref)

        for s in range(nseg):
            def add(s=s):
                acc_ref[...] += _dot(a_refs[s][...], b_ref[...], dims)
            if nseg == 1:
                add()
            else:
                pl.when((sel >= starts[s]) & (sel < starts[s] + counts[s]))(add)

        @pl.when(kk == nk - 1)
        def _():
            finish(acc_ref[...])

    in_specs = [a_spec(s) for s in range(nseg)] + [b_spec] + ([o_spec] if has_res else [])
    args = tuple(segs) + (b,) + ((res,) if has_res else ())
    return pl.pallas_call(
        body, name=name,
        out_shape=jax.ShapeDtypeStruct((m, n), out_dtype),
        grid=(m // tm, n // tn, nk),
        in_specs=in_specs, out_specs=o_spec,
        scratch_shapes=[pltpu.VMEM((tm, tn), F32)] if use_acc else [],
        compiler_params=_params(("parallel", "parallel", "arbitrary")),
    )(*args)


def _rw(name, fn, ins, outs, accs=(), tr=256, ncb=1):
    t = next(a.shape[0] for kind, a, _, _ in ins if kind == "row")
    assert t % tr == 0
    n_in = len(ins)
    n_pieces = sum(len(w) for w, _ in outs)

    def spec(kind, arr, width, base):
        if kind == "row":
            return pl.BlockSpec((tr, width), lambda j, i: (i, base + j))
        return pl.BlockSpec((arr.shape[0], width), lambda j, i: (0, base + j))

    in_specs = [spec(*s) for s in ins]
    out_shapes, out_specs = [], []
    for widths, dt in outs:
        w = sum(widths)
        out_shapes.append(jax.ShapeDtypeStruct((t, w * ncb), dt))
        out_specs.append(pl.BlockSpec((tr, w), lambda j, i: (i, j)))
    for rows, width in accs:
        out_shapes.append(jax.ShapeDtypeStruct((rows, width * ncb), F32))
        out_specs.append(pl.BlockSpec((rows, width), lambda j, i: (0, j)))

    def body(*refs):
        vals = [r[...] for r in refs[:n_in]]
        res = fn(*vals)
        o_refs = refs[n_in:n_in + len(outs)]
        a_refs = refs[n_in + len(outs):]
        p = 0
        for (widths, _), o_ref in zip(outs, o_refs):
            off = 0
            for w in widths:
                if len(widths) == 1:
                    o_ref[...] = res[p].astype(o_ref.dtype)
                else:
                    o_ref[:, off:off + w] = res[p].astype(o_ref.dtype)
                off += w
                p += 1
        i = pl.program_id(1)
        for a_ref, v in zip(a_refs, res[n_pieces:]):
            @pl.when(i == 0)
            def _(a_ref=a_ref, v=v):
                a_ref[...] = v

            @pl.when(i > 0)
            def _(a_ref=a_ref, v=v):
                a_ref[...] += v

    return pl.pallas_call(
        body, name=name, out_shape=out_shapes,
        grid=(ncb, t // tr), in_specs=in_specs, out_specs=out_specs,
        compiler_params=_params(("parallel", "arbitrary")),
    )(*[a for _, a, _, _ in ins])


def _rms(x, g):
    x = x.astype(F32)
    return x * lax.rsqrt(jnp.mean(x * x, axis=-1, keepdims=True) + EPS) * g


def _silu(x):
    return x * jax.nn.sigmoid(x)


def _colsum(v):
    return jnp.sum(v, axis=0, keepdims=True)


def _attn_pair(q, kc, kp, vc, vp, qg, kg, coef0, coef1, first):
    w = 2 * HD
    ri = lax.broadcasted_iota(jnp.int32, (w, w), 0)
    ci = lax.broadcasted_iota(jnp.int32, (w, w), 1)
    same_head = ((ri < HD) == (ci < HD)).astype(F32)
    lane = lax.broadcasted_iota(jnp.int32, (1, w), 1)

    def norm(x, g):
        ms = _spread(x * x, same_head) * (1.0 / HD)
        return x * lax.rsqrt(ms + EPS) * g

    qn, kcn, kpn = norm(q, qg), norm(kc, kg), norm(kp, kg)
    scale = 1.0 / math.sqrt(HD)
    a_idx = lax.broadcasted_iota(jnp.int32, (QB, QB), 0)
    c_idx = lax.broadcasted_iota(jnp.int32, (QB, QB), 1)
    rel_c = a_idx - c_idx
    dist_c = rel_c.astype(F32)
    dist_p = dist_c + float(QB)
    keep_c = rel_c >= 0
    keep_p = jnp.logical_and(rel_c <= 0, jnp.logical_not(first))
    out = jnp.zeros((QB, w), F32)
    lb = jnp.zeros((QB, w), F32)
    for hh, coef in enumerate((coef0, coef1)):
        mask = ((lane < HD) if hh == 0 else (lane >= HD)).astype(F32)
        qm = qn * mask
        lc = jnp.where(keep_c, _dot(qm, kcn, NT) * scale - coef * dist_c, NEG)
        lp = jnp.where(keep_p, _dot(qm, kpn, NT) * scale - coef * dist_p, NEG)
        m = lax.stop_gradient(jnp.maximum(jnp.max(lc, axis=-1, keepdims=True),
                                          jnp.max(lp, axis=-1, keepdims=True)))
        pc = jnp.exp(lc - m)
        pp = jnp.exp(lp - m)
        l = jnp.sum(pc, axis=-1, keepdims=True) + jnp.sum(pp, axis=-1, keepdims=True)
        inv = 1.0 / l
        out = out + (_dot(pc * inv, vc) + _dot(pp * inv, vp)) * mask
        lb = lb + (m + jnp.log(l)) * mask
    return out, lb


def _stream_rows(n, r, d):
    start = pl.multiple_of(n * (QB * d), QB * d)
    if d == 1:
        return pl.ds(start, QB)
    return pl.ds(start + r, QB, stride=d)


def _attn_specs(t, bases):
    w = 2 * HD
    ins = [pl.BlockSpec((t, w), functools.partial(lambda p, c, b: (0, b + p), b=b)) for b in bases]
    gain = pl.BlockSpec((1, w), lambda p, c: (0, 0))
    blk = pl.BlockSpec((t, w), lambda p, c: (0, p))
    return ins, gain, blk


def _attn_fwd(name, proj, bases, qg, kg, coefs, d):
    t = proj.shape[0]
    npairs = coefs.shape[0] // 2
    nb = t // (d * QB)
    w = 2 * HD
    ins, gain, blk = _attn_specs(t, bases)

    def body(coef_ref, q_ref, k_ref, v_ref, qg_ref, kg_ref, o_ref, l_ref, qf, kf, vf):
        p = pl.program_id(0)
        c0, c1 = coef_ref[2 * p], coef_ref[2 * p + 1]
        qf[...] = q_ref[...].astype(F32)
        kf[...] = k_ref[...].astype(F32)
        vf[...] = v_ref[...].astype(F32)

        def stream(r, carry):
            def step(n, c2):
                cur = _stream_rows(n, r, d)
                prv = _stream_rows(jnp.maximum(n - 1, 0), r, d)
                o, lb = _attn_pair(qf[cur, :], kf[cur, :], kf[prv, :], vf[cur, :], vf[prv, :],
                                   qg_ref[...], kg_ref[...], c0, c1, n == 0)
                o_ref[cur, :] = o
                l_ref[cur, :] = lb
                return c2
            return lax.fori_loop(0, nb, step, carry)

        lax.fori_loop(0, d, stream, 0)

    return pl.pallas_call(
        body, name=name,
        out_shape=[jax.ShapeDtypeStruct((t, npairs * w), F32)] * 2,
        grid_spec=pltpu.PrefetchScalarGridSpec(
            num_scalar_prefetch=1, grid=(npairs,),
            in_specs=ins + [gain, gain], out_specs=[blk, blk],
            scratch_shapes=[pltpu.VMEM((t, w), F32)] * 3),
        compiler_params=_params(("arbitrary",)),
    )(coefs, proj, proj, proj, qg, kg)


def _attn_bwd(name, proj, bases, qg, kg, coefs, d, do, dl):
    t = proj.shape[0]
    npairs = coefs.shape[0] // 2
    nb = t // (d * QB)
    w = 2 * HD
    ins, gain, blk = _attn_specs(t, bases)

    def body(coef_ref, q_ref, k_ref, v_ref, qg_ref, kg_ref, do_ref, dl_ref,
             dq_ref, dk_ref, dv_ref, dqg_ref, dkg_ref, qf, kf, vf, dqf, dkf, dvf):
        p = pl.program_id(0)
        c0, c1 = coef_ref[2 * p], coef_ref[2 * p + 1]
        qf[...] = q_ref[...].astype(F32)
        kf[...] = k_ref[...].astype(F32)
        vf[...] = v_ref[...].astype(F32)
        dkf[...] = jnp.zeros_like(dkf)
        dvf[...] = jnp.zeros_like(dvf)

        def stream(r, carry):
            def step(n, c2):
                dqg_acc, dkg_acc = c2
                cur = _stream_rows(n, r, d)
                prv = _stream_rows(jnp.maximum(n - 1, 0), r, d)
                first = n == 0
                f = lambda a, b, c, e, g, g1, g2: _attn_pair(a, b, c, e, g, g1, g2, c0, c1, first)
                _, vjp = jax.vjp(f, qf[cur, :], kf[cur, :], kf[prv, :], vf[cur, :], vf[prv, :],
                                 qg_ref[...], kg_ref[...])
                dq, dkc, dkp, dvc, dvp, dqg, dkg = vjp((do_ref[cur, :], dl_ref[cur, :]))
                dqf[cur, :] = dq
                dkf[cur, :] += dkc
                dkf[prv, :] += dkp
                dvf[cur, :] += dvc
                dvf[prv, :] += dvp
                return dqg_acc + dqg, dkg_acc + dkg
            return lax.fori_loop(0, nb, step, carry)

        zero = jnp.zeros((1, w), F32)
        dqg, dkg = lax.fori_loop(0, d, stream, (zero, zero))
        dq_ref[...] = dqf[...].astype(dq_ref.dtype)
        dk_ref[...] = dkf[...].astype(dk_ref.dtype)
        dv_ref[...] = dvf[...].astype(dv_ref.dtype)

        @pl.when(p == 0)
        def _():
            dqg_ref[...] = dqg
            dkg_ref[...] = dkg

        @pl.when(p > 0)
        def _():
            dqg_ref[...] += dqg
            dkg_ref[...] += dkg

    big = jax.ShapeDtypeStruct((t, npairs * w), ACT_DTYPE)
    small = jax.ShapeDtypeStruct((1, w), F32)
    return pl.pallas_call(
        body, name=name,
        out_shape=[big, big, big, small, small],
        grid_spec=pltpu.PrefetchScalarGridSpec(
            num_scalar_prefetch=1, grid=(npairs,),
            in_specs=ins + [gain, gain, blk, blk],
            out_specs=[blk, blk, blk, gain, gain],
            scratch_shapes=[pltpu.VMEM((t, w), F32)] * 6),
        compiler_params=_params(("arbitrary",)),
    )(coefs, proj, proj, proj, qg, kg, do, dl)


def _shift_down(u, s):
    if s == 0:
        return u
    rows = lax.broadcasted_iota(jnp.int32, u.shape, 0)
    return jnp.where(rows >= s, pltpu.roll(u, s, 0), 0.0)


def _shift_up(u, s):
    if s == 0:
        return u
    t = u.shape[0]
    rows = lax.broadcasted_iota(jnp.int32, u.shape, 0)
    return jnp.where(rows < t - s, pltpu.roll(u, t - s, 0), 0.0)


def _conv_pre(u, w, b):
    y = b
    for kk in range(SSD_K):
        y = y + w[kk:kk + 1, :] * _shift_down(u, SSD_K - 1 - kk)
    return y


def _conv_fwd(name, src, base, w, b, cw=128):
    t = src.shape[0]
    c = w.shape[1]

    def body(u_ref, w_ref, b_ref, o_ref):
        y = _conv_pre(u_ref[...].astype(F32), w_ref[...], b_ref[...])
        o_ref[...] = _silu(y).astype(o_ref.dtype)

    return pl.pallas_call(
        body, name=name, out_shape=jax.ShapeDtypeStruct((t, c), ACT_DTYPE),
        grid=(c // cw,),
        in_specs=[pl.BlockSpec((t, cw), lambda j: (0, base + j)),
                  pl.BlockSpec((SSD_K, cw), lambda j: (0, j)),
                  pl.BlockSpec((1, cw), lambda j: (0, j))],
        out_specs=pl.BlockSpec((t, cw), lambda j: (0, j)),
        compiler_params=_params(("parallel",)),
    )(src, w, b)


def _conv_bwd(name, src, base, w, b, dout, cw=128):
    t = src.shape[0]
    c = w.shape[1]

    def body(u_ref, w_ref, b_ref, d_ref, du_ref, dw_ref, db_ref):
        u = u_ref[...].astype(F32)
        wv = w_ref[...]
        y = _conv_pre(u, wv, b_ref[...])
        sg = jax.nn.sigmoid(y)
        dy = d_ref[...].astype(F32) * (sg * (1.0 + y * (1.0 - sg)))
        du = jnp.zeros_like(u)
        for kk in range(SSD_K):
            s = SSD_K - 1 - kk
            du = du + wv[kk:kk + 1, :] * _shift_up(dy, s)
            dw_ref[kk:kk + 1, :] = _colsum(dy * _shift_down(u, s))
        du_ref[...] = du.astype(du_ref.dtype)
        db_ref[...] = _colsum(dy)

    return pl.pallas_call(
        body, name=name,
        out_shape=[jax.ShapeDtypeStruct((t, c), ACT_DTYPE),
                   jax.ShapeDtypeStruct((SSD_K, c), F32),
                   jax.ShapeDtypeStruct((1, c), F32)],
        grid=(c // cw,),
        in_specs=[pl.BlockSpec((t, cw), lambda j: (0, base + j)),
                  pl.BlockSpec((SSD_K, cw), lambda j: (0, j)),
                  pl.BlockSpec((1, cw), lambda j: (0, j)),
                  pl.BlockSpec((t, cw), lambda j: (0, j))],
        out_specs=[pl.BlockSpec((t, cw), lambda j: (0, j)),
                   pl.BlockSpec((SSD_K, cw), lambda j: (0, j)),
                   pl.BlockSpec((1, cw), lambda j: (0, j))],
        compiler_params=_params(("parallel",)),
    )(src, w, b, dout)


def _softplus(x):
    return jnp.maximum(x, 0.0) + jnp.log(1.0 + jnp.exp(-jnp.abs(x)))


def _ssd_chunk(xbc, dtraw, bias, alog, states):
    wd = states[0].shape[1]
    nj = wd // SSD_P
    inner = SSD_G * wd
    dt = _softplus(dtraw + bias)
    a = dt * (-jnp.exp(alog))
    li = lax.broadcasted_iota(jnp.int32, (SSD_Q, SSD_Q), 0)
    si = lax.broadcasted_iota(jnp.int32, (SSD_Q, SSD_Q), 1)
    causal = li >= si
    acs = _running_sum(a, causal.astype(F32))
    acs_t = acs.T
    a_last = acs[SSD_Q - 1:SSD_Q, :]
    grow = jnp.exp(acs)
    shrink = jnp.exp(a_last - acs)
    hrow = lax.broadcasted_iota(jnp.int32, (LANE, wd), 0)
    wcol = lax.broadcasted_iota(jnp.int32, (LANE, wd), 1)
    lane = lax.broadcasted_iota(jnp.int32, (1, LANE), 1)
    ys, snext = [], []
    for g in range(SSD_G):
        lo = (hrow - g * nj) * SSD_P
        head_lanes = jnp.logical_and(wcol >= lo, wcol < lo + SSD_P).astype(F32)
        xs = xbc[:, g * wd:(g + 1) * wd]
        bm = xbc[:, inner + g * SSD_N:inner + (g + 1) * SSD_N]
        cm = xbc[:, inner + (SSD_G + g) * SSD_N:inner + (SSD_G + g + 1) * SSD_N]
        xdt = xs * _spread(dt, head_lanes)
        grow_x = _spread(grow, head_lanes)
        y_off = _dot(cm, states[g]) * grow_x
        s_new = (states[g] * grow_x[SSD_Q - 1:SSD_Q, :]
                 + _dot(bm, xdt * _spread(shrink, head_lanes), TN))
        cb = _dot(cm, bm, NT)
        pieces = []
        for i in range(wd // LANE):
            xp = xdt[:, i * LANE:(i + 1) * LANE]
            acc = jnp.zeros((SSD_Q, LANE), F32)
            for hh in range(LANE // SSD_P):
                h = g * nj + i * (LANE // SSD_P) + hh
                decay = jnp.exp(jnp.where(causal, acs[:, h:h + 1] - acs_t[h:h + 1, :], NEG))
                keep = jnp.logical_and(lane >= hh * SSD_P, lane < (hh + 1) * SSD_P).astype(F32)
                acc = acc + _dot(cb * decay, xp * keep)
            pieces.append(acc)
        y_diag = pieces[0] if len(pieces) == 1 else jnp.concatenate(pieces, axis=1)
        ys.append(y_diag + y_off)
        snext.append(s_new)
    return ys, snext


def _ssd_specs(cdim, wd, rev, nc):
    ch = (lambda c: nc - 1 - c) if rev else (lambda c: c)
    full = lambda width: pl.BlockSpec((SSD_Q, width), lambda c: (ch(c), 0))
    vec = pl.BlockSpec((1, LANE), lambda c: (0, 0))
    st = pl.BlockSpec((1, SSD_G, SSD_N, wd), lambda c: (ch(c), 0, 0, 0))
    return full, vec, st


def _ssd_fwd(name, xbc, dtraw, bias, alog, inner):
    t, cdim = xbc.shape
    wd = inner // SSD_G
    nc = t // SSD_Q
    full, vec, st = _ssd_specs(cdim, wd, False, nc)

    def body(x_ref, r_ref, b_ref, a_ref, y_ref, st_ref, s_scr):
        @pl.when(pl.program_id(0) == 0)
        def _():
            s_scr[...] = jnp.zeros_like(s_scr)

        sprev = [s_scr[g] for g in range(SSD_G)]
        ys, snext = _ssd_chunk(x_ref[...].astype(F32), r_ref[...], b_ref[...], a_ref[...], sprev)
        for g in range(SSD_G):
            st_ref[0, g] = sprev[g]
            y_ref[:, g * wd:(g + 1) * wd] = ys[g]
            s_scr[g] = snext[g]

    return pl.pallas_call(
        body, name=name,
        out_shape=[jax.ShapeDtypeStruct((t, inner), F32),
                   jax.ShapeDtypeStruct((nc, SSD_G, SSD_N, wd), F32)],
        grid=(nc,),
        in_specs=[full(cdim), full(LANE), vec, vec],
        out_specs=[full(inner), st],
        scratch_shapes=[pltpu.VMEM((SSD_G, SSD_N, wd), F32)],
        compiler_params=_params(("arbitrary",)),
    )(xbc, dtraw, bias, alog)


def _ssd_bwd(name, xbc, dtraw, bias, alog, states, dy, dxs_extra):
    t, cdim = xbc.shape
    inner = dy.shape[1]
    wd = inner // SSD_G
    nc = t // SSD_Q
    full, vec, st = _ssd_specs(cdim, wd, True, nc)

    def body(x_ref, r_ref, b_ref, a_ref, st_ref, dy_ref, dx0_ref,
             dx_ref, dr_ref, db_ref, da_ref, ds_scr):
        first = pl.program_id(0) == 0

        @pl.when(first)
        def _():
            ds_scr[...] = jnp.zeros_like(ds_scr)

        sprev = [st_ref[0, g] for g in range(SSD_G)]
        _, vjp = jax.vjp(_ssd_chunk, x_ref[...].astype(F32), r_ref[...], b_ref[...], a_ref[...],
                         sprev)
        dyv = dy_ref[...]
        dys = [dyv[:, g * wd:(g + 1) * wd] for g in range(SSD_G)]
        dsn = [ds_scr[g] for g in range(SSD_G)]
        dx, dr, db, da, dsp = vjp((dys, dsn))
        dx_ref[:, :inner] = dx[:, :inner] + dx0_ref[...].astype(F32)
        dx_ref[:, inner:] = dx[:, inner:]
        dr_ref[...] = dr
        for g in range(SSD_G):
            ds_scr[g] = dsp[g]

        @pl.when(first)
        def _():
            db_ref[...] = db
            da_ref[...] = da

        @pl.when(jnp.logical_not(first))
        def _():
            db_ref[...] += db
            da_ref[...] += da

    return pl.pallas_call(
        body, name=name,
        out_shape=[jax.ShapeDtypeStruct((t, cdim), F32),
                   jax.ShapeDtypeStruct((t, LANE), F32),
                   jax.ShapeDtypeStruct((1, LANE), F32),
                   jax.ShapeDtypeStruct((1, LANE), F32)],
        grid=(nc,),
        in_specs=[full(cdim), full(LANE), vec, vec, st, full(inner), full(inner)],
        out_specs=[full(cdim), full(LANE), vec, vec],
        scratch_shapes=[pltpu.VMEM((SSD_G, SSD_N, wd), F32)],
        compiler_params=_params(("arbitrary",)),
    )(xbc, dtraw, bias, alog, states, dy, dxs_extra)


def _act(g, u):
    return _silu(g.astype(F32)) * u.astype(F32)


def _mix(o0, o1, o2, l0, l1, l2):
    m = lax.stop_gradient(jnp.maximum(jnp.maximum(l0, l1), l2))
    e0, e1, e2 = jnp.exp(l0 - m), jnp.exp(l1 - m), jnp.exp(l2 - m)
    return (e0 * o0 + e1 * o1 + e2 * o2) / (e0 + e1 + e2)


def _gate(y, xs, z, dexp, gain):
    v = (y + xs.astype(F32) * dexp) * _silu(z.astype(F32))
    return _rms(v, gain)


def _merge(ga, gs, ap, sp):
    return jax.nn.sigmoid(ga.astype(F32)) * ap + jax.nn.sigmoid(gs.astype(F32)) * sp


def _alibi_coefs(hp):
    n = hp * len(PATTERNS)
    slopes = np.exp2(-ALIBI_MAX_EXP * np.arange(1, n + 1, dtype=np.float32) / n).astype(np.float32)
    return [jnp.asarray(slopes[g * hp:(g + 1) * hp] * np.float32(d))
            for g, (_, d) in enumerate(PATTERNS)]


def _local_step(x, tgt, w, p):
    t, d = x.shape
    dff = w["d1"].shape[0]
    aw = w["abt"].shape[1]
    hp = aw // HD
    qkv = len(PATTERNS) * aw
    inner = w["sb"].shape[0]
    nh = p["dt_bias"].shape[1]
    gw = inner // SSD_G
    cdim = inner + 2 * SSD_G * SSD_N
    z_off, xbc_off = 3 * qkv, 3 * qkv + inner
    ga_off = xbc_off + cdim
    gs_off = ga_off + d
    hw = d // 2
    assert z_off % gw == 0 and xbc_off % LANE == 0 and ga_off % hw == 0 and gs_off % hw == 0
    assert (nh // SSD_G) * SSD_P == gw and hp % 2 == 0 and aw % LANE == 0 and nh <= LANE
    gdt = MXU_DTYPE

    row = lambda a, width, base=0: ("row", a, width, base)
    const = lambda a, width, base=0: ("const", a, width, base)

    def rms_fwd(name, xin, g):
        return _rw(name, lambda xv, gv: (_rms(xv, gv),), [row(xin, d), const(g, d)],
                   [((d,), ACT_DTYPE)])[0]

    def rms_bwd(name, xin, g, dh, dres):
        def fn(xv, gv, dhv, drv):
            _, vjp = jax.vjp(_rms, xv, gv)
            dx, dg = vjp(dhv.astype(F32))
            return drv + dx, dg
        return _rw(name, fn, [row(xin, d), const(g, d), row(dh, d), row(dres, d)],
                   [((d,), F32)], accs=[(1, d)])

    def ffn_fwd(tag, xin, g, wgut, wd):
        h = rms_fwd(tag + "_norm", xin, g)
        gu = _mm(tag + "_up", h, wgut, "nt", ACT_DTYPE)
        a = _rw(tag + "_act", lambda gv, uv: (_act(gv, uv),),
                [row(gu, dff, 0), row(gu, dff, 1)], [((dff,), ACT_DTYPE)])[0]
        xo = _mm(tag + "_down", a, wd, "nn", F32, res=xin, scale=0.5)
        return xo, (h, gu, a)

    def ffn_bwd(tag, xin, g, wgut, wd, saved, dxo):
        h, gu, a = saved
        da = _mm(tag + "_da", dxo, wd, "nt", ACT_DTYPE, scale=0.5)
        dwd = _mm(tag + "_dwd", a, dxo, "tn", gdt, scale=0.5)

        def fn(gv, uv, dav):
            _, vjp = jax.vjp(_act, gv, uv)
            return vjp(dav.astype(F32))
        dgu = _rw(tag + "_dact", fn, [row(gu, dff, 0), row(gu, dff, 1), row(da, dff)],
                  [((dff, dff), ACT_DTYPE)])[0]
        dwgut = _mm(tag + "_dwgu", dgu, h, "tn", gdt)
        dh = _mm(tag + "_dh", dgu, wgut, "nn", F32)
        dx, dg = rms_bwd(tag + "_dnorm", xin, g, dh, dxo)
        return dx, dg, dwgut, dwd

    x1, ffn1_saved = ffn_fwd("ffn1", x, p["ffn1_norm"], w["gu1t"], w["d1"])
    h2 = rms_fwd("mix_norm", x1, p["mix_norm"])
    proj = _mm("in_proj", h2, w["maint"], "nt", ACT_DTYPE)
    dtraw = _mm("dt_proj", h2, w["dtt"], "nt", F32)

    coefs = _alibi_coefs(hp)
    qg2 = jnp.concatenate([p["q_norm"], p["q_norm"]], axis=1)
    kg2 = jnp.concatenate([p["k_norm"], p["k_norm"]], axis=1)
    pw = 2 * HD
    attn_bases = [[(off + gi * aw) // pw for off in (0, qkv, 2 * qkv)]
                  for gi in range(len(PATTERNS))]
    attn_o, attn_l = [], []
    for gi, (_, dil) in enumerate(PATTERNS):
        o, l = _attn_fwd(f"attn_fwd{gi}", proj, attn_bases[gi], qg2, kg2, coefs[gi], dil)
        attn_o.append(o)
        attn_l.append(l)
    ao = _rw("attn_mix", lambda *v: (_mix(*v),), [row(a, aw) for a in attn_o + attn_l],
             [((aw,), ACT_DTYPE)])[0]

    xbc = _conv_fwd("conv_fwd", proj, xbc_off // LANE, p["conv_w"], p["conv_b"])
    pad = lambda v: jnp.pad(v, ((0, 0), (0, LANE - nh)))
    bias_p, alog_p = pad(p["dt_bias"]), pad(p["a_log"])
    yssd, states = _ssd_fwd("ssd_fwd", xbc, dtraw, bias_p, alog_p, inner)
    dexp = jnp.repeat(p["d_skip"], SSD_P, axis=1)
    gate_ins = [row(yssd, gw), row(xbc, gw), row(proj, gw, z_off // gw),
                const(dexp, gw), const(p["ssd_norm"], gw)]
    yn = _rw("ssd_gate", lambda *v: (_gate(*v),), gate_ins, [((gw,), ACT_DTYPE)], ncb=SSD_G)[0]

    ap = _mm("attn_out", ao, w["abt"], "nt", F32)
    sp = _mm("ssd_out", yn, w["sb"], "nn", F32)
    merge_ins = [row(proj, hw, ga_off // hw), row(proj, hw, gs_off // hw), row(ap, hw), row(sp, hw)]
    mg = _rw("merge", lambda *v: (_merge(*v),), merge_ins, [((hw,), ACT_DTYPE)], ncb=2)[0]
    x2 = _mm("mix_out", mg, w["out"], "nn", F32, res=x1)
    x3, ffn2_saved = ffn_fwd("ffn2", x2, p["ffn2_norm"], w["gu2t"], w["d2"])

    def loss_fn(yv, tv):
        e = yv - tv
        return e * (1.0 / d), _colsum(e * e)
    dy, loss_vec = _rw("loss", loss_fn, [row(x3, d), row(tgt, d)], [((d,), F32)], accs=[(1, d)])

    gw_, gp = {}, {}
    dx2, gp["ffn2_norm"], gw_["gu2t"], gw_["d2"] = ffn_bwd(
        "ffn2", x2, p["ffn2_norm"], w["gu2t"], w["d2"], ffn2_saved, dy)
    dmg = _mm("d_merge", dx2, w["out"], "nt", ACT_DTYPE)
    gw_["out"] = _mm("dw_out", mg, dx2, "tn", gdt)

    def merge_bwd(gav, gsv, apv, spv, dv):
        _, vjp = jax.vjp(_merge, gav, gsv, apv, spv)
        return vjp(dv.astype(F32))
    dga, dgs, dap, dsp = _rw("d_merge_gate", merge_bwd, merge_ins + [row(dmg, hw)],
                             [((hw,), ACT_DTYPE)] * 4, ncb=2)
    gw_["abt"] = _mm("dw_ab", dap, ao, "tn", gdt)
    dao = _mm("d_attn_o", dap, w["abt"], "nn", F32)
    gw_["sb"] = _mm("dw_sb", yn, dsp, "tn", gdt)
    dyn = _mm("d_ssd_y", dsp, w["sb"], "nt", F32)

    def gate_bwd(yv, xv, zv, dev, gv, dv):
        _, vjp = jax.vjp(_gate, yv, xv, zv, dev, gv)
        return vjp(dv)
    dyssd, dxs_gate, dz, ddexp, gp["ssd_norm"] = _rw(
        "d_ssd_gate", gate_bwd, gate_ins + [row(dyn, gw)],
        [((gw,), F32), ((gw,), F32), ((gw,), ACT_DTYPE)], accs=[(1, gw), (1, gw)], ncb=SSD_G)
    gp["d_skip"] = ddexp.reshape(nh, SSD_P).sum(axis=1).reshape(1, nh)

    dxbc, ddtraw, dbias, dalog = _ssd_bwd("ssd_bwd", xbc, dtraw, bias_p, alog_p, states,
                                          dyssd, dxs_gate)
    gp["dt_bias"], gp["a_log"] = dbias[:, :nh], dalog[:, :nh]
    du, gp["conv_w"], gp["conv_b"] = _conv_bwd("conv_bwd", proj, xbc_off // LANE,
                                               p["conv_w"], p["conv_b"], dxbc)

    def mix_bwd(*v):
        _, vjp = jax.vjp(_mix, *v[:6])
        return vjp(v[6])
    dmix = _rw("d_attn_mix", mix_bwd, [row(a, aw) for a in attn_o + attn_l] + [row(dao, aw)],
               [((aw,), F32)] * 6)
    dq, dk, dv = [], [], []
    dqg = dkg = None
    for gi, (_, dil) in enumerate(PATTERNS):
        r = _attn_bwd(f"attn_bwd{gi}", proj, attn_bases[gi], qg2, kg2, coefs[gi], dil,
                      dmix[gi], dmix[3 + gi])
        dq.append(r[0])
        dk.append(r[1])
        dv.append(r[2])
        dqg = r[3] if dqg is None else dqg + r[3]
        dkg = r[4] if dkg is None else dkg + r[4]
    gp["q_norm"] = dqg[:, :HD] + dqg[:, HD:]
    gp["k_norm"] = dkg[:, :HD] + dkg[:, HD:]

    segs = dq + dk + dv + [dz, du, dga, dgs]
    gw_["maint"] = _mm("dw_in", segs, h2, "tn", gdt)
    gw_["dtt"] = _mm("dw_dt", ddtraw, h2, "tn", gdt)
    dh2 = _mm("d_h2_main", segs, w["maint"], "nn", F32)
    dh2 = _mm("d_h2_dt", ddtraw, w["dtt"], "nn", F32, res=dh2)
    dx1, gp["mix_norm"] = rms_bwd("d_mix_norm", x1, p["mix_norm"], dh2, dx2)
    dx0, gp["ffn1_norm"], gw_["gu1t"], gw_["d1"] = ffn_bwd(
        "ffn1", x, p["ffn1_norm"], w["gu1t"], w["d1"], ffn1_saved, dx1)
    return loss_vec, dx0, gw_, gp


MESH = pl.DeviceIdType.MESH
HBM_SPEC = pl.BlockSpec(memory_space=pltpu.HBM)


def _mesh_pos():
    return lax.axis_index("x"), lax.axis_index("y"), lax.axis_index("c")


def _flip(pos, k):
    x, y, c = pos
    return (1 - x if k & 4 else x, 1 - y if k & 2 else y, 1 - c if k & 1 else c)


def _dev_index(pos):
    return 4 * pos[0] + 2 * pos[1] + pos[2]


def _rows_of(ref, base, stride, rows, pos):
    start = pl.multiple_of(base + stride * _dev_index(pos), ROW_ALIGN)
    return ref.at[pl.ds(start, rows)]


def _gather(name, shards, dests, out_shapes):
    n = len(shards)
    n_out = len(out_shapes)

    def body(*refs):
        x_refs = refs[:n]
        o_refs = refs[n:n + n_out]
        send_sems, recv_sems, local_sems = refs[n + n_out:]
        me = _mesh_pos()
        sibling = _flip(me, 1)
        chips = [_flip(me, 4), _flip(me, 2), _flip(me, 6)]

        def slot(i, block):
            k_out, base, stride = dests[i]
            return _rows_of(o_refs[k_out], base, stride, shards[i].shape[0], block)

        def copy(i, k, block, to, src=None):
            dst = slot(i, block)
            return pltpu.make_async_remote_copy(
                src_ref=dst if src is None else src, dst_ref=dst,
                send_sem=send_sems.at[7 * i + k], recv_sem=recv_sems.at[7 * i + k],
                device_id=to, device_id_type=MESH)

        mine = [pltpu.make_async_copy(x_refs[i], slot(i, me), local_sems.at[i]) for i in range(n)]
        for cp in mine:
            cp.start()
        first = []
        for i in range(n):
            first.append(copy(i, 0, me, sibling, src=x_refs[i]))
            first += [copy(i, 1 + j, me, chip, src=x_refs[i]) for j, chip in enumerate(chips)]
        for cp in first:
            cp.start()
        passed = []
        for j, chip in enumerate(chips):
            for i in range(n):
                copy(i, 1 + j, chip, me).wait_recv()
                fwd = copy(i, 4 + j, chip, sibling)
                fwd.start()
                passed.append(fwd)
        for i in range(n):
            copy(i, 0, sibling, me).wait_recv()
            for j, chip in enumerate(chips):
                copy(i, 4 + j, _flip(chip, 1), me).wait_recv()
        for cp in first + passed:
            cp.wait_send()
        for cp in mine:
            cp.wait()

    return pl.pallas_call(
        body, name=name,
        out_shape=[jax.ShapeDtypeStruct(s, dt) for s, dt in out_shapes],
        in_specs=[HBM_SPEC] * n, out_specs=[HBM_SPEC] * n_out,
        scratch_shapes=[pltpu.SemaphoreType.DMA((7 * n,)), pltpu.SemaphoreType.DMA((7 * n,)),
                        pltpu.SemaphoreType.DMA((n,))],
    )(*shards)


def _exchange(name, grads, srcs, small):
    n = len(srcs)
    ng = len(grads)

    def body(*refs):
        g_refs = refs[:ng]
        m_ref = refs[ng]
        r_refs = refs[ng + 1:ng + 1 + n]
        s_ref = refs[ng + 1 + n]
        send_sems, recv_sems, local_sems = refs[ng + 2 + n:]
        me = _mesh_pos()
        my = _dev_index(me)

        def slab(i, pos):
            gi, base, stride, rows = srcs[i]
            return _rows_of(g_refs[gi], base, stride, rows, pos)

        own = [pltpu.make_async_copy(slab(i, me), r_refs[i].at[my], local_sems.at[i])
               for i in range(n)]
        own.append(pltpu.make_async_copy(m_ref, s_ref.at[my], local_sems.at[n]))
        for cp in own:
            cp.start()

        def copies(k, src_pos, slot_pos):
            peer = _flip(me, k)
            si = _dev_index(slot_pos)
            out = [pltpu.make_async_remote_copy(
                src_ref=slab(i, src_pos), dst_ref=r_refs[i].at[si],
                send_sem=send_sems.at[7 * i + k - 1], recv_sem=recv_sems.at[7 * i + k - 1],
                device_id=peer, device_id_type=MESH) for i in range(n)]
            out.append(pltpu.make_async_remote_copy(
                src_ref=m_ref, dst_ref=s_ref.at[si],
                send_sem=send_sems.at[7 * n + k - 1], recv_sem=recv_sems.at[7 * n + k - 1],
                device_id=peer, device_id_type=MESH))
            return out

        sent = [cp for k in range(1, NDEV) for cp in copies(k, _flip(me, k), me)]
        for cp in sent:
            cp.start()
        for k in range(1, NDEV):
            for cp in copies(k, me, _flip(me, k)):
                cp.wait_recv()
        for cp in sent:
            cp.wait_send()
        for cp in own:
            cp.wait()

    out_shape = [jax.ShapeDtypeStruct((NDEV, rows, grads[gi].shape[1]), grads[gi].dtype)
                 for gi, _, _, rows in srcs]
    out_shape.append(jax.ShapeDtypeStruct((NDEV,) + small.shape, small.dtype))
    return pl.pallas_call(
        body, name=name, out_shape=out_shape,
        in_specs=[HBM_SPEC] * (ng + 1), out_specs=[HBM_SPEC] * (n + 1),
        scratch_shapes=[pltpu.SemaphoreType.DMA((7 * (n + 1),)),
                        pltpu.SemaphoreType.DMA((7 * (n + 1),)),
                        pltpu.SemaphoreType.DMA((n + 1,))],
    )(*grads, small)


def _sum_slabs(name, a):
    s, r, c = a.shape

    def body(a_ref, o_ref):
        acc = a_ref[0].astype(F32)
        for i in range(1, s):
            acc = acc + a_ref[i].astype(F32)
        o_ref[...] = acc

    return pl.pallas_call(body, name=name, out_shape=jax.ShapeDtypeStruct((r, c), F32))(a)


def _adamw_update(g, w, m, v):
    mn = ADAM_B1 * m + (1.0 - ADAM_B1) * g
    vn = ADAM_B2 * v + (1.0 - ADAM_B2) * (g * g)
    m_hat = mn / (1.0 - ADAM_B1 ** ADAM_STEP)
    v_hat = vn / (1.0 - ADAM_B2 ** ADAM_STEP)
    delta = -ADAM_LR * (m_hat / (jnp.sqrt(v_hat) + ADAM_EPS) + ADAM_WD * w)
    return delta, mn, vn


def _adamw(name, gsrc, w, m, v, transposed=False, tr=256):
    s = gsrc.shape[0]
    r, c = w.shape
    step = LANE if transposed else 8
    tr = max(t for t in range(step, min(tr, r) + 1, step) if r % t == 0)

    def body(g_ref, w_ref, m_ref, v_ref, go_ref, d_ref, mo_ref, vo_ref):
        g = g_ref[0].astype(F32)
        for i in range(1, s):
            g = g + g_ref[i].astype(F32)
        if transposed:
            g = g.T[:, :c]
        delta, mn, vn = _adamw_update(g, w_ref[...], m_ref[...], v_ref[...])
        go_ref[...] = g
        d_ref[...] = delta
        mo_ref[...] = mn
        vo_ref[...] = vn

    blk = pl.BlockSpec((tr, c), lambda i: (i, 0))
    if transposed:
        g_spec = pl.BlockSpec((s, gsrc.shape[1], tr), lambda i: (0, 0, i))
    else:
        g_spec = pl.BlockSpec((s, tr, c), lambda i: (0, i, 0))
    return pl.pallas_call(
        body, name=name, out_shape=[jax.ShapeDtypeStruct((r, c), F32)] * 4,
        grid=(r // tr,),
        in_specs=[g_spec, blk, blk, blk], out_specs=[blk] * 4,
        compiler_params=_params(("parallel",)),
    )(gsrc, w, m, v)


REPLICATED = ("ffn1_norm", "mix_norm", "q_norm", "k_norm", "conv_b", "dt_bias", "a_log",
              "d_skip", "ssd_norm", "ffn2_norm")
ALL_WEIGHTS = ("ffn1_norm", "ffn1_w_gate", "ffn1_w_up", "ffn1_w_down", "mix_norm", "w_in",
               "q_norm", "k_norm", "conv_w", "conv_b", "dt_bias", "a_log", "d_skip", "ssd_norm",
               "w_attn_branch", "w_ssd_branch", "w_out", "ffn2_norm", "ffn2_w_gate", "ffn2_w_up",
               "ffn2_w_down")
BIG = (("ffn1_w_gate", True, "gu1t", 0), ("ffn1_w_up", True, "gu1t", 1),
       ("ffn1_w_down", False, "d1", 0), ("w_in", True, "wint", 0),
       ("w_attn_branch", True, "abt", 0), ("w_ssd_branch", False, "sb", 0),
       ("w_out", False, "out", 0),
       ("ffn2_w_gate", True, "gu2t", 0), ("ffn2_w_up", True, "gu2t", 1),
       ("ffn2_w_down", False, "d2", 0))


def _nrows(shape, cols):
    return -(-math.prod(shape) // cols)


def _pack_rows(arrs, cols, row_tile):
    parts = []
    for a in arrs:
        flat = a.reshape(-1)
        nr = -(-flat.shape[0] // cols)
        parts.append(jnp.pad(flat, (0, nr * cols - flat.shape[0])).reshape(nr, cols))
    out = jnp.concatenate(parts, axis=0)
    return jnp.pad(out, ((0, _round_up(out.shape[0], row_tile) - out.shape[0]), (0, 0)))


def _unpack_rows(packed, shapes):
    cols = packed.shape[-1]
    out, r0 = [], 0
    for sh in shapes:
        nr = _nrows(sh, cols)
        out.append(packed[r0:r0 + nr].reshape(-1)[:math.prod(sh)].reshape(tuple(sh)))
        r0 += nr
    return out


def kernel(x, ffn1_norm, ffn1_w_gate, ffn1_w_up, ffn1_w_down, mix_norm, w_in, q_norm, k_norm, conv_w, conv_b, dt_bias, a_log, d_skip, ssd_norm, w_attn_branch, w_ssd_branch, w_out, ffn2_norm, ffn2_w_gate, ffn2_w_up, ffn2_w_down, loss_target, m_ffn1_norm, m_ffn1_w_gate, m_ffn1_w_up, m_ffn1_w_down, m_mix_norm, m_w_in, m_q_norm, m_k_norm, m_conv_w, m_conv_b, m_dt_bias, m_a_log, m_d_skip, m_ssd_norm, m_w_attn_branch, m_w_ssd_branch, m_w_out, m_ffn2_norm, m_ffn2_w_gate, m_ffn2_w_up, m_ffn2_w_down, v_ffn1_norm, v_ffn1_w_gate, v_ffn1_w_up, v_ffn1_w_down, v_mix_norm, v_w_in, v_q_norm, v_k_norm, v_conv_w, v_conv_b, v_dt_bias, v_a_log, v_d_skip, v_ssd_norm, v_w_attn_branch, v_w_ssd_branch, v_w_out, v_ffn2_norm, v_ffn2_w_gate, v_ffn2_w_up, v_ffn2_w_down):
    given = dict(locals())
    wts = {n: given[n] for n in ALL_WEIGHTS}
    mom = {n: given["m_" + n] for n in ALL_WEIGHTS}
    var = {n: given["v_" + n] for n in ALL_WEIGHTS}
    d = x.shape[-1]
    nh = dt_bias.shape[1]
    my = _dev_index(_mesh_pos())

    def row_form(n, col_sharded):
        a = wts[n][0].T if col_sharded else wts[n][0]
        return jnp.pad(a, ((0, _round_up(a.shape[0], ROW_ALIGN) - a.shape[0]), (0, 0)))

    shards, dests, out_names, out_shapes = [], [], [], []
    for n, col_sharded, buf, pos in BIG:
        sh = row_form(n, col_sharded).astype(MXU_DTYPE)
        if buf not in out_names:
            blocks = sum(1 for e in BIG if e[2] == buf)
            out_names.append(buf)
            out_shapes.append(((blocks * NDEV * sh.shape[0], sh.shape[1]), MXU_DTYPE))
        shards.append(sh)
        dests.append((out_names.index(buf), pos * NDEV * sh.shape[0], sh.shape[0]))
    conv_rows = _pack_rows([conv_w[0]], LANE, ROW_ALIGN)
    shards.append(conv_rows)
    dests.append((len(out_names), 0, conv_rows.shape[0]))
    out_shapes.append(((NDEV * conv_rows.shape[0], LANE), F32))
    gathered = _gather("gather_weights", shards, dests, out_shapes)
    full = dict(zip(out_names, gathered[:-1]))

    in_cols = w_in.shape[2]
    in_pad = _round_up(in_cols, ROW_ALIGN)
    wint = full["wint"].reshape(NDEV, in_pad, d)[:, :in_cols].reshape(NDEV * in_cols, d)
    dt_off = NDEV * in_cols - 2 * d - nh
    w = {k: full[k] for k in ("gu1t", "d1", "abt", "sb", "out", "gu2t", "d2")}
    w["maint"] = jnp.concatenate([wint[:dt_off], wint[dt_off + nh:]], axis=0)
    w["dtt"] = jnp.pad(wint[dt_off:dt_off + nh], ((0, LANE - nh), (0, 0)))
    p = {n: wts[n] for n in REPLICATED}
    conv_all = gathered[-1].reshape(NDEV, conv_rows.shape[0] * LANE)[:, :math.prod(conv_w.shape[1:])]
    p["conv_w"] = (conv_all.reshape((NDEV,) + conv_w.shape[1:]).transpose(1, 0, 2)
                   .reshape(conv_w.shape[1], NDEV * conv_w.shape[2]))

    loss_vec, dx, gw, gp = _local_step(x[0], loss_target[0], w, p)

    gwin = jnp.concatenate([gw["maint"][:dt_off], gw["dtt"][:nh], gw["maint"][dt_off:]], axis=0)
    gwin = jnp.pad(gwin.reshape(NDEV, in_cols, d), ((0, 0), (0, in_pad - in_cols), (0, 0)))
    gfull = dict(gw)
    gfull["wint"] = gwin.reshape(NDEV * in_pad, d)
    grad_names = [k for k in out_names]
    grads = [gfull[k] for k in grad_names]
    srcs = []
    for (n, col_sharded, buf, pos), sh in zip(BIG, shards):
        srcs.append((grad_names.index(buf), pos * NDEV * sh.shape[0], sh.shape[0], sh.shape[0]))
    small_names = REPLICATED + ("conv_w",)
    small_shapes = [gp[n].shape for n in small_names]
    small = _pack_rows([gp[n] for n in small_names], LANE, 8)
    *recv, small_all = _exchange("exchange_grads", grads, srcs, small)

    outs = [{}, {}, {}, {}]
    for (n, col_sharded, _, _), rv in zip(BIG, recv):
        res = _adamw("adamw_" + n, rv, wts[n][0], mom[n][0], var[n][0], transposed=col_sharded)
        for k in range(4):
            outs[k][n] = res[k][None]

    small_g = _unpack_rows(_sum_slabs("sum_small_grads", small_all), small_shapes)
    small_g = dict(zip(small_names, small_g))
    cs = conv_w.shape[2]
    small_g["conv_w"] = lax.dynamic_slice_in_dim(small_g["conv_w"], my * cs, cs, axis=1)
    small_shard_shapes = [wts[n].shape[-2:] for n in small_names]
    sg = _pack_rows([small_g[n] for n in small_names], LANE, 8)
    sw = _pack_rows([wts[n] for n in small_names], LANE, 8)
    sm = _pack_rows([mom[n] for n in small_names], LANE, 8)
    sv = _pack_rows([var[n] for n in small_names], LANE, 8)
    res_small = _adamw("adamw_small", sg[None], sw, sm, sv, tr=sg.shape[0])
    for k in range(4):
        for n, a in zip(small_names, _unpack_rows(res_small[k], small_shard_shapes)):
            outs[k][n] = a.reshape(wts[n].shape)

    loss = lax.psum(0.5 * jnp.sum(loss_vec) / d, ("x", "y", "c"))
    result = [loss, dx[None]]
    for k in range(4):
        result += [outs[k][n] for n in ALL_WEIGHTS]
    return tuple(result)
```

```python
import functools
import math

import numpy as np
import jax
import jax.numpy as jnp
from jax import lax
from jax.experimental import pallas as pl
from jax.experimental.pallas import tpu as pltpu

F32 = jnp.float32
BF16 = jnp.bfloat16
MXU_DTYPE = BF16
ACT_DTYPE = BF16

NDEV = 8
EPS = 1e-6
HD = 64
QB = 128
PATTERNS = ((128, 1), (512, 4), (2048, 16))
ALIBI_MAX_EXP = 8.0
SSD_P = 64
SSD_N = 128
SSD_G = 4
SSD_Q = 128
SSD_K = 4
NEG = -1e30
LANE = 128
ROW_ALIGN = 16
VMEM_LIMIT = 56 * 1024 * 1024

ADAM_LR, ADAM_B1, ADAM_B2, ADAM_EPS, ADAM_WD, ADAM_STEP = 0.001, 0.9, 0.999, 1e-8, 0.01, 10

NN = (((1,), (0,)), ((), ()))
NT = (((1,), (1,)), ((), ()))
TN = (((0,), (0,)), ((), ()))


def _dot(a, b, dims=NN):
    return lax.dot_general(a.astype(MXU_DTYPE), b.astype(MXU_DTYPE), dims,
                           preferred_element_type=F32)


def _split3(a):
    hi = a.astype(BF16)
    r = a - hi.astype(F32)
    mid = r.astype(BF16)
    lo = (r - mid.astype(F32)).astype(BF16)
    return hi, mid, lo


def _dot3(a, b, dims=NN, split=0):
    if split == 0:
        bb = b.astype(BF16)
        parts = [lax.dot_general(s, bb, dims, preferred_element_type=F32) for s in _split3(a)]
    else:
        aa = a.astype(BF16)
        parts = [lax.dot_general(aa, s, dims, preferred_element_type=F32) for s in _split3(b)]
    return parts[0] + parts[1] + parts[2]


@jax.custom_vjp
def _spread(v, e):
    return _dot3(v, e)


def _spread_fwd(v, e):
    return _dot3(v, e), e


def _spread_bwd(e, g):
    return _dot3(g, e, NT), jnp.zeros_like(e)


_spread.defvjp(_spread_fwd, _spread_bwd)


@jax.custom_vjp
def _running_sum(a, lower):
    return _dot3(lower, a, NN, split=1)


def _running_sum_fwd(a, lower):
    return _dot3(lower, a, NN, split=1), lower


def _running_sum_bwd(lower, g):
    return _dot3(lower, g, TN, split=1), jnp.zeros_like(lower)


_running_sum.defvjp(_running_sum_fwd, _running_sum_bwd)


def _tile(n, cap):
    if n <= cap:
        return n
    best = None
    for t in range(LANE, cap + 1, LANE):
        if n % t == 0:
            best = t
    assert best is not None, (n, cap)
    return best


def _params(sem):
    return pltpu.CompilerParams(dimension_semantics=sem, vmem_limit_bytes=VMEM_LIMIT)


def _round_up(n, m):
    return -(-n // m) * m


class _Order:
    tokens = []
    last = None

    @classmethod
    def take(cls):
        out, cls.tokens = cls.tokens, []
        return out

    @classmethod
    def done(cls, result):
        cls.last = result[0] if isinstance(result, (list, tuple)) else result
        return result


ANY_SPEC = pl.BlockSpec(memory_space=pl.ANY)


def _mm(name, a, b, mode, out_dtype=F32, res=None, scale=1.0,
        cap_m=1024, cap_n=1408, cap_k=1024):
    segs = list(a) if isinstance(a, (list, tuple)) else [a]
    nseg = len(segs)
    if mode == "tn":
        k = segs[0].shape[0]
        widths = [s.shape[1] for s in segs]
        m = sum(widths)
        k2, n = b.shape
        tm = _tile(math.gcd(*widths), cap_m)
        tk = _tile(k, cap_k)
        counts = [wd // tm for wd in widths]
    else:
        m = segs[0].shape[0]
        widths = [s.shape[1] for s in segs]
        k = sum(widths)
        (k2, n) = b.shape if mode == "nn" else b.shape[::-1]
        tm = _tile(m, cap_m)
        tk = _tile(math.gcd(*widths), cap_k)
        counts = [wd // tk for wd in widths]
    assert k == k2, (name, [s.shape for s in segs], b.shape, mode)
    tn = _tile(n, cap_n)
    nk = k // tk
    starts = [sum(counts[:s]) for s in range(nseg)]
    dims = {"nn": NN, "nt": NT, "tn": TN}[mode]

    def a_spec(s):
        lo, cnt = starts[s], counts[s]
        if mode == "tn":
            if nseg == 1:
                return pl.BlockSpec((tk, tm), lambda i, j, kk: (kk, i))
            return pl.BlockSpec(
                (tk, tm), lambda i, j, kk: (jnp.where((i >= lo) & (i < lo + cnt), kk, 0),
                                            jnp.clip(i - lo, 0, cnt - 1)))
        if nseg == 1:
            return pl.BlockSpec((tm, tk), lambda i, j, kk: (i, kk))
        return pl.BlockSpec((tm, tk), lambda i, j, kk: (i, jnp.clip(kk - lo, 0, cnt - 1)))

    b_spec = (pl.BlockSpec((tn, tk), lambda i, j, kk: (j, kk)) if mode == "nt"
              else pl.BlockSpec((tk, tn), lambda i, j, kk: (kk, j)))
    o_spec = pl.BlockSpec((tm, tn), lambda i, j, kk: (i, j))
    has_res = res is not None
    use_acc = nk > 1 or nseg > 1
    ties = _Order.take()
    nt_ = len(ties)

    def body(*refs):
        a_refs = refs[:nseg]
        b_ref = refs[nseg]
        r_ref = refs[nseg + 1] if has_res else None
        o_ref = refs[nseg + 1 + has_res + nt_]
        scr = refs[nseg + 2 + has_res + nt_:]

        def finish(acc):
            if scale != 1.0:
                acc = acc * scale
            if has_res:
                acc = r_ref[...].astype(F32) + acc
            o_ref[...] = acc.astype(o_ref.dtype)

        if not use_acc:
            finish(_dot(a_refs[0][...], b_ref[...], dims))
            return
        acc_ref = scr[0]
        kk = pl.program_id(2)
        sel = pl.program_id(0) if mode == "tn" else kk

        @pl.when(kk == 0)
        def _():
            acc_ref[...] = jnp.zeros_like(acc_ref)

        for s in range(nseg):
            def add(s=s):
                acc_ref[...] += _dot(a_refs[s][...], b_ref[...], dims)
            if nseg == 1:
                add()
            else:
                pl.when((sel >= starts[s]) & (sel < starts[s] + counts[s]))(add)

        @pl.when(kk == nk - 1)
        def _():
            finish(acc_ref[...])

    in_specs = ([a_spec(s) for s in range(nseg)] + [b_spec] + ([o_spec] if has_res else [])
                + [ANY_SPEC] * nt_)
    args = tuple(segs) + (b,) + ((res,) if has_res else ()) + tuple(ties)
    return _Order.done(pl.pallas_call(
        body, name=name,
        out_shape=jax.ShapeDtypeStruct((m, n), out_dtype),
        grid=(m // tm, n // tn, nk),
        in_specs=in_specs, out_specs=o_spec,
        scratch_shapes=[pltpu.VMEM((tm, tn), F32)] if use_acc else [],
        compiler_params=_params(("parallel", "parallel", "arbitrary")),
    )(*args))


def _rw(name, fn, ins, outs, accs=(), tr=256, ncb=1):
    t = next(a.shape[0] for kind, a, _, _ in ins if kind == "row")
    assert t % tr == 0
    n_in = len(ins)
    n_pieces = sum(len(w) for w, _ in outs)

    def spec(kind, arr, width, base):
        if kind == "row":
            return pl.BlockSpec((tr, width), lambda j, i: (i, base + j))
        return pl.BlockSpec((arr.shape[0], width), lambda j, i: (0, base + j))

    in_specs = [spec(*s) for s in ins]
    out_shapes, out_specs = [], []
    for widths, dt in outs:
        w = sum(widths)
        out_shapes.append(jax.ShapeDtypeStruct((t, w * ncb), dt))
        out_specs.append(pl.BlockSpec((tr, w), lambda j, i: (i, j)))
    for rows, width in accs:
        out_shapes.append(jax.ShapeDtypeStruct((rows, width * ncb), F32))
        out_specs.append(pl.BlockSpec((rows, width), lambda j, i: (0, j)))

    ties = _Order.take()
    nt_ = len(ties)
    in_specs = in_specs + [ANY_SPEC] * nt_

    def body(*refs):
        vals = [r[...] for r in refs[:n_in]]
        res = fn(*vals)
        o_refs = refs[n_in + nt_:n_in + nt_ + len(outs)]
        a_refs = refs[n_in + nt_ + len(outs):]
        p = 0
        for (widths, _), o_ref in zip(outs, o_refs):
            off = 0
            for w in widths:
                if len(widths) == 1:
                    o_ref[...] = res[p].astype(o_ref.dtype)
                else:
                    o_ref[:, off:off + w] = res[p].astype(o_ref.dtype)
                off += w
                p += 1
        i = pl.program_id(1)
        for a_ref, v in zip(a_refs, res[n_pieces:]):
            @pl.when(i == 0)
            def _(a_ref=a_ref, v=v):
                a_ref[...] = v

            @pl.when(i > 0)
            def _(a_ref=a_ref, v=v):
                a_ref[...] += v

    return _Order.done(pl.pallas_call(
        body, name=name, out_shape=out_shapes,
        grid=(ncb, t // tr), in_specs=in_specs, out_specs=out_specs,
        compiler_params=_params(("parallel", "arbitrary")),
    )(*[a for _, a, _, _ in ins], *ties))


def _rms(x, g):
    x = x.astype(F32)
    return x * lax.rsqrt(jnp.mean(x * x, axis=-1, keepdims=True) + EPS) * g


def _silu(x):
    return x * jax.nn.sigmoid(x)


def _colsum(v):
    return jnp.sum(v, axis=0, keepdims=True)


def _attn_pair(q, kc, kp, vc, vp, qg, kg, coef0, coef1, first):
    w = 2 * HD
    ri = lax.broadcasted_iota(jnp.int32, (w, w), 0)
    ci = lax.broadcasted_iota(jnp.int32, (w, w), 1)
    same_head = ((ri < HD) == (ci < HD)).astype(F32)
    lane = lax.broadcasted_iota(jnp.int32, (1, w), 1)

    def norm(x, g):
        ms = _spread(x * x, same_head) * (1.0 / HD)
        return x * lax.rsqrt(ms + EPS) * g

    qn, kcn, kpn = norm(q, qg), norm(kc, kg), norm(kp, kg)
    scale = 1.0 / math.sqrt(HD)
    a_idx = lax.broadcasted_iota(jnp.int32, (QB, QB), 0)
    c_idx = lax.broadcasted_iota(jnp.int32, (QB, QB), 1)
    rel_c = a_idx - c_idx
    dist_c = rel_c.astype(F32)
    dist_p = dist_c + float(QB)
    keep_c = rel_c >= 0
    keep_p = jnp.logical_and(rel_c <= 0, jnp.logical_not(first))
    out = jnp.zeros((QB, w), F32)
    lb = jnp.zeros((QB, w), F32)
    for hh, coef in enumerate((coef0, coef1)):
        mask = ((lane < HD) if hh == 0 else (lane >= HD)).astype(F32)
        qm = qn * mask
        lc = jnp.where(keep_c, _dot(qm, kcn, NT) * scale - coef * dist_c, NEG)
        lp = jnp.where(keep_p, _dot(qm, kpn, NT) * scale - coef * dist_p, NEG)
        m = lax.stop_gradient(jnp.maximum(jnp.max(lc, axis=-1, keepdims=True),
                                          jnp.max(lp, axis=-1, keepdims=True)))
        pc = jnp.exp(lc - m)
        pp = jnp.exp(lp - m)
        l = jnp.sum(pc, axis=-1, keepdims=True) + jnp.sum(pp, axis=-1, keepdims=True)
        inv = 1.0 / l
        out = out + (_dot(pc * inv, vc) + _dot(pp * inv, vp)) * mask
        lb = lb + (m + jnp.log(l)) * mask
    return out, lb


def _stream_rows(n, r, d):
    start = pl.multiple_of(n * (QB * d), QB * d)
    if d == 1:
        return pl.ds(start, QB)
    return pl.ds(start + r, QB, stride=d)


def _attn_specs(t, bases):
    w = 2 * HD
    ins = [pl.BlockSpec((t, w), functools.partial(lambda p, c, b: (0, b + p), b=b)) for b in bases]
    gain = pl.BlockSpec((1, w), lambda p, c: (0, 0))
    blk = pl.BlockSpec((t, w), lambda p, c: (0, p))
    return ins, gain, blk


def _attn_fwd(name, proj, bases, qg, kg, coefs, d):
    t = proj.shape[0]
    npairs = coefs.shape[0] // 2
    nb = t // (d * QB)
    w = 2 * HD
    ins, gain, blk = _attn_specs(t, bases)

    def body(coef_ref, q_ref, k_ref, v_ref, qg_ref, kg_ref, o_ref, l_ref, qf, kf, vf):
        p = pl.program_id(0)
        c0, c1 = coef_ref[2 * p], coef_ref[2 * p + 1]
        qf[...] = q_ref[...].astype(F32)
        kf[...] = k_ref[...].astype(F32)
        vf[...] = v_ref[...].astype(F32)

        def stream(r, carry):
            def step(n, c2):
                cur = _stream_rows(n, r, d)
                prv = _stream_rows(jnp.maximum(n - 1, 0), r, d)
                o, lb = _attn_pair(qf[cur, :], kf[cur, :], kf[prv, :], vf[cur, :], vf[prv, :],
                                   qg_ref[...], kg_ref[...], c0, c1, n == 0)
                o_ref[cur, :] = o
                l_ref[cur, :] = lb
                return c2
            return lax.fori_loop(0, nb, step, carry)

        lax.fori_loop(0, d, stream, 0)

    return pl.pallas_call(
        body, name=name,
        out_shape=[jax.ShapeDtypeStruct((t, npairs * w), F32)] * 2,
        grid_spec=pltpu.PrefetchScalarGridSpec(
            num_scalar_prefetch=1, grid=(npairs,),
            in_specs=ins + [gain, gain], out_specs=[blk, blk],
            scratch_shapes=[pltpu.VMEM((t, w), F32)] * 3),
        compiler_params=_params(("arbitrary",)),
    )(coefs, proj, proj, proj, qg, kg)


def _attn_bwd(name, proj, bases, qg, kg, coefs, d, do, dl):
    t = proj.shape[0]
    npairs = coefs.shape[0] // 2
    nb = t // (d * QB)
    w = 2 * HD
    ins, gain, blk = _attn_specs(t, bases)

    def body(coef_ref, q_ref, k_ref, v_ref, qg_ref, kg_ref, do_ref, dl_ref,
             dq_ref, dk_ref, dv_ref, dqg_ref, dkg_ref, qf, kf, vf, dqf, dkf, dvf):
        p = pl.program_id(0)
        c0, c1 = coef_ref[2 * p], coef_ref[2 * p + 1]
        qf[...] = q_ref[...].astype(F32)
        kf[...] = k_ref[...].astype(F32)
        vf[...] = v_ref[...].astype(F32)
        dkf[...] = jnp.zeros_like(dkf)
        dvf[...] = jnp.zeros_like(dvf)

        def stream(r, carry):
            def step(n, c2):
                dqg_acc, dkg_acc = c2
                cur = _stream_rows(n, r, d)
                prv = _stream_rows(jnp.maximum(n - 1, 0), r, d)
                first = n == 0
                f = lambda a, b, c, e, g, g1, g2: _attn_pair(a, b, c, e, g, g1, g2, c0, c1, first)
                _, vjp = jax.vjp(f, qf[cur, :], kf[cur, :], kf[prv, :], vf[cur, :], vf[prv, :],
                                 qg_ref[...], kg_ref[...])
                dq, dkc, dkp, dvc, dvp, dqg, dkg = vjp((do_ref[cur, :], dl_ref[cur, :]))
                dqf[cur, :] = dq
                dkf[cur, :] += dkc
                dkf[prv, :] += dkp
                dvf[cur, :] += dvc
                dvf[prv, :] += dvp
                return dqg_acc + dqg, dkg_acc + dkg
            return lax.fori_loop(0, nb, step, carry)

        zero = jnp.zeros((1, w), F32)
        dqg, dkg = lax.fori_loop(0, d, stream, (zero, zero))
        dq_ref[...] = dqf[...].astype(dq_ref.dtype)
        dk_ref[...] = dkf[...].astype(dk_ref.dtype)
        dv_ref[...] = dvf[...].astype(dv_ref.dtype)

        @pl.when(p == 0)
        def _():
            dqg_ref[...] = dqg
            dkg_ref[...] = dkg

        @pl.when(p > 0)
        def _():
            dqg_ref[...] += dqg
            dkg_ref[...] += dkg

    big = jax.ShapeDtypeStruct((t, npairs * w), ACT_DTYPE)
    small = jax.ShapeDtypeStruct((1, w), F32)
    return pl.pallas_call(
        body, name=name,
        out_shape=[big, big, big, small, small],
        grid_spec=pltpu.PrefetchScalarGridSpec(
            num_scalar_prefetch=1, grid=(npairs,),
            in_specs=ins + [gain, gain, blk, blk],
            out_specs=[blk, blk, blk, gain, gain],
            scratch_shapes=[pltpu.VMEM((t, w), F32)] * 6),
        compiler_params=_params(("arbitrary",)),
    )(coefs, proj, proj, proj, qg, kg, do, dl)


def _shift_down(u, s):
    if s == 0:
        return u
    rows = lax.broadcasted_iota(jnp.int32, u.shape, 0)
    return jnp.where(rows >= s, pltpu.roll(u, s, 0), 0.0)


def _shift_up(u, s):
    if s == 0:
        return u
    t = u.shape[0]
    rows = lax.broadcasted_iota(jnp.int32, u.shape, 0)
    return jnp.where(rows < t - s, pltpu.roll(u, t - s, 0), 0.0)


def _conv_pre(u, w, b):
    y = b
    for kk in range(SSD_K):
        y = y + w[kk:kk + 1, :] * _shift_down(u, SSD_K - 1 - kk)
    return y


def _conv_fwd(name, src, base, w, b, cw=128):
    t = src.shape[0]
    c = w.shape[1]

    def body(u_ref, w_ref, b_ref, o_ref):
        y = _conv_pre(u_ref[...].astype(F32), w_ref[...], b_ref[...])
        o_ref[...] = _silu(y).astype(o_ref.dtype)

    return pl.pallas_call(
        body, name=name, out_shape=jax.ShapeDtypeStruct((t, c), ACT_DTYPE),
        grid=(c // cw,),
        in_specs=[pl.BlockSpec((t, cw), lambda j: (0, base + j)),
                  pl.BlockSpec((SSD_K, cw), lambda j: (0, j)),
                  pl.BlockSpec((1, cw), lambda j: (0, j))],
        out_specs=pl.BlockSpec((t, cw), lambda j: (0, j)),
        compiler_params=_params(("parallel",)),
    )(src, w, b)


def _conv_bwd(name, src, base, w, b, dout, cw=128):
    t = src.shape[0]
    c = w.shape[1]

    def body(u_ref, w_ref, b_ref, d_ref, du_ref, dw_ref, db_ref):
        u = u_ref[...].astype(F32)
        wv = w_ref[...]
        y = _conv_pre(u, wv, b_ref[...])
        sg = jax.nn.sigmoid(y)
        dy = d_ref[...].astype(F32) * (sg * (1.0 + y * (1.0 - sg)))
        du = jnp.zeros_like(u)
        for kk in range(SSD_K):
            s = SSD_K - 1 - kk
            du = du + wv[kk:kk + 1, :] * _shift_up(dy, s)
            dw_ref[kk:kk + 1, :] = _colsum(dy * _shift_down(u, s))
        du_ref[...] = du.astype(du_ref.dtype)
        db_ref[...] = _colsum(dy)

    return pl.pallas_call(
        body, name=name,
        out_shape=[jax.ShapeDtypeStruct((t, c), ACT_DTYPE),
                   jax.ShapeDtypeStruct((SSD_K, c), F32),
                   jax.ShapeDtypeStruct((1, c), F32)],
        grid=(c // cw,),
        in_specs=[pl.BlockSpec((t, cw), lambda j: (0, base + j)),
                  pl.BlockSpec((SSD_K, cw), lambda j: (0, j)),
                  pl.BlockSpec((1, cw), lambda j: (0, j)),
                  pl.BlockSpec((t, cw), lambda j: (0, j))],
        out_specs=[pl.BlockSpec((t, cw), lambda j: (0, j)),
                   pl.BlockSpec((SSD_K, cw), lambda j: (0, j)),
                   pl.BlockSpec((1, cw), lambda j: (0, j))],
        compiler_params=_params(("parallel",)),
    )(src, w, b, dout)


def _softplus(x):
    return jnp.maximum(x, 0.0) + jnp.log(1.0 + jnp.exp(-jnp.abs(x)))


def _ssd_chunk(xbc, dtraw, bias, alog, states):
    wd = states[0].shape[1]
    nj = wd // SSD_P
    inner = SSD_G * wd
    dt = _softplus(dtraw + bias)
    a = dt * (-jnp.exp(alog))
    li = lax.broadcasted_iota(jnp.int32, (SSD_Q, SSD_Q), 0)
    si = lax.broadcasted_iota(jnp.int32, (SSD_Q, SSD_Q), 1)
    causal = li >= si
    acs = _running_sum(a, causal.astype(F32))
    acs_t = acs.T
    a_last = acs[SSD_Q - 1:SSD_Q, :]
    grow = jnp.exp(acs)
    shrink = jnp.exp(a_last - acs)
    hrow = lax.broadcasted_iota(jnp.int32, (LANE, wd), 0)
    wcol = lax.broadcasted_iota(jnp.int32, (LANE, wd), 1)
    lane = lax.broadcasted_iota(jnp.int32, (1, LANE), 1)
    ys, snext = [], []
    for g in range(SSD_G):
        lo = (hrow - g * nj) * SSD_P
        head_lanes = jnp.logical_and(wcol >= lo, wcol < lo + SSD_P).astype(F32)
        xs = xbc[:, g * wd:(g + 1) * wd]
        bm = xbc[:, inner + g * SSD_N:inner + (g + 1) * SSD_N]
        cm = xbc[:, inner + (SSD_G + g) * SSD_N:inner + (SSD_G + g + 1) * SSD_N]
        xdt = xs * _spread(dt, head_lanes)
        grow_x = _spread(grow, head_lanes)
        y_off = _dot(cm, states[g]) * grow_x
        s_new = (states[g] * grow_x[SSD_Q - 1:SSD_Q, :]
                 + _dot(bm, xdt * _spread(shrink, head_lanes), TN))
        cb = _dot(cm, bm, NT)
        pieces = []
        for i in range(wd // LANE):
            xp = xdt[:, i * LANE:(i + 1) * LANE]
            acc = jnp.zeros((SSD_Q, LANE), F32)
            for hh in range(LANE // SSD_P):
                h = g * nj + i * (LANE // SSD_P) + hh
                decay = jnp.exp(jnp.where(causal, acs[:, h:h + 1] - acs_t[h:h + 1, :], NEG))
                keep = jnp.logical_and(lane >= hh * SSD_P, lane < (hh + 1) * SSD_P).astype(F32)
                acc = acc + _dot(cb * decay, xp * keep)
            pieces.append(acc)
        y_diag = pieces[0] if len(pieces) == 1 else jnp.concatenate(pieces, axis=1)
        ys.append(y_diag + y_off)
        snext.append(s_new)
    return ys, snext


def _ssd_specs(cdim, wd, rev, nc):
    ch = (lambda c: nc - 1 - c) if rev else (lambda c: c)
    full = lambda width: pl.BlockSpec((SSD_Q, width), lambda c: (ch(c), 0))
    vec = pl.BlockSpec((1, LANE), lambda c: (0, 0))
    st = pl.BlockSpec((1, SSD_G, SSD_N, wd), lambda c: (ch(c), 0, 0, 0))
    return full, vec, st


def _ssd_fwd(name, xbc, dtraw, bias, alog, inner):
    t, cdim = xbc.shape
    wd = inner // SSD_G
    nc = t // SSD_Q
    full, vec, st = _ssd_specs(cdim, wd, False, nc)

    def body(x_ref, r_ref, b_ref, a_ref, y_ref, st_ref, s_scr):
        @pl.when(pl.program_id(0) == 0)
        def _():
            s_scr[...] = jnp.zeros_like(s_scr)

        sprev = [s_scr[g] for g in range(SSD_G)]
        ys, snext = _ssd_chunk(x_ref[...].astype(F32), r_ref[...], b_ref[...], a_ref[...], sprev)
        for g in range(SSD_G):
            st_ref[0, g] = sprev[g]
            y_ref[:, g * wd:(g + 1) * wd] = ys[g]
            s_scr[g] = snext[g]

    return pl.pallas_call(
        body, name=name,
        out_shape=[jax.ShapeDtypeStruct((t, inner), F32),
                   jax.ShapeDtypeStruct((nc, SSD_G, SSD_N, wd), F32)],
        grid=(nc,),
        in_specs=[full(cdim), full(LANE), vec, vec],
        out_specs=[full(inner), st],
        scratch_shapes=[pltpu.VMEM((SSD_G, SSD_N, wd), F32)],
        compiler_params=_params(("arbitrary",)),
    )(xbc, dtraw, bias, alog)


def _ssd_bwd(name, xbc, dtraw, bias, alog, states, dy, dxs_extra):
    t, cdim = xbc.shape
    inner = dy.shape[1]
    wd = inner // SSD_G
    nc = t // SSD_Q
    full, vec, st = _ssd_specs(cdim, wd, True, nc)

    def body(x_ref, r_ref, b_ref, a_ref, st_ref, dy_ref, dx0_ref,
             dx_ref, dr_ref, db_ref, da_ref, ds_scr):
        first = pl.program_id(0) == 0

        @pl.when(first)
        def _():
            ds_scr[...] = jnp.zeros_like(ds_scr)

        sprev = [st_ref[0, g] for g in range(SSD_G)]
        _, vjp = jax.vjp(_ssd_chunk, x_ref[...].astype(F32), r_ref[...], b_ref[...], a_ref[...],
                         sprev)
        dyv = dy_ref[...]
        dys = [dyv[:, g * wd:(g + 1) * wd] for g in range(SSD_G)]
        dsn = [ds_scr[g] for g in range(SSD_G)]
        dx, dr, db, da, dsp = vjp((dys, dsn))
        dx_ref[:, :inner] = dx[:, :inner] + dx0_ref[...].astype(F32)
        dx_ref[:, inner:] = dx[:, inner:]
        dr_ref[...] = dr
        for g in range(SSD_G):
            ds_scr[g] = dsp[g]

        @pl.when(first)
        def _():
            db_ref[...] = db
            da_ref[...] = da

        @pl.when(jnp.logical_not(first))
        def _():
            db_ref[...] += db
            da_ref[...] += da

    return pl.pallas_call(
        body, name=name,
        out_shape=[jax.ShapeDtypeStruct((t, cdim), F32),
                   jax.ShapeDtypeStruct((t, LANE), F32),
                   jax.ShapeDtypeStruct((1, LANE), F32),
                   jax.ShapeDtypeStruct((1, LANE), F32)],
        grid=(nc,),
        in_specs=[full(cdim), full(LANE), vec, vec, st, full(inner), full(inner)],
        out_specs=[full(cdim), full(LANE), vec, vec],
        scratch_shapes=[pltpu.VMEM((SSD_G, SSD_N, wd), F32)],
        compiler_params=_params(("arbitrary",)),
    )(xbc, dtraw, bias, alog, states, dy, dxs_extra)


def _act(g, u):
    return _silu(g.astype(F32)) * u.astype(F32)


def _mix(o0, o1, o2, l0, l1, l2):
    m = lax.stop_gradient(jnp.maximum(jnp.maximum(l0, l1), l2))
    e0, e1, e2 = jnp.exp(l0 - m), jnp.exp(l1 - m), jnp.exp(l2 - m)
    return (e0 * o0 + e1 * o1 + e2 * o2) / (e0 + e1 + e2)


def _gate(y, xs, z, dexp, gain):
    v = (y + xs.astype(F32) * dexp) * _silu(z.astype(F32))
    return _rms(v, gain)


def _merge(ga, gs, ap, sp):
    return jax.nn.sigmoid(ga.astype(F32)) * ap + jax.nn.sigmoid(gs.astype(F32)) * sp


def _alibi_coefs(hp):
    n = hp * len(PATTERNS)
    slopes = np.exp2(-ALIBI_MAX_EXP * np.arange(1, n + 1, dtype=np.float32) / n).astype(np.float32)
    return [jnp.asarray(slopes[g * hp:(g + 1) * hp] * np.float32(d))
            for g, (_, d) in enumerate(PATTERNS)]


def _local_step(x, tgt, w, p, gw_=None, aw=None):
    t, d = x.shape
    dff = w["d1"].shape[0]
    aw = w["abt"].shape[1] if aw is None else aw
    hp = aw // HD
    qkv = len(PATTERNS) * aw
    inner = p["ssd_norm"].shape[1]
    nh = p["dt_bias"].shape[1]
    gw_ = {} if gw_ is None else gw_
    gw = inner // SSD_G
    cdim = inner + 2 * SSD_G * SSD_N
    z_off, xbc_off = 3 * qkv, 3 * qkv + inner
    ga_off = xbc_off + cdim
    gs_off = ga_off + d
    hw = d // 2
    assert z_off % gw == 0 and xbc_off % LANE == 0 and ga_off % hw == 0 and gs_off % hw == 0
    assert (nh // SSD_G) * SSD_P == gw and hp % 2 == 0 and aw % LANE == 0 and nh <= LANE
    gdt = MXU_DTYPE

    row = lambda a, width, base=0: ("row", a, width, base)
    const = lambda a, width, base=0: ("const", a, width, base)

    def rms_fwd(name, xin, g):
        return _rw(name, lambda xv, gv: (_rms(xv, gv),), [row(xin, d), const(g, d)],
                   [((d,), ACT_DTYPE)])[0]

    def rms_bwd(name, xin, g, dh, dres):
        def fn(xv, gv, dhv, drv):
            _, vjp = jax.vjp(_rms, xv, gv)
            dx, dg = vjp(dhv.astype(F32))
            return drv + dx, dg
        return _rw(name, fn, [row(xin, d), const(g, d), row(dh, d), row(dres, d)],
                   [((d,), F32)], accs=[(1, d)])

    def ffn_fwd(tag, xin, g, wgut, wd):
        h = rms_fwd(tag + "_norm", xin, g)
        gu = _mm(tag + "_up", h, wgut, "nt", ACT_DTYPE)
        a = _rw(tag + "_act", lambda gv, uv: (_act(gv, uv),),
                [row(gu, dff, 0), row(gu, dff, 1)], [((dff,), ACT_DTYPE)])[0]
        xo = _mm(tag + "_down", a, wd, "nn", F32, res=xin, scale=0.5)
        return xo, (h, gu, a)

    def ffn_bwd(tag, xin, g, wgut, wd, saved, dxo, key_gu, key_d):
        h, gu, a = saved
        da = _mm(tag + "_da", dxo, wd, "nt", ACT_DTYPE, scale=0.5)
        gw_[key_d] = _mm(tag + "_dwd", a, dxo, "tn", gdt, scale=0.5)

        def fn(gv, uv, dav):
            _, vjp = jax.vjp(_act, gv, uv)
            return vjp(dav.astype(F32))
        dgu = _rw(tag + "_dact", fn, [row(gu, dff, 0), row(gu, dff, 1), row(da, dff)],
                  [((dff, dff), ACT_DTYPE)])[0]
        gw_[key_gu] = _mm(tag + "_dwgu", dgu, h, "tn", gdt)
        dh = _mm(tag + "_dh", dgu, wgut, "nn", F32)
        return rms_bwd(tag + "_dnorm", xin, g, dh, dxo)

    x1, ffn1_saved = ffn_fwd("ffn1", x, p["ffn1_norm"], w["gu1t"], w["d1"])
    h2 = rms_fwd("mix_norm", x1, p["mix_norm"])
    proj = _mm("in_proj", h2, w["maint"], "nt", ACT_DTYPE)
    dtraw = _mm("dt_proj", h2, w["dtt"], "nt", F32)

    coefs = _alibi_coefs(hp)
    qg2 = jnp.concatenate([p["q_norm"], p["q_norm"]], axis=1)
    kg2 = jnp.concatenate([p["k_norm"], p["k_norm"]], axis=1)
    pw = 2 * HD
    attn_bases = [[(off + gi * aw) // pw for off in (0, qkv, 2 * qkv)]
                  for gi in range(len(PATTERNS))]
    attn_o, attn_l = [], []
    for gi, (_, dil) in enumerate(PATTERNS):
        o, l = _attn_fwd(f"attn_fwd{gi}", proj, attn_bases[gi], qg2, kg2, coefs[gi], dil)
        attn_o.append(o)
        attn_l.append(l)
    ao = _rw("attn_mix", lambda *v: (_mix(*v),), [row(a, aw) for a in attn_o + attn_l],
             [((aw,), ACT_DTYPE)])[0]

    xbc = _conv_fwd("conv_fwd", proj, xbc_off // LANE, p["conv_w"], p["conv_b"])
    pad = lambda v: jnp.pad(v, ((0, 0), (0, LANE - nh)))
    bias_p, alog_p = pad(p["dt_bias"]), pad(p["a_log"])
    yssd, states = _ssd_fwd("ssd_fwd", xbc, dtraw, bias_p, alog_p, inner)
    dexp = jnp.repeat(p["d_skip"], SSD_P, axis=1)
    gate_ins = [row(yssd, gw), row(xbc, gw), row(proj, gw, z_off // gw),
                const(dexp, gw), const(p["ssd_norm"], gw)]
    yn = _rw("ssd_gate", lambda *v: (_gate(*v),), gate_ins, [((gw,), ACT_DTYPE)], ncb=SSD_G)[0]

    ap = _mm("attn_out", ao, w["abt"], "nt", F32)
    sp = _mm("ssd_out", yn, w["sb"], "nn", F32)
    merge_ins = [row(proj, hw, ga_off // hw), row(proj, hw, gs_off // hw), row(ap, hw), row(sp, hw)]
    mg = _rw("merge", lambda *v: (_merge(*v),), merge_ins, [((hw,), ACT_DTYPE)], ncb=2)[0]
    x2 = _mm("mix_out", mg, w["out"], "nn", F32, res=x1)
    x3, ffn2_saved = ffn_fwd("ffn2", x2, p["ffn2_norm"], w["gu2t"], w["d2"])

    def loss_fn(yv, tv):
        e = yv - tv
        return e * (1.0 / d), _colsum(e * e)
    dy, loss_vec = _rw("loss", loss_fn, [row(x3, d), row(tgt, d)], [((d,), F32)], accs=[(1, d)])

    gp = {}
    dx2, gp["ffn2_norm"] = ffn_bwd(
        "ffn2", x2, p["ffn2_norm"], w["gu2t"], w["d2"], ffn2_saved, dy, "gu2t", "d2")
    dmg = _mm("d_merge", dx2, w["out"], "nt", ACT_DTYPE)
    gw_["out"] = _mm("dw_out", mg, dx2, "tn", gdt)

    def merge_bwd(gav, gsv, apv, spv, dv):
        _, vjp = jax.vjp(_merge, gav, gsv, apv, spv)
        return vjp(dv.astype(F32))
    dga, dgs, dap, dsp = _rw("d_merge_gate", merge_bwd, merge_ins + [row(dmg, hw)],
                             [((hw,), ACT_DTYPE)] * 4, ncb=2)
    gw_["abt"] = _mm("dw_ab", dap, ao, "tn", gdt)
    dao = _mm("d_attn_o", dap, w["abt"], "nn", F32)
    gw_["sb"] = _mm("dw_sb", yn, dsp, "tn", gdt)
    dyn = _mm("d_ssd_y", dsp, w["sb"], "nt", F32)

    def gate_bwd(yv, xv, zv, dev, gv, dv):
        _, vjp = jax.vjp(_gate, yv, xv, zv, dev, gv)
        return vjp(dv)
    dyssd, dxs_gate, dz, ddexp, gp["ssd_norm"] = _rw(
        "d_ssd_gate", gate_bwd, gate_ins + [row(dyn, gw)],
        [((gw,), F32), ((gw,), F32), ((gw,), ACT_DTYPE)], accs=[(1, gw), (1, gw)], ncb=SSD_G)
    gp["d_skip"] = ddexp.reshape(nh, SSD_P).sum(axis=1).reshape(1, nh)

    dxbc, ddtraw, dbias, dalog = _ssd_bwd("ssd_bwd", xbc, dtraw, bias_p, alog_p, states,
                                          dyssd, dxs_gate)
    gp["dt_bias"], gp["a_log"] = dbias[:, :nh], dalog[:, :nh]
    du, gp["conv_w"], gp["conv_b"] = _conv_bwd("conv_bwd", proj, xbc_off // LANE,
                                               p["conv_w"], p["conv_b"], dxbc)

    def mix_bwd(*v):
        _, vjp = jax.vjp(_mix, *v[:6])
        return vjp(v[6])
    dmix = _rw("d_attn_mix", mix_bwd, [row(a, aw) for a in attn_o + attn_l] + [row(dao, aw)],
               [((aw,), F32)] * 6)
    dq, dk, dv = [], [], []
    dqg = dkg = None
    for gi, (_, dil) in enumerate(PATTERNS):
        r = _attn_bwd(f"attn_bwd{gi}", proj, attn_bases[gi], qg2, kg2, coefs[gi], dil,
                      dmix[gi], dmix[3 + gi])
        dq.append(r[0])
        dk.append(r[1])
        dv.append(r[2])
        dqg = r[3] if dqg is None else dqg + r[3]
        dkg = r[4] if dkg is None else dkg + r[4]
    gp["q_norm"] = dqg[:, :HD] + dqg[:, HD:]
    gp["k_norm"] = dkg[:, :HD] + dkg[:, HD:]

    segs = dq + dk + dv + [dz, du, dga, dgs]
    gw_["maint"] = _mm("dw_in", segs, h2, "tn", gdt)
    gw_["dtt"] = _mm("dw_dt", ddtraw, h2, "tn", gdt)
    dh2 = _mm("d_h2_main", segs, w["maint"], "nn", F32)
    dh2 = _mm("d_h2_dt", ddtraw, w["dtt"], "nn", F32, res=dh2)
    dx1, gp["mix_norm"] = rms_bwd("d_mix_norm", x1, p["mix_norm"], dh2, dx2)
    dx0, gp["ffn1_norm"] = ffn_bwd(
        "ffn1", x, p["ffn1_norm"], w["gu1t"], w["d1"], ffn1_saved, dx1, "gu1t", "d1")
    return loss_vec, dx0, gw_, gp


MESH = pl.DeviceIdType.MESH
HBM_SPEC = pl.BlockSpec(memory_space=pltpu.HBM)


def _mesh_pos():
    return lax.axis_index("x"), lax.axis_index("y"), lax.axis_index("c")


def _flip(pos, k):
    x, y, c = pos
    return (1 - x if k & 4 else x, 1 - y if k & 2 else y, 1 - c if k & 1 else c)


def _dev_index(pos):
    return 4 * pos[0] + 2 * pos[1] + pos[2]


def _rows_of(ref, base, stride, rows, pos):
    start = pl.multiple_of(base + stride * _dev_index(pos), ROW_ALIGN)
    return ref.at[pl.ds(start, rows)]


def _gather(name, shards, dests, out_shapes):
    n = len(shards)
    n_out = len(out_shapes)

    def body(*refs):
        x_refs = refs[:n]
        o_refs = refs[n:n + n_out]
        send_sems, recv_sems, local_sems = refs[n + n_out:]
        me = _mesh_pos()
        sibling = _flip(me, 1)
        chips = [_flip(me, 4), _flip(me, 2), _flip(me, 6)]

        def slot(i, block):
            k_out, base, stride = dests[i]
            return _rows_of(o_refs[k_out], base, stride, shards[i].shape[0], block)

        def copy(i, k, block, to, src=None):
            dst = slot(i, block)
            return pltpu.make_async_remote_copy(
                src_ref=dst if src is None else src, dst_ref=dst,
                send_sem=send_sems.at[7 * i + k], recv_sem=recv_sems.at[7 * i + k],
                device_id=to, device_id_type=MESH)

        mine = [pltpu.make_async_copy(x_refs[i], slot(i, me), local_sems.at[i]) for i in range(n)]
        for cp in mine:
            cp.start()
        first = []
        for i in range(n):
            first.append(copy(i, 0, me, sibling, src=x_refs[i]))
            first += [copy(i, 1 + j, me, chip, src=x_refs[i]) for j, chip in enumerate(chips)]
        for cp in first:
            cp.start()
        passed = []
        for j, chip in enumerate(chips):
            for i in range(n):
                copy(i, 1 + j, chip, me).wait_recv()
                fwd = copy(i, 4 + j, chip, sibling)
                fwd.start()
                passed.append(fwd)
        for i in range(n):
            copy(i, 0, sibling, me).wait_recv()
            for j, chip in enumerate(chips):
                copy(i, 4 + j, _flip(chip, 1), me).wait_recv()
        for cp in first + passed:
            cp.wait_send()
        for cp in mine:
            cp.wait()

    return pl.pallas_call(
        body, name=name,
        out_shape=[jax.ShapeDtypeStruct(s, dt) for s, dt in out_shapes],
        in_specs=[HBM_SPEC] * n, out_specs=[HBM_SPEC] * n_out,
        scratch_shapes=[pltpu.SemaphoreType.DMA((7 * n,)), pltpu.SemaphoreType.DMA((7 * n,)),
                        pltpu.SemaphoreType.DMA((n,))],
    )(*shards)


def _exchange(name, grads, srcs, small):
    n = len(srcs)
    ng = len(grads)

    def body(*refs):
        g_refs = refs[:ng]
        m_ref = refs[ng]
        r_refs = refs[ng + 1:ng + 1 + n]
        s_ref = refs[ng + 1 + n]
        send_sems, recv_sems, local_sems = refs[ng + 2 + n:]
        me = _mesh_pos()
        my = _dev_index(me)

        def slab(i, pos):
            gi, base, stride, rows = srcs[i]
            return _rows_of(g_refs[gi], base, stride, rows, pos)

        own = [pltpu.make_async_copy(slab(i, me), r_refs[i].at[my], local_sems.at[i])
               for i in range(n)]
        own.append(pltpu.make_async_copy(m_ref, s_ref.at[my], local_sems.at[n]))
        for cp in own:
            cp.start()

        def copies(k, src_pos, slot_pos):
            peer = _flip(me, k)
            si = _dev_index(slot_pos)
            out = [pltpu.make_async_remote_copy(
                src_ref=slab(i, src_pos), dst_ref=r_refs[i].at[si],
                send_sem=send_sems.at[7 * i + k - 1], recv_sem=recv_sems.at[7 * i + k - 1],
                device_id=peer, device_id_type=MESH) for i in range(n)]
            out.append(pltpu.make_async_remote_copy(
                src_ref=m_ref, dst_ref=s_ref.at[si],
                send_sem=send_sems.at[7 * n + k - 1], recv_sem=recv_sems.at[7 * n + k - 1],
                device_id=peer, device_id_type=MESH))
            return out

        sent = [cp for k in range(1, NDEV) for cp in copies(k, _flip(me, k), me)]
        for cp in sent:
            cp.start()
        for k in range(1, NDEV):
            for cp in copies(k, me, _flip(me, k)):
                cp.wait_recv()
        for cp in sent:
            cp.wait_send()
        for cp in own:
            cp.wait()

    out_shape = [jax.ShapeDtypeStruct((NDEV, rows, grads[gi].shape[1]), grads[gi].dtype)
                 for gi, _, _, rows in srcs]
    out_shape.append(jax.ShapeDtypeStruct((NDEV,) + small.shape, small.dtype))
    return pl.pallas_call(
        body, name=name, out_shape=out_shape,
        in_specs=[HBM_SPEC] * (ng + 1), out_specs=[HBM_SPEC] * (n + 1),
        scratch_shapes=[pltpu.SemaphoreType.DMA((7 * (n + 1),)),
                        pltpu.SemaphoreType.DMA((7 * (n + 1),)),
                        pltpu.SemaphoreType.DMA((n + 1,))],
    )(*grads, small)


SEM_SPEC = pl.BlockSpec(memory_space=pltpu.SEMAPHORE)
SIDE_EFFECT = pltpu.SideEffectType.DATAFLOW_SIDE_EFFECTING


def _split_refs(plan, srcs, lands, i, src_for, land_from):
    si, sbase, sstride, li, lbase, lstride, rows = plan[i]
    return (_rows_of(srcs[si], sbase, sstride, rows, src_for),
            _rows_of(lands[li], lbase, lstride, rows, land_from))


def _split_start(name, srcs, lands, plan, after=()):
    ns, nl, n = len(srcs), len(lands), len(plan)

    def body(*refs):
        s_refs = refs[:ns]
        l_refs = refs[ns:ns + nl]
        send_sems, recv_sems = refs[ns + nl + len(after):ns + nl + len(after) + 2]
        token = refs[ns + nl + len(after) + 2 + ns + nl]
        local_sems = refs[-1]
        me = _mesh_pos()
        own = []
        for i in range(n):
            src, dst = _split_refs(plan, s_refs, l_refs, i, me, me)
            own.append(pltpu.make_async_copy(src, dst, local_sems.at[i]))
            own[-1].start()
        for k in range(1, NDEV):
            peer = _flip(me, k)
            for i in range(n):
                src, dst = _split_refs(plan, s_refs, l_refs, i, peer, me)
                pltpu.make_async_remote_copy(
                    src_ref=src, dst_ref=dst,
                    send_sem=send_sems.at[7 * i + k - 1], recv_sem=recv_sems.at[7 * i + k - 1],
                    device_id=peer, device_id_type=MESH).start()
        for cp in own:
            cp.wait()
        token[...] = jnp.zeros_like(token)

    hbm = lambda a: pltpu.HBM(a.shape, a.dtype)
    out_shape = ((pltpu.SemaphoreType.DMA((7 * n,)), pltpu.SemaphoreType.DMA((7 * n,)))
                 + tuple(hbm(a) for a in srcs) + tuple(hbm(a) for a in lands)
                 + (jax.ShapeDtypeStruct((8, LANE), F32),))
    out = pl.pallas_call(
        body, name=name, out_shape=out_shape,
        in_specs=[HBM_SPEC] * (ns + nl) + [ANY_SPEC] * len(after),
        out_specs=(SEM_SPEC, SEM_SPEC) + (HBM_SPEC,) * (ns + nl)
        + (pl.BlockSpec(memory_space=pltpu.VMEM),),
        input_output_aliases={i: 2 + i for i in range(ns + nl)},
        scratch_shapes=[pltpu.SemaphoreType.DMA((n,))],
        compiler_params=pltpu.CompilerParams(has_side_effects=SIDE_EFFECT),
    )(*[pltpu.with_memory_space_constraint(a, pltpu.HBM) for a in tuple(srcs) + tuple(lands)],
      *after)
    _Order.tokens.append(out[-1])
    return out[0], out[1], out[2:2 + ns], out[2 + ns:2 + ns + nl]


def _split_wait(name, started, plan):
    send_sems, recv_sems, srcs, lands = started
    ns, nl, n = len(srcs), len(lands), len(plan)
    after = [_Order.last] if _Order.last is not None else []

    def body(*refs):
        s_refs = refs[:ns]
        l_refs = refs[ns:ns + nl]
        send_sems, recv_sems = refs[ns + nl:ns + nl + 2]
        me = _mesh_pos()
        for k in range(1, NDEV):
            peer = _flip(me, k)
            for i in range(n):
                src, dst = _split_refs(plan, s_refs, l_refs, i, peer, peer)
                cp = pltpu.make_async_remote_copy(
                    src_ref=src, dst_ref=dst,
                    send_sem=send_sems.at[7 * i + k - 1], recv_sem=recv_sems.at[7 * i + k - 1],
                    device_id=peer, device_id_type=MESH)
                cp.wait_send()
                cp.wait_recv()

    hbm = lambda a: pltpu.HBM(a.shape, a.dtype)
    out = pl.pallas_call(
        body, name=name,
        out_shape=tuple(hbm(a) for a in srcs) + tuple(hbm(a) for a in lands),
        in_specs=[HBM_SPEC] * (ns + nl) + [SEM_SPEC, SEM_SPEC] + [ANY_SPEC] * len(after),
        out_specs=(HBM_SPEC,) * (ns + nl),
        input_output_aliases={i: i for i in range(ns + nl)},
        compiler_params=pltpu.CompilerParams(has_side_effects=SIDE_EFFECT),
    )(*srcs, *lands, send_sems, recv_sems, *after)
    return list(out[ns:])


def _sum_slabs(name, a):
    s, r, c = a.shape

    def body(a_ref, o_ref):
        acc = a_ref[0].astype(F32)
        for i in range(1, s):
            acc = acc + a_ref[i].astype(F32)
        o_ref[...] = acc

    return pl.pallas_call(body, name=name, out_shape=jax.ShapeDtypeStruct((r, c), F32))(a)


def _adamw_update(g, w, m, v):
    mn = ADAM_B1 * m + (1.0 - ADAM_B1) * g
    vn = ADAM_B2 * v + (1.0 - ADAM_B2) * (g * g)
    m_hat = mn / (1.0 - ADAM_B1 ** ADAM_STEP)
    v_hat = vn / (1.0 - ADAM_B2 ** ADAM_STEP)
    delta = -ADAM_LR * (m_hat / (jnp.sqrt(v_hat) + ADAM_EPS) + ADAM_WD * w)
    return delta, mn, vn


def _adamw(name, gsrc, w, m, v, transposed=False, tr=256):
    s = gsrc.shape[0]
    r, c = w.shape
    step = LANE if transposed else 8
    tr = max(t for t in range(step, min(tr, r) + 1, step) if r % t == 0)

    def body(g_ref, w_ref, m_ref, v_ref, go_ref, d_ref, mo_ref, vo_ref):
        g = g_ref[0].astype(F32)
        for i in range(1, s):
            g = g + g_ref[i].astype(F32)
        if transposed:
            g = g.T[:, :c]
        delta, mn, vn = _adamw_update(g, w_ref[...], m_ref[...], v_ref[...])
        go_ref[...] = g
        d_ref[...] = delta
        mo_ref[...] = mn
        vo_ref[...] = vn

    blk = pl.BlockSpec((tr, c), lambda i: (i, 0))
    if transposed:
        g_spec = pl.BlockSpec((s, gsrc.shape[1], tr), lambda i: (0, 0, i))
    else:
        g_spec = pl.BlockSpec((s, tr, c), lambda i: (0, i, 0))
    return pl.pallas_call(
        body, name=name, out_shape=[jax.ShapeDtypeStruct((r, c), F32)] * 4,
        grid=(r // tr,),
        in_specs=[g_spec, blk, blk, blk], out_specs=[blk] * 4,
        compiler_params=_params(("parallel",)),
    )(gsrc, w, m, v)


REPLICATED = ("ffn1_norm", "mix_norm", "q_norm", "k_norm", "conv_b", "dt_bias", "a_log",
              "d_skip", "ssd_norm", "ffn2_norm")
ALL_WEIGHTS = ("ffn1_norm", "ffn1_w_gate", "ffn1_w_up", "ffn1_w_down", "mix_norm", "w_in",
               "q_norm", "k_norm", "conv_w", "conv_b", "dt_bias", "a_log", "d_skip", "ssd_norm",
               "w_attn_branch", "w_ssd_branch", "w_out", "ffn2_norm", "ffn2_w_gate", "ffn2_w_up",
               "ffn2_w_down")
BIG = (("ffn1_w_gate", True, "gu1t", 0), ("ffn1_w_up", True, "gu1t", 1),
       ("ffn1_w_down", False, "d1", 0), ("w_in", True, "wint", 0),
       ("w_attn_branch", True, "abt", 0), ("w_ssd_branch", False, "sb", 0),
       ("w_out", False, "out", 0),
       ("ffn2_w_gate", True, "gu2t", 0), ("ffn2_w_up", True, "gu2t", 1),
       ("ffn2_w_down", False, "d2", 0))


def _nrows(shape, cols):
    return -(-math.prod(shape) // cols)


def _pack_rows(arrs, cols, row_tile):
    parts = []
    for a in arrs:
        flat = a.reshape(-1)
        nr = -(-flat.shape[0] // cols)
        parts.append(jnp.pad(flat, (0, nr * cols - flat.shape[0])).reshape(nr, cols))
    out = jnp.concatenate(parts, axis=0)
    return jnp.pad(out, ((0, _round_up(out.shape[0], row_tile) - out.shape[0]), (0, 0)))


def _unpack_rows(packed, shapes):
    cols = packed.shape[-1]
    out, r0 = [], 0
    for sh in shapes:
        nr = _nrows(sh, cols)
        out.append(packed[r0:r0 + nr].reshape(-1)[:math.prod(sh)].reshape(tuple(sh)))
        r0 += nr
    return out


def kernel(x, ffn1_norm, ffn1_w_gate, ffn1_w_up, ffn1_w_down, mix_norm, w_in, q_norm, k_norm, conv_w, conv_b, dt_bias, a_log, d_skip, ssd_norm, w_attn_branch, w_ssd_branch, w_out, ffn2_norm, ffn2_w_gate, ffn2_w_up, ffn2_w_down, loss_target, m_ffn1_norm, m_ffn1_w_gate, m_ffn1_w_up, m_ffn1_w_down, m_mix_norm, m_w_in, m_q_norm, m_k_norm, m_conv_w, m_conv_b, m_dt_bias, m_a_log, m_d_skip, m_ssd_norm, m_w_attn_branch, m_w_ssd_branch, m_w_out, m_ffn2_norm, m_ffn2_w_gate, m_ffn2_w_up, m_ffn2_w_down, v_ffn1_norm, v_ffn1_w_gate, v_ffn1_w_up, v_ffn1_w_down, v_mix_norm, v_w_in, v_q_norm, v_k_norm, v_conv_w, v_conv_b, v_dt_bias, v_a_log, v_d_skip, v_ssd_norm, v_w_attn_branch, v_w_ssd_branch, v_w_out, v_ffn2_norm, v_ffn2_w_gate, v_ffn2_w_up, v_ffn2_w_down):
    given = dict(locals())
    wts = {n: given[n] for n in ALL_WEIGHTS}
    mom = {n: given["m_" + n] for n in ALL_WEIGHTS}
    var = {n: given["v_" + n] for n in ALL_WEIGHTS}
    d = x.shape[-1]
    nh = dt_bias.shape[1]
    my = _dev_index(_mesh_pos())

    def row_form(n, col_sharded):
        a = wts[n][0].T if col_sharded else wts[n][0]
        return jnp.pad(a, ((0, _round_up(a.shape[0], ROW_ALIGN) - a.shape[0]), (0, 0)))

    _Order.tokens, _Order.last = [], None
    shard = {n: row_form(n, cs).astype(MXU_DTYPE) for n, cs, _, _ in BIG}
    entries = {buf: [e for e in BIG if e[2] == buf] for buf in dict.fromkeys(e[2] for e in BIG)}

    def buf_shape(buf):
        r, c = shard[entries[buf][0][0]].shape
        return (len(entries[buf]) * NDEV * r, c)

    def gather_plan(bufs):
        srcs, lands, plan = [], [], []
        for li, buf in enumerate(bufs):
            lands.append(lax.empty(buf_shape(buf), MXU_DTYPE))
            for n, _, _, pos in entries[buf]:
                r = shard[n].shape[0]
                plan.append((len(srcs), 0, 0, li, pos * NDEV * r, r, r))
                srcs.append(shard[n])
        return srcs, lands, plan

    def scatter_plan(bufs, grads):
        srcs, lands, plan, names = [], [], [], []
        for si, buf in enumerate(bufs):
            srcs.append(grads[buf])
            for n, _, _, pos in entries[buf]:
                r, c = shard[n].shape
                plan.append((si, pos * NDEV * r, r, len(lands), 0, r, r))
                lands.append(lax.empty((NDEV * r, c), MXU_DTYPE))
                names.append(n)
        return srcs, lands, plan, names

    first_bufs = ("gu1t", "d1")
    shards, dests, out_shapes = [], [], []
    for bi, buf in enumerate(first_bufs):
        out_shapes.append((buf_shape(buf), MXU_DTYPE))
        for n, _, _, pos in entries[buf]:
            r = shard[n].shape[0]
            shards.append(shard[n])
            dests.append((bi, pos * NDEV * r, r))
    conv_rows = _pack_rows([conv_w[0]], LANE, ROW_ALIGN)
    shards.append(conv_rows)
    dests.append((len(first_bufs), 0, conv_rows.shape[0]))
    out_shapes.append(((NDEV * conv_rows.shape[0], LANE), F32))
    gathered = _gather("gather_first", shards, dests, out_shapes)

    in_cols = w_in.shape[2]
    in_pad = _round_up(in_cols, ROW_ALIGN)
    dt_off = NDEV * in_cols - 2 * d - nh
    second_bufs = ("wint",)
    third_bufs = ("abt", "sb", "out", "gu2t", "d2")
    plan2 = gather_plan(second_bufs)
    started2 = _split_start("gather_in_start", *plan2, after=[gathered[0]])
    started3 = []

    class Weights(dict):
        def __missing__(self, key):
            if key in ("maint", "dtt"):
                wint = _split_wait("gather_in_wait", started2, plan2[2])[0]
                plan3 = gather_plan(third_bufs)
                started3.append((_split_start("gather_rest_start", *plan3, after=[wint]), plan3[2]))
                wint = wint.reshape(NDEV, in_pad, d)[:, :in_cols].reshape(NDEV * in_cols, d)
                self["maint"] = jnp.concatenate([wint[:dt_off], wint[dt_off + nh:]], axis=0)
                self["dtt"] = jnp.pad(wint[dt_off:dt_off + nh], ((0, LANE - nh), (0, 0)))
            else:
                st, plan = started3[0]
                for buf, a in zip(third_bufs, _split_wait("gather_rest_wait", st, plan)):
                    self[buf] = a
            return self[key]

    w = Weights(gu1t=gathered[0], d1=gathered[1])
    p = {n: wts[n] for n in REPLICATED}
    conv_all = gathered[-1].reshape(NDEV, conv_rows.shape[0] * LANE)[:, :math.prod(conv_w.shape[1:])]
    p["conv_w"] = (conv_all.reshape((NDEV,) + conv_w.shape[1:]).transpose(1, 0, 2)
                   .reshape(conv_w.shape[1], NDEV * conv_w.shape[2]))

    groups = (("scatter_late", ("gu2t", "d2", "out", "abt", "sb")),
              ("scatter_in", ("maint", "dtt")),
              ("scatter_first", ("gu1t", "d1")))
    in_flight = []

    class Grads(dict):
        def __setitem__(self, key, value):
            dict.__setitem__(self, key, value)
            for tag, need in groups:
                if key in need and all(k in self for k in need):
                    if tag == "scatter_in":
                        gwin = jnp.concatenate([self["maint"][:dt_off], self["dtt"][:nh],
                                                self["maint"][dt_off:]], axis=0)
                        gwin = jnp.pad(gwin.reshape(NDEV, in_cols, d),
                                       ((0, 0), (0, in_pad - in_cols), (0, 0)))
                        bufs, grads = ("wint",), {"wint": gwin.reshape(NDEV * in_pad, d)}
                    else:
                        bufs, grads = need, self
                    srcs, lands, plan, names = scatter_plan(bufs, grads)
                    in_flight.append((tag, _split_start(tag + "_start", srcs, lands, plan),
                                      plan, names))

    loss_vec, dx, gw, gp = _local_step(x[0], loss_target[0], w, p, Grads(),
                                       aw=w_attn_branch.shape[1])

    small_names = REPLICATED + ("conv_w",)
    small_shapes = [gp[n].shape for n in small_names]
    small = _pack_rows([gp[n] for n in small_names], LANE, 8)
    small_all = _exchange("exchange_small", [], [], small)[0]

    outs = [{}, {}, {}, {}]
    col_sharded_of = {n: cs for n, cs, _, _ in BIG}
    for tag, started, plan, names in in_flight:
        for n, rv in zip(names, _split_wait(tag + "_wait", started, plan)):
            rv = rv.reshape(NDEV, shard[n].shape[0], shard[n].shape[1])
            res = _Order.done(_adamw("adamw_" + n, rv, wts[n][0], mom[n][0], var[n][0],
                                     transposed=col_sharded_of[n]))
            for k in range(4):
                outs[k][n] = res[k][None]

    small_g = _unpack_rows(_sum_slabs("sum_small_grads", small_all), small_shapes)
    small_g = dict(zip(small_names, small_g))
    cs = conv_w.shape[2]
    small_g["conv_w"] = lax.dynamic_slice_in_dim(small_g["conv_w"], my * cs, cs, axis=1)
    small_shard_shapes = [wts[n].shape[-2:] for n in small_names]
    sg = _pack_rows([small_g[n] for n in small_names], LANE, 8)
    sw = _pack_rows([wts[n] for n in small_names], LANE, 8)
    sm = _pack_rows([mom[n] for n in small_names], LANE, 8)
    sv = _pack_rows([var[n] for n in small_names], LANE, 8)
    res_small = _adamw("adamw_small", sg[None], sw, sm, sv, tr=sg.shape[0])
    for k in range(4):
        for n, a in zip(small_names, _unpack_rows(res_small[k], small_shard_shapes)):
            outs[k][n] = a.reshape(wts[n].shape)

    loss = lax.psum(0.5 * jnp.sum(loss_vec) / d, ("x", "y", "c"))
    result = [loss, dx[None]]
    for k in range(4):
        result += [outs[k][n] for n in ALL_WEIGHTS]
    return tuple(result)
```

```python
import functools
import math

import numpy as np
import jax
import jax.numpy as jnp
from jax import lax
from jax.experimental import pallas as pl
from jax.experimental.pallas import tpu as pltpu

F32 = jnp.float32
BF16 = jnp.bfloat16
MXU_DTYPE = BF16
ACT_DTYPE = BF16

NDEV = 8
EPS = 1e-6
HD = 64
QB = 128
PATTERNS = ((128, 1), (512, 4), (2048, 16))
ALIBI_MAX_EXP = 8.0
SSD_P = 64
SSD_N = 128
SSD_G = 4
SSD_Q = 128
SSD_K = 4
NEG = -1e30
LANE = 128
ROW_ALIGN = 16
VMEM_LIMIT = 56 * 1024 * 1024

ADAM_LR, ADAM_B1, ADAM_B2, ADAM_EPS, ADAM_WD, ADAM_STEP = 0.001, 0.9, 0.999, 1e-8, 0.01, 10

NN = (((1,), (0,)), ((), ()))
NT = (((1,), (1,)), ((), ()))
TN = (((0,), (0,)), ((), ()))


def _dot(a, b, dims=NN):
    return lax.dot_general(a.astype(MXU_DTYPE), b.astype(MXU_DTYPE), dims,
                           preferred_element_type=F32)


def _split3(a):
    hi = a.astype(BF16)
    r = a - hi.astype(F32)
    mid = r.astype(BF16)
    lo = (r - mid.astype(F32)).astype(BF16)
    return hi, mid, lo


def _dot3(a, b, dims=NN, split=0):
    if split == 0:
        bb = b.astype(BF16)
        parts = [lax.dot_general(s, bb, dims, preferred_element_type=F32) for s in _split3(a)]
    else:
        aa = a.astype(BF16)
        parts = [lax.dot_general(aa, s, dims, preferred_element_type=F32) for s in _split3(b)]
    return parts[0] + parts[1] + parts[2]


@jax.custom_vjp
def _spread(v, e):
    return _dot3(v, e)


def _spread_fwd(v, e):
    return _dot3(v, e), e


def _spread_bwd(e, g):
    return _dot3(g, e, NT), jnp.zeros_like(e)


_spread.defvjp(_spread_fwd, _spread_bwd)


@jax.custom_vjp
def _running_sum(a, lower):
    return _dot3(lower, a, NN, split=1)


def _running_sum_fwd(a, lower):
    return _dot3(lower, a, NN, split=1), lower


def _running_sum_bwd(lower, g):
    return _dot3(lower, g, TN, split=1), jnp.zeros_like(lower)


_running_sum.defvjp(_running_sum_fwd, _running_sum_bwd)


def _tile(n, cap):
    if n <= cap:
        return n
    best = None
    for t in range(LANE, cap + 1, LANE):
        if n % t == 0:
            best = t
    assert best is not None, (n, cap)
    return best


def _params(sem):
    return pltpu.CompilerParams(dimension_semantics=sem, vmem_limit_bytes=VMEM_LIMIT)


def _round_up(n, m):
    return -(-n // m) * m


class _Order:
    tokens = []
    last = None

    @classmethod
    def take(cls):
        out, cls.tokens = cls.tokens, []
        return out

    @classmethod
    def done(cls, result):
        cls.last = result[0] if isinstance(result, (list, tuple)) else result
        return result


ANY_SPEC = pl.BlockSpec(memory_space=pl.ANY)


def _mm(name, a, b, mode, out_dtype=F32, res=None, scale=1.0,
        cap_m=1408, cap_n=1408, cap_k=1408):
    segs = list(a) if isinstance(a, (list, tuple)) else [a]
    nseg = len(segs)
    if mode == "tn":
        k = segs[0].shape[0]
        widths = [s.shape[1] for s in segs]
        m = sum(widths)
        k2, n = b.shape
        tm = _tile(math.gcd(*widths), cap_m)
        tk = _tile(k, cap_k)
        counts = [wd // tm for wd in widths]
    else:
        m = segs[0].shape[0]
        widths = [s.shape[1] for s in segs]
        k = sum(widths)
        (k2, n) = b.shape if mode == "nn" else b.shape[::-1]
        tm = _tile(m, cap_m)
        tk = _tile(math.gcd(*widths), cap_k)
        counts = [wd // tk for wd in widths]
    assert k == k2, (name, [s.shape for s in segs], b.shape, mode)
    tn = _tile(n, cap_n)
    nk = k // tk
    starts = [sum(counts[:s]) for s in range(nseg)]
    dims = {"nn": NN, "nt": NT, "tn": TN}[mode]

    def a_spec(s):
        lo, cnt = starts[s], counts[s]
        if mode == "tn":
            if nseg == 1:
                return pl.BlockSpec((tk, tm), lambda i, j, kk: (kk, i))
            return pl.BlockSpec(
                (tk, tm), lambda i, j, kk: (jnp.where((i >= lo) & (i < lo + cnt), kk, 0),
                                            jnp.clip(i - lo, 0, cnt - 1)))
        if nseg == 1:
            return pl.BlockSpec((tm, tk), lambda i, j, kk: (i, kk))
        return pl.BlockSpec((tm, tk), lambda i, j, kk: (i, jnp.clip(kk - lo, 0, cnt - 1)))

    b_spec = (pl.BlockSpec((tn, tk), lambda i, j, kk: (j, kk)) if mode == "nt"
              else pl.BlockSpec((tk, tn), lambda i, j, kk: (kk, j)))
    o_spec = pl.BlockSpec((tm, tn), lambda i, j, kk: (i, j))
    has_res = res is not None
    use_acc = nk > 1 or nseg > 1
    ties = _Order.take()
    nt_ = len(ties)

    def body(*refs):
        a_refs = refs[:nseg]
        b_ref = refs[nseg]
        r_ref = refs[nseg + 1] if has_res else None
        o_ref = refs[nseg + 1 + has_res + nt_]
        scr = refs[nseg + 2 + has_res + nt_:]

        def finish(acc):
            if scale != 1.0:
                acc = acc * scale
            if has_res:
                acc = r_ref[...].astype(F32) + acc
            o_ref[...] = acc.astype(o_ref.dtype)

        if not use_acc:
            finish(_dot(a_refs[0][...], b_ref[...], dims))
            return
        acc_ref = scr[0]
        kk = pl.program_id(2)
        sel = pl.program_id(0) if mode == "tn" else kk

        @pl.when(kk == 0)
        def _():
            acc_ref[...] = jnp.zeros_like(acc_ref)

        for s in range(nseg):
            def add(s=s):
                acc_ref[...] += _dot(a_refs[s][...], b_ref[...], dims)
            if nseg == 1:
                add()
            else:
                pl.when((sel >= starts[s]) & (sel < starts[s] + counts[s]))(add)

        @pl.when(kk == nk - 1)
        def _():
            finish(acc_ref[...])

    in_specs = ([a_spec(s) for s in range(nseg)] + [b_spec] + ([o_spec] if has_res else [])
                + [ANY_SPEC] * nt_)
    args = tuple(segs) + (b,) + ((res,) if has_res else ()) + tuple(ties)
    return _Order.done(pl.pallas_call(
        body, name=name,
        out_shape=jax.ShapeDtypeStruct((m, n), out_dtype),
        grid=(m // tm, n // tn, nk),
        in_specs=in_specs, out_specs=o_spec,
        scratch_shapes=[pltpu.VMEM((tm, tn), F32)] if use_acc else [],
        compiler_params=_params(("parallel", "parallel", "arbitrary")),
    )(*args))


def _rw(name, fn, ins, outs, accs=(), tr=256, ncb=1):
    t = next(a.shape[0] for kind, a, _, _ in ins if kind == "row")
    assert t % tr == 0
    n_in = len(ins)
    n_pieces = sum(len(w) for w, _ in outs)

    def spec(kind, arr, width, base):
        if kind == "row":
            return pl.BlockSpec((tr, width), lambda j, i: (i, base + j))
        return pl.BlockSpec((arr.shape[0], width), lambda j, i: (0, base + j))

    in_specs = [spec(*s) for s in ins]
    out_shapes, out_specs = [], []
    for widths, dt in outs:
        w = sum(widths)
        out_shapes.append(jax.ShapeDtypeStruct((t, w * ncb), dt))
        out_specs.append(pl.BlockSpec((tr, w), lambda j, i: (i, j)))
    for rows, width in accs:
        out_shapes.append(jax.ShapeDtypeStruct((rows, width * ncb), F32))
        out_specs.append(pl.BlockSpec((rows, width), lambda j, i: (0, j)))

    ties = _Order.take()
    nt_ = len(ties)
    in_specs = in_specs + [ANY_SPEC] * nt_

    def body(*refs):
        vals = [r[...] for r in refs[:n_in]]
        res = fn(*vals)
        o_refs = refs[n_in + nt_:n_in + nt_ + len(outs)]
        a_refs = refs[n_in + nt_ + len(outs):]
        p = 0
        for (widths, _), o_ref in zip(outs, o_refs):
            off = 0
            for w in widths:
                if len(widths) == 1:
                    o_ref[...] = res[p].astype(o_ref.dtype)
                else:
                    o_ref[:, off:off + w] = res[p].astype(o_ref.dtype)
                off += w
                p += 1
        i = pl.program_id(1)
        for a_ref, v in zip(a_refs, res[n_pieces:]):
            @pl.when(i == 0)
            def _(a_ref=a_ref, v=v):
                a_ref[...] = v

            @pl.when(i > 0)
            def _(a_ref=a_ref, v=v):
                a_ref[...] += v

    return _Order.done(pl.pallas_call(
        body, name=name, out_shape=out_shapes,
        grid=(ncb, t // tr), in_specs=in_specs, out_specs=out_specs,
        compiler_params=_params(("parallel", "arbitrary")),
    )(*[a for _, a, _, _ in ins], *ties))


def _rms(x, g):
    x = x.astype(F32)
    return x * lax.rsqrt(jnp.mean(x * x, axis=-1, keepdims=True) + EPS) * g


def _silu(x):
    return x * jax.nn.sigmoid(x)


def _colsum(v):
    return jnp.sum(v, axis=0, keepdims=True)


def _pair_norm(x, g):
    w = 2 * HD
    ri = lax.broadcasted_iota(jnp.int32, (w, w), 0)
    ci = lax.broadcasted_iota(jnp.int32, (w, w), 1)
    same_head = ((ri < HD) == (ci < HD)).astype(F32)
    ms = _spread(x * x, same_head) * (1.0 / HD)
    return x * lax.rsqrt(ms + EPS) * g


def _attn_pair(qn, kcn, kpn, vc, vp, coef0, coef1, first):
    w = 2 * HD
    lane = lax.broadcasted_iota(jnp.int32, (1, w), 1)
    scale = 1.0 / math.sqrt(HD)
    a_idx = lax.broadcasted_iota(jnp.int32, (QB, QB), 0)
    c_idx = lax.broadcasted_iota(jnp.int32, (QB, QB), 1)
    rel_c = a_idx - c_idx
    dist_c = rel_c.astype(F32)
    dist_p = dist_c + float(QB)
    keep_c = rel_c >= 0
    keep_p = jnp.logical_and(rel_c <= 0, jnp.logical_not(first))
    out = jnp.zeros((QB, w), F32)
    lb = jnp.zeros((QB, w), F32)
    for hh, coef in enumerate((coef0, coef1)):
        mask = ((lane < HD) if hh == 0 else (lane >= HD)).astype(F32)
        qm = qn * mask
        lc = jnp.where(keep_c, _dot(qm, kcn, NT) * scale - coef * dist_c, NEG)
        lp = jnp.where(keep_p, _dot(qm, kpn, NT) * scale - coef * dist_p, NEG)
        m = lax.stop_gradient(jnp.maximum(jnp.max(lc, axis=-1, keepdims=True),
                                          jnp.max(lp, axis=-1, keepdims=True)))
        pc = jnp.exp(lc - m)
        pp = jnp.exp(lp - m)
        l = jnp.sum(pc, axis=-1, keepdims=True) + jnp.sum(pp, axis=-1, keepdims=True)
        inv = 1.0 / l
        out = out + (_dot(pc * inv, vc) + _dot(pp * inv, vp)) * mask
        lb = lb + (m + jnp.log(l)) * mask
    return out, lb


NORM_ROWS = 128
ATTN_UNROLL_FWD = 4
ATTN_UNROLL_BWD = 2


def _unit_rows(u, d):
    r = u & (d - 1)
    n = u >> (d.bit_length() - 1)

    def rows(blk):
        start = pl.multiple_of(blk * (QB * d), QB * d)
        return pl.ds(start, QB) if d == 1 else pl.ds(start + r, QB, stride=d)

    return rows(n), rows(jnp.maximum(n - 1, 0)), n == 0


def _attn_prologue(t, q_ref, k_ref, v_ref, qg_ref, kg_ref, qf, kf, vf):
    def chunk(c, carry):
        rows = pl.ds(pl.multiple_of(c * NORM_ROWS, NORM_ROWS), NORM_ROWS)
        qf[rows, :] = _pair_norm(q_ref[rows, :].astype(F32), qg_ref[...])
        kf[rows, :] = _pair_norm(k_ref[rows, :].astype(F32), kg_ref[...])
        vf[rows, :] = v_ref[rows, :].astype(F32)
        return carry
    lax.fori_loop(0, t // NORM_ROWS, chunk, 0)


def _attn_specs(t, bases):
    w = 2 * HD
    ins = [pl.BlockSpec((t, w), functools.partial(lambda p, c, b: (0, b + p), b=b)) for b in bases]
    gain = pl.BlockSpec((1, w), lambda p, c: (0, 0))
    blk = pl.BlockSpec((t, w), lambda p, c: (0, p))
    return ins, gain, blk


def _attn_fwd(name, proj, bases, qg, kg, coefs, d):
    t = proj.shape[0]
    npairs = coefs.shape[0] // 2
    w = 2 * HD
    ins, gain, blk = _attn_specs(t, bases)

    def body(coef_ref, q_ref, k_ref, v_ref, qg_ref, kg_ref, o_ref, l_ref, qf, kf, vf):
        p = pl.program_id(0)
        c0, c1 = coef_ref[2 * p], coef_ref[2 * p + 1]
        _attn_prologue(t, q_ref, k_ref, v_ref, qg_ref, kg_ref, qf, kf, vf)

        def step(u, carry):
            cur, prv, first = _unit_rows(u, d)
            o, lb = _attn_pair(qf[cur, :], kf[cur, :], kf[prv, :], vf[cur, :], vf[prv, :],
                               c0, c1, first)
            o_ref[cur, :] = o
            l_ref[cur, :] = lb
            return carry

        lax.fori_loop(0, t // QB, step, 0, unroll=ATTN_UNROLL_FWD)

    return pl.pallas_call(
        body, name=name,
        out_shape=[jax.ShapeDtypeStruct((t, npairs * w), F32)] * 2,
        grid_spec=pltpu.PrefetchScalarGridSpec(
            num_scalar_prefetch=1, grid=(npairs,),
            in_specs=ins + [gain, gain], out_specs=[blk, blk],
            scratch_shapes=[pltpu.VMEM((t, w), F32)] * 3),
        compiler_params=_params(("arbitrary",)),
    )(coefs, proj, proj, proj, qg, kg)


def _attn_bwd(name, proj, bases, qg, kg, coefs, d, do, dl):
    t = proj.shape[0]
    npairs = coefs.shape[0] // 2
    w = 2 * HD
    ins, gain, blk = _attn_specs(t, bases)

    def body(coef_ref, q_ref, k_ref, v_ref, qg_ref, kg_ref, do_ref, dl_ref,
             dq_ref, dk_ref, dv_ref, dqg_ref, dkg_ref, qf, kf, vf, dqf, dkf, dvf):
        p = pl.program_id(0)
        c0, c1 = coef_ref[2 * p], coef_ref[2 * p + 1]
        _attn_prologue(t, q_ref, k_ref, v_ref, qg_ref, kg_ref, qf, kf, vf)
        dkf[...] = jnp.zeros_like(dkf)
        dvf[...] = jnp.zeros_like(dvf)

        def step(u, carry):
            cur, prv, first = _unit_rows(u, d)
            f = lambda a, b, c, e, g: _attn_pair(a, b, c, e, g, c0, c1, first)
            _, vjp = jax.vjp(f, qf[cur, :], kf[cur, :], kf[prv, :], vf[cur, :], vf[prv, :])
            dq, dkc, dkp, dvc, dvp = vjp((do_ref[cur, :], dl_ref[cur, :]))
            dqf[cur, :] = dq
            dkf[cur, :] += dkc
            dkf[prv, :] += dkp
            dvf[cur, :] += dvc
            dvf[prv, :] += dvp
            return carry

        lax.fori_loop(0, t // QB, step, 0, unroll=ATTN_UNROLL_BWD)

        def chunk(c, carry):
            dqg_acc, dkg_acc = carry
            rows = pl.ds(pl.multiple_of(c * NORM_ROWS, NORM_ROWS), NORM_ROWS)
            _, vq = jax.vjp(_pair_norm, q_ref[rows, :].astype(F32), qg_ref[...])
            dq, dqg = vq(dqf[rows, :])
            _, vk = jax.vjp(_pair_norm, k_ref[rows, :].astype(F32), kg_ref[...])
            dk, dkg = vk(dkf[rows, :])
            dq_ref[rows, :] = dq.astype(dq_ref.dtype)
            dk_ref[rows, :] = dk.astype(dk_ref.dtype)
            dv_ref[rows, :] = dvf[rows, :].astype(dv_ref.dtype)
            return dqg_acc + dqg, dkg_acc + dkg

        zero = jnp.zeros((1, w), F32)
        dqg, dkg = lax.fori_loop(0, t // NORM_ROWS, chunk, (zero, zero))

        @pl.when(p == 0)
        def _():
            dqg_ref[...] = dqg
            dkg_ref[...] = dkg

        @pl.when(p > 0)
        def _():
            dqg_ref[...] += dqg
            dkg_ref[...] += dkg

    big = jax.ShapeDtypeStruct((t, npairs * w), ACT_DTYPE)
    small = jax.ShapeDtypeStruct((1, w), F32)
    return pl.pallas_call(
        body, name=name,
        out_shape=[big, big, big, small, small],
        grid_spec=pltpu.PrefetchScalarGridSpec(
            num_scalar_prefetch=1, grid=(npairs,),
            in_specs=ins + [gain, gain, blk, blk],
            out_specs=[blk, blk, blk, gain, gain],
            scratch_shapes=[pltpu.VMEM((t, w), F32)] * 6),
        compiler_params=_params(("arbitrary",)),
    )(coefs, proj, proj, proj, qg, kg, do, dl)


def _shift_down(u, s):
    if s == 0:
        return u
    rows = lax.broadcasted_iota(jnp.int32, u.shape, 0)
    return jnp.where(rows >= s, pltpu.roll(u, s, 0), 0.0)


def _shift_up(u, s):
    if s == 0:
        return u
    t = u.shape[0]
    rows = lax.broadcasted_iota(jnp.int32, u.shape, 0)
    return jnp.where(rows < t - s, pltpu.roll(u, t - s, 0), 0.0)


def _conv_pre(u, w, b):
    y = b
    for kk in range(SSD_K):
        y = y + w[kk:kk + 1, :] * _shift_down(u, SSD_K - 1 - kk)
    return y


def _conv_fwd(name, src, base, w, b, cw=128):
    t = src.shape[0]
    c = w.shape[1]

    def body(u_ref, w_ref, b_ref, o_ref):
        y = _conv_pre(u_ref[...].astype(F32), w_ref[...], b_ref[...])
        o_ref[...] = _silu(y).astype(o_ref.dtype)

    return pl.pallas_call(
        body, name=name, out_shape=jax.ShapeDtypeStruct((t, c), ACT_DTYPE),
        grid=(c // cw,),
        in_specs=[pl.BlockSpec((t, cw), lambda j: (0, base + j)),
                  pl.BlockSpec((SSD_K, cw), lambda j: (0, j)),
                  pl.BlockSpec((1, cw), lambda j: (0, j))],
        out_specs=pl.BlockSpec((t, cw), lambda j: (0, j)),
        compiler_params=_params(("parallel",)),
    )(src, w, b)


def _conv_bwd(name, src, base, w, b, dout, cw=128):
    t = src.shape[0]
    c = w.shape[1]

    def body(u_ref, w_ref, b_ref, d_ref, du_ref, dw_ref, db_ref):
        u = u_ref[...].astype(F32)
        wv = w_ref[...]
        y = _conv_pre(u, wv, b_ref[...])
        sg = jax.nn.sigmoid(y)
        dy = d_ref[...].astype(F32) * (sg * (1.0 + y * (1.0 - sg)))
        du = jnp.zeros_like(u)
        for kk in range(SSD_K):
            s = SSD_K - 1 - kk
            du = du + wv[kk:kk + 1, :] * _shift_up(dy, s)
            dw_ref[kk:kk + 1, :] = _colsum(dy * _shift_down(u, s))
        du_ref[...] = du.astype(du_ref.dtype)
        db_ref[...] = _colsum(dy)

    return pl.pallas_call(
        body, name=name,
        out_shape=[jax.ShapeDtypeStruct((t, c), ACT_DTYPE),
                   jax.ShapeDtypeStruct((SSD_K, c), F32),
                   jax.ShapeDtypeStruct((1, c), F32)],
        grid=(c // cw,),
        in_specs=[pl.BlockSpec((t, cw), lambda j: (0, base + j)),
                  pl.BlockSpec((SSD_K, cw), lambda j: (0, j)),
                  pl.BlockSpec((1, cw), lambda j: (0, j)),
                  pl.BlockSpec((t, cw), lambda j: (0, j))],
        out_specs=[pl.BlockSpec((t, cw), lambda j: (0, j)),
                   pl.BlockSpec((SSD_K, cw), lambda j: (0, j)),
                   pl.BlockSpec((1, cw), lambda j: (0, j))],
        compiler_params=_params(("parallel",)),
    )(src, w, b, dout)


def _softplus(x):
    return jnp.maximum(x, 0.0) + jnp.log(1.0 + jnp.exp(-jnp.abs(x)))


def _ssd_chunk(xbc, dtraw, bias, alog, states):
    wd = states[0].shape[1]
    nj = wd // SSD_P
    inner = SSD_G * wd
    dt = _softplus(dtraw + bias)
    a = dt * (-jnp.exp(alog))
    li = lax.broadcasted_iota(jnp.int32, (SSD_Q, SSD_Q), 0)
    si = lax.broadcasted_iota(jnp.int32, (SSD_Q, SSD_Q), 1)
    causal = li >= si
    acs = _running_sum(a, causal.astype(F32))
    acs_t = acs.T
    a_last = acs[SSD_Q - 1:SSD_Q, :]
    grow = jnp.exp(acs)
    shrink = jnp.exp(a_last - acs)
    hrow = lax.broadcasted_iota(jnp.int32, (LANE, wd), 0)
    wcol = lax.broadcasted_iota(jnp.int32, (LANE, wd), 1)
    lane = lax.broadcasted_iota(jnp.int32, (1, LANE), 1)
    ys, snext = [], []
    for g in range(SSD_G):
        lo = (hrow - g * nj) * SSD_P
        head_lanes = jnp.logical_and(wcol >= lo, wcol < lo + SSD_P).astype(F32)
        xs = xbc[:, g * wd:(g + 1) * wd]
        bm = xbc[:, inner + g * SSD_N:inner + (g + 1) * SSD_N]
        cm = xbc[:, inner + (SSD_G + g) * SSD_N:inner + (SSD_G + g + 1) * SSD_N]
        xdt = xs * _spread(dt, head_lanes)
        grow_x = _spread(grow, head_lanes)
        y_off = _dot(cm, states[g]) * grow_x
        s_new = (states[g] * grow_x[SSD_Q - 1:SSD_Q, :]
                 + _dot(bm, xdt * _spread(shrink, head_lanes), TN))
        cb = _dot(cm, bm, NT)
        pieces = []
        for i in range(wd // LANE):
            xp = xdt[:, i * LANE:(i + 1) * LANE]
            acc = jnp.zeros((SSD_Q, LANE), F32)
            for hh in range(LANE // SSD_P):
                h = g * nj + i * (LANE // SSD_P) + hh
                decay = jnp.exp(jnp.where(causal, acs[:, h:h + 1] - acs_t[h:h + 1, :], NEG))
                keep = jnp.logical_and(lane >= hh * SSD_P, lane < (hh + 1) * SSD_P).astype(F32)
                acc = acc + _dot(cb * decay, xp * keep)
            pieces.append(acc)
        y_diag = pieces[0] if len(pieces) == 1 else jnp.concatenate(pieces, axis=1)
        ys.append(y_diag + y_off)
        snext.append(s_new)
    return ys, snext


def _ssd_specs(cdim, wd, rev, nc):
    ch = (lambda c: nc - 1 - c) if rev else (lambda c: c)
    full = lambda width: pl.BlockSpec((SSD_Q, width), lambda c: (ch(c), 0))
    vec = pl.BlockSpec((1, LANE), lambda c: (0, 0))
    st = pl.BlockSpec((1, SSD_G, SSD_N, wd), lambda c: (ch(c), 0, 0, 0))
    return full, vec, st


def _ssd_fwd(name, xbc, dtraw, bias, alog, inner):
    t, cdim = xbc.shape
    wd = inner // SSD_G
    nc = t // SSD_Q
    full, vec, st = _ssd_specs(cdim, wd, False, nc)

    def body(x_ref, r_ref, b_ref, a_ref, y_ref, st_ref, s_scr):
        @pl.when(pl.program_id(0) == 0)
        def _():
            s_scr[...] = jnp.zeros_like(s_scr)

        sprev = [s_scr[g] for g in range(SSD_G)]
        ys, snext = _ssd_chunk(x_ref[...].astype(F32), r_ref[...], b_ref[...], a_ref[...], sprev)
        for g in range(SSD_G):
            st_ref[0, g] = sprev[g]
            y_ref[:, g * wd:(g + 1) * wd] = ys[g]
            s_scr[g] = snext[g]

    return pl.pallas_call(
        body, name=name,
        out_shape=[jax.ShapeDtypeStruct((t, inner), F32),
                   jax.ShapeDtypeStruct((nc, SSD_G, SSD_N, wd), F32)],
        grid=(nc,),
        in_specs=[full(cdim), full(LANE), vec, vec],
        out_specs=[full(inner), st],
        scratch_shapes=[pltpu.VMEM((SSD_G, SSD_N, wd), F32)],
        compiler_params=_params(("arbitrary",)),
    )(xbc, dtraw, bias, alog)


def _ssd_bwd(name, xbc, dtraw, bias, alog, states, dy, dxs_extra):
    t, cdim = xbc.shape
    inner = dy.shape[1]
    wd = inner // SSD_G
    nc = t // SSD_Q
    full, vec, st = _ssd_specs(cdim, wd, True, nc)

    def body(x_ref, r_ref, b_ref, a_ref, st_ref, dy_ref, dx0_ref,
             dx_ref, dr_ref, db_ref, da_ref, ds_scr):
        first = pl.program_id(0) == 0

        @pl.when(first)
        def _():
            ds_scr[...] = jnp.zeros_like(ds_scr)

        sprev = [st_ref[0, g] for g in range(SSD_G)]
        _, vjp = jax.vjp(_ssd_chunk, x_ref[...].astype(F32), r_ref[...], b_ref[...], a_ref[...],
                         sprev)
        dyv = dy_ref[...]
        dys = [dyv[:, g * wd:(g + 1) * wd] for g in range(SSD_G)]
        dsn = [ds_scr[g] for g in range(SSD_G)]
        dx, dr, db, da, dsp = vjp((dys, dsn))
        dx_ref[:, :inner] = dx[:, :inner] + dx0_ref[...].astype(F32)
        dx_ref[:, inner:] = dx[:, inner:]
        dr_ref[...] = dr
        for g in range(SSD_G):
            ds_scr[g] = dsp[g]

        @pl.when(first)
        def _():
            db_ref[...] = db
            da_ref[...] = da

        @pl.when(jnp.logical_not(first))
        def _():
            db_ref[...] += db
            da_ref[...] += da

    return pl.pallas_call(
        body, name=name,
        out_shape=[jax.ShapeDtypeStruct((t, cdim), F32),
                   jax.ShapeDtypeStruct((t, LANE), F32),
                   jax.ShapeDtypeStruct((1, LANE), F32),
                   jax.ShapeDtypeStruct((1, LANE), F32)],
        grid=(nc,),
        in_specs=[full(cdim), full(LANE), vec, vec, st, full(inner), full(inner)],
        out_specs=[full(cdim), full(LANE), vec, vec],
        scratch_shapes=[pltpu.VMEM((SSD_G, SSD_N, wd), F32)],
        compiler_params=_params(("arbitrary",)),
    )(xbc, dtraw, bias, alog, states, dy, dxs_extra)


def _act(g, u):
    return _silu(g.astype(F32)) * u.astype(F32)


def _mix(o0, o1, o2, l0, l1, l2):
    m = lax.stop_gradient(jnp.maximum(jnp.maximum(l0, l1), l2))
    e0, e1, e2 = jnp.exp(l0 - m), jnp.exp(l1 - m), jnp.exp(l2 - m)
    return (e0 * o0 + e1 * o1 + e2 * o2) / (e0 + e1 + e2)


def _gate(y, xs, z, dexp, gain):
    v = (y + xs.astype(F32) * dexp) * _silu(z.astype(F32))
    return _rms(v, gain)


def _merge(ga, gs, ap, sp):
    return jax.nn.sigmoid(ga.astype(F32)) * ap + jax.nn.sigmoid(gs.astype(F32)) * sp


def _alibi_coefs(hp):
    n = hp * len(PATTERNS)
    slopes = np.exp2(-ALIBI_MAX_EXP * np.arange(1, n + 1, dtype=np.float32) / n).astype(np.float32)
    return [jnp.asarray(slopes[g * hp:(g + 1) * hp] * np.float32(d))
            for g, (_, d) in enumerate(PATTERNS)]


def _local_step(x, tgt, w, p, gw_=None, aw=None):
    t, d = x.shape
    dff = w["d1"].shape[0]
    aw = w["abt"].shape[1] if aw is None else aw
    hp = aw // HD
    qkv = len(PATTERNS) * aw
    inner = p["ssd_norm"].shape[1]
    nh = p["dt_bias"].shape[1]
    gw_ = {} if gw_ is None else gw_
    gw = inner // SSD_G
    cdim = inner + 2 * SSD_G * SSD_N
    z_off, xbc_off = 3 * qkv, 3 * qkv + inner
    ga_off = xbc_off + cdim
    gs_off = ga_off + d
    hw = d // 2
    assert z_off % gw == 0 and xbc_off % LANE == 0 and ga_off % hw == 0 and gs_off % hw == 0
    assert (nh // SSD_G) * SSD_P == gw and hp % 2 == 0 and aw % LANE == 0 and nh <= LANE
    gdt = MXU_DTYPE

    row = lambda a, width, base=0: ("row", a, width, base)
    const = lambda a, width, base=0: ("const", a, width, base)

    def rms_fwd(name, xin, g):
        return _rw(name, lambda xv, gv: (_rms(xv, gv),), [row(xin, d), const(g, d)],
                   [((d,), ACT_DTYPE)])[0]

    def rms_bwd(name, xin, g, dh, dres):
        def fn(xv, gv, dhv, drv):
            _, vjp = jax.vjp(_rms, xv, gv)
            dx, dg = vjp(dhv.astype(F32))
            return drv + dx, dg
        return _rw(name, fn, [row(xin, d), const(g, d), row(dh, d), row(dres, d)],
                   [((d,), F32)], accs=[(1, d)])

    def ffn_fwd(tag, xin, g, wgut, wd):
        h = rms_fwd(tag + "_norm", xin, g)
        gu = _mm(tag + "_up", h, wgut, "nt", ACT_DTYPE)
        a = _rw(tag + "_act", lambda gv, uv: (_act(gv, uv),),
                [row(gu, dff, 0), row(gu, dff, 1)], [((dff,), ACT_DTYPE)])[0]
        xo = _mm(tag + "_down", a, wd, "nn", F32, res=xin, scale=0.5)
        return xo, (h, gu, a)

    def ffn_bwd(tag, xin, g, wgut, wd, saved, dxo, key_gu, key_d):
        h, gu, a = saved
        da = _mm(tag + "_da", dxo, wd, "nt", ACT_DTYPE, scale=0.5)
        gw_[key_d] = _mm(tag + "_dwd", a, dxo, "tn", gdt, scale=0.5)

        def fn(gv, uv, dav):
            _, vjp = jax.vjp(_act, gv, uv)
            return vjp(dav.astype(F32))
        dgu = _rw(tag + "_dact", fn, [row(gu, dff, 0), row(gu, dff, 1), row(da, dff)],
                  [((dff, dff), ACT_DTYPE)])[0]
        gw_[key_gu] = _mm(tag + "_dwgu", dgu, h, "tn", gdt)
        dh = _mm(tag + "_dh", dgu, wgut, "nn", F32)
        return rms_bwd(tag + "_dnorm", xin, g, dh, dxo)

    x1, ffn1_saved = ffn_fwd("ffn1", x, p["ffn1_norm"], w["gu1t"], w["d1"])
    h2 = rms_fwd("mix_norm", x1, p["mix_norm"])
    proj = _mm("in_proj", h2, w["maint"], "nt", ACT_DTYPE, cap_m=512, cap_n=2944)
    dtraw = _mm("dt_proj", h2, w["dtt"], "nt", F32)

    coefs = _alibi_coefs(hp)
    qg2 = jnp.concatenate([p["q_norm"], p["q_norm"]], axis=1)
    kg2 = jnp.concatenate([p["k_norm"], p["k_norm"]], axis=1)
    pw = 2 * HD
    attn_bases = [[(off + gi * aw) // pw for off in (0, qkv, 2 * qkv)]
                  for gi in range(len(PATTERNS))]
    attn_o, attn_l = [], []
    for gi, (_, dil) in enumerate(PATTERNS):
        o, l = _attn_fwd(f"attn_fwd{gi}", proj, attn_bases[gi], qg2, kg2, coefs[gi], dil)
        attn_o.append(o)
        attn_l.append(l)
    ao = _rw("attn_mix", lambda *v: (_mix(*v),), [row(a, aw) for a in attn_o + attn_l],
             [((aw,), ACT_DTYPE)])[0]

    xbc = _conv_fwd("conv_fwd", proj, xbc_off // LANE, p["conv_w"], p["conv_b"])
    pad = lambda v: jnp.pad(v, ((0, 0), (0, LANE - nh)))
    bias_p, alog_p = pad(p["dt_bias"]), pad(p["a_log"])
    yssd, states = _ssd_fwd("ssd_fwd", xbc, dtraw, bias_p, alog_p, inner)
    dexp = jnp.repeat(p["d_skip"], SSD_P, axis=1)
    gate_ins = [row(yssd, gw), row(xbc, gw), row(proj, gw, z_off // gw),
                const(dexp, gw), const(p["ssd_norm"], gw)]
    yn = _rw("ssd_gate", lambda *v: (_gate(*v),), gate_ins, [((gw,), ACT_DTYPE)], ncb=SSD_G)[0]

    ap = _mm("attn_out", ao, w["abt"], "nt", F32)
    sp = _mm("ssd_out", yn, w["sb"], "nn", F32)
    merge_ins = [row(proj, hw, ga_off // hw), row(proj, hw, gs_off // hw), row(ap, hw), row(sp, hw)]
    mg = _rw("merge", lambda *v: (_merge(*v),), merge_ins, [((hw,), ACT_DTYPE)], ncb=2)[0]
    x2 = _mm("mix_out", mg, w["out"], "nn", F32, res=x1)
    x3, ffn2_saved = ffn_fwd("ffn2", x2, p["ffn2_norm"], w["gu2t"], w["d2"])

    def loss_fn(yv, tv):
        e = yv - tv
        return e * (1.0 / d), _colsum(e * e)
    dy, loss_vec = _rw("loss", loss_fn, [row(x3, d), row(tgt, d)], [((d,), F32)], accs=[(1, d)])

    gp = {}
    dx2, gp["ffn2_norm"] = ffn_bwd(
        "ffn2", x2, p["ffn2_norm"], w["gu2t"], w["d2"], ffn2_saved, dy, "gu2t", "d2")
    dmg = _mm("d_merge", dx2, w["out"], "nt", ACT_DTYPE)
    gw_["out"] = _mm("dw_out", mg, dx2, "tn", gdt)

    def merge_bwd(gav, gsv, apv, spv, dv):
        _, vjp = jax.vjp(_merge, gav, gsv, apv, spv)
        return vjp(dv.astype(F32))
    dga, dgs, dap, dsp = _rw("d_merge_gate", merge_bwd, merge_ins + [row(dmg, hw)],
                             [((hw,), ACT_DTYPE)] * 4, ncb=2)
    gw_["abt"] = _mm("dw_ab", dap, ao, "tn", gdt)
    dao = _mm("d_attn_o", dap, w["abt"], "nn", F32)
    gw_["sb"] = _mm("dw_sb", yn, dsp, "tn", gdt)
    dyn = _mm("d_ssd_y", dsp, w["sb"], "nt", F32)

    def gate_bwd(yv, xv, zv, dev, gv, dv):
        _, vjp = jax.vjp(_gate, yv, xv, zv, dev, gv)
        return vjp(dv)
    dyssd, dxs_gate, dz, ddexp, gp["ssd_norm"] = _rw(
        "d_ssd_gate", gate_bwd, gate_ins + [row(dyn, gw)],
        [((gw,), F32), ((gw,), F32), ((gw,), ACT_DTYPE)], accs=[(1, gw), (1, gw)], ncb=SSD_G)
    gp["d_skip"] = ddexp.reshape(nh, SSD_P).sum(axis=1).reshape(1, nh)

    dxbc, ddtraw, dbias, dalog = _ssd_bwd("ssd_bwd", xbc, dtraw, bias_p, alog_p, states,
                                          dyssd, dxs_gate)
    gp["dt_bias"], gp["a_log"] = dbias[:, :nh], dalog[:, :nh]
    du, gp["conv_w"], gp["conv_b"] = _conv_bwd("conv_bwd", proj, xbc_off // LANE,
                                               p["conv_w"], p["conv_b"], dxbc)

    def mix_bwd(*v):
        _, vjp = jax.vjp(_mix, *v[:6])
        return vjp(v[6])
    dmix = _rw("d_attn_mix", mix_bwd, [row(a, aw) for a in attn_o + attn_l] + [row(dao, aw)],
               [((aw,), F32)] * 6)
    dq, dk, dv = [], [], []
    dqg = dkg = None
    for gi, (_, dil) in enumerate(PATTERNS):
        r = _attn_bwd(f"attn_bwd{gi}", proj, attn_bases[gi], qg2, kg2, coefs[gi], dil,
                      dmix[gi], dmix[3 + gi])
        dq.append(r[0])
        dk.append(r[1])
        dv.append(r[2])
        dqg = r[3] if dqg is None else dqg + r[3]
        dkg = r[4] if dkg is None else dkg + r[4]
    gp["q_norm"] = dqg[:, :HD] + dqg[:, HD:]
    gp["k_norm"] = dkg[:, :HD] + dkg[:, HD:]

    segs = dq + dk + dv + [dz, du, dga, dgs]
    gw_["maint"] = _mm("dw_in", segs, h2, "tn", gdt)
    gw_["dtt"] = _mm("dw_dt", ddtraw, h2, "tn", gdt)
    dh2 = _mm("d_h2_main", segs, w["maint"], "nn", F32)
    dh2 = _mm("d_h2_dt", ddtraw, w["dtt"], "nn", F32, res=dh2)
    dx1, gp["mix_norm"] = rms_bwd("d_mix_norm", x1, p["mix_norm"], dh2, dx2)
    dx0, gp["ffn1_norm"] = ffn_bwd(
        "ffn1", x, p["ffn1_norm"], w["gu1t"], w["d1"], ffn1_saved, dx1, "gu1t", "d1")
    return loss_vec, dx0, gw_, gp


MESH = pl.DeviceIdType.MESH
HBM_SPEC = pl.BlockSpec(memory_space=pltpu.HBM)


def _mesh_pos():
    return lax.axis_index("x"), lax.axis_index("y"), lax.axis_index("c")


def _flip(pos, k):
    x, y, c = pos
    return (1 - x if k & 4 else x, 1 - y if k & 2 else y, 1 - c if k & 1 else c)


def _dev_index(pos):
    return 4 * pos[0] + 2 * pos[1] + pos[2]


def _rows_of(ref, base, stride, rows, pos):
    start = pl.multiple_of(base + stride * _dev_index(pos), ROW_ALIGN)
    return ref.at[pl.ds(start, rows)]


def _gather(name, shards, dests, out_shapes):
    n = len(shards)
    n_out = len(out_shapes)

    def body(*refs):
        x_refs = refs[:n]
        o_refs = refs[n:n + n_out]
        send_sems, recv_sems, local_sems = refs[n + n_out:]
        me = _mesh_pos()
        sibling = _flip(me, 1)
        chips = [_flip(me, 4), _flip(me, 2), _flip(me, 6)]

        def slot(i, block):
            k_out, base, stride = dests[i]
            return _rows_of(o_refs[k_out], base, stride, shards[i].shape[0], block)

        def copy(i, k, block, to, src=None):
            dst = slot(i, block)
            return pltpu.make_async_remote_copy(
                src_ref=dst if src is None else src, dst_ref=dst,
                send_sem=send_sems.at[7 * i + k], recv_sem=recv_sems.at[7 * i + k],
                device_id=to, device_id_type=MESH)

        mine = [pltpu.make_async_copy(x_refs[i], slot(i, me), local_sems.at[i]) for i in range(n)]
        for cp in mine:
            cp.start()
        first = []
        for i in range(n):
            first.append(copy(i, 0, me, sibling, src=x_refs[i]))
            first += [copy(i, 1 + j, me, chip, src=x_refs[i]) for j, chip in enumerate(chips)]
        for cp in first:
            cp.start()
        passed = []
        for j, chip in enumerate(chips):
            for i in range(n):
                copy(i, 1 + j, chip, me).wait_recv()
                fwd = copy(i, 4 + j, chip, sibling)
                fwd.start()
                passed.append(fwd)
        for i in range(n):
            copy(i, 0, sibling, me).wait_recv()
            for j, chip in enumerate(chips):
                copy(i, 4 + j, _flip(chip, 1), me).wait_recv()
        for cp in first + passed:
            cp.wait_send()
        for cp in mine:
            cp.wait()

    return pl.pallas_call(
        body, name=name,
        out_shape=[jax.ShapeDtypeStruct(s, dt) for s, dt in out_shapes],
        in_specs=[HBM_SPEC] * n, out_specs=[HBM_SPEC] * n_out,
        scratch_shapes=[pltpu.SemaphoreType.DMA((7 * n,)), pltpu.SemaphoreType.DMA((7 * n,)),
                        pltpu.SemaphoreType.DMA((n,))],
    )(*shards)


def _exchange(name, grads, srcs, small):
    n = len(srcs)
    ng = len(grads)

    def body(*refs):
        g_refs = refs[:ng]
        m_ref = refs[ng]
        r_refs = refs[ng + 1:ng + 1 + n]
        s_ref = refs[ng + 1 + n]
        send_sems, recv_sems, local_sems = refs[ng + 2 + n:]
        me = _mesh_pos()
        my = _dev_index(me)

        def slab(i, pos):
            gi, base, stride, rows = srcs[i]
            return _rows_of(g_refs[gi], base, stride, rows, pos)

        own = [pltpu.make_async_copy(slab(i, me), r_refs[i].at[my], local_sems.at[i])
               for i in range(n)]
        own.append(pltpu.make_async_copy(m_ref, s_ref.at[my], local_sems.at[n]))
        for cp in own:
            cp.start()

        def copies(k, src_pos, slot_pos):
            peer = _flip(me, k)
            si = _dev_index(slot_pos)
            out = [pltpu.make_async_remote_copy(
                src_ref=slab(i, src_pos), dst_ref=r_refs[i].at[si],
                send_sem=send_sems.at[7 * i + k - 1], recv_sem=recv_sems.at[7 * i + k - 1],
                device_id=peer, device_id_type=MESH) for i in range(n)]
            out.append(pltpu.make_async_remote_copy(
                src_ref=m_ref, dst_ref=s_ref.at[si],
                send_sem=send_sems.at[7 * n + k - 1], recv_sem=recv_sems.at[7 * n + k - 1],
                device_id=peer, device_id_type=MESH))
            return out

        sent = [cp for k in range(1, NDEV) for cp in copies(k, _flip(me, k), me)]
        for cp in sent:
            cp.start()
        for k in range(1, NDEV):
            for cp in copies(k, me, _flip(me, k)):
                cp.wait_recv()
        for cp in sent:
            cp.wait_send()
        for cp in own:
            cp.wait()

    out_shape = [jax.ShapeDtypeStruct((NDEV, rows, grads[gi].shape[1]), grads[gi].dtype)
                 for gi, _, _, rows in srcs]
    out_shape.append(jax.ShapeDtypeStruct((NDEV,) + small.shape, small.dtype))
    return pl.pallas_call(
        body, name=name, out_shape=out_shape,
        in_specs=[HBM_SPEC] * (ng + 1), out_specs=[HBM_SPEC] * (n + 1),
        scratch_shapes=[pltpu.SemaphoreType.DMA((7 * (n + 1),)),
                        pltpu.SemaphoreType.DMA((7 * (n + 1),)),
                        pltpu.SemaphoreType.DMA((n + 1,))],
    )(*grads, small)


SEM_SPEC = pl.BlockSpec(memory_space=pltpu.SEMAPHORE)
SIDE_EFFECT = pltpu.SideEffectType.DATAFLOW_SIDE_EFFECTING


def _split_refs(plan, srcs, lands, i, src_for, land_from):
    si, sbase, sstride, li, lbase, lstride, rows = plan[i]
    return (_rows_of(srcs[si], sbase, sstride, rows, src_for),
            _rows_of(lands[li], lbase, lstride, rows, land_from))


def _split_start(name, srcs, lands, plan, after=()):
    ns, nl, n = len(srcs), len(lands), len(plan)

    def body(*refs):
        s_refs = refs[:ns]
        l_refs = refs[ns:ns + nl]
        send_sems, recv_sems = refs[ns + nl + len(after):ns + nl + len(after) + 2]
        token = refs[ns + nl + len(after) + 2 + ns + nl]
        local_sems = refs[-1]
        me = _mesh_pos()
        own = []
        for i in range(n):
            src, dst = _split_refs(plan, s_refs, l_refs, i, me, me)
            own.append(pltpu.make_async_copy(src, dst, local_sems.at[i]))
            own[-1].start()
        for cp in own:
            cp.wait()
        for k in range(1, NDEV):
            peer = _flip(me, k)
            for i in range(n):
                src, dst = _split_refs(plan, s_refs, l_refs, i, peer, me)
                pltpu.make_async_remote_copy(
                    src_ref=src, dst_ref=dst,
                    send_sem=send_sems.at[7 * i + k - 1], recv_sem=recv_sems.at[7 * i + k - 1],
                    device_id=peer, device_id_type=MESH).start()
        token[...] = jnp.zeros_like(token)

    hbm = lambda a: pltpu.HBM(a.shape, a.dtype)
    out_shape = ((pltpu.SemaphoreType.DMA((7 * n,)), pltpu.SemaphoreType.DMA((7 * n,)))
                 + tuple(hbm(a) for a in srcs) + tuple(hbm(a) for a in lands)
                 + (jax.ShapeDtypeStruct((8, LANE), F32),))
    out = pl.pallas_call(
        body, name=name, out_shape=out_shape,
        in_specs=[HBM_SPEC] * (ns + nl) + [ANY_SPEC] * len(after),
        out_specs=(SEM_SPEC, SEM_SPEC) + (HBM_SPEC,) * (ns + nl)
        + (pl.BlockSpec(memory_space=pltpu.VMEM),),
        input_output_aliases={i: 2 + i for i in range(ns + nl)},
        scratch_shapes=[pltpu.SemaphoreType.DMA((n,))],
        compiler_params=pltpu.CompilerParams(has_side_effects=SIDE_EFFECT),
    )(*[pltpu.with_memory_space_constraint(a, pltpu.HBM) for a in tuple(srcs) + tuple(lands)],
      *after)
    _Order.tokens.append(out[-1])
    return out[0], out[1], out[2:2 + ns], out[2 + ns:2 + ns + nl]


def _split_wait(name, started, plan):
    send_sems, recv_sems, srcs, lands = started
    ns, nl, n = len(srcs), len(lands), len(plan)
    after = [_Order.last] if _Order.last is not None else []

    def body(*refs):
        s_refs = refs[:ns]
        l_refs = refs[ns:ns + nl]
        send_sems, recv_sems = refs[ns + nl:ns + nl + 2]
        me = _mesh_pos()
        for k in range(1, NDEV):
            peer = _flip(me, k)
            for i in range(n):
                src, dst = _split_refs(plan, s_refs, l_refs, i, peer, peer)
                cp = pltpu.make_async_remote_copy(
                    src_ref=src, dst_ref=dst,
                    send_sem=send_sems.at[7 * i + k - 1], recv_sem=recv_sems.at[7 * i + k - 1],
                    device_id=peer, device_id_type=MESH)
                cp.wait_send()
                cp.wait_recv()

    hbm = lambda a: pltpu.HBM(a.shape, a.dtype)
    out = pl.pallas_call(
        body, name=name,
        out_shape=tuple(hbm(a) for a in srcs) + tuple(hbm(a) for a in lands),
        in_specs=[HBM_SPEC] * (ns + nl) + [SEM_SPEC, SEM_SPEC] + [ANY_SPEC] * len(after),
        out_specs=(HBM_SPEC,) * (ns + nl),
        input_output_aliases={i: i for i in range(ns + nl)},
        compiler_params=pltpu.CompilerParams(has_side_effects=SIDE_EFFECT),
    )(*srcs, *lands, send_sems, recv_sems, *after)
    return list(out[ns:])


def _sum_slabs(name, a):
    s, r, c = a.shape

    def body(a_ref, o_ref):
        acc = a_ref[0].astype(F32)
        for i in range(1, s):
            acc = acc + a_ref[i].astype(F32)
        o_ref[...] = acc

    return pl.pallas_call(body, name=name, out_shape=jax.ShapeDtypeStruct((r, c), F32))(a)


def _adamw_update(g, w, m, v):
    mn = ADAM_B1 * m + (1.0 - ADAM_B1) * g
    vn = ADAM_B2 * v + (1.0 - ADAM_B2) * (g * g)
    m_hat = mn / (1.0 - ADAM_B1 ** ADAM_STEP)
    v_hat = vn / (1.0 - ADAM_B2 ** ADAM_STEP)
    delta = -ADAM_LR * (m_hat / (jnp.sqrt(v_hat) + ADAM_EPS) + ADAM_WD * w)
    return delta, mn, vn


def _adamw(name, gsrc, w, m, v, transposed=False, tr=256):
    s = gsrc.shape[0]
    r, c = w.shape
    step = LANE if transposed else 8
    tr = max(t for t in range(step, min(tr, r) + 1, step) if r % t == 0)

    def body(g_ref, w_ref, m_ref, v_ref, go_ref, d_ref, mo_ref, vo_ref):
        g = g_ref[0].astype(F32)
        for i in range(1, s):
            g = g + g_ref[i].astype(F32)
        if transposed:
            g = g.T[:, :c]
        delta, mn, vn = _adamw_update(g, w_ref[...], m_ref[...], v_ref[...])
        go_ref[...] = g
        d_ref[...] = delta
        mo_ref[...] = mn
        vo_ref[...] = vn

    blk = pl.BlockSpec((tr, c), lambda i: (i, 0))
    if transposed:
        g_spec = pl.BlockSpec((s, gsrc.shape[1], tr), lambda i: (0, 0, i))
    else:
        g_spec = pl.BlockSpec((s, tr, c), lambda i: (0, i, 0))
    return pl.pallas_call(
        body, name=name, out_shape=[jax.ShapeDtypeStruct((r, c), F32)] * 4,
        grid=(r // tr,),
        in_specs=[g_spec, blk, blk, blk], out_specs=[blk] * 4,
        compiler_params=_params(("parallel",)),
    )(gsrc, w, m, v)


REPLICATED = ("ffn1_norm", "mix_norm", "q_norm", "k_norm", "conv_b", "dt_bias", "a_log",
              "d_skip", "ssd_norm", "ffn2_norm")
ALL_WEIGHTS = ("ffn1_norm", "ffn1_w_gate", "ffn1_w_up", "ffn1_w_down", "mix_norm", "w_in",
               "q_norm", "k_norm", "conv_w", "conv_b", "dt_bias", "a_log", "d_skip", "ssd_norm",
               "w_attn_branch", "w_ssd_branch", "w_out", "ffn2_norm", "ffn2_w_gate", "ffn2_w_up",
               "ffn2_w_down")
BIG = (("ffn1_w_gate", True, "gu1t", 0), ("ffn1_w_up", True, "gu1t", 1),
       ("ffn1_w_down", False, "d1", 0), ("w_in", True, "wint", 0),
       ("w_attn_branch", True, "abt", 0), ("w_ssd_branch", False, "sb", 0),
       ("w_out", False, "out", 0),
       ("ffn2_w_gate", True, "gu2t", 0), ("ffn2_w_up", True, "gu2t", 1),
       ("ffn2_w_down", False, "d2", 0))


def _nrows(shape, cols):
    return -(-math.prod(shape) // cols)


def _pack_rows(arrs, cols, row_tile):
    parts = []
    for a in arrs:
        flat = a.reshape(-1)
        nr = -(-flat.shape[0] // cols)
        parts.append(jnp.pad(flat, (0, nr * cols - flat.shape[0])).reshape(nr, cols))
    out = jnp.concatenate(parts, axis=0)
    return jnp.pad(out, ((0, _round_up(out.shape[0], row_tile) - out.shape[0]), (0, 0)))


def _unpack_rows(packed, shapes):
    cols = packed.shape[-1]
    out, r0 = [], 0
    for sh in shapes:
        nr = _nrows(sh, cols)
        out.append(packed[r0:r0 + nr].reshape(-1)[:math.prod(sh)].reshape(tuple(sh)))
        r0 += nr
    return out


def kernel(x, ffn1_norm, ffn1_w_gate, ffn1_w_up, ffn1_w_down, mix_norm, w_in, q_norm, k_norm, conv_w, conv_b, dt_bias, a_log, d_skip, ssd_norm, w_attn_branch, w_ssd_branch, w_out, ffn2_norm, ffn2_w_gate, ffn2_w_up, ffn2_w_down, loss_target, m_ffn1_norm, m_ffn1_w_gate, m_ffn1_w_up, m_ffn1_w_down, m_mix_norm, m_w_in, m_q_norm, m_k_norm, m_conv_w, m_conv_b, m_dt_bias, m_a_log, m_d_skip, m_ssd_norm, m_w_attn_branch, m_w_ssd_branch, m_w_out, m_ffn2_norm, m_ffn2_w_gate, m_ffn2_w_up, m_ffn2_w_down, v_ffn1_norm, v_ffn1_w_gate, v_ffn1_w_up, v_ffn1_w_down, v_mix_norm, v_w_in, v_q_norm, v_k_norm, v_conv_w, v_conv_b, v_dt_bias, v_a_log, v_d_skip, v_ssd_norm, v_w_attn_branch, v_w_ssd_branch, v_w_out, v_ffn2_norm, v_ffn2_w_gate, v_ffn2_w_up, v_ffn2_w_down):
    given = dict(locals())
    wts = {n: given[n] for n in ALL_WEIGHTS}
    mom = {n: given["m_" + n] for n in ALL_WEIGHTS}
    var = {n: given["v_" + n] for n in ALL_WEIGHTS}
    d = x.shape[-1]
    nh = dt_bias.shape[1]
    my = _dev_index(_mesh_pos())

    def row_form(n, col_sharded):
        a = wts[n][0].T if col_sharded else wts[n][0]
        return jnp.pad(a, ((0, _round_up(a.shape[0], ROW_ALIGN) - a.shape[0]), (0, 0)))

    _Order.tokens, _Order.last = [], None
    shard = {n: row_form(n, cs).astype(MXU_DTYPE) for n, cs, _, _ in BIG}
    entries = {buf: [e for e in BIG if e[2] == buf] for buf in dict.fromkeys(e[2] for e in BIG)}

    def buf_shape(buf):
        r, c = shard[entries[buf][0][0]].shape
        return (len(entries[buf]) * NDEV * r, c)

    def gather_plan(bufs):
        srcs, lands, plan = [], [], []
        for li, buf in enumerate(bufs):
            lands.append(lax.empty(buf_shape(buf), MXU_DTYPE))
            for n, _, _, pos in entries[buf]:
                r = shard[n].shape[0]
                plan.append((len(srcs), 0, 0, li, pos * NDEV * r, r, r))
                srcs.append(shard[n])
        return srcs, lands, plan

    def scatter_plan(bufs, grads):
        srcs, lands, plan, names = [], [], [], []
        for si, buf in enumerate(bufs):
            srcs.append(grads[buf])
            for n, _, _, pos in entries[buf]:
                r, c = shard[n].shape
                plan.append((si, pos * NDEV * r, r, len(lands), 0, r, r))
                lands.append(lax.empty((NDEV * r, c), MXU_DTYPE))
                names.append(n)
        return srcs, lands, plan, names

    first_bufs = ("gu1t", "d1")
    shards, dests, out_shapes = [], [], []
    for bi, buf in enumerate(first_bufs):
        out_shapes.append((buf_shape(buf), MXU_DTYPE))
        for n, _, _, pos in entries[buf]:
            r = shard[n].shape[0]
            shards.append(shard[n])
            dests.append((bi, pos * NDEV * r, r))
    conv_rows = _pack_rows([conv_w[0]], LANE, ROW_ALIGN)
    shards.append(conv_rows)
    dests.append((len(first_bufs), 0, conv_rows.shape[0]))
    out_shapes.append(((NDEV * conv_rows.shape[0], LANE), F32))
    gathered = _gather("gather_first", shards, dests, out_shapes)

    in_cols = w_in.shape[2]
    in_pad = _round_up(in_cols, ROW_ALIGN)
    dt_off = NDEV * in_cols - 2 * d - nh
    second_bufs = ("wint",)
    third_bufs = ("abt", "sb", "out", "gu2t", "d2")
    plan2 = gather_plan(second_bufs)
    started2 = _split_start("gather_in_start", *plan2, after=[gathered[0]])
    started3 = []

    class Weights(dict):
        def __missing__(self, key):
            if key in ("maint", "dtt"):
                wint = _split_wait("gather_in_wait", started2, plan2[2])[0]
                plan3 = gather_plan(third_bufs)
                started3.append((_split_start("gather_rest_start", *plan3, after=[wint]), plan3[2]))
                wint = wint.reshape(NDEV, in_pad, d)[:, :in_cols].reshape(NDEV * in_cols, d)
                self["maint"] = jnp.concatenate([wint[:dt_off], wint[dt_off + nh:]], axis=0)
                self["dtt"] = jnp.pad(wint[dt_off:dt_off + nh], ((0, LANE - nh), (0, 0)))
            else:
                st, plan = started3[0]
                for buf, a in zip(third_bufs, _split_wait("gather_rest_wait", st, plan)):
                    self[buf] = a
            return self[key]

    w = Weights(gu1t=gathered[0], d1=gathered[1])
    p = {n: wts[n] for n in REPLICATED}
    conv_all = gathered[-1].reshape(NDEV, conv_rows.shape[0] * LANE)[:, :math.prod(conv_w.shape[1:])]
    p["conv_w"] = (conv_all.reshape((NDEV,) + conv_w.shape[1:]).transpose(1, 0, 2)
                   .reshape(conv_w.shape[1], NDEV * conv_w.shape[2]))

    groups = (("scatter_late", ("gu2t", "d2", "out", "abt", "sb")),
              ("scatter_in", ("maint", "dtt")),
              ("scatter_first", ("gu1t", "d1")))
    in_flight = []

    class Grads(dict):
        def __setitem__(self, key, value):
            dict.__setitem__(self, key, value)
            for tag, need in groups:
                if key in need and all(k in self for k in need):
                    if tag == "scatter_in":
                        gwin = jnp.concatenate([self["maint"][:dt_off], self["dtt"][:nh],
                                                self["maint"][dt_off:]], axis=0)
                        gwin = jnp.pad(gwin.reshape(NDEV, in_cols, d),
                                       ((0, 0), (0, in_pad - in_cols), (0, 0)))
                        bufs, grads = ("wint",), {"wint": gwin.reshape(NDEV * in_pad, d)}
                    else:
                        bufs, grads = need, self
                    srcs, lands, plan, names = scatter_plan(bufs, grads)
                    in_flight.append((tag, _split_start(tag + "_start", srcs, lands, plan),
                                      plan, names))

    loss_vec, dx, gw, gp = _local_step(x[0], loss_target[0], w, p, Grads(),
                                       aw=w_attn_branch.shape[1])

    small_names = REPLICATED + ("conv_w",)
    small_shapes = [gp[n].shape for n in small_names]
    small = _pack_rows([gp[n] for n in small_names], LANE, 8)
    small_all = _exchange("exchange_small", [], [], small)[0]

    outs = [{}, {}, {}, {}]
    col_sharded_of = {n: cs for n, cs, _, _ in BIG}
    for tag, started, plan, names in in_flight:
        for n, rv in zip(names, _split_wait(tag + "_wait", started, plan)):
            rv = rv.reshape(NDEV, shard[n].shape[0], shard[n].shape[1])
            res = _Order.done(_adamw("adamw_" + n, rv, wts[n][0], mom[n][0], var[n][0],
                                     transposed=col_sharded_of[n]))
            for k in range(4):
                outs[k][n] = res[k][None]

    small_g = _unpack_rows(_sum_slabs("sum_small_grads", small_all), small_shapes)
    small_g = dict(zip(small_names, small_g))
    cs = conv_w.shape[2]
    small_g["conv_w"] = lax.dynamic_slice_in_dim(small_g["conv_w"], my * cs, cs, axis=1)
    small_shard_shapes = [wts[n].shape[-2:] for n in small_names]
    sg = _pack_rows([small_g[n] for n in small_names], LANE, 8)
    sw = _pack_rows([wts[n] for n in small_names], LANE, 8)
    sm = _pack_rows([mom[n] for n in small_names], LANE, 8)
    sv = _pack_rows([var[n] for n in small_names], LANE, 8)
    res_small = _adamw("adamw_small", sg[None], sw, sm, sv, tr=sg.shape[0])
    for k in range(4):
        for n, a in zip(small_names, _unpack_rows(res_small[k], small_shard_shapes)):
            outs[k][n] = a.reshape(wts[n].shape)

    loss = lax.psum(0.5 * jnp.sum(loss_vec) / d, ("x", "y", "c"))
    result = [loss, dx[None]]
    for k in range(4):
        result += [outs[k][n] for n in ALL_WEIGHTS]
    return tuple(result)
```

```python
import functools
import math

import numpy as np
import jax
import jax.numpy as jnp
from jax import lax
from jax.experimental import pallas as pl
from jax.experimental.pallas import tpu as pltpu

F32 = jnp.float32
BF16 = jnp.bfloat16
MXU_DTYPE = BF16
ACT_DTYPE = BF16

NDEV = 8
EPS = 1e-6
HD = 64
QB = 128
PATTERNS = ((128, 1), (512, 4), (2048, 16))
ALIBI_MAX_EXP = 8.0
SSD_P = 64
SSD_N = 128
SSD_G = 4
SSD_Q = 128
SSD_K = 4
NEG = -1e30
LANE = 128
ROW_ALIGN = 16
VMEM_LIMIT = 56 * 1024 * 1024

ADAM_LR, ADAM_B1, ADAM_B2, ADAM_EPS, ADAM_WD, ADAM_STEP = 0.001, 0.9, 0.999, 1e-8, 0.01, 10

NN = (((1,), (0,)), ((), ()))
NT = (((1,), (1,)), ((), ()))
TN = (((0,), (0,)), ((), ()))


def _dot(a, b, dims=NN):
    return lax.dot_general(a.astype(MXU_DTYPE), b.astype(MXU_DTYPE), dims,
                           preferred_element_type=F32)


def _split3(a):
    hi = a.astype(BF16)
    r = a - hi.astype(F32)
    mid = r.astype(BF16)
    lo = (r - mid.astype(F32)).astype(BF16)
    return hi, mid, lo


def _dot3(a, b, dims=NN, split=0):
    if split == 0:
        bb = b.astype(BF16)
        parts = [lax.dot_general(s, bb, dims, preferred_element_type=F32) for s in _split3(a)]
    else:
        aa = a.astype(BF16)
        parts = [lax.dot_general(aa, s, dims, preferred_element_type=F32) for s in _split3(b)]
    return parts[0] + parts[1] + parts[2]


@jax.custom_vjp
def _spread(v, e):
    return _dot3(v, e)


def _spread_fwd(v, e):
    return _dot3(v, e), e


def _spread_bwd(e, g):
    return _dot3(g, e, NT), jnp.zeros_like(e)


_spread.defvjp(_spread_fwd, _spread_bwd)


@jax.custom_vjp
def _running_sum(a, lower):
    return _dot3(lower, a, NN, split=1)


def _running_sum_fwd(a, lower):
    return _dot3(lower, a, NN, split=1), lower


def _running_sum_bwd(lower, g):
    return _dot3(lower, g, TN, split=1), jnp.zeros_like(lower)


_running_sum.defvjp(_running_sum_fwd, _running_sum_bwd)


def _tile(n, cap):
    if n <= cap:
        return n
    best = None
    for t in range(LANE, cap + 1, LANE):
        if n % t == 0:
            best = t
    assert best is not None, (n, cap)
    return best


def _params(sem):
    return pltpu.CompilerParams(dimension_semantics=sem, vmem_limit_bytes=VMEM_LIMIT)


def _round_up(n, m):
    return -(-n // m) * m


class _Order:
    tokens = []
    last = None

    @classmethod
    def take(cls):
        out, cls.tokens = cls.tokens, []
        return out

    @classmethod
    def done(cls, result):
        cls.last = result[0] if isinstance(result, (list, tuple)) else result
        return result


ANY_SPEC = pl.BlockSpec(memory_space=pl.ANY)


def _mm(name, a, b, mode, out_dtype=F32, res=None, scale=1.0,
        cap_m=1408, cap_n=1408, cap_k=1408):
    segs = list(a) if isinstance(a, (list, tuple)) else [a]
    nseg = len(segs)
    if mode == "tn":
        k = segs[0].shape[0]
        widths = [s.shape[1] for s in segs]
        m = sum(widths)
        k2, n = b.shape
        tm = _tile(math.gcd(*widths), cap_m)
        tk = _tile(k, cap_k)
        counts = [wd // tm for wd in widths]
    else:
        m = segs[0].shape[0]
        widths = [s.shape[1] for s in segs]
        k = sum(widths)
        (k2, n) = b.shape if mode == "nn" else b.shape[::-1]
        tm = _tile(m, cap_m)
        tk = _tile(math.gcd(*widths), cap_k)
        counts = [wd // tk for wd in widths]
    assert k == k2, (name, [s.shape for s in segs], b.shape, mode)
    tn = _tile(n, cap_n)
    nk = k // tk
    starts = [sum(counts[:s]) for s in range(nseg)]
    dims = {"nn": NN, "nt": NT, "tn": TN}[mode]

    def a_spec(s):
        lo, cnt = starts[s], counts[s]
        if mode == "tn":
            if nseg == 1:
                return pl.BlockSpec((tk, tm), lambda i, j, kk: (kk, i))
            return pl.BlockSpec(
                (tk, tm), lambda i, j, kk: (jnp.where((i >= lo) & (i < lo + cnt), kk, 0),
                                            jnp.clip(i - lo, 0, cnt - 1)))
        if nseg == 1:
            return pl.BlockSpec((tm, tk), lambda i, j, kk: (i, kk))
        return pl.BlockSpec((tm, tk), lambda i, j, kk: (i, jnp.clip(kk - lo, 0, cnt - 1)))

    b_spec = (pl.BlockSpec((tn, tk), lambda i, j, kk: (j, kk)) if mode == "nt"
              else pl.BlockSpec((tk, tn), lambda i, j, kk: (kk, j)))
    o_spec = pl.BlockSpec((tm, tn), lambda i, j, kk: (i, j))
    has_res = res is not None
    use_acc = nk > 1 or nseg > 1
    ties = _Order.take()
    nt_ = len(ties)

    def body(*refs):
        a_refs = refs[:nseg]
        b_ref = refs[nseg]
        r_ref = refs[nseg + 1] if has_res else None
        o_ref = refs[nseg + 1 + has_res + nt_]
        scr = refs[nseg + 2 + has_res + nt_:]

        def finish(acc):
            if scale != 1.0:
                acc = acc * scale
            if has_res:
                acc = r_ref[...].astype(F32) + acc
            o_ref[...] = acc.astype(o_ref.dtype)

        if not use_acc:
            finish(_dot(a_refs[0][...], b_ref[...], dims))
            return
        acc_ref = scr[0]
        kk = pl.program_id(2)
        sel = pl.program_id(0) if mode == "tn" else kk

        @pl.when(kk == 0)
        def _():
            acc_ref[...] = jnp.zeros_like(acc_ref)

        for s in range(nseg):
            def add(s=s):
                acc_ref[...] += _dot(a_refs[s][...], b_ref[...], dims)
            if nseg == 1:
                add()
            else:
                pl.when((sel >= starts[s]) & (sel < starts[s] + counts[s]))(add)

        @pl.when(kk == nk - 1)
        def _():
            finish(acc_ref[...])

    in_specs = ([a_spec(s) for s in range(nseg)] + [b_spec] + ([o_spec] if has_res else [])
                + [ANY_SPEC] * nt_)
    args = tuple(segs) + (b,) + ((res,) if has_res else ()) + tuple(ties)
    return _Order.done(pl.pallas_call(
        body, name=name,
        out_shape=jax.ShapeDtypeStruct((m, n), out_dtype),
        grid=(m // tm, n // tn, nk),
        in_specs=in_specs, out_specs=o_spec,
        scratch_shapes=[pltpu.VMEM((tm, tn), F32)] if use_acc else [],
        compiler_params=_params(("parallel", "parallel", "arbitrary")),
    )(*args))


def _rw(name, fn, ins, outs, accs=(), tr=256, ncb=1):
    t = next(a.shape[0] for kind, a, _, _ in ins if kind == "row")
    assert t % tr == 0
    n_in = len(ins)
    n_pieces = sum(len(w) for w, _ in outs)

    def spec(kind, arr, width, base):
        if kind == "row":
            return pl.BlockSpec((tr, width), lambda j, i: (i, base + j))
        return pl.BlockSpec((arr.shape[0], width), lambda j, i: (0, base + j))

    in_specs = [spec(*s) for s in ins]
    out_shapes, out_specs = [], []
    for widths, dt in outs:
        w = sum(widths)
        out_shapes.append(jax.ShapeDtypeStruct((t, w * ncb), dt))
        out_specs.append(pl.BlockSpec((tr, w), lambda j, i: (i, j)))
    for rows, width in accs:
        out_shapes.append(jax.ShapeDtypeStruct((rows, width * ncb), F32))
        out_specs.append(pl.BlockSpec((rows, width), lambda j, i: (0, j)))

    ties = _Order.take()
    nt_ = len(ties)
    in_specs = in_specs + [ANY_SPEC] * nt_

    def body(*refs):
        vals = [r[...] for r in refs[:n_in]]
        res = fn(*vals)
        o_refs = refs[n_in + nt_:n_in + nt_ + len(outs)]
        a_refs = refs[n_in + nt_ + len(outs):]
        p = 0
        for (widths, _), o_ref in zip(outs, o_refs):
            off = 0
            for w in widths:
                if len(widths) == 1:
                    o_ref[...] = res[p].astype(o_ref.dtype)
                else:
                    o_ref[:, off:off + w] = res[p].astype(o_ref.dtype)
                off += w
                p += 1
        i = pl.program_id(1)
        for a_ref, v in zip(a_refs, res[n_pieces:]):
            @pl.when(i == 0)
            def _(a_ref=a_ref, v=v):
                a_ref[...] = v

            @pl.when(i > 0)
            def _(a_ref=a_ref, v=v):
                a_ref[...] += v

    return _Order.done(pl.pallas_call(
        body, name=name, out_shape=out_shapes,
        grid=(ncb, t // tr), in_specs=in_specs, out_specs=out_specs,
        compiler_params=_params(("parallel", "arbitrary")),
    )(*[a for _, a, _, _ in ins], *ties))


def _rms(x, g):
    x = x.astype(F32)
    return x * lax.rsqrt(jnp.mean(x * x, axis=-1, keepdims=True) + EPS) * g


def _silu(x):
    return x * jax.nn.sigmoid(x)


def _colsum(v):
    return jnp.sum(v, axis=0, keepdims=True)


def _pair_norm(x, g):
    w = 2 * HD
    ri = lax.broadcasted_iota(jnp.int32, (w, w), 0)
    ci = lax.broadcasted_iota(jnp.int32, (w, w), 1)
    same_head = ((ri < HD) == (ci < HD)).astype(F32)
    ms = _spread(x * x, same_head) * (1.0 / HD)
    return x * lax.rsqrt(ms + EPS) * g


def _attn_pair(qn, kcn, kpn, vc, vp, coef0, coef1, first):
    w = 2 * HD
    lane = lax.broadcasted_iota(jnp.int32, (1, w), 1)
    scale = 1.0 / math.sqrt(HD)
    a_idx = lax.broadcasted_iota(jnp.int32, (QB, QB), 0)
    c_idx = lax.broadcasted_iota(jnp.int32, (QB, QB), 1)
    rel_c = a_idx - c_idx
    dist_c = rel_c.astype(F32)
    dist_p = dist_c + float(QB)
    keep_c = rel_c >= 0
    keep_p = jnp.logical_and(rel_c <= 0, jnp.logical_not(first))
    out = jnp.zeros((QB, w), F32)
    lb = jnp.zeros((QB, w), F32)
    for hh, coef in enumerate((coef0, coef1)):
        mask = ((lane < HD) if hh == 0 else (lane >= HD)).astype(F32)
        qm = qn * mask
        lc = jnp.where(keep_c, _dot(qm, kcn, NT) * scale - coef * dist_c, NEG)
        lp = jnp.where(keep_p, _dot(qm, kpn, NT) * scale - coef * dist_p, NEG)
        m = lax.stop_gradient(jnp.maximum(jnp.max(lc, axis=-1, keepdims=True),
                                          jnp.max(lp, axis=-1, keepdims=True)))
        pc = jnp.exp(lc - m)
        pp = jnp.exp(lp - m)
        l = jnp.sum(pc, axis=-1, keepdims=True) + jnp.sum(pp, axis=-1, keepdims=True)
        inv = 1.0 / l
        out = out + (_dot(pc * inv, vc) + _dot(pp * inv, vp)) * mask
        lb = lb + (m + jnp.log(l)) * mask
    return out, lb


NORM_ROWS = 128
ATTN_UNROLL_FWD = 4
ATTN_UNROLL_BWD = 2


def _unit_rows(u, d):
    r = u & (d - 1)
    n = u >> (d.bit_length() - 1)

    def rows(blk):
        start = pl.multiple_of(blk * (QB * d), QB * d)
        return pl.ds(start, QB) if d == 1 else pl.ds(start + r, QB, stride=d)

    return rows(n), rows(jnp.maximum(n - 1, 0)), n == 0


def _attn_prologue(t, q_ref, k_ref, v_ref, qg_ref, kg_ref, qf, kf, vf):
    def chunk(c, carry):
        rows = pl.ds(pl.multiple_of(c * NORM_ROWS, NORM_ROWS), NORM_ROWS)
        qf[rows, :] = _pair_norm(q_ref[rows, :].astype(F32), qg_ref[...])
        kf[rows, :] = _pair_norm(k_ref[rows, :].astype(F32), kg_ref[...])
        vf[rows, :] = v_ref[rows, :].astype(F32)
        return carry
    lax.fori_loop(0, t // NORM_ROWS, chunk, 0)


def _attn_specs(t, bases):
    w = 2 * HD
    ins = [pl.BlockSpec((t, w), functools.partial(lambda p, c, b: (0, b + p), b=b)) for b in bases]
    gain = pl.BlockSpec((1, w), lambda p, c: (0, 0))
    blk = pl.BlockSpec((t, w), lambda p, c: (0, p))
    return ins, gain, blk


def _attn_fwd(name, proj, bases, qg, kg, coefs, d):
    t = proj.shape[0]
    npairs = coefs.shape[0] // 2
    w = 2 * HD
    ins, gain, blk = _attn_specs(t, bases)

    def body(coef_ref, q_ref, k_ref, v_ref, qg_ref, kg_ref, o_ref, l_ref, qf, kf, vf):
        p = pl.program_id(0)
        c0, c1 = coef_ref[2 * p], coef_ref[2 * p + 1]
        _attn_prologue(t, q_ref, k_ref, v_ref, qg_ref, kg_ref, qf, kf, vf)

        def step(u, carry):
            cur, prv, first = _unit_rows(u, d)
            o, lb = _attn_pair(qf[cur, :], kf[cur, :], kf[prv, :], vf[cur, :], vf[prv, :],
                               c0, c1, first)
            o_ref[cur, :] = o
            l_ref[cur, :] = lb
            return carry

        lax.fori_loop(0, t // QB, step, 0, unroll=ATTN_UNROLL_FWD)

    return pl.pallas_call(
        body, name=name,
        out_shape=[jax.ShapeDtypeStruct((t, npairs * w), F32)] * 2,
        grid_spec=pltpu.PrefetchScalarGridSpec(
            num_scalar_prefetch=1, grid=(npairs,),
            in_specs=ins + [gain, gain], out_specs=[blk, blk],
            scratch_shapes=[pltpu.VMEM((t, w), F32)] * 3),
        compiler_params=_params(("arbitrary",)),
    )(coefs, proj, proj, proj, qg, kg)


def _attn_bwd(name, proj, bases, qg, kg, coefs, d, do, dl):
    t = proj.shape[0]
    npairs = coefs.shape[0] // 2
    w = 2 * HD
    ins, gain, blk = _attn_specs(t, bases)

    def body(coef_ref, q_ref, k_ref, v_ref, qg_ref, kg_ref, do_ref, dl_ref,
             dq_ref, dk_ref, dv_ref, dqg_ref, dkg_ref, qf, kf, vf, dqf, dkf, dvf):
        p = pl.program_id(0)
        c0, c1 = coef_ref[2 * p], coef_ref[2 * p + 1]
        _attn_prologue(t, q_ref, k_ref, v_ref, qg_ref, kg_ref, qf, kf, vf)
        dkf[...] = jnp.zeros_like(dkf)
        dvf[...] = jnp.zeros_like(dvf)

        def step(u, carry):
            cur, prv, first = _unit_rows(u, d)
            f = lambda a, b, c, e, g: _attn_pair(a, b, c, e, g, c0, c1, first)
            _, vjp = jax.vjp(f, qf[cur, :], kf[cur, :], kf[prv, :], vf[cur, :], vf[prv, :])
            dq, dkc, dkp, dvc, dvp = vjp((do_ref[cur, :], dl_ref[cur, :]))
            dqf[cur, :] = dq
            dkf[cur, :] += dkc
            dkf[prv, :] += dkp
            dvf[cur, :] += dvc
            dvf[prv, :] += dvp
            return carry

        lax.fori_loop(0, t // QB, step, 0, unroll=ATTN_UNROLL_BWD)

        def chunk(c, carry):
            dqg_acc, dkg_acc = carry
            rows = pl.ds(pl.multiple_of(c * NORM_ROWS, NORM_ROWS), NORM_ROWS)
            _, vq = jax.vjp(_pair_norm, q_ref[rows, :].astype(F32), qg_ref[...])
            dq, dqg = vq(dqf[rows, :])
            _, vk = jax.vjp(_pair_norm, k_ref[rows, :].astype(F32), kg_ref[...])
            dk, dkg = vk(dkf[rows, :])
            dq_ref[rows, :] = dq.astype(dq_ref.dtype)
            dk_ref[rows, :] = dk.astype(dk_ref.dtype)
            dv_ref[rows, :] = dvf[rows, :].astype(dv_ref.dtype)
            return dqg_acc + dqg, dkg_acc + dkg

        zero = jnp.zeros((1, w), F32)
        dqg, dkg = lax.fori_loop(0, t // NORM_ROWS, chunk, (zero, zero))

        @pl.when(p == 0)
        def _():
            dqg_ref[...] = dqg
            dkg_ref[...] = dkg

        @pl.when(p > 0)
        def _():
            dqg_ref[...] += dqg
            dkg_ref[...] += dkg

    big = jax.ShapeDtypeStruct((t, npairs * w), ACT_DTYPE)
    small = jax.ShapeDtypeStruct((1, w), F32)
    return pl.pallas_call(
        body, name=name,
        out_shape=[big, big, big, small, small],
        grid_spec=pltpu.PrefetchScalarGridSpec(
            num_scalar_prefetch=1, grid=(npairs,),
            in_specs=ins + [gain, gain, blk, blk],
            out_specs=[blk, blk, blk, gain, gain],
            scratch_shapes=[pltpu.VMEM((t, w), F32)] * 6),
        compiler_params=_params(("arbitrary",)),
    )(coefs, proj, proj, proj, qg, kg, do, dl)


def _shift_down(u, s):
    if s == 0:
        return u
    rows = lax.broadcasted_iota(jnp.int32, u.shape, 0)
    return jnp.where(rows >= s, pltpu.roll(u, s, 0), 0.0)


def _shift_up(u, s):
    if s == 0:
        return u
    t = u.shape[0]
    rows = lax.broadcasted_iota(jnp.int32, u.shape, 0)
    return jnp.where(rows < t - s, pltpu.roll(u, t - s, 0), 0.0)


def _conv_pre(u, w, b):
    y = b
    for kk in range(SSD_K):
        y = y + w[kk:kk + 1, :] * _shift_down(u, SSD_K - 1 - kk)
    return y


def _conv_fwd(name, src, base, w, b, cw=128):
    t = src.shape[0]
    c = w.shape[1]

    def body(u_ref, w_ref, b_ref, o_ref):
        y = _conv_pre(u_ref[...].astype(F32), w_ref[...], b_ref[...])
        o_ref[...] = _silu(y).astype(o_ref.dtype)

    return pl.pallas_call(
        body, name=name, out_shape=jax.ShapeDtypeStruct((t, c), ACT_DTYPE),
        grid=(c // cw,),
        in_specs=[pl.BlockSpec((t, cw), lambda j: (0, base + j)),
                  pl.BlockSpec((SSD_K, cw), lambda j: (0, j)),
                  pl.BlockSpec((1, cw), lambda j: (0, j))],
        out_specs=pl.BlockSpec((t, cw), lambda j: (0, j)),
        compiler_params=_params(("parallel",)),
    )(src, w, b)


def _conv_bwd(name, src, base, w, b, dout, cw=128):
    t = src.shape[0]
    c = w.shape[1]

    def body(u_ref, w_ref, b_ref, d_ref, du_ref, dw_ref, db_ref):
        u = u_ref[...].astype(F32)
        wv = w_ref[...]
        y = _conv_pre(u, wv, b_ref[...])
        sg = jax.nn.sigmoid(y)
        dy = d_ref[...].astype(F32) * (sg * (1.0 + y * (1.0 - sg)))
        du = jnp.zeros_like(u)
        for kk in range(SSD_K):
            s = SSD_K - 1 - kk
            du = du + wv[kk:kk + 1, :] * _shift_up(dy, s)
            dw_ref[kk:kk + 1, :] = _colsum(dy * _shift_down(u, s))
        du_ref[...] = du.astype(du_ref.dtype)
        db_ref[...] = _colsum(dy)

    return pl.pallas_call(
        body, name=name,
        out_shape=[jax.ShapeDtypeStruct((t, c), ACT_DTYPE),
                   jax.ShapeDtypeStruct((SSD_K, c), F32),
                   jax.ShapeDtypeStruct((1, c), F32)],
        grid=(c // cw,),
        in_specs=[pl.BlockSpec((t, cw), lambda j: (0, base + j)),
                  pl.BlockSpec((SSD_K, cw), lambda j: (0, j)),
                  pl.BlockSpec((1, cw), lambda j: (0, j)),
                  pl.BlockSpec((t, cw), lambda j: (0, j))],
        out_specs=[pl.BlockSpec((t, cw), lambda j: (0, j)),
                   pl.BlockSpec((SSD_K, cw), lambda j: (0, j)),
                   pl.BlockSpec((1, cw), lambda j: (0, j))],
        compiler_params=_params(("parallel",)),
    )(src, w, b, dout)


def _softplus(x):
    return jnp.maximum(x, 0.0) + jnp.log(1.0 + jnp.exp(-jnp.abs(x)))


def _ssd_chunk(xbc, dtraw, bias, alog, states):
    wd = states[0].shape[1]
    nj = wd // SSD_P
    inner = SSD_G * wd
    dt = _softplus(dtraw + bias)
    a = dt * (-jnp.exp(alog))
    li = lax.broadcasted_iota(jnp.int32, (SSD_Q, SSD_Q), 0)
    si = lax.broadcasted_iota(jnp.int32, (SSD_Q, SSD_Q), 1)
    causal = li >= si
    acs = _running_sum(a, causal.astype(F32))
    acs_t = acs.T
    a_last = acs[SSD_Q - 1:SSD_Q, :]
    grow = jnp.exp(acs)
    shrink = jnp.exp(a_last - acs)
    hrow = lax.broadcasted_iota(jnp.int32, (LANE, wd), 0)
    wcol = lax.broadcasted_iota(jnp.int32, (LANE, wd), 1)
    lane = lax.broadcasted_iota(jnp.int32, (1, LANE), 1)
    ys, snext = [], []
    for g in range(SSD_G):
        lo = (hrow - g * nj) * SSD_P
        head_lanes = jnp.logical_and(wcol >= lo, wcol < lo + SSD_P).astype(F32)
        xs = xbc[:, g * wd:(g + 1) * wd]
        bm = xbc[:, inner + g * SSD_N:inner + (g + 1) * SSD_N]
        cm = xbc[:, inner + (SSD_G + g) * SSD_N:inner + (SSD_G + g + 1) * SSD_N]
        xdt = xs * _spread(dt, head_lanes)
        grow_x = _spread(grow, head_lanes)
        y_off = _dot(cm, states[g]) * grow_x
        s_new = (states[g] * grow_x[SSD_Q - 1:SSD_Q, :]
                 + _dot(bm, xdt * _spread(shrink, head_lanes), TN))
        cb = _dot(cm, bm, NT)
        pieces = []
        for i in range(wd // LANE):
            xp = xdt[:, i * LANE:(i + 1) * LANE]
            acc = jnp.zeros((SSD_Q, LANE), F32)
            for hh in range(LANE // SSD_P):
                h = g * nj + i * (LANE // SSD_P) + hh
                decay = jnp.exp(jnp.where(causal, acs[:, h:h + 1] - acs_t[h:h + 1, :], NEG))
                keep = jnp.logical_and(lane >= hh * SSD_P, lane < (hh + 1) * SSD_P).astype(F32)
                acc = acc + _dot(cb * decay, xp * keep)
            pieces.append(acc)
        y_diag = pieces[0] if len(pieces) == 1 else jnp.concatenate(pieces, axis=1)
        ys.append(y_diag + y_off)
        snext.append(s_new)
    return ys, snext


def _ssd_specs(cdim, wd, rev, nc):
    ch = (lambda c: nc - 1 - c) if rev else (lambda c: c)
    full = lambda width: pl.BlockSpec((SSD_Q, width), lambda c: (ch(c), 0))
    vec = pl.BlockSpec((1, LANE), lambda c: (0, 0))
    st = pl.BlockSpec((1, SSD_G, SSD_N, wd), lambda c: (ch(c), 0, 0, 0))
    return full, vec, st


def _ssd_fwd(name, xbc, dtraw, bias, alog, inner):
    t, cdim = xbc.shape
    wd = inner // SSD_G
    nc = t // SSD_Q
    full, vec, st = _ssd_specs(cdim, wd, False, nc)

    def body(x_ref, r_ref, b_ref, a_ref, y_ref, st_ref, s_scr):
        @pl.when(pl.program_id(0) == 0)
        def _():
            s_scr[...] = jnp.zeros_like(s_scr)

        sprev = [s_scr[g] for g in range(SSD_G)]
        ys, snext = _ssd_chunk(x_ref[...].astype(F32), r_ref[...], b_ref[...], a_ref[...], sprev)
        for g in range(SSD_G):
            st_ref[0, g] = sprev[g]
            y_ref[:, g * wd:(g + 1) * wd] = ys[g]
            s_scr[g] = snext[g]

    return pl.pallas_call(
        body, name=name,
        out_shape=[jax.ShapeDtypeStruct((t, inner), F32),
                   jax.ShapeDtypeStruct((nc, SSD_G, SSD_N, wd), F32)],
        grid=(nc,),
        in_specs=[full(cdim), full(LANE), vec, vec],
        out_specs=[full(inner), st],
        scratch_shapes=[pltpu.VMEM((SSD_G, SSD_N, wd), F32)],
        compiler_params=_params(("arbitrary",)),
    )(xbc, dtraw, bias, alog)


def _ssd_bwd(name, xbc, dtraw, bias, alog, states, dy, dxs_extra):
    t, cdim = xbc.shape
    inner = dy.shape[1]
    wd = inner // SSD_G
    nc = t // SSD_Q
    full, vec, st = _ssd_specs(cdim, wd, True, nc)

    def body(x_ref, r_ref, b_ref, a_ref, st_ref, dy_ref, dx0_ref,
             dx_ref, dr_ref, db_ref, da_ref, ds_scr):
        first = pl.program_id(0) == 0

        @pl.when(first)
        def _():
            ds_scr[...] = jnp.zeros_like(ds_scr)

        sprev = [st_ref[0, g] for g in range(SSD_G)]
        _, vjp = jax.vjp(_ssd_chunk, x_ref[...].astype(F32), r_ref[...], b_ref[...], a_ref[...],
                         sprev)
        dyv = dy_ref[...]
        dys = [dyv[:, g * wd:(g + 1) * wd] for g in range(SSD_G)]
        dsn = [ds_scr[g] for g in range(SSD_G)]
        dx, dr, db, da, dsp = vjp((dys, dsn))
        dx_ref[:, :inner] = dx[:, :inner] + dx0_ref[...].astype(F32)
        dx_ref[:, inner:] = dx[:, inner:]
        dr_ref[...] = dr
        for g in range(SSD_G):
            ds_scr[g] = dsp[g]

        @pl.when(first)
        def _():
            db_ref[...] = db
            da_ref[...] = da

        @pl.when(jnp.logical_not(first))
        def _():
            db_ref[...] += db
            da_ref[...] += da

    return pl.pallas_call(
        body, name=name,
        out_shape=[jax.ShapeDtypeStruct((t, cdim), F32),
                   jax.ShapeDtypeStruct((t, LANE), F32),
                   jax.ShapeDtypeStruct((1, LANE), F32),
                   jax.ShapeDtypeStruct((1, LANE), F32)],
        grid=(nc,),
        in_specs=[full(cdim), full(LANE), vec, vec, st, full(inner), full(inner)],
        out_specs=[full(cdim), full(LANE), vec, vec],
        scratch_shapes=[pltpu.VMEM((SSD_G, SSD_N, wd), F32)],
        compiler_params=_params(("arbitrary",)),
    )(xbc, dtraw, bias, alog, states, dy, dxs_extra)


def _act(g, u):
    return _silu(g.astype(F32)) * u.astype(F32)


def _mix(o0, o1, o2, l0, l1, l2):
    m = lax.stop_gradient(jnp.maximum(jnp.maximum(l0, l1), l2))
    e0, e1, e2 = jnp.exp(l0 - m), jnp.exp(l1 - m), jnp.exp(l2 - m)
    return (e0 * o0 + e1 * o1 + e2 * o2) / (e0 + e1 + e2)


def _gate(y, xs, z, dexp, gain):
    v = (y + xs.astype(F32) * dexp) * _silu(z.astype(F32))
    return _rms(v, gain)


def _merge(ga, gs, ap, sp):
    return jax.nn.sigmoid(ga.astype(F32)) * ap + jax.nn.sigmoid(gs.astype(F32)) * sp


def _alibi_coefs(hp):
    n = hp * len(PATTERNS)
    slopes = np.exp2(-ALIBI_MAX_EXP * np.arange(1, n + 1, dtype=np.float32) / n).astype(np.float32)
    return [jnp.asarray(slopes[g * hp:(g + 1) * hp] * np.float32(d))
            for g, (_, d) in enumerate(PATTERNS)]


def _local_step(x, tgt, w, p, gw_=None, aw=None):
    t, d = x.shape
    dff = w["d1"].shape[0]
    aw = w["abt"].shape[1] if aw is None else aw
    hp = aw // HD
    qkv = len(PATTERNS) * aw
    inner = p["ssd_norm"].shape[1]
    nh = p["dt_bias"].shape[1]
    gw_ = {} if gw_ is None else gw_
    gw = inner // SSD_G
    cdim = inner + 2 * SSD_G * SSD_N
    z_off, xbc_off = 3 * qkv, 3 * qkv + inner
    ga_off = xbc_off + cdim
    gs_off = ga_off + d
    hw = d // 2
    assert z_off % gw == 0 and xbc_off % LANE == 0 and ga_off % hw == 0 and gs_off % hw == 0
    assert (nh // SSD_G) * SSD_P == gw and hp % 2 == 0 and aw % LANE == 0 and nh <= LANE
    gdt = MXU_DTYPE

    row = lambda a, width, base=0: ("row", a, width, base)
    const = lambda a, width, base=0: ("const", a, width, base)

    def rms_fwd(name, xin, g):
        return _rw(name, lambda xv, gv: (_rms(xv, gv),), [row(xin, d), const(g, d)],
                   [((d,), ACT_DTYPE)])[0]

    def rms_bwd(name, xin, g, dh, dres):
        def fn(xv, gv, dhv, drv):
            _, vjp = jax.vjp(_rms, xv, gv)
            dx, dg = vjp(dhv.astype(F32))
            return drv + dx, dg
        return _rw(name, fn, [row(xin, d), const(g, d), row(dh, d), row(dres, d)],
                   [((d,), F32)], accs=[(1, d)])

    def ffn_fwd(tag, xin, g, wgut, wd):
        h = rms_fwd(tag + "_norm", xin, g)
        gu = _mm(tag + "_up", h, wgut, "nt", ACT_DTYPE)
        a = _rw(tag + "_act", lambda gv, uv: (_act(gv, uv),),
                [row(gu, dff, 0), row(gu, dff, 1)], [((dff,), ACT_DTYPE)])[0]
        xo = _mm(tag + "_down", a, wd, "nn", F32, res=xin, scale=0.5)
        return xo, (h, gu, a)

    def ffn_bwd(tag, xin, g, wgut, wd, saved, dxo, key_gu, key_d):
        h, gu, a = saved
        da = _mm(tag + "_da", dxo, wd, "nt", ACT_DTYPE, scale=0.5)
        gw_[key_d] = _mm(tag + "_dwd", a, dxo, "tn", gdt, scale=0.5)

        def fn(gv, uv, dav):
            _, vjp = jax.vjp(_act, gv, uv)
            return vjp(dav.astype(F32))
        dgu = _rw(tag + "_dact", fn, [row(gu, dff, 0), row(gu, dff, 1), row(da, dff)],
                  [((dff, dff), ACT_DTYPE)])[0]
        gw_[key_gu] = _mm(tag + "_dwgu", dgu, h, "tn", gdt)
        dh = _mm(tag + "_dh", dgu, wgut, "nn", F32)
        return rms_bwd(tag + "_dnorm", xin, g, dh, dxo)

    x1, ffn1_saved = ffn_fwd("ffn1", x, p["ffn1_norm"], w["gu1t"], w["d1"])
    h2 = rms_fwd("mix_norm", x1, p["mix_norm"])
    proj = _mm("in_proj", h2, w["maint"], "nt", ACT_DTYPE, cap_m=512, cap_n=2944)
    dtraw = _mm("dt_proj", h2, w["dtt"], "nt", F32)

    coefs = _alibi_coefs(hp)
    qg2 = jnp.concatenate([p["q_norm"], p["q_norm"]], axis=1)
    kg2 = jnp.concatenate([p["k_norm"], p["k_norm"]], axis=1)
    pw = 2 * HD
    attn_bases = [[(off + gi * aw) // pw for off in (0, qkv, 2 * qkv)]
                  for gi in range(len(PATTERNS))]
    attn_o, attn_l = [], []
    for gi, (_, dil) in enumerate(PATTERNS):
        o, l = _attn_fwd(f"attn_fwd{gi}", proj, attn_bases[gi], qg2, kg2, coefs[gi], dil)
        attn_o.append(o)
        attn_l.append(l)
    ao = _rw("attn_mix", lambda *v: (_mix(*v),), [row(a, aw) for a in attn_o + attn_l],
             [((aw,), ACT_DTYPE)])[0]

    xbc = _conv_fwd("conv_fwd", proj, xbc_off // LANE, p["conv_w"], p["conv_b"])
    pad = lambda v: jnp.pad(v, ((0, 0), (0, LANE - nh)))
    bias_p, alog_p = pad(p["dt_bias"]), pad(p["a_log"])
    yssd, states = _ssd_fwd("ssd_fwd", xbc, dtraw, bias_p, alog_p, inner)
    dexp = jnp.repeat(p["d_skip"], SSD_P, axis=1)
    gate_ins = [row(yssd, gw), row(xbc, gw), row(proj, gw, z_off // gw),
                const(dexp, gw), const(p["ssd_norm"], gw)]
    yn = _rw("ssd_gate", lambda *v: (_gate(*v),), gate_ins, [((gw,), ACT_DTYPE)], ncb=SSD_G)[0]

    ap = _mm("attn_out", ao, w["abt"], "nt", F32)
    sp = _mm("ssd_out", yn, w["sb"], "nn", F32)
    merge_ins = [row(proj, hw, ga_off // hw), row(proj, hw, gs_off // hw), row(ap, hw), row(sp, hw)]
    mg = _rw("merge", lambda *v: (_merge(*v),), merge_ins, [((hw,), ACT_DTYPE)], ncb=2)[0]
    x2 = _mm("mix_out", mg, w["out"], "nn", F32, res=x1)
    x3, ffn2_saved = ffn_fwd("ffn2", x2, p["ffn2_norm"], w["gu2t"], w["d2"])

    def loss_fn(yv, tv):
        e = yv - tv
        return e * (1.0 / d), _colsum(e * e)
    dy, loss_vec = _rw("loss", loss_fn, [row(x3, d), row(tgt, d)], [((d,), F32)], accs=[(1, d)])

    gp = {}
    dx2, gp["ffn2_norm"] = ffn_bwd(
        "ffn2", x2, p["ffn2_norm"], w["gu2t"], w["d2"], ffn2_saved, dy, "gu2t", "d2")
    dmg = _mm("d_merge", dx2, w["out"], "nt", ACT_DTYPE)
    gw_["out"] = _mm("dw_out", mg, dx2, "tn", gdt)

    def merge_bwd(gav, gsv, apv, spv, dv):
        _, vjp = jax.vjp(_merge, gav, gsv, apv, spv)
        return vjp(dv.astype(F32))
    dga, dgs, dap, dsp = _rw("d_merge_gate", merge_bwd, merge_ins + [row(dmg, hw)],
                             [((hw,), ACT_DTYPE)] * 4, ncb=2)
    gw_["abt"] = _mm("dw_ab", dap, ao, "tn", gdt)
    dao = _mm("d_attn_o", dap, w["abt"], "nn", F32)
    gw_["sb"] = _mm("dw_sb", yn, dsp, "tn", gdt)
    dyn = _mm("d_ssd_y", dsp, w["sb"], "nt", F32)

    def gate_bwd(yv, xv, zv, dev, gv, dv):
        _, vjp = jax.vjp(_gate, yv, xv, zv, dev, gv)
        return vjp(dv)
    dyssd, dxs_gate, dz, ddexp, gp["ssd_norm"] = _rw(
        "d_ssd_gate", gate_bwd, gate_ins + [row(dyn, gw)],
        [((gw,), F32), ((gw,), F32), ((gw,), ACT_DTYPE)], accs=[(1, gw), (1, gw)], ncb=SSD_G)
    gp["d_skip"] = ddexp.reshape(nh, SSD_P).sum(axis=1).reshape(1, nh)

    dxbc, ddtraw, dbias, dalog = _ssd_bwd("ssd_bwd", xbc, dtraw, bias_p, alog_p, states,
                                          dyssd, dxs_gate)
    gp["dt_bias"], gp["a_log"] = dbias[:, :nh], dalog[:, :nh]
    du, gp["conv_w"], gp["conv_b"] = _conv_bwd("conv_bwd", proj, xbc_off // LANE,
                                               p["conv_w"], p["conv_b"], dxbc)

    def mix_bwd(*v):
        _, vjp = jax.vjp(_mix, *v[:6])
        return vjp(v[6])
    dmix = _rw("d_attn_mix", mix_bwd, [row(a, aw) for a in attn_o + attn_l] + [row(dao, aw)],
               [((aw,), F32)] * 6)
    dq, dk, dv = [], [], []
    dqg = dkg = None
    for gi, (_, dil) in enumerate(PATTERNS):
        r = _attn_bwd(f"attn_bwd{gi}", proj, attn_bases[gi], qg2, kg2, coefs[gi], dil,
                      dmix[gi], dmix[3 + gi])
        dq.append(r[0])
        dk.append(r[1])
        dv.append(r[2])
        dqg = r[3] if dqg is None else dqg + r[3]
        dkg = r[4] if dkg is None else dkg + r[4]
    gp["q_norm"] = dqg[:, :HD] + dqg[:, HD:]
    gp["k_norm"] = dkg[:, :HD] + dkg[:, HD:]

    segs = dq + dk + dv + [dz, du, dga, dgs]
    gw_["maint"] = _mm("dw_in", segs, h2, "tn", gdt)
    gw_["dtt"] = _mm("dw_dt", ddtraw, h2, "tn", gdt)
    dh2 = _mm("d_h2_main", segs, w["maint"], "nn", F32)
    dh2 = _mm("d_h2_dt", ddtraw, w["dtt"], "nn", F32, res=dh2)
    dx1, gp["mix_norm"] = rms_bwd("d_mix_norm", x1, p["mix_norm"], dh2, dx2)
    dx0, gp["ffn1_norm"] = ffn_bwd(
        "ffn1", x, p["ffn1_norm"], w["gu1t"], w["d1"], ffn1_saved, dx1, "gu1t", "d1")
    return loss_vec, dx0, gw_, gp


MESH = pl.DeviceIdType.MESH
HBM_SPEC = pl.BlockSpec(memory_space=pltpu.HBM)


def _mesh_pos():
    return lax.axis_index("x"), lax.axis_index("y"), lax.axis_index("c")


def _flip(pos, k):
    x, y, c = pos
    return (1 - x if k & 4 else x, 1 - y if k & 2 else y, 1 - c if k & 1 else c)


def _dev_index(pos):
    return 4 * pos[0] + 2 * pos[1] + pos[2]


def _rows_of(ref, base, stride, rows, pos):
    start = pl.multiple_of(base + stride * _dev_index(pos), ROW_ALIGN)
    return ref.at[pl.ds(start, rows)]


def _gather(name, shards, dests, out_shapes):
    n = len(shards)
    n_out = len(out_shapes)

    def body(*refs):
        x_refs = refs[:n]
        o_refs = refs[n:n + n_out]
        send_sems, recv_sems, local_sems = refs[n + n_out:]
        me = _mesh_pos()
        sibling = _flip(me, 1)
        chips = [_flip(me, 4), _flip(me, 2), _flip(me, 6)]

        def slot(i, block):
            k_out, base, stride = dests[i]
            return _rows_of(o_refs[k_out], base, stride, shards[i].shape[0], block)

        def copy(i, k, block, to, src=None):
            dst = slot(i, block)
            return pltpu.make_async_remote_copy(
                src_ref=dst if src is None else src, dst_ref=dst,
                send_sem=send_sems.at[7 * i + k], recv_sem=recv_sems.at[7 * i + k],
                device_id=to, device_id_type=MESH)

        mine = [pltpu.make_async_copy(x_refs[i], slot(i, me), local_sems.at[i]) for i in range(n)]
        for cp in mine:
            cp.start()
        first = []
        for i in range(n):
            first.append(copy(i, 0, me, sibling, src=x_refs[i]))
            first += [copy(i, 1 + j, me, chip, src=x_refs[i]) for j, chip in enumerate(chips)]
        for cp in first:
            cp.start()
        passed = []
        for j, chip in enumerate(chips):
            for i in range(n):
                copy(i, 1 + j, chip, me).wait_recv()
                fwd = copy(i, 4 + j, chip, sibling)
                fwd.start()
                passed.append(fwd)
        for i in range(n):
            copy(i, 0, sibling, me).wait_recv()
            for j, chip in enumerate(chips):
                copy(i, 4 + j, _flip(chip, 1), me).wait_recv()
        for cp in first + passed:
            cp.wait_send()
        for cp in mine:
            cp.wait()

    return pl.pallas_call(
        body, name=name,
        out_shape=[jax.ShapeDtypeStruct(s, dt) for s, dt in out_shapes],
        in_specs=[HBM_SPEC] * n, out_specs=[HBM_SPEC] * n_out,
        scratch_shapes=[pltpu.SemaphoreType.DMA((7 * n,)), pltpu.SemaphoreType.DMA((7 * n,)),
                        pltpu.SemaphoreType.DMA((n,))],
    )(*shards)


def _exchange(name, grads, srcs, small):
    n = len(srcs)
    ng = len(grads)

    def body(*refs):
        g_refs = refs[:ng]
        m_ref = refs[ng]
        r_refs = refs[ng + 1:ng + 1 + n]
        s_ref = refs[ng + 1 + n]
        send_sems, recv_sems, local_sems = refs[ng + 2 + n:]
        me = _mesh_pos()
        my = _dev_index(me)

        def slab(i, pos):
            gi, base, stride, rows = srcs[i]
            return _rows_of(g_refs[gi], base, stride, rows, pos)

        own = [pltpu.make_async_copy(slab(i, me), r_refs[i].at[my], local_sems.at[i])
               for i in range(n)]
        own.append(pltpu.make_async_copy(m_ref, s_ref.at[my], local_sems.at[n]))
        for cp in own:
            cp.start()

        def copies(k, src_pos, slot_pos):
            peer = _flip(me, k)
            si = _dev_index(slot_pos)
            out = [pltpu.make_async_remote_copy(
                src_ref=slab(i, src_pos), dst_ref=r_refs[i].at[si],
                send_sem=send_sems.at[7 * i + k - 1], recv_sem=recv_sems.at[7 * i + k - 1],
                device_id=peer, device_id_type=MESH) for i in range(n)]
            out.append(pltpu.make_async_remote_copy(
                src_ref=m_ref, dst_ref=s_ref.at[si],
                send_sem=send_sems.at[7 * n + k - 1], recv_sem=recv_sems.at[7 * n + k - 1],
                device_id=peer, device_id_type=MESH))
            return out

        sent = [cp for k in range(1, NDEV) for cp in copies(k, _flip(me, k), me)]
        for cp in sent:
            cp.start()
        for k in range(1, NDEV):
            for cp in copies(k, me, _flip(me, k)):
                cp.wait_recv()
        for cp in sent:
            cp.wait_send()
        for cp in own:
            cp.wait()

    out_shape = [jax.ShapeDtypeStruct((NDEV, rows, grads[gi].shape[1]), grads[gi].dtype)
                 for gi, _, _, rows in srcs]
    out_shape.append(jax.ShapeDtypeStruct((NDEV,) + small.shape, small.dtype))
    return pl.pallas_call(
        body, name=name, out_shape=out_shape,
        in_specs=[HBM_SPEC] * (ng + 1), out_specs=[HBM_SPEC] * (n + 1),
        scratch_shapes=[pltpu.SemaphoreType.DMA((7 * (n + 1),)),
                        pltpu.SemaphoreType.DMA((7 * (n + 1),)),
                        pltpu.SemaphoreType.DMA((n + 1,))],
    )(*grads, small)


SEM_SPEC = pl.BlockSpec(memory_space=pltpu.SEMAPHORE)
SIDE_EFFECT = pltpu.SideEffectType.DATAFLOW_SIDE_EFFECTING


def _split_refs(plan, srcs, lands, i, src_for, land_from):
    si, sbase, sstride, li, lbase, lstride, rows = plan[i]
    return (_rows_of(srcs[si], sbase, sstride, rows, src_for),
            _rows_of(lands[li], lbase, lstride, rows, land_from))


def _split_start(name, srcs, lands, plan, after=()):
    ns, nl, n = len(srcs), len(lands), len(plan)

    def body(*refs):
        s_refs = refs[:ns]
        l_refs = refs[ns:ns + nl]
        send_sems, recv_sems = refs[ns + nl + len(after):ns + nl + len(after) + 2]
        local_sems = refs[ns + nl + len(after) + 2]
        token = refs[ns + nl + len(after) + 3 + ns + nl]
        me = _mesh_pos()
        for i in range(n):
            src, dst = _split_refs(plan, s_refs, l_refs, i, me, me)
            pltpu.make_async_copy(src, dst, local_sems.at[i]).start()
        for k in range(1, NDEV):
            peer = _flip(me, k)
            for i in range(n):
                src, dst = _split_refs(plan, s_refs, l_refs, i, peer, me)
                pltpu.make_async_remote_copy(
                    src_ref=src, dst_ref=dst,
                    send_sem=send_sems.at[7 * i + k - 1], recv_sem=recv_sems.at[7 * i + k - 1],
                    device_id=peer, device_id_type=MESH).start()
        token[...] = jnp.zeros_like(token)

    hbm = lambda a: pltpu.HBM(a.shape, a.dtype)
    out_shape = ((pltpu.SemaphoreType.DMA((7 * n,)), pltpu.SemaphoreType.DMA((7 * n,)),
                  pltpu.SemaphoreType.DMA((n,)))
                 + tuple(hbm(a) for a in srcs) + tuple(hbm(a) for a in lands)
                 + (jax.ShapeDtypeStruct((8, LANE), F32),))
    out = pl.pallas_call(
        body, name=name, out_shape=out_shape,
        in_specs=[HBM_SPEC] * (ns + nl) + [ANY_SPEC] * len(after),
        out_specs=(SEM_SPEC, SEM_SPEC, SEM_SPEC) + (HBM_SPEC,) * (ns + nl)
        + (pl.BlockSpec(memory_space=pltpu.VMEM),),
        input_output_aliases={i: 3 + i for i in range(ns + nl)},
        compiler_params=pltpu.CompilerParams(has_side_effects=SIDE_EFFECT),
    )(*[pltpu.with_memory_space_constraint(a, pltpu.HBM) for a in tuple(srcs) + tuple(lands)],
      *after)
    _Order.tokens.append(out[-1])
    return out[0], out[1], out[2], out[3:3 + ns], out[3 + ns:3 + ns + nl]


def _split_wait(name, started, plan):
    send_sems, recv_sems, local_sems, srcs, lands = started
    ns, nl, n = len(srcs), len(lands), len(plan)
    after = [_Order.last] if _Order.last is not None else []

    def body(*refs):
        s_refs = refs[:ns]
        l_refs = refs[ns:ns + nl]
        send_sems, recv_sems, local_sems = refs[ns + nl:ns + nl + 3]
        me = _mesh_pos()
        for i in range(n):
            src, dst = _split_refs(plan, s_refs, l_refs, i, me, me)
            pltpu.make_async_copy(src, dst, local_sems.at[i]).wait()
        for k in range(1, NDEV):
            peer = _flip(me, k)
            for i in range(n):
                src, dst = _split_refs(plan, s_refs, l_refs, i, peer, peer)
                cp = pltpu.make_async_remote_copy(
                    src_ref=src, dst_ref=dst,
                    send_sem=send_sems.at[7 * i + k - 1], recv_sem=recv_sems.at[7 * i + k - 1],
                    device_id=peer, device_id_type=MESH)
                cp.wait_send()
                cp.wait_recv()

    hbm = lambda a: pltpu.HBM(a.shape, a.dtype)
    out = pl.pallas_call(
        body, name=name,
        out_shape=tuple(hbm(a) for a in srcs) + tuple(hbm(a) for a in lands),
        in_specs=[HBM_SPEC] * (ns + nl) + [SEM_SPEC] * 3 + [ANY_SPEC] * len(after),
        out_specs=(HBM_SPEC,) * (ns + nl),
        input_output_aliases={i: i for i in range(ns + nl)},
        compiler_params=pltpu.CompilerParams(has_side_effects=SIDE_EFFECT),
    )(*srcs, *lands, send_sems, recv_sems, local_sems, *after)
    return list(out[ns:])


def _sum_slabs(name, a):
    s, r, c = a.shape

    def body(a_ref, o_ref):
        acc = a_ref[0].astype(F32)
        for i in range(1, s):
            acc = acc + a_ref[i].astype(F32)
        o_ref[...] = acc

    return pl.pallas_call(body, name=name, out_shape=jax.ShapeDtypeStruct((r, c), F32))(a)


def _adamw_update(g, w, m, v):
    mn = ADAM_B1 * m + (1.0 - ADAM_B1) * g
    vn = ADAM_B2 * v + (1.0 - ADAM_B2) * (g * g)
    m_hat = mn / (1.0 - ADAM_B1 ** ADAM_STEP)
    v_hat = vn / (1.0 - ADAM_B2 ** ADAM_STEP)
    delta = -ADAM_LR * (m_hat / (jnp.sqrt(v_hat) + ADAM_EPS) + ADAM_WD * w)
    return delta, mn, vn


def _adamw(name, gsrc, w, m, v, transposed=False, tr=256):
    s = gsrc.shape[0]
    r, c = w.shape
    step = LANE if transposed else 8
    tr = max(t for t in range(step, min(tr, r) + 1, step) if r % t == 0)

    def body(g_ref, w_ref, m_ref, v_ref, go_ref, d_ref, mo_ref, vo_ref):
        g = g_ref[0].astype(F32)
        for i in range(1, s):
            g = g + g_ref[i].astype(F32)
        if transposed:
            g = g.T[:, :c]
        delta, mn, vn = _adamw_update(g, w_ref[...], m_ref[...], v_ref[...])
        go_ref[...] = g
        d_ref[...] = delta
        mo_ref[...] = mn
        vo_ref[...] = vn

    blk = pl.BlockSpec((tr, c), lambda i: (i, 0))
    if transposed:
        g_spec = pl.BlockSpec((s, gsrc.shape[1], tr), lambda i: (0, 0, i))
    else:
        g_spec = pl.BlockSpec((s, tr, c), lambda i: (0, i, 0))
    return pl.pallas_call(
        body, name=name, out_shape=[jax.ShapeDtypeStruct((r, c), F32)] * 4,
        grid=(r // tr,),
        in_specs=[g_spec, blk, blk, blk], out_specs=[blk] * 4,
        compiler_params=_params(("parallel",)),
    )(gsrc, w, m, v)


REPLICATED = ("ffn1_norm", "mix_norm", "q_norm", "k_norm", "conv_b", "dt_bias", "a_log",
              "d_skip", "ssd_norm", "ffn2_norm")
ALL_WEIGHTS = ("ffn1_norm", "ffn1_w_gate", "ffn1_w_up", "ffn1_w_down", "mix_norm", "w_in",
               "q_norm", "k_norm", "conv_w", "conv_b", "dt_bias", "a_log", "d_skip", "ssd_norm",
               "w_attn_branch", "w_ssd_branch", "w_out", "ffn2_norm", "ffn2_w_gate", "ffn2_w_up",
               "ffn2_w_down")
BIG = (("ffn1_w_gate", True, "gu1t", 0), ("ffn1_w_up", True, "gu1t", 1),
       ("ffn1_w_down", False, "d1", 0), ("w_in", True, "wint", 0),
       ("w_attn_branch", True, "abt", 0), ("w_ssd_branch", False, "sb", 0),
       ("w_out", False, "out", 0),
       ("ffn2_w_gate", True, "gu2t", 0), ("ffn2_w_up", True, "gu2t", 1),
       ("ffn2_w_down", False, "d2", 0))


def _nrows(shape, cols):
    return -(-math.prod(shape) // cols)


def _pack_rows(arrs, cols, row_tile):
    parts = []
    for a in arrs:
        flat = a.reshape(-1)
        nr = -(-flat.shape[0] // cols)
        parts.append(jnp.pad(flat, (0, nr * cols - flat.shape[0])).reshape(nr, cols))
    out = jnp.concatenate(parts, axis=0)
    return jnp.pad(out, ((0, _round_up(out.shape[0], row_tile) - out.shape[0]), (0, 0)))


def _unpack_rows(packed, shapes):
    cols = packed.shape[-1]
    out, r0 = [], 0
    for sh in shapes:
        nr = _nrows(sh, cols)
        out.append(packed[r0:r0 + nr].reshape(-1)[:math.prod(sh)].reshape(tuple(sh)))
        r0 += nr
    return out


def kernel(x, ffn1_norm, ffn1_w_gate, ffn1_w_up, ffn1_w_down, mix_norm, w_in, q_norm, k_norm, conv_w, conv_b, dt_bias, a_log, d_skip, ssd_norm, w_attn_branch, w_ssd_branch, w_out, ffn2_norm, ffn2_w_gate, ffn2_w_up, ffn2_w_down, loss_target, m_ffn1_norm, m_ffn1_w_gate, m_ffn1_w_up, m_ffn1_w_down, m_mix_norm, m_w_in, m_q_norm, m_k_norm, m_conv_w, m_conv_b, m_dt_bias, m_a_log, m_d_skip, m_ssd_norm, m_w_attn_branch, m_w_ssd_branch, m_w_out, m_ffn2_norm, m_ffn2_w_gate, m_ffn2_w_up, m_ffn2_w_down, v_ffn1_norm, v_ffn1_w_gate, v_ffn1_w_up, v_ffn1_w_down, v_mix_norm, v_w_in, v_q_norm, v_k_norm, v_conv_w, v_conv_b, v_dt_bias, v_a_log, v_d_skip, v_ssd_norm, v_w_attn_branch, v_w_ssd_branch, v_w_out, v_ffn2_norm, v_ffn2_w_gate, v_ffn2_w_up, v_ffn2_w_down):
    given = dict(locals())
    wts = {n: given[n] for n in ALL_WEIGHTS}
    mom = {n: given["m_" + n] for n in ALL_WEIGHTS}
    var = {n: given["v_" + n] for n in ALL_WEIGHTS}
    d = x.shape[-1]
    nh = dt_bias.shape[1]
    my = _dev_index(_mesh_pos())

    def row_form(n, col_sharded):
        a = wts[n][0].T if col_sharded else wts[n][0]
        return jnp.pad(a, ((0, _round_up(a.shape[0], ROW_ALIGN) - a.shape[0]), (0, 0)))

    _Order.tokens, _Order.last = [], None
    shard = {n: row_form(n, cs).astype(MXU_DTYPE) for n, cs, _, _ in BIG}
    entries = {buf: [e for e in BIG if e[2] == buf] for buf in dict.fromkeys(e[2] for e in BIG)}

    def buf_shape(buf):
        r, c = shard[entries[buf][0][0]].shape
        return (len(entries[buf]) * NDEV * r, c)

    def gather_plan(bufs):
        srcs, lands, plan = [], [], []
        for li, buf in enumerate(bufs):
            lands.append(lax.empty(buf_shape(buf), MXU_DTYPE))
            for n, _, _, pos in entries[buf]:
                r = shard[n].shape[0]
                plan.append((len(srcs), 0, 0, li, pos * NDEV * r, r, r))
                srcs.append(shard[n])
        return srcs, lands, plan

    def scatter_plan(bufs, grads):
        srcs, lands, plan, names = [], [], [], []
        for si, buf in enumerate(bufs):
            srcs.append(grads[buf])
            for n, _, _, pos in entries[buf]:
                r, c = shard[n].shape
                plan.append((si, pos * NDEV * r, r, len(lands), 0, r, r))
                lands.append(lax.empty((NDEV * r, c), MXU_DTYPE))
                names.append(n)
        return srcs, lands, plan, names

    first_bufs = ("gu1t", "d1")
    shards, dests, out_shapes = [], [], []
    for bi, buf in enumerate(first_bufs):
        out_shapes.append((buf_shape(buf), MXU_DTYPE))
        for n, _, _, pos in entries[buf]:
            r = shard[n].shape[0]
            shards.append(shard[n])
            dests.append((bi, pos * NDEV * r, r))
    conv_rows = _pack_rows([conv_w[0]], LANE, ROW_ALIGN)
    shards.append(conv_rows)
    dests.append((len(first_bufs), 0, conv_rows.shape[0]))
    out_shapes.append(((NDEV * conv_rows.shape[0], LANE), F32))
    gathered = _gather("gather_first", shards, dests, out_shapes)

    in_cols = w_in.shape[2]
    in_pad = _round_up(in_cols, ROW_ALIGN)
    dt_off = NDEV * in_cols - 2 * d - nh
    second_bufs = ("wint",)
    third_bufs = ("abt", "sb", "out", "gu2t", "d2")
    plan2 = gather_plan(second_bufs)
    started2 = _split_start("gather_in_start", *plan2, after=[gathered[0]])
    started3 = []

    class Weights(dict):
        def __missing__(self, key):
            if key in ("maint", "dtt"):
                wint = _split_wait("gather_in_wait", started2, plan2[2])[0]
                plan3 = gather_plan(third_bufs)
                started3.append((_split_start("gather_rest_start", *plan3, after=[wint]), plan3[2]))
                wint = wint.reshape(NDEV, in_pad, d)[:, :in_cols].reshape(NDEV * in_cols, d)
                self["maint"] = jnp.concatenate([wint[:dt_off], wint[dt_off + nh:]], axis=0)
                self["dtt"] = jnp.pad(wint[dt_off:dt_off + nh], ((0, LANE - nh), (0, 0)))
            else:
                st, plan = started3[0]
                for buf, a in zip(third_bufs, _split_wait("gather_rest_wait", st, plan)):
                    self[buf] = a
            return self[key]

    w = Weights(gu1t=gathered[0], d1=gathered[1])
    p = {n: wts[n] for n in REPLICATED}
    conv_all = gathered[-1].reshape(NDEV, conv_rows.shape[0] * LANE)[:, :math.prod(conv_w.shape[1:])]
    p["conv_w"] = (conv_all.reshape((NDEV,) + conv_w.shape[1:]).transpose(1, 0, 2)
                   .reshape(conv_w.shape[1], NDEV * conv_w.shape[2]))

    groups = (("scatter_late", ("gu2t", "d2", "out", "abt", "sb")),
              ("scatter_in", ("maint", "dtt")),
              ("scatter_first", ("gu1t", "d1")))
    in_flight = []

    class Grads(dict):
        def __setitem__(self, key, value):
            dict.__setitem__(self, key, value)
            for tag, need in groups:
                if key in need and all(k in self for k in need):
                    if tag == "scatter_in":
                        gwin = jnp.concatenate([self["maint"][:dt_off], self["dtt"][:nh],
                                                self["maint"][dt_off:]], axis=0)
                        gwin = jnp.pad(gwin.reshape(NDEV, in_cols, d),
                                       ((0, 0), (0, in_pad - in_cols), (0, 0)))
                        bufs, grads = ("wint",), {"wint": gwin.reshape(NDEV * in_pad, d)}
                    else:
                        bufs, grads = need, self
                    srcs, lands, plan, names = scatter_plan(bufs, grads)
                    in_flight.append((tag, _split_start(tag + "_start", srcs, lands, plan),
                                      plan, names))

    loss_vec, dx, gw, gp = _local_step(x[0], loss_target[0], w, p, Grads(),
                                       aw=w_attn_branch.shape[1])

    small_names = REPLICATED + ("conv_w",)
    small_shapes = [gp[n].shape for n in small_names]
    small = _pack_rows([gp[n] for n in small_names], LANE, 8)
    small_all = _exchange("exchange_small", [], [], small)[0]

    outs = [{}, {}, {}, {}]
    col_sharded_of = {n: cs for n, cs, _, _ in BIG}
    for tag, started, plan, names in in_flight:
        for n, rv in zip(names, _split_wait(tag + "_wait", started, plan)):
            rv = rv.reshape(NDEV, shard[n].shape[0], shard[n].shape[1])
            res = _Order.done(_adamw("adamw_" + n, rv, wts[n][0], mom[n][0], var[n][0],
                                     transposed=col_sharded_of[n]))
            for k in range(4):
                outs[k][n] = res[k][None]

    small_g = _unpack_rows(_sum_slabs("sum_small_grads", small_all), small_shapes)
    small_g = dict(zip(small_names, small_g))
    cs = conv_w.shape[2]
    small_g["conv_w"] = lax.dynamic_slice_in_dim(small_g["conv_w"], my * cs, cs, axis=1)
    small_shard_shapes = [wts[n].shape[-2:] for n in small_names]
    sg = _pack_rows([small_g[n] for n in small_names], LANE, 8)
    sw = _pack_rows([wts[n] for n in small_names], LANE, 8)
    sm = _pack_rows([mom[n] for n in small_names], LANE, 8)
    sv = _pack_rows([var[n] for n in small_names], LANE, 8)
    res_small = _adamw("adamw_small", sg[None], sw, sm, sv, tr=sg.shape[0])
    for k in range(4):
        for n, a in zip(small_names, _unpack_rows(res_small[k], small_shard_shapes)):
            outs[k][n] = a.reshape(wts[n].shape)

    loss = lax.psum(0.5 * jnp.sum(loss_vec) / d, ("x", "y", "c"))
    result = [loss, dx[None]]
    for k in range(4):
        result += [outs[k][n] for n in ALL_WEIGHTS]
    return tuple(result)
```

```python
import functools
import math

import numpy as np
import jax
import jax.numpy as jnp
from jax import lax
from jax.experimental import pallas as pl
from jax.experimental.pallas import tpu as pltpu

F32 = jnp.float32
BF16 = jnp.bfloat16
MXU_DTYPE = BF16
ACT_DTYPE = BF16

NDEV = 8
EPS = 1e-6
HD = 64
QB = 128
PATTERNS = ((128, 1), (512, 4), (2048, 16))
ALIBI_MAX_EXP = 8.0
SSD_P = 64
SSD_N = 128
SSD_G = 4
SSD_Q = 128
SSD_K = 4
NEG = -1e30
LANE = 128
ROW_ALIGN = 16
VMEM_LIMIT = 56 * 1024 * 1024

ADAM_LR, ADAM_B1, ADAM_B2, ADAM_EPS, ADAM_WD, ADAM_STEP = 0.001, 0.9, 0.999, 1e-8, 0.01, 10

NN = (((1,), (0,)), ((), ()))
NT = (((1,), (1,)), ((), ()))
TN = (((0,), (0,)), ((), ()))


def _dot(a, b, dims=NN):
    return lax.dot_general(a.astype(MXU_DTYPE), b.astype(MXU_DTYPE), dims,
                           preferred_element_type=F32)


def _split3(a):
    hi = a.astype(BF16)
    r = a - hi.astype(F32)
    mid = r.astype(BF16)
    lo = (r - mid.astype(F32)).astype(BF16)
    return hi, mid, lo


def _dot3(a, b, dims=NN, split=0):
    if split == 0:
        bb = b.astype(BF16)
        parts = [lax.dot_general(s, bb, dims, preferred_element_type=F32) for s in _split3(a)]
    else:
        aa = a.astype(BF16)
        parts = [lax.dot_general(aa, s, dims, preferred_element_type=F32) for s in _split3(b)]
    return parts[0] + parts[1] + parts[2]


@jax.custom_vjp
def _spread(v, e):
    return _dot3(v, e)


def _spread_fwd(v, e):
    return _dot3(v, e), e


def _spread_bwd(e, g):
    return _dot3(g, e, NT), jnp.zeros_like(e)


_spread.defvjp(_spread_fwd, _spread_bwd)


@jax.custom_vjp
def _running_sum(a, lower):
    return _dot3(lower, a, NN, split=1)


def _running_sum_fwd(a, lower):
    return _dot3(lower, a, NN, split=1), lower


def _running_sum_bwd(lower, g):
    return _dot3(lower, g, TN, split=1), jnp.zeros_like(lower)


_running_sum.defvjp(_running_sum_fwd, _running_sum_bwd)


def _tile(n, cap):
    if n <= cap:
        return n
    best = None
    for t in range(LANE, cap + 1, LANE):
        if n % t == 0:
            best = t
    assert best is not None, (n, cap)
    return best


def _params(sem):
    return pltpu.CompilerParams(dimension_semantics=sem, vmem_limit_bytes=VMEM_LIMIT)


def _round_up(n, m):
    return -(-n // m) * m


class _Order:
    tokens = []
    last = None

    @classmethod
    def take(cls):
        out, cls.tokens = cls.tokens, []
        return out

    @classmethod
    def done(cls, result):
        cls.last = result[0] if isinstance(result, (list, tuple)) else result
        return result


ANY_SPEC = pl.BlockSpec(memory_space=pl.ANY)


def _mm(name, a, b, mode, out_dtype=F32, res=None, scale=1.0,
        cap_m=1408, cap_n=1408, cap_k=1408):
    segs = list(a) if isinstance(a, (list, tuple)) else [a]
    nseg = len(segs)
    if mode == "tn":
        k = segs[0].shape[0]
        widths = [s.shape[1] for s in segs]
        m = sum(widths)
        k2, n = b.shape
        tm = _tile(math.gcd(*widths), cap_m)
        tk = _tile(k, cap_k)
        counts = [wd // tm for wd in widths]
    else:
        m = segs[0].shape[0]
        widths = [s.shape[1] for s in segs]
        k = sum(widths)
        (k2, n) = b.shape if mode == "nn" else b.shape[::-1]
        tm = _tile(m, cap_m)
        tk = _tile(math.gcd(*widths), cap_k)
        counts = [wd // tk for wd in widths]
    assert k == k2, (name, [s.shape for s in segs], b.shape, mode)
    tn = _tile(n, cap_n)
    nk = k // tk
    starts = [sum(counts[:s]) for s in range(nseg)]
    dims = {"nn": NN, "nt": NT, "tn": TN}[mode]

    def a_spec(s):
        lo, cnt = starts[s], counts[s]
        if mode == "tn":
            if nseg == 1:
                return pl.BlockSpec((tk, tm), lambda i, j, kk: (kk, i))
            return pl.BlockSpec(
                (tk, tm), lambda i, j, kk: (jnp.where((i >= lo) & (i < lo + cnt), kk, 0),
                                            jnp.clip(i - lo, 0, cnt - 1)))
        if nseg == 1:
            return pl.BlockSpec((tm, tk), lambda i, j, kk: (i, kk))
        return pl.BlockSpec((tm, tk), lambda i, j, kk: (i, jnp.clip(kk - lo, 0, cnt - 1)))

    b_spec = (pl.BlockSpec((tn, tk), lambda i, j, kk: (j, kk)) if mode == "nt"
              else pl.BlockSpec((tk, tn), lambda i, j, kk: (kk, j)))
    o_spec = pl.BlockSpec((tm, tn), lambda i, j, kk: (i, j))
    has_res = res is not None
    use_acc = nk > 1 or nseg > 1
    ties = _Order.take()
    nt_ = len(ties)

    def body(*refs):
        a_refs = refs[:nseg]
        b_ref = refs[nseg]
        r_ref = refs[nseg + 1] if has_res else None
        o_ref = refs[nseg + 1 + has_res + nt_]
        scr = refs[nseg + 2 + has_res + nt_:]

        def finish(acc):
            if scale != 1.0:
                acc = acc * scale
            if has_res:
                acc = r_ref[...].astype(F32) + acc
            o_ref[...] = acc.astype(o_ref.dtype)

        if not use_acc:
            finish(_dot(a_refs[0][...], b_ref[...], dims))
            return
        acc_ref = scr[0]
        kk = pl.program_id(2)
        sel = pl.program_id(0) if mode == "tn" else kk

        @pl.when(kk == 0)
        def _():
            acc_ref[...] = jnp.zeros_like(acc_ref)

        for s in range(nseg):
            def add(s=s):
                acc_ref[...] += _dot(a_refs[s][...], b_ref[...], dims)
            if nseg == 1:
                add()
            else:
                pl.when((sel >= starts[s]) & (sel < starts[s] + counts[s]))(add)

        @pl.when(kk == nk - 1)
        def _():
            finish(acc_ref[...])

    in_specs = ([a_spec(s) for s in range(nseg)] + [b_spec] + ([o_spec] if has_res else [])
                + [ANY_SPEC] * nt_)
    args = tuple(segs) + (b,) + ((res,) if has_res else ()) + tuple(ties)
    return _Order.done(pl.pallas_call(
        body, name=name,
        out_shape=jax.ShapeDtypeStruct((m, n), out_dtype),
        grid=(m // tm, n // tn, nk),
        in_specs=in_specs, out_specs=o_spec,
        scratch_shapes=[pltpu.VMEM((tm, tn), F32)] if use_acc else [],
        compiler_params=_params(("parallel", "parallel", "arbitrary")),
    )(*args))


def _rw(name, fn, ins, outs, accs=(), tr=256, ncb=1):
    t = next(a.shape[0] for kind, a, _, _ in ins if kind == "row")
    assert t % tr == 0
    n_in = len(ins)
    n_pieces = sum(len(w) for w, _ in outs)

    def spec(kind, arr, width, base):
        if kind == "row":
            return pl.BlockSpec((tr, width), lambda j, i: (i, base + j))
        return pl.BlockSpec((arr.shape[0], width), lambda j, i: (0, base + j))

    in_specs = [spec(*s) for s in ins]
    out_shapes, out_specs = [], []
    for widths, dt in outs:
        w = sum(widths)
        out_shapes.append(jax.ShapeDtypeStruct((t, w * ncb), dt))
        out_specs.append(pl.BlockSpec((tr, w), lambda j, i: (i, j)))
    for rows, width in accs:
        out_shapes.append(jax.ShapeDtypeStruct((rows, width * ncb), F32))
        out_specs.append(pl.BlockSpec((rows, width), lambda j, i: (0, j)))

    ties = _Order.take()
    nt_ = len(ties)
    in_specs = in_specs + [ANY_SPEC] * nt_

    def body(*refs):
        vals = [r[...] for r in refs[:n_in]]
        res = fn(*vals)
        o_refs = refs[n_in + nt_:n_in + nt_ + len(outs)]
        a_refs = refs[n_in + nt_ + len(outs):]
        p = 0
        for (widths, _), o_ref in zip(outs, o_refs):
            off = 0
            for w in widths:
                if len(widths) == 1:
                    o_ref[...] = res[p].astype(o_ref.dtype)
                else:
                    o_ref[:, off:off + w] = res[p].astype(o_ref.dtype)
                off += w
                p += 1
        i = pl.program_id(1)
        for a_ref, v in zip(a_refs, res[n_pieces:]):
            @pl.when(i == 0)
            def _(a_ref=a_ref, v=v):
                a_ref[...] = v

            @pl.when(i > 0)
            def _(a_ref=a_ref, v=v):
                a_ref[...] += v

    return _Order.done(pl.pallas_call(
        body, name=name, out_shape=out_shapes,
        grid=(ncb, t // tr), in_specs=in_specs, out_specs=out_specs,
        compiler_params=_params(("parallel", "arbitrary")),
    )(*[a for _, a, _, _ in ins], *ties))


def _rms(x, g):
    x = x.astype(F32)
    return x * lax.rsqrt(jnp.mean(x * x, axis=-1, keepdims=True) + EPS) * g


def _silu(x):
    return x * jax.nn.sigmoid(x)


def _colsum(v):
    return jnp.sum(v, axis=0, keepdims=True)


def _pair_norm(x, g):
    w = 2 * HD
    ri = lax.broadcasted_iota(jnp.int32, (w, w), 0)
    ci = lax.broadcasted_iota(jnp.int32, (w, w), 1)
    same_head = ((ri < HD) == (ci < HD)).astype(F32)
    ms = _spread(x * x, same_head) * (1.0 / HD)
    return x * lax.rsqrt(ms + EPS) * g


def _attn_pair(qn, kcn, kpn, vc, vp, coef0, coef1, first):
    w = 2 * HD
    lane = lax.broadcasted_iota(jnp.int32, (1, w), 1)
    scale = 1.0 / math.sqrt(HD)
    a_idx = lax.broadcasted_iota(jnp.int32, (QB, QB), 0)
    c_idx = lax.broadcasted_iota(jnp.int32, (QB, QB), 1)
    rel_c = a_idx - c_idx
    dist_c = rel_c.astype(F32)
    dist_p = dist_c + float(QB)
    keep_c = rel_c >= 0
    keep_p = jnp.logical_and(rel_c <= 0, jnp.logical_not(first))
    out = jnp.zeros((QB, w), F32)
    lb = jnp.zeros((QB, w), F32)
    for hh, coef in enumerate((coef0, coef1)):
        mask = ((lane < HD) if hh == 0 else (lane >= HD)).astype(F32)
        qm = qn * mask
        lc = jnp.where(keep_c, _dot(qm, kcn, NT) * scale - coef * dist_c, NEG)
        lp = jnp.where(keep_p, _dot(qm, kpn, NT) * scale - coef * dist_p, NEG)
        m = lax.stop_gradient(jnp.maximum(jnp.max(lc, axis=-1, keepdims=True),
                                          jnp.max(lp, axis=-1, keepdims=True)))
        pc = jnp.exp(lc - m)
        pp = jnp.exp(lp - m)
        l = jnp.sum(pc, axis=-1, keepdims=True) + jnp.sum(pp, axis=-1, keepdims=True)
        inv = 1.0 / l
        out = out + (_dot(pc * inv, vc) + _dot(pp * inv, vp)) * mask
        lb = lb + (m + jnp.log(l)) * mask
    return out, lb


NORM_ROWS = 128
ATTN_UNROLL_FWD = 4
ATTN_UNROLL_BWD = 2


def _unit_rows(u, d):
    r = u & (d - 1)
    n = u >> (d.bit_length() - 1)

    def rows(blk):
        start = pl.multiple_of(blk * (QB * d), QB * d)
        return pl.ds(start, QB) if d == 1 else pl.ds(start + r, QB, stride=d)

    return rows(n), rows(jnp.maximum(n - 1, 0)), n == 0


def _attn_prologue(t, q_ref, k_ref, v_ref, qg_ref, kg_ref, qf, kf, vf):
    def chunk(c, carry):
        rows = pl.ds(pl.multiple_of(c * NORM_ROWS, NORM_ROWS), NORM_ROWS)
        qf[rows, :] = _pair_norm(q_ref[rows, :].astype(F32), qg_ref[...])
        kf[rows, :] = _pair_norm(k_ref[rows, :].astype(F32), kg_ref[...])
        vf[rows, :] = v_ref[rows, :].astype(F32)
        return carry
    lax.fori_loop(0, t // NORM_ROWS, chunk, 0)


def _attn_specs(t, bases):
    w = 2 * HD
    ins = [pl.BlockSpec((t, w), functools.partial(lambda p, c, b: (0, b + p), b=b)) for b in bases]
    gain = pl.BlockSpec((1, w), lambda p, c: (0, 0))
    blk = pl.BlockSpec((t, w), lambda p, c: (0, p))
    return ins, gain, blk


def _attn_fwd(name, proj, bases, qg, kg, coefs, d):
    t = proj.shape[0]
    npairs = coefs.shape[0] // 2
    w = 2 * HD
    ins, gain, blk = _attn_specs(t, bases)

    def body(coef_ref, q_ref, k_ref, v_ref, qg_ref, kg_ref, o_ref, l_ref, qf, kf, vf):
        p = pl.program_id(0)
        c0, c1 = coef_ref[2 * p], coef_ref[2 * p + 1]
        _attn_prologue(t, q_ref, k_ref, v_ref, qg_ref, kg_ref, qf, kf, vf)

        def step(u, carry):
            cur, prv, first = _unit_rows(u, d)
            o, lb = _attn_pair(qf[cur, :], kf[cur, :], kf[prv, :], vf[cur, :], vf[prv, :],
                               c0, c1, first)
            o_ref[cur, :] = o
            l_ref[cur, :] = lb
            return carry

        lax.fori_loop(0, t // QB, step, 0, unroll=ATTN_UNROLL_FWD)

    return pl.pallas_call(
        body, name=name,
        out_shape=[jax.ShapeDtypeStruct((t, npairs * w), F32)] * 2,
        grid_spec=pltpu.PrefetchScalarGridSpec(
            num_scalar_prefetch=1, grid=(npairs,),
            in_specs=ins + [gain, gain], out_specs=[blk, blk],
            scratch_shapes=[pltpu.VMEM((t, w), F32)] * 3),
        compiler_params=_params(("arbitrary",)),
    )(coefs, proj, proj, proj, qg, kg)


def _attn_bwd(name, proj, bases, qg, kg, coefs, d, do, dl):
    t = proj.shape[0]
    npairs = coefs.shape[0] // 2
    w = 2 * HD
    ins, gain, blk = _attn_specs(t, bases)

    def body(coef_ref, q_ref, k_ref, v_ref, qg_ref, kg_ref, do_ref, dl_ref,
             dq_ref, dk_ref, dv_ref, dqg_ref, dkg_ref, qf, kf, vf, dqf, dkf, dvf):
        p = pl.program_id(0)
        c0, c1 = coef_ref[2 * p], coef_ref[2 * p + 1]
        _attn_prologue(t, q_ref, k_ref, v_ref, qg_ref, kg_ref, qf, kf, vf)
        dkf[...] = jnp.zeros_like(dkf)
        dvf[...] = jnp.zeros_like(dvf)

        def step(u, carry):
            cur, prv, first = _unit_rows(u, d)
            f = lambda a, b, c, e, g: _attn_pair(a, b, c, e, g, c0, c1, first)
            _, vjp = jax.vjp(f, qf[cur, :], kf[cur, :], kf[prv, :], vf[cur, :], vf[prv, :])
            dq, dkc, dkp, dvc, dvp = vjp((do_ref[cur, :], dl_ref[cur, :]))
            dqf[cur, :] = dq
            dkf[cur, :] += dkc
            dkf[prv, :] += dkp
            dvf[cur, :] += dvc
            dvf[prv, :] += dvp
            return carry

        lax.fori_loop(0, t // QB, step, 0, unroll=ATTN_UNROLL_BWD)

        def chunk(c, carry):
            dqg_acc, dkg_acc = carry
            rows = pl.ds(pl.multiple_of(c * NORM_ROWS, NORM_ROWS), NORM_ROWS)
            _, vq = jax.vjp(_pair_norm, q_ref[rows, :].astype(F32), qg_ref[...])
            dq, dqg = vq(dqf[rows, :])
            _, vk = jax.vjp(_pair_norm, k_ref[rows, :].astype(F32), kg_ref[...])
            dk, dkg = vk(dkf[rows, :])
            dq_ref[rows, :] = dq.astype(dq_ref.dtype)
            dk_ref[rows, :] = dk.astype(dk_ref.dtype)
            dv_ref[rows, :] = dvf[rows, :].astype(dv_ref.dtype)
            return dqg_acc + dqg, dkg_acc + dkg

        zero = jnp.zeros((1, w), F32)
        dqg, dkg = lax.fori_loop(0, t // NORM_ROWS, chunk, (zero, zero))

        @pl.when(p == 0)
        def _():
            dqg_ref[...] = dqg
            dkg_ref[...] = dkg

        @pl.when(p > 0)
        def _():
            dqg_ref[...] += dqg
            dkg_ref[...] += dkg

    big = jax.ShapeDtypeStruct((t, npairs * w), ACT_DTYPE)
    small = jax.ShapeDtypeStruct((1, w), F32)
    return pl.pallas_call(
        body, name=name,
        out_shape=[big, big, big, small, small],
        grid_spec=pltpu.PrefetchScalarGridSpec(
            num_scalar_prefetch=1, grid=(npairs,),
            in_specs=ins + [gain, gain, blk, blk],
            out_specs=[blk, blk, blk, gain, gain],
            scratch_shapes=[pltpu.VMEM((t, w), F32)] * 6),
        compiler_params=_params(("arbitrary",)),
    )(coefs, proj, proj, proj, qg, kg, do, dl)


def _shift_down(u, s):
    if s == 0:
        return u
    rows = lax.broadcasted_iota(jnp.int32, u.shape, 0)
    return jnp.where(rows >= s, pltpu.roll(u, s, 0), 0.0)


def _shift_up(u, s):
    if s == 0:
        return u
    t = u.shape[0]
    rows = lax.broadcasted_iota(jnp.int32, u.shape, 0)
    return jnp.where(rows < t - s, pltpu.roll(u, t - s, 0), 0.0)


def _conv_pre(u, w, b):
    y = b
    for kk in range(SSD_K):
        y = y + w[kk:kk + 1, :] * _shift_down(u, SSD_K - 1 - kk)
    return y


def _conv_fwd(name, src, base, w, b, cw=128):
    t = src.shape[0]
    c = w.shape[1]

    def body(u_ref, w_ref, b_ref, o_ref):
        y = _conv_pre(u_ref[...].astype(F32), w_ref[...], b_ref[...])
        o_ref[...] = _silu(y).astype(o_ref.dtype)

    return pl.pallas_call(
        body, name=name, out_shape=jax.ShapeDtypeStruct((t, c), ACT_DTYPE),
        grid=(c // cw,),
        in_specs=[pl.BlockSpec((t, cw), lambda j: (0, base + j)),
                  pl.BlockSpec((SSD_K, cw), lambda j: (0, j)),
                  pl.BlockSpec((1, cw), lambda j: (0, j))],
        out_specs=pl.BlockSpec((t, cw), lambda j: (0, j)),
        compiler_params=_params(("parallel",)),
    )(src, w, b)


def _conv_bwd(name, src, base, w, b, dout, cw=128):
    t = src.shape[0]
    c = w.shape[1]

    def body(u_ref, w_ref, b_ref, d_ref, du_ref, dw_ref, db_ref):
        u = u_ref[...].astype(F32)
        wv = w_ref[...]
        y = _conv_pre(u, wv, b_ref[...])
        sg = jax.nn.sigmoid(y)
        dy = d_ref[...].astype(F32) * (sg * (1.0 + y * (1.0 - sg)))
        du = jnp.zeros_like(u)
        for kk in range(SSD_K):
            s = SSD_K - 1 - kk
            du = du + wv[kk:kk + 1, :] * _shift_up(dy, s)
            dw_ref[kk:kk + 1, :] = _colsum(dy * _shift_down(u, s))
        du_ref[...] = du.astype(du_ref.dtype)
        db_ref[...] = _colsum(dy)

    return pl.pallas_call(
        body, name=name,
        out_shape=[jax.ShapeDtypeStruct((t, c), ACT_DTYPE),
                   jax.ShapeDtypeStruct((SSD_K, c), F32),
                   jax.ShapeDtypeStruct((1, c), F32)],
        grid=(c // cw,),
        in_specs=[pl.BlockSpec((t, cw), lambda j: (0, base + j)),
                  pl.BlockSpec((SSD_K, cw), lambda j: (0, j)),
                  pl.BlockSpec((1, cw), lambda j: (0, j)),
                  pl.BlockSpec((t, cw), lambda j: (0, j))],
        out_specs=[pl.BlockSpec((t, cw), lambda j: (0, j)),
                   pl.BlockSpec((SSD_K, cw), lambda j: (0, j)),
                   pl.BlockSpec((1, cw), lambda j: (0, j))],
        compiler_params=_params(("parallel",)),
    )(src, w, b, dout)


def _softplus(x):
    return jnp.maximum(x, 0.0) + jnp.log(1.0 + jnp.exp(-jnp.abs(x)))


def _ssd_chunk(xbc, dtraw, bias, alog, states):
    wd = states[0].shape[1]
    nj = wd // SSD_P
    inner = SSD_G * wd
    dt = _softplus(dtraw + bias)
    a = dt * (-jnp.exp(alog))
    li = lax.broadcasted_iota(jnp.int32, (SSD_Q, SSD_Q), 0)
    si = lax.broadcasted_iota(jnp.int32, (SSD_Q, SSD_Q), 1)
    causal = li >= si
    acs = _running_sum(a, causal.astype(F32))
    acs_t = acs.T
    a_last = acs[SSD_Q - 1:SSD_Q, :]
    grow = jnp.exp(acs)
    shrink = jnp.exp(a_last - acs)
    hrow = lax.broadcasted_iota(jnp.int32, (LANE, wd), 0)
    wcol = lax.broadcasted_iota(jnp.int32, (LANE, wd), 1)
    lane = lax.broadcasted_iota(jnp.int32, (1, LANE), 1)
    ys, snext = [], []
    for g in range(SSD_G):
        lo = (hrow - g * nj) * SSD_P
        head_lanes = jnp.logical_and(wcol >= lo, wcol < lo + SSD_P).astype(F32)
        xs = xbc[:, g * wd:(g + 1) * wd]
        bm = xbc[:, inner + g * SSD_N:inner + (g + 1) * SSD_N]
        cm = xbc[:, inner + (SSD_G + g) * SSD_N:inner + (SSD_G + g + 1) * SSD_N]
        xdt = xs * _spread(dt, head_lanes)
        grow_x = _spread(grow, head_lanes)
        y_off = _dot(cm, states[g]) * grow_x
        s_new = (states[g] * grow_x[SSD_Q - 1:SSD_Q, :]
                 + _dot(bm, xdt * _spread(shrink, head_lanes), TN))
        cb = _dot(cm, bm, NT)
        pieces = []
        for i in range(wd // LANE):
            xp = xdt[:, i * LANE:(i + 1) * LANE]
            acc = jnp.zeros((SSD_Q, LANE), F32)
            for hh in range(LANE // SSD_P):
                h = g * nj + i * (LANE // SSD_P) + hh
                decay = jnp.exp(jnp.where(causal, acs[:, h:h + 1] - acs_t[h:h + 1, :], NEG))
                keep = jnp.logical_and(lane >= hh * SSD_P, lane < (hh + 1) * SSD_P).astype(F32)
                acc = acc + _dot(cb * decay, xp * keep)
            pieces.append(acc)
        y_diag = pieces[0] if len(pieces) == 1 else jnp.concatenate(pieces, axis=1)
        ys.append(y_diag + y_off)
        snext.append(s_new)
    return ys, snext


def _ssd_specs(cdim, wd, rev, nc):
    ch = (lambda c: nc - 1 - c) if rev else (lambda c: c)
    full = lambda width: pl.BlockSpec((SSD_Q, width), lambda c: (ch(c), 0))
    vec = pl.BlockSpec((1, LANE), lambda c: (0, 0))
    st = pl.BlockSpec((1, SSD_G, SSD_N, wd), lambda c: (ch(c), 0, 0, 0))
    return full, vec, st


def _ssd_fwd(name, xbc, dtraw, bias, alog, inner):
    t, cdim = xbc.shape
    wd = inner // SSD_G
    nc = t // SSD_Q
    full, vec, st = _ssd_specs(cdim, wd, False, nc)

    def body(x_ref, r_ref, b_ref, a_ref, y_ref, st_ref, s_scr):
        @pl.when(pl.program_id(0) == 0)
        def _():
            s_scr[...] = jnp.zeros_like(s_scr)

        sprev = [s_scr[g] for g in range(SSD_G)]
        ys, snext = _ssd_chunk(x_ref[...].astype(F32), r_ref[...], b_ref[...], a_ref[...], sprev)
        for g in range(SSD_G):
            st_ref[0, g] = sprev[g]
            y_ref[:, g * wd:(g + 1) * wd] = ys[g]
            s_scr[g] = snext[g]

    return pl.pallas_call(
        body, name=name,
        out_shape=[jax.ShapeDtypeStruct((t, inner), F32),
                   jax.ShapeDtypeStruct((nc, SSD_G, SSD_N, wd), F32)],
        grid=(nc,),
        in_specs=[full(cdim), full(LANE), vec, vec],
        out_specs=[full(inner), st],
        scratch_shapes=[pltpu.VMEM((SSD_G, SSD_N, wd), F32)],
        compiler_params=_params(("arbitrary",)),
    )(xbc, dtraw, bias, alog)


def _ssd_bwd(name, xbc, dtraw, bias, alog, states, dy, dxs_extra):
    t, cdim = xbc.shape
    inner = dy.shape[1]
    wd = inner // SSD_G
    nc = t // SSD_Q
    full, vec, st = _ssd_specs(cdim, wd, True, nc)

    def body(x_ref, r_ref, b_ref, a_ref, st_ref, dy_ref, dx0_ref,
             dx_ref, dr_ref, db_ref, da_ref, ds_scr):
        first = pl.program_id(0) == 0

        @pl.when(first)
        def _():
            ds_scr[...] = jnp.zeros_like(ds_scr)

        sprev = [st_ref[0, g] for g in range(SSD_G)]
        _, vjp = jax.vjp(_ssd_chunk, x_ref[...].astype(F32), r_ref[...], b_ref[...], a_ref[...],
                         sprev)
        dyv = dy_ref[...]
        dys = [dyv[:, g * wd:(g + 1) * wd] for g in range(SSD_G)]
        dsn = [ds_scr[g] for g in range(SSD_G)]
        dx, dr, db, da, dsp = vjp((dys, dsn))
        dx_ref[:, :inner] = dx[:, :inner] + dx0_ref[...].astype(F32)
        dx_ref[:, inner:] = dx[:, inner:]
        dr_ref[...] = dr
        for g in range(SSD_G):
            ds_scr[g] = dsp[g]

        @pl.when(first)
        def _():
            db_ref[...] = db
            da_ref[...] = da

        @pl.when(jnp.logical_not(first))
        def _():
            db_ref[...] += db
            da_ref[...] += da

    return pl.pallas_call(
        body, name=name,
        out_shape=[jax.ShapeDtypeStruct((t, cdim), F32),
                   jax.ShapeDtypeStruct((t, LANE), F32),
                   jax.ShapeDtypeStruct((1, LANE), F32),
                   jax.ShapeDtypeStruct((1, LANE), F32)],
        grid=(nc,),
        in_specs=[full(cdim), full(LANE), vec, vec, st, full(inner), full(inner)],
        out_specs=[full(cdim), full(LANE), vec, vec],
        scratch_shapes=[pltpu.VMEM((SSD_G, SSD_N, wd), F32)],
        compiler_params=_params(("arbitrary",)),
    )(xbc, dtraw, bias, alog, states, dy, dxs_extra)


def _act(g, u):
    return _silu(g.astype(F32)) * u.astype(F32)


def _mix(o0, o1, o2, l0, l1, l2):
    m = lax.stop_gradient(jnp.maximum(jnp.maximum(l0, l1), l2))
    e0, e1, e2 = jnp.exp(l0 - m), jnp.exp(l1 - m), jnp.exp(l2 - m)
    return (e0 * o0 + e1 * o1 + e2 * o2) / (e0 + e1 + e2)


def _gate(y, xs, z, dexp, gain):
    v = (y + xs.astype(F32) * dexp) * _silu(z.astype(F32))
    return _rms(v, gain)


def _merge(ga, gs, ap, sp):
    return jax.nn.sigmoid(ga.astype(F32)) * ap + jax.nn.sigmoid(gs.astype(F32)) * sp


def _alibi_coefs(hp):
    n = hp * len(PATTERNS)
    slopes = np.exp2(-ALIBI_MAX_EXP * np.arange(1, n + 1, dtype=np.float32) / n).astype(np.float32)
    return [jnp.asarray(slopes[g * hp:(g + 1) * hp] * np.float32(d))
            for g, (_, d) in enumerate(PATTERNS)]


def _local_step(x, tgt, w, p, gw_=None, aw=None):
    t, d = x.shape
    dff = w["gu1t"].shape[0] // 2
    aw = w["abt"].shape[1] if aw is None else aw
    hp = aw // HD
    qkv = len(PATTERNS) * aw
    inner = p["ssd_norm"].shape[1]
    nh = p["dt_bias"].shape[1]
    gw_ = {} if gw_ is None else gw_
    gw = inner // SSD_G
    cdim = inner + 2 * SSD_G * SSD_N
    z_off, xbc_off = 3 * qkv, 3 * qkv + inner
    ga_off = xbc_off + cdim
    gs_off = ga_off + d
    hw = d // 2
    assert z_off % gw == 0 and xbc_off % LANE == 0 and ga_off % hw == 0 and gs_off % hw == 0
    assert (nh // SSD_G) * SSD_P == gw and hp % 2 == 0 and aw % LANE == 0 and nh <= LANE
    gdt = MXU_DTYPE

    row = lambda a, width, base=0: ("row", a, width, base)
    const = lambda a, width, base=0: ("const", a, width, base)

    def rms_fwd(name, xin, g):
        return _rw(name, lambda xv, gv: (_rms(xv, gv),), [row(xin, d), const(g, d)],
                   [((d,), ACT_DTYPE)])[0]

    def rms_bwd(name, xin, g, dh, dres):
        def fn(xv, gv, dhv, drv):
            _, vjp = jax.vjp(_rms, xv, gv)
            dx, dg = vjp(dhv.astype(F32))
            return drv + dx, dg
        return _rw(name, fn, [row(xin, d), const(g, d), row(dh, d), row(dres, d)],
                   [((d,), F32)], accs=[(1, d)])

    def ffn_fwd(tag, xin, g, key_gu, key_d):
        h = rms_fwd(tag + "_norm", xin, g)
        gu = _mm(tag + "_up", h, w[key_gu], "nt", ACT_DTYPE)
        a = _rw(tag + "_act", lambda gv, uv: (_act(gv, uv),),
                [row(gu, dff, 0), row(gu, dff, 1)], [((dff,), ACT_DTYPE)])[0]
        xo = _mm(tag + "_down", a, w[key_d], "nn", F32, res=xin, scale=0.5)
        return xo, (h, gu, a)

    def ffn_bwd(tag, xin, g, wgut, wd, saved, dxo, key_gu, key_d):
        h, gu, a = saved
        da = _mm(tag + "_da", dxo, wd, "nt", ACT_DTYPE, scale=0.5)
        gw_[key_d] = _mm(tag + "_dwd", a, dxo, "tn", gdt, scale=0.5)

        def fn(gv, uv, dav):
            _, vjp = jax.vjp(_act, gv, uv)
            return vjp(dav.astype(F32))
        dgu = _rw(tag + "_dact", fn, [row(gu, dff, 0), row(gu, dff, 1), row(da, dff)],
                  [((dff, dff), ACT_DTYPE)])[0]
        gw_[key_gu] = _mm(tag + "_dwgu", dgu, h, "tn", gdt)
        dh = _mm(tag + "_dh", dgu, wgut, "nn", F32)
        return rms_bwd(tag + "_dnorm", xin, g, dh, dxo)

    x1, ffn1_saved = ffn_fwd("ffn1", x, p["ffn1_norm"], "gu1t", "d1")
    h2 = rms_fwd("mix_norm", x1, p["mix_norm"])
    proj = _mm("in_proj", h2, w["maint"], "nt", ACT_DTYPE, cap_m=512, cap_n=2944)
    dtraw = _mm("dt_proj", h2, w["dtt"], "nt", F32)

    coefs = _alibi_coefs(hp)
    qg2 = jnp.concatenate([p["q_norm"], p["q_norm"]], axis=1)
    kg2 = jnp.concatenate([p["k_norm"], p["k_norm"]], axis=1)
    pw = 2 * HD
    attn_bases = [[(off + gi * aw) // pw for off in (0, qkv, 2 * qkv)]
                  for gi in range(len(PATTERNS))]
    attn_o, attn_l = [], []
    for gi, (_, dil) in enumerate(PATTERNS):
        o, l = _attn_fwd(f"attn_fwd{gi}", proj, attn_bases[gi], qg2, kg2, coefs[gi], dil)
        attn_o.append(o)
        attn_l.append(l)
    ao = _rw("attn_mix", lambda *v: (_mix(*v),), [row(a, aw) for a in attn_o + attn_l],
             [((aw,), ACT_DTYPE)])[0]

    xbc = _conv_fwd("conv_fwd", proj, xbc_off // LANE, p["conv_w"], p["conv_b"])
    pad = lambda v: jnp.pad(v, ((0, 0), (0, LANE - nh)))
    bias_p, alog_p = pad(p["dt_bias"]), pad(p["a_log"])
    yssd, states = _ssd_fwd("ssd_fwd", xbc, dtraw, bias_p, alog_p, inner)
    dexp = jnp.repeat(p["d_skip"], SSD_P, axis=1)
    gate_ins = [row(yssd, gw), row(xbc, gw), row(proj, gw, z_off // gw),
                const(dexp, gw), const(p["ssd_norm"], gw)]
    yn = _rw("ssd_gate", lambda *v: (_gate(*v),), gate_ins, [((gw,), ACT_DTYPE)], ncb=SSD_G)[0]

    ap = _mm("attn_out", ao, w["abt"], "nt", F32)
    sp = _mm("ssd_out", yn, w["sb"], "nn", F32)
    merge_ins = [row(proj, hw, ga_off // hw), row(proj, hw, gs_off // hw), row(ap, hw), row(sp, hw)]
    mg = _rw("merge", lambda *v: (_merge(*v),), merge_ins, [((hw,), ACT_DTYPE)], ncb=2)[0]
    x2 = _mm("mix_out", mg, w["out"], "nn", F32, res=x1)
    x3, ffn2_saved = ffn_fwd("ffn2", x2, p["ffn2_norm"], "gu2t", "d2")

    def loss_fn(yv, tv):
        e = yv - tv
        return e * (1.0 / d), _colsum(e * e)
    dy, loss_vec = _rw("loss", loss_fn, [row(x3, d), row(tgt, d)], [((d,), F32)], accs=[(1, d)])

    gp = {}
    dx2, gp["ffn2_norm"] = ffn_bwd(
        "ffn2", x2, p["ffn2_norm"], w["gu2t"], w["d2"], ffn2_saved, dy, "gu2t", "d2")
    dmg = _mm("d_merge", dx2, w["out"], "nt", ACT_DTYPE)
    gw_["out"] = _mm("dw_out", mg, dx2, "tn", gdt)

    def merge_bwd(gav, gsv, apv, spv, dv):
        _, vjp = jax.vjp(_merge, gav, gsv, apv, spv)
        return vjp(dv.astype(F32))
    dga, dgs, dap, dsp = _rw("d_merge_gate", merge_bwd, merge_ins + [row(dmg, hw)],
                             [((hw,), ACT_DTYPE)] * 4, ncb=2)
    gw_["abt"] = _mm("dw_ab", dap, ao, "tn", gdt)
    dao = _mm("d_attn_o", dap, w["abt"], "nn", F32)
    gw_["sb"] = _mm("dw_sb", yn, dsp, "tn", gdt)
    dyn = _mm("d_ssd_y", dsp, w["sb"], "nt", F32)

    def gate_bwd(yv, xv, zv, dev, gv, dv):
        _, vjp = jax.vjp(_gate, yv, xv, zv, dev, gv)
        return vjp(dv)
    dyssd, dxs_gate, dz, ddexp, gp["ssd_norm"] = _rw(
        "d_ssd_gate", gate_bwd, gate_ins + [row(dyn, gw)],
        [((gw,), F32), ((gw,), F32), ((gw,), ACT_DTYPE)], accs=[(1, gw), (1, gw)], ncb=SSD_G)
    gp["d_skip"] = ddexp.reshape(nh, SSD_P).sum(axis=1).reshape(1, nh)

    dxbc, ddtraw, dbias, dalog = _ssd_bwd("ssd_bwd", xbc, dtraw, bias_p, alog_p, states,
                                          dyssd, dxs_gate)
    gp["dt_bias"], gp["a_log"] = dbias[:, :nh], dalog[:, :nh]
    du, gp["conv_w"], gp["conv_b"] = _conv_bwd("conv_bwd", proj, xbc_off // LANE,
                                               p["conv_w"], p["conv_b"], dxbc)

    def mix_bwd(*v):
        _, vjp = jax.vjp(_mix, *v[:6])
        return vjp(v[6])
    dmix = _rw("d_attn_mix", mix_bwd, [row(a, aw) for a in attn_o + attn_l] + [row(dao, aw)],
               [((aw,), F32)] * 6)
    dq, dk, dv = [], [], []
    dqg = dkg = None
    for gi, (_, dil) in enumerate(PATTERNS):
        r = _attn_bwd(f"attn_bwd{gi}", proj, attn_bases[gi], qg2, kg2, coefs[gi], dil,
                      dmix[gi], dmix[3 + gi])
        dq.append(r[0])
        dk.append(r[1])
        dv.append(r[2])
        dqg = r[3] if dqg is None else dqg + r[3]
        dkg = r[4] if dkg is None else dkg + r[4]
    gp["q_norm"] = dqg[:, :HD] + dqg[:, HD:]
    gp["k_norm"] = dkg[:, :HD] + dkg[:, HD:]

    segs = dq + dk + dv + [dz, du, dga, dgs]
    gw_["maint"] = _mm("dw_in", segs, h2, "tn", gdt)
    gw_["dtt"] = _mm("dw_dt", ddtraw, h2, "tn", gdt)
    dh2 = _mm("d_h2_main", segs, w["maint"], "nn", F32)
    dh2 = _mm("d_h2_dt", ddtraw, w["dtt"], "nn", F32, res=dh2)
    dx1, gp["mix_norm"] = rms_bwd("d_mix_norm", x1, p["mix_norm"], dh2, dx2)
    dx0, gp["ffn1_norm"] = ffn_bwd(
        "ffn1", x, p["ffn1_norm"], w["gu1t"], w["d1"], ffn1_saved, dx1, "gu1t", "d1")
    return loss_vec, dx0, gw_, gp


MESH = pl.DeviceIdType.MESH
HBM_SPEC = pl.BlockSpec(memory_space=pltpu.HBM)


def _mesh_pos():
    return lax.axis_index("x"), lax.axis_index("y"), lax.axis_index("c")


def _flip(pos, k):
    x, y, c = pos
    return (1 - x if k & 4 else x, 1 - y if k & 2 else y, 1 - c if k & 1 else c)


def _dev_index(pos):
    return 4 * pos[0] + 2 * pos[1] + pos[2]


def _rows_of(ref, base, stride, rows, pos):
    start = pl.multiple_of(base + stride * _dev_index(pos), ROW_ALIGN)
    return ref.at[pl.ds(start, rows)]


def _gather(name, shards, dests, out_shapes):
    n = len(shards)
    n_out = len(out_shapes)

    def body(*refs):
        x_refs = refs[:n]
        o_refs = refs[n:n + n_out]
        send_sems, recv_sems, local_sems = refs[n + n_out:]
        me = _mesh_pos()
        sibling = _flip(me, 1)
        chips = [_flip(me, 4), _flip(me, 2), _flip(me, 6)]

        def slot(i, block):
            k_out, base, stride = dests[i]
            return _rows_of(o_refs[k_out], base, stride, shards[i].shape[0], block)

        def copy(i, k, block, to, src=None):
            dst = slot(i, block)
            return pltpu.make_async_remote_copy(
                src_ref=dst if src is None else src, dst_ref=dst,
                send_sem=send_sems.at[7 * i + k], recv_sem=recv_sems.at[7 * i + k],
                device_id=to, device_id_type=MESH)

        mine = [pltpu.make_async_copy(x_refs[i], slot(i, me), local_sems.at[i]) for i in range(n)]
        for cp in mine:
            cp.start()
        first = []
        for i in range(n):
            first.append(copy(i, 0, me, sibling, src=x_refs[i]))
            first += [copy(i, 1 + j, me, chip, src=x_refs[i]) for j, chip in enumerate(chips)]
        for cp in first:
            cp.start()
        passed = []
        for j, chip in enumerate(chips):
            for i in range(n):
                copy(i, 1 + j, chip, me).wait_recv()
                fwd = copy(i, 4 + j, chip, sibling)
                fwd.start()
                passed.append(fwd)
        for i in range(n):
            copy(i, 0, sibling, me).wait_recv()
            for j, chip in enumerate(chips):
                copy(i, 4 + j, _flip(chip, 1), me).wait_recv()
        for cp in first + passed:
            cp.wait_send()
        for cp in mine:
            cp.wait()

    return pl.pallas_call(
        body, name=name,
        out_shape=[jax.ShapeDtypeStruct(s, dt) for s, dt in out_shapes],
        in_specs=[HBM_SPEC] * n, out_specs=[HBM_SPEC] * n_out,
        scratch_shapes=[pltpu.SemaphoreType.DMA((7 * n,)), pltpu.SemaphoreType.DMA((7 * n,)),
                        pltpu.SemaphoreType.DMA((n,))],
    )(*shards)


def _exchange(name, grads, srcs, small):
    n = len(srcs)
    ng = len(grads)

    def body(*refs):
        g_refs = refs[:ng]
        m_ref = refs[ng]
        r_refs = refs[ng + 1:ng + 1 + n]
        s_ref = refs[ng + 1 + n]
        send_sems, recv_sems, local_sems = refs[ng + 2 + n:]
        me = _mesh_pos()
        my = _dev_index(me)

        def slab(i, pos):
            gi, base, stride, rows = srcs[i]
            return _rows_of(g_refs[gi], base, stride, rows, pos)

        own = [pltpu.make_async_copy(slab(i, me), r_refs[i].at[my], local_sems.at[i])
               for i in range(n)]
        own.append(pltpu.make_async_copy(m_ref, s_ref.at[my], local_sems.at[n]))
        for cp in own:
            cp.start()

        def copies(k, src_pos, slot_pos):
            peer = _flip(me, k)
            si = _dev_index(slot_pos)
            out = [pltpu.make_async_remote_copy(
                src_ref=slab(i, src_pos), dst_ref=r_refs[i].at[si],
                send_sem=send_sems.at[7 * i + k - 1], recv_sem=recv_sems.at[7 * i + k - 1],
                device_id=peer, device_id_type=MESH) for i in range(n)]
            out.append(pltpu.make_async_remote_copy(
                src_ref=m_ref, dst_ref=s_ref.at[si],
                send_sem=send_sems.at[7 * n + k - 1], recv_sem=recv_sems.at[7 * n + k - 1],
                device_id=peer, device_id_type=MESH))
            return out

        sent = [cp for k in range(1, NDEV) for cp in copies(k, _flip(me, k), me)]
        for cp in sent:
            cp.start()
        for k in range(1, NDEV):
            for cp in copies(k, me, _flip(me, k)):
                cp.wait_recv()
        for cp in sent:
            cp.wait_send()
        for cp in own:
            cp.wait()

    out_shape = [jax.ShapeDtypeStruct((NDEV, rows, grads[gi].shape[1]), grads[gi].dtype)
                 for gi, _, _, rows in srcs]
    out_shape.append(jax.ShapeDtypeStruct((NDEV,) + small.shape, small.dtype))
    return pl.pallas_call(
        body, name=name, out_shape=out_shape,
        in_specs=[HBM_SPEC] * (ng + 1), out_specs=[HBM_SPEC] * (n + 1),
        scratch_shapes=[pltpu.SemaphoreType.DMA((7 * (n + 1),)),
                        pltpu.SemaphoreType.DMA((7 * (n + 1),)),
                        pltpu.SemaphoreType.DMA((n + 1,))],
    )(*grads, small)


SEM_SPEC = pl.BlockSpec(memory_space=pltpu.SEMAPHORE)
SIDE_EFFECT = pltpu.SideEffectType.DATAFLOW_SIDE_EFFECTING


def _split_refs(plan, srcs, lands, i, src_for, land_from):
    si, sbase, sstride, li, lbase, lstride, rows = plan[i]
    return (_rows_of(srcs[si], sbase, sstride, rows, src_for),
            _rows_of(lands[li], lbase, lstride, rows, land_from))


ALL_PEERS = tuple(range(1, NDEV))
SAME_CORE_AND_SIBLING = (1, 4, 2, 6)
OTHER_CHIPS = (4, 2, 6)


def _split_start(name, srcs, lands, plan, after=(), relations=ALL_PEERS):
    ns, nl, n = len(srcs), len(lands), len(plan)

    def body(*refs):
        s_refs = refs[:ns]
        l_refs = refs[ns:ns + nl]
        send_sems, recv_sems = refs[ns + nl + len(after):ns + nl + len(after) + 2]
        local_sems = refs[ns + nl + len(after) + 2]
        token = refs[ns + nl + len(after) + 3 + ns + nl]
        me = _mesh_pos()
        for i in range(n):
            src, dst = _split_refs(plan, s_refs, l_refs, i, me, me)
            pltpu.make_async_copy(src, dst, local_sems.at[i]).start()
        for k in relations:
            peer = _flip(me, k)
            for i in range(n):
                src, dst = _split_refs(plan, s_refs, l_refs, i, peer, me)
                pltpu.make_async_remote_copy(
                    src_ref=src, dst_ref=dst,
                    send_sem=send_sems.at[7 * i + k - 1], recv_sem=recv_sems.at[7 * i + k - 1],
                    device_id=peer, device_id_type=MESH).start()
        token[...] = jnp.zeros_like(token)

    hbm = lambda a: pltpu.HBM(a.shape, a.dtype)
    out_shape = ((pltpu.SemaphoreType.DMA((7 * n,)), pltpu.SemaphoreType.DMA((7 * n,)),
                  pltpu.SemaphoreType.DMA((n,)))
                 + tuple(hbm(a) for a in srcs) + tuple(hbm(a) for a in lands)
                 + (jax.ShapeDtypeStruct((8, LANE), F32),))
    out = pl.pallas_call(
        body, name=name, out_shape=out_shape,
        in_specs=[HBM_SPEC] * (ns + nl) + [ANY_SPEC] * len(after),
        out_specs=(SEM_SPEC, SEM_SPEC, SEM_SPEC) + (HBM_SPEC,) * (ns + nl)
        + (pl.BlockSpec(memory_space=pltpu.VMEM),),
        input_output_aliases={i: 3 + i for i in range(ns + nl)},
        compiler_params=pltpu.CompilerParams(has_side_effects=SIDE_EFFECT),
    )(*[pltpu.with_memory_space_constraint(a, pltpu.HBM) for a in tuple(srcs) + tuple(lands)],
      *after)
    _Order.tokens.append(out[-1])
    return out[0], out[1], out[2], out[3:3 + ns], out[3 + ns:3 + ns + nl]


def _split_wait(name, started, plan, relations=ALL_PEERS):
    send_sems, recv_sems, local_sems, srcs, lands = started
    ns, nl, n = len(srcs), len(lands), len(plan)
    after = [_Order.last] if _Order.last is not None else []

    def body(*refs):
        s_refs = refs[:ns]
        l_refs = refs[ns:ns + nl]
        send_sems, recv_sems, local_sems = refs[ns + nl:ns + nl + 3]
        me = _mesh_pos()
        for i in range(n):
            src, dst = _split_refs(plan, s_refs, l_refs, i, me, me)
            pltpu.make_async_copy(src, dst, local_sems.at[i]).wait()
        for k in relations:
            peer = _flip(me, k)
            for i in range(n):
                src, dst = _split_refs(plan, s_refs, l_refs, i, peer, peer)
                cp = pltpu.make_async_remote_copy(
                    src_ref=src, dst_ref=dst,
                    send_sem=send_sems.at[7 * i + k - 1], recv_sem=recv_sems.at[7 * i + k - 1],
                    device_id=peer, device_id_type=MESH)
                cp.wait_send()
                cp.wait_recv()

    hbm = lambda a: pltpu.HBM(a.shape, a.dtype)
    out = pl.pallas_call(
        body, name=name,
        out_shape=tuple(hbm(a) for a in srcs) + tuple(hbm(a) for a in lands),
        in_specs=[HBM_SPEC] * (ns + nl) + [SEM_SPEC] * 3 + [ANY_SPEC] * len(after),
        out_specs=(HBM_SPEC,) * (ns + nl),
        input_output_aliases={i: i for i in range(ns + nl)},
        compiler_params=pltpu.CompilerParams(has_side_effects=SIDE_EFFECT),
    )(*srcs, *lands, send_sems, recv_sems, local_sems, *after)
    return list(out[ns:])


def _forward_refs(plan, lands, i, block):
    _, _, _, li, lbase, lstride, rows = plan[i]
    return _rows_of(lands[li], lbase, lstride, rows, block)


def _forward_start(name, lands, plan):
    nl, n = len(lands), len(plan)

    def body(*refs):
        l_refs = refs[:nl]
        send_sems, recv_sems = refs[nl:nl + 2]
        token = refs[nl + 2 + nl]
        me = _mesh_pos()
        for j, kc in enumerate(OTHER_CHIPS):
            for i in range(n):
                rows = _forward_refs(plan, l_refs, i, _flip(me, kc))
                pltpu.make_async_remote_copy(
                    src_ref=rows, dst_ref=rows,
                    send_sem=send_sems.at[3 * i + j], recv_sem=recv_sems.at[3 * i + j],
                    device_id=_flip(me, 1), device_id_type=MESH).start()
        token[...] = jnp.zeros_like(token)

    hbm = lambda a: pltpu.HBM(a.shape, a.dtype)
    out = pl.pallas_call(
        body, name=name,
        out_shape=((pltpu.SemaphoreType.DMA((3 * n,)), pltpu.SemaphoreType.DMA((3 * n,)))
                   + tuple(hbm(a) for a in lands) + (jax.ShapeDtypeStruct((8, LANE), F32),)),
        in_specs=[HBM_SPEC] * nl,
        out_specs=(SEM_SPEC, SEM_SPEC) + (HBM_SPEC,) * nl
        + (pl.BlockSpec(memory_space=pltpu.VMEM),),
        input_output_aliases={i: 2 + i for i in range(nl)},
        compiler_params=pltpu.CompilerParams(has_side_effects=SIDE_EFFECT),
    )(*[pltpu.with_memory_space_constraint(a, pltpu.HBM) for a in lands])
    _Order.tokens.append(out[-1])
    return out[0], out[1], out[2:2 + nl]


def _forward_wait(name, started, plan):
    send_sems, recv_sems, lands = started
    nl, n = len(lands), len(plan)
    after = [_Order.last] if _Order.last is not None else []

    def body(*refs):
        l_refs = refs[:nl]
        send_sems, recv_sems = refs[nl:nl + 2]
        me = _mesh_pos()
        for j, kc in enumerate(OTHER_CHIPS):
            for i in range(n):
                sent = _forward_refs(plan, l_refs, i, _flip(me, kc))
                came = _forward_refs(plan, l_refs, i, _flip(_flip(me, 1), kc))
                cp = pltpu.make_async_remote_copy(
                    src_ref=sent, dst_ref=came,
                    send_sem=send_sems.at[3 * i + j], recv_sem=recv_sems.at[3 * i + j],
                    device_id=_flip(me, 1), device_id_type=MESH)
                cp.wait_send()
                cp.wait_recv()

    hbm = lambda a: pltpu.HBM(a.shape, a.dtype)
    out = pl.pallas_call(
        body, name=name, out_shape=tuple(hbm(a) for a in lands),
        in_specs=[HBM_SPEC] * nl + [SEM_SPEC] * 2 + [ANY_SPEC] * len(after),
        out_specs=(HBM_SPEC,) * nl,
        input_output_aliases={i: i for i in range(nl)},
        compiler_params=pltpu.CompilerParams(has_side_effects=SIDE_EFFECT),
    )(*lands, send_sems, recv_sems, *after)
    return list(out)


def _regroup_rows(name, padded, r, rp, lo, hi):
    d = padded.shape[1]
    pack = 4 // padded.dtype.itemsize
    assert r % pack == 0 and rp % ROW_ALIGN == 0 and lo % (8 * pack) == 0 and hi % (8 * pack) == 0
    r2, rp2, lo2, hi2 = r // pack, rp // pack, lo // pack, hi // pack
    u32 = jnp.uint32

    def body(x_ref, main_ref, cut_ref):
        x = pltpu.bitcast(x_ref[...], u32)
        joined = jnp.concatenate([x[rp2 * j:rp2 * j + r2] for j in range(NDEV)], axis=0)
        main = jnp.concatenate([joined[:lo2], joined[hi2:]], axis=0)
        cut = jnp.concatenate([joined[lo2:hi2], jnp.zeros((LANE // pack - (hi2 - lo2), LANE), u32)],
                              axis=0)
        main_ref[...] = pltpu.bitcast(main, padded.dtype)
        cut_ref[...] = pltpu.bitcast(cut, padded.dtype)

    return pl.pallas_call(
        body, name=name,
        out_shape=[jax.ShapeDtypeStruct((NDEV * r - (hi - lo), d), padded.dtype),
                   jax.ShapeDtypeStruct((LANE, d), padded.dtype)],
        grid=(d // LANE,),
        in_specs=[pl.BlockSpec((NDEV * rp, LANE), lambda i: (0, i))],
        out_specs=[pl.BlockSpec((NDEV * r - (hi - lo), LANE), lambda i: (0, i)),
                   pl.BlockSpec((LANE, LANE), lambda i: (0, i))],
        compiler_params=_params(("parallel",)),
    )(padded)


def _ungroup_rows(name, main, cut, r, rp, lo, hi):
    d = main.shape[1]
    pack = 4 // main.dtype.itemsize
    r2, rp2, lo2, hi2 = r // pack, rp // pack, lo // pack, hi // pack
    u32 = jnp.uint32

    def body(main_ref, cut_ref, o_ref):
        m = pltpu.bitcast(main_ref[...], u32)
        c = pltpu.bitcast(cut_ref[...], u32)
        joined = jnp.concatenate([m[:lo2], c[:hi2 - lo2], m[lo2:]], axis=0)
        zeros = jnp.zeros((rp2 - r2, LANE), u32)
        parts = []
        for j in range(NDEV):
            parts += [joined[r2 * j:r2 * (j + 1)], zeros]
        o_ref[...] = pltpu.bitcast(jnp.concatenate(parts, axis=0), main.dtype)

    return pl.pallas_call(
        body, name=name,
        out_shape=jax.ShapeDtypeStruct((NDEV * rp, d), main.dtype),
        grid=(d // LANE,),
        in_specs=[pl.BlockSpec((main.shape[0], LANE), lambda i: (0, i)),
                  pl.BlockSpec((LANE, LANE), lambda i: (0, i))],
        out_specs=pl.BlockSpec((NDEV * rp, LANE), lambda i: (0, i)),
        compiler_params=_params(("parallel",)),
    )(main, cut)


def _sum_slabs(name, a):
    s, r, c = a.shape

    def body(a_ref, o_ref):
        acc = a_ref[0].astype(F32)
        for i in range(1, s):
            acc = acc + a_ref[i].astype(F32)
        o_ref[...] = acc

    return pl.pallas_call(body, name=name, out_shape=jax.ShapeDtypeStruct((r, c), F32))(a)


def _adamw_update(g, w, m, v):
    mn = ADAM_B1 * m + (1.0 - ADAM_B1) * g
    vn = ADAM_B2 * v + (1.0 - ADAM_B2) * (g * g)
    m_hat = mn / (1.0 - ADAM_B1 ** ADAM_STEP)
    v_hat = vn / (1.0 - ADAM_B2 ** ADAM_STEP)
    delta = -ADAM_LR * (m_hat / (jnp.sqrt(v_hat) + ADAM_EPS) + ADAM_WD * w)
    return delta, mn, vn


def _adamw(name, gsrc, w, m, v, transposed=False, tr=256):
    s = gsrc.shape[0]
    r, c = w.shape
    step = LANE if transposed else 8
    tr = max(t for t in range(step, min(tr, r) + 1, step) if r % t == 0)

    def body(g_ref, w_ref, m_ref, v_ref, go_ref, d_ref, mo_ref, vo_ref):
        g = g_ref[0].astype(F32)
        for i in range(1, s):
            g = g + g_ref[i].astype(F32)
        if transposed:
            g = g.T[:, :c]
        delta, mn, vn = _adamw_update(g, w_ref[...], m_ref[...], v_ref[...])
        go_ref[...] = g
        d_ref[...] = delta
        mo_ref[...] = mn
        vo_ref[...] = vn

    blk = pl.BlockSpec((tr, c), lambda i: (i, 0))
    if transposed:
        g_spec = pl.BlockSpec((s, gsrc.shape[1], tr), lambda i: (0, 0, i))
    else:
        g_spec = pl.BlockSpec((s, tr, c), lambda i: (0, i, 0))
    return pl.pallas_call(
        body, name=name, out_shape=[jax.ShapeDtypeStruct((r, c), F32)] * 4,
        grid=(r // tr,),
        in_specs=[g_spec, blk, blk, blk], out_specs=[blk] * 4,
        compiler_params=_params(("parallel",)),
    )(gsrc, w, m, v)


REPLICATED = ("ffn1_norm", "mix_norm", "q_norm", "k_norm", "conv_b", "dt_bias", "a_log",
              "d_skip", "ssd_norm", "ffn2_norm")
ALL_WEIGHTS = ("ffn1_norm", "ffn1_w_gate", "ffn1_w_up", "ffn1_w_down", "mix_norm", "w_in",
               "q_norm", "k_norm", "conv_w", "conv_b", "dt_bias", "a_log", "d_skip", "ssd_norm",
               "w_attn_branch", "w_ssd_branch", "w_out", "ffn2_norm", "ffn2_w_gate", "ffn2_w_up",
               "ffn2_w_down")
BIG = (("ffn1_w_gate", True, "gu1t", 0), ("ffn1_w_up", True, "gu1t", 1),
       ("ffn1_w_down", False, "d1", 0), ("w_in", True, "wint", 0),
       ("w_attn_branch", True, "abt", 0), ("w_ssd_branch", False, "sb", 0),
       ("w_out", False, "out", 0),
       ("ffn2_w_gate", True, "gu2t", 0), ("ffn2_w_up", True, "gu2t", 1),
       ("ffn2_w_down", False, "d2", 0))


def _nrows(shape, cols):
    return -(-math.prod(shape) // cols)


def _pack_rows(arrs, cols, row_tile):
    parts = []
    for a in arrs:
        flat = a.reshape(-1)
        nr = -(-flat.shape[0] // cols)
        parts.append(jnp.pad(flat, (0, nr * cols - flat.shape[0])).reshape(nr, cols))
    out = jnp.concatenate(parts, axis=0)
    return jnp.pad(out, ((0, _round_up(out.shape[0], row_tile) - out.shape[0]), (0, 0)))


def _unpack_rows(packed, shapes):
    cols = packed.shape[-1]
    out, r0 = [], 0
    for sh in shapes:
        nr = _nrows(sh, cols)
        out.append(packed[r0:r0 + nr].reshape(-1)[:math.prod(sh)].reshape(tuple(sh)))
        r0 += nr
    return out


def kernel(x, ffn1_norm, ffn1_w_gate, ffn1_w_up, ffn1_w_down, mix_norm, w_in, q_norm, k_norm, conv_w, conv_b, dt_bias, a_log, d_skip, ssd_norm, w_attn_branch, w_ssd_branch, w_out, ffn2_norm, ffn2_w_gate, ffn2_w_up, ffn2_w_down, loss_target, m_ffn1_norm, m_ffn1_w_gate, m_ffn1_w_up, m_ffn1_w_down, m_mix_norm, m_w_in, m_q_norm, m_k_norm, m_conv_w, m_conv_b, m_dt_bias, m_a_log, m_d_skip, m_ssd_norm, m_w_attn_branch, m_w_ssd_branch, m_w_out, m_ffn2_norm, m_ffn2_w_gate, m_ffn2_w_up, m_ffn2_w_down, v_ffn1_norm, v_ffn1_w_gate, v_ffn1_w_up, v_ffn1_w_down, v_mix_norm, v_w_in, v_q_norm, v_k_norm, v_conv_w, v_conv_b, v_dt_bias, v_a_log, v_d_skip, v_ssd_norm, v_w_attn_branch, v_w_ssd_branch, v_w_out, v_ffn2_norm, v_ffn2_w_gate, v_ffn2_w_up, v_ffn2_w_down):
    given = dict(locals())
    wts = {n: given[n] for n in ALL_WEIGHTS}
    mom = {n: given["m_" + n] for n in ALL_WEIGHTS}
    var = {n: given["v_" + n] for n in ALL_WEIGHTS}
    d = x.shape[-1]
    nh = dt_bias.shape[1]
    my = _dev_index(_mesh_pos())

    def row_form(n, col_sharded):
        a = wts[n][0].T if col_sharded else wts[n][0]
        return jnp.pad(a, ((0, _round_up(a.shape[0], ROW_ALIGN) - a.shape[0]), (0, 0)))

    _Order.tokens, _Order.last = [], None
    shard = {n: row_form(n, cs).astype(MXU_DTYPE) for n, cs, _, _ in BIG}
    entries = {buf: [e for e in BIG if e[2] == buf] for buf in dict.fromkeys(e[2] for e in BIG)}

    def buf_shape(buf):
        r, c = shard[entries[buf][0][0]].shape
        return (len(entries[buf]) * NDEV * r, c)

    def gather_plan(bufs):
        srcs, lands, plan = [], [], []
        for li, buf in enumerate(bufs):
            lands.append(lax.empty(buf_shape(buf), MXU_DTYPE))
            for n, _, _, pos in entries[buf]:
                r = shard[n].shape[0]
                plan.append((len(srcs), 0, 0, li, pos * NDEV * r, r, r))
                srcs.append(shard[n])
        return srcs, lands, plan

    def scatter_plan(bufs, grads):
        srcs, lands, plan, names = [], [], [], []
        for si, buf in enumerate(bufs):
            srcs.append(grads[buf])
            for n, _, _, pos in entries[buf]:
                r, c = shard[n].shape
                plan.append((si, pos * NDEV * r, r, len(lands), 0, r, r))
                lands.append(lax.empty((NDEV * r, c), MXU_DTYPE))
                names.append(n)
        return srcs, lands, plan, names

    first_bufs = ("gu1t", "d1")
    shards, dests, out_shapes = [], [], []
    for bi, buf in enumerate(first_bufs):
        out_shapes.append((buf_shape(buf), MXU_DTYPE))
        for n, _, _, pos in entries[buf]:
            r = shard[n].shape[0]
            shards.append(shard[n])
            dests.append((bi, pos * NDEV * r, r))
    conv_rows = _pack_rows([conv_w[0]], LANE, ROW_ALIGN)
    shards.append(conv_rows)
    dests.append((len(first_bufs), 0, conv_rows.shape[0]))
    out_shapes.append(((NDEV * conv_rows.shape[0], LANE), F32))
    gathered = _gather("gather_first", shards, dests, out_shapes)

    in_cols = w_in.shape[2]
    in_pad = _round_up(in_cols, ROW_ALIGN)
    dt_off = NDEV * in_cols - 2 * d - nh
    second_bufs = ("wint",)
    third_bufs = ("abt", "sb", "out", "gu2t", "d2")
    plan2 = gather_plan(second_bufs)
    started2 = _split_start("gather_in_start", *plan2, after=[gathered[0]],
                            relations=SAME_CORE_AND_SIBLING)
    forwarded, started3 = [], []

    class Weights(dict):
        def __missing__(self, key):
            if key == "d1":
                lands = _split_wait("gather_in_wait", started2, plan2[2],
                                    relations=SAME_CORE_AND_SIBLING)
                forwarded.append(_forward_start("gather_in_forward", lands, plan2[2]))
                self["d1"] = gathered[1]
            elif key in ("maint", "dtt"):
                wint = _forward_wait("gather_in_arrive", forwarded[0], plan2[2])[0]
                plan3 = gather_plan(third_bufs)
                started3.append((_split_start("gather_rest_start", *plan3, after=[wint]), plan3[2]))
                self["maint"], self["dtt"] = _regroup_rows(
                    "regroup_w_in", wint, in_cols, in_pad, dt_off, dt_off + nh)
            else:
                st, plan = started3[0]
                for buf, a in zip(third_bufs, _split_wait("gather_rest_wait", st, plan)):
                    self[buf] = a
            return self[key]

    w = Weights(gu1t=gathered[0])
    p = {n: wts[n] for n in REPLICATED}
    conv_all = gathered[-1].reshape(NDEV, conv_rows.shape[0] * LANE)[:, :math.prod(conv_w.shape[1:])]
    p["conv_w"] = (conv_all.reshape((NDEV,) + conv_w.shape[1:]).transpose(1, 0, 2)
                   .reshape(conv_w.shape[1], NDEV * conv_w.shape[2]))

    groups = (("scatter_late", ("gu2t", "d2", "out", "abt", "sb")),
              ("scatter_in", ("maint", "dtt")),
              ("scatter_first", ("gu1t", "d1")))
    in_flight = []

    class Grads(dict):
        def __setitem__(self, key, value):
            dict.__setitem__(self, key, value)
            for tag, need in groups:
                if key in need and all(k in self for k in need):
                    if tag == "scatter_in":
                        gwin = _ungroup_rows("ungroup_w_in", self["maint"], self["dtt"],
                                             in_cols, in_pad, dt_off, dt_off + nh)
                        bufs, grads = ("wint",), {"wint": gwin}
                    else:
                        bufs, grads = need, self
                    srcs, lands, plan, names = scatter_plan(bufs, grads)
                    in_flight.append((tag, _split_start(tag + "_start", srcs, lands, plan),
                                      plan, names))

    loss_vec, dx, gw, gp = _local_step(x[0], loss_target[0], w, p, Grads(),
                                       aw=w_attn_branch.shape[1])

    small_names = REPLICATED + ("conv_w",)
    small_shapes = [gp[n].shape for n in small_names]
    small = _pack_rows([gp[n] for n in small_names], LANE, 8)
    small_all = _exchange("exchange_small", [], [], small)[0]

    outs = [{}, {}, {}, {}]
    col_sharded_of = {n: cs for n, cs, _, _ in BIG}
    for tag, started, plan, names in in_flight:
        for n, rv in zip(names, _split_wait(tag + "_wait", started, plan)):
            rv = rv.reshape(NDEV, shard[n].shape[0], shard[n].shape[1])
            res = _Order.done(_adamw("adamw_" + n, rv, wts[n][0], mom[n][0], var[n][0],
                                     transposed=col_sharded_of[n]))
            for k in range(4):
                outs[k][n] = res[k][None]

    small_g = _unpack_rows(_sum_slabs("sum_small_grads", small_all), small_shapes)
    small_g = dict(zip(small_names, small_g))
    cs = conv_w.shape[2]
    small_g["conv_w"] = lax.dynamic_slice_in_dim(small_g["conv_w"], my * cs, cs, axis=1)
    small_shard_shapes = [wts[n].shape[-2:] for n in small_names]
    sg = _pack_rows([small_g[n] for n in small_names], LANE, 8)
    sw = _pack_rows([wts[n] for n in small_names], LANE, 8)
    sm = _pack_rows([mom[n] for n in small_names], LANE, 8)
    sv = _pack_rows([var[n] for n in small_names], LANE, 8)
    res_small = _adamw("adamw_small", sg[None], sw, sm, sv, tr=sg.shape[0])
    for k in range(4):
        for n, a in zip(small_names, _unpack_rows(res_small[k], small_shard_shapes)):
            outs[k][n] = a.reshape(wts[n].shape)

    loss = lax.psum(0.5 * jnp.sum(loss_vec) / d, ("x", "y", "c"))
    result = [loss, dx[None]]
    for k in range(4):
        result += [outs[k][n] for n in ALL_WEIGHTS]
    return tuple(result)
```

```python
import functools
import math

import numpy as np
import jax
import jax.numpy as jnp
from jax import lax
from jax.experimental import pallas as pl
from jax.experimental.pallas import tpu as pltpu

F32 = jnp.float32
BF16 = jnp.bfloat16
MXU_DTYPE = BF16
ACT_DTYPE = BF16

NDEV = 8
EPS = 1e-6
HD = 64
QB = 128
PATTERNS = ((128, 1), (512, 4), (2048, 16))
ALIBI_MAX_EXP = 8.0
SSD_P = 64
SSD_N = 128
SSD_G = 4
SSD_Q = 128
SSD_K = 4
NEG = -1e30
LANE = 128
ROW_ALIGN = 16
VMEM_LIMIT = 56 * 1024 * 1024

ADAM_LR, ADAM_B1, ADAM_B2, ADAM_EPS, ADAM_WD, ADAM_STEP = 0.001, 0.9, 0.999, 1e-8, 0.01, 10

NN = (((1,), (0,)), ((), ()))
NT = (((1,), (1,)), ((), ()))
TN = (((0,), (0,)), ((), ()))


def _dot(a, b, dims=NN):
    return lax.dot_general(a.astype(MXU_DTYPE), b.astype(MXU_DTYPE), dims,
                           preferred_element_type=F32)


def _split3(a):
    hi = a.astype(BF16)
    r = a - hi.astype(F32)
    mid = r.astype(BF16)
    lo = (r - mid.astype(F32)).astype(BF16)
    return hi, mid, lo


def _dot3(a, b, dims=NN, split=0):
    if split == 0:
        bb = b.astype(BF16)
        parts = [lax.dot_general(s, bb, dims, preferred_element_type=F32) for s in _split3(a)]
    else:
        aa = a.astype(BF16)
        parts = [lax.dot_general(aa, s, dims, preferred_element_type=F32) for s in _split3(b)]
    return parts[0] + parts[1] + parts[2]


@jax.custom_vjp
def _spread(v, e):
    return _dot3(v, e)


def _spread_fwd(v, e):
    return _dot3(v, e), e


def _spread_bwd(e, g):
    return _dot3(g, e, NT), jnp.zeros_like(e)


_spread.defvjp(_spread_fwd, _spread_bwd)


@jax.custom_vjp
def _running_sum(a, lower):
    return _dot3(lower, a, NN, split=1)


def _running_sum_fwd(a, lower):
    return _dot3(lower, a, NN, split=1), lower


def _running_sum_bwd(lower, g):
    return _dot3(lower, g, TN, split=1), jnp.zeros_like(lower)


_running_sum.defvjp(_running_sum_fwd, _running_sum_bwd)


def _tile(n, cap):
    if n <= cap:
        return n
    best = None
    for t in range(LANE, cap + 1, LANE):
        if n % t == 0:
            best = t
    assert best is not None, (n, cap)
    return best


def _params(sem):
    return pltpu.CompilerParams(dimension_semantics=sem, vmem_limit_bytes=VMEM_LIMIT)


def _round_up(n, m):
    return -(-n // m) * m


class _Order:
    tokens = []
    last = None

    @classmethod
    def take(cls):
        out, cls.tokens = cls.tokens, []
        return out

    @classmethod
    def done(cls, result):
        cls.last = result[0] if isinstance(result, (list, tuple)) else result
        return result


ANY_SPEC = pl.BlockSpec(memory_space=pl.ANY)


def _mm(name, a, b, mode, out_dtype=F32, res=None, scale=1.0,
        cap_m=1408, cap_n=1408, cap_k=1408):
    segs = list(a) if isinstance(a, (list, tuple)) else [a]
    nseg = len(segs)
    if mode == "tn":
        k = segs[0].shape[0]
        widths = [s.shape[1] for s in segs]
        m = sum(widths)
        k2, n = b.shape
        tm = _tile(math.gcd(*widths), cap_m)
        tk = _tile(k, cap_k)
        counts = [wd // tm for wd in widths]
    else:
        m = segs[0].shape[0]
        widths = [s.shape[1] for s in segs]
        k = sum(widths)
        (k2, n) = b.shape if mode == "nn" else b.shape[::-1]
        tm = _tile(m, cap_m)
        tk = _tile(math.gcd(*widths), cap_k)
        counts = [wd // tk for wd in widths]
    assert k == k2, (name, [s.shape for s in segs], b.shape, mode)
    tn = _tile(n, cap_n)
    nk = k // tk
    starts = [sum(counts[:s]) for s in range(nseg)]
    dims = {"nn": NN, "nt": NT, "tn": TN}[mode]

    def a_spec(s):
        lo, cnt = starts[s], counts[s]
        if mode == "tn":
            if nseg == 1:
                return pl.BlockSpec((tk, tm), lambda i, j, kk: (kk, i))
            return pl.BlockSpec(
                (tk, tm), lambda i, j, kk: (jnp.where((i >= lo) & (i < lo + cnt), kk, 0),
                                            jnp.clip(i - lo, 0, cnt - 1)))
        if nseg == 1:
            return pl.BlockSpec((tm, tk), lambda i, j, kk: (i, kk))
        return pl.BlockSpec((tm, tk), lambda i, j, kk: (i, jnp.clip(kk - lo, 0, cnt - 1)))

    b_spec = (pl.BlockSpec((tn, tk), lambda i, j, kk: (j, kk)) if mode == "nt"
              else pl.BlockSpec((tk, tn), lambda i, j, kk: (kk, j)))
    o_spec = pl.BlockSpec((tm, tn), lambda i, j, kk: (i, j))
    has_res = res is not None
    use_acc = nk > 1 or nseg > 1
    ties = _Order.take()
    nt_ = len(ties)

    def body(*refs):
        a_refs = refs[:nseg]
        b_ref = refs[nseg]
        r_ref = refs[nseg + 1] if has_res else None
        o_ref = refs[nseg + 1 + has_res + nt_]
        scr = refs[nseg + 2 + has_res + nt_:]

        def finish(acc):
            if scale != 1.0:
                acc = acc * scale
            if has_res:
                acc = r_ref[...].astype(F32) + acc
            o_ref[...] = acc.astype(o_ref.dtype)

        if not use_acc:
            finish(_dot(a_refs[0][...], b_ref[...], dims))
            return
        acc_ref = scr[0]
        kk = pl.program_id(2)
        sel = pl.program_id(0) if mode == "tn" else kk

        @pl.when(kk == 0)
        def _():
            acc_ref[...] = jnp.zeros_like(acc_ref)

        for s in range(nseg):
            def add(s=s):
                acc_ref[...] += _dot(a_refs[s][...], b_ref[...], dims)
            if nseg == 1:
                add()
            else:
                pl.when((sel >= starts[s]) & (sel < starts[s] + counts[s]))(add)

        @pl.when(kk == nk - 1)
        def _():
            finish(acc_ref[...])

    in_specs = ([a_spec(s) for s in range(nseg)] + [b_spec] + ([o_spec] if has_res else [])
                + [ANY_SPEC] * nt_)
    args = tuple(segs) + (b,) + ((res,) if has_res else ()) + tuple(ties)
    return _Order.done(pl.pallas_call(
        body, name=name,
        out_shape=jax.ShapeDtypeStruct((m, n), out_dtype),
        grid=(m // tm, n // tn, nk),
        in_specs=in_specs, out_specs=o_spec,
        scratch_shapes=[pltpu.VMEM((tm, tn), F32)] if use_acc else [],
        compiler_params=_params(("parallel", "parallel", "arbitrary")),
    )(*args))


def _rw(name, fn, ins, outs, accs=(), tr=256, ncb=1):
    t = next(a.shape[0] for kind, a, _, _ in ins if kind == "row")
    assert t % tr == 0
    n_in = len(ins)
    n_pieces = sum(len(w) for w, _ in outs)

    def spec(kind, arr, width, base):
        if kind == "row":
            return pl.BlockSpec((tr, width), lambda j, i: (i, base + j))
        return pl.BlockSpec((arr.shape[0], width), lambda j, i: (0, base + j))

    in_specs = [spec(*s) for s in ins]
    out_shapes, out_specs = [], []
    for widths, dt in outs:
        w = sum(widths)
        out_shapes.append(jax.ShapeDtypeStruct((t, w * ncb), dt))
        out_specs.append(pl.BlockSpec((tr, w), lambda j, i: (i, j)))
    for rows, width in accs:
        out_shapes.append(jax.ShapeDtypeStruct((rows, width * ncb), F32))
        out_specs.append(pl.BlockSpec((rows, width), lambda j, i: (0, j)))

    ties = _Order.take()
    nt_ = len(ties)
    in_specs = in_specs + [ANY_SPEC] * nt_

    def body(*refs):
        vals = [r[...] for r in refs[:n_in]]
        res = fn(*vals)
        o_refs = refs[n_in + nt_:n_in + nt_ + len(outs)]
        a_refs = refs[n_in + nt_ + len(outs):]
        p = 0
        for (widths, _), o_ref in zip(outs, o_refs):
            off = 0
            for w in widths:
                if len(widths) == 1:
                    o_ref[...] = res[p].astype(o_ref.dtype)
                else:
                    o_ref[:, off:off + w] = res[p].astype(o_ref.dtype)
                off += w
                p += 1
        i = pl.program_id(1)
        for a_ref, v in zip(a_refs, res[n_pieces:]):
            @pl.when(i == 0)
            def _(a_ref=a_ref, v=v):
                a_ref[...] = v

            @pl.when(i > 0)
            def _(a_ref=a_ref, v=v):
                a_ref[...] += v

    return _Order.done(pl.pallas_call(
        body, name=name, out_shape=out_shapes,
        grid=(ncb, t // tr), in_specs=in_specs, out_specs=out_specs,
        compiler_params=_params(("parallel", "arbitrary")),
    )(*[a for _, a, _, _ in ins], *ties))


def _rms(x, g):
    x = x.astype(F32)
    return x * lax.rsqrt(jnp.mean(x * x, axis=-1, keepdims=True) + EPS) * g


def _silu(x):
    return x * jax.nn.sigmoid(x)


def _colsum(v):
    return jnp.sum(v, axis=0, keepdims=True)


def _pair_norm(x, g):
    w = 2 * HD
    ri = lax.broadcasted_iota(jnp.int32, (w, w), 0)
    ci = lax.broadcasted_iota(jnp.int32, (w, w), 1)
    same_head = ((ri < HD) == (ci < HD)).astype(F32)
    ms = _spread(x * x, same_head) * (1.0 / HD)
    return x * lax.rsqrt(ms + EPS) * g


ATTN_SCALE = 1.0 / math.sqrt(HD)


def _attn_bias(coef):
    key = lax.broadcasted_iota(jnp.int32, (QB, QB), 0)
    qry = lax.broadcasted_iota(jnp.int32, (QB, QB), 1)
    dist = (qry - key).astype(F32)
    own = jnp.where(qry >= key, -coef * dist, NEG)
    prev = jnp.where(qry <= key, -coef * (dist + float(QB)), NEG)
    return own, prev


BNT = (((2,), (2,)), ((0,), (0,)))
BTN = (((1,), (1,)), ((0,), (0,)))


def _attn_pair(qn, kcn, kpn, vc, vp, b_own, b_prev):
    nb = qn.shape[0]
    w = 2 * HD
    lane = lax.broadcasted_iota(jnp.int32, (1, 1, w), 2)
    eye = (lax.broadcasted_iota(jnp.int32, (QB, QB), 0)
           == lax.broadcasted_iota(jnp.int32, (QB, QB), 1)).astype(F32)
    out = jnp.zeros((nb, QB, w), F32)
    lb = jnp.zeros((nb * QB, w), F32)
    for hh in range(2):
        mask = ((lane < HD) if hh == 0 else (lane >= HD)).astype(F32)
        qm = qn * mask
        lc = _dot(kcn, qm, BNT) + b_own[hh]
        lp = _dot(kpn, qm, BNT) + b_prev[hh]
        m = lax.stop_gradient(jnp.maximum(jnp.max(lc, axis=1, keepdims=True),
                                          jnp.max(lp, axis=1, keepdims=True)))
        pc = jnp.exp(lc - m)
        pp = jnp.exp(lp - m)
        l = jnp.sum(pc, axis=1, keepdims=True) + jnp.sum(pp, axis=1, keepdims=True)
        inv = 1.0 / l
        out = out + (_dot(pc * inv, vc, BTN) + _dot(pp * inv, vp, BTN)) * mask
        diag = (eye * (m + jnp.log(l))).reshape(nb * QB, QB)
        lb = lb + _spread(diag, jnp.broadcast_to(mask[0], (QB, w)))
    return out, lb.reshape(nb, QB, w)


NORM_ROWS = 128
NORM_UNROLL = 4
EPILOGUE_ROWS = 512
ATTN_BATCH_FWD = 8
ATTN_BATCH_BWD = 4


def _unit_rows(u, d):
    r = u & (d - 1)
    n = u >> (d.bit_length() - 1)

    def rows(blk):
        start = pl.multiple_of(blk * (QB * d), QB * d)
        return pl.ds(start, QB) if d == 1 else pl.ds(start + r, QB, stride=d)

    return rows(n), rows(jnp.maximum(n - 1, 0)), n == 0


def _unit_batch(i, nbatch, d, bias, qf, kf, vf):
    units = [_unit_rows(i * nbatch + j, d) for j in range(nbatch)]
    cur = lambda ref: jnp.stack([ref[c, :] for c, _, _ in units])
    prv = lambda ref: jnp.stack([ref[p, :] for _, p, _ in units])
    b_own = [b[0] for b in bias]
    b_prev = [jnp.stack([jnp.where(first, NEG, b[1]) for _, _, first in units]) for b in bias]
    return units, (cur(qf), cur(kf), prv(kf), cur(vf), prv(vf), b_own, b_prev)


def _q_norm(x, g):
    return _pair_norm(x, g * ATTN_SCALE)


def _attn_prologue(t, q_ref, k_ref, v_ref, qg_ref, kg_ref, qf, kf, vf):
    def chunk(c, carry):
        rows = pl.ds(pl.multiple_of(c * NORM_ROWS, NORM_ROWS), NORM_ROWS)
        qf[rows, :] = _q_norm(q_ref[rows, :].astype(F32), qg_ref[...])
        kf[rows, :] = _pair_norm(k_ref[rows, :].astype(F32), kg_ref[...])
        vf[rows, :] = v_ref[rows, :].astype(F32)
        return carry
    lax.fori_loop(0, t // NORM_ROWS, chunk, 0, unroll=NORM_UNROLL)


def _attn_specs(t, bases):
    w = 2 * HD
    ins = [pl.BlockSpec((t, w), functools.partial(lambda p, c, b: (0, b + p), b=b)) for b in bases]
    gain = pl.BlockSpec((1, w), lambda p, c: (0, 0))
    blk = pl.BlockSpec((t, w), lambda p, c: (0, p))
    return ins, gain, blk


def _attn_fwd(name, proj, bases, qg, kg, coefs, d):
    t = proj.shape[0]
    npairs = coefs.shape[0] // 2
    w = 2 * HD
    ins, gain, blk = _attn_specs(t, bases)

    def body(coef_ref, q_ref, k_ref, v_ref, qg_ref, kg_ref, o_ref, l_ref, qf, kf, vf):
        p = pl.program_id(0)
        bias = (_attn_bias(coef_ref[2 * p]), _attn_bias(coef_ref[2 * p + 1]))
        _attn_prologue(t, q_ref, k_ref, v_ref, qg_ref, kg_ref, qf, kf, vf)

        def step(i, carry):
            units, ins = _unit_batch(i, ATTN_BATCH_FWD, d, bias, qf, kf, vf)
            o, lb = _attn_pair(*ins)
            for j, (cur, _, _) in enumerate(units):
                o_ref[cur, :] = o[j]
                l_ref[cur, :] = lb[j]
            return carry

        lax.fori_loop(0, t // QB // ATTN_BATCH_FWD, step, 0)

    return pl.pallas_call(
        body, name=name,
        out_shape=[jax.ShapeDtypeStruct((t, npairs * w), F32)] * 2,
        grid_spec=pltpu.PrefetchScalarGridSpec(
            num_scalar_prefetch=1, grid=(npairs,),
            in_specs=ins + [gain, gain], out_specs=[blk, blk],
            scratch_shapes=[pltpu.VMEM((t, w), F32)] * 3),
        compiler_params=_params(("arbitrary",)),
    )(coefs, proj, proj, proj, qg, kg)


def _attn_bwd(name, proj, bases, qg, kg, coefs, d, do, dl):
    t = proj.shape[0]
    npairs = coefs.shape[0] // 2
    w = 2 * HD
    ins, gain, blk = _attn_specs(t, bases)

    def body(coef_ref, q_ref, k_ref, v_ref, qg_ref, kg_ref, do_ref, dl_ref,
             dq_ref, dk_ref, dv_ref, dqg_ref, dkg_ref, qf, kf, vf, dqf, dkf, dvf):
        p = pl.program_id(0)
        bias = (_attn_bias(coef_ref[2 * p]), _attn_bias(coef_ref[2 * p + 1]))
        _attn_prologue(t, q_ref, k_ref, v_ref, qg_ref, kg_ref, qf, kf, vf)
        dkf[...] = jnp.zeros_like(dkf)
        dvf[...] = jnp.zeros_like(dvf)

        def step(i, carry):
            units, ins = _unit_batch(i, ATTN_BATCH_BWD, d, bias, qf, kf, vf)
            f = lambda a, b, c, e, g: _attn_pair(a, b, c, e, g, *ins[5:])
            _, vjp = jax.vjp(f, *ins[:5])
            cot = (jnp.stack([do_ref[cur, :] for cur, _, _ in units]),
                   jnp.stack([dl_ref[cur, :] for cur, _, _ in units]))
            dq, dkc, dkp, dvc, dvp = vjp(cot)
            for j, (cur, prv, _) in enumerate(units):
                dqf[cur, :] = dq[j]
                dkf[cur, :] += dkc[j]
                dkf[prv, :] += dkp[j]
                dvf[cur, :] += dvc[j]
                dvf[prv, :] += dvp[j]
            return carry

        lax.fori_loop(0, t // QB // ATTN_BATCH_BWD, step, 0)

        def chunk(c, carry):
            dqg_acc, dkg_acc = carry
            rows = pl.ds(pl.multiple_of(c * EPILOGUE_ROWS, EPILOGUE_ROWS), EPILOGUE_ROWS)
            _, vq = jax.vjp(_q_norm, q_ref[rows, :].astype(F32), qg_ref[...])
            dq, dqg = vq(dqf[rows, :])
            _, vk = jax.vjp(_pair_norm, k_ref[rows, :].astype(F32), kg_ref[...])
            dk, dkg = vk(dkf[rows, :])
            dq_ref[rows, :] = dq.astype(dq_ref.dtype)
            dk_ref[rows, :] = dk.astype(dk_ref.dtype)
            dv_ref[rows, :] = dvf[rows, :].astype(dv_ref.dtype)
            return dqg_acc + dqg, dkg_acc + dkg

        zero = jnp.zeros((1, w), F32)
        dqg, dkg = lax.fori_loop(0, t // EPILOGUE_ROWS, chunk, (zero, zero))

        @pl.when(p == 0)
        def _():
            dqg_ref[...] = dqg
            dkg_ref[...] = dkg

        @pl.when(p > 0)
        def _():
            dqg_ref[...] += dqg
            dkg_ref[...] += dkg

    big = jax.ShapeDtypeStruct((t, npairs * w), ACT_DTYPE)
    small = jax.ShapeDtypeStruct((1, w), F32)
    return pl.pallas_call(
        body, name=name,
        out_shape=[big, big, big, small, small],
        grid_spec=pltpu.PrefetchScalarGridSpec(
            num_scalar_prefetch=1, grid=(npairs,),
            in_specs=ins + [gain, gain, blk, blk],
            out_specs=[blk, blk, blk, gain, gain],
            scratch_shapes=[pltpu.VMEM((t, w), F32)] * 6),
        compiler_params=_params(("arbitrary",)),
    )(coefs, proj, proj, proj, qg, kg, do, dl)


def _shift_down(u, s):
    if s == 0:
        return u
    rows = lax.broadcasted_iota(jnp.int32, u.shape, 0)
    return jnp.where(rows >= s, pltpu.roll(u, s, 0), 0.0)


def _shift_up(u, s):
    if s == 0:
        return u
    t = u.shape[0]
    rows = lax.broadcasted_iota(jnp.int32, u.shape, 0)
    return jnp.where(rows < t - s, pltpu.roll(u, t - s, 0), 0.0)


def _conv_pre(u, w, b):
    y = b
    for kk in range(SSD_K):
        y = y + w[kk:kk + 1, :] * _shift_down(u, SSD_K - 1 - kk)
    return y


def _conv_fwd(name, src, base, w, b, cw=128):
    t = src.shape[0]
    c = w.shape[1]

    def body(u_ref, w_ref, b_ref, o_ref):
        y = _conv_pre(u_ref[...].astype(F32), w_ref[...], b_ref[...])
        o_ref[...] = _silu(y).astype(o_ref.dtype)

    return pl.pallas_call(
        body, name=name, out_shape=jax.ShapeDtypeStruct((t, c), ACT_DTYPE),
        grid=(c // cw,),
        in_specs=[pl.BlockSpec((t, cw), lambda j: (0, base + j)),
                  pl.BlockSpec((SSD_K, cw), lambda j: (0, j)),
                  pl.BlockSpec((1, cw), lambda j: (0, j))],
        out_specs=pl.BlockSpec((t, cw), lambda j: (0, j)),
        compiler_params=_params(("parallel",)),
    )(src, w, b)


def _conv_bwd(name, src, base, w, b, dout, cw=128):
    t = src.shape[0]
    c = w.shape[1]

    def body(u_ref, w_ref, b_ref, d_ref, du_ref, dw_ref, db_ref):
        u = u_ref[...].astype(F32)
        wv = w_ref[...]
        y = _conv_pre(u, wv, b_ref[...])
        sg = jax.nn.sigmoid(y)
        dy = d_ref[...].astype(F32) * (sg * (1.0 + y * (1.0 - sg)))
        du = jnp.zeros_like(u)
        for kk in range(SSD_K):
            s = SSD_K - 1 - kk
            du = du + wv[kk:kk + 1, :] * _shift_up(dy, s)
            dw_ref[kk:kk + 1, :] = _colsum(dy * _shift_down(u, s))
        du_ref[...] = du.astype(du_ref.dtype)
        db_ref[...] = _colsum(dy)

    return pl.pallas_call(
        body, name=name,
        out_shape=[jax.ShapeDtypeStruct((t, c), ACT_DTYPE),
                   jax.ShapeDtypeStruct((SSD_K, c), F32),
                   jax.ShapeDtypeStruct((1, c), F32)],
        grid=(c // cw,),
        in_specs=[pl.BlockSpec((t, cw), lambda j: (0, base + j)),
                  pl.BlockSpec((SSD_K, cw), lambda j: (0, j)),
                  pl.BlockSpec((1, cw), lambda j: (0, j)),
                  pl.BlockSpec((t, cw), lambda j: (0, j))],
        out_specs=[pl.BlockSpec((t, cw), lambda j: (0, j)),
                   pl.BlockSpec((SSD_K, cw), lambda j: (0, j)),
                   pl.BlockSpec((1, cw), lambda j: (0, j))],
        compiler_params=_params(("parallel",)),
    )(src, w, b, dout)


def _softplus(x):
    return jnp.maximum(x, 0.0) + jnp.log(1.0 + jnp.exp(-jnp.abs(x)))


def _ssd_chunk(xbc, dtraw, bias, alog, states):
    wd = states[0].shape[1]
    nj = wd // SSD_P
    inner = SSD_G * wd
    dt = _softplus(dtraw + bias)
    a = dt * (-jnp.exp(alog))
    li = lax.broadcasted_iota(jnp.int32, (SSD_Q, SSD_Q), 0)
    si = lax.broadcasted_iota(jnp.int32, (SSD_Q, SSD_Q), 1)
    causal = li >= si
    acs = _running_sum(a, causal.astype(F32))
    acs_t = acs.T
    a_last = acs[SSD_Q - 1:SSD_Q, :]
    grow = jnp.exp(acs)
    shrink = jnp.exp(a_last - acs)
    hrow = lax.broadcasted_iota(jnp.int32, (LANE, wd), 0)
    wcol = lax.broadcasted_iota(jnp.int32, (LANE, wd), 1)
    lane = lax.broadcasted_iota(jnp.int32, (1, LANE), 1)
    ys, snext = [], []
    for g in range(SSD_G):
        lo = (hrow - g * nj) * SSD_P
        head_lanes = jnp.logical_and(wcol >= lo, wcol < lo + SSD_P).astype(F32)
        xs = xbc[:, g * wd:(g + 1) * wd]
        bm = xbc[:, inner + g * SSD_N:inner + (g + 1) * SSD_N]
        cm = xbc[:, inner + (SSD_G + g) * SSD_N:inner + (SSD_G + g + 1) * SSD_N]
        xdt = xs * _spread(dt, head_lanes)
        grow_x = _spread(grow, head_lanes)
        y_off = _dot(cm, states[g]) * grow_x
        s_new = (states[g] * grow_x[SSD_Q - 1:SSD_Q, :]
                 + _dot(bm, xdt * _spread(shrink, head_lanes), TN))
        cb = _dot(cm, bm, NT)
        pieces = []
        for i in range(wd // LANE):
            xp = xdt[:, i * LANE:(i + 1) * LANE]
            acc = jnp.zeros((SSD_Q, LANE), F32)
            for hh in range(LANE // SSD_P):
                h = g * nj + i * (LANE // SSD_P) + hh
                decay = jnp.exp(jnp.where(causal, acs[:, h:h + 1] - acs_t[h:h + 1, :], NEG))
                keep = jnp.logical_and(lane >= hh * SSD_P, lane < (hh + 1) * SSD_P).astype(F32)
                acc = acc + _dot(cb * decay, xp * keep)
            pieces.append(acc)
        y_diag = pieces[0] if len(pieces) == 1 else jnp.concatenate(pieces, axis=1)
        ys.append(y_diag + y_off)
        snext.append(s_new)
    return ys, snext


def _ssd_specs(cdim, wd, rev, nc):
    ch = (lambda c: nc - 1 - c) if rev else (lambda c: c)
    full = lambda width: pl.BlockSpec((SSD_Q, width), lambda c: (ch(c), 0))
    vec = pl.BlockSpec((1, LANE), lambda c: (0, 0))
    st = pl.BlockSpec((1, SSD_G, SSD_N, wd), lambda c: (ch(c), 0, 0, 0))
    return full, vec, st


def _ssd_fwd(name, xbc, dtraw, bias, alog, inner):
    t, cdim = xbc.shape
    wd = inner // SSD_G
    nc = t // SSD_Q
    full, vec, st = _ssd_specs(cdim, wd, False, nc)

    def body(x_ref, r_ref, b_ref, a_ref, y_ref, st_ref, s_scr):
        @pl.when(pl.program_id(0) == 0)
        def _():
            s_scr[...] = jnp.zeros_like(s_scr)

        sprev = [s_scr[g] for g in range(SSD_G)]
        ys, snext = _ssd_chunk(x_ref[...].astype(F32), r_ref[...], b_ref[...], a_ref[...], sprev)
        for g in range(SSD_G):
            st_ref[0, g] = sprev[g]
            y_ref[:, g * wd:(g + 1) * wd] = ys[g]
            s_scr[g] = snext[g]

    return pl.pallas_call(
        body, name=name,
        out_shape=[jax.ShapeDtypeStruct((t, inner), F32),
                   jax.ShapeDtypeStruct((nc, SSD_G, SSD_N, wd), F32)],
        grid=(nc,),
        in_specs=[full(cdim), full(LANE), vec, vec],
        out_specs=[full(inner), st],
        scratch_shapes=[pltpu.VMEM((SSD_G, SSD_N, wd), F32)],
        compiler_params=_params(("arbitrary",)),
    )(xbc, dtraw, bias, alog)


def _ssd_bwd(name, xbc, dtraw, bias, alog, states, dy, dxs_extra):
    t, cdim = xbc.shape
    inner = dy.shape[1]
    wd = inner // SSD_G
    nc = t // SSD_Q
    full, vec, st = _ssd_specs(cdim, wd, True, nc)

    def body(x_ref, r_ref, b_ref, a_ref, st_ref, dy_ref, dx0_ref,
             dx_ref, dr_ref, db_ref, da_ref, ds_scr):
        first = pl.program_id(0) == 0

        @pl.when(first)
        def _():
            ds_scr[...] = jnp.zeros_like(ds_scr)

        sprev = [st_ref[0, g] for g in range(SSD_G)]
        _, vjp = jax.vjp(_ssd_chunk, x_ref[...].astype(F32), r_ref[...], b_ref[...], a_ref[...],
                         sprev)
        dyv = dy_ref[...]
        dys = [dyv[:, g * wd:(g + 1) * wd] for g in range(SSD_G)]
        dsn = [ds_scr[g] for g in range(SSD_G)]
        dx, dr, db, da, dsp = vjp((dys, dsn))
        dx_ref[:, :inner] = dx[:, :inner] + dx0_ref[...].astype(F32)
        dx_ref[:, inner:] = dx[:, inner:]
        dr_ref[...] = dr
        for g in range(SSD_G):
            ds_scr[g] = dsp[g]

        @pl.when(first)
        def _():
            db_ref[...] = db
            da_ref[...] = da

        @pl.when(jnp.logical_not(first))
        def _():
            db_ref[...] += db
            da_ref[...] += da

    return pl.pallas_call(
        body, name=name,
        out_shape=[jax.ShapeDtypeStruct((t, cdim), F32),
                   jax.ShapeDtypeStruct((t, LANE), F32),
                   jax.ShapeDtypeStruct((1, LANE), F32),
                   jax.ShapeDtypeStruct((1, LANE), F32)],
        grid=(nc,),
        in_specs=[full(cdim), full(LANE), vec, vec, st, full(inner), full(inner)],
        out_specs=[full(cdim), full(LANE), vec, vec],
        scratch_shapes=[pltpu.VMEM((SSD_G, SSD_N, wd), F32)],
        compiler_params=_params(("arbitrary",)),
    )(xbc, dtraw, bias, alog, states, dy, dxs_extra)


def _act(g, u):
    return _silu(g.astype(F32)) * u.astype(F32)


def _mix(o0, o1, o2, l0, l1, l2):
    m = lax.stop_gradient(jnp.maximum(jnp.maximum(l0, l1), l2))
    e0, e1, e2 = jnp.exp(l0 - m), jnp.exp(l1 - m), jnp.exp(l2 - m)
    return (e0 * o0 + e1 * o1 + e2 * o2) / (e0 + e1 + e2)


def _gate(y, xs, z, dexp, gain):
    v = (y + xs.astype(F32) * dexp) * _silu(z.astype(F32))
    return _rms(v, gain)


def _merge(ga, gs, ap, sp):
    return jax.nn.sigmoid(ga.astype(F32)) * ap + jax.nn.sigmoid(gs.astype(F32)) * sp


def _alibi_coefs(hp):
    n = hp * len(PATTERNS)
    slopes = np.exp2(-ALIBI_MAX_EXP * np.arange(1, n + 1, dtype=np.float32) / n).astype(np.float32)
    return [jnp.asarray(slopes[g * hp:(g + 1) * hp] * np.float32(d))
            for g, (_, d) in enumerate(PATTERNS)]


def _local_step(x, tgt, w, p, gw_=None, aw=None):
    t, d = x.shape
    dff = w["gu1t"].shape[0] // 2
    aw = w["abt"].shape[1] if aw is None else aw
    hp = aw // HD
    qkv = len(PATTERNS) * aw
    inner = p["ssd_norm"].shape[1]
    nh = p["dt_bias"].shape[1]
    gw_ = {} if gw_ is None else gw_
    gw = inner // SSD_G
    cdim = inner + 2 * SSD_G * SSD_N
    z_off, xbc_off = 3 * qkv, 3 * qkv + inner
    ga_off = xbc_off + cdim
    gs_off = ga_off + d
    hw = d // 2
    assert z_off % gw == 0 and xbc_off % LANE == 0 and ga_off % hw == 0 and gs_off % hw == 0
    assert (nh // SSD_G) * SSD_P == gw and hp % 2 == 0 and aw % LANE == 0 and nh <= LANE
    gdt = MXU_DTYPE

    row = lambda a, width, base=0: ("row", a, width, base)
    const = lambda a, width, base=0: ("const", a, width, base)

    def rms_fwd(name, xin, g):
        return _rw(name, lambda xv, gv: (_rms(xv, gv),), [row(xin, d), const(g, d)],
                   [((d,), ACT_DTYPE)])[0]

    def rms_bwd(name, xin, g, dh, dres):
        def fn(xv, gv, dhv, drv):
            _, vjp = jax.vjp(_rms, xv, gv)
            dx, dg = vjp(dhv.astype(F32))
            return drv + dx, dg
        return _rw(name, fn, [row(xin, d), const(g, d), row(dh, d), row(dres, d)],
                   [((d,), F32)], accs=[(1, d)])

    def ffn_fwd(tag, xin, g, key_gu, key_d):
        h = rms_fwd(tag + "_norm", xin, g)
        gu = _mm(tag + "_up", h, w[key_gu], "nt", ACT_DTYPE)
        a = _rw(tag + "_act", lambda gv, uv: (_act(gv, uv),),
                [row(gu, dff, 0), row(gu, dff, 1)], [((dff,), ACT_DTYPE)])[0]
        xo = _mm(tag + "_down", a, w[key_d], "nn", F32, res=xin, scale=0.5)
        return xo, (h, gu, a)

    def ffn_bwd(tag, xin, g, wgut, wd, saved, dxo, key_gu, key_d):
        h, gu, a = saved
        da = _mm(tag + "_da", dxo, wd, "nt", ACT_DTYPE, scale=0.5)
        gw_[key_d] = _mm(tag + "_dwd", a, dxo, "tn", gdt, scale=0.5)

        def fn(gv, uv, dav):
            _, vjp = jax.vjp(_act, gv, uv)
            return vjp(dav.astype(F32))
        dgu = _rw(tag + "_dact", fn, [row(gu, dff, 0), row(gu, dff, 1), row(da, dff)],
                  [((dff, dff), ACT_DTYPE)])[0]
        gw_[key_gu] = _mm(tag + "_dwgu", dgu, h, "tn", gdt)
        dh = _mm(tag + "_dh", dgu, wgut, "nn", F32)
        return rms_bwd(tag + "_dnorm", xin, g, dh, dxo)

    x1, ffn1_saved = ffn_fwd("ffn1", x, p["ffn1_norm"], "gu1t", "d1")
    h2 = rms_fwd("mix_norm", x1, p["mix_norm"])
    proj = _mm("in_proj", h2, w["maint"], "nt", ACT_DTYPE, cap_m=512, cap_n=2944)
    dtraw = _mm("dt_proj", h2, w["dtt"], "nt", F32)

    coefs = _alibi_coefs(hp)
    qg2 = jnp.concatenate([p["q_norm"], p["q_norm"]], axis=1)
    kg2 = jnp.concatenate([p["k_norm"], p["k_norm"]], axis=1)
    pw = 2 * HD
    attn_bases = [[(off + gi * aw) // pw for off in (0, qkv, 2 * qkv)]
                  for gi in range(len(PATTERNS))]
    attn_o, attn_l = [], []
    for gi, (_, dil) in enumerate(PATTERNS):
        o, l = _attn_fwd(f"attn_fwd{gi}", proj, attn_bases[gi], qg2, kg2, coefs[gi], dil)
        attn_o.append(o)
        attn_l.append(l)
    ao = _rw("attn_mix", lambda *v: (_mix(*v),), [row(a, aw) for a in attn_o + attn_l],
             [((aw,), ACT_DTYPE)])[0]

    xbc = _conv_fwd("conv_fwd", proj, xbc_off // LANE, p["conv_w"], p["conv_b"])
    pad = lambda v: jnp.pad(v, ((0, 0), (0, LANE - nh)))
    bias_p, alog_p = pad(p["dt_bias"]), pad(p["a_log"])
    yssd, states = _ssd_fwd("ssd_fwd", xbc, dtraw, bias_p, alog_p, inner)
    dexp = jnp.repeat(p["d_skip"], SSD_P, axis=1)
    gate_ins = [row(yssd, gw), row(xbc, gw), row(proj, gw, z_off // gw),
                const(dexp, gw), const(p["ssd_norm"], gw)]
    yn = _rw("ssd_gate", lambda *v: (_gate(*v),), gate_ins, [((gw,), ACT_DTYPE)], ncb=SSD_G)[0]

    ap = _mm("attn_out", ao, w["abt"], "nt", F32)
    sp = _mm("ssd_out", yn, w["sb"], "nn", F32)
    merge_ins = [row(proj, hw, ga_off // hw), row(proj, hw, gs_off // hw), row(ap, hw), row(sp, hw)]
    mg = _rw("merge", lambda *v: (_merge(*v),), merge_ins, [((hw,), ACT_DTYPE)], ncb=2)[0]
    x2 = _mm("mix_out", mg, w["out"], "nn", F32, res=x1)
    x3, ffn2_saved = ffn_fwd("ffn2", x2, p["ffn2_norm"], "gu2t", "d2")

    def loss_fn(yv, tv):
        e = yv - tv
        return e * (1.0 / d), _colsum(e * e)
    dy, loss_vec = _rw("loss", loss_fn, [row(x3, d), row(tgt, d)], [((d,), F32)], accs=[(1, d)])

    gp = {}
    dx2, gp["ffn2_norm"] = ffn_bwd(
        "ffn2", x2, p["ffn2_norm"], w["gu2t"], w["d2"], ffn2_saved, dy, "gu2t", "d2")
    dmg = _mm("d_merge", dx2, w["out"], "nt", ACT_DTYPE)
    gw_["out"] = _mm("dw_out", mg, dx2, "tn", gdt)

    def merge_bwd(gav, gsv, apv, spv, dv):
        _, vjp = jax.vjp(_merge, gav, gsv, apv, spv)
        return vjp(dv.astype(F32))
    dga, dgs, dap, dsp = _rw("d_merge_gate", merge_bwd, merge_ins + [row(dmg, hw)],
                             [((hw,), ACT_DTYPE)] * 4, ncb=2)
    gw_["abt"] = _mm("dw_ab", dap, ao, "tn", gdt)
    dao = _mm("d_attn_o", dap, w["abt"], "nn", F32)
    gw_["sb"] = _mm("dw_sb", yn, dsp, "tn", gdt)
    dyn = _mm("d_ssd_y", dsp, w["sb"], "nt", F32)

    def gate_bwd(yv, xv, zv, dev, gv, dv):
        _, vjp = jax.vjp(_gate, yv, xv, zv, dev, gv)
        return vjp(dv)
    dyssd, dxs_gate, dz, ddexp, gp["ssd_norm"] = _rw(
        "d_ssd_gate", gate_bwd, gate_ins + [row(dyn, gw)],
        [((gw,), F32), ((gw,), F32), ((gw,), ACT_DTYPE)], accs=[(1, gw), (1, gw)], ncb=SSD_G)
    gp["d_skip"] = ddexp.reshape(nh, SSD_P).sum(axis=1).reshape(1, nh)

    dxbc, ddtraw, dbias, dalog = _ssd_bwd("ssd_bwd", xbc, dtraw, bias_p, alog_p, states,
                                          dyssd, dxs_gate)
    gp["dt_bias"], gp["a_log"] = dbias[:, :nh], dalog[:, :nh]
    du, gp["conv_w"], gp["conv_b"] = _conv_bwd("conv_bwd", proj, xbc_off // LANE,
                                               p["conv_w"], p["conv_b"], dxbc)

    def mix_bwd(*v):
        _, vjp = jax.vjp(_mix, *v[:6])
        return vjp(v[6])
    dmix = _rw("d_attn_mix", mix_bwd, [row(a, aw) for a in attn_o + attn_l] + [row(dao, aw)],
               [((aw,), F32)] * 6)
    dq, dk, dv = [], [], []
    dqg = dkg = None
    for gi, (_, dil) in enumerate(PATTERNS):
        r = _attn_bwd(f"attn_bwd{gi}", proj, attn_bases[gi], qg2, kg2, coefs[gi], dil,
                      dmix[gi], dmix[3 + gi])
        dq.append(r[0])
        dk.append(r[1])
        dv.append(r[2])
        dqg = r[3] if dqg is None else dqg + r[3]
        dkg = r[4] if dkg is None else dkg + r[4]
    gp["q_norm"] = dqg[:, :HD] + dqg[:, HD:]
    gp["k_norm"] = dkg[:, :HD] + dkg[:, HD:]

    segs = dq + dk + dv + [dz, du, dga, dgs]
    gw_["maint"] = _mm("dw_in", segs, h2, "tn", gdt)
    gw_["dtt"] = _mm("dw_dt", ddtraw, h2, "tn", gdt)
    dh2 = _mm("d_h2_main", segs, w["maint"], "nn", F32)
    dh2 = _mm("d_h2_dt", ddtraw, w["dtt"], "nn", F32, res=dh2)
    dx1, gp["mix_norm"] = rms_bwd("d_mix_norm", x1, p["mix_norm"], dh2, dx2)
    dx0, gp["ffn1_norm"] = ffn_bwd(
        "ffn1", x, p["ffn1_norm"], w["gu1t"], w["d1"], ffn1_saved, dx1, "gu1t", "d1")
    return loss_vec, dx0, gw_, gp


MESH = pl.DeviceIdType.MESH
HBM_SPEC = pl.BlockSpec(memory_space=pltpu.HBM)


def _mesh_pos():
    return lax.axis_index("x"), lax.axis_index("y"), lax.axis_index("c")


def _flip(pos, k):
    x, y, c = pos
    return (1 - x if k & 4 else x, 1 - y if k & 2 else y, 1 - c if k & 1 else c)


def _dev_index(pos):
    return 4 * pos[0] + 2 * pos[1] + pos[2]


def _rows_of(ref, base, stride, rows, pos):
    start = pl.multiple_of(base + stride * _dev_index(pos), ROW_ALIGN)
    return ref.at[pl.ds(start, rows)]


def _gather(name, shards, dests, out_shapes):
    n = len(shards)
    n_out = len(out_shapes)

    def body(*refs):
        x_refs = refs[:n]
        o_refs = refs[n:n + n_out]
        send_sems, recv_sems, local_sems = refs[n + n_out:]
        me = _mesh_pos()
        sibling = _flip(me, 1)
        chips = [_flip(me, 4), _flip(me, 2), _flip(me, 6)]

        def slot(i, block):
            k_out, base, stride = dests[i]
            return _rows_of(o_refs[k_out], base, stride, shards[i].shape[0], block)

        def copy(i, k, block, to, src=None):
            dst = slot(i, block)
            return pltpu.make_async_remote_copy(
                src_ref=dst if src is None else src, dst_ref=dst,
                send_sem=send_sems.at[7 * i + k], recv_sem=recv_sems.at[7 * i + k],
                device_id=to, device_id_type=MESH)

        mine = [pltpu.make_async_copy(x_refs[i], slot(i, me), local_sems.at[i]) for i in range(n)]
        for cp in mine:
            cp.start()
        first = []
        for i in range(n):
            first.append(copy(i, 0, me, sibling, src=x_refs[i]))
            first += [copy(i, 1 + j, me, chip, src=x_refs[i]) for j, chip in enumerate(chips)]
        for cp in first:
            cp.start()
        passed = []
        for j, chip in enumerate(chips):
            for i in range(n):
                copy(i, 1 + j, chip, me).wait_recv()
                fwd = copy(i, 4 + j, chip, sibling)
                fwd.start()
                passed.append(fwd)
        for i in range(n):
            copy(i, 0, sibling, me).wait_recv()
            for j, chip in enumerate(chips):
                copy(i, 4 + j, _flip(chip, 1), me).wait_recv()
        for cp in first + passed:
            cp.wait_send()
        for cp in mine:
            cp.wait()

    return pl.pallas_call(
        body, name=name,
        out_shape=[jax.ShapeDtypeStruct(s, dt) for s, dt in out_shapes],
        in_specs=[HBM_SPEC] * n, out_specs=[HBM_SPEC] * n_out,
        scratch_shapes=[pltpu.SemaphoreType.DMA((7 * n,)), pltpu.SemaphoreType.DMA((7 * n,)),
                        pltpu.SemaphoreType.DMA((n,))],
    )(*shards)


def _exchange(name, grads, srcs, small):
    n = len(srcs)
    ng = len(grads)

    def body(*refs):
        g_refs = refs[:ng]
        m_ref = refs[ng]
        r_refs = refs[ng + 1:ng + 1 + n]
        s_ref = refs[ng + 1 + n]
        send_sems, recv_sems, local_sems = refs[ng + 2 + n:]
        me = _mesh_pos()
        my = _dev_index(me)

        def slab(i, pos):
            gi, base, stride, rows = srcs[i]
            return _rows_of(g_refs[gi], base, stride, rows, pos)

        own = [pltpu.make_async_copy(slab(i, me), r_refs[i].at[my], local_sems.at[i])
               for i in range(n)]
        own.append(pltpu.make_async_copy(m_ref, s_ref.at[my], local_sems.at[n]))
        for cp in own:
            cp.start()

        def copies(k, src_pos, slot_pos):
            peer = _flip(me, k)
            si = _dev_index(slot_pos)
            out = [pltpu.make_async_remote_copy(
                src_ref=slab(i, src_pos), dst_ref=r_refs[i].at[si],
                send_sem=send_sems.at[7 * i + k - 1], recv_sem=recv_sems.at[7 * i + k - 1],
                device_id=peer, device_id_type=MESH) for i in range(n)]
            out.append(pltpu.make_async_remote_copy(
                src_ref=m_ref, dst_ref=s_ref.at[si],
                send_sem=send_sems.at[7 * n + k - 1], recv_sem=recv_sems.at[7 * n + k - 1],
                device_id=peer, device_id_type=MESH))
            return out

        sent = [cp for k in range(1, NDEV) for cp in copies(k, _flip(me, k), me)]
        for cp in sent:
            cp.start()
        for k in range(1, NDEV):
            for cp in copies(k, me, _flip(me, k)):
                cp.wait_recv()
        for cp in sent:
            cp.wait_send()
        for cp in own:
            cp.wait()

    out_shape = [jax.ShapeDtypeStruct((NDEV, rows, grads[gi].shape[1]), grads[gi].dtype)
                 for gi, _, _, rows in srcs]
    out_shape.append(jax.ShapeDtypeStruct((NDEV,) + small.shape, small.dtype))
    return pl.pallas_call(
        body, name=name, out_shape=out_shape,
        in_specs=[HBM_SPEC] * (ng + 1), out_specs=[HBM_SPEC] * (n + 1),
        scratch_shapes=[pltpu.SemaphoreType.DMA((7 * (n + 1),)),
                        pltpu.SemaphoreType.DMA((7 * (n + 1),)),
                        pltpu.SemaphoreType.DMA((n + 1,))],
    )(*grads, small)


SEM_SPEC = pl.BlockSpec(memory_space=pltpu.SEMAPHORE)
SIDE_EFFECT = pltpu.SideEffectType.DATAFLOW_SIDE_EFFECTING


def _split_refs(plan, srcs, lands, i, src_for, land_from):
    si, sbase, sstride, li, lbase, lstride, rows = plan[i]
    return (_rows_of(srcs[si], sbase, sstride, rows, src_for),
            _rows_of(lands[li], lbase, lstride, rows, land_from))


ALL_PEERS = tuple(range(1, NDEV))
SAME_CORE_AND_SIBLING = (1, 4, 2, 6)
OTHER_CHIPS = (4, 2, 6)


def _split_start(name, srcs, lands, plan, after=(), relations=ALL_PEERS):
    ns, nl, n = len(srcs), len(lands), len(plan)

    def body(*refs):
        s_refs = refs[:ns]
        l_refs = refs[ns:ns + nl]
        send_sems, recv_sems = refs[ns + nl + len(after):ns + nl + len(after) + 2]
        local_sems = refs[ns + nl + len(after) + 2]
        token = refs[ns + nl + len(after) + 3 + ns + nl]
        me = _mesh_pos()
        for i in range(n):
            src, dst = _split_refs(plan, s_refs, l_refs, i, me, me)
            pltpu.make_async_copy(src, dst, local_sems.at[i]).start()
        for k in relations:
            peer = _flip(me, k)
            for i in range(n):
                src, dst = _split_refs(plan, s_refs, l_refs, i, peer, me)
                pltpu.make_async_remote_copy(
                    src_ref=src, dst_ref=dst,
                    send_sem=send_sems.at[7 * i + k - 1], recv_sem=recv_sems.at[7 * i + k - 1],
                    device_id=peer, device_id_type=MESH).start()
        token[...] = jnp.zeros_like(token)

    hbm = lambda a: pltpu.HBM(a.shape, a.dtype)
    out_shape = ((pltpu.SemaphoreType.DMA((7 * n,)), pltpu.SemaphoreType.DMA((7 * n,)),
                  pltpu.SemaphoreType.DMA((n,)))
                 + tuple(hbm(a) for a in srcs) + tuple(hbm(a) for a in lands)
                 + (jax.ShapeDtypeStruct((8, LANE), F32),))
    out = pl.pallas_call(
        body, name=name, out_shape=out_shape,
        in_specs=[HBM_SPEC] * (ns + nl) + [ANY_SPEC] * len(after),
        out_specs=(SEM_SPEC, SEM_SPEC, SEM_SPEC) + (HBM_SPEC,) * (ns + nl)
        + (pl.BlockSpec(memory_space=pltpu.VMEM),),
        input_output_aliases={i: 3 + i for i in range(ns + nl)},
        compiler_params=pltpu.CompilerParams(has_side_effects=SIDE_EFFECT),
    )(*[pltpu.with_memory_space_constraint(a, pltpu.HBM) for a in tuple(srcs) + tuple(lands)],
      *after)
    _Order.tokens.append(out[-1])
    return out[0], out[1], out[2], out[3:3 + ns], out[3 + ns:3 + ns + nl]


def _split_wait(name, started, plan, relations=ALL_PEERS):
    send_sems, recv_sems, local_sems, srcs, lands = started
    ns, nl, n = len(srcs), len(lands), len(plan)
    after = [_Order.last] if _Order.last is not None else []

    def body(*refs):
        s_refs = refs[:ns]
        l_refs = refs[ns:ns + nl]
        send_sems, recv_sems, local_sems = refs[ns + nl:ns + nl + 3]
        me = _mesh_pos()
        for i in range(n):
            src, dst = _split_refs(plan, s_refs, l_refs, i, me, me)
            pltpu.make_async_copy(src, dst, local_sems.at[i]).wait()
        for k in relations:
            peer = _flip(me, k)
            for i in range(n):
                src, dst = _split_refs(plan, s_refs, l_refs, i, peer, peer)
                cp = pltpu.make_async_remote_copy(
                    src_ref=src, dst_ref=dst,
                    send_sem=send_sems.at[7 * i + k - 1], recv_sem=recv_sems.at[7 * i + k - 1],
                    device_id=peer, device_id_type=MESH)
                cp.wait_send()
                cp.wait_recv()

    hbm = lambda a: pltpu.HBM(a.shape, a.dtype)
    out = pl.pallas_call(
        body, name=name,
        out_shape=tuple(hbm(a) for a in srcs) + tuple(hbm(a) for a in lands),
        in_specs=[HBM_SPEC] * (ns + nl) + [SEM_SPEC] * 3 + [ANY_SPEC] * len(after),
        out_specs=(HBM_SPEC,) * (ns + nl),
        input_output_aliases={i: i for i in range(ns + nl)},
        compiler_params=pltpu.CompilerParams(has_side_effects=SIDE_EFFECT),
    )(*srcs, *lands, send_sems, recv_sems, local_sems, *after)
    return list(out[ns:])


def _forward_refs(plan, lands, i, block):
    _, _, _, li, lbase, lstride, rows = plan[i]
    return _rows_of(lands[li], lbase, lstride, rows, block)


def _forward_start(name, lands, plan):
    nl, n = len(lands), len(plan)

    def body(*refs):
        l_refs = refs[:nl]
        send_sems, recv_sems = refs[nl:nl + 2]
        token = refs[nl + 2 + nl]
        me = _mesh_pos()
        for j, kc in enumerate(OTHER_CHIPS):
            for i in range(n):
                rows = _forward_refs(plan, l_refs, i, _flip(me, kc))
                pltpu.make_async_remote_copy(
                    src_ref=rows, dst_ref=rows,
                    send_sem=send_sems.at[3 * i + j], recv_sem=recv_sems.at[3 * i + j],
                    device_id=_flip(me, 1), device_id_type=MESH).start()
        token[...] = jnp.zeros_like(token)

    hbm = lambda a: pltpu.HBM(a.shape, a.dtype)
    out = pl.pallas_call(
        body, name=name,
        out_shape=((pltpu.SemaphoreType.DMA((3 * n,)), pltpu.SemaphoreType.DMA((3 * n,)))
                   + tuple(hbm(a) for a in lands) + (jax.ShapeDtypeStruct((8, LANE), F32),)),
        in_specs=[HBM_SPEC] * nl,
        out_specs=(SEM_SPEC, SEM_SPEC) + (HBM_SPEC,) * nl
        + (pl.BlockSpec(memory_space=pltpu.VMEM),),
        input_output_aliases={i: 2 + i for i in range(nl)},
        compiler_params=pltpu.CompilerParams(has_side_effects=SIDE_EFFECT),
    )(*[pltpu.with_memory_space_constraint(a, pltpu.HBM) for a in lands])
    _Order.tokens.append(out[-1])
    return out[0], out[1], out[2:2 + nl]


def _forward_wait(name, started, plan):
    send_sems, recv_sems, lands = started
    nl, n = len(lands), len(plan)
    after = [_Order.last] if _Order.last is not None else []

    def body(*refs):
        l_refs = refs[:nl]
        send_sems, recv_sems = refs[nl:nl + 2]
        me = _mesh_pos()
        for j, kc in enumerate(OTHER_CHIPS):
            for i in range(n):
                sent = _forward_refs(plan, l_refs, i, _flip(me, kc))
                came = _forward_refs(plan, l_refs, i, _flip(_flip(me, 1), kc))
                cp = pltpu.make_async_remote_copy(
                    src_ref=sent, dst_ref=came,
                    send_sem=send_sems.at[3 * i + j], recv_sem=recv_sems.at[3 * i + j],
                    device_id=_flip(me, 1), device_id_type=MESH)
                cp.wait_send()
                cp.wait_recv()

    hbm = lambda a: pltpu.HBM(a.shape, a.dtype)
    out = pl.pallas_call(
        body, name=name, out_shape=tuple(hbm(a) for a in lands),
        in_specs=[HBM_SPEC] * nl + [SEM_SPEC] * 2 + [ANY_SPEC] * len(after),
        out_specs=(HBM_SPEC,) * nl,
        input_output_aliases={i: i for i in range(nl)},
        compiler_params=pltpu.CompilerParams(has_side_effects=SIDE_EFFECT),
    )(*lands, send_sems, recv_sems, *after)
    return list(out)


def _regroup_rows(name, padded, r, rp, lo, hi):
    d = padded.shape[1]
    pack = 4 // padded.dtype.itemsize
    assert r % pack == 0 and rp % ROW_ALIGN == 0 and lo % (8 * pack) == 0 and hi % (8 * pack) == 0
    r2, rp2, lo2, hi2 = r // pack, rp // pack, lo // pack, hi // pack
    u32 = jnp.uint32

    def body(x_ref, main_ref, cut_ref):
        x = pltpu.bitcast(x_ref[...], u32)
        joined = jnp.concatenate([x[rp2 * j:rp2 * j + r2] for j in range(NDEV)], axis=0)
        main = jnp.concatenate([joined[:lo2], joined[hi2:]], axis=0)
        cut = jnp.concatenate([joined[lo2:hi2], jnp.zeros((LANE // pack - (hi2 - lo2), LANE), u32)],
                              axis=0)
        main_ref[...] = pltpu.bitcast(main, padded.dtype)
        cut_ref[...] = pltpu.bitcast(cut, padded.dtype)

    return pl.pallas_call(
        body, name=name,
        out_shape=[jax.ShapeDtypeStruct((NDEV * r - (hi - lo), d), padded.dtype),
                   jax.ShapeDtypeStruct((LANE, d), padded.dtype)],
        grid=(d // LANE,),
        in_specs=[pl.BlockSpec((NDEV * rp, LANE), lambda i: (0, i))],
        out_specs=[pl.BlockSpec((NDEV * r - (hi - lo), LANE), lambda i: (0, i)),
                   pl.BlockSpec((LANE, LANE), lambda i: (0, i))],
        compiler_params=_params(("parallel",)),
    )(padded)


def _ungroup_rows(name, main, cut, r, rp, lo, hi):
    d = main.shape[1]
    pack = 4 // main.dtype.itemsize
    r2, rp2, lo2, hi2 = r // pack, rp // pack, lo // pack, hi // pack
    u32 = jnp.uint32

    def body(main_ref, cut_ref, o_ref):
        m = pltpu.bitcast(main_ref[...], u32)
        c = pltpu.bitcast(cut_ref[...], u32)
        joined = jnp.concatenate([m[:lo2], c[:hi2 - lo2], m[lo2:]], axis=0)
        zeros = jnp.zeros((rp2 - r2, LANE), u32)
        parts = []
        for j in range(NDEV):
            parts += [joined[r2 * j:r2 * (j + 1)], zeros]
        o_ref[...] = pltpu.bitcast(jnp.concatenate(parts, axis=0), main.dtype)

    return pl.pallas_call(
        body, name=name,
        out_shape=jax.ShapeDtypeStruct((NDEV * rp, d), main.dtype),
        grid=(d // LANE,),
        in_specs=[pl.BlockSpec((main.shape[0], LANE), lambda i: (0, i)),
                  pl.BlockSpec((LANE, LANE), lambda i: (0, i))],
        out_specs=pl.BlockSpec((NDEV * rp, LANE), lambda i: (0, i)),
        compiler_params=_params(("parallel",)),
    )(main, cut)


def _sum_slabs(name, a):
    s, r, c = a.shape

    def body(a_ref, o_ref):
        acc = a_ref[0].astype(F32)
        for i in range(1, s):
            acc = acc + a_ref[i].astype(F32)
        o_ref[...] = acc

    return pl.pallas_call(body, name=name, out_shape=jax.ShapeDtypeStruct((r, c), F32))(a)


def _adamw_update(g, w, m, v):
    mn = ADAM_B1 * m + (1.0 - ADAM_B1) * g
    vn = ADAM_B2 * v + (1.0 - ADAM_B2) * (g * g)
    m_hat = mn / (1.0 - ADAM_B1 ** ADAM_STEP)
    v_hat = vn / (1.0 - ADAM_B2 ** ADAM_STEP)
    delta = -ADAM_LR * (m_hat / (jnp.sqrt(v_hat) + ADAM_EPS) + ADAM_WD * w)
    return delta, mn, vn


def _adamw(name, gsrc, w, m, v, transposed=False, tr=256):
    s = gsrc.shape[0]
    r, c = w.shape
    step = LANE if transposed else 8
    tr = max(t for t in range(step, min(tr, r) + 1, step) if r % t == 0)

    def body(g_ref, w_ref, m_ref, v_ref, go_ref, d_ref, mo_ref, vo_ref):
        g = g_ref[0].astype(F32)
        for i in range(1, s):
            g = g + g_ref[i].astype(F32)
        if transposed:
            g = g.T[:, :c]
        delta, mn, vn = _adamw_update(g, w_ref[...], m_ref[...], v_ref[...])
        go_ref[...] = g
        d_ref[...] = delta
        mo_ref[...] = mn
        vo_ref[...] = vn

    blk = pl.BlockSpec((tr, c), lambda i: (i, 0))
    if transposed:
        g_spec = pl.BlockSpec((s, gsrc.shape[1], tr), lambda i: (0, 0, i))
    else:
        g_spec = pl.BlockSpec((s, tr, c), lambda i: (0, i, 0))
    return pl.pallas_call(
        body, name=name, out_shape=[jax.ShapeDtypeStruct((r, c), F32)] * 4,
        grid=(r // tr,),
        in_specs=[g_spec, blk, blk, blk], out_specs=[blk] * 4,
        compiler_params=_params(("parallel",)),
    )(gsrc, w, m, v)


REPLICATED = ("ffn1_norm", "mix_norm", "q_norm", "k_norm", "conv_b", "dt_bias", "a_log",
              "d_skip", "ssd_norm", "ffn2_norm")
ALL_WEIGHTS = ("ffn1_norm", "ffn1_w_gate", "ffn1_w_up", "ffn1_w_down", "mix_norm", "w_in",
               "q_norm", "k_norm", "conv_w", "conv_b", "dt_bias", "a_log", "d_skip", "ssd_norm",
               "w_attn_branch", "w_ssd_branch", "w_out", "ffn2_norm", "ffn2_w_gate", "ffn2_w_up",
               "ffn2_w_down")
BIG = (("ffn1_w_gate", True, "gu1t", 0), ("ffn1_w_up", True, "gu1t", 1),
       ("ffn1_w_down", False, "d1", 0), ("w_in", True, "wint", 0),
       ("w_attn_branch", True, "abt", 0), ("w_ssd_branch", False, "sb", 0),
       ("w_out", False, "out", 0),
       ("ffn2_w_gate", True, "gu2t", 0), ("ffn2_w_up", True, "gu2t", 1),
       ("ffn2_w_down", False, "d2", 0))


def _nrows(shape, cols):
    return -(-math.prod(shape) // cols)


def _pack_rows(arrs, cols, row_tile):
    parts = []
    for a in arrs:
        flat = a.reshape(-1)
        nr = -(-flat.shape[0] // cols)
        parts.append(jnp.pad(flat, (0, nr * cols - flat.shape[0])).reshape(nr, cols))
    out = jnp.concatenate(parts, axis=0)
    return jnp.pad(out, ((0, _round_up(out.shape[0], row_tile) - out.shape[0]), (0, 0)))


def _unpack_rows(packed, shapes):
    cols = packed.shape[-1]
    out, r0 = [], 0
    for sh in shapes:
        nr = _nrows(sh, cols)
        out.append(packed[r0:r0 + nr].reshape(-1)[:math.prod(sh)].reshape(tuple(sh)))
        r0 += nr
    return out


def kernel(x, ffn1_norm, ffn1_w_gate, ffn1_w_up, ffn1_w_down, mix_norm, w_in, q_norm, k_norm, conv_w, conv_b, dt_bias, a_log, d_skip, ssd_norm, w_attn_branch, w_ssd_branch, w_out, ffn2_norm, ffn2_w_gate, ffn2_w_up, ffn2_w_down, loss_target, m_ffn1_norm, m_ffn1_w_gate, m_ffn1_w_up, m_ffn1_w_down, m_mix_norm, m_w_in, m_q_norm, m_k_norm, m_conv_w, m_conv_b, m_dt_bias, m_a_log, m_d_skip, m_ssd_norm, m_w_attn_branch, m_w_ssd_branch, m_w_out, m_ffn2_norm, m_ffn2_w_gate, m_ffn2_w_up, m_ffn2_w_down, v_ffn1_norm, v_ffn1_w_gate, v_ffn1_w_up, v_ffn1_w_down, v_mix_norm, v_w_in, v_q_norm, v_k_norm, v_conv_w, v_conv_b, v_dt_bias, v_a_log, v_d_skip, v_ssd_norm, v_w_attn_branch, v_w_ssd_branch, v_w_out, v_ffn2_norm, v_ffn2_w_gate, v_ffn2_w_up, v_ffn2_w_down):
    given = dict(locals())
    wts = {n: given[n] for n in ALL_WEIGHTS}
    mom = {n: given["m_" + n] for n in ALL_WEIGHTS}
    var = {n: given["v_" + n] for n in ALL_WEIGHTS}
    d = x.shape[-1]
    nh = dt_bias.shape[1]
    my = _dev_index(_mesh_pos())

    def row_form(n, col_sharded):
        a = wts[n][0].T if col_sharded else wts[n][0]
        return jnp.pad(a, ((0, _round_up(a.shape[0], ROW_ALIGN) - a.shape[0]), (0, 0)))

    _Order.tokens, _Order.last = [], None
    shard = {n: row_form(n, cs).astype(MXU_DTYPE) for n, cs, _, _ in BIG}
    entries = {buf: [e for e in BIG if e[2] == buf] for buf in dict.fromkeys(e[2] for e in BIG)}

    def buf_shape(buf):
        r, c = shard[entries[buf][0][0]].shape
        return (len(entries[buf]) * NDEV * r, c)

    def gather_plan(bufs):
        srcs, lands, plan = [], [], []
        for li, buf in enumerate(bufs):
            lands.append(lax.empty(buf_shape(buf), MXU_DTYPE))
            for n, _, _, pos in entries[buf]:
                r = shard[n].shape[0]
                plan.append((len(srcs), 0, 0, li, pos * NDEV * r, r, r))
                srcs.append(shard[n])
        return srcs, lands, plan

    def scatter_plan(bufs, grads):
        srcs, lands, plan, names = [], [], [], []
        for si, buf in enumerate(bufs):
            srcs.append(grads[buf])
            for n, _, _, pos in entries[buf]:
                r, c = shard[n].shape
                plan.append((si, pos * NDEV * r, r, len(lands), 0, r, r))
                lands.append(lax.empty((NDEV * r, c), MXU_DTYPE))
                names.append(n)
        return srcs, lands, plan, names

    first_bufs = ("gu1t", "d1")
    shards, dests, out_shapes = [], [], []
    for bi, buf in enumerate(first_bufs):
        out_shapes.append((buf_shape(buf), MXU_DTYPE))
        for n, _, _, pos in entries[buf]:
            r = shard[n].shape[0]
            shards.append(shard[n])
            dests.append((bi, pos * NDEV * r, r))
    conv_rows = _pack_rows([conv_w[0]], LANE, ROW_ALIGN)
    shards.append(conv_rows)
    dests.append((len(first_bufs), 0, conv_rows.shape[0]))
    out_shapes.append(((NDEV * conv_rows.shape[0], LANE), F32))
    gathered = _gather("gather_first", shards, dests, out_shapes)

    in_cols = w_in.shape[2]
    in_pad = _round_up(in_cols, ROW_ALIGN)
    dt_off = NDEV * in_cols - 2 * d - nh
    second_bufs = ("wint",)
    third_bufs = ("abt", "sb", "out", "gu2t", "d2")
    plan2 = gather_plan(second_bufs)
    started2 = _split_start("gather_in_start", *plan2, after=[gathered[0]],
                            relations=SAME_CORE_AND_SIBLING)
    forwarded, started3 = [], []

    class Weights(dict):
        def __missing__(self, key):
            if key == "d1":
                lands = _split_wait("gather_in_wait", started2, plan2[2],
                                    relations=SAME_CORE_AND_SIBLING)
                forwarded.append(_forward_start("gather_in_forward", lands, plan2[2]))
                self["d1"] = gathered[1]
            elif key in ("maint", "dtt"):
                wint = _forward_wait("gather_in_arrive", forwarded[0], plan2[2])[0]
                plan3 = gather_plan(third_bufs)
                started3.append((_split_start("gather_rest_start", *plan3, after=[wint]), plan3[2]))
                self["maint"], self["dtt"] = _regroup_rows(
                    "regroup_w_in", wint, in_cols, in_pad, dt_off, dt_off + nh)
            else:
                st, plan = started3[0]
                for buf, a in zip(third_bufs, _split_wait("gather_rest_wait", st, plan)):
                    self[buf] = a
            return self[key]

    w = Weights(gu1t=gathered[0])
    p = {n: wts[n] for n in REPLICATED}
    conv_all = gathered[-1].reshape(NDEV, conv_rows.shape[0] * LANE)[:, :math.prod(conv_w.shape[1:])]
    p["conv_w"] = (conv_all.reshape((NDEV,) + conv_w.shape[1:]).transpose(1, 0, 2)
                   .reshape(conv_w.shape[1], NDEV * conv_w.shape[2]))

    groups = (("scatter_late", ("gu2t", "d2", "out", "abt", "sb")),
              ("scatter_in", ("maint", "dtt")),
              ("scatter_first", ("gu1t", "d1")))
    in_flight = []

    class Grads(dict):
        def __setitem__(self, key, value):
            dict.__setitem__(self, key, value)
            for tag, need in groups:
                if key in need and all(k in self for k in need):
                    if tag == "scatter_in":
                        gwin = _ungroup_rows("ungroup_w_in", self["maint"], self["dtt"],
                                             in_cols, in_pad, dt_off, dt_off + nh)
                        bufs, grads = ("wint",), {"wint": gwin}
                    else:
                        bufs, grads = need, self
                    srcs, lands, plan, names = scatter_plan(bufs, grads)
                    in_flight.append((tag, _split_start(tag + "_start", srcs, lands, plan),
                                      plan, names))

    loss_vec, dx, gw, gp = _local_step(x[0], loss_target[0], w, p, Grads(),
                                       aw=w_attn_branch.shape[1])

    small_names = REPLICATED + ("conv_w",)
    small_shapes = [gp[n].shape for n in small_names]
    small = _pack_rows([gp[n] for n in small_names], LANE, 8)
    small_all = _exchange("exchange_small", [], [], small)[0]

    outs = [{}, {}, {}, {}]
    col_sharded_of = {n: cs for n, cs, _, _ in BIG}
    for tag, started, plan, names in in_flight:
        for n, rv in zip(names, _split_wait(tag + "_wait", started, plan)):
            rv = rv.reshape(NDEV, shard[n].shape[0], shard[n].shape[1])
            res = _Order.done(_adamw("adamw_" + n, rv, wts[n][0], mom[n][0], var[n][0],
                                     transposed=col_sharded_of[n]))
            for k in range(4):
                outs[k][n] = res[k][None]

    small_g = _unpack_rows(_sum_slabs("sum_small_grads", small_all), small_shapes)
    small_g = dict(zip(small_names, small_g))
    cs = conv_w.shape[2]
    small_g["conv_w"] = lax.dynamic_slice_in_dim(small_g["conv_w"], my * cs, cs, axis=1)
    small_shard_shapes = [wts[n].shape[-2:] for n in small_names]
    sg = _pack_rows([small_g[n] for n in small_names], LANE, 8)
    sw = _pack_rows([wts[n] for n in small_names], LANE, 8)
    sm = _pack_rows([mom[n] for n in small_names], LANE, 8)
    sv = _pack_rows([var[n] for n in small_names], LANE, 8)
    res_small = _adamw("adamw_small", sg[None], sw, sm, sv, tr=sg.shape[0])
    for k in range(4):
        for n, a in zip(small_names, _unpack_rows(res_small[k], small_shard_shapes)):
            outs[k][n] = a.reshape(wts[n].shape)

    loss = lax.psum(0.5 * jnp.sum(loss_vec) / d, ("x", "y", "c"))
    result = [loss, dx[None]]
    for k in range(4):
        result += [outs[k][n] for n in ALL_WEIGHTS]
    return tuple(result)
```

```python
import functools
import math

import numpy as np
import jax
import jax.numpy as jnp
from jax import lax
from jax.experimental import pallas as pl
from jax.experimental.pallas import tpu as pltpu

F32 = jnp.float32
BF16 = jnp.bfloat16
MXU_DTYPE = BF16
ACT_DTYPE = BF16

NDEV = 8
EPS = 1e-6
HD = 64
QB = 128
PATTERNS = ((128, 1), (512, 4), (2048, 16))
ALIBI_MAX_EXP = 8.0
SSD_P = 64
SSD_N = 128
SSD_G = 4
SSD_Q = 128
SSD_K = 4
NEG = -1e30
LANE = 128
ROW_ALIGN = 16
VMEM_LIMIT = 56 * 1024 * 1024

ADAM_LR, ADAM_B1, ADAM_B2, ADAM_EPS, ADAM_WD, ADAM_STEP = 0.001, 0.9, 0.999, 1e-8, 0.01, 10

NN = (((1,), (0,)), ((), ()))
NT = (((1,), (1,)), ((), ()))
TN = (((0,), (0,)), ((), ()))


def _dot(a, b, dims=NN):
    return lax.dot_general(a.astype(MXU_DTYPE), b.astype(MXU_DTYPE), dims,
                           preferred_element_type=F32)


def _split3(a):
    hi = a.astype(BF16)
    r = a - hi.astype(F32)
    mid = r.astype(BF16)
    lo = (r - mid.astype(F32)).astype(BF16)
    return hi, mid, lo


def _dot3(a, b, dims=NN, split=0):
    if split == 0:
        bb = b.astype(BF16)
        parts = [lax.dot_general(s, bb, dims, preferred_element_type=F32) for s in _split3(a)]
    else:
        aa = a.astype(BF16)
        parts = [lax.dot_general(aa, s, dims, preferred_element_type=F32) for s in _split3(b)]
    return parts[0] + parts[1] + parts[2]


@jax.custom_vjp
def _spread(v, e):
    return _dot3(v, e)


def _spread_fwd(v, e):
    return _dot3(v, e), e


def _spread_bwd(e, g):
    return _dot3(g, e, NT), jnp.zeros_like(e)


_spread.defvjp(_spread_fwd, _spread_bwd)


@jax.custom_vjp
def _running_sum(a, lower):
    return _dot3(lower, a, NN, split=1)


def _running_sum_fwd(a, lower):
    return _dot3(lower, a, NN, split=1), lower


def _running_sum_bwd(lower, g):
    return _dot3(lower, g, TN, split=1), jnp.zeros_like(lower)


_running_sum.defvjp(_running_sum_fwd, _running_sum_bwd)


def _tile(n, cap):
    if n <= cap:
        return n
    best = None
    for t in range(LANE, cap + 1, LANE):
        if n % t == 0:
            best = t
    assert best is not None, (n, cap)
    return best


def _params(sem):
    return pltpu.CompilerParams(dimension_semantics=sem, vmem_limit_bytes=VMEM_LIMIT)


def _round_up(n, m):
    return -(-n // m) * m


class _Order:
    tokens = []
    last = None

    @classmethod
    def take(cls):
        out, cls.tokens = cls.tokens, []
        return out

    @classmethod
    def done(cls, result):
        cls.last = result[0] if isinstance(result, (list, tuple)) else result
        return result


ANY_SPEC = pl.BlockSpec(memory_space=pl.ANY)


def _mm(name, a, b, mode, out_dtype=F32, res=None, scale=1.0,
        cap_m=1408, cap_n=1408, cap_k=1408):
    segs = list(a) if isinstance(a, (list, tuple)) else [a]
    nseg = len(segs)
    if mode == "tn":
        k = segs[0].shape[0]
        widths = [s.shape[1] for s in segs]
        m = sum(widths)
        k2, n = b.shape
        tm = _tile(math.gcd(*widths), cap_m)
        tk = _tile(k, cap_k)
        counts = [wd // tm for wd in widths]
    else:
        m = segs[0].shape[0]
        widths = [s.shape[1] for s in segs]
        k = sum(widths)
        (k2, n) = b.shape if mode == "nn" else b.shape[::-1]
        tm = _tile(m, cap_m)
        tk = _tile(math.gcd(*widths), cap_k)
        counts = [wd // tk for wd in widths]
    assert k == k2, (name, [s.shape for s in segs], b.shape, mode)
    tn = _tile(n, cap_n)
    nk = k // tk
    starts = [sum(counts[:s]) for s in range(nseg)]
    dims = {"nn": NN, "nt": NT, "tn": TN}[mode]

    def a_spec(s):
        lo, cnt = starts[s], counts[s]
        if mode == "tn":
            if nseg == 1:
                return pl.BlockSpec((tk, tm), lambda i, j, kk: (kk, i))
            return pl.BlockSpec(
                (tk, tm), lambda i, j, kk: (jnp.where((i >= lo) & (i < lo + cnt), kk, 0),
                                            jnp.clip(i - lo, 0, cnt - 1)))
        if nseg == 1:
            return pl.BlockSpec((tm, tk), lambda i, j, kk: (i, kk))
        return pl.BlockSpec((tm, tk), lambda i, j, kk: (i, jnp.clip(kk - lo, 0, cnt - 1)))

    b_spec = (pl.BlockSpec((tn, tk), lambda i, j, kk: (j, kk)) if mode == "nt"
              else pl.BlockSpec((tk, tn), lambda i, j, kk: (kk, j)))
    o_spec = pl.BlockSpec((tm, tn), lambda i, j, kk: (i, j))
    has_res = res is not None
    use_acc = nk > 1 or nseg > 1
    ties = _Order.take()
    nt_ = len(ties)

    def body(*refs):
        a_refs = refs[:nseg]
        b_ref = refs[nseg]
        r_ref = refs[nseg + 1] if has_res else None
        o_ref = refs[nseg + 1 + has_res + nt_]
        scr = refs[nseg + 2 + has_res + nt_:]

        def finish(acc):
            if scale != 1.0:
                acc = acc * scale
            if has_res:
                acc = r_ref[...].astype(F32) + acc
            o_ref[...] = acc.astype(o_ref.dtype)

        if not use_acc:
            finish(_dot(a_refs[0][...], b_ref[...], dims))
            return
        acc_ref = scr[0]
        kk = pl.program_id(2)
        sel = pl.program_id(0) if mode == "tn" else kk

        @pl.when(kk == 0)
        def _():
            acc_ref[...] = jnp.zeros_like(acc_ref)

        for s in range(nseg):
            def add(s=s):
                acc_ref[...] += _dot(a_refs[s][...], b_ref[...], dims)
            if nseg == 1:
                add()
            else:
                pl.when((sel >= starts[s]) & (sel < starts[s] + counts[s]))(add)

        @pl.when(kk == nk - 1)
        def _():
            finish(acc_ref[...])

    in_specs = ([a_spec(s) for s in range(nseg)] + [b_spec] + ([o_spec] if has_res else [])
                + [ANY_SPEC] * nt_)
    args = tuple(segs) + (b,) + ((res,) if has_res else ()) + tuple(ties)
    return _Order.done(pl.pallas_call(
        body, name=name,
        out_shape=jax.ShapeDtypeStruct((m, n), out_dtype),
        grid=(m // tm, n // tn, nk),
        in_specs=in_specs, out_specs=o_spec,
        scratch_shapes=[pltpu.VMEM((tm, tn), F32)] if use_acc else [],
        compiler_params=_params(("parallel", "parallel", "arbitrary")),
    )(*args))


def _act(g, u):
    return _silu(g.astype(F32)) * u.astype(F32)


def _ffn_up(name, h, wgut, cap_m=512, cap_n=1408):
    m, k = h.shape
    dff = wgut.shape[0] // 2
    tm, tn = _tile(m, cap_m), _tile(dff, cap_n)
    nj = dff // tn
    ties = _Order.take()

    def body(h_ref, wg_ref, wu_ref, *rest):
        g_ref, u_ref, a_ref = rest[len(ties):]
        hv = h_ref[...]
        g = _dot(hv, wg_ref[...], NT)
        u = _dot(hv, wu_ref[...], NT)
        g_ref[...] = g.astype(g_ref.dtype)
        u_ref[...] = u.astype(u_ref.dtype)
        a_ref[...] = _act(g, u).astype(a_ref.dtype)

    o_spec = pl.BlockSpec((tm, tn), lambda i, j: (i, j))
    return _Order.done(pl.pallas_call(
        body, name=name, out_shape=[jax.ShapeDtypeStruct((m, dff), ACT_DTYPE)] * 3,
        grid=(m // tm, nj),
        in_specs=[pl.BlockSpec((tm, k), lambda i, j: (i, 0)),
                  pl.BlockSpec((tn, k), lambda i, j: (j, 0)),
                  pl.BlockSpec((tn, k), lambda i, j: (nj + j, 0))] + [ANY_SPEC] * len(ties),
        out_specs=[o_spec] * 3,
        compiler_params=_params(("parallel", "parallel")),
    )(h, wgut, wgut, *ties))


def _ffn_dact(name, dxo, wd, g, u, scale, cap_m=512, cap_n=1408):
    m, k = dxo.shape
    dff = wd.shape[0]
    tm, tn = _tile(m, cap_m), _tile(dff, cap_n)
    ties = _Order.take()

    def body(d_ref, w_ref, g_ref, u_ref, *rest):
        dg_ref, du_ref = rest[len(ties):]
        da = _dot(d_ref[...], w_ref[...], NT) * scale
        _, vjp = jax.vjp(_act, g_ref[...], u_ref[...])
        dg, du = vjp(da)
        dg_ref[...] = dg.astype(dg_ref.dtype)
        du_ref[...] = du.astype(du_ref.dtype)

    o_spec = pl.BlockSpec((tm, tn), lambda i, j: (i, j))
    return _Order.done(pl.pallas_call(
        body, name=name, out_shape=[jax.ShapeDtypeStruct((m, dff), ACT_DTYPE)] * 2,
        grid=(m // tm, dff // tn),
        in_specs=[pl.BlockSpec((tm, k), lambda i, j: (i, 0)),
                  pl.BlockSpec((tn, k), lambda i, j: (j, 0)), o_spec, o_spec]
        + [ANY_SPEC] * len(ties),
        out_specs=[o_spec] * 2,
        compiler_params=_params(("parallel", "parallel")),
    )(dxo, wd, g, u, *ties))


def _rw(name, fn, ins, outs, accs=(), tr=256, ncb=1):
    t = next(a.shape[0] for kind, a, _, _ in ins if kind == "row")
    assert t % tr == 0
    n_in = len(ins)
    n_pieces = sum(len(w) for w, _ in outs)

    def spec(kind, arr, width, base):
        if kind == "row":
            return pl.BlockSpec((tr, width), lambda j, i: (i, base + j))
        return pl.BlockSpec((arr.shape[0], width), lambda j, i: (0, base + j))

    in_specs = [spec(*s) for s in ins]
    out_shapes, out_specs = [], []
    for widths, dt in outs:
        w = sum(widths)
        out_shapes.append(jax.ShapeDtypeStruct((t, w * ncb), dt))
        out_specs.append(pl.BlockSpec((tr, w), lambda j, i: (i, j)))
    for rows, width in accs:
        out_shapes.append(jax.ShapeDtypeStruct((rows, width * ncb), F32))
        out_specs.append(pl.BlockSpec((rows, width), lambda j, i: (0, j)))

    ties = _Order.take()
    nt_ = len(ties)
    in_specs = in_specs + [ANY_SPEC] * nt_

    def body(*refs):
        vals = [r[...] for r in refs[:n_in]]
        res = fn(*vals)
        o_refs = refs[n_in + nt_:n_in + nt_ + len(outs)]
        a_refs = refs[n_in + nt_ + len(outs):]
        p = 0
        for (widths, _), o_ref in zip(outs, o_refs):
            off = 0
            for w in widths:
                if len(widths) == 1:
                    o_ref[...] = res[p].astype(o_ref.dtype)
                else:
                    o_ref[:, off:off + w] = res[p].astype(o_ref.dtype)
                off += w
                p += 1
        i = pl.program_id(1)
        for a_ref, v in zip(a_refs, res[n_pieces:]):
            @pl.when(i == 0)
            def _(a_ref=a_ref, v=v):
                a_ref[...] = v

            @pl.when(i > 0)
            def _(a_ref=a_ref, v=v):
                a_ref[...] += v

    return _Order.done(pl.pallas_call(
        body, name=name, out_shape=out_shapes,
        grid=(ncb, t // tr), in_specs=in_specs, out_specs=out_specs,
        compiler_params=_params(("parallel", "arbitrary")),
    )(*[a for _, a, _, _ in ins], *ties))


def _rms(x, g):
    x = x.astype(F32)
    return x * lax.rsqrt(jnp.mean(x * x, axis=-1, keepdims=True) + EPS) * g


def _silu(x):
    return x * jax.nn.sigmoid(x)


def _colsum(v):
    return jnp.sum(v, axis=0, keepdims=True)


def _pair_norm(x, g):
    w = 2 * HD
    ri = lax.broadcasted_iota(jnp.int32, (w, w), 0)
    ci = lax.broadcasted_iota(jnp.int32, (w, w), 1)
    same_head = ((ri < HD) == (ci < HD)).astype(F32)
    ms = _spread(x * x, same_head) * (1.0 / HD)
    return x * lax.rsqrt(ms + EPS) * g


ATTN_SCALE = 1.0 / math.sqrt(HD)


def _attn_bias(coef):
    key = lax.broadcasted_iota(jnp.int32, (QB, QB), 0)
    qry = lax.broadcasted_iota(jnp.int32, (QB, QB), 1)
    dist = (qry - key).astype(F32)
    own = jnp.where(qry >= key, -coef * dist, NEG)
    prev = jnp.where(qry <= key, -coef * (dist + float(QB)), NEG)
    return own, prev


BNT = (((2,), (2,)), ((0,), (0,)))
BTN = (((1,), (1,)), ((0,), (0,)))


def _attn_pair(qn, kcn, kpn, vc, vp, b_own, b_prev):
    nb = qn.shape[0]
    w = 2 * HD
    lane = lax.broadcasted_iota(jnp.int32, (1, 1, w), 2)
    eye = (lax.broadcasted_iota(jnp.int32, (QB, QB), 0)
           == lax.broadcasted_iota(jnp.int32, (QB, QB), 1)).astype(F32)
    out = jnp.zeros((nb, QB, w), F32)
    lb = jnp.zeros((nb * QB, w), F32)
    for hh in range(2):
        mask = ((lane < HD) if hh == 0 else (lane >= HD)).astype(F32)
        qm = qn * mask
        lc = _dot(kcn, qm, BNT) + b_own[hh]
        lp = _dot(kpn, qm, BNT) + b_prev[hh]
        m = lax.stop_gradient(jnp.maximum(jnp.max(lc, axis=1, keepdims=True),
                                          jnp.max(lp, axis=1, keepdims=True)))
        pc = jnp.exp(lc - m)
        pp = jnp.exp(lp - m)
        l = jnp.sum(pc, axis=1, keepdims=True) + jnp.sum(pp, axis=1, keepdims=True)
        inv = 1.0 / l
        out = out + (_dot(pc * inv, vc, BTN) + _dot(pp * inv, vp, BTN)) * mask
        diag = (eye * (m + jnp.log(l))).reshape(nb * QB, QB)
        lb = lb + _spread(diag, jnp.broadcast_to(mask[0], (QB, w)))
    return out, lb.reshape(nb, QB, w)


NORM_ROWS = 128
NORM_UNROLL = 4
EPILOGUE_ROWS = 512
ATTN_BATCH_FWD = 8
ATTN_BATCH_BWD = 4


def _unit_rows(u, d):
    r = u & (d - 1)
    n = u >> (d.bit_length() - 1)

    def rows(blk):
        start = pl.multiple_of(blk * (QB * d), QB * d)
        return pl.ds(start, QB) if d == 1 else pl.ds(start + r, QB, stride=d)

    return rows(n), rows(jnp.maximum(n - 1, 0)), n == 0


def _unit_batch(i, nbatch, d, bias, qf, kf, vf):
    units = [_unit_rows(i * nbatch + j, d) for j in range(nbatch)]
    cur = lambda ref: jnp.stack([ref[c, :] for c, _, _ in units])
    prv = lambda ref: jnp.stack([ref[p, :] for _, p, _ in units])
    b_own = [b[0] for b in bias]
    b_prev = [jnp.stack([jnp.where(first, NEG, b[1]) for _, _, first in units]) for b in bias]
    return units, (cur(qf), cur(kf), prv(kf), cur(vf), prv(vf), b_own, b_prev)


def _q_norm(x, g):
    return _pair_norm(x, g * ATTN_SCALE)


def _attn_prologue(t, q_ref, k_ref, v_ref, qg_ref, kg_ref, qf, kf, vf):
    def chunk(c, carry):
        rows = pl.ds(pl.multiple_of(c * NORM_ROWS, NORM_ROWS), NORM_ROWS)
        qf[rows, :] = _q_norm(q_ref[rows, :].astype(F32), qg_ref[...])
        kf[rows, :] = _pair_norm(k_ref[rows, :].astype(F32), kg_ref[...])
        vf[rows, :] = v_ref[rows, :].astype(F32)
        return carry
    lax.fori_loop(0, t // NORM_ROWS, chunk, 0, unroll=NORM_UNROLL)


def _attn_specs(t, bases):
    w = 2 * HD
    ins = [pl.BlockSpec((t, w), functools.partial(lambda p, c, b: (0, b + p), b=b)) for b in bases]
    gain = pl.BlockSpec((1, w), lambda p, c: (0, 0))
    blk = pl.BlockSpec((t, w), lambda p, c: (0, p))
    return ins, gain, blk


def _attn_fwd(name, proj, bases, qg, kg, coefs, d):
    t = proj.shape[0]
    npairs = coefs.shape[0] // 2
    w = 2 * HD
    ins, gain, blk = _attn_specs(t, bases)

    def body(coef_ref, q_ref, k_ref, v_ref, qg_ref, kg_ref, o_ref, l_ref, qf, kf, vf):
        p = pl.program_id(0)
        bias = (_attn_bias(coef_ref[2 * p]), _attn_bias(coef_ref[2 * p + 1]))
        _attn_prologue(t, q_ref, k_ref, v_ref, qg_ref, kg_ref, qf, kf, vf)

        def step(i, carry):
            units, ins = _unit_batch(i, ATTN_BATCH_FWD, d, bias, qf, kf, vf)
            o, lb = _attn_pair(*ins)
            for j, (cur, _, _) in enumerate(units):
                o_ref[cur, :] = o[j]
                l_ref[cur, :] = lb[j]
            return carry

        lax.fori_loop(0, t // QB // ATTN_BATCH_FWD, step, 0)

    return pl.pallas_call(
        body, name=name,
        out_shape=[jax.ShapeDtypeStruct((t, npairs * w), F32)] * 2,
        grid_spec=pltpu.PrefetchScalarGridSpec(
            num_scalar_prefetch=1, grid=(npairs,),
            in_specs=ins + [gain, gain], out_specs=[blk, blk],
            scratch_shapes=[pltpu.VMEM((t, w), F32)] * 3),
        compiler_params=_params(("arbitrary",)),
    )(coefs, proj, proj, proj, qg, kg)


def _attn_bwd(name, proj, bases, qg, kg, coefs, d, do, dl):
    t = proj.shape[0]
    npairs = coefs.shape[0] // 2
    w = 2 * HD
    ins, gain, blk = _attn_specs(t, bases)

    def body(coef_ref, q_ref, k_ref, v_ref, qg_ref, kg_ref, do_ref, dl_ref,
             dq_ref, dk_ref, dv_ref, dqg_ref, dkg_ref, qf, kf, vf, dqf, dkf, dvf):
        p = pl.program_id(0)
        bias = (_attn_bias(coef_ref[2 * p]), _attn_bias(coef_ref[2 * p + 1]))
        _attn_prologue(t, q_ref, k_ref, v_ref, qg_ref, kg_ref, qf, kf, vf)
        dkf[...] = jnp.zeros_like(dkf)
        dvf[...] = jnp.zeros_like(dvf)

        def step(i, carry):
            units, ins = _unit_batch(i, ATTN_BATCH_BWD, d, bias, qf, kf, vf)
            f = lambda a, b, c, e, g: _attn_pair(a, b, c, e, g, *ins[5:])
            _, vjp = jax.vjp(f, *ins[:5])
            cot = (jnp.stack([do_ref[cur, :] for cur, _, _ in units]),
                   jnp.stack([dl_ref[cur, :] for cur, _, _ in units]))
            dq, dkc, dkp, dvc, dvp = vjp(cot)
            for j, (cur, prv, _) in enumerate(units):
                dqf[cur, :] = dq[j]
                dkf[cur, :] += dkc[j]
                dkf[prv, :] += dkp[j]
                dvf[cur, :] += dvc[j]
                dvf[prv, :] += dvp[j]
            return carry

        lax.fori_loop(0, t // QB // ATTN_BATCH_BWD, step, 0)

        def chunk(c, carry):
            dqg_acc, dkg_acc = carry
            rows = pl.ds(pl.multiple_of(c * EPILOGUE_ROWS, EPILOGUE_ROWS), EPILOGUE_ROWS)
            _, vq = jax.vjp(_q_norm, q_ref[rows, :].astype(F32), qg_ref[...])
            dq, dqg = vq(dqf[rows, :])
            _, vk = jax.vjp(_pair_norm, k_ref[rows, :].astype(F32), kg_ref[...])
            dk, dkg = vk(dkf[rows, :])
            dq_ref[rows, :] = dq.astype(dq_ref.dtype)
            dk_ref[rows, :] = dk.astype(dk_ref.dtype)
            dv_ref[rows, :] = dvf[rows, :].astype(dv_ref.dtype)
            return dqg_acc + dqg, dkg_acc + dkg

        zero = jnp.zeros((1, w), F32)
        dqg, dkg = lax.fori_loop(0, t // EPILOGUE_ROWS, chunk, (zero, zero))

        @pl.when(p == 0)
        def _():
            dqg_ref[...] = dqg
            dkg_ref[...] = dkg

        @pl.when(p > 0)
        def _():
            dqg_ref[...] += dqg
            dkg_ref[...] += dkg

    big = jax.ShapeDtypeStruct((t, npairs * w), ACT_DTYPE)
    small = jax.ShapeDtypeStruct((1, w), F32)
    return pl.pallas_call(
        body, name=name,
        out_shape=[big, big, big, small, small],
        grid_spec=pltpu.PrefetchScalarGridSpec(
            num_scalar_prefetch=1, grid=(npairs,),
            in_specs=ins + [gain, gain, blk, blk],
            out_specs=[blk, blk, blk, gain, gain],
            scratch_shapes=[pltpu.VMEM((t, w), F32)] * 6),
        compiler_params=_params(("arbitrary",)),
    )(coefs, proj, proj, proj, qg, kg, do, dl)


def _shift_down(u, s):
    if s == 0:
        return u
    rows = lax.broadcasted_iota(jnp.int32, u.shape, 0)
    return jnp.where(rows >= s, pltpu.roll(u, s, 0), 0.0)


def _shift_up(u, s):
    if s == 0:
        return u
    t = u.shape[0]
    rows = lax.broadcasted_iota(jnp.int32, u.shape, 0)
    return jnp.where(rows < t - s, pltpu.roll(u, t - s, 0), 0.0)


def _conv_pre(u, w, b):
    y = b
    for kk in range(SSD_K):
        y = y + w[kk:kk + 1, :] * _shift_down(u, SSD_K - 1 - kk)
    return y


def _conv_fwd(name, src, base, w, b, cw=128):
    t = src.shape[0]
    c = w.shape[1]

    def body(u_ref, w_ref, b_ref, o_ref):
        y = _conv_pre(u_ref[...].astype(F32), w_ref[...], b_ref[...])
        o_ref[...] = _silu(y).astype(o_ref.dtype)

    return pl.pallas_call(
        body, name=name, out_shape=jax.ShapeDtypeStruct((t, c), ACT_DTYPE),
        grid=(c // cw,),
        in_specs=[pl.BlockSpec((t, cw), lambda j: (0, base + j)),
                  pl.BlockSpec((SSD_K, cw), lambda j: (0, j)),
                  pl.BlockSpec((1, cw), lambda j: (0, j))],
        out_specs=pl.BlockSpec((t, cw), lambda j: (0, j)),
        compiler_params=_params(("parallel",)),
    )(src, w, b)


def _conv_bwd(name, src, base, w, b, dout, cw=128):
    t = src.shape[0]
    c = w.shape[1]

    def body(u_ref, w_ref, b_ref, d_ref, du_ref, dw_ref, db_ref):
        u = u_ref[...].astype(F32)
        wv = w_ref[...]
        y = _conv_pre(u, wv, b_ref[...])
        sg = jax.nn.sigmoid(y)
        dy = d_ref[...].astype(F32) * (sg * (1.0 + y * (1.0 - sg)))
        du = jnp.zeros_like(u)
        for kk in range(SSD_K):
            s = SSD_K - 1 - kk
            du = du + wv[kk:kk + 1, :] * _shift_up(dy, s)
            dw_ref[kk:kk + 1, :] = _colsum(dy * _shift_down(u, s))
        du_ref[...] = du.astype(du_ref.dtype)
        db_ref[...] = _colsum(dy)

    return pl.pallas_call(
        body, name=name,
        out_shape=[jax.ShapeDtypeStruct((t, c), ACT_DTYPE),
                   jax.ShapeDtypeStruct((SSD_K, c), F32),
                   jax.ShapeDtypeStruct((1, c), F32)],
        grid=(c // cw,),
        in_specs=[pl.BlockSpec((t, cw), lambda j: (0, base + j)),
                  pl.BlockSpec((SSD_K, cw), lambda j: (0, j)),
                  pl.BlockSpec((1, cw), lambda j: (0, j)),
                  pl.BlockSpec((t, cw), lambda j: (0, j))],
        out_specs=[pl.BlockSpec((t, cw), lambda j: (0, j)),
                   pl.BlockSpec((SSD_K, cw), lambda j: (0, j)),
                   pl.BlockSpec((1, cw), lambda j: (0, j))],
        compiler_params=_params(("parallel",)),
    )(src, w, b, dout)


def _softplus(x):
    return jnp.maximum(x, 0.0) + jnp.log(1.0 + jnp.exp(-jnp.abs(x)))


def _ssd_chunk(xbc, dtraw, bias, alog, states):
    wd = states[0].shape[1]
    nj = wd // SSD_P
    inner = SSD_G * wd
    dt = _softplus(dtraw + bias)
    a = dt * (-jnp.exp(alog))
    li = lax.broadcasted_iota(jnp.int32, (SSD_Q, SSD_Q), 0)
    si = lax.broadcasted_iota(jnp.int32, (SSD_Q, SSD_Q), 1)
    causal = li >= si
    acs = _running_sum(a, causal.astype(F32))
    acs_t = acs.T
    a_last = acs[SSD_Q - 1:SSD_Q, :]
    grow = jnp.exp(acs)
    shrink = jnp.exp(a_last - acs)
    hrow = lax.broadcasted_iota(jnp.int32, (LANE, wd), 0)
    wcol = lax.broadcasted_iota(jnp.int32, (LANE, wd), 1)
    lane = lax.broadcasted_iota(jnp.int32, (1, LANE), 1)
    ys, snext = [], []
    for g in range(SSD_G):
        lo = (hrow - g * nj) * SSD_P
        head_lanes = jnp.logical_and(wcol >= lo, wcol < lo + SSD_P).astype(F32)
        xs = xbc[:, g * wd:(g + 1) * wd]
        bm = xbc[:, inner + g * SSD_N:inner + (g + 1) * SSD_N]
        cm = xbc[:, inner + (SSD_G + g) * SSD_N:inner + (SSD_G + g + 1) * SSD_N]
        xdt = xs * _dot(dt, head_lanes)
        grow_x = _spread(grow, head_lanes)
        y_off = _dot(cm, states[g]) * grow_x
        s_new = (states[g] * grow_x[SSD_Q - 1:SSD_Q, :]
                 + _dot(bm, xdt * _dot(shrink, head_lanes), TN))
        cb = _dot(cm, bm, NT)
        pieces = []
        for i in range(wd // LANE):
            xp = xdt[:, i * LANE:(i + 1) * LANE]
            acc = jnp.zeros((SSD_Q, LANE), F32)
            for hh in range(LANE // SSD_P):
                h = g * nj + i * (LANE // SSD_P) + hh
                decay = jnp.exp(jnp.where(causal, acs[:, h:h + 1] - acs_t[h:h + 1, :], NEG))
                keep = jnp.logical_and(lane >= hh * SSD_P, lane < (hh + 1) * SSD_P).astype(F32)
                acc = acc + _dot(cb * decay, xp * keep)
            pieces.append(acc)
        y_diag = pieces[0] if len(pieces) == 1 else jnp.concatenate(pieces, axis=1)
        ys.append(y_diag + y_off)
        snext.append(s_new)
    return ys, snext


def _ssd_specs(cdim, wd, rev, nc):
    ch = (lambda c: nc - 1 - c) if rev else (lambda c: c)
    full = lambda width: pl.BlockSpec((SSD_Q, width), lambda c: (ch(c), 0))
    vec = pl.BlockSpec((1, LANE), lambda c: (0, 0))
    st = pl.BlockSpec((1, SSD_G, SSD_N, wd), lambda c: (ch(c), 0, 0, 0))
    return full, vec, st


def _ssd_fwd(name, xbc, dtraw, bias, alog, inner):
    t, cdim = xbc.shape
    wd = inner // SSD_G
    nc = t // SSD_Q
    full, vec, st = _ssd_specs(cdim, wd, False, nc)

    def body(x_ref, r_ref, b_ref, a_ref, y_ref, st_ref, s_scr):
        @pl.when(pl.program_id(0) == 0)
        def _():
            s_scr[...] = jnp.zeros_like(s_scr)

        sprev = [s_scr[g] for g in range(SSD_G)]
        ys, snext = _ssd_chunk(x_ref[...].astype(F32), r_ref[...], b_ref[...], a_ref[...], sprev)
        for g in range(SSD_G):
            st_ref[0, g] = sprev[g]
            y_ref[:, g * wd:(g + 1) * wd] = ys[g]
            s_scr[g] = snext[g]

    return pl.pallas_call(
        body, name=name,
        out_shape=[jax.ShapeDtypeStruct((t, inner), F32),
                   jax.ShapeDtypeStruct((nc, SSD_G, SSD_N, wd), F32)],
        grid=(nc,),
        in_specs=[full(cdim), full(LANE), vec, vec],
        out_specs=[full(inner), st],
        scratch_shapes=[pltpu.VMEM((SSD_G, SSD_N, wd), F32)],
        compiler_params=_params(("arbitrary",)),
    )(xbc, dtraw, bias, alog)


def _ssd_bwd(name, xbc, dtraw, bias, alog, states, dy, dxs_extra):
    t, cdim = xbc.shape
    inner = dy.shape[1]
    wd = inner // SSD_G
    nc = t // SSD_Q
    full, vec, st = _ssd_specs(cdim, wd, True, nc)

    def body(x_ref, r_ref, b_ref, a_ref, st_ref, dy_ref, dx0_ref,
             dx_ref, dr_ref, db_ref, da_ref, ds_scr):
        first = pl.program_id(0) == 0

        @pl.when(first)
        def _():
            ds_scr[...] = jnp.zeros_like(ds_scr)

        sprev = [st_ref[0, g] for g in range(SSD_G)]
        _, vjp = jax.vjp(_ssd_chunk, x_ref[...].astype(F32), r_ref[...], b_ref[...], a_ref[...],
                         sprev)
        dyv = dy_ref[...]
        dys = [dyv[:, g * wd:(g + 1) * wd] for g in range(SSD_G)]
        dsn = [ds_scr[g] for g in range(SSD_G)]
        dx, dr, db, da, dsp = vjp((dys, dsn))
        dx_ref[:, :inner] = dx[:, :inner] + dx0_ref[...].astype(F32)
        dx_ref[:, inner:] = dx[:, inner:]
        dr_ref[...] = dr
        for g in range(SSD_G):
            ds_scr[g] = dsp[g]

        @pl.when(first)
        def _():
            db_ref[...] = db
            da_ref[...] = da

        @pl.when(jnp.logical_not(first))
        def _():
            db_ref[...] += db
            da_ref[...] += da

    return pl.pallas_call(
        body, name=name,
        out_shape=[jax.ShapeDtypeStruct((t, cdim), F32),
                   jax.ShapeDtypeStruct((t, LANE), F32),
                   jax.ShapeDtypeStruct((1, LANE), F32),
                   jax.ShapeDtypeStruct((1, LANE), F32)],
        grid=(nc,),
        in_specs=[full(cdim), full(LANE), vec, vec, st, full(inner), full(inner)],
        out_specs=[full(cdim), full(LANE), vec, vec],
        scratch_shapes=[pltpu.VMEM((SSD_G, SSD_N, wd), F32)],
        compiler_params=_params(("arbitrary",)),
    )(xbc, dtraw, bias, alog, states, dy, dxs_extra)


def _mix(o0, o1, o2, l0, l1, l2):
    m = lax.stop_gradient(jnp.maximum(jnp.maximum(l0, l1), l2))
    e0, e1, e2 = jnp.exp(l0 - m), jnp.exp(l1 - m), jnp.exp(l2 - m)
    return (e0 * o0 + e1 * o1 + e2 * o2) / (e0 + e1 + e2)


def _gate(y, xs, z, dexp, gain):
    v = (y + xs.astype(F32) * dexp) * _silu(z.astype(F32))
    return _rms(v, gain)


def _merge(ga, gs, ap, sp):
    return jax.nn.sigmoid(ga.astype(F32)) * ap + jax.nn.sigmoid(gs.astype(F32)) * sp


def _alibi_coefs(hp):
    n = hp * len(PATTERNS)
    slopes = np.exp2(-ALIBI_MAX_EXP * np.arange(1, n + 1, dtype=np.float32) / n).astype(np.float32)
    return [jnp.asarray(slopes[g * hp:(g + 1) * hp] * np.float32(d))
            for g, (_, d) in enumerate(PATTERNS)]


def _local_step(x, tgt, w, p, gw_=None, aw=None):
    t, d = x.shape
    dff = w["gu1t"].shape[0] // 2
    aw = w["abt"].shape[1] if aw is None else aw
    hp = aw // HD
    qkv = len(PATTERNS) * aw
    inner = p["ssd_norm"].shape[1]
    nh = p["dt_bias"].shape[1]
    gw_ = {} if gw_ is None else gw_
    gw = inner // SSD_G
    cdim = inner + 2 * SSD_G * SSD_N
    z_off, xbc_off = 3 * qkv, 3 * qkv + inner
    ga_off = xbc_off + cdim
    gs_off = ga_off + d
    hw = d // 2
    assert z_off % gw == 0 and xbc_off % LANE == 0 and ga_off % hw == 0 and gs_off % hw == 0
    assert (nh // SSD_G) * SSD_P == gw and hp % 2 == 0 and aw % LANE == 0 and nh <= LANE
    gdt = MXU_DTYPE

    row = lambda a, width, base=0: ("row", a, width, base)
    const = lambda a, width, base=0: ("const", a, width, base)

    def rms_fwd(name, xin, g):
        return _rw(name, lambda xv, gv: (_rms(xv, gv),), [row(xin, d), const(g, d)],
                   [((d,), ACT_DTYPE)])[0]

    def rms_bwd(name, xin, g, dh, dres):
        def fn(xv, gv, dhv, drv):
            _, vjp = jax.vjp(_rms, xv, gv)
            dx, dg = vjp(dhv.astype(F32))
            return drv + dx, dg
        return _rw(name, fn, [row(xin, d), const(g, d), row(dh, d), row(dres, d)],
                   [((d,), F32)], accs=[(1, d)])

    def ffn_fwd(tag, xin, g, key_gu, key_d):
        h = rms_fwd(tag + "_norm", xin, g)
        gate, up, a = _ffn_up(tag + "_up", h, w[key_gu])
        xo = _mm(tag + "_down", a, w[key_d], "nn", F32, res=xin, scale=0.5)
        return xo, (h, gate, up, a)

    def ffn_bwd(tag, xin, g, wgut, wd, saved, dxo, key_gu, key_d):
        h, gate, up, a = saved
        gw_[key_d] = _mm(tag + "_dwd", a, dxo, "tn", gdt, scale=0.5)
        dgu = _ffn_dact(tag + "_dact", dxo, wd, gate, up, 0.5)
        gw_[key_gu] = _mm(tag + "_dwgu", dgu, h, "tn", gdt)
        dh = _mm(tag + "_dh", dgu, wgut, "nn", F32)
        return rms_bwd(tag + "_dnorm", xin, g, dh, dxo)

    x1, ffn1_saved = ffn_fwd("ffn1", x, p["ffn1_norm"], "gu1t", "d1")
    h2 = rms_fwd("mix_norm", x1, p["mix_norm"])
    proj = _mm("in_proj", h2, w["maint"], "nt", ACT_DTYPE, cap_m=512, cap_n=2944)
    dtraw = _mm("dt_proj", h2, w["dtt"], "nt", F32)

    coefs = _alibi_coefs(hp)
    qg2 = jnp.concatenate([p["q_norm"], p["q_norm"]], axis=1)
    kg2 = jnp.concatenate([p["k_norm"], p["k_norm"]], axis=1)
    pw = 2 * HD
    attn_bases = [[(off + gi * aw) // pw for off in (0, qkv, 2 * qkv)]
                  for gi in range(len(PATTERNS))]
    attn_o, attn_l = [], []
    for gi, (_, dil) in enumerate(PATTERNS):
        o, l = _attn_fwd(f"attn_fwd{gi}", proj, attn_bases[gi], qg2, kg2, coefs[gi], dil)
        attn_o.append(o)
        attn_l.append(l)
    ao = _rw("attn_mix", lambda *v: (_mix(*v),), [row(a, aw) for a in attn_o + attn_l],
             [((aw,), ACT_DTYPE)])[0]

    xbc = _conv_fwd("conv_fwd", proj, xbc_off // LANE, p["conv_w"], p["conv_b"])
    pad = lambda v: jnp.pad(v, ((0, 0), (0, LANE - nh)))
    bias_p, alog_p = pad(p["dt_bias"]), pad(p["a_log"])
    yssd, states = _ssd_fwd("ssd_fwd", xbc, dtraw, bias_p, alog_p, inner)
    dexp = jnp.repeat(p["d_skip"], SSD_P, axis=1)
    gate_ins = [row(yssd, gw), row(xbc, gw), row(proj, gw, z_off // gw),
                const(dexp, gw), const(p["ssd_norm"], gw)]
    yn = _rw("ssd_gate", lambda *v: (_gate(*v),), gate_ins, [((gw,), ACT_DTYPE)], ncb=SSD_G)[0]

    ap = _mm("attn_out", ao, w["abt"], "nt", F32)
    sp = _mm("ssd_out", yn, w["sb"], "nn", F32)
    merge_ins = [row(proj, hw, ga_off // hw), row(proj, hw, gs_off // hw), row(ap, hw), row(sp, hw)]
    mg = _rw("merge", lambda *v: (_merge(*v),), merge_ins, [((hw,), ACT_DTYPE)], ncb=2)[0]
    x2 = _mm("mix_out", mg, w["out"], "nn", F32, res=x1)
    x3, ffn2_saved = ffn_fwd("ffn2", x2, p["ffn2_norm"], "gu2t", "d2")

    def loss_fn(yv, tv):
        e = yv - tv
        return e * (1.0 / d), _colsum(e * e)
    dy, loss_vec = _rw("loss", loss_fn, [row(x3, d), row(tgt, d)], [((d,), F32)], accs=[(1, d)])

    gp = {}
    dx2, gp["ffn2_norm"] = ffn_bwd(
        "ffn2", x2, p["ffn2_norm"], w["gu2t"], w["d2"], ffn2_saved, dy, "gu2t", "d2")
    dmg = _mm("d_merge", dx2, w["out"], "nt", ACT_DTYPE)
    gw_["out"] = _mm("dw_out", mg, dx2, "tn", gdt)

    def merge_bwd(gav, gsv, apv, spv, dv):
        _, vjp = jax.vjp(_merge, gav, gsv, apv, spv)
        return vjp(dv.astype(F32))
    dga, dgs, dap, dsp = _rw("d_merge_gate", merge_bwd, merge_ins + [row(dmg, hw)],
                             [((hw,), ACT_DTYPE)] * 4, ncb=2)
    gw_["abt"] = _mm("dw_ab", dap, ao, "tn", gdt)
    dao = _mm("d_attn_o", dap, w["abt"], "nn", F32)
    gw_["sb"] = _mm("dw_sb", yn, dsp, "tn", gdt)
    dyn = _mm("d_ssd_y", dsp, w["sb"], "nt", F32)

    def gate_bwd(yv, xv, zv, dev, gv, dv):
        _, vjp = jax.vjp(_gate, yv, xv, zv, dev, gv)
        return vjp(dv)
    dyssd, dxs_gate, dz, ddexp, gp["ssd_norm"] = _rw(
        "d_ssd_gate", gate_bwd, gate_ins + [row(dyn, gw)],
        [((gw,), F32), ((gw,), F32), ((gw,), ACT_DTYPE)], accs=[(1, gw), (1, gw)], ncb=SSD_G)
    gp["d_skip"] = ddexp.reshape(nh, SSD_P).sum(axis=1).reshape(1, nh)

    dxbc, ddtraw, dbias, dalog = _ssd_bwd("ssd_bwd", xbc, dtraw, bias_p, alog_p, states,
                                          dyssd, dxs_gate)
    gp["dt_bias"], gp["a_log"] = dbias[:, :nh], dalog[:, :nh]
    du, gp["conv_w"], gp["conv_b"] = _conv_bwd("conv_bwd", proj, xbc_off // LANE,
                                               p["conv_w"], p["conv_b"], dxbc)

    def mix_bwd(*v):
        _, vjp = jax.vjp(_mix, *v[:6])
        return vjp(v[6])
    dmix = _rw("d_attn_mix", mix_bwd, [row(a, aw) for a in attn_o + attn_l] + [row(dao, aw)],
               [((aw,), F32)] * 6)
    dq, dk, dv = [], [], []
    dqg = dkg = None
    for gi, (_, dil) in enumerate(PATTERNS):
        r = _attn_bwd(f"attn_bwd{gi}", proj, attn_bases[gi], qg2, kg2, coefs[gi], dil,
                      dmix[gi], dmix[3 + gi])
        dq.append(r[0])
        dk.append(r[1])
        dv.append(r[2])
        dqg = r[3] if dqg is None else dqg + r[3]
        dkg = r[4] if dkg is None else dkg + r[4]
    gp["q_norm"] = dqg[:, :HD] + dqg[:, HD:]
    gp["k_norm"] = dkg[:, :HD] + dkg[:, HD:]

    segs = dq + dk + dv + [dz, du, dga, dgs]
    gw_["maint"] = _mm("dw_in", segs, h2, "tn", gdt)
    gw_["dtt"] = _mm("dw_dt", ddtraw, h2, "tn", gdt)
    dh2 = _mm("d_h2_main", segs, w["maint"], "nn", F32)
    dh2 = _mm("d_h2_dt", ddtraw, w["dtt"], "nn", F32, res=dh2)
    dx1, gp["mix_norm"] = rms_bwd("d_mix_norm", x1, p["mix_norm"], dh2, dx2)
    dx0, gp["ffn1_norm"] = ffn_bwd(
        "ffn1", x, p["ffn1_norm"], w["gu1t"], w["d1"], ffn1_saved, dx1, "gu1t", "d1")
    return loss_vec, dx0, gw_, gp


MESH = pl.DeviceIdType.MESH
HBM_SPEC = pl.BlockSpec(memory_space=pltpu.HBM)


def _mesh_pos():
    return lax.axis_index("x"), lax.axis_index("y"), lax.axis_index("c")


def _flip(pos, k):
    x, y, c = pos
    return (1 - x if k & 4 else x, 1 - y if k & 2 else y, 1 - c if k & 1 else c)


def _dev_index(pos):
    return 4 * pos[0] + 2 * pos[1] + pos[2]


def _rows_of(ref, base, stride, rows, pos):
    start = pl.multiple_of(base + stride * _dev_index(pos), ROW_ALIGN)
    return ref.at[pl.ds(start, rows)]


def _gather(name, shards, dests, out_shapes):
    n = len(shards)
    n_out = len(out_shapes)

    def body(*refs):
        x_refs = refs[:n]
        o_refs = refs[n:n + n_out]
        send_sems, recv_sems, local_sems = refs[n + n_out:]
        me = _mesh_pos()
        sibling = _flip(me, 1)
        chips = [_flip(me, 4), _flip(me, 2), _flip(me, 6)]

        def slot(i, block):
            k_out, base, stride = dests[i]
            return _rows_of(o_refs[k_out], base, stride, shards[i].shape[0], block)

        def copy(i, k, block, to, src=None):
            dst = slot(i, block)
            return pltpu.make_async_remote_copy(
                src_ref=dst if src is None else src, dst_ref=dst,
                send_sem=send_sems.at[7 * i + k], recv_sem=recv_sems.at[7 * i + k],
                device_id=to, device_id_type=MESH)

        mine = [pltpu.make_async_copy(x_refs[i], slot(i, me), local_sems.at[i]) for i in range(n)]
        for cp in mine:
            cp.start()
        first = []
        for i in range(n):
            first.append(copy(i, 0, me, sibling, src=x_refs[i]))
            first += [copy(i, 1 + j, me, chip, src=x_refs[i]) for j, chip in enumerate(chips)]
        for cp in first:
            cp.start()
        passed = []
        for j, chip in enumerate(chips):
            for i in range(n):
                copy(i, 1 + j, chip, me).wait_recv()
                fwd = copy(i, 4 + j, chip, sibling)
                fwd.start()
                passed.append(fwd)
        for i in range(n):
            copy(i, 0, sibling, me).wait_recv()
            for j, chip in enumerate(chips):
                copy(i, 4 + j, _flip(chip, 1), me).wait_recv()
        for cp in first + passed:
            cp.wait_send()
        for cp in mine:
            cp.wait()

    return pl.pallas_call(
        body, name=name,
        out_shape=[jax.ShapeDtypeStruct(s, dt) for s, dt in out_shapes],
        in_specs=[HBM_SPEC] * n, out_specs=[HBM_SPEC] * n_out,
        scratch_shapes=[pltpu.SemaphoreType.DMA((7 * n,)), pltpu.SemaphoreType.DMA((7 * n,)),
                        pltpu.SemaphoreType.DMA((n,))],
    )(*shards)


def _exchange(name, grads, srcs, small):
    n = len(srcs)
    ng = len(grads)

    def body(*refs):
        g_refs = refs[:ng]
        m_ref = refs[ng]
        r_refs = refs[ng + 1:ng + 1 + n]
        s_ref = refs[ng + 1 + n]
        send_sems, recv_sems, local_sems = refs[ng + 2 + n:]
        me = _mesh_pos()
        my = _dev_index(me)

        def slab(i, pos):
            gi, base, stride, rows = srcs[i]
            return _rows_of(g_refs[gi], base, stride, rows, pos)

        own = [pltpu.make_async_copy(slab(i, me), r_refs[i].at[my], local_sems.at[i])
               for i in range(n)]
        own.append(pltpu.make_async_copy(m_ref, s_ref.at[my], local_sems.at[n]))
        for cp in own:
            cp.start()

        def copies(k, src_pos, slot_pos):
            peer = _flip(me, k)
            si = _dev_index(slot_pos)
            out = [pltpu.make_async_remote_copy(
                src_ref=slab(i, src_pos), dst_ref=r_refs[i].at[si],
                send_sem=send_sems.at[7 * i + k - 1], recv_sem=recv_sems.at[7 * i + k - 1],
                device_id=peer, device_id_type=MESH) for i in range(n)]
            out.append(pltpu.make_async_remote_copy(
                src_ref=m_ref, dst_ref=s_ref.at[si],
                send_sem=send_sems.at[7 * n + k - 1], recv_sem=recv_sems.at[7 * n + k - 1],
                device_id=peer, device_id_type=MESH))
            return out

        sent = [cp for k in range(1, NDEV) for cp in copies(k, _flip(me, k), me)]
        for cp in sent:
            cp.start()
        for k in range(1, NDEV):
            for cp in copies(k, me, _flip(me, k)):
                cp.wait_recv()
        for cp in sent:
            cp.wait_send()
        for cp in own:
            cp.wait()

    out_shape = [jax.ShapeDtypeStruct((NDEV, rows, grads[gi].shape[1]), grads[gi].dtype)
                 for gi, _, _, rows in srcs]
    out_shape.append(jax.ShapeDtypeStruct((NDEV,) + small.shape, small.dtype))
    return pl.pallas_call(
        body, name=name, out_shape=out_shape,
        in_specs=[HBM_SPEC] * (ng + 1), out_specs=[HBM_SPEC] * (n + 1),
        scratch_shapes=[pltpu.SemaphoreType.DMA((7 * (n + 1),)),
                        pltpu.SemaphoreType.DMA((7 * (n + 1),)),
                        pltpu.SemaphoreType.DMA((n + 1,))],
    )(*grads, small)


SEM_SPEC = pl.BlockSpec(memory_space=pltpu.SEMAPHORE)
SIDE_EFFECT = pltpu.SideEffectType.DATAFLOW_SIDE_EFFECTING


def _split_refs(plan, srcs, lands, i, src_for, land_from):
    si, sbase, sstride, li, lbase, lstride, rows = plan[i]
    return (_rows_of(srcs[si], sbase, sstride, rows, src_for),
            _rows_of(lands[li], lbase, lstride, rows, land_from))


ALL_PEERS = tuple(range(1, NDEV))
SAME_CORE_AND_SIBLING = (1, 4, 2, 6)
OTHER_CHIPS = (4, 2, 6)


def _split_start(name, srcs, lands, plan, after=(), relations=ALL_PEERS):
    ns, nl, n = len(srcs), len(lands), len(plan)

    def body(*refs):
        s_refs = refs[:ns]
        l_refs = refs[ns:ns + nl]
        send_sems, recv_sems = refs[ns + nl + len(after):ns + nl + len(after) + 2]
        local_sems = refs[ns + nl + len(after) + 2]
        token = refs[ns + nl + len(after) + 3 + ns + nl]
        me = _mesh_pos()
        for i in range(n):
            src, dst = _split_refs(plan, s_refs, l_refs, i, me, me)
            pltpu.make_async_copy(src, dst, local_sems.at[i]).start()
        for k in relations:
            peer = _flip(me, k)
            for i in range(n):
                src, dst = _split_refs(plan, s_refs, l_refs, i, peer, me)
                pltpu.make_async_remote_copy(
                    src_ref=src, dst_ref=dst,
                    send_sem=send_sems.at[7 * i + k - 1], recv_sem=recv_sems.at[7 * i + k - 1],
                    device_id=peer, device_id_type=MESH).start()
        token[...] = jnp.zeros_like(token)

    hbm = lambda a: pltpu.HBM(a.shape, a.dtype)
    out_shape = ((pltpu.SemaphoreType.DMA((7 * n,)), pltpu.SemaphoreType.DMA((7 * n,)),
                  pltpu.SemaphoreType.DMA((n,)))
                 + tuple(hbm(a) for a in srcs) + tuple(hbm(a) for a in lands)
                 + (jax.ShapeDtypeStruct((8, LANE), F32),))
    out = pl.pallas_call(
        body, name=name, out_shape=out_shape,
        in_specs=[HBM_SPEC] * (ns + nl) + [ANY_SPEC] * len(after),
        out_specs=(SEM_SPEC, SEM_SPEC, SEM_SPEC) + (HBM_SPEC,) * (ns + nl)
        + (pl.BlockSpec(memory_space=pltpu.VMEM),),
        input_output_aliases={i: 3 + i for i in range(ns + nl)},
        compiler_params=pltpu.CompilerParams(has_side_effects=SIDE_EFFECT),
    )(*[pltpu.with_memory_space_constraint(a, pltpu.HBM) for a in tuple(srcs) + tuple(lands)],
      *after)
    _Order.tokens.append(out[-1])
    return out[0], out[1], out[2], out[3:3 + ns], out[3 + ns:3 + ns + nl]


def _split_wait(name, started, plan, relations=ALL_PEERS):
    send_sems, recv_sems, local_sems, srcs, lands = started
    ns, nl, n = len(srcs), len(lands), len(plan)
    after = [_Order.last] if _Order.last is not None else []

    def body(*refs):
        s_refs = refs[:ns]
        l_refs = refs[ns:ns + nl]
        send_sems, recv_sems, local_sems = refs[ns + nl:ns + nl + 3]
        me = _mesh_pos()
        for i in range(n):
            src, dst = _split_refs(plan, s_refs, l_refs, i, me, me)
            pltpu.make_async_copy(src, dst, local_sems.at[i]).wait()
        for k in relations:
            peer = _flip(me, k)
            for i in range(n):
                src, dst = _split_refs(plan, s_refs, l_refs, i, peer, peer)
                cp = pltpu.make_async_remote_copy(
                    src_ref=src, dst_ref=dst,
                    send_sem=send_sems.at[7 * i + k - 1], recv_sem=recv_sems.at[7 * i + k - 1],
                    device_id=peer, device_id_type=MESH)
                cp.wait_send()
                cp.wait_recv()

    hbm = lambda a: pltpu.HBM(a.shape, a.dtype)
    out = pl.pallas_call(
        body, name=name,
        out_shape=tuple(hbm(a) for a in srcs) + tuple(hbm(a) for a in lands),
        in_specs=[HBM_SPEC] * (ns + nl) + [SEM_SPEC] * 3 + [ANY_SPEC] * len(after),
        out_specs=(HBM_SPEC,) * (ns + nl),
        input_output_aliases={i: i for i in range(ns + nl)},
        compiler_params=pltpu.CompilerParams(has_side_effects=SIDE_EFFECT),
    )(*srcs, *lands, send_sems, recv_sems, local_sems, *after)
    return list(out[ns:])


def _forward_refs(plan, lands, i, block):
    _, _, _, li, lbase, lstride, rows = plan[i]
    return _rows_of(lands[li], lbase, lstride, rows, block)


def _forward_start(name, lands, plan):
    nl, n = len(lands), len(plan)

    def body(*refs):
        l_refs = refs[:nl]
        send_sems, recv_sems = refs[nl:nl + 2]
        token = refs[nl + 2 + nl]
        me = _mesh_pos()
        for j, kc in enumerate(OTHER_CHIPS):
            for i in range(n):
                rows = _forward_refs(plan, l_refs, i, _flip(me, kc))
                pltpu.make_async_remote_copy(
                    src_ref=rows, dst_ref=rows,
                    send_sem=send_sems.at[3 * i + j], recv_sem=recv_sems.at[3 * i + j],
                    device_id=_flip(me, 1), device_id_type=MESH).start()
        token[...] = jnp.zeros_like(token)

    hbm = lambda a: pltpu.HBM(a.shape, a.dtype)
    out = pl.pallas_call(
        body, name=name,
        out_shape=((pltpu.SemaphoreType.DMA((3 * n,)), pltpu.SemaphoreType.DMA((3 * n,)))
                   + tuple(hbm(a) for a in lands) + (jax.ShapeDtypeStruct((8, LANE), F32),)),
        in_specs=[HBM_SPEC] * nl,
        out_specs=(SEM_SPEC, SEM_SPEC) + (HBM_SPEC,) * nl
        + (pl.BlockSpec(memory_space=pltpu.VMEM),),
        input_output_aliases={i: 2 + i for i in range(nl)},
        compiler_params=pltpu.CompilerParams(has_side_effects=SIDE_EFFECT),
    )(*[pltpu.with_memory_space_constraint(a, pltpu.HBM) for a in lands])
    _Order.tokens.append(out[-1])
    return out[0], out[1], out[2:2 + nl]


def _forward_wait(name, started, plan):
    send_sems, recv_sems, lands = started
    nl, n = len(lands), len(plan)
    after = [_Order.last] if _Order.last is not None else []

    def body(*refs):
        l_refs = refs[:nl]
        send_sems, recv_sems = refs[nl:nl + 2]
        me = _mesh_pos()
        for j, kc in enumerate(OTHER_CHIPS):
            for i in range(n):
                sent = _forward_refs(plan, l_refs, i, _flip(me, kc))
                came = _forward_refs(plan, l_refs, i, _flip(_flip(me, 1), kc))
                cp = pltpu.make_async_remote_copy(
                    src_ref=sent, dst_ref=came,
                    send_sem=send_sems.at[3 * i + j], recv_sem=recv_sems.at[3 * i + j],
                    device_id=_flip(me, 1), device_id_type=MESH)
                cp.wait_send()
                cp.wait_recv()

    hbm = lambda a: pltpu.HBM(a.shape, a.dtype)
    out = pl.pallas_call(
        body, name=name, out_shape=tuple(hbm(a) for a in lands),
        in_specs=[HBM_SPEC] * nl + [SEM_SPEC] * 2 + [ANY_SPEC] * len(after),
        out_specs=(HBM_SPEC,) * nl,
        input_output_aliases={i: i for i in range(nl)},
        compiler_params=pltpu.CompilerParams(has_side_effects=SIDE_EFFECT),
    )(*lands, send_sems, recv_sems, *after)
    return list(out)


def _regroup_rows(name, padded, r, rp, lo, hi):
    d = padded.shape[1]
    pack = 4 // padded.dtype.itemsize
    assert r % pack == 0 and rp % ROW_ALIGN == 0 and lo % (8 * pack) == 0 and hi % (8 * pack) == 0
    r2, rp2, lo2, hi2 = r // pack, rp // pack, lo // pack, hi // pack
    u32 = jnp.uint32

    def body(x_ref, main_ref, cut_ref):
        x = pltpu.bitcast(x_ref[...], u32)
        joined = jnp.concatenate([x[rp2 * j:rp2 * j + r2] for j in range(NDEV)], axis=0)
        main = jnp.concatenate([joined[:lo2], joined[hi2:]], axis=0)
        cut = jnp.concatenate([joined[lo2:hi2], jnp.zeros((LANE // pack - (hi2 - lo2), LANE), u32)],
                              axis=0)
        main_ref[...] = pltpu.bitcast(main, padded.dtype)
        cut_ref[...] = pltpu.bitcast(cut, padded.dtype)

    return pl.pallas_call(
        body, name=name,
        out_shape=[jax.ShapeDtypeStruct((NDEV * r - (hi - lo), d), padded.dtype),
                   jax.ShapeDtypeStruct((LANE, d), padded.dtype)],
        grid=(d // LANE,),
        in_specs=[pl.BlockSpec((NDEV * rp, LANE), lambda i: (0, i))],
        out_specs=[pl.BlockSpec((NDEV * r - (hi - lo), LANE), lambda i: (0, i)),
                   pl.BlockSpec((LANE, LANE), lambda i: (0, i))],
        compiler_params=_params(("parallel",)),
    )(padded)


def _ungroup_rows(name, main, cut, r, rp, lo, hi):
    d = main.shape[1]
    pack = 4 // main.dtype.itemsize
    r2, rp2, lo2, hi2 = r // pack, rp // pack, lo // pack, hi // pack
    u32 = jnp.uint32

    def body(main_ref, cut_ref, o_ref):
        m = pltpu.bitcast(main_ref[...], u32)
        c = pltpu.bitcast(cut_ref[...], u32)
        joined = jnp.concatenate([m[:lo2], c[:hi2 - lo2], m[lo2:]], axis=0)
        zeros = jnp.zeros((rp2 - r2, LANE), u32)
        parts = []
        for j in range(NDEV):
            parts += [joined[r2 * j:r2 * (j + 1)], zeros]
        o_ref[...] = pltpu.bitcast(jnp.concatenate(parts, axis=0), main.dtype)

    return pl.pallas_call(
        body, name=name,
        out_shape=jax.ShapeDtypeStruct((NDEV * rp, d), main.dtype),
        grid=(d // LANE,),
        in_specs=[pl.BlockSpec((main.shape[0], LANE), lambda i: (0, i)),
                  pl.BlockSpec((LANE, LANE), lambda i: (0, i))],
        out_specs=pl.BlockSpec((NDEV * rp, LANE), lambda i: (0, i)),
        compiler_params=_params(("parallel",)),
    )(main, cut)


def _sum_slabs(name, a):
    s, r, c = a.shape

    def body(a_ref, o_ref):
        acc = a_ref[0].astype(F32)
        for i in range(1, s):
            acc = acc + a_ref[i].astype(F32)
        o_ref[...] = acc

    return pl.pallas_call(body, name=name, out_shape=jax.ShapeDtypeStruct((r, c), F32))(a)


def _adamw_update(g, w, m, v):
    mn = ADAM_B1 * m + (1.0 - ADAM_B1) * g
    vn = ADAM_B2 * v + (1.0 - ADAM_B2) * (g * g)
    m_hat = mn / (1.0 - ADAM_B1 ** ADAM_STEP)
    v_hat = vn / (1.0 - ADAM_B2 ** ADAM_STEP)
    delta = -ADAM_LR * (m_hat / (jnp.sqrt(v_hat) + ADAM_EPS) + ADAM_WD * w)
    return delta, mn, vn


def _adamw(name, gsrc, w, m, v, transposed=False, tr=256):
    s = gsrc.shape[0]
    r, c = w.shape
    step = LANE if transposed else 8
    tr = max(t for t in range(step, min(tr, r) + 1, step) if r % t == 0)

    def body(g_ref, w_ref, m_ref, v_ref, go_ref, d_ref, mo_ref, vo_ref):
        g = g_ref[0].astype(F32)
        for i in range(1, s):
            g = g + g_ref[i].astype(F32)
        if transposed:
            g = g.T[:, :c]
        delta, mn, vn = _adamw_update(g, w_ref[...], m_ref[...], v_ref[...])
        go_ref[...] = g
        d_ref[...] = delta
        mo_ref[...] = mn
        vo_ref[...] = vn

    blk = pl.BlockSpec((tr, c), lambda i: (i, 0))
    if transposed:
        g_spec = pl.BlockSpec((s, gsrc.shape[1], tr), lambda i: (0, 0, i))
    else:
        g_spec = pl.BlockSpec((s, tr, c), lambda i: (0, i, 0))
    return pl.pallas_call(
        body, name=name, out_shape=[jax.ShapeDtypeStruct((r, c), F32)] * 4,
        grid=(r // tr,),
        in_specs=[g_spec, blk, blk, blk], out_specs=[blk] * 4,
        compiler_params=_params(("parallel",)),
    )(gsrc, w, m, v)


REPLICATED = ("ffn1_norm", "mix_norm", "q_norm", "k_norm", "conv_b", "dt_bias", "a_log",
              "d_skip", "ssd_norm", "ffn2_norm")
ALL_WEIGHTS = ("ffn1_norm", "ffn1_w_gate", "ffn1_w_up", "ffn1_w_down", "mix_norm", "w_in",
               "q_norm", "k_norm", "conv_w", "conv_b", "dt_bias", "a_log", "d_skip", "ssd_norm",
               "w_attn_branch", "w_ssd_branch", "w_out", "ffn2_norm", "ffn2_w_gate", "ffn2_w_up",
               "ffn2_w_down")
BIG = (("ffn1_w_gate", True, "gu1t", 0), ("ffn1_w_up", True, "gu1t", 1),
       ("ffn1_w_down", False, "d1", 0), ("w_in", True, "wint", 0),
       ("w_attn_branch", True, "abt", 0), ("w_ssd_branch", False, "sb", 0),
       ("w_out", False, "out", 0),
       ("ffn2_w_gate", True, "gu2t", 0), ("ffn2_w_up", True, "gu2t", 1),
       ("ffn2_w_down", False, "d2", 0))


def _nrows(shape, cols):
    return -(-math.prod(shape) // cols)


def _pack_rows(arrs, cols, row_tile):
    parts = []
    for a in arrs:
        flat = a.reshape(-1)
        nr = -(-flat.shape[0] // cols)
        parts.append(jnp.pad(flat, (0, nr * cols - flat.shape[0])).reshape(nr, cols))
    out = jnp.concatenate(parts, axis=0)
    return jnp.pad(out, ((0, _round_up(out.shape[0], row_tile) - out.shape[0]), (0, 0)))


def _unpack_rows(packed, shapes):
    cols = packed.shape[-1]
    out, r0 = [], 0
    for sh in shapes:
        nr = _nrows(sh, cols)
        out.append(packed[r0:r0 + nr].reshape(-1)[:math.prod(sh)].reshape(tuple(sh)))
        r0 += nr
    return out


def kernel(x, ffn1_norm, ffn1_w_gate, ffn1_w_up, ffn1_w_down, mix_norm, w_in, q_norm, k_norm, conv_w, conv_b, dt_bias, a_log, d_skip, ssd_norm, w_attn_branch, w_ssd_branch, w_out, ffn2_norm, ffn2_w_gate, ffn2_w_up, ffn2_w_down, loss_target, m_ffn1_norm, m_ffn1_w_gate, m_ffn1_w_up, m_ffn1_w_down, m_mix_norm, m_w_in, m_q_norm, m_k_norm, m_conv_w, m_conv_b, m_dt_bias, m_a_log, m_d_skip, m_ssd_norm, m_w_attn_branch, m_w_ssd_branch, m_w_out, m_ffn2_norm, m_ffn2_w_gate, m_ffn2_w_up, m_ffn2_w_down, v_ffn1_norm, v_ffn1_w_gate, v_ffn1_w_up, v_ffn1_w_down, v_mix_norm, v_w_in, v_q_norm, v_k_norm, v_conv_w, v_conv_b, v_dt_bias, v_a_log, v_d_skip, v_ssd_norm, v_w_attn_branch, v_w_ssd_branch, v_w_out, v_ffn2_norm, v_ffn2_w_gate, v_ffn2_w_up, v_ffn2_w_down):
    given = dict(locals())
    wts = {n: given[n] for n in ALL_WEIGHTS}
    mom = {n: given["m_" + n] for n in ALL_WEIGHTS}
    var = {n: given["v_" + n] for n in ALL_WEIGHTS}
    d = x.shape[-1]
    nh = dt_bias.shape[1]
    my = _dev_index(_mesh_pos())

    def row_form(n, col_sharded):
        a = wts[n][0].T if col_sharded else wts[n][0]
        return jnp.pad(a, ((0, _round_up(a.shape[0], ROW_ALIGN) - a.shape[0]), (0, 0)))

    _Order.tokens, _Order.last = [], None
    shard = {n: row_form(n, cs).astype(MXU_DTYPE) for n, cs, _, _ in BIG}
    entries = {buf: [e for e in BIG if e[2] == buf] for buf in dict.fromkeys(e[2] for e in BIG)}

    def buf_shape(buf):
        r, c = shard[entries[buf][0][0]].shape
        return (len(entries[buf]) * NDEV * r, c)

    def gather_plan(bufs):
        srcs, lands, plan = [], [], []
        for li, buf in enumerate(bufs):
            lands.append(lax.empty(buf_shape(buf), MXU_DTYPE))
            for n, _, _, pos in entries[buf]:
                r = shard[n].shape[0]
                plan.append((len(srcs), 0, 0, li, pos * NDEV * r, r, r))
                srcs.append(shard[n])
        return srcs, lands, plan

    def scatter_plan(bufs, grads):
        srcs, lands, plan, names = [], [], [], []
        for si, buf in enumerate(bufs):
            srcs.append(grads[buf])
            for n, _, _, pos in entries[buf]:
                r, c = shard[n].shape
                plan.append((si, pos * NDEV * r, r, len(lands), 0, r, r))
                lands.append(lax.empty((NDEV * r, c), MXU_DTYPE))
                names.append(n)
        return srcs, lands, plan, names

    first_bufs = ("gu1t", "d1")
    shards, dests, out_shapes = [], [], []
    for bi, buf in enumerate(first_bufs):
        out_shapes.append((buf_shape(buf), MXU_DTYPE))
        for n, _, _, pos in entries[buf]:
            r = shard[n].shape[0]
            shards.append(shard[n])
            dests.append((bi, pos * NDEV * r, r))
    conv_rows = _pack_rows([conv_w[0]], LANE, ROW_ALIGN)
    shards.append(conv_rows)
    dests.append((len(first_bufs), 0, conv_rows.shape[0]))
    out_shapes.append(((NDEV * conv_rows.shape[0], LANE), F32))
    gathered = _gather("gather_first", shards, dests, out_shapes)

    in_cols = w_in.shape[2]
    in_pad = _round_up(in_cols, ROW_ALIGN)
    dt_off = NDEV * in_cols - 2 * d - nh
    second_bufs = ("wint",)
    third_bufs = ("abt", "sb", "out", "gu2t", "d2")
    plan2 = gather_plan(second_bufs)
    started2 = _split_start("gather_in_start", *plan2, after=[gathered[0]],
                            relations=SAME_CORE_AND_SIBLING)
    forwarded, started3 = [], []

    class Weights(dict):
        def __missing__(self, key):
            if key == "d1":
                lands = _split_wait("gather_in_wait", started2, plan2[2],
                                    relations=SAME_CORE_AND_SIBLING)
                forwarded.append(_forward_start("gather_in_forward", lands, plan2[2]))
                self["d1"] = gathered[1]
            elif key in ("maint", "dtt"):
                wint = _forward_wait("gather_in_arrive", forwarded[0], plan2[2])[0]
                plan3 = gather_plan(third_bufs)
                started3.append((_split_start("gather_rest_start", *plan3, after=[wint]), plan3[2]))
                self["maint"], self["dtt"] = _regroup_rows(
                    "regroup_w_in", wint, in_cols, in_pad, dt_off, dt_off + nh)
            else:
                st, plan = started3[0]
                for buf, a in zip(third_bufs, _split_wait("gather_rest_wait", st, plan)):
                    self[buf] = a
            return self[key]

    w = Weights(gu1t=gathered[0])
    p = {n: wts[n] for n in REPLICATED}
    conv_all = gathered[-1].reshape(NDEV, conv_rows.shape[0] * LANE)[:, :math.prod(conv_w.shape[1:])]
    p["conv_w"] = (conv_all.reshape((NDEV,) + conv_w.shape[1:]).transpose(1, 0, 2)
                   .reshape(conv_w.shape[1], NDEV * conv_w.shape[2]))

    groups = (("scatter_late", ("gu2t", "d2", "out", "abt", "sb")),
              ("scatter_in", ("maint", "dtt")),
              ("scatter_first", ("gu1t", "d1")))
    in_flight = []

    class Grads(dict):
        def __setitem__(self, key, value):
            dict.__setitem__(self, key, value)
            for tag, need in groups:
                if key in need and all(k in self for k in need):
                    if tag == "scatter_in":
                        gwin = _ungroup_rows("ungroup_w_in", self["maint"], self["dtt"],
                                             in_cols, in_pad, dt_off, dt_off + nh)
                        bufs, grads = ("wint",), {"wint": gwin}
                    else:
                        bufs, grads = need, self
                    srcs, lands, plan, names = scatter_plan(bufs, grads)
                    in_flight.append((tag, _split_start(tag + "_start", srcs, lands, plan),
                                      plan, names))

    loss_vec, dx, gw, gp = _local_step(x[0], loss_target[0], w, p, Grads(),
                                       aw=w_attn_branch.shape[1])

    small_names = REPLICATED + ("conv_w",)
    small_shapes = [gp[n].shape for n in small_names]
    small = _pack_rows([gp[n] for n in small_names], LANE, 8)
    small_all = _exchange("exchange_small", [], [], small)[0]

    outs = [{}, {}, {}, {}]
    col_sharded_of = {n: cs for n, cs, _, _ in BIG}
    for tag, started, plan, names in in_flight:
        for n, rv in zip(names, _split_wait(tag + "_wait", started, plan)):
            rv = rv.reshape(NDEV, shard[n].shape[0], shard[n].shape[1])
            res = _Order.done(_adamw("adamw_" + n, rv, wts[n][0], mom[n][0], var[n][0],
                                     transposed=col_sharded_of[n]))
            for k in range(4):
                outs[k][n] = res[k][None]

    small_g = _unpack_rows(_sum_slabs("sum_small_grads", small_all), small_shapes)
    small_g = dict(zip(small_names, small_g))
    cs = conv_w.shape[2]
    small_g["conv_w"] = lax.dynamic_slice_in_dim(small_g["conv_w"], my * cs, cs, axis=1)
    small_shard_shapes = [wts[n].shape[-2:] for n in small_names]
    sg = _pack_rows([small_g[n] for n in small_names], LANE, 8)
    sw = _pack_rows([wts[n] for n in small_names], LANE, 8)
    sm = _pack_rows([mom[n] for n in small_names], LANE, 8)
    sv = _pack_rows([var[n] for n in small_names], LANE, 8)
    res_small = _adamw("adamw_small", sg[None], sw, sm, sv, tr=sg.shape[0])
    for k in range(4):
        for n, a in zip(small_names, _unpack_rows(res_small[k], small_shard_shapes)):
            outs[k][n] = a.reshape(wts[n].shape)

    loss = lax.psum(0.5 * jnp.sum(loss_vec) / d, ("x", "y", "c"))
    result = [loss, dx[None]]
    for k in range(4):
        result += [outs[k][n] for n in ALL_WEIGHTS]
    return tuple(result)
```

```python
import functools
import math

import numpy as np
import jax
import jax.numpy as jnp
from jax import lax
from jax.experimental import pallas as pl
from jax.experimental.pallas import tpu as pltpu

F32 = jnp.float32
BF16 = jnp.bfloat16
MXU_DTYPE = BF16
ACT_DTYPE = BF16

NDEV = 8
EPS = 1e-6
HD = 64
QB = 128
PATTERNS = ((128, 1), (512, 4), (2048, 16))
ALIBI_MAX_EXP = 8.0
SSD_P = 64
SSD_N = 128
SSD_G = 4
SSD_Q = 128
SSD_K = 4
NEG = -1e30
LANE = 128
ROW_ALIGN = 16
VMEM_LIMIT = 56 * 1024 * 1024

ADAM_LR, ADAM_B1, ADAM_B2, ADAM_EPS, ADAM_WD, ADAM_STEP = 0.001, 0.9, 0.999, 1e-8, 0.01, 10

NN = (((1,), (0,)), ((), ()))
NT = (((1,), (1,)), ((), ()))
TN = (((0,), (0,)), ((), ()))


def _dot(a, b, dims=NN):
    return lax.dot_general(a.astype(MXU_DTYPE), b.astype(MXU_DTYPE), dims,
                           preferred_element_type=F32)


def _split3(a):
    hi = a.astype(BF16)
    r = a - hi.astype(F32)
    mid = r.astype(BF16)
    lo = (r - mid.astype(F32)).astype(BF16)
    return hi, mid, lo


def _dot3(a, b, dims=NN, split=0):
    if split == 0:
        bb = b.astype(BF16)
        parts = [lax.dot_general(s, bb, dims, preferred_element_type=F32) for s in _split3(a)]
    else:
        aa = a.astype(BF16)
        parts = [lax.dot_general(aa, s, dims, preferred_element_type=F32) for s in _split3(b)]
    return parts[0] + parts[1] + parts[2]


@jax.custom_vjp
def _spread(v, e):
    return _dot3(v, e)


def _spread_fwd(v, e):
    return _dot3(v, e), e


def _spread_bwd(e, g):
    return _dot3(g, e, NT), jnp.zeros_like(e)


_spread.defvjp(_spread_fwd, _spread_bwd)


@jax.custom_vjp
def _running_sum(a, lower):
    return _dot3(lower, a, NN, split=1)


def _running_sum_fwd(a, lower):
    return _dot3(lower, a, NN, split=1), lower


def _running_sum_bwd(lower, g):
    return _dot3(lower, g, TN, split=1), jnp.zeros_like(lower)


_running_sum.defvjp(_running_sum_fwd, _running_sum_bwd)


def _tile(n, cap):
    if n <= cap:
        return n
    best = None
    for t in range(LANE, cap + 1, LANE):
        if n % t == 0:
            best = t
    assert best is not None, (n, cap)
    return best


def _params(sem):
    return pltpu.CompilerParams(dimension_semantics=sem, vmem_limit_bytes=VMEM_LIMIT)


def _round_up(n, m):
    return -(-n // m) * m


class _Order:
    tokens = []
    last = None

    @classmethod
    def take(cls):
        out, cls.tokens = cls.tokens, []
        return out

    @classmethod
    def done(cls, result):
        cls.last = result[0] if isinstance(result, (list, tuple)) else result
        return result


ANY_SPEC = pl.BlockSpec(memory_space=pl.ANY)


def _mm(name, a, b, mode, out_dtype=F32, res=None, scale=1.0,
        cap_m=1408, cap_n=1408, cap_k=1408):
    segs = list(a) if isinstance(a, (list, tuple)) else [a]
    nseg = len(segs)
    if mode == "tn":
        k = segs[0].shape[0]
        widths = [s.shape[1] for s in segs]
        m = sum(widths)
        k2, n = b.shape
        tm = _tile(math.gcd(*widths), cap_m)
        tk = _tile(k, cap_k)
        counts = [wd // tm for wd in widths]
    else:
        m = segs[0].shape[0]
        widths = [s.shape[1] for s in segs]
        k = sum(widths)
        (k2, n) = b.shape if mode == "nn" else b.shape[::-1]
        tm = _tile(m, cap_m)
        tk = _tile(math.gcd(*widths), cap_k)
        counts = [wd // tk for wd in widths]
    assert k == k2, (name, [s.shape for s in segs], b.shape, mode)
    tn = _tile(n, cap_n)
    nk = k // tk
    starts = [sum(counts[:s]) for s in range(nseg)]
    dims = {"nn": NN, "nt": NT, "tn": TN}[mode]

    def a_spec(s):
        lo, cnt = starts[s], counts[s]
        if mode == "tn":
            if nseg == 1:
                return pl.BlockSpec((tk, tm), lambda i, j, kk: (kk, i))
            return pl.BlockSpec(
                (tk, tm), lambda i, j, kk: (jnp.where((i >= lo) & (i < lo + cnt), kk, 0),
                                            jnp.clip(i - lo, 0, cnt - 1)))
        if nseg == 1:
            return pl.BlockSpec((tm, tk), lambda i, j, kk: (i, kk))
        return pl.BlockSpec((tm, tk), lambda i, j, kk: (i, jnp.clip(kk - lo, 0, cnt - 1)))

    b_spec = (pl.BlockSpec((tn, tk), lambda i, j, kk: (j, kk)) if mode == "nt"
              else pl.BlockSpec((tk, tn), lambda i, j, kk: (kk, j)))
    o_spec = pl.BlockSpec((tm, tn), lambda i, j, kk: (i, j))
    has_res = res is not None
    use_acc = nk > 1 or nseg > 1
    ties = _Order.take()
    nt_ = len(ties)

    def body(*refs):
        a_refs = refs[:nseg]
        b_ref = refs[nseg]
        r_ref = refs[nseg + 1] if has_res else None
        o_ref = refs[nseg + 1 + has_res + nt_]
        scr = refs[nseg + 2 + has_res + nt_:]

        def finish(acc):
            if scale != 1.0:
                acc = acc * scale
            if has_res:
                acc = r_ref[...].astype(F32) + acc
            o_ref[...] = acc.astype(o_ref.dtype)

        if not use_acc:
            finish(_dot(a_refs[0][...], b_ref[...], dims))
            return
        acc_ref = scr[0]
        kk = pl.program_id(2)
        sel = pl.program_id(0) if mode == "tn" else kk

        @pl.when(kk == 0)
        def _():
            acc_ref[...] = jnp.zeros_like(acc_ref)

        for s in range(nseg):
            def add(s=s):
                acc_ref[...] += _dot(a_refs[s][...], b_ref[...], dims)
            if nseg == 1:
                add()
            else:
                pl.when((sel >= starts[s]) & (sel < starts[s] + counts[s]))(add)

        @pl.when(kk == nk - 1)
        def _():
            finish(acc_ref[...])

    in_specs = ([a_spec(s) for s in range(nseg)] + [b_spec] + ([o_spec] if has_res else [])
                + [ANY_SPEC] * nt_)
    args = tuple(segs) + (b,) + ((res,) if has_res else ()) + tuple(ties)
    return _Order.done(pl.pallas_call(
        body, name=name,
        out_shape=jax.ShapeDtypeStruct((m, n), out_dtype),
        grid=(m // tm, n // tn, nk),
        in_specs=in_specs, out_specs=o_spec,
        scratch_shapes=[pltpu.VMEM((tm, tn), F32)] if use_acc else [],
        compiler_params=_params(("parallel", "parallel", "arbitrary")),
    )(*args))


def _act(g, u):
    return _silu(g.astype(F32)) * u.astype(F32)


def _ffn_up(name, h, wgut, cap_m=512, cap_n=1408):
    m, k = h.shape
    dff = wgut.shape[0] // 2
    tm, tn = _tile(m, cap_m), _tile(dff, cap_n)
    nj = dff // tn
    ties = _Order.take()

    def body(h_ref, wg_ref, wu_ref, *rest):
        g_ref, u_ref, a_ref = rest[len(ties):]
        hv = h_ref[...]
        g = _dot(hv, wg_ref[...], NT)
        u = _dot(hv, wu_ref[...], NT)
        g_ref[...] = g.astype(g_ref.dtype)
        u_ref[...] = u.astype(u_ref.dtype)
        a_ref[...] = _act(g, u).astype(a_ref.dtype)

    o_spec = pl.BlockSpec((tm, tn), lambda i, j: (i, j))
    return _Order.done(pl.pallas_call(
        body, name=name, out_shape=[jax.ShapeDtypeStruct((m, dff), ACT_DTYPE)] * 3,
        grid=(m // tm, nj),
        in_specs=[pl.BlockSpec((tm, k), lambda i, j: (i, 0)),
                  pl.BlockSpec((tn, k), lambda i, j: (j, 0)),
                  pl.BlockSpec((tn, k), lambda i, j: (nj + j, 0))] + [ANY_SPEC] * len(ties),
        out_specs=[o_spec] * 3,
        compiler_params=_params(("parallel", "parallel")),
    )(h, wgut, wgut, *ties))


def _ffn_dact(name, dxo, wd, g, u, scale, cap_m=512, cap_n=1408):
    m, k = dxo.shape
    dff = wd.shape[0]
    tm, tn = _tile(m, cap_m), _tile(dff, cap_n)
    ties = _Order.take()

    def body(d_ref, w_ref, g_ref, u_ref, *rest):
        dg_ref, du_ref = rest[len(ties):]
        da = _dot(d_ref[...], w_ref[...], NT) * scale
        _, vjp = jax.vjp(_act, g_ref[...], u_ref[...])
        dg, du = vjp(da)
        dg_ref[...] = dg.astype(dg_ref.dtype)
        du_ref[...] = du.astype(du_ref.dtype)

    o_spec = pl.BlockSpec((tm, tn), lambda i, j: (i, j))
    return _Order.done(pl.pallas_call(
        body, name=name, out_shape=[jax.ShapeDtypeStruct((m, dff), ACT_DTYPE)] * 2,
        grid=(m // tm, dff // tn),
        in_specs=[pl.BlockSpec((tm, k), lambda i, j: (i, 0)),
                  pl.BlockSpec((tn, k), lambda i, j: (j, 0)), o_spec, o_spec]
        + [ANY_SPEC] * len(ties),
        out_specs=[o_spec] * 2,
        compiler_params=_params(("parallel", "parallel")),
    )(dxo, wd, g, u, *ties))


def _rw(name, fn, ins, outs, accs=(), tr=256, ncb=1):
    t = next(a.shape[0] for kind, a, _, _ in ins if kind == "row")
    assert t % tr == 0
    n_in = len(ins)
    n_pieces = sum(len(w) for w, _ in outs)

    def spec(kind, arr, width, base):
        if kind == "row":
            return pl.BlockSpec((tr, width), lambda j, i: (i, base + j))
        return pl.BlockSpec((arr.shape[0], width), lambda j, i: (0, base + j))

    in_specs = [spec(*s) for s in ins]
    out_shapes, out_specs = [], []
    for widths, dt in outs:
        w = sum(widths)
        out_shapes.append(jax.ShapeDtypeStruct((t, w * ncb), dt))
        out_specs.append(pl.BlockSpec((tr, w), lambda j, i: (i, j)))
    for rows, width in accs:
        out_shapes.append(jax.ShapeDtypeStruct((rows, width * ncb), F32))
        out_specs.append(pl.BlockSpec((rows, width), lambda j, i: (0, j)))

    ties = _Order.take()
    nt_ = len(ties)
    in_specs = in_specs + [ANY_SPEC] * nt_

    def body(*refs):
        vals = [r[...] for r in refs[:n_in]]
        res = fn(*vals)
        o_refs = refs[n_in + nt_:n_in + nt_ + len(outs)]
        a_refs = refs[n_in + nt_ + len(outs):]
        p = 0
        for (widths, _), o_ref in zip(outs, o_refs):
            off = 0
            for w in widths:
                if len(widths) == 1:
                    o_ref[...] = res[p].astype(o_ref.dtype)
                else:
                    o_ref[:, off:off + w] = res[p].astype(o_ref.dtype)
                off += w
                p += 1
        i = pl.program_id(1)
        for a_ref, v in zip(a_refs, res[n_pieces:]):
            @pl.when(i == 0)
            def _(a_ref=a_ref, v=v):
                a_ref[...] = v

            @pl.when(i > 0)
            def _(a_ref=a_ref, v=v):
                a_ref[...] += v

    return _Order.done(pl.pallas_call(
        body, name=name, out_shape=out_shapes,
        grid=(ncb, t // tr), in_specs=in_specs, out_specs=out_specs,
        compiler_params=_params(("parallel", "arbitrary")),
    )(*[a for _, a, _, _ in ins], *ties))


def _rms(x, g):
    x = x.astype(F32)
    return x * lax.rsqrt(jnp.mean(x * x, axis=-1, keepdims=True) + EPS) * g


def _silu(x):
    return x * jax.nn.sigmoid(x)


def _colsum(v):
    return jnp.sum(v, axis=0, keepdims=True)


def _pair_norm(x, g):
    w = 2 * HD
    ri = lax.broadcasted_iota(jnp.int32, (w, w), 0)
    ci = lax.broadcasted_iota(jnp.int32, (w, w), 1)
    same_head = ((ri < HD) == (ci < HD)).astype(F32)
    ms = _spread(x * x, same_head) * (1.0 / HD)
    return x * lax.rsqrt(ms + EPS) * g


ATTN_SCALE = 1.0 / math.sqrt(HD)


def _attn_bias(coef):
    key = lax.broadcasted_iota(jnp.int32, (QB, QB), 0)
    qry = lax.broadcasted_iota(jnp.int32, (QB, QB), 1)
    dist = (qry - key).astype(F32)
    own = jnp.where(qry >= key, -coef * dist, NEG)
    prev = jnp.where(qry <= key, -coef * (dist + float(QB)), NEG)
    return own, prev


BNT = (((2,), (2,)), ((0,), (0,)))
BTN = (((1,), (1,)), ((0,), (0,)))


def _attn_pair(qn, kcn, kpn, vc, vp, b_own, b_prev):
    nb = qn.shape[0]
    w = 2 * HD
    lane = lax.broadcasted_iota(jnp.int32, (1, 1, w), 2)
    eye = (lax.broadcasted_iota(jnp.int32, (QB, QB), 0)
           == lax.broadcasted_iota(jnp.int32, (QB, QB), 1)).astype(F32)
    out = jnp.zeros((nb, QB, w), F32)
    lb = jnp.zeros((nb * QB, w), F32)
    for hh in range(2):
        mask = ((lane < HD) if hh == 0 else (lane >= HD)).astype(F32)
        qm = qn * mask
        lc = _dot(kcn, qm, BNT) + b_own[hh]
        lp = _dot(kpn, qm, BNT) + b_prev[hh]
        m = lax.stop_gradient(jnp.maximum(jnp.max(lc, axis=1, keepdims=True),
                                          jnp.max(lp, axis=1, keepdims=True)))
        pc = jnp.exp(lc - m)
        pp = jnp.exp(lp - m)
        l = jnp.sum(pc, axis=1, keepdims=True) + jnp.sum(pp, axis=1, keepdims=True)
        inv = 1.0 / l
        out = out + (_dot(pc * inv, vc, BTN) + _dot(pp * inv, vp, BTN)) * mask
        diag = (eye * (m + jnp.log(l))).reshape(nb * QB, QB)
        lb = lb + _spread(diag, jnp.broadcast_to(mask[0], (QB, w)))
    return out, lb.reshape(nb, QB, w)


NORM_ROWS = 128
NORM_UNROLL = 4
EPILOGUE_ROWS = 512
ATTN_BATCH_FWD = 8
ATTN_BATCH_BWD = 4


def _unit_rows(u, d):
    r = u & (d - 1)
    n = u >> (d.bit_length() - 1)

    def rows(blk):
        start = pl.multiple_of(blk * (QB * d), QB * d)
        return pl.ds(start, QB) if d == 1 else pl.ds(start + r, QB, stride=d)

    return rows(n), rows(jnp.maximum(n - 1, 0)), n == 0


def _unit_batch(i, nbatch, d, bias, qf, kf, vf):
    units = [_unit_rows(i * nbatch + j, d) for j in range(nbatch)]
    cur = lambda ref: jnp.stack([ref[c, :] for c, _, _ in units])
    prv = lambda ref: jnp.stack([ref[p, :] for _, p, _ in units])
    b_own = [b[0] for b in bias]
    b_prev = [jnp.stack([jnp.where(first, NEG, b[1]) for _, _, first in units]) for b in bias]
    return units, (cur(qf), cur(kf), prv(kf), cur(vf), prv(vf), b_own, b_prev)


def _q_norm(x, g):
    return _pair_norm(x, g * ATTN_SCALE)


def _attn_prologue(t, q_ref, k_ref, v_ref, qg_ref, kg_ref, qf, kf, vf):
    def chunk(c, carry):
        rows = pl.ds(pl.multiple_of(c * NORM_ROWS, NORM_ROWS), NORM_ROWS)
        qf[rows, :] = _q_norm(q_ref[rows, :].astype(F32), qg_ref[...])
        kf[rows, :] = _pair_norm(k_ref[rows, :].astype(F32), kg_ref[...])
        vf[rows, :] = v_ref[rows, :].astype(F32)
        return carry
    lax.fori_loop(0, t // NORM_ROWS, chunk, 0, unroll=NORM_UNROLL)


def _attn_specs(t, bases):
    w = 2 * HD
    ins = [pl.BlockSpec((t, w), functools.partial(lambda p, c, b: (0, b + p), b=b)) for b in bases]
    gain = pl.BlockSpec((1, w), lambda p, c: (0, 0))
    blk = pl.BlockSpec((t, w), lambda p, c: (0, p))
    return ins, gain, blk


def _attn_fwd(name, proj, bases, qg, kg, coefs, d):
    t = proj.shape[0]
    npairs = coefs.shape[0] // 2
    w = 2 * HD
    ins, gain, blk = _attn_specs(t, bases)

    def body(coef_ref, q_ref, k_ref, v_ref, qg_ref, kg_ref, o_ref, l_ref, qf, kf, vf):
        p = pl.program_id(0)
        bias = (_attn_bias(coef_ref[2 * p]), _attn_bias(coef_ref[2 * p + 1]))
        _attn_prologue(t, q_ref, k_ref, v_ref, qg_ref, kg_ref, qf, kf, vf)

        def step(i, carry):
            units, ins = _unit_batch(i, ATTN_BATCH_FWD, d, bias, qf, kf, vf)
            o, lb = _attn_pair(*ins)
            for j, (cur, _, _) in enumerate(units):
                o_ref[cur, :] = o[j]
                l_ref[cur, :] = lb[j]
            return carry

        lax.fori_loop(0, t // QB // ATTN_BATCH_FWD, step, 0)

    return pl.pallas_call(
        body, name=name,
        out_shape=[jax.ShapeDtypeStruct((t, npairs * w), F32)] * 2,
        grid_spec=pltpu.PrefetchScalarGridSpec(
            num_scalar_prefetch=1, grid=(npairs,),
            in_specs=ins + [gain, gain], out_specs=[blk, blk],
            scratch_shapes=[pltpu.VMEM((t, w), F32)] * 3),
        compiler_params=_params(("arbitrary",)),
    )(coefs, proj, proj, proj, qg, kg)


def _attn_bwd(name, proj, bases, qg, kg, coefs, d, do, dl):
    t = proj.shape[0]
    npairs = coefs.shape[0] // 2
    w = 2 * HD
    ins, gain, blk = _attn_specs(t, bases)

    def body(coef_ref, q_ref, k_ref, v_ref, qg_ref, kg_ref, do_ref, dl_ref,
             dq_ref, dk_ref, dv_ref, dqg_ref, dkg_ref, qf, kf, vf, dqf, dkf, dvf):
        p = pl.program_id(0)
        bias = (_attn_bias(coef_ref[2 * p]), _attn_bias(coef_ref[2 * p + 1]))
        _attn_prologue(t, q_ref, k_ref, v_ref, qg_ref, kg_ref, qf, kf, vf)
        dkf[...] = jnp.zeros_like(dkf)
        dvf[...] = jnp.zeros_like(dvf)

        def step(i, carry):
            units, ins = _unit_batch(i, ATTN_BATCH_BWD, d, bias, qf, kf, vf)
            f = lambda a, b, c, e, g: _attn_pair(a, b, c, e, g, *ins[5:])
            _, vjp = jax.vjp(f, *ins[:5])
            cot = (jnp.stack([do_ref[cur, :] for cur, _, _ in units]),
                   jnp.stack([dl_ref[cur, :] for cur, _, _ in units]))
            dq, dkc, dkp, dvc, dvp = vjp(cot)
            for j, (cur, prv, _) in enumerate(units):
                dqf[cur, :] = dq[j]
                dkf[cur, :] += dkc[j]
                dkf[prv, :] += dkp[j]
                dvf[cur, :] += dvc[j]
                dvf[prv, :] += dvp[j]
            return carry

        lax.fori_loop(0, t // QB // ATTN_BATCH_BWD, step, 0)

        def chunk(c, carry):
            dqg_acc, dkg_acc = carry
            rows = pl.ds(pl.multiple_of(c * EPILOGUE_ROWS, EPILOGUE_ROWS), EPILOGUE_ROWS)
            _, vq = jax.vjp(_q_norm, q_ref[rows, :].astype(F32), qg_ref[...])
            dq, dqg = vq(dqf[rows, :])
            _, vk = jax.vjp(_pair_norm, k_ref[rows, :].astype(F32), kg_ref[...])
            dk, dkg = vk(dkf[rows, :])
            dq_ref[rows, :] = dq.astype(dq_ref.dtype)
            dk_ref[rows, :] = dk.astype(dk_ref.dtype)
            dv_ref[rows, :] = dvf[rows, :].astype(dv_ref.dtype)
            return dqg_acc + dqg, dkg_acc + dkg

        zero = jnp.zeros((1, w), F32)
        dqg, dkg = lax.fori_loop(0, t // EPILOGUE_ROWS, chunk, (zero, zero))

        @pl.when(p == 0)
        def _():
            dqg_ref[...] = dqg
            dkg_ref[...] = dkg

        @pl.when(p > 0)
        def _():
            dqg_ref[...] += dqg
            dkg_ref[...] += dkg

    big = jax.ShapeDtypeStruct((t, npairs * w), ACT_DTYPE)
    small = jax.ShapeDtypeStruct((1, w), F32)
    return pl.pallas_call(
        body, name=name,
        out_shape=[big, big, big, small, small],
        grid_spec=pltpu.PrefetchScalarGridSpec(
            num_scalar_prefetch=1, grid=(npairs,),
            in_specs=ins + [gain, gain, blk, blk],
            out_specs=[blk, blk, blk, gain, gain],
            scratch_shapes=[pltpu.VMEM((t, w), F32)] * 6),
        compiler_params=_params(("arbitrary",)),
    )(coefs, proj, proj, proj, qg, kg, do, dl)


def _shift_down(u, s):
    if s == 0:
        return u
    rows = lax.broadcasted_iota(jnp.int32, u.shape, 0)
    return jnp.where(rows >= s, pltpu.roll(u, s, 0), 0.0)


def _shift_up(u, s):
    if s == 0:
        return u
    t = u.shape[0]
    rows = lax.broadcasted_iota(jnp.int32, u.shape, 0)
    return jnp.where(rows < t - s, pltpu.roll(u, t - s, 0), 0.0)


def _conv_pre(u, w, b):
    y = b
    for kk in range(SSD_K):
        y = y + w[kk:kk + 1, :] * _shift_down(u, SSD_K - 1 - kk)
    return y


def _conv_fwd(name, src, base, w, b, cw=128):
    t = src.shape[0]
    c = w.shape[1]

    def body(u_ref, w_ref, b_ref, o_ref):
        y = _conv_pre(u_ref[...].astype(F32), w_ref[...], b_ref[...])
        o_ref[...] = _silu(y).astype(o_ref.dtype)

    return pl.pallas_call(
        body, name=name, out_shape=jax.ShapeDtypeStruct((t, c), ACT_DTYPE),
        grid=(c // cw,),
        in_specs=[pl.BlockSpec((t, cw), lambda j: (0, base + j)),
                  pl.BlockSpec((SSD_K, cw), lambda j: (0, j)),
                  pl.BlockSpec((1, cw), lambda j: (0, j))],
        out_specs=pl.BlockSpec((t, cw), lambda j: (0, j)),
        compiler_params=_params(("parallel",)),
    )(src, w, b)


def _conv_bwd(name, src, base, w, b, dout, cw=128):
    t = src.shape[0]
    c = w.shape[1]

    def body(u_ref, w_ref, b_ref, d_ref, du_ref, dw_ref, db_ref):
        u = u_ref[...].astype(F32)
        wv = w_ref[...]
        y = _conv_pre(u, wv, b_ref[...])
        sg = jax.nn.sigmoid(y)
        dy = d_ref[...].astype(F32) * (sg * (1.0 + y * (1.0 - sg)))
        du = jnp.zeros_like(u)
        for kk in range(SSD_K):
            s = SSD_K - 1 - kk
            du = du + wv[kk:kk + 1, :] * _shift_up(dy, s)
            dw_ref[kk:kk + 1, :] = _colsum(dy * _shift_down(u, s))
        du_ref[...] = du.astype(du_ref.dtype)
        db_ref[...] = _colsum(dy)

    return pl.pallas_call(
        body, name=name,
        out_shape=[jax.ShapeDtypeStruct((t, c), ACT_DTYPE),
                   jax.ShapeDtypeStruct((SSD_K, c), F32),
                   jax.ShapeDtypeStruct((1, c), F32)],
        grid=(c // cw,),
        in_specs=[pl.BlockSpec((t, cw), lambda j: (0, base + j)),
                  pl.BlockSpec((SSD_K, cw), lambda j: (0, j)),
                  pl.BlockSpec((1, cw), lambda j: (0, j)),
                  pl.BlockSpec((t, cw), lambda j: (0, j))],
        out_specs=[pl.BlockSpec((t, cw), lambda j: (0, j)),
                   pl.BlockSpec((SSD_K, cw), lambda j: (0, j)),
                   pl.BlockSpec((1, cw), lambda j: (0, j))],
        compiler_params=_params(("parallel",)),
    )(src, w, b, dout)


def _softplus(x):
    return jnp.maximum(x, 0.0) + jnp.log(1.0 + jnp.exp(-jnp.abs(x)))


def _ssd_chunk(xbc, dtraw, bias, alog, states):
    wd = states[0].shape[1]
    nj = wd // SSD_P
    inner = SSD_G * wd
    dt = _softplus(dtraw + bias)
    a = dt * (-jnp.exp(alog))
    li = lax.broadcasted_iota(jnp.int32, (SSD_Q, SSD_Q), 0)
    si = lax.broadcasted_iota(jnp.int32, (SSD_Q, SSD_Q), 1)
    causal = li >= si
    acs = _running_sum(a, causal.astype(F32))
    acs_t = acs.T
    a_last = acs[SSD_Q - 1:SSD_Q, :]
    grow = jnp.exp(acs)
    shrink = jnp.exp(a_last - acs)
    hrow = lax.broadcasted_iota(jnp.int32, (LANE, wd), 0)
    wcol = lax.broadcasted_iota(jnp.int32, (LANE, wd), 1)
    lane = lax.broadcasted_iota(jnp.int32, (1, LANE), 1)
    ys, snext = [], []
    for g in range(SSD_G):
        lo = (hrow - g * nj) * SSD_P
        head_lanes = jnp.logical_and(wcol >= lo, wcol < lo + SSD_P).astype(F32)
        xs = xbc[:, g * wd:(g + 1) * wd]
        bm = xbc[:, inner + g * SSD_N:inner + (g + 1) * SSD_N]
        cm = xbc[:, inner + (SSD_G + g) * SSD_N:inner + (SSD_G + g + 1) * SSD_N]
        xdt = xs * _dot(dt, head_lanes)
        grow_x = _spread(grow, head_lanes)
        y_off = _dot(cm, states[g]) * grow_x
        s_new = (states[g] * grow_x[SSD_Q - 1:SSD_Q, :]
                 + _dot(bm, xdt * _dot(shrink, head_lanes), TN))
        cb = _dot(cm, bm, NT)
        pieces = []
        for i in range(wd // LANE):
            xp = xdt[:, i * LANE:(i + 1) * LANE]
            acc = jnp.zeros((SSD_Q, LANE), F32)
            for hh in range(LANE // SSD_P):
                h = g * nj + i * (LANE // SSD_P) + hh
                decay = jnp.exp(jnp.where(causal, acs[:, h:h + 1] - acs_t[h:h + 1, :], NEG))
                keep = jnp.logical_and(lane >= hh * SSD_P, lane < (hh + 1) * SSD_P).astype(F32)
                acc = acc + _dot(cb * decay, xp * keep)
            pieces.append(acc)
        y_diag = pieces[0] if len(pieces) == 1 else jnp.concatenate(pieces, axis=1)
        ys.append(y_diag + y_off)
        snext.append(s_new)
    return ys, snext


def _ssd_specs(cdim, wd, rev, nc):
    ch = (lambda c: nc - 1 - c) if rev else (lambda c: c)
    full = lambda width: pl.BlockSpec((SSD_Q, width), lambda c: (ch(c), 0))
    vec = pl.BlockSpec((1, LANE), lambda c: (0, 0))
    st = pl.BlockSpec((1, SSD_G, SSD_N, wd), lambda c: (ch(c), 0, 0, 0))
    return full, vec, st


def _ssd_fwd(name, xbc, dtraw, bias, alog, inner):
    t, cdim = xbc.shape
    wd = inner // SSD_G
    nc = t // SSD_Q
    full, vec, st = _ssd_specs(cdim, wd, False, nc)

    def body(x_ref, r_ref, b_ref, a_ref, y_ref, st_ref, s_scr):
        @pl.when(pl.program_id(0) == 0)
        def _():
            s_scr[...] = jnp.zeros_like(s_scr)

        sprev = [s_scr[g] for g in range(SSD_G)]
        ys, snext = _ssd_chunk(x_ref[...].astype(F32), r_ref[...], b_ref[...], a_ref[...], sprev)
        for g in range(SSD_G):
            st_ref[0, g] = sprev[g]
            y_ref[:, g * wd:(g + 1) * wd] = ys[g]
            s_scr[g] = snext[g]

    return pl.pallas_call(
        body, name=name,
        out_shape=[jax.ShapeDtypeStruct((t, inner), F32),
                   jax.ShapeDtypeStruct((nc, SSD_G, SSD_N, wd), F32)],
        grid=(nc,),
        in_specs=[full(cdim), full(LANE), vec, vec],
        out_specs=[full(inner), st],
        scratch_shapes=[pltpu.VMEM((SSD_G, SSD_N, wd), F32)],
        compiler_params=_params(("arbitrary",)),
    )(xbc, dtraw, bias, alog)


def _ssd_bwd(name, xbc, dtraw, bias, alog, states, dy, dxs_extra):
    t, cdim = xbc.shape
    inner = dy.shape[1]
    wd = inner // SSD_G
    nc = t // SSD_Q
    full, vec, st = _ssd_specs(cdim, wd, True, nc)

    def body(x_ref, r_ref, b_ref, a_ref, st_ref, dy_ref, dx0_ref,
             dx_ref, dr_ref, db_ref, da_ref, ds_scr):
        first = pl.program_id(0) == 0

        @pl.when(first)
        def _():
            ds_scr[...] = jnp.zeros_like(ds_scr)

        sprev = [st_ref[0, g] for g in range(SSD_G)]
        _, vjp = jax.vjp(_ssd_chunk, x_ref[...].astype(F32), r_ref[...], b_ref[...], a_ref[...],
                         sprev)
        dyv = dy_ref[...]
        dys = [dyv[:, g * wd:(g + 1) * wd] for g in range(SSD_G)]
        dsn = [ds_scr[g] for g in range(SSD_G)]
        dx, dr, db, da, dsp = vjp((dys, dsn))
        dx_ref[:, :inner] = dx[:, :inner] + dx0_ref[...].astype(F32)
        dx_ref[:, inner:] = dx[:, inner:]
        dr_ref[...] = dr
        for g in range(SSD_G):
            ds_scr[g] = dsp[g]

        @pl.when(first)
        def _():
            db_ref[...] = db
            da_ref[...] = da

        @pl.when(jnp.logical_not(first))
        def _():
            db_ref[...] += db
            da_ref[...] += da

    return pl.pallas_call(
        body, name=name,
        out_shape=[jax.ShapeDtypeStruct((t, cdim), F32),
                   jax.ShapeDtypeStruct((t, LANE), F32),
                   jax.ShapeDtypeStruct((1, LANE), F32),
                   jax.ShapeDtypeStruct((1, LANE), F32)],
        grid=(nc,),
        in_specs=[full(cdim), full(LANE), vec, vec, st, full(inner), full(inner)],
        out_specs=[full(cdim), full(LANE), vec, vec],
        scratch_shapes=[pltpu.VMEM((SSD_G, SSD_N, wd), F32)],
        compiler_params=_params(("arbitrary",)),
    )(xbc, dtraw, bias, alog, states, dy, dxs_extra)


def _mix(o0, o1, o2, l0, l1, l2):
    m = lax.stop_gradient(jnp.maximum(jnp.maximum(l0, l1), l2))
    e0, e1, e2 = jnp.exp(l0 - m), jnp.exp(l1 - m), jnp.exp(l2 - m)
    return (e0 * o0 + e1 * o1 + e2 * o2) / (e0 + e1 + e2)


def _gate(y, xs, z, dexp, gain):
    v = (y + xs.astype(F32) * dexp) * _silu(z.astype(F32))
    return _rms(v, gain)


def _merge(ga, gs, ap, sp):
    return jax.nn.sigmoid(ga.astype(F32)) * ap + jax.nn.sigmoid(gs.astype(F32)) * sp


def _alibi_coefs(hp):
    n = hp * len(PATTERNS)
    slopes = np.exp2(-ALIBI_MAX_EXP * np.arange(1, n + 1, dtype=np.float32) / n).astype(np.float32)
    return [jnp.asarray(slopes[g * hp:(g + 1) * hp] * np.float32(d))
            for g, (_, d) in enumerate(PATTERNS)]


def _local_step(x, tgt, w, p, gw_=None, aw=None):
    t, d = x.shape
    dff = w["gu1t"].shape[0] // 2
    aw = w["abt"].shape[1] if aw is None else aw
    hp = aw // HD
    qkv = len(PATTERNS) * aw
    inner = p["ssd_norm"].shape[1]
    nh = p["dt_bias"].shape[1]
    gw_ = {} if gw_ is None else gw_
    gw = inner // SSD_G
    cdim = inner + 2 * SSD_G * SSD_N
    z_off, xbc_off = 3 * qkv, 3 * qkv + inner
    ga_off = xbc_off + cdim
    gs_off = ga_off + d
    hw = d // 2
    assert z_off % gw == 0 and xbc_off % LANE == 0 and ga_off % hw == 0 and gs_off % hw == 0
    assert (nh // SSD_G) * SSD_P == gw and hp % 2 == 0 and aw % LANE == 0 and nh <= LANE
    gdt = MXU_DTYPE

    row = lambda a, width, base=0: ("row", a, width, base)
    const = lambda a, width, base=0: ("const", a, width, base)

    def rms_fwd(name, xin, g):
        return _rw(name, lambda xv, gv: (_rms(xv, gv),), [row(xin, d), const(g, d)],
                   [((d,), ACT_DTYPE)])[0]

    def rms_bwd(name, xin, g, dh, dres):
        def fn(xv, gv, dhv, drv):
            _, vjp = jax.vjp(_rms, xv, gv)
            dx, dg = vjp(dhv.astype(F32))
            return drv + dx, dg
        return _rw(name, fn, [row(xin, d), const(g, d), row(dh, d), row(dres, d)],
                   [((d,), F32)], accs=[(1, d)])

    def ffn_fwd(tag, xin, g, key_gu, key_d):
        h = rms_fwd(tag + "_norm", xin, g)
        gate, up, a = _ffn_up(tag + "_up", h, w[key_gu])
        xo = _mm(tag + "_down", a, w[key_d], "nn", F32, res=xin, scale=0.5)
        return xo, (h, gate, up, a)

    def ffn_bwd(tag, xin, g, wgut, wd, saved, dxo, key_gu, key_d):
        h, gate, up, a = saved
        gw_[key_d] = _mm(tag + "_dwd", a, dxo, "tn", gdt, scale=0.5)
        dgu = _ffn_dact(tag + "_dact", dxo, wd, gate, up, 0.5)
        gw_[key_gu] = _mm(tag + "_dwgu", dgu, h, "tn", gdt)
        dh = _mm(tag + "_dh", dgu, wgut, "nn", F32)
        return rms_bwd(tag + "_dnorm", xin, g, dh, dxo)

    x1, ffn1_saved = ffn_fwd("ffn1", x, p["ffn1_norm"], "gu1t", "d1")
    if hasattr(w, "after_first_ffn"):
        w.after_first_ffn()
    h2 = rms_fwd("mix_norm", x1, p["mix_norm"])
    proj = _mm("in_proj", h2, w["maint"], "nt", ACT_DTYPE, cap_m=512, cap_n=2944)
    dtraw = _mm("dt_proj", h2, w["dtt"], "nt", F32)

    coefs = _alibi_coefs(hp)
    qg2 = jnp.concatenate([p["q_norm"], p["q_norm"]], axis=1)
    kg2 = jnp.concatenate([p["k_norm"], p["k_norm"]], axis=1)
    pw = 2 * HD
    attn_bases = [[(off + gi * aw) // pw for off in (0, qkv, 2 * qkv)]
                  for gi in range(len(PATTERNS))]
    attn_o, attn_l = [], []
    for gi, (_, dil) in enumerate(PATTERNS):
        o, l = _attn_fwd(f"attn_fwd{gi}", proj, attn_bases[gi], qg2, kg2, coefs[gi], dil)
        attn_o.append(o)
        attn_l.append(l)
    ao = _rw("attn_mix", lambda *v: (_mix(*v),), [row(a, aw) for a in attn_o + attn_l],
             [((aw,), ACT_DTYPE)])[0]

    xbc = _conv_fwd("conv_fwd", proj, xbc_off // LANE, p["conv_w"], p["conv_b"])
    pad = lambda v: jnp.pad(v, ((0, 0), (0, LANE - nh)))
    bias_p, alog_p = pad(p["dt_bias"]), pad(p["a_log"])
    yssd, states = _ssd_fwd("ssd_fwd", xbc, dtraw, bias_p, alog_p, inner)
    dexp = jnp.repeat(p["d_skip"], SSD_P, axis=1)
    gate_ins = [row(yssd, gw), row(xbc, gw), row(proj, gw, z_off // gw),
                const(dexp, gw), const(p["ssd_norm"], gw)]
    yn = _rw("ssd_gate", lambda *v: (_gate(*v),), gate_ins, [((gw,), ACT_DTYPE)], ncb=SSD_G)[0]

    ap = _mm("attn_out", ao, w["abt"], "nt", F32)
    sp = _mm("ssd_out", yn, w["sb"], "nn", F32)
    merge_ins = [row(proj, hw, ga_off // hw), row(proj, hw, gs_off // hw), row(ap, hw), row(sp, hw)]
    mg = _rw("merge", lambda *v: (_merge(*v),), merge_ins, [((hw,), ACT_DTYPE)], ncb=2)[0]
    x2 = _mm("mix_out", mg, w["out"], "nn", F32, res=x1)
    x3, ffn2_saved = ffn_fwd("ffn2", x2, p["ffn2_norm"], "gu2t", "d2")

    def loss_fn(yv, tv):
        e = yv - tv
        return e * (1.0 / d), _colsum(e * e)
    dy, loss_vec = _rw("loss", loss_fn, [row(x3, d), row(tgt, d)], [((d,), F32)], accs=[(1, d)])

    gp = {}
    dx2, gp["ffn2_norm"] = ffn_bwd(
        "ffn2", x2, p["ffn2_norm"], w["gu2t"], w["d2"], ffn2_saved, dy, "gu2t", "d2")
    dmg = _mm("d_merge", dx2, w["out"], "nt", ACT_DTYPE)
    gw_["out"] = _mm("dw_out", mg, dx2, "tn", gdt)

    def merge_bwd(gav, gsv, apv, spv, dv):
        _, vjp = jax.vjp(_merge, gav, gsv, apv, spv)
        return vjp(dv.astype(F32))
    dga, dgs, dap, dsp = _rw("d_merge_gate", merge_bwd, merge_ins + [row(dmg, hw)],
                             [((hw,), ACT_DTYPE)] * 4, ncb=2)
    gw_["abt"] = _mm("dw_ab", dap, ao, "tn", gdt)
    dao = _mm("d_attn_o", dap, w["abt"], "nn", F32)
    gw_["sb"] = _mm("dw_sb", yn, dsp, "tn", gdt)
    dyn = _mm("d_ssd_y", dsp, w["sb"], "nt", F32)

    def gate_bwd(yv, xv, zv, dev, gv, dv):
        _, vjp = jax.vjp(_gate, yv, xv, zv, dev, gv)
        return vjp(dv)
    dyssd, dxs_gate, dz, ddexp, gp["ssd_norm"] = _rw(
        "d_ssd_gate", gate_bwd, gate_ins + [row(dyn, gw)],
        [((gw,), F32), ((gw,), F32), ((gw,), ACT_DTYPE)], accs=[(1, gw), (1, gw)], ncb=SSD_G)
    gp["d_skip"] = ddexp.reshape(nh, SSD_P).sum(axis=1).reshape(1, nh)

    dxbc, ddtraw, dbias, dalog = _ssd_bwd("ssd_bwd", xbc, dtraw, bias_p, alog_p, states,
                                          dyssd, dxs_gate)
    gp["dt_bias"], gp["a_log"] = dbias[:, :nh], dalog[:, :nh]
    du, gp["conv_w"], gp["conv_b"] = _conv_bwd("conv_bwd", proj, xbc_off // LANE,
                                               p["conv_w"], p["conv_b"], dxbc)

    def mix_bwd(*v):
        _, vjp = jax.vjp(_mix, *v[:6])
        return vjp(v[6])
    dmix = _rw("d_attn_mix", mix_bwd, [row(a, aw) for a in attn_o + attn_l] + [row(dao, aw)],
               [((aw,), F32)] * 6)
    dq, dk, dv = [], [], []
    dqg = dkg = None
    for gi, (_, dil) in enumerate(PATTERNS):
        r = _attn_bwd(f"attn_bwd{gi}", proj, attn_bases[gi], qg2, kg2, coefs[gi], dil,
                      dmix[gi], dmix[3 + gi])
        dq.append(r[0])
        dk.append(r[1])
        dv.append(r[2])
        dqg = r[3] if dqg is None else dqg + r[3]
        dkg = r[4] if dkg is None else dkg + r[4]
    gp["q_norm"] = dqg[:, :HD] + dqg[:, HD:]
    gp["k_norm"] = dkg[:, :HD] + dkg[:, HD:]

    segs = dq + dk + dv + [dz, du, dga, dgs]
    gw_["maint"] = _mm("dw_in", segs, h2, "tn", gdt)
    gw_["dtt"] = _mm("dw_dt", ddtraw, h2, "tn", gdt)
    dh2 = _mm("d_h2_main", segs, w["maint"], "nn", F32)
    dh2 = _mm("d_h2_dt", ddtraw, w["dtt"], "nn", F32, res=dh2)
    dx1, gp["mix_norm"] = rms_bwd("d_mix_norm", x1, p["mix_norm"], dh2, dx2)
    dx0, gp["ffn1_norm"] = ffn_bwd(
        "ffn1", x, p["ffn1_norm"], w["gu1t"], w["d1"], ffn1_saved, dx1, "gu1t", "d1")
    return loss_vec, dx0, gw_, gp


MESH = pl.DeviceIdType.MESH
HBM_SPEC = pl.BlockSpec(memory_space=pltpu.HBM)


def _mesh_pos():
    return lax.axis_index("x"), lax.axis_index("y"), lax.axis_index("c")


def _flip(pos, k):
    x, y, c = pos
    return (1 - x if k & 4 else x, 1 - y if k & 2 else y, 1 - c if k & 1 else c)


def _dev_index(pos):
    return 4 * pos[0] + 2 * pos[1] + pos[2]


def _rows_of(ref, base, stride, rows, pos):
    start = pl.multiple_of(base + stride * _dev_index(pos), ROW_ALIGN)
    return ref.at[pl.ds(start, rows)]


def _gather(name, shards, dests, out_shapes):
    n = len(shards)
    n_out = len(out_shapes)

    def body(*refs):
        x_refs = refs[:n]
        o_refs = refs[n:n + n_out]
        send_sems, recv_sems, local_sems = refs[n + n_out:]
        me = _mesh_pos()
        sibling = _flip(me, 1)
        chips = [_flip(me, 4), _flip(me, 2), _flip(me, 6)]

        def slot(i, block):
            k_out, base, stride = dests[i]
            return _rows_of(o_refs[k_out], base, stride, shards[i].shape[0], block)

        def copy(i, k, block, to, src=None):
            dst = slot(i, block)
            return pltpu.make_async_remote_copy(
                src_ref=dst if src is None else src, dst_ref=dst,
                send_sem=send_sems.at[7 * i + k], recv_sem=recv_sems.at[7 * i + k],
                device_id=to, device_id_type=MESH)

        mine = [pltpu.make_async_copy(x_refs[i], slot(i, me), local_sems.at[i]) for i in range(n)]
        for cp in mine:
            cp.start()
        first = []
        for i in range(n):
            first.append(copy(i, 0, me, sibling, src=x_refs[i]))
            first += [copy(i, 1 + j, me, chip, src=x_refs[i]) for j, chip in enumerate(chips)]
        for cp in first:
            cp.start()
        passed = []
        for j, chip in enumerate(chips):
            for i in range(n):
                copy(i, 1 + j, chip, me).wait_recv()
                fwd = copy(i, 4 + j, chip, sibling)
                fwd.start()
                passed.append(fwd)
        for i in range(n):
            copy(i, 0, sibling, me).wait_recv()
            for j, chip in enumerate(chips):
                copy(i, 4 + j, _flip(chip, 1), me).wait_recv()
        for cp in first + passed:
            cp.wait_send()
        for cp in mine:
            cp.wait()

    return pl.pallas_call(
        body, name=name,
        out_shape=[jax.ShapeDtypeStruct(s, dt) for s, dt in out_shapes],
        in_specs=[HBM_SPEC] * n, out_specs=[HBM_SPEC] * n_out,
        scratch_shapes=[pltpu.SemaphoreType.DMA((7 * n,)), pltpu.SemaphoreType.DMA((7 * n,)),
                        pltpu.SemaphoreType.DMA((n,))],
    )(*shards)


def _exchange(name, grads, srcs, small):
    n = len(srcs)
    ng = len(grads)

    def body(*refs):
        g_refs = refs[:ng]
        m_ref = refs[ng]
        r_refs = refs[ng + 1:ng + 1 + n]
        s_ref = refs[ng + 1 + n]
        send_sems, recv_sems, local_sems = refs[ng + 2 + n:]
        me = _mesh_pos()
        my = _dev_index(me)

        def slab(i, pos):
            gi, base, stride, rows = srcs[i]
            return _rows_of(g_refs[gi], base, stride, rows, pos)

        own = [pltpu.make_async_copy(slab(i, me), r_refs[i].at[my], local_sems.at[i])
               for i in range(n)]
        own.append(pltpu.make_async_copy(m_ref, s_ref.at[my], local_sems.at[n]))
        for cp in own:
            cp.start()

        def copies(k, src_pos, slot_pos):
            peer = _flip(me, k)
            si = _dev_index(slot_pos)
            out = [pltpu.make_async_remote_copy(
                src_ref=slab(i, src_pos), dst_ref=r_refs[i].at[si],
                send_sem=send_sems.at[7 * i + k - 1], recv_sem=recv_sems.at[7 * i + k - 1],
                device_id=peer, device_id_type=MESH) for i in range(n)]
            out.append(pltpu.make_async_remote_copy(
                src_ref=m_ref, dst_ref=s_ref.at[si],
                send_sem=send_sems.at[7 * n + k - 1], recv_sem=recv_sems.at[7 * n + k - 1],
                device_id=peer, device_id_type=MESH))
            return out

        sent = [cp for k in range(1, NDEV) for cp in copies(k, _flip(me, k), me)]
        for cp in sent:
            cp.start()
        for k in range(1, NDEV):
            for cp in copies(k, me, _flip(me, k)):
                cp.wait_recv()
        for cp in sent:
            cp.wait_send()
        for cp in own:
            cp.wait()

    out_shape = [jax.ShapeDtypeStruct((NDEV, rows, grads[gi].shape[1]), grads[gi].dtype)
                 for gi, _, _, rows in srcs]
    out_shape.append(jax.ShapeDtypeStruct((NDEV,) + small.shape, small.dtype))
    return pl.pallas_call(
        body, name=name, out_shape=out_shape,
        in_specs=[HBM_SPEC] * (ng + 1), out_specs=[HBM_SPEC] * (n + 1),
        scratch_shapes=[pltpu.SemaphoreType.DMA((7 * (n + 1),)),
                        pltpu.SemaphoreType.DMA((7 * (n + 1),)),
                        pltpu.SemaphoreType.DMA((n + 1,))],
    )(*grads, small)


SEM_SPEC = pl.BlockSpec(memory_space=pltpu.SEMAPHORE)
SIDE_EFFECT = pltpu.SideEffectType.DATAFLOW_SIDE_EFFECTING


def _split_refs(plan, srcs, lands, i, src_for, land_from):
    si, sbase, sstride, li, lbase, lstride, rows = plan[i]
    return (_rows_of(srcs[si], sbase, sstride, rows, src_for),
            _rows_of(lands[li], lbase, lstride, rows, land_from))


ALL_PEERS = tuple(range(1, NDEV))
SAME_CORE_AND_SIBLING = (1, 4, 2, 6)
OTHER_CHIPS = (4, 2, 6)


def _split_start(name, srcs, lands, plan, after=(), relations=ALL_PEERS):
    ns, nl, n = len(srcs), len(lands), len(plan)

    def body(*refs):
        s_refs = refs[:ns]
        l_refs = refs[ns:ns + nl]
        send_sems, recv_sems = refs[ns + nl + len(after):ns + nl + len(after) + 2]
        local_sems = refs[ns + nl + len(after) + 2]
        token = refs[ns + nl + len(after) + 3 + ns + nl]
        me = _mesh_pos()
        for i in range(n):
            src, dst = _split_refs(plan, s_refs, l_refs, i, me, me)
            pltpu.make_async_copy(src, dst, local_sems.at[i]).start()
        for k in relations:
            peer = _flip(me, k)
            for i in range(n):
                src, dst = _split_refs(plan, s_refs, l_refs, i, peer, me)
                pltpu.make_async_remote_copy(
                    src_ref=src, dst_ref=dst,
                    send_sem=send_sems.at[7 * i + k - 1], recv_sem=recv_sems.at[7 * i + k - 1],
                    device_id=peer, device_id_type=MESH).start()
        token[...] = jnp.zeros_like(token)

    hbm = lambda a: pltpu.HBM(a.shape, a.dtype)
    out_shape = ((pltpu.SemaphoreType.DMA((7 * n,)), pltpu.SemaphoreType.DMA((7 * n,)),
                  pltpu.SemaphoreType.DMA((n,)))
                 + tuple(hbm(a) for a in srcs) + tuple(hbm(a) for a in lands)
                 + (jax.ShapeDtypeStruct((8, LANE), F32),))
    out = pl.pallas_call(
        body, name=name, out_shape=out_shape,
        in_specs=[HBM_SPEC] * (ns + nl) + [ANY_SPEC] * len(after),
        out_specs=(SEM_SPEC, SEM_SPEC, SEM_SPEC) + (HBM_SPEC,) * (ns + nl)
        + (pl.BlockSpec(memory_space=pltpu.VMEM),),
        input_output_aliases={i: 3 + i for i in range(ns + nl)},
        compiler_params=pltpu.CompilerParams(has_side_effects=SIDE_EFFECT),
    )(*[pltpu.with_memory_space_constraint(a, pltpu.HBM) for a in tuple(srcs) + tuple(lands)],
      *after)
    _Order.tokens.append(out[-1])
    return out[0], out[1], out[2], out[3:3 + ns], out[3 + ns:3 + ns + nl]


def _split_wait(name, started, plan, relations=ALL_PEERS):
    send_sems, recv_sems, local_sems, srcs, lands = started
    ns, nl, n = len(srcs), len(lands), len(plan)
    after = [_Order.last] if _Order.last is not None else []

    def body(*refs):
        s_refs = refs[:ns]
        l_refs = refs[ns:ns + nl]
        send_sems, recv_sems, local_sems = refs[ns + nl:ns + nl + 3]
        me = _mesh_pos()
        for i in range(n):
            src, dst = _split_refs(plan, s_refs, l_refs, i, me, me)
            pltpu.make_async_copy(src, dst, local_sems.at[i]).wait()
        for k in relations:
            peer = _flip(me, k)
            for i in range(n):
                src, dst = _split_refs(plan, s_refs, l_refs, i, peer, peer)
                cp = pltpu.make_async_remote_copy(
                    src_ref=src, dst_ref=dst,
                    send_sem=send_sems.at[7 * i + k - 1], recv_sem=recv_sems.at[7 * i + k - 1],
                    device_id=peer, device_id_type=MESH)
                cp.wait_send()
                cp.wait_recv()

    hbm = lambda a: pltpu.HBM(a.shape, a.dtype)
    out = pl.pallas_call(
        body, name=name,
        out_shape=tuple(hbm(a) for a in srcs) + tuple(hbm(a) for a in lands),
        in_specs=[HBM_SPEC] * (ns + nl) + [SEM_SPEC] * 3 + [ANY_SPEC] * len(after),
        out_specs=(HBM_SPEC,) * (ns + nl),
        input_output_aliases={i: i for i in range(ns + nl)},
        compiler_params=pltpu.CompilerParams(has_side_effects=SIDE_EFFECT),
    )(*srcs, *lands, send_sems, recv_sems, local_sems, *after)
    return list(out[ns:])


def _forward_refs(plan, lands, i, block):
    _, _, _, li, lbase, lstride, rows = plan[i]
    return _rows_of(lands[li], lbase, lstride, rows, block)


def _forward_start(name, lands, plan):
    nl, n = len(lands), len(plan)

    def body(*refs):
        l_refs = refs[:nl]
        send_sems, recv_sems = refs[nl:nl + 2]
        token = refs[nl + 2 + nl]
        me = _mesh_pos()
        for j, kc in enumerate(OTHER_CHIPS):
            for i in range(n):
                rows = _forward_refs(plan, l_refs, i, _flip(me, kc))
                pltpu.make_async_remote_copy(
                    src_ref=rows, dst_ref=rows,
                    send_sem=send_sems.at[3 * i + j], recv_sem=recv_sems.at[3 * i + j],
                    device_id=_flip(me, 1), device_id_type=MESH).start()
        token[...] = jnp.zeros_like(token)

    hbm = lambda a: pltpu.HBM(a.shape, a.dtype)
    out = pl.pallas_call(
        body, name=name,
        out_shape=((pltpu.SemaphoreType.DMA((3 * n,)), pltpu.SemaphoreType.DMA((3 * n,)))
                   + tuple(hbm(a) for a in lands) + (jax.ShapeDtypeStruct((8, LANE), F32),)),
        in_specs=[HBM_SPEC] * nl,
        out_specs=(SEM_SPEC, SEM_SPEC) + (HBM_SPEC,) * nl
        + (pl.BlockSpec(memory_space=pltpu.VMEM),),
        input_output_aliases={i: 2 + i for i in range(nl)},
        compiler_params=pltpu.CompilerParams(has_side_effects=SIDE_EFFECT),
    )(*[pltpu.with_memory_space_constraint(a, pltpu.HBM) for a in lands])
    _Order.tokens.append(out[-1])
    return out[0], out[1], out[2:2 + nl]


def _forward_wait(name, started, plan):
    send_sems, recv_sems, lands = started
    nl, n = len(lands), len(plan)
    after = [_Order.last] if _Order.last is not None else []

    def body(*refs):
        l_refs = refs[:nl]
        send_sems, recv_sems = refs[nl:nl + 2]
        me = _mesh_pos()
        for j, kc in enumerate(OTHER_CHIPS):
            for i in range(n):
                sent = _forward_refs(plan, l_refs, i, _flip(me, kc))
                came = _forward_refs(plan, l_refs, i, _flip(_flip(me, 1), kc))
                cp = pltpu.make_async_remote_copy(
                    src_ref=sent, dst_ref=came,
                    send_sem=send_sems.at[3 * i + j], recv_sem=recv_sems.at[3 * i + j],
                    device_id=_flip(me, 1), device_id_type=MESH)
                cp.wait_send()
                cp.wait_recv()

    hbm = lambda a: pltpu.HBM(a.shape, a.dtype)
    out = pl.pallas_call(
        body, name=name, out_shape=tuple(hbm(a) for a in lands),
        in_specs=[HBM_SPEC] * nl + [SEM_SPEC] * 2 + [ANY_SPEC] * len(after),
        out_specs=(HBM_SPEC,) * nl,
        input_output_aliases={i: i for i in range(nl)},
        compiler_params=pltpu.CompilerParams(has_side_effects=SIDE_EFFECT),
    )(*lands, send_sems, recv_sems, *after)
    return list(out)


def _regroup_rows(name, padded, r, rp, lo, hi):
    d = padded.shape[1]
    pack = 4 // padded.dtype.itemsize
    assert r % pack == 0 and rp % ROW_ALIGN == 0 and lo % (8 * pack) == 0 and hi % (8 * pack) == 0
    r2, rp2, lo2, hi2 = r // pack, rp // pack, lo // pack, hi // pack
    u32 = jnp.uint32

    def body(x_ref, main_ref, cut_ref):
        x = pltpu.bitcast(x_ref[...], u32)
        joined = jnp.concatenate([x[rp2 * j:rp2 * j + r2] for j in range(NDEV)], axis=0)
        main = jnp.concatenate([joined[:lo2], joined[hi2:]], axis=0)
        cut = jnp.concatenate([joined[lo2:hi2], jnp.zeros((LANE // pack - (hi2 - lo2), LANE), u32)],
                              axis=0)
        main_ref[...] = pltpu.bitcast(main, padded.dtype)
        cut_ref[...] = pltpu.bitcast(cut, padded.dtype)

    return pl.pallas_call(
        body, name=name,
        out_shape=[jax.ShapeDtypeStruct((NDEV * r - (hi - lo), d), padded.dtype),
                   jax.ShapeDtypeStruct((LANE, d), padded.dtype)],
        grid=(d // LANE,),
        in_specs=[pl.BlockSpec((NDEV * rp, LANE), lambda i: (0, i))],
        out_specs=[pl.BlockSpec((NDEV * r - (hi - lo), LANE), lambda i: (0, i)),
                   pl.BlockSpec((LANE, LANE), lambda i: (0, i))],
        compiler_params=_params(("parallel",)),
    )(padded)


def _ungroup_rows(name, main, cut, r, rp, lo, hi):
    d = main.shape[1]
    pack = 4 // main.dtype.itemsize
    r2, rp2, lo2, hi2 = r // pack, rp // pack, lo // pack, hi // pack
    u32 = jnp.uint32

    def body(main_ref, cut_ref, o_ref):
        m = pltpu.bitcast(main_ref[...], u32)
        c = pltpu.bitcast(cut_ref[...], u32)
        joined = jnp.concatenate([m[:lo2], c[:hi2 - lo2], m[lo2:]], axis=0)
        zeros = jnp.zeros((rp2 - r2, LANE), u32)
        parts = []
        for j in range(NDEV):
            parts += [joined[r2 * j:r2 * (j + 1)], zeros]
        o_ref[...] = pltpu.bitcast(jnp.concatenate(parts, axis=0), main.dtype)

    return pl.pallas_call(
        body, name=name,
        out_shape=jax.ShapeDtypeStruct((NDEV * rp, d), main.dtype),
        grid=(d // LANE,),
        in_specs=[pl.BlockSpec((main.shape[0], LANE), lambda i: (0, i)),
                  pl.BlockSpec((LANE, LANE), lambda i: (0, i))],
        out_specs=pl.BlockSpec((NDEV * rp, LANE), lambda i: (0, i)),
        compiler_params=_params(("parallel",)),
    )(main, cut)


def _sum_slabs(name, a):
    s, r, c = a.shape

    def body(a_ref, o_ref):
        acc = a_ref[0].astype(F32)
        for i in range(1, s):
            acc = acc + a_ref[i].astype(F32)
        o_ref[...] = acc

    return pl.pallas_call(body, name=name, out_shape=jax.ShapeDtypeStruct((r, c), F32))(a)


def _adamw_update(g, w, m, v):
    mn = ADAM_B1 * m + (1.0 - ADAM_B1) * g
    vn = ADAM_B2 * v + (1.0 - ADAM_B2) * (g * g)
    m_hat = mn / (1.0 - ADAM_B1 ** ADAM_STEP)
    v_hat = vn / (1.0 - ADAM_B2 ** ADAM_STEP)
    delta = -ADAM_LR * (m_hat / (jnp.sqrt(v_hat) + ADAM_EPS) + ADAM_WD * w)
    return delta, mn, vn


def _adamw(name, gsrc, w, m, v, transposed=False, tr=256):
    s = gsrc.shape[0]
    lead = w.ndim == 3
    r, c = w.shape[-2:]
    step = LANE if transposed else 8
    tr = max(t for t in range(step, min(tr, r) + 1, step) if r % t == 0)

    def body(g_ref, w_ref, m_ref, v_ref, go_ref, d_ref, mo_ref, vo_ref):
        g = g_ref[0].astype(F32)
        for i in range(1, s):
            g = g + g_ref[i].astype(F32)
        if transposed:
            g = g.T[:, :c]
        delta, mn, vn = _adamw_update(g, w_ref[...], m_ref[...], v_ref[...])
        go_ref[...] = g
        d_ref[...] = delta
        mo_ref[...] = mn
        vo_ref[...] = vn

    if lead:
        blk = pl.BlockSpec((None, tr, c), lambda i: (0, i, 0))
    else:
        blk = pl.BlockSpec((tr, c), lambda i: (i, 0))
    if transposed:
        g_spec = pl.BlockSpec((s, gsrc.shape[1], tr), lambda i: (0, 0, i))
    else:
        g_spec = pl.BlockSpec((s, tr, c), lambda i: (0, i, 0))
    return pl.pallas_call(
        body, name=name, out_shape=[jax.ShapeDtypeStruct(w.shape, F32)] * 4,
        grid=(r // tr,),
        in_specs=[g_spec, blk, blk, blk], out_specs=[blk] * 4,
        compiler_params=_params(("parallel",)),
    )(gsrc, w, m, v)


REPLICATED = ("ffn1_norm", "mix_norm", "q_norm", "k_norm", "conv_b", "dt_bias", "a_log",
              "d_skip", "ssd_norm", "ffn2_norm")
ALL_WEIGHTS = ("ffn1_norm", "ffn1_w_gate", "ffn1_w_up", "ffn1_w_down", "mix_norm", "w_in",
               "q_norm", "k_norm", "conv_w", "conv_b", "dt_bias", "a_log", "d_skip", "ssd_norm",
               "w_attn_branch", "w_ssd_branch", "w_out", "ffn2_norm", "ffn2_w_gate", "ffn2_w_up",
               "ffn2_w_down")
BIG = (("ffn1_w_gate", True, "gu1t", 0), ("ffn1_w_up", True, "gu1t", 1),
       ("ffn1_w_down", False, "d1", 0), ("w_in", True, "wint", 0),
       ("w_attn_branch", True, "abt", 0), ("w_ssd_branch", False, "sb", 0),
       ("w_out", False, "out", 0),
       ("ffn2_w_gate", True, "gu2t", 0), ("ffn2_w_up", True, "gu2t", 1),
       ("ffn2_w_down", False, "d2", 0))


def _nrows(shape, cols):
    return -(-math.prod(shape) // cols)


def _pack_rows(arrs, cols, row_tile):
    parts = []
    for a in arrs:
        flat = a.reshape(-1)
        nr = -(-flat.shape[0] // cols)
        parts.append(jnp.pad(flat, (0, nr * cols - flat.shape[0])).reshape(nr, cols))
    out = jnp.concatenate(parts, axis=0)
    return jnp.pad(out, ((0, _round_up(out.shape[0], row_tile) - out.shape[0]), (0, 0)))


def _unpack_rows(packed, shapes):
    cols = packed.shape[-1]
    out, r0 = [], 0
    for sh in shapes:
        nr = _nrows(sh, cols)
        out.append(packed[r0:r0 + nr].reshape(-1)[:math.prod(sh)].reshape(tuple(sh)))
        r0 += nr
    return out


def kernel(x, ffn1_norm, ffn1_w_gate, ffn1_w_up, ffn1_w_down, mix_norm, w_in, q_norm, k_norm, conv_w, conv_b, dt_bias, a_log, d_skip, ssd_norm, w_attn_branch, w_ssd_branch, w_out, ffn2_norm, ffn2_w_gate, ffn2_w_up, ffn2_w_down, loss_target, m_ffn1_norm, m_ffn1_w_gate, m_ffn1_w_up, m_ffn1_w_down, m_mix_norm, m_w_in, m_q_norm, m_k_norm, m_conv_w, m_conv_b, m_dt_bias, m_a_log, m_d_skip, m_ssd_norm, m_w_attn_branch, m_w_ssd_branch, m_w_out, m_ffn2_norm, m_ffn2_w_gate, m_ffn2_w_up, m_ffn2_w_down, v_ffn1_norm, v_ffn1_w_gate, v_ffn1_w_up, v_ffn1_w_down, v_mix_norm, v_w_in, v_q_norm, v_k_norm, v_conv_w, v_conv_b, v_dt_bias, v_a_log, v_d_skip, v_ssd_norm, v_w_attn_branch, v_w_ssd_branch, v_w_out, v_ffn2_norm, v_ffn2_w_gate, v_ffn2_w_up, v_ffn2_w_down):
    given = dict(locals())
    wts = {n: given[n] for n in ALL_WEIGHTS}
    mom = {n: given["m_" + n] for n in ALL_WEIGHTS}
    var = {n: given["v_" + n] for n in ALL_WEIGHTS}
    d = x.shape[-1]
    nh = dt_bias.shape[1]
    my = _dev_index(_mesh_pos())

    def row_form(n, col_sharded):
        a = wts[n][0].T if col_sharded else wts[n][0]
        return jnp.pad(a, ((0, _round_up(a.shape[0], ROW_ALIGN) - a.shape[0]), (0, 0)))

    _Order.tokens, _Order.last = [], None
    shard = {n: row_form(n, cs).astype(MXU_DTYPE) for n, cs, _, _ in BIG}
    entries = {buf: [e for e in BIG if e[2] == buf] for buf in dict.fromkeys(e[2] for e in BIG)}

    def buf_shape(buf):
        r, c = shard[entries[buf][0][0]].shape
        return (len(entries[buf]) * NDEV * r, c)

    def gather_plan(bufs):
        srcs, lands, plan = [], [], []
        for li, buf in enumerate(bufs):
            lands.append(lax.empty(buf_shape(buf), MXU_DTYPE))
            for n, _, _, pos in entries[buf]:
                r = shard[n].shape[0]
                plan.append((len(srcs), 0, 0, li, pos * NDEV * r, r, r))
                srcs.append(shard[n])
        return srcs, lands, plan

    def scatter_plan(bufs, grads):
        srcs, lands, plan, names = [], [], [], []
        for si, buf in enumerate(bufs):
            srcs.append(grads[buf])
            for n, _, _, pos in entries[buf]:
                r, c = shard[n].shape
                plan.append((si, pos * NDEV * r, r, len(lands), 0, r, r))
                lands.append(lax.empty((NDEV * r, c), MXU_DTYPE))
                names.append(n)
        return srcs, lands, plan, names

    first_bufs = ("gu1t", "d1")
    shards, dests, out_shapes = [], [], []
    for bi, buf in enumerate(first_bufs):
        out_shapes.append((buf_shape(buf), MXU_DTYPE))
        for n, _, _, pos in entries[buf]:
            r = shard[n].shape[0]
            shards.append(shard[n])
            dests.append((bi, pos * NDEV * r, r))
    conv_rows = _pack_rows([conv_w[0]], LANE, ROW_ALIGN)
    shards.append(conv_rows)
    dests.append((len(first_bufs), 0, conv_rows.shape[0]))
    out_shapes.append(((NDEV * conv_rows.shape[0], LANE), F32))
    gathered = _gather("gather_first", shards, dests, out_shapes)

    in_cols = w_in.shape[2]
    in_pad = _round_up(in_cols, ROW_ALIGN)
    dt_off = NDEV * in_cols - 2 * d - nh
    second_bufs = ("wint",)
    third_bufs = ("abt", "sb", "out", "gu2t", "d2")
    plan2 = gather_plan(second_bufs)
    started2 = _split_start("gather_in_start", *plan2, after=[gathered[0]],
                            relations=SAME_CORE_AND_SIBLING)
    forwarded, started3 = [], []

    class Weights(dict):
        def after_first_ffn(self):
            lands = _split_wait("gather_in_wait", started2, plan2[2],
                                relations=SAME_CORE_AND_SIBLING)
            forwarded.append(_forward_start("gather_in_forward", lands, plan2[2]))

        def __missing__(self, key):
            if key in ("maint", "dtt"):
                wint = _forward_wait("gather_in_arrive", forwarded[0], plan2[2])[0]
                plan3 = gather_plan(third_bufs)
                started3.append((_split_start("gather_rest_start", *plan3, after=[wint]), plan3[2]))
                self["maint"], self["dtt"] = _regroup_rows(
                    "regroup_w_in", wint, in_cols, in_pad, dt_off, dt_off + nh)
            else:
                st, plan = started3[0]
                for buf, a in zip(third_bufs, _split_wait("gather_rest_wait", st, plan)):
                    self[buf] = a
            return self[key]

    w = Weights(gu1t=gathered[0], d1=gathered[1])
    p = {n: wts[n] for n in REPLICATED}
    conv_all = gathered[-1].reshape(NDEV, conv_rows.shape[0] * LANE)[:, :math.prod(conv_w.shape[1:])]
    p["conv_w"] = (conv_all.reshape((NDEV,) + conv_w.shape[1:]).transpose(1, 0, 2)
                   .reshape(conv_w.shape[1], NDEV * conv_w.shape[2]))

    groups = (("scatter_late", ("gu2t", "d2", "out", "abt", "sb")),
              ("scatter_in", ("maint", "dtt")),
              ("scatter_first", ("gu1t", "d1")))
    in_flight = []

    class Grads(dict):
        def __setitem__(self, key, value):
            dict.__setitem__(self, key, value)
            for tag, need in groups:
                if key in need and all(k in self for k in need):
                    if tag == "scatter_in":
                        gwin = _ungroup_rows("ungroup_w_in", self["maint"], self["dtt"],
                                             in_cols, in_pad, dt_off, dt_off + nh)
                        bufs, grads = ("wint",), {"wint": gwin}
                    else:
                        bufs, grads = need, self
                    srcs, lands, plan, names = scatter_plan(bufs, grads)
                    in_flight.append((tag, _split_start(tag + "_start", srcs, lands, plan),
                                      plan, names))

    loss_vec, dx, gw, gp = _local_step(x[0], loss_target[0], w, p, Grads(),
                                       aw=w_attn_branch.shape[1])

    small_names = REPLICATED + ("conv_w",)
    small_shapes = [gp[n].shape for n in small_names]
    small = _pack_rows([gp[n] for n in small_names], LANE, 8)
    small_all = _exchange("exchange_small", [], [], small)[0]

    outs = [{}, {}, {}, {}]
    col_sharded_of = {n: cs for n, cs, _, _ in BIG}
    for tag, started, plan, names in in_flight:
        for n, rv in zip(names, _split_wait(tag + "_wait", started, plan)):
            rv = rv.reshape(NDEV, shard[n].shape[0], shard[n].shape[1])
            res = _Order.done(_adamw("adamw_" + n, rv, wts[n], mom[n], var[n],
                                     transposed=col_sharded_of[n]))
            for k in range(4):
                outs[k][n] = res[k]

    small_g = _unpack_rows(_sum_slabs("sum_small_grads", small_all), small_shapes)
    small_g = dict(zip(small_names, small_g))
    cs = conv_w.shape[2]
    small_g["conv_w"] = lax.dynamic_slice_in_dim(small_g["conv_w"], my * cs, cs, axis=1)
    small_shard_shapes = [wts[n].shape[-2:] for n in small_names]
    sg = _pack_rows([small_g[n] for n in small_names], LANE, 8)
    sw = _pack_rows([wts[n] for n in small_names], LANE, 8)
    sm = _pack_rows([mom[n] for n in small_names], LANE, 8)
    sv = _pack_rows([var[n] for n in small_names], LANE, 8)
    res_small = _adamw("adamw_small", sg[None], sw, sm, sv, tr=sg.shape[0])
    for k in range(4):
        for n, a in zip(small_names, _unpack_rows(res_small[k], small_shard_shapes)):
            outs[k][n] = a.reshape(wts[n].shape)

    loss = lax.psum(0.5 * jnp.sum(loss_vec) / d, ("x", "y", "c"))
    result = [loss, dx[None]]
    for k in range(4):
        result += [outs[k][n] for n in ALL_WEIGHTS]
    return tuple(result)
```

```python
import functools
import math

import numpy as np
import jax
import jax.numpy as jnp
from jax import lax
from jax.experimental import pallas as pl
from jax.experimental.pallas import tpu as pltpu

F32 = jnp.float32
BF16 = jnp.bfloat16
MXU_DTYPE = BF16
ACT_DTYPE = BF16

NDEV = 8
EPS = 1e-6
HD = 64
QB = 128
PATTERNS = ((128, 1), (512, 4), (2048, 16))
ALIBI_MAX_EXP = 8.0
SSD_P = 64
SSD_N = 128
SSD_G = 4
SSD_Q = 128
SSD_K = 4
NEG = -1e30
LANE = 128
ROW_ALIGN = 16
VMEM_LIMIT = 56 * 1024 * 1024

ADAM_LR, ADAM_B1, ADAM_B2, ADAM_EPS, ADAM_WD, ADAM_STEP = 0.001, 0.9, 0.999, 1e-8, 0.01, 10

NN = (((1,), (0,)), ((), ()))
NT = (((1,), (1,)), ((), ()))
TN = (((0,), (0,)), ((), ()))


def _dot(a, b, dims=NN):
    return lax.dot_general(a.astype(MXU_DTYPE), b.astype(MXU_DTYPE), dims,
                           preferred_element_type=F32)


def _split3(a):
    hi = a.astype(BF16)
    r = a - hi.astype(F32)
    mid = r.astype(BF16)
    lo = (r - mid.astype(F32)).astype(BF16)
    return hi, mid, lo


def _dot3(a, b, dims=NN, split=0):
    if split == 0:
        bb = b.astype(BF16)
        parts = [lax.dot_general(s, bb, dims, preferred_element_type=F32) for s in _split3(a)]
    else:
        aa = a.astype(BF16)
        parts = [lax.dot_general(aa, s, dims, preferred_element_type=F32) for s in _split3(b)]
    return parts[0] + parts[1] + parts[2]


@jax.custom_vjp
def _spread(v, e):
    return _dot3(v, e)


def _spread_fwd(v, e):
    return _dot3(v, e), e


def _spread_bwd(e, g):
    return _dot3(g, e, NT), jnp.zeros_like(e)


_spread.defvjp(_spread_fwd, _spread_bwd)


@jax.custom_vjp
def _running_sum(a, lower):
    return _dot3(lower, a, NN, split=1)


def _running_sum_fwd(a, lower):
    return _dot3(lower, a, NN, split=1), lower


def _running_sum_bwd(lower, g):
    return _dot3(lower, g, TN, split=1), jnp.zeros_like(lower)


_running_sum.defvjp(_running_sum_fwd, _running_sum_bwd)


def _tile(n, cap):
    if n <= cap:
        return n
    best = None
    for t in range(LANE, cap + 1, LANE):
        if n % t == 0:
            best = t
    assert best is not None, (n, cap)
    return best


def _params(sem):
    return pltpu.CompilerParams(dimension_semantics=sem, vmem_limit_bytes=VMEM_LIMIT)


def _round_up(n, m):
    return -(-n // m) * m


class _Order:
    tokens = []
    last = None

    @classmethod
    def take(cls):
        out, cls.tokens = cls.tokens, []
        return out

    @classmethod
    def done(cls, result):
        cls.last = result[0] if isinstance(result, (list, tuple)) else result
        return result


ANY_SPEC = pl.BlockSpec(memory_space=pl.ANY)


def _mm(name, a, b, mode, out_dtype=F32, res=None, scale=1.0,
        cap_m=1408, cap_n=1408, cap_k=1408):
    segs = list(a) if isinstance(a, (list, tuple)) else [a]
    nseg = len(segs)
    if mode == "tn":
        k = segs[0].shape[0]
        widths = [s.shape[1] for s in segs]
        m = sum(widths)
        k2, n = b.shape
        tm = _tile(math.gcd(*widths), cap_m)
        tk = _tile(k, cap_k)
        counts = [wd // tm for wd in widths]
    else:
        m = segs[0].shape[0]
        widths = [s.shape[1] for s in segs]
        k = sum(widths)
        (k2, n) = b.shape if mode == "nn" else b.shape[::-1]
        tm = _tile(m, cap_m)
        tk = _tile(math.gcd(*widths), cap_k)
        counts = [wd // tk for wd in widths]
    assert k == k2, (name, [s.shape for s in segs], b.shape, mode)
    tn = _tile(n, cap_n)
    nk = k // tk
    starts = [sum(counts[:s]) for s in range(nseg)]
    dims = {"nn": NN, "nt": NT, "tn": TN}[mode]

    def a_spec(s):
        lo, cnt = starts[s], counts[s]
        if mode == "tn":
            if nseg == 1:
                return pl.BlockSpec((tk, tm), lambda i, j, kk: (kk, i))
            return pl.BlockSpec(
                (tk, tm), lambda i, j, kk: (jnp.where((i >= lo) & (i < lo + cnt), kk, 0),
                                            jnp.clip(i - lo, 0, cnt - 1)))
        if nseg == 1:
            return pl.BlockSpec((tm, tk), lambda i, j, kk: (i, kk))
        return pl.BlockSpec((tm, tk), lambda i, j, kk: (i, jnp.clip(kk - lo, 0, cnt - 1)))

    b_spec = (pl.BlockSpec((tn, tk), lambda i, j, kk: (j, kk)) if mode == "nt"
              else pl.BlockSpec((tk, tn), lambda i, j, kk: (kk, j)))
    o_spec = pl.BlockSpec((tm, tn), lambda i, j, kk: (i, j))
    has_res = res is not None
    use_acc = nk > 1 or nseg > 1
    ties = _Order.take()
    nt_ = len(ties)

    def body(*refs):
        a_refs = refs[:nseg]
        b_ref = refs[nseg]
        r_ref = refs[nseg + 1] if has_res else None
        o_ref = refs[nseg + 1 + has_res + nt_]
        scr = refs[nseg + 2 + has_res + nt_:]

        def finish(acc):
            if scale != 1.0:
                acc = acc * scale
            if has_res:
                acc = r_ref[...].astype(F32) + acc
            o_ref[...] = acc.astype(o_ref.dtype)

        if not use_acc:
            finish(_dot(a_refs[0][...], b_ref[...], dims))
            return
        acc_ref = scr[0]
        kk = pl.program_id(2)
        sel = pl.program_id(0) if mode == "tn" else kk

        @pl.when(kk == 0)
        def _():
            acc_ref[...] = jnp.zeros_like(acc_ref)

        for s in range(nseg):
            def add(s=s):
                acc_ref[...] += _dot(a_refs[s][...], b_ref[...], dims)
            if nseg == 1:
                add()
            else:
                pl.when((sel >= starts[s]) & (sel < starts[s] + counts[s]))(add)

        @pl.when(kk == nk - 1)
        def _():
            finish(acc_ref[...])

    in_specs = ([a_spec(s) for s in range(nseg)] + [b_spec] + ([o_spec] if has_res else [])
                + [ANY_SPEC] * nt_)
    args = tuple(segs) + (b,) + ((res,) if has_res else ()) + tuple(ties)
    return _Order.done(pl.pallas_call(
        body, name=name,
        out_shape=jax.ShapeDtypeStruct((m, n), out_dtype),
        grid=(m // tm, n // tn, nk),
        in_specs=in_specs, out_specs=o_spec,
        scratch_shapes=[pltpu.VMEM((tm, tn), F32)] if use_acc else [],
        compiler_params=_params(("parallel", "parallel", "arbitrary")),
    )(*args))


def _act(g, u):
    return _silu(g.astype(F32)) * u.astype(F32)


def _ffn_up(name, h, wgut, cap_m=512, cap_n=1408):
    m, k = h.shape
    dff = wgut.shape[0] // 2
    tm, tn = _tile(m, cap_m), _tile(dff, cap_n)
    nj = dff // tn
    ties = _Order.take()

    def body(h_ref, wg_ref, wu_ref, *rest):
        g_ref, u_ref, a_ref = rest[len(ties):]
        hv = h_ref[...]
        g = _dot(hv, wg_ref[...], NT)
        u = _dot(hv, wu_ref[...], NT)
        g_ref[...] = g.astype(g_ref.dtype)
        u_ref[...] = u.astype(u_ref.dtype)
        a_ref[...] = _act(g, u).astype(a_ref.dtype)

    o_spec = pl.BlockSpec((tm, tn), lambda i, j: (i, j))
    return _Order.done(pl.pallas_call(
        body, name=name, out_shape=[jax.ShapeDtypeStruct((m, dff), ACT_DTYPE)] * 3,
        grid=(m // tm, nj),
        in_specs=[pl.BlockSpec((tm, k), lambda i, j: (i, 0)),
                  pl.BlockSpec((tn, k), lambda i, j: (j, 0)),
                  pl.BlockSpec((tn, k), lambda i, j: (nj + j, 0))] + [ANY_SPEC] * len(ties),
        out_specs=[o_spec] * 3,
        compiler_params=_params(("parallel", "parallel")),
    )(h, wgut, wgut, *ties))


def _ffn_dact(name, dxo, wd, g, u, scale, cap_m=512, cap_n=1408):
    m, k = dxo.shape
    dff = wd.shape[0]
    tm, tn = _tile(m, cap_m), _tile(dff, cap_n)
    ties = _Order.take()

    def body(d_ref, w_ref, g_ref, u_ref, *rest):
        dg_ref, du_ref = rest[len(ties):]
        da = _dot(d_ref[...], w_ref[...], NT) * scale
        _, vjp = jax.vjp(_act, g_ref[...], u_ref[...])
        dg, du = vjp(da)
        dg_ref[...] = dg.astype(dg_ref.dtype)
        du_ref[...] = du.astype(du_ref.dtype)

    o_spec = pl.BlockSpec((tm, tn), lambda i, j: (i, j))
    return _Order.done(pl.pallas_call(
        body, name=name, out_shape=[jax.ShapeDtypeStruct((m, dff), ACT_DTYPE)] * 2,
        grid=(m // tm, dff // tn),
        in_specs=[pl.BlockSpec((tm, k), lambda i, j: (i, 0)),
                  pl.BlockSpec((tn, k), lambda i, j: (j, 0)), o_spec, o_spec]
        + [ANY_SPEC] * len(ties),
        out_specs=[o_spec] * 2,
        compiler_params=_params(("parallel", "parallel")),
    )(dxo, wd, g, u, *ties))


def _rw(name, fn, ins, outs, accs=(), tr=256, ncb=1):
    t = next(a.shape[0] for kind, a, _, _ in ins if kind == "row")
    assert t % tr == 0
    n_in = len(ins)
    n_pieces = sum(len(w) for w, _ in outs)

    def spec(kind, arr, width, base):
        if kind == "row":
            return pl.BlockSpec((tr, width), lambda j, i: (i, base + j))
        return pl.BlockSpec((arr.shape[0], width), lambda j, i: (0, base + j))

    in_specs = [spec(*s) for s in ins]
    out_shapes, out_specs = [], []
    for widths, dt in outs:
        w = sum(widths)
        out_shapes.append(jax.ShapeDtypeStruct((t, w * ncb), dt))
        out_specs.append(pl.BlockSpec((tr, w), lambda j, i: (i, j)))
    for rows, width in accs:
        out_shapes.append(jax.ShapeDtypeStruct((rows, width * ncb), F32))
        out_specs.append(pl.BlockSpec((rows, width), lambda j, i: (0, j)))

    ties = _Order.take()
    nt_ = len(ties)
    in_specs = in_specs + [ANY_SPEC] * nt_

    def body(*refs):
        vals = [r[...] for r in refs[:n_in]]
        res = fn(*vals)
        o_refs = refs[n_in + nt_:n_in + nt_ + len(outs)]
        a_refs = refs[n_in + nt_ + len(outs):]
        p = 0
        for (widths, _), o_ref in zip(outs, o_refs):
            off = 0
            for w in widths:
                if len(widths) == 1:
                    o_ref[...] = res[p].astype(o_ref.dtype)
                else:
                    o_ref[:, off:off + w] = res[p].astype(o_ref.dtype)
                off += w
                p += 1
        i = pl.program_id(1)
        for a_ref, v in zip(a_refs, res[n_pieces:]):
            @pl.when(i == 0)
            def _(a_ref=a_ref, v=v):
                a_ref[...] = v

            @pl.when(i > 0)
            def _(a_ref=a_ref, v=v):
                a_ref[...] += v

    return _Order.done(pl.pallas_call(
        body, name=name, out_shape=out_shapes,
        grid=(ncb, t // tr), in_specs=in_specs, out_specs=out_specs,
        compiler_params=_params(("parallel", "arbitrary")),
    )(*[a for _, a, _, _ in ins], *ties))


def _rms(x, g):
    x = x.astype(F32)
    return x * lax.rsqrt(jnp.mean(x * x, axis=-1, keepdims=True) + EPS) * g


def _silu(x):
    return x * jax.nn.sigmoid(x)


def _colsum(v):
    return jnp.sum(v, axis=0, keepdims=True)


def _pair_norm(x, g):
    w = 2 * HD
    ri = lax.broadcasted_iota(jnp.int32, (w, w), 0)
    ci = lax.broadcasted_iota(jnp.int32, (w, w), 1)
    same_head = ((ri < HD) == (ci < HD)).astype(F32)
    ms = _spread(x * x, same_head) * (1.0 / HD)
    return x * lax.rsqrt(ms + EPS) * g


ATTN_SCALE = 1.0 / math.sqrt(HD)


def _attn_bias(coef):
    key = lax.broadcasted_iota(jnp.int32, (QB, QB), 0)
    qry = lax.broadcasted_iota(jnp.int32, (QB, QB), 1)
    dist = (qry - key).astype(F32)
    own = jnp.where(qry >= key, -coef * dist, NEG)
    prev = jnp.where(qry <= key, -coef * (dist + float(QB)), NEG)
    return own, prev


BNT = (((2,), (2,)), ((0,), (0,)))
BTN = (((1,), (1,)), ((0,), (0,)))


def _attn_pair(qn, kcn, kpn, vc, vp, b_own, b_prev):
    nb = qn.shape[0]
    w = 2 * HD
    lane = lax.broadcasted_iota(jnp.int32, (1, 1, w), 2)
    eye = (lax.broadcasted_iota(jnp.int32, (QB, QB), 0)
           == lax.broadcasted_iota(jnp.int32, (QB, QB), 1)).astype(F32)
    out = jnp.zeros((nb, QB, w), F32)
    lb = jnp.zeros((nb * QB, w), F32)
    for hh in range(2):
        mask = ((lane < HD) if hh == 0 else (lane >= HD)).astype(F32)
        qm = qn * mask
        lc = _dot(kcn, qm, BNT) + b_own[hh]
        lp = _dot(kpn, qm, BNT) + b_prev[hh]
        m = lax.stop_gradient(jnp.maximum(jnp.max(lc, axis=1, keepdims=True),
                                          jnp.max(lp, axis=1, keepdims=True)))
        pc = jnp.exp(lc - m)
        pp = jnp.exp(lp - m)
        l = jnp.sum(pc, axis=1, keepdims=True) + jnp.sum(pp, axis=1, keepdims=True)
        inv = 1.0 / l
        out = out + (_dot(pc * inv, vc, BTN) + _dot(pp * inv, vp, BTN)) * mask
        diag = (eye * (m + jnp.log(l))).reshape(nb * QB, QB)
        lb = lb + _spread(diag, jnp.broadcast_to(mask[0], (QB, w)))
    return out, lb.reshape(nb, QB, w)


NORM_ROWS = 128
NORM_UNROLL = 4
EPILOGUE_ROWS = 512
ATTN_BATCH_FWD = 8
ATTN_BATCH_BWD = 4


def _unit_rows(u, d):
    r = u & (d - 1)
    n = u >> (d.bit_length() - 1)

    def rows(blk):
        start = pl.multiple_of(blk * (QB * d), QB * d)
        return pl.ds(start, QB) if d == 1 else pl.ds(start + r, QB, stride=d)

    return rows(n), rows(jnp.maximum(n - 1, 0)), n == 0


def _unit_batch(i, nbatch, d, bias, qf, kf, vf):
    units = [_unit_rows(i * nbatch + j, d) for j in range(nbatch)]
    cur = lambda ref: jnp.stack([ref[c, :] for c, _, _ in units])
    prv = lambda ref: jnp.stack([ref[p, :] for _, p, _ in units])
    b_own = [b[0] for b in bias]
    b_prev = [jnp.stack([jnp.where(first, NEG, b[1]) for _, _, first in units]) for b in bias]
    return units, (cur(qf), cur(kf), prv(kf), cur(vf), prv(vf), b_own, b_prev)


def _q_norm(x, g):
    return _pair_norm(x, g * ATTN_SCALE)


def _attn_prologue(t, q_ref, k_ref, v_ref, qg_ref, kg_ref, qf, kf, vf):
    def chunk(c, carry):
        rows = pl.ds(pl.multiple_of(c * NORM_ROWS, NORM_ROWS), NORM_ROWS)
        qf[rows, :] = _q_norm(q_ref[rows, :].astype(F32), qg_ref[...])
        kf[rows, :] = _pair_norm(k_ref[rows, :].astype(F32), kg_ref[...])
        vf[rows, :] = v_ref[rows, :].astype(F32)
        return carry
    lax.fori_loop(0, t // NORM_ROWS, chunk, 0, unroll=NORM_UNROLL)


def _attn_specs(t, bases):
    w = 2 * HD
    ins = [pl.BlockSpec((t, w), functools.partial(lambda p, c, b: (0, b + p), b=b)) for b in bases]
    gain = pl.BlockSpec((1, w), lambda p, c: (0, 0))
    blk = pl.BlockSpec((t, w), lambda p, c: (0, p))
    return ins, gain, blk


def _attn_fwd(name, proj, bases, qg, kg, coefs, d):
    t = proj.shape[0]
    npairs = coefs.shape[0] // 2
    w = 2 * HD
    ins, gain, blk = _attn_specs(t, bases)

    def body(coef_ref, q_ref, k_ref, v_ref, qg_ref, kg_ref, o_ref, l_ref, qf, kf, vf):
        p = pl.program_id(0)
        bias = (_attn_bias(coef_ref[2 * p]), _attn_bias(coef_ref[2 * p + 1]))
        _attn_prologue(t, q_ref, k_ref, v_ref, qg_ref, kg_ref, qf, kf, vf)

        def step(i, carry):
            units, ins = _unit_batch(i, ATTN_BATCH_FWD, d, bias, qf, kf, vf)
            o, lb = _attn_pair(*ins)
            for j, (cur, _, _) in enumerate(units):
                o_ref[cur, :] = o[j]
                l_ref[cur, :] = lb[j]
            return carry

        lax.fori_loop(0, t // QB // ATTN_BATCH_FWD, step, 0)

    return pl.pallas_call(
        body, name=name,
        out_shape=[jax.ShapeDtypeStruct((t, npairs * w), F32)] * 2,
        grid_spec=pltpu.PrefetchScalarGridSpec(
            num_scalar_prefetch=1, grid=(npairs,),
            in_specs=ins + [gain, gain], out_specs=[blk, blk],
            scratch_shapes=[pltpu.VMEM((t, w), F32)] * 3),
        compiler_params=_params(("arbitrary",)),
    )(coefs, proj, proj, proj, qg, kg)


def _attn_bwd(name, proj, bases, qg, kg, coefs, d, do, dl):
    t = proj.shape[0]
    npairs = coefs.shape[0] // 2
    w = 2 * HD
    ins, gain, blk = _attn_specs(t, bases)

    def body(coef_ref, q_ref, k_ref, v_ref, qg_ref, kg_ref, do_ref, dl_ref,
             dq_ref, dk_ref, dv_ref, dqg_ref, dkg_ref, qf, kf, vf, dqf, dkf, dvf):
        p = pl.program_id(0)
        bias = (_attn_bias(coef_ref[2 * p]), _attn_bias(coef_ref[2 * p + 1]))
        _attn_prologue(t, q_ref, k_ref, v_ref, qg_ref, kg_ref, qf, kf, vf)
        dkf[...] = jnp.zeros_like(dkf)
        dvf[...] = jnp.zeros_like(dvf)

        def step(i, carry):
            units, ins = _unit_batch(i, ATTN_BATCH_BWD, d, bias, qf, kf, vf)
            f = lambda a, b, c, e, g: _attn_pair(a, b, c, e, g, *ins[5:])
            _, vjp = jax.vjp(f, *ins[:5])
            cot = (jnp.stack([do_ref[cur, :] for cur, _, _ in units]),
                   jnp.stack([dl_ref[cur, :] for cur, _, _ in units]))
            dq, dkc, dkp, dvc, dvp = vjp(cot)
            for j, (cur, prv, _) in enumerate(units):
                dqf[cur, :] = dq[j]
                dkf[cur, :] += dkc[j]
                dkf[prv, :] += dkp[j]
                dvf[cur, :] += dvc[j]
                dvf[prv, :] += dvp[j]
            return carry

        lax.fori_loop(0, t // QB // ATTN_BATCH_BWD, step, 0)

        def chunk(c, carry):
            dqg_acc, dkg_acc = carry
            rows = pl.ds(pl.multiple_of(c * EPILOGUE_ROWS, EPILOGUE_ROWS), EPILOGUE_ROWS)
            _, vq = jax.vjp(_q_norm, q_ref[rows, :].astype(F32), qg_ref[...])
            dq, dqg = vq(dqf[rows, :])
            _, vk = jax.vjp(_pair_norm, k_ref[rows, :].astype(F32), kg_ref[...])
            dk, dkg = vk(dkf[rows, :])
            dq_ref[rows, :] = dq.astype(dq_ref.dtype)
            dk_ref[rows, :] = dk.astype(dk_ref.dtype)
            dv_ref[rows, :] = dvf[rows, :].astype(dv_ref.dtype)
            return dqg_acc + dqg, dkg_acc + dkg

        zero = jnp.zeros((1, w), F32)
        dqg, dkg = lax.fori_loop(0, t // EPILOGUE_ROWS, chunk, (zero, zero))

        @pl.when(p == 0)
        def _():
            dqg_ref[...] = dqg
            dkg_ref[...] = dkg

        @pl.when(p > 0)
        def _():
            dqg_ref[...] += dqg
            dkg_ref[...] += dkg

    big = jax.ShapeDtypeStruct((t, npairs * w), ACT_DTYPE)
    small = jax.ShapeDtypeStruct((1, w), F32)
    return pl.pallas_call(
        body, name=name,
        out_shape=[big, big, big, small, small],
        grid_spec=pltpu.PrefetchScalarGridSpec(
            num_scalar_prefetch=1, grid=(npairs,),
            in_specs=ins + [gain, gain, blk, blk],
            out_specs=[blk, blk, blk, gain, gain],
            scratch_shapes=[pltpu.VMEM((t, w), F32)] * 6),
        compiler_params=_params(("arbitrary",)),
    )(coefs, proj, proj, proj, qg, kg, do, dl)


def _shift_down(u, s):
    if s == 0:
        return u
    rows = lax.broadcasted_iota(jnp.int32, u.shape, 0)
    return jnp.where(rows >= s, pltpu.roll(u, s, 0), 0.0)


def _shift_up(u, s):
    if s == 0:
        return u
    t = u.shape[0]
    rows = lax.broadcasted_iota(jnp.int32, u.shape, 0)
    return jnp.where(rows < t - s, pltpu.roll(u, t - s, 0), 0.0)


def _conv_pre(u, w, b):
    y = b
    for kk in range(SSD_K):
        y = y + w[kk:kk + 1, :] * _shift_down(u, SSD_K - 1 - kk)
    return y


def _conv_fwd(name, src, base, w, b, cw=128):
    t = src.shape[0]
    c = w.shape[1]

    def body(u_ref, w_ref, b_ref, o_ref):
        y = _conv_pre(u_ref[...].astype(F32), w_ref[...], b_ref[...])
        o_ref[...] = _silu(y).astype(o_ref.dtype)

    return pl.pallas_call(
        body, name=name, out_shape=jax.ShapeDtypeStruct((t, c), ACT_DTYPE),
        grid=(c // cw,),
        in_specs=[pl.BlockSpec((t, cw), lambda j: (0, base + j)),
                  pl.BlockSpec((SSD_K, cw), lambda j: (0, j)),
                  pl.BlockSpec((1, cw), lambda j: (0, j))],
        out_specs=pl.BlockSpec((t, cw), lambda j: (0, j)),
        compiler_params=_params(("parallel",)),
    )(src, w, b)


def _conv_bwd(name, src, base, w, b, dout, cw=128):
    t = src.shape[0]
    c = w.shape[1]

    def body(u_ref, w_ref, b_ref, d_ref, du_ref, dw_ref, db_ref):
        u = u_ref[...].astype(F32)
        wv = w_ref[...]
        y = _conv_pre(u, wv, b_ref[...])
        sg = jax.nn.sigmoid(y)
        dy = d_ref[...].astype(F32) * (sg * (1.0 + y * (1.0 - sg)))
        du = jnp.zeros_like(u)
        for kk in range(SSD_K):
            s = SSD_K - 1 - kk
            du = du + wv[kk:kk + 1, :] * _shift_up(dy, s)
            dw_ref[kk:kk + 1, :] = _colsum(dy * _shift_down(u, s))
        du_ref[...] = du.astype(du_ref.dtype)
        db_ref[...] = _colsum(dy)

    return pl.pallas_call(
        body, name=name,
        out_shape=[jax.ShapeDtypeStruct((t, c), ACT_DTYPE),
                   jax.ShapeDtypeStruct((SSD_K, c), F32),
                   jax.ShapeDtypeStruct((1, c), F32)],
        grid=(c // cw,),
        in_specs=[pl.BlockSpec((t, cw), lambda j: (0, base + j)),
                  pl.BlockSpec((SSD_K, cw), lambda j: (0, j)),
                  pl.BlockSpec((1, cw), lambda j: (0, j)),
                  pl.BlockSpec((t, cw), lambda j: (0, j))],
        out_specs=[pl.BlockSpec((t, cw), lambda j: (0, j)),
                   pl.BlockSpec((SSD_K, cw), lambda j: (0, j)),
                   pl.BlockSpec((1, cw), lambda j: (0, j))],
        compiler_params=_params(("parallel",)),
    )(src, w, b, dout)


def _softplus(x):
    return jnp.maximum(x, 0.0) + jnp.log(1.0 + jnp.exp(-jnp.abs(x)))


def _ssd_chunk(xbc, dtraw, bias, alog, states):
    wd = states[0].shape[1]
    nj = wd // SSD_P
    inner = SSD_G * wd
    dt = _softplus(dtraw + bias)
    a = dt * (-jnp.exp(alog))
    li = lax.broadcasted_iota(jnp.int32, (SSD_Q, SSD_Q), 0)
    si = lax.broadcasted_iota(jnp.int32, (SSD_Q, SSD_Q), 1)
    causal = li >= si
    acs = _running_sum(a, causal.astype(F32))
    acs_t = acs.T
    a_last = acs[SSD_Q - 1:SSD_Q, :]
    grow = jnp.exp(acs)
    shrink = jnp.exp(a_last - acs)
    hrow = lax.broadcasted_iota(jnp.int32, (LANE, wd), 0)
    wcol = lax.broadcasted_iota(jnp.int32, (LANE, wd), 1)
    lane = lax.broadcasted_iota(jnp.int32, (1, LANE), 1)
    ys, snext = [], []
    for g in range(SSD_G):
        lo = (hrow - g * nj) * SSD_P
        head_lanes = jnp.logical_and(wcol >= lo, wcol < lo + SSD_P).astype(F32)
        xs = xbc[:, g * wd:(g + 1) * wd]
        bm = xbc[:, inner + g * SSD_N:inner + (g + 1) * SSD_N]
        cm = xbc[:, inner + (SSD_G + g) * SSD_N:inner + (SSD_G + g + 1) * SSD_N]
        xdt = xs * _dot(dt, head_lanes)
        grow_x = _spread(grow, head_lanes)
        y_off = _dot(cm, states[g]) * grow_x
        s_new = (states[g] * grow_x[SSD_Q - 1:SSD_Q, :]
                 + _dot(bm, xdt * _dot(shrink, head_lanes), TN))
        cb = _dot(cm, bm, NT)
        pieces = []
        for i in range(wd // LANE):
            xp = xdt[:, i * LANE:(i + 1) * LANE]
            acc = jnp.zeros((SSD_Q, LANE), F32)
            for hh in range(LANE // SSD_P):
                h = g * nj + i * (LANE // SSD_P) + hh
                decay = jnp.exp(jnp.where(causal, acs[:, h:h + 1] - acs_t[h:h + 1, :], NEG))
                keep = jnp.logical_and(lane >= hh * SSD_P, lane < (hh + 1) * SSD_P).astype(F32)
                acc = acc + _dot(cb * decay, xp * keep)
            pieces.append(acc)
        y_diag = pieces[0] if len(pieces) == 1 else jnp.concatenate(pieces, axis=1)
        ys.append(y_diag + y_off)
        snext.append(s_new)
    return ys, snext


def _ssd_specs(cdim, wd, rev, nc):
    ch = (lambda c: nc - 1 - c) if rev else (lambda c: c)
    full = lambda width: pl.BlockSpec((SSD_Q, width), lambda c: (ch(c), 0))
    vec = pl.BlockSpec((1, LANE), lambda c: (0, 0))
    st = pl.BlockSpec((1, SSD_G, SSD_N, wd), lambda c: (ch(c), 0, 0, 0))
    return full, vec, st


def _ssd_fwd(name, xbc, dtraw, bias, alog, inner):
    t, cdim = xbc.shape
    wd = inner // SSD_G
    nc = t // SSD_Q
    full, vec, st = _ssd_specs(cdim, wd, False, nc)

    def body(x_ref, r_ref, b_ref, a_ref, y_ref, st_ref, s_scr):
        @pl.when(pl.program_id(0) == 0)
        def _():
            s_scr[...] = jnp.zeros_like(s_scr)

        sprev = [s_scr[g] for g in range(SSD_G)]
        ys, snext = _ssd_chunk(x_ref[...].astype(F32), r_ref[...], b_ref[...], a_ref[...], sprev)
        for g in range(SSD_G):
            st_ref[0, g] = sprev[g]
            y_ref[:, g * wd:(g + 1) * wd] = ys[g]
            s_scr[g] = snext[g]

    return pl.pallas_call(
        body, name=name,
        out_shape=[jax.ShapeDtypeStruct((t, inner), F32),
                   jax.ShapeDtypeStruct((nc, SSD_G, SSD_N, wd), F32)],
        grid=(nc,),
        in_specs=[full(cdim), full(LANE), vec, vec],
        out_specs=[full(inner), st],
        scratch_shapes=[pltpu.VMEM((SSD_G, SSD_N, wd), F32)],
        compiler_params=_params(("arbitrary",)),
    )(xbc, dtraw, bias, alog)


def _ssd_bwd(name, xbc, dtraw, bias, alog, states, dy, dxs_extra):
    t, cdim = xbc.shape
    inner = dy.shape[1]
    wd = inner // SSD_G
    nc = t // SSD_Q
    full, vec, st = _ssd_specs(cdim, wd, True, nc)

    def body(x_ref, r_ref, b_ref, a_ref, st_ref, dy_ref, dx0_ref,
             dx_ref, dr_ref, db_ref, da_ref, ds_scr):
        first = pl.program_id(0) == 0

        @pl.when(first)
        def _():
            ds_scr[...] = jnp.zeros_like(ds_scr)

        sprev = [st_ref[0, g] for g in range(SSD_G)]
        _, vjp = jax.vjp(_ssd_chunk, x_ref[...].astype(F32), r_ref[...], b_ref[...], a_ref[...],
                         sprev)
        dyv = dy_ref[...]
        dys = [dyv[:, g * wd:(g + 1) * wd] for g in range(SSD_G)]
        dsn = [ds_scr[g] for g in range(SSD_G)]
        dx, dr, db, da, dsp = vjp((dys, dsn))
        dx_ref[:, :inner] = dx[:, :inner] + dx0_ref[...].astype(F32)
        dx_ref[:, inner:] = dx[:, inner:]
        dr_ref[...] = dr
        for g in range(SSD_G):
            ds_scr[g] = dsp[g]

        @pl.when(first)
        def _():
            db_ref[...] = db
            da_ref[...] = da

        @pl.when(jnp.logical_not(first))
        def _():
            db_ref[...] += db
            da_ref[...] += da

    return pl.pallas_call(
        body, name=name,
        out_shape=[jax.ShapeDtypeStruct((t, cdim), F32),
                   jax.ShapeDtypeStruct((t, LANE), F32),
                   jax.ShapeDtypeStruct((1, LANE), F32),
                   jax.ShapeDtypeStruct((1, LANE), F32)],
        grid=(nc,),
        in_specs=[full(cdim), full(LANE), vec, vec, st, full(inner), full(inner)],
        out_specs=[full(cdim), full(LANE), vec, vec],
        scratch_shapes=[pltpu.VMEM((SSD_G, SSD_N, wd), F32)],
        compiler_params=_params(("arbitrary",)),
    )(xbc, dtraw, bias, alog, states, dy, dxs_extra)


def _mix(o0, o1, o2, l0, l1, l2):
    m = lax.stop_gradient(jnp.maximum(jnp.maximum(l0, l1), l2))
    e0, e1, e2 = jnp.exp(l0 - m), jnp.exp(l1 - m), jnp.exp(l2 - m)
    return (e0 * o0 + e1 * o1 + e2 * o2) / (e0 + e1 + e2)


def _gate(y, xs, z, dexp, gain):
    v = (y + xs.astype(F32) * dexp) * _silu(z.astype(F32))
    return _rms(v, gain)


def _merge(ga, gs, ap, sp):
    return jax.nn.sigmoid(ga.astype(F32)) * ap + jax.nn.sigmoid(gs.astype(F32)) * sp


def _alibi_coefs(hp):
    n = hp * len(PATTERNS)
    slopes = np.exp2(-ALIBI_MAX_EXP * np.arange(1, n + 1, dtype=np.float32) / n).astype(np.float32)
    return [jnp.asarray(slopes[g * hp:(g + 1) * hp] * np.float32(d))
            for g, (_, d) in enumerate(PATTERNS)]


def _local_step(x, tgt, w, p, gw_=None, aw=None):
    t, d = x.shape
    dff = w["gu1t"].shape[0] // 2
    aw = w["abt"].shape[1] if aw is None else aw
    hp = aw // HD
    qkv = len(PATTERNS) * aw
    inner = p["ssd_norm"].shape[1]
    nh = p["dt_bias"].shape[1]
    gw_ = {} if gw_ is None else gw_
    gw = inner // SSD_G
    cdim = inner + 2 * SSD_G * SSD_N
    z_off, xbc_off = 3 * qkv, 3 * qkv + inner
    ga_off = xbc_off + cdim
    gs_off = ga_off + d
    hw = d // 2
    assert z_off % gw == 0 and xbc_off % LANE == 0 and ga_off % hw == 0 and gs_off % hw == 0
    assert (nh // SSD_G) * SSD_P == gw and hp % 2 == 0 and aw % LANE == 0 and nh <= LANE
    gdt = MXU_DTYPE

    row = lambda a, width, base=0: ("row", a, width, base)
    const = lambda a, width, base=0: ("const", a, width, base)

    def rms_fwd(name, xin, g):
        return _rw(name, lambda xv, gv: (_rms(xv, gv),), [row(xin, d), const(g, d)],
                   [((d,), ACT_DTYPE)])[0]

    def rms_bwd(name, xin, g, dh, dres):
        def fn(xv, gv, dhv, drv):
            _, vjp = jax.vjp(_rms, xv, gv)
            dx, dg = vjp(dhv.astype(F32))
            return drv + dx, dg
        return _rw(name, fn, [row(xin, d), const(g, d), row(dh, d), row(dres, d)],
                   [((d,), F32)], accs=[(1, d)])

    def ffn_fwd(tag, xin, g, key_gu, key_d):
        h = rms_fwd(tag + "_norm", xin, g)
        gate, up, a = _ffn_up(tag + "_up", h, w[key_gu])
        xo = _mm(tag + "_down", a, w[key_d], "nn", F32, res=xin, scale=0.5)
        return xo, (h, gate, up, a)

    def ffn_bwd(tag, xin, g, wgut, wd, saved, dxo, key_gu, key_d):
        h, gate, up, a = saved
        gw_[key_d] = _mm(tag + "_dwd", a, dxo, "tn", gdt, scale=0.5)
        dgu = _ffn_dact(tag + "_dact", dxo, wd, gate, up, 0.5)
        gw_[key_gu] = _mm(tag + "_dwgu", dgu, h, "tn", gdt)
        dh = _mm(tag + "_dh", dgu, wgut, "nn", F32)
        return rms_bwd(tag + "_dnorm", xin, g, dh, dxo)

    x1, ffn1_saved = ffn_fwd("ffn1", x, p["ffn1_norm"], "gu1t", "d1")
    if hasattr(w, "after_first_ffn"):
        w.after_first_ffn()
    h2 = rms_fwd("mix_norm", x1, p["mix_norm"])
    proj = _mm("in_proj", h2, w["maint"], "nt", ACT_DTYPE, cap_m=512, cap_n=2944)
    dtraw = _mm("dt_proj", h2, w["dtt"], "nt", F32)

    coefs = _alibi_coefs(hp)
    qg2 = jnp.concatenate([p["q_norm"], p["q_norm"]], axis=1)
    kg2 = jnp.concatenate([p["k_norm"], p["k_norm"]], axis=1)
    pw = 2 * HD
    attn_bases = [[(off + gi * aw) // pw for off in (0, qkv, 2 * qkv)]
                  for gi in range(len(PATTERNS))]
    attn_o, attn_l = [], []
    for gi, (_, dil) in enumerate(PATTERNS):
        o, l = _attn_fwd(f"attn_fwd{gi}", proj, attn_bases[gi], qg2, kg2, coefs[gi], dil)
        attn_o.append(o)
        attn_l.append(l)
    ao = _rw("attn_mix", lambda *v: (_mix(*v),), [row(a, aw) for a in attn_o + attn_l],
             [((aw,), ACT_DTYPE)])[0]

    xbc = _conv_fwd("conv_fwd", proj, xbc_off // LANE, p["conv_w"], p["conv_b"])
    pad = lambda v: jnp.pad(v, ((0, 0), (0, LANE - nh)))
    bias_p, alog_p = pad(p["dt_bias"]), pad(p["a_log"])
    yssd, states = _ssd_fwd("ssd_fwd", xbc, dtraw, bias_p, alog_p, inner)
    dexp = jnp.repeat(p["d_skip"], SSD_P, axis=1)
    gate_ins = [row(yssd, gw), row(xbc, gw), row(proj, gw, z_off // gw),
                const(dexp, gw), const(p["ssd_norm"], gw)]
    yn = _rw("ssd_gate", lambda *v: (_gate(*v),), gate_ins, [((gw,), ACT_DTYPE)], ncb=SSD_G)[0]

    ap = _mm("attn_out", ao, w["abt"], "nt", F32)
    sp = _mm("ssd_out", yn, w["sb"], "nn", F32)
    merge_ins = [row(proj, hw, ga_off // hw), row(proj, hw, gs_off // hw), row(ap, hw), row(sp, hw)]
    mg = _rw("merge", lambda *v: (_merge(*v),), merge_ins, [((hw,), ACT_DTYPE)], ncb=2)[0]
    x2 = _mm("mix_out", mg, w["out"], "nn", F32, res=x1)
    x3, ffn2_saved = ffn_fwd("ffn2", x2, p["ffn2_norm"], "gu2t", "d2")

    def loss_fn(yv, tv):
        e = yv - tv
        return e * (1.0 / d), _colsum(e * e)
    dy, loss_vec = _rw("loss", loss_fn, [row(x3, d), row(tgt, d)], [((d,), F32)], accs=[(1, d)])

    gp = {}
    dx2, gp["ffn2_norm"] = ffn_bwd(
        "ffn2", x2, p["ffn2_norm"], w["gu2t"], w["d2"], ffn2_saved, dy, "gu2t", "d2")
    dmg = _mm("d_merge", dx2, w["out"], "nt", ACT_DTYPE)
    gw_["out"] = _mm("dw_out", mg, dx2, "tn", gdt)

    def merge_bwd(gav, gsv, apv, spv, dv):
        _, vjp = jax.vjp(_merge, gav, gsv, apv, spv)
        return vjp(dv.astype(F32))
    dga, dgs, dap, dsp = _rw("d_merge_gate", merge_bwd, merge_ins + [row(dmg, hw)],
                             [((hw,), ACT_DTYPE)] * 4, ncb=2)
    gw_["abt"] = _mm("dw_ab", dap, ao, "tn", gdt)
    dao = _mm("d_attn_o", dap, w["abt"], "nn", F32)
    gw_["sb"] = _mm("dw_sb", yn, dsp, "tn", gdt)
    dyn = _mm("d_ssd_y", dsp, w["sb"], "nt", F32)

    def gate_bwd(yv, xv, zv, dev, gv, dv):
        _, vjp = jax.vjp(_gate, yv, xv, zv, dev, gv)
        return vjp(dv)
    dyssd, dxs_gate, dz, ddexp, gp["ssd_norm"] = _rw(
        "d_ssd_gate", gate_bwd, gate_ins + [row(dyn, gw)],
        [((gw,), F32), ((gw,), F32), ((gw,), ACT_DTYPE)], accs=[(1, gw), (1, gw)], ncb=SSD_G)
    gp["d_skip"] = ddexp.reshape(nh, SSD_P).sum(axis=1).reshape(1, nh)

    dxbc, ddtraw, dbias, dalog = _ssd_bwd("ssd_bwd", xbc, dtraw, bias_p, alog_p, states,
                                          dyssd, dxs_gate)
    gp["dt_bias"], gp["a_log"] = dbias[:, :nh], dalog[:, :nh]
    du, gp["conv_w"], gp["conv_b"] = _conv_bwd("conv_bwd", proj, xbc_off // LANE,
                                               p["conv_w"], p["conv_b"], dxbc)

    def mix_bwd(*v):
        _, vjp = jax.vjp(_mix, *v[:6])
        return vjp(v[6])
    dmix = _rw("d_attn_mix", mix_bwd, [row(a, aw) for a in attn_o + attn_l] + [row(dao, aw)],
               [((aw,), F32)] * 6)
    dq, dk, dv = [], [], []
    dqg = dkg = None
    for gi, (_, dil) in enumerate(PATTERNS):
        r = _attn_bwd(f"attn_bwd{gi}", proj, attn_bases[gi], qg2, kg2, coefs[gi], dil,
                      dmix[gi], dmix[3 + gi])
        dq.append(r[0])
        dk.append(r[1])
        dv.append(r[2])
        dqg = r[3] if dqg is None else dqg + r[3]
        dkg = r[4] if dkg is None else dkg + r[4]
    gp["q_norm"] = dqg[:, :HD] + dqg[:, HD:]
    gp["k_norm"] = dkg[:, :HD] + dkg[:, HD:]

    segs = dq + dk + dv + [dz, du, dga, dgs]
    gw_["maint"] = _mm("dw_in", segs, h2, "tn", gdt)
    gw_["dtt"] = _mm("dw_dt", ddtraw, h2, "tn", gdt)
    dh2 = _mm("d_h2_main", segs, w["maint"], "nn", F32)
    dh2 = _mm("d_h2_dt", ddtraw, w["dtt"], "nn", F32, res=dh2)
    dx1, gp["mix_norm"] = rms_bwd("d_mix_norm", x1, p["mix_norm"], dh2, dx2)
    dx0, gp["ffn1_norm"] = ffn_bwd(
        "ffn1", x, p["ffn1_norm"], w["gu1t"], w["d1"], ffn1_saved, dx1, "gu1t", "d1")
    return loss_vec, dx0, gw_, gp


MESH = pl.DeviceIdType.MESH
HBM_SPEC = pl.BlockSpec(memory_space=pltpu.HBM)


def _mesh_pos():
    return lax.axis_index("x"), lax.axis_index("y"), lax.axis_index("c")


def _flip(pos, k):
    x, y, c = pos
    return (1 - x if k & 4 else x, 1 - y if k & 2 else y, 1 - c if k & 1 else c)


def _dev_index(pos):
    return 4 * pos[0] + 2 * pos[1] + pos[2]


def _rows_of(ref, base, stride, rows, pos):
    start = pl.multiple_of(base + stride * _dev_index(pos), ROW_ALIGN)
    return ref.at[pl.ds(start, rows)]


def _gather(name, shards, dests, out_shapes):
    n = len(shards)
    n_out = len(out_shapes)

    def body(*refs):
        x_refs = refs[:n]
        o_refs = refs[n:n + n_out]
        send_sems, recv_sems, local_sems = refs[n + n_out:]
        me = _mesh_pos()
        sibling = _flip(me, 1)
        chips = [_flip(me, 4), _flip(me, 2), _flip(me, 6)]

        def slot(i, block):
            k_out, base, stride = dests[i]
            return _rows_of(o_refs[k_out], base, stride, shards[i].shape[0], block)

        def copy(i, k, block, to, src=None):
            dst = slot(i, block)
            return pltpu.make_async_remote_copy(
                src_ref=dst if src is None else src, dst_ref=dst,
                send_sem=send_sems.at[7 * i + k], recv_sem=recv_sems.at[7 * i + k],
                device_id=to, device_id_type=MESH)

        mine = [pltpu.make_async_copy(x_refs[i], slot(i, me), local_sems.at[i]) for i in range(n)]
        for cp in mine:
            cp.start()
        first = []
        for i in range(n):
            first.append(copy(i, 0, me, sibling, src=x_refs[i]))
            first += [copy(i, 1 + j, me, chip, src=x_refs[i]) for j, chip in enumerate(chips)]
        for cp in first:
            cp.start()
        passed = []
        for j, chip in enumerate(chips):
            for i in range(n):
                copy(i, 1 + j, chip, me).wait_recv()
                fwd = copy(i, 4 + j, chip, sibling)
                fwd.start()
                passed.append(fwd)
        for i in range(n):
            copy(i, 0, sibling, me).wait_recv()
            for j, chip in enumerate(chips):
                copy(i, 4 + j, _flip(chip, 1), me).wait_recv()
        for cp in first + passed:
            cp.wait_send()
        for cp in mine:
            cp.wait()

    return pl.pallas_call(
        body, name=name,
        out_shape=[jax.ShapeDtypeStruct(s, dt) for s, dt in out_shapes],
        in_specs=[HBM_SPEC] * n, out_specs=[HBM_SPEC] * n_out,
        scratch_shapes=[pltpu.SemaphoreType.DMA((7 * n,)), pltpu.SemaphoreType.DMA((7 * n,)),
                        pltpu.SemaphoreType.DMA((n,))],
    )(*shards)


def _exchange(name, grads, srcs, small):
    n = len(srcs)
    ng = len(grads)

    def body(*refs):
        g_refs = refs[:ng]
        m_ref = refs[ng]
        r_refs = refs[ng + 1:ng + 1 + n]
        s_ref = refs[ng + 1 + n]
        send_sems, recv_sems, local_sems = refs[ng + 2 + n:]
        me = _mesh_pos()
        my = _dev_index(me)

        def slab(i, pos):
            gi, base, stride, rows = srcs[i]
            return _rows_of(g_refs[gi], base, stride, rows, pos)

        own = [pltpu.make_async_copy(slab(i, me), r_refs[i].at[my], local_sems.at[i])
               for i in range(n)]
        own.append(pltpu.make_async_copy(m_ref, s_ref.at[my], local_sems.at[n]))
        for cp in own:
            cp.start()

        def copies(k, src_pos, slot_pos):
            peer = _flip(me, k)
            si = _dev_index(slot_pos)
            out = [pltpu.make_async_remote_copy(
                src_ref=slab(i, src_pos), dst_ref=r_refs[i].at[si],
                send_sem=send_sems.at[7 * i + k - 1], recv_sem=recv_sems.at[7 * i + k - 1],
                device_id=peer, device_id_type=MESH) for i in range(n)]
            out.append(pltpu.make_async_remote_copy(
                src_ref=m_ref, dst_ref=s_ref.at[si],
                send_sem=send_sems.at[7 * n + k - 1], recv_sem=recv_sems.at[7 * n + k - 1],
                device_id=peer, device_id_type=MESH))
            return out

        sent = [cp for k in range(1, NDEV) for cp in copies(k, _flip(me, k), me)]
        for cp in sent:
            cp.start()
        for k in range(1, NDEV):
            for cp in copies(k, me, _flip(me, k)):
                cp.wait_recv()
        for cp in sent:
            cp.wait_send()
        for cp in own:
            cp.wait()

    out_shape = [jax.ShapeDtypeStruct((NDEV, rows, grads[gi].shape[1]), grads[gi].dtype)
                 for gi, _, _, rows in srcs]
    out_shape.append(jax.ShapeDtypeStruct((NDEV,) + small.shape, small.dtype))
    return pl.pallas_call(
        body, name=name, out_shape=out_shape,
        in_specs=[HBM_SPEC] * (ng + 1), out_specs=[HBM_SPEC] * (n + 1),
        scratch_shapes=[pltpu.SemaphoreType.DMA((7 * (n + 1),)),
                        pltpu.SemaphoreType.DMA((7 * (n + 1),)),
                        pltpu.SemaphoreType.DMA((n + 1,))],
    )(*grads, small)


SEM_SPEC = pl.BlockSpec(memory_space=pltpu.SEMAPHORE)
SIDE_EFFECT = pltpu.SideEffectType.DATAFLOW_SIDE_EFFECTING


def _split_refs(plan, srcs, lands, i, src_for, land_from):
    si, sbase, sstride, li, lbase, lstride, rows = plan[i]
    return (_rows_of(srcs[si], sbase, sstride, rows, src_for),
            _rows_of(lands[li], lbase, lstride, rows, land_from))


ALL_PEERS = tuple(range(1, NDEV))
SAME_CORE_AND_SIBLING = (1, 4, 2, 6)
OTHER_CHIPS = (4, 2, 6)


def _split_start(name, srcs, lands, plan, after=(), relations=ALL_PEERS):
    ns, nl, n = len(srcs), len(lands), len(plan)

    def body(*refs):
        s_refs = refs[:ns]
        l_refs = refs[ns:ns + nl]
        send_sems, recv_sems = refs[ns + nl + len(after):ns + nl + len(after) + 2]
        local_sems = refs[ns + nl + len(after) + 2]
        token = refs[ns + nl + len(after) + 3 + ns + nl]
        me = _mesh_pos()
        for i in range(n):
            src, dst = _split_refs(plan, s_refs, l_refs, i, me, me)
            pltpu.make_async_copy(src, dst, local_sems.at[i]).start()
        for k in relations:
            peer = _flip(me, k)
            for i in range(n):
                src, dst = _split_refs(plan, s_refs, l_refs, i, peer, me)
                pltpu.make_async_remote_copy(
                    src_ref=src, dst_ref=dst,
                    send_sem=send_sems.at[7 * i + k - 1], recv_sem=recv_sems.at[7 * i + k - 1],
                    device_id=peer, device_id_type=MESH).start()
        token[...] = jnp.zeros_like(token)

    hbm = lambda a: pltpu.HBM(a.shape, a.dtype)
    out_shape = ((pltpu.SemaphoreType.DMA((7 * n,)), pltpu.SemaphoreType.DMA((7 * n,)),
                  pltpu.SemaphoreType.DMA((n,)))
                 + tuple(hbm(a) for a in srcs) + tuple(hbm(a) for a in lands)
                 + (jax.ShapeDtypeStruct((8, LANE), F32),))
    out = pl.pallas_call(
        body, name=name, out_shape=out_shape,
        in_specs=[HBM_SPEC] * (ns + nl) + [ANY_SPEC] * len(after),
        out_specs=(SEM_SPEC, SEM_SPEC, SEM_SPEC) + (HBM_SPEC,) * (ns + nl)
        + (pl.BlockSpec(memory_space=pltpu.VMEM),),
        input_output_aliases={i: 3 + i for i in range(ns + nl)},
        compiler_params=pltpu.CompilerParams(has_side_effects=SIDE_EFFECT),
    )(*[pltpu.with_memory_space_constraint(a, pltpu.HBM) for a in tuple(srcs) + tuple(lands)],
      *after)
    _Order.tokens.append(out[-1])
    return out[0], out[1], out[2], out[3:3 + ns], out[3 + ns:3 + ns + nl]


def _split_wait(name, started, plan, relations=ALL_PEERS):
    send_sems, recv_sems, local_sems, srcs, lands = started
    ns, nl, n = len(srcs), len(lands), len(plan)
    after = [_Order.last] if _Order.last is not None else []

    def body(*refs):
        s_refs = refs[:ns]
        l_refs = refs[ns:ns + nl]
        send_sems, recv_sems, local_sems = refs[ns + nl:ns + nl + 3]
        me = _mesh_pos()
        for i in range(n):
            src, dst = _split_refs(plan, s_refs, l_refs, i, me, me)
            pltpu.make_async_copy(src, dst, local_sems.at[i]).wait()
        for k in relations:
            peer = _flip(me, k)
            for i in range(n):
                src, dst = _split_refs(plan, s_refs, l_refs, i, peer, peer)
                cp = pltpu.make_async_remote_copy(
                    src_ref=src, dst_ref=dst,
                    send_sem=send_sems.at[7 * i + k - 1], recv_sem=recv_sems.at[7 * i + k - 1],
                    device_id=peer, device_id_type=MESH)
                cp.wait_send()
                cp.wait_recv()

    hbm = lambda a: pltpu.HBM(a.shape, a.dtype)
    out = pl.pallas_call(
        body, name=name,
        out_shape=tuple(hbm(a) for a in srcs) + tuple(hbm(a) for a in lands),
        in_specs=[HBM_SPEC] * (ns + nl) + [SEM_SPEC] * 3 + [ANY_SPEC] * len(after),
        out_specs=(HBM_SPEC,) * (ns + nl),
        input_output_aliases={i: i for i in range(ns + nl)},
        compiler_params=pltpu.CompilerParams(has_side_effects=SIDE_EFFECT),
    )(*srcs, *lands, send_sems, recv_sems, local_sems, *after)
    return list(out[ns:])


def _forward_refs(plan, lands, i, block):
    _, _, _, li, lbase, lstride, rows = plan[i]
    return _rows_of(lands[li], lbase, lstride, rows, block)


def _forward_start(name, lands, plan):
    nl, n = len(lands), len(plan)

    def body(*refs):
        l_refs = refs[:nl]
        send_sems, recv_sems = refs[nl:nl + 2]
        token = refs[nl + 2 + nl]
        me = _mesh_pos()
        for j, kc in enumerate(OTHER_CHIPS):
            for i in range(n):
                rows = _forward_refs(plan, l_refs, i, _flip(me, kc))
                pltpu.make_async_remote_copy(
                    src_ref=rows, dst_ref=rows,
                    send_sem=send_sems.at[3 * i + j], recv_sem=recv_sems.at[3 * i + j],
                    device_id=_flip(me, 1), device_id_type=MESH).start()
        token[...] = jnp.zeros_like(token)

    hbm = lambda a: pltpu.HBM(a.shape, a.dtype)
    out = pl.pallas_call(
        body, name=name,
        out_shape=((pltpu.SemaphoreType.DMA((3 * n,)), pltpu.SemaphoreType.DMA((3 * n,)))
                   + tuple(hbm(a) for a in lands) + (jax.ShapeDtypeStruct((8, LANE), F32),)),
        in_specs=[HBM_SPEC] * nl,
        out_specs=(SEM_SPEC, SEM_SPEC) + (HBM_SPEC,) * nl
        + (pl.BlockSpec(memory_space=pltpu.VMEM),),
        input_output_aliases={i: 2 + i for i in range(nl)},
        compiler_params=pltpu.CompilerParams(has_side_effects=SIDE_EFFECT),
    )(*[pltpu.with_memory_space_constraint(a, pltpu.HBM) for a in lands])
    _Order.tokens.append(out[-1])
    return out[0], out[1], out[2:2 + nl]


def _forward_wait(name, started, plan):
    send_sems, recv_sems, lands = started
    nl, n = len(lands), len(plan)
    after = [_Order.last] if _Order.last is not None else []

    def body(*refs):
        l_refs = refs[:nl]
        send_sems, recv_sems = refs[nl:nl + 2]
        me = _mesh_pos()
        for j, kc in enumerate(OTHER_CHIPS):
            for i in range(n):
                sent = _forward_refs(plan, l_refs, i, _flip(me, kc))
                came = _forward_refs(plan, l_refs, i, _flip(_flip(me, 1), kc))
                cp = pltpu.make_async_remote_copy(
                    src_ref=sent, dst_ref=came,
                    send_sem=send_sems.at[3 * i + j], recv_sem=recv_sems.at[3 * i + j],
                    device_id=_flip(me, 1), device_id_type=MESH)
                cp.wait_send()
                cp.wait_recv()

    hbm = lambda a: pltpu.HBM(a.shape, a.dtype)
    out = pl.pallas_call(
        body, name=name, out_shape=tuple(hbm(a) for a in lands),
        in_specs=[HBM_SPEC] * nl + [SEM_SPEC] * 2 + [ANY_SPEC] * len(after),
        out_specs=(HBM_SPEC,) * nl,
        input_output_aliases={i: i for i in range(nl)},
        compiler_params=pltpu.CompilerParams(has_side_effects=SIDE_EFFECT),
    )(*lands, send_sems, recv_sems, *after)
    return list(out)


def _regroup_rows(name, padded, r, rp, lo, hi):
    d = padded.shape[1]
    pack = 4 // padded.dtype.itemsize
    assert r % pack == 0 and rp % ROW_ALIGN == 0 and lo % (8 * pack) == 0 and hi % (8 * pack) == 0
    r2, rp2, lo2, hi2 = r // pack, rp // pack, lo // pack, hi // pack
    u32 = jnp.uint32

    def body(x_ref, main_ref, cut_ref):
        x = pltpu.bitcast(x_ref[...], u32)
        joined = jnp.concatenate([x[rp2 * j:rp2 * j + r2] for j in range(NDEV)], axis=0)
        main = jnp.concatenate([joined[:lo2], joined[hi2:]], axis=0)
        cut = jnp.concatenate([joined[lo2:hi2], jnp.zeros((LANE // pack - (hi2 - lo2), LANE), u32)],
                              axis=0)
        main_ref[...] = pltpu.bitcast(main, padded.dtype)
        cut_ref[...] = pltpu.bitcast(cut, padded.dtype)

    return pl.pallas_call(
        body, name=name,
        out_shape=[jax.ShapeDtypeStruct((NDEV * r - (hi - lo), d), padded.dtype),
                   jax.ShapeDtypeStruct((LANE, d), padded.dtype)],
        grid=(d // LANE,),
        in_specs=[pl.BlockSpec((NDEV * rp, LANE), lambda i: (0, i))],
        out_specs=[pl.BlockSpec((NDEV * r - (hi - lo), LANE), lambda i: (0, i)),
                   pl.BlockSpec((LANE, LANE), lambda i: (0, i))],
        compiler_params=_params(("parallel",)),
    )(padded)


def _ungroup_rows(name, main, cut, r, rp, lo, hi):
    d = main.shape[1]
    pack = 4 // main.dtype.itemsize
    r2, rp2, lo2, hi2 = r // pack, rp // pack, lo // pack, hi // pack
    u32 = jnp.uint32

    def body(main_ref, cut_ref, o_ref):
        m = pltpu.bitcast(main_ref[...], u32)
        c = pltpu.bitcast(cut_ref[...], u32)
        joined = jnp.concatenate([m[:lo2], c[:hi2 - lo2], m[lo2:]], axis=0)
        zeros = jnp.zeros((rp2 - r2, LANE), u32)
        parts = []
        for j in range(NDEV):
            parts += [joined[r2 * j:r2 * (j + 1)], zeros]
        o_ref[...] = pltpu.bitcast(jnp.concatenate(parts, axis=0), main.dtype)

    return pl.pallas_call(
        body, name=name,
        out_shape=jax.ShapeDtypeStruct((NDEV * rp, d), main.dtype),
        grid=(d // LANE,),
        in_specs=[pl.BlockSpec((main.shape[0], LANE), lambda i: (0, i)),
                  pl.BlockSpec((LANE, LANE), lambda i: (0, i))],
        out_specs=pl.BlockSpec((NDEV * rp, LANE), lambda i: (0, i)),
        compiler_params=_params(("parallel",)),
    )(main, cut)


def _sum_slabs(name, a):
    s, r, c = a.shape

    def body(a_ref, o_ref):
        acc = a_ref[0].astype(F32)
        for i in range(1, s):
            acc = acc + a_ref[i].astype(F32)
        o_ref[...] = acc

    return pl.pallas_call(body, name=name, out_shape=jax.ShapeDtypeStruct((r, c), F32))(a)


def _adamw_update(g, w, m, v):
    mn = ADAM_B1 * m + (1.0 - ADAM_B1) * g
    vn = ADAM_B2 * v + (1.0 - ADAM_B2) * (g * g)
    m_hat = mn / (1.0 - ADAM_B1 ** ADAM_STEP)
    v_hat = vn / (1.0 - ADAM_B2 ** ADAM_STEP)
    delta = -ADAM_LR * (m_hat / (jnp.sqrt(v_hat) + ADAM_EPS) + ADAM_WD * w)
    return delta, mn, vn


def _transposed_shard(name, a, tr=256):
    r, c = a.shape
    cp = _round_up(c, ROW_ALIGN)
    tr = max(t for t in range(LANE, min(tr, r) + 1, LANE) if r % t == 0)

    def body(a_ref, o_ref):
        t = a_ref[...].T
        if cp > c:
            t = jnp.concatenate([t, jnp.zeros((cp - c, tr), t.dtype)], axis=0)
        o_ref[...] = t.astype(o_ref.dtype)

    return pl.pallas_call(
        body, name=name, out_shape=jax.ShapeDtypeStruct((cp, r), MXU_DTYPE),
        grid=(r // tr,),
        in_specs=[pl.BlockSpec((tr, c), lambda i: (i, 0))],
        out_specs=pl.BlockSpec((cp, tr), lambda i: (0, i)),
        compiler_params=_params(("parallel",)),
    )(a)


def _adamw(name, gsrc, w, m, v, transposed=False, tr=256):
    s = gsrc.shape[0]
    lead = w.ndim == 3
    r, c = w.shape[-2:]
    step = LANE if transposed else 8
    tr = max(t for t in range(step, min(tr, r) + 1, step) if r % t == 0)

    def body(g_ref, w_ref, m_ref, v_ref, go_ref, d_ref, mo_ref, vo_ref):
        g = g_ref[0].astype(F32)
        for i in range(1, s):
            g = g + g_ref[i].astype(F32)
        if transposed:
            g = g.T[:, :c]
        delta, mn, vn = _adamw_update(g, w_ref[...], m_ref[...], v_ref[...])
        go_ref[...] = g
        d_ref[...] = delta
        mo_ref[...] = mn
        vo_ref[...] = vn

    if lead:
        blk = pl.BlockSpec((None, tr, c), lambda i: (0, i, 0))
    else:
        blk = pl.BlockSpec((tr, c), lambda i: (i, 0))
    if transposed:
        g_spec = pl.BlockSpec((s, gsrc.shape[1], tr), lambda i: (0, 0, i))
    else:
        g_spec = pl.BlockSpec((s, tr, c), lambda i: (0, i, 0))
    return pl.pallas_call(
        body, name=name, out_shape=[jax.ShapeDtypeStruct(w.shape, F32)] * 4,
        grid=(r // tr,),
        in_specs=[g_spec, blk, blk, blk], out_specs=[blk] * 4,
        compiler_params=_params(("parallel",)),
    )(gsrc, w, m, v)


REPLICATED = ("ffn1_norm", "mix_norm", "q_norm", "k_norm", "conv_b", "dt_bias", "a_log",
              "d_skip", "ssd_norm", "ffn2_norm")
ALL_WEIGHTS = ("ffn1_norm", "ffn1_w_gate", "ffn1_w_up", "ffn1_w_down", "mix_norm", "w_in",
               "q_norm", "k_norm", "conv_w", "conv_b", "dt_bias", "a_log", "d_skip", "ssd_norm",
               "w_attn_branch", "w_ssd_branch", "w_out", "ffn2_norm", "ffn2_w_gate", "ffn2_w_up",
               "ffn2_w_down")
BIG = (("ffn1_w_gate", True, "gu1t", 0), ("ffn1_w_up", True, "gu1t", 1),
       ("ffn1_w_down", False, "d1", 0), ("w_in", True, "wint", 0),
       ("w_attn_branch", True, "abt", 0), ("w_ssd_branch", False, "sb", 0),
       ("w_out", False, "out", 0),
       ("ffn2_w_gate", True, "gu2t", 0), ("ffn2_w_up", True, "gu2t", 1),
       ("ffn2_w_down", False, "d2", 0))


def _nrows(shape, cols):
    return -(-math.prod(shape) // cols)


def _pack_rows(arrs, cols, row_tile):
    parts = []
    for a in arrs:
        flat = a.reshape(-1)
        nr = -(-flat.shape[0] // cols)
        parts.append(jnp.pad(flat, (0, nr * cols - flat.shape[0])).reshape(nr, cols))
    out = jnp.concatenate(parts, axis=0)
    return jnp.pad(out, ((0, _round_up(out.shape[0], row_tile) - out.shape[0]), (0, 0)))


def _unpack_rows(packed, shapes):
    cols = packed.shape[-1]
    out, r0 = [], 0
    for sh in shapes:
        nr = _nrows(sh, cols)
        out.append(packed[r0:r0 + nr].reshape(-1)[:math.prod(sh)].reshape(tuple(sh)))
        r0 += nr
    return out


def kernel(x, ffn1_norm, ffn1_w_gate, ffn1_w_up, ffn1_w_down, mix_norm, w_in, q_norm, k_norm, conv_w, conv_b, dt_bias, a_log, d_skip, ssd_norm, w_attn_branch, w_ssd_branch, w_out, ffn2_norm, ffn2_w_gate, ffn2_w_up, ffn2_w_down, loss_target, m_ffn1_norm, m_ffn1_w_gate, m_ffn1_w_up, m_ffn1_w_down, m_mix_norm, m_w_in, m_q_norm, m_k_norm, m_conv_w, m_conv_b, m_dt_bias, m_a_log, m_d_skip, m_ssd_norm, m_w_attn_branch, m_w_ssd_branch, m_w_out, m_ffn2_norm, m_ffn2_w_gate, m_ffn2_w_up, m_ffn2_w_down, v_ffn1_norm, v_ffn1_w_gate, v_ffn1_w_up, v_ffn1_w_down, v_mix_norm, v_w_in, v_q_norm, v_k_norm, v_conv_w, v_conv_b, v_dt_bias, v_a_log, v_d_skip, v_ssd_norm, v_w_attn_branch, v_w_ssd_branch, v_w_out, v_ffn2_norm, v_ffn2_w_gate, v_ffn2_w_up, v_ffn2_w_down):
    given = dict(locals())
    wts = {n: given[n] for n in ALL_WEIGHTS}
    mom = {n: given["m_" + n] for n in ALL_WEIGHTS}
    var = {n: given["v_" + n] for n in ALL_WEIGHTS}
    d = x.shape[-1]
    nh = dt_bias.shape[1]
    my = _dev_index(_mesh_pos())

    def row_form(n, col_sharded):
        if col_sharded:
            return _transposed_shard("transpose_" + n, wts[n][0])
        return wts[n][0].astype(MXU_DTYPE)

    _Order.tokens, _Order.last = [], None
    shard = {n: row_form(n, cs) for n, cs, _, _ in BIG}
    entries = {buf: [e for e in BIG if e[2] == buf] for buf in dict.fromkeys(e[2] for e in BIG)}

    def buf_shape(buf):
        r, c = shard[entries[buf][0][0]].shape
        return (len(entries[buf]) * NDEV * r, c)

    def gather_plan(bufs):
        srcs, lands, plan = [], [], []
        for li, buf in enumerate(bufs):
            lands.append(lax.empty(buf_shape(buf), MXU_DTYPE))
            for n, _, _, pos in entries[buf]:
                r = shard[n].shape[0]
                plan.append((len(srcs), 0, 0, li, pos * NDEV * r, r, r))
                srcs.append(shard[n])
        return srcs, lands, plan

    def scatter_plan(bufs, grads):
        srcs, lands, plan, names = [], [], [], []
        for si, buf in enumerate(bufs):
            srcs.append(grads[buf])
            for n, _, _, pos in entries[buf]:
                r, c = shard[n].shape
                plan.append((si, pos * NDEV * r, r, len(lands), 0, r, r))
                lands.append(lax.empty((NDEV * r, c), MXU_DTYPE))
                names.append(n)
        return srcs, lands, plan, names

    first_bufs = ("gu1t", "d1")
    shards, dests, out_shapes = [], [], []
    for bi, buf in enumerate(first_bufs):
        out_shapes.append((buf_shape(buf), MXU_DTYPE))
        for n, _, _, pos in entries[buf]:
            r = shard[n].shape[0]
            shards.append(shard[n])
            dests.append((bi, pos * NDEV * r, r))
    conv_rows = _pack_rows([conv_w[0]], LANE, ROW_ALIGN)
    shards.append(conv_rows)
    dests.append((len(first_bufs), 0, conv_rows.shape[0]))
    out_shapes.append(((NDEV * conv_rows.shape[0], LANE), F32))
    gathered = _gather("gather_first", shards, dests, out_shapes)

    in_cols = w_in.shape[2]
    in_pad = _round_up(in_cols, ROW_ALIGN)
    dt_off = NDEV * in_cols - 2 * d - nh
    second_bufs = ("wint",)
    third_bufs = ("abt", "sb", "out", "gu2t", "d2")
    plan2 = gather_plan(second_bufs)
    started2 = _split_start("gather_in_start", *plan2, after=[gathered[0]],
                            relations=SAME_CORE_AND_SIBLING)
    forwarded, started3 = [], []

    class Weights(dict):
        def after_first_ffn(self):
            lands = _split_wait("gather_in_wait", started2, plan2[2],
                                relations=SAME_CORE_AND_SIBLING)
            forwarded.append(_forward_start("gather_in_forward", lands, plan2[2]))

        def __missing__(self, key):
            if key in ("maint", "dtt"):
                wint = _forward_wait("gather_in_arrive", forwarded[0], plan2[2])[0]
                plan3 = gather_plan(third_bufs)
                started3.append((_split_start("gather_rest_start", *plan3, after=[wint]), plan3[2]))
                self["maint"], self["dtt"] = _regroup_rows(
                    "regroup_w_in", wint, in_cols, in_pad, dt_off, dt_off + nh)
            else:
                st, plan = started3[0]
                for buf, a in zip(third_bufs, _split_wait("gather_rest_wait", st, plan)):
                    self[buf] = a
            return self[key]

    w = Weights(gu1t=gathered[0], d1=gathered[1])
    p = {n: wts[n] for n in REPLICATED}
    conv_all = gathered[-1].reshape(NDEV, conv_rows.shape[0] * LANE)[:, :math.prod(conv_w.shape[1:])]
    p["conv_w"] = (conv_all.reshape((NDEV,) + conv_w.shape[1:]).transpose(1, 0, 2)
                   .reshape(conv_w.shape[1], NDEV * conv_w.shape[2]))

    groups = (("scatter_late", ("gu2t", "d2", "out", "abt", "sb")),
              ("scatter_in", ("maint", "dtt")),
              ("scatter_first", ("gu1t", "d1")))
    in_flight = []

    class Grads(dict):
        def __setitem__(self, key, value):
            dict.__setitem__(self, key, value)
            for tag, need in groups:
                if key in need and all(k in self for k in need):
                    if tag == "scatter_in":
                        gwin = _ungroup_rows("ungroup_w_in", self["maint"], self["dtt"],
                                             in_cols, in_pad, dt_off, dt_off + nh)
                        bufs, grads = ("wint",), {"wint": gwin}
                    else:
                        bufs, grads = need, self
                    srcs, lands, plan, names = scatter_plan(bufs, grads)
                    in_flight.append((tag, _split_start(tag + "_start", srcs, lands, plan),
                                      plan, names))

    loss_vec, dx, gw, gp = _local_step(x[0], loss_target[0], w, p, Grads(),
                                       aw=w_attn_branch.shape[1])

    small_names = REPLICATED + ("conv_w",)
    small_shapes = [gp[n].shape for n in small_names]
    small = _pack_rows([gp[n] for n in small_names], LANE, 8)
    small_all = _exchange("exchange_small", [], [], small)[0]

    outs = [{}, {}, {}, {}]
    col_sharded_of = {n: cs for n, cs, _, _ in BIG}
    for tag, started, plan, names in in_flight:
        for n, rv in zip(names, _split_wait(tag + "_wait", started, plan)):
            rv = rv.reshape(NDEV, shard[n].shape[0], shard[n].shape[1])
            res = _Order.done(_adamw("adamw_" + n, rv, wts[n][0], mom[n][0], var[n][0],
                                     transposed=col_sharded_of[n]))
            for k in range(4):
                outs[k][n] = res[k][None]

    small_g = _unpack_rows(_sum_slabs("sum_small_grads", small_all), small_shapes)
    small_g = dict(zip(small_names, small_g))
    cs = conv_w.shape[2]
    small_g["conv_w"] = lax.dynamic_slice_in_dim(small_g["conv_w"], my * cs, cs, axis=1)
    small_shard_shapes = [wts[n].shape[-2:] for n in small_names]
    sg = _pack_rows([small_g[n] for n in small_names], LANE, 8)
    sw = _pack_rows([wts[n] for n in small_names], LANE, 8)
    sm = _pack_rows([mom[n] for n in small_names], LANE, 8)
    sv = _pack_rows([var[n] for n in small_names], LANE, 8)
    res_small = _adamw("adamw_small", sg[None], sw, sm, sv, tr=sg.shape[0])
    for k in range(4):
        for n, a in zip(small_names, _unpack_rows(res_small[k], small_shard_shapes)):
            outs[k][n] = a.reshape(wts[n].shape)

    loss = lax.psum(0.5 * jnp.sum(loss_vec) / d, ("x", "y", "c"))
    result = [loss, dx[None]]
    for k in range(4):
        result += [outs[k][n] for n in ALL_WEIGHTS]
    return tuple(result)
```

```python
import functools
import math

import numpy as np
import jax
import jax.numpy as jnp
from jax import lax
from jax.experimental import pallas as pl
from jax.experimental.pallas import tpu as pltpu

F32 = jnp.float32
BF16 = jnp.bfloat16
MXU_DTYPE = BF16
ACT_DTYPE = BF16

NDEV = 8
EPS = 1e-6
HD = 64
QB = 128
PATTERNS = ((128, 1), (512, 4), (2048, 16))
ALIBI_MAX_EXP = 8.0
SSD_P = 64
SSD_N = 128
SSD_G = 4
SSD_Q = 128
SSD_K = 4
NEG = -1e30
LANE = 128
ROW_ALIGN = 16
VMEM_LIMIT = 56 * 1024 * 1024

ADAM_LR, ADAM_B1, ADAM_B2, ADAM_EPS, ADAM_WD, ADAM_STEP = 0.001, 0.9, 0.999, 1e-8, 0.01, 10

NN = (((1,), (0,)), ((), ()))
NT = (((1,), (1,)), ((), ()))
TN = (((0,), (0,)), ((), ()))


def _dot(a, b, dims=NN):
    return lax.dot_general(a.astype(MXU_DTYPE), b.astype(MXU_DTYPE), dims,
                           preferred_element_type=F32)


def _split3(a):
    hi = a.astype(BF16)
    r = a - hi.astype(F32)
    mid = r.astype(BF16)
    lo = (r - mid.astype(F32)).astype(BF16)
    return hi, mid, lo


def _dot3(a, b, dims=NN, split=0):
    if split == 0:
        bb = b.astype(BF16)
        parts = [lax.dot_general(s, bb, dims, preferred_element_type=F32) for s in _split3(a)]
    else:
        aa = a.astype(BF16)
        parts = [lax.dot_general(aa, s, dims, preferred_element_type=F32) for s in _split3(b)]
    return parts[0] + parts[1] + parts[2]


@jax.custom_vjp
def _spread(v, e):
    return _dot3(v, e)


def _spread_fwd(v, e):
    return _dot3(v, e), e


def _spread_bwd(e, g):
    return _dot3(g, e, NT), jnp.zeros_like(e)


_spread.defvjp(_spread_fwd, _spread_bwd)


@jax.custom_vjp
def _running_sum(a, lower):
    return _dot3(lower, a, NN, split=1)


def _running_sum_fwd(a, lower):
    return _dot3(lower, a, NN, split=1), lower


def _running_sum_bwd(lower, g):
    return _dot3(lower, g, TN, split=1), jnp.zeros_like(lower)


_running_sum.defvjp(_running_sum_fwd, _running_sum_bwd)


def _tile(n, cap):
    if n <= cap:
        return n
    best = None
    for t in range(LANE, cap + 1, LANE):
        if n % t == 0:
            best = t
    assert best is not None, (n, cap)
    return best


def _params(sem):
    return pltpu.CompilerParams(dimension_semantics=sem, vmem_limit_bytes=VMEM_LIMIT)


def _round_up(n, m):
    return -(-n // m) * m


class _Order:
    tokens = []
    last = None

    @classmethod
    def take(cls):
        out, cls.tokens = cls.tokens, []
        return out

    @classmethod
    def done(cls, result):
        cls.last = result[0] if isinstance(result, (list, tuple)) else result
        return result


ANY_SPEC = pl.BlockSpec(memory_space=pl.ANY)


def _mm(name, a, b, mode, out_dtype=F32, res=None, scale=1.0,
        cap_m=1408, cap_n=1408, cap_k=1408):
    segs = list(a) if isinstance(a, (list, tuple)) else [a]
    nseg = len(segs)
    if mode == "tn":
        k = segs[0].shape[0]
        widths = [s.shape[1] for s in segs]
        m = sum(widths)
        k2, n = b.shape
        tm = _tile(math.gcd(*widths), cap_m)
        tk = _tile(k, cap_k)
        counts = [wd // tm for wd in widths]
    else:
        m = segs[0].shape[0]
        widths = [s.shape[1] for s in segs]
        k = sum(widths)
        (k2, n) = b.shape if mode == "nn" else b.shape[::-1]
        tm = _tile(m, cap_m)
        tk = _tile(math.gcd(*widths), cap_k)
        counts = [wd // tk for wd in widths]
    assert k == k2, (name, [s.shape for s in segs], b.shape, mode)
    tn = _tile(n, cap_n)
    nk = k // tk
    starts = [sum(counts[:s]) for s in range(nseg)]
    dims = {"nn": NN, "nt": NT, "tn": TN}[mode]

    def a_spec(s):
        lo, cnt = starts[s], counts[s]
        if mode == "tn":
            if nseg == 1:
                return pl.BlockSpec((tk, tm), lambda i, j, kk: (kk, i))
            return pl.BlockSpec(
                (tk, tm), lambda i, j, kk: (jnp.where((i >= lo) & (i < lo + cnt), kk, 0),
                                            jnp.clip(i - lo, 0, cnt - 1)))
        if nseg == 1:
            return pl.BlockSpec((tm, tk), lambda i, j, kk: (i, kk))
        return pl.BlockSpec((tm, tk), lambda i, j, kk: (i, jnp.clip(kk - lo, 0, cnt - 1)))

    b_spec = (pl.BlockSpec((tn, tk), lambda i, j, kk: (j, kk)) if mode == "nt"
              else pl.BlockSpec((tk, tn), lambda i, j, kk: (kk, j)))
    o_spec = pl.BlockSpec((tm, tn), lambda i, j, kk: (i, j))
    has_res = res is not None
    use_acc = nk > 1 or nseg > 1
    ties = _Order.take()
    nt_ = len(ties)

    def body(*refs):
        a_refs = refs[:nseg]
        b_ref = refs[nseg]
        r_ref = refs[nseg + 1] if has_res else None
        o_ref = refs[nseg + 1 + has_res + nt_]
        scr = refs[nseg + 2 + has_res + nt_:]

        def finish(acc):
            if scale != 1.0:
                acc = acc * scale
            if has_res:
                acc = r_ref[...].astype(F32) + acc
            o_ref[...] = acc.astype(o_ref.dtype)

        if not use_acc:
            finish(_dot(a_refs[0][...], b_ref[...], dims))
            return
        acc_ref = scr[0]
        kk = pl.program_id(2)
        sel = pl.program_id(0) if mode == "tn" else kk

        @pl.when(kk == 0)
        def _():
            acc_ref[...] = jnp.zeros_like(acc_ref)

        for s in range(nseg):
            def add(s=s):
                acc_ref[...] += _dot(a_refs[s][...], b_ref[...], dims)
            if nseg == 1:
                add()
            else:
                pl.when((sel >= starts[s]) & (sel < starts[s] + counts[s]))(add)

        @pl.when(kk == nk - 1)
        def _():
            finish(acc_ref[...])

    in_specs = ([a_spec(s) for s in range(nseg)] + [b_spec] + ([o_spec] if has_res else [])
                + [ANY_SPEC] * nt_)
    args = tuple(segs) + (b,) + ((res,) if has_res else ()) + tuple(ties)
    return _Order.done(pl.pallas_call(
        body, name=name,
        out_shape=jax.ShapeDtypeStruct((m, n), out_dtype),
        grid=(m // tm, n // tn, nk),
        in_specs=in_specs, out_specs=o_spec,
        scratch_shapes=[pltpu.VMEM((tm, tn), F32)] if use_acc else [],
        compiler_params=_params(("parallel", "parallel", "arbitrary")),
    )(*args))


def _act(g, u):
    return _silu(g.astype(F32)) * u.astype(F32)


def _ffn_up(name, h, wgut, cap_m=512, cap_n=1408):
    m, k = h.shape
    dff = wgut.shape[0] // 2
    tm, tn = _tile(m, cap_m), _tile(dff, cap_n)
    nj = dff // tn
    ties = _Order.take()

    def body(h_ref, wg_ref, wu_ref, *rest):
        g_ref, u_ref, a_ref = rest[len(ties):]
        hv = h_ref[...]
        g = _dot(hv, wg_ref[...], NT)
        u = _dot(hv, wu_ref[...], NT)
        g_ref[...] = g.astype(g_ref.dtype)
        u_ref[...] = u.astype(u_ref.dtype)
        a_ref[...] = _act(g, u).astype(a_ref.dtype)

    o_spec = pl.BlockSpec((tm, tn), lambda i, j: (i, j))
    return _Order.done(pl.pallas_call(
        body, name=name, out_shape=[jax.ShapeDtypeStruct((m, dff), ACT_DTYPE)] * 3,
        grid=(m // tm, nj),
        in_specs=[pl.BlockSpec((tm, k), lambda i, j: (i, 0)),
                  pl.BlockSpec((tn, k), lambda i, j: (j, 0)),
                  pl.BlockSpec((tn, k), lambda i, j: (nj + j, 0))] + [ANY_SPEC] * len(ties),
        out_specs=[o_spec] * 3,
        compiler_params=_params(("parallel", "parallel")),
    )(h, wgut, wgut, *ties))


def _ffn_dact(name, dxo, wd, g, u, scale, cap_m=512, cap_n=1408):
    m, k = dxo.shape
    dff = wd.shape[0]
    tm, tn = _tile(m, cap_m), _tile(dff, cap_n)
    ties = _Order.take()

    def body(d_ref, w_ref, g_ref, u_ref, *rest):
        dg_ref, du_ref = rest[len(ties):]
        da = _dot(d_ref[...], w_ref[...], NT) * scale
        _, vjp = jax.vjp(_act, g_ref[...], u_ref[...])
        dg, du = vjp(da)
        dg_ref[...] = dg.astype(dg_ref.dtype)
        du_ref[...] = du.astype(du_ref.dtype)

    o_spec = pl.BlockSpec((tm, tn), lambda i, j: (i, j))
    return _Order.done(pl.pallas_call(
        body, name=name, out_shape=[jax.ShapeDtypeStruct((m, dff), ACT_DTYPE)] * 2,
        grid=(m // tm, dff // tn),
        in_specs=[pl.BlockSpec((tm, k), lambda i, j: (i, 0)),
                  pl.BlockSpec((tn, k), lambda i, j: (j, 0)), o_spec, o_spec]
        + [ANY_SPEC] * len(ties),
        out_specs=[o_spec] * 2,
        compiler_params=_params(("parallel", "parallel")),
    )(dxo, wd, g, u, *ties))


def _rw(name, fn, ins, outs, accs=(), tr=256, ncb=1):
    t = next(a.shape[0] for kind, a, _, _ in ins if kind == "row")
    assert t % tr == 0
    n_in = len(ins)
    n_pieces = sum(len(w) for w, _ in outs)

    def spec(kind, arr, width, base):
        if kind == "row":
            return pl.BlockSpec((tr, width), lambda j, i: (i, base + j))
        return pl.BlockSpec((arr.shape[0], width), lambda j, i: (0, base + j))

    in_specs = [spec(*s) for s in ins]
    out_shapes, out_specs = [], []
    for widths, dt in outs:
        w = sum(widths)
        out_shapes.append(jax.ShapeDtypeStruct((t, w * ncb), dt))
        out_specs.append(pl.BlockSpec((tr, w), lambda j, i: (i, j)))
    for rows, width in accs:
        out_shapes.append(jax.ShapeDtypeStruct((rows, width * ncb), F32))
        out_specs.append(pl.BlockSpec((rows, width), lambda j, i: (0, j)))

    ties = _Order.take()
    nt_ = len(ties)
    in_specs = in_specs + [ANY_SPEC] * nt_

    def body(*refs):
        vals = [r[...] for r in refs[:n_in]]
        res = fn(*vals)
        o_refs = refs[n_in + nt_:n_in + nt_ + len(outs)]
        a_refs = refs[n_in + nt_ + len(outs):]
        p = 0
        for (widths, _), o_ref in zip(outs, o_refs):
            off = 0
            for w in widths:
                if len(widths) == 1:
                    o_ref[...] = res[p].astype(o_ref.dtype)
                else:
                    o_ref[:, off:off + w] = res[p].astype(o_ref.dtype)
                off += w
                p += 1
        i = pl.program_id(1)
        for a_ref, v in zip(a_refs, res[n_pieces:]):
            @pl.when(i == 0)
            def _(a_ref=a_ref, v=v):
                a_ref[...] = v

            @pl.when(i > 0)
            def _(a_ref=a_ref, v=v):
                a_ref[...] += v

    return _Order.done(pl.pallas_call(
        body, name=name, out_shape=out_shapes,
        grid=(ncb, t // tr), in_specs=in_specs, out_specs=out_specs,
        compiler_params=_params(("parallel", "arbitrary")),
    )(*[a for _, a, _, _ in ins], *ties))


def _rms(x, g):
    x = x.astype(F32)
    return x * lax.rsqrt(jnp.mean(x * x, axis=-1, keepdims=True) + EPS) * g


def _silu(x):
    return x * jax.nn.sigmoid(x)


def _colsum(v):
    return jnp.sum(v, axis=0, keepdims=True)


def _pair_norm(x, g):
    w = 2 * HD
    ri = lax.broadcasted_iota(jnp.int32, (w, w), 0)
    ci = lax.broadcasted_iota(jnp.int32, (w, w), 1)
    same_head = ((ri < HD) == (ci < HD)).astype(F32)
    ms = _spread(x * x, same_head) * (1.0 / HD)
    return x * lax.rsqrt(ms + EPS) * g


ATTN_SCALE = 1.0 / math.sqrt(HD)


def _attn_bias(coef):
    key = lax.broadcasted_iota(jnp.int32, (QB, QB), 0)
    qry = lax.broadcasted_iota(jnp.int32, (QB, QB), 1)
    dist = (qry - key).astype(F32)
    own = jnp.where(qry >= key, -coef * dist, NEG)
    prev = jnp.where(qry <= key, -coef * (dist + float(QB)), NEG)
    return own, prev


BNT = (((2,), (2,)), ((0,), (0,)))
BTN = (((1,), (1,)), ((0,), (0,)))


def _attn_pair(qn, kcn, kpn, vc, vp, b_own, b_prev):
    nb = qn.shape[0]
    w = 2 * HD
    lane = lax.broadcasted_iota(jnp.int32, (1, 1, w), 2)
    eye = (lax.broadcasted_iota(jnp.int32, (QB, QB), 0)
           == lax.broadcasted_iota(jnp.int32, (QB, QB), 1)).astype(F32)
    out = jnp.zeros((nb, QB, w), F32)
    lb = jnp.zeros((nb * QB, w), F32)
    for hh in range(2):
        mask = ((lane < HD) if hh == 0 else (lane >= HD)).astype(F32)
        qm = qn * mask
        lc = _dot(kcn, qm, BNT) + b_own[hh]
        lp = _dot(kpn, qm, BNT) + b_prev[hh]
        m = lax.stop_gradient(jnp.maximum(jnp.max(lc, axis=1, keepdims=True),
                                          jnp.max(lp, axis=1, keepdims=True)))
        pc = jnp.exp(lc - m)
        pp = jnp.exp(lp - m)
        l = jnp.sum(pc, axis=1, keepdims=True) + jnp.sum(pp, axis=1, keepdims=True)
        inv = 1.0 / l
        out = out + (_dot(pc * inv, vc, BTN) + _dot(pp * inv, vp, BTN)) * mask
        diag = (eye * (m + jnp.log(l))).reshape(nb * QB, QB)
        lb = lb + _spread(diag, jnp.broadcast_to(mask[0], (QB, w)))
    return out, lb.reshape(nb, QB, w)


NORM_ROWS = 128
NORM_UNROLL = 4
EPILOGUE_ROWS = 512
ATTN_BATCH_FWD = 8
ATTN_BATCH_BWD = 4


def _unit_rows(u, d):
    r = u & (d - 1)
    n = u >> (d.bit_length() - 1)

    def rows(blk):
        start = pl.multiple_of(blk * (QB * d), QB * d)
        return pl.ds(start, QB) if d == 1 else pl.ds(start + r, QB, stride=d)

    return rows(n), rows(jnp.maximum(n - 1, 0)), n == 0


def _unit_batch(i, nbatch, d, bias, qf, kf, vf):
    units = [_unit_rows(i * nbatch + j, d) for j in range(nbatch)]
    cur = lambda ref: jnp.stack([ref[c, :] for c, _, _ in units])
    prv = lambda ref: jnp.stack([ref[p, :] for _, p, _ in units])
    b_own = [b[0] for b in bias]
    b_prev = [jnp.stack([jnp.where(first, NEG, b[1]) for _, _, first in units]) for b in bias]
    return units, (cur(qf), cur(kf), prv(kf), cur(vf), prv(vf), b_own, b_prev)


def _q_norm(x, g):
    return _pair_norm(x, g * ATTN_SCALE)


def _attn_prologue(t, q_ref, k_ref, v_ref, qg_ref, kg_ref, qf, kf, vf):
    def chunk(c, carry):
        rows = pl.ds(pl.multiple_of(c * NORM_ROWS, NORM_ROWS), NORM_ROWS)
        qf[rows, :] = _q_norm(q_ref[rows, :].astype(F32), qg_ref[...])
        kf[rows, :] = _pair_norm(k_ref[rows, :].astype(F32), kg_ref[...])
        vf[rows, :] = v_ref[rows, :].astype(F32)
        return carry
    lax.fori_loop(0, t // NORM_ROWS, chunk, 0, unroll=NORM_UNROLL)


def _attn_specs(t, bases):
    w = 2 * HD
    ins = [pl.BlockSpec((t, w), functools.partial(lambda p, c, b: (0, b + p), b=b)) for b in bases]
    gain = pl.BlockSpec((1, w), lambda p, c: (0, 0))
    blk = pl.BlockSpec((t, w), lambda p, c: (0, p))
    return ins, gain, blk


def _attn_fwd(name, proj, bases, qg, kg, coefs, d):
    t = proj.shape[0]
    npairs = coefs.shape[0] // 2
    w = 2 * HD
    ins, gain, blk = _attn_specs(t, bases)

    def body(coef_ref, q_ref, k_ref, v_ref, qg_ref, kg_ref, o_ref, l_ref, qf, kf, vf):
        p = pl.program_id(0)
        bias = (_attn_bias(coef_ref[2 * p]), _attn_bias(coef_ref[2 * p + 1]))
        _attn_prologue(t, q_ref, k_ref, v_ref, qg_ref, kg_ref, qf, kf, vf)

        def step(i, carry):
            units, ins = _unit_batch(i, ATTN_BATCH_FWD, d, bias, qf, kf, vf)
            o, lb = _attn_pair(*ins)
            for j, (cur, _, _) in enumerate(units):
                o_ref[cur, :] = o[j]
                l_ref[cur, :] = lb[j]
            return carry

        lax.fori_loop(0, t // QB // ATTN_BATCH_FWD, step, 0)

    return pl.pallas_call(
        body, name=name,
        out_shape=[jax.ShapeDtypeStruct((t, npairs * w), F32)] * 2,
        grid_spec=pltpu.PrefetchScalarGridSpec(
            num_scalar_prefetch=1, grid=(npairs,),
            in_specs=ins + [gain, gain], out_specs=[blk, blk],
            scratch_shapes=[pltpu.VMEM((t, w), F32)] * 3),
        compiler_params=_params(("arbitrary",)),
    )(coefs, proj, proj, proj, qg, kg)


def _attn_bwd(name, proj, bases, qg, kg, coefs, d, do, dl):
    t = proj.shape[0]
    npairs = coefs.shape[0] // 2
    w = 2 * HD
    ins, gain, blk = _attn_specs(t, bases)

    def body(coef_ref, q_ref, k_ref, v_ref, qg_ref, kg_ref, do_ref, dl_ref,
             dq_ref, dk_ref, dv_ref, dqg_ref, dkg_ref, qf, kf, vf, dqf, dkf, dvf):
        p = pl.program_id(0)
        bias = (_attn_bias(coef_ref[2 * p]), _attn_bias(coef_ref[2 * p + 1]))
        _attn_prologue(t, q_ref, k_ref, v_ref, qg_ref, kg_ref, qf, kf, vf)
        dkf[...] = jnp.zeros_like(dkf)
        dvf[...] = jnp.zeros_like(dvf)

        def step(i, carry):
            units, ins = _unit_batch(i, ATTN_BATCH_BWD, d, bias, qf, kf, vf)
            f = lambda a, b, c, e, g: _attn_pair(a, b, c, e, g, *ins[5:])
            _, vjp = jax.vjp(f, *ins[:5])
            cot = (jnp.stack([do_ref[cur, :] for cur, _, _ in units]),
                   jnp.stack([dl_ref[cur, :] for cur, _, _ in units]))
            dq, dkc, dkp, dvc, dvp = vjp(cot)
            for j, (cur, prv, _) in enumerate(units):
                dqf[cur, :] = dq[j]
                dkf[cur, :] += dkc[j]
                dkf[prv, :] += dkp[j]
                dvf[cur, :] += dvc[j]
                dvf[prv, :] += dvp[j]
            return carry

        lax.fori_loop(0, t // QB // ATTN_BATCH_BWD, step, 0)

        def chunk(c, carry):
            dqg_acc, dkg_acc = carry
            rows = pl.ds(pl.multiple_of(c * EPILOGUE_ROWS, EPILOGUE_ROWS), EPILOGUE_ROWS)
            _, vq = jax.vjp(_q_norm, q_ref[rows, :].astype(F32), qg_ref[...])
            dq, dqg = vq(dqf[rows, :])
            _, vk = jax.vjp(_pair_norm, k_ref[rows, :].astype(F32), kg_ref[...])
            dk, dkg = vk(dkf[rows, :])
            dq_ref[rows, :] = dq.astype(dq_ref.dtype)
            dk_ref[rows, :] = dk.astype(dk_ref.dtype)
            dv_ref[rows, :] = dvf[rows, :].astype(dv_ref.dtype)
            return dqg_acc + dqg, dkg_acc + dkg

        zero = jnp.zeros((1, w), F32)
        dqg, dkg = lax.fori_loop(0, t // EPILOGUE_ROWS, chunk, (zero, zero))

        @pl.when(p == 0)
        def _():
            dqg_ref[...] = dqg
            dkg_ref[...] = dkg

        @pl.when(p > 0)
        def _():
            dqg_ref[...] += dqg
            dkg_ref[...] += dkg

    big = jax.ShapeDtypeStruct((t, npairs * w), ACT_DTYPE)
    small = jax.ShapeDtypeStruct((1, w), F32)
    return pl.pallas_call(
        body, name=name,
        out_shape=[big, big, big, small, small],
        grid_spec=pltpu.PrefetchScalarGridSpec(
            num_scalar_prefetch=1, grid=(npairs,),
            in_specs=ins + [gain, gain, blk, blk],
            out_specs=[blk, blk, blk, gain, gain],
            scratch_shapes=[pltpu.VMEM((t, w), F32)] * 6),
        compiler_params=_params(("arbitrary",)),
    )(coefs, proj, proj, proj, qg, kg, do, dl)


def _shift_down(u, s):
    if s == 0:
        return u
    rows = lax.broadcasted_iota(jnp.int32, u.shape, 0)
    return jnp.where(rows >= s, pltpu.roll(u, s, 0), 0.0)


def _shift_up(u, s):
    if s == 0:
        return u
    t = u.shape[0]
    rows = lax.broadcasted_iota(jnp.int32, u.shape, 0)
    return jnp.where(rows < t - s, pltpu.roll(u, t - s, 0), 0.0)


def _conv_pre(u, w, b):
    y = b
    for kk in range(SSD_K):
        y = y + w[kk:kk + 1, :] * _shift_down(u, SSD_K - 1 - kk)
    return y


def _conv_fwd(name, src, base, w, b, cw=128):
    t = src.shape[0]
    c = w.shape[1]

    def body(u_ref, w_ref, b_ref, o_ref):
        y = _conv_pre(u_ref[...].astype(F32), w_ref[...], b_ref[...])
        o_ref[...] = _silu(y).astype(o_ref.dtype)

    return pl.pallas_call(
        body, name=name, out_shape=jax.ShapeDtypeStruct((t, c), ACT_DTYPE),
        grid=(c // cw,),
        in_specs=[pl.BlockSpec((t, cw), lambda j: (0, base + j)),
                  pl.BlockSpec((SSD_K, cw), lambda j: (0, j)),
                  pl.BlockSpec((1, cw), lambda j: (0, j))],
        out_specs=pl.BlockSpec((t, cw), lambda j: (0, j)),
        compiler_params=_params(("parallel",)),
    )(src, w, b)


def _conv_bwd(name, src, base, w, b, dout, cw=128):
    t = src.shape[0]
    c = w.shape[1]

    def body(u_ref, w_ref, b_ref, d_ref, du_ref, dw_ref, db_ref):
        u = u_ref[...].astype(F32)
        wv = w_ref[...]
        y = _conv_pre(u, wv, b_ref[...])
        sg = jax.nn.sigmoid(y)
        dy = d_ref[...].astype(F32) * (sg * (1.0 + y * (1.0 - sg)))
        du = jnp.zeros_like(u)
        for kk in range(SSD_K):
            s = SSD_K - 1 - kk
            du = du + wv[kk:kk + 1, :] * _shift_up(dy, s)
            dw_ref[kk:kk + 1, :] = _colsum(dy * _shift_down(u, s))
        du_ref[...] = du.astype(du_ref.dtype)
        db_ref[...] = _colsum(dy)

    return pl.pallas_call(
        body, name=name,
        out_shape=[jax.ShapeDtypeStruct((t, c), ACT_DTYPE),
                   jax.ShapeDtypeStruct((SSD_K, c), F32),
                   jax.ShapeDtypeStruct((1, c), F32)],
        grid=(c // cw,),
        in_specs=[pl.BlockSpec((t, cw), lambda j: (0, base + j)),
                  pl.BlockSpec((SSD_K, cw), lambda j: (0, j)),
                  pl.BlockSpec((1, cw), lambda j: (0, j)),
                  pl.BlockSpec((t, cw), lambda j: (0, j))],
        out_specs=[pl.BlockSpec((t, cw), lambda j: (0, j)),
                   pl.BlockSpec((SSD_K, cw), lambda j: (0, j)),
                   pl.BlockSpec((1, cw), lambda j: (0, j))],
        compiler_params=_params(("parallel",)),
    )(src, w, b, dout)


def _softplus(x):
    return jnp.maximum(x, 0.0) + jnp.log(1.0 + jnp.exp(-jnp.abs(x)))


def _ssd_chunk(xbc, dtraw, bias, alog, states):
    wd = states[0].shape[1]
    nj = wd // SSD_P
    inner = SSD_G * wd
    dt = _softplus(dtraw + bias)
    a = dt * (-jnp.exp(alog))
    li = lax.broadcasted_iota(jnp.int32, (SSD_Q, SSD_Q), 0)
    si = lax.broadcasted_iota(jnp.int32, (SSD_Q, SSD_Q), 1)
    causal = li >= si
    acs = _running_sum(a, causal.astype(F32))
    acs_t = acs.T
    a_last = acs[SSD_Q - 1:SSD_Q, :]
    grow = jnp.exp(acs)
    shrink = jnp.exp(a_last - acs)
    hrow = lax.broadcasted_iota(jnp.int32, (LANE, wd), 0)
    wcol = lax.broadcasted_iota(jnp.int32, (LANE, wd), 1)
    lane = lax.broadcasted_iota(jnp.int32, (1, LANE), 1)
    ys, snext = [], []
    for g in range(SSD_G):
        lo = (hrow - g * nj) * SSD_P
        head_lanes = jnp.logical_and(wcol >= lo, wcol < lo + SSD_P).astype(F32)
        xs = xbc[:, g * wd:(g + 1) * wd]
        bm = xbc[:, inner + g * SSD_N:inner + (g + 1) * SSD_N]
        cm = xbc[:, inner + (SSD_G + g) * SSD_N:inner + (SSD_G + g + 1) * SSD_N]
        xdt = xs * _dot(dt, head_lanes)
        grow_x = _spread(grow, head_lanes)
        y_off = _dot(cm, states[g]) * grow_x
        s_new = (states[g] * grow_x[SSD_Q - 1:SSD_Q, :]
                 + _dot(bm, xdt * _dot(shrink, head_lanes), TN))
        cb = _dot(cm, bm, NT)
        pieces = []
        for i in range(wd // LANE):
            xp = xdt[:, i * LANE:(i + 1) * LANE]
            acc = jnp.zeros((SSD_Q, LANE), F32)
            for hh in range(LANE // SSD_P):
                h = g * nj + i * (LANE // SSD_P) + hh
                decay = jnp.exp(jnp.where(causal, acs[:, h:h + 1] - acs_t[h:h + 1, :], NEG))
                keep = jnp.logical_and(lane >= hh * SSD_P, lane < (hh + 1) * SSD_P).astype(F32)
                acc = acc + _dot(cb * decay, xp * keep)
            pieces.append(acc)
        y_diag = pieces[0] if len(pieces) == 1 else jnp.concatenate(pieces, axis=1)
        ys.append(y_diag + y_off)
        snext.append(s_new)
    return ys, snext


def _ssd_specs(cdim, wd, rev, nc):
    ch = (lambda c: nc - 1 - c) if rev else (lambda c: c)
    full = lambda width: pl.BlockSpec((SSD_Q, width), lambda c: (ch(c), 0))
    vec = pl.BlockSpec((1, LANE), lambda c: (0, 0))
    st = pl.BlockSpec((1, SSD_G, SSD_N, wd), lambda c: (ch(c), 0, 0, 0))
    return full, vec, st


def _ssd_fwd(name, xbc, dtraw, bias, alog, inner):
    t, cdim = xbc.shape
    wd = inner // SSD_G
    nc = t // SSD_Q
    full, vec, st = _ssd_specs(cdim, wd, False, nc)

    def body(x_ref, r_ref, b_ref, a_ref, y_ref, st_ref, s_scr):
        @pl.when(pl.program_id(0) == 0)
        def _():
            s_scr[...] = jnp.zeros_like(s_scr)

        sprev = [s_scr[g] for g in range(SSD_G)]
        ys, snext = _ssd_chunk(x_ref[...].astype(F32), r_ref[...], b_ref[...], a_ref[...], sprev)
        for g in range(SSD_G):
            st_ref[0, g] = sprev[g]
            y_ref[:, g * wd:(g + 1) * wd] = ys[g]
            s_scr[g] = snext[g]

    return pl.pallas_call(
        body, name=name,
        out_shape=[jax.ShapeDtypeStruct((t, inner), F32),
                   jax.ShapeDtypeStruct((nc, SSD_G, SSD_N, wd), F32)],
        grid=(nc,),
        in_specs=[full(cdim), full(LANE), vec, vec],
        out_specs=[full(inner), st],
        scratch_shapes=[pltpu.VMEM((SSD_G, SSD_N, wd), F32)],
        compiler_params=_params(("arbitrary",)),
    )(xbc, dtraw, bias, alog)


def _ssd_bwd(name, xbc, dtraw, bias, alog, states, dy, dxs_extra):
    t, cdim = xbc.shape
    inner = dy.shape[1]
    wd = inner // SSD_G
    nc = t // SSD_Q
    full, vec, st = _ssd_specs(cdim, wd, True, nc)

    def body(x_ref, r_ref, b_ref, a_ref, st_ref, dy_ref, dx0_ref,
             dx_ref, dr_ref, db_ref, da_ref, ds_scr):
        first = pl.program_id(0) == 0

        @pl.when(first)
        def _():
            ds_scr[...] = jnp.zeros_like(ds_scr)

        sprev = [st_ref[0, g] for g in range(SSD_G)]
        _, vjp = jax.vjp(_ssd_chunk, x_ref[...].astype(F32), r_ref[...], b_ref[...], a_ref[...],
                         sprev)
        dyv = dy_ref[...]
        dys = [dyv[:, g * wd:(g + 1) * wd] for g in range(SSD_G)]
        dsn = [ds_scr[g] for g in range(SSD_G)]
        dx, dr, db, da, dsp = vjp((dys, dsn))
        dx_ref[:, :inner] = dx[:, :inner] + dx0_ref[...].astype(F32)
        dx_ref[:, inner:] = dx[:, inner:]
        dr_ref[...] = dr
        for g in range(SSD_G):
            ds_scr[g] = dsp[g]

        @pl.when(first)
        def _():
            db_ref[...] = db
            da_ref[...] = da

        @pl.when(jnp.logical_not(first))
        def _():
            db_ref[...] += db
            da_ref[...] += da

    return pl.pallas_call(
        body, name=name,
        out_shape=[jax.ShapeDtypeStruct((t, cdim), F32),
                   jax.ShapeDtypeStruct((t, LANE), F32),
                   jax.ShapeDtypeStruct((1, LANE), F32),
                   jax.ShapeDtypeStruct((1, LANE), F32)],
        grid=(nc,),
        in_specs=[full(cdim), full(LANE), vec, vec, st, full(inner), full(inner)],
        out_specs=[full(cdim), full(LANE), vec, vec],
        scratch_shapes=[pltpu.VMEM((SSD_G, SSD_N, wd), F32)],
        compiler_params=_params(("arbitrary",)),
    )(xbc, dtraw, bias, alog, states, dy, dxs_extra)


def _mix(o0, o1, o2, l0, l1, l2):
    m = lax.stop_gradient(jnp.maximum(jnp.maximum(l0, l1), l2))
    e0, e1, e2 = jnp.exp(l0 - m), jnp.exp(l1 - m), jnp.exp(l2 - m)
    return (e0 * o0 + e1 * o1 + e2 * o2) / (e0 + e1 + e2)


def _gate(y, xs, z, dexp, gain):
    v = (y + xs.astype(F32) * dexp) * _silu(z.astype(F32))
    return _rms(v, gain)


def _merge(ga, gs, ap, sp):
    return jax.nn.sigmoid(ga.astype(F32)) * ap + jax.nn.sigmoid(gs.astype(F32)) * sp


def _alibi_coefs(hp):
    n = hp * len(PATTERNS)
    slopes = np.exp2(-ALIBI_MAX_EXP * np.arange(1, n + 1, dtype=np.float32) / n).astype(np.float32)
    return [jnp.asarray(slopes[g * hp:(g + 1) * hp] * np.float32(d))
            for g, (_, d) in enumerate(PATTERNS)]


def _local_step(x, tgt, w, p, gw_=None, aw=None):
    t, d = x.shape
    dff = w["gu1t"].shape[0] // 2
    aw = w["abt"].shape[1] if aw is None else aw
    hp = aw // HD
    qkv = len(PATTERNS) * aw
    inner = p["ssd_norm"].shape[1]
    nh = p["dt_bias"].shape[1]
    gw_ = {} if gw_ is None else gw_
    gw = inner // SSD_G
    cdim = inner + 2 * SSD_G * SSD_N
    z_off, xbc_off = 3 * qkv, 3 * qkv + inner
    ga_off = xbc_off + cdim
    gs_off = ga_off + d
    hw = d // 2
    assert z_off % gw == 0 and xbc_off % LANE == 0 and ga_off % hw == 0 and gs_off % hw == 0
    assert (nh // SSD_G) * SSD_P == gw and hp % 2 == 0 and aw % LANE == 0 and nh <= LANE
    gdt = MXU_DTYPE

    row = lambda a, width, base=0: ("row", a, width, base)
    const = lambda a, width, base=0: ("const", a, width, base)

    def rms_fwd(name, xin, g):
        return _rw(name, lambda xv, gv: (_rms(xv, gv),), [row(xin, d), const(g, d)],
                   [((d,), ACT_DTYPE)])[0]

    def rms_bwd(name, xin, g, dh, dres):
        def fn(xv, gv, dhv, drv):
            _, vjp = jax.vjp(_rms, xv, gv)
            dx, dg = vjp(dhv.astype(F32))
            return drv + dx, dg
        return _rw(name, fn, [row(xin, d), const(g, d), row(dh, d), row(dres, d)],
                   [((d,), F32)], accs=[(1, d)])

    def ffn_fwd(tag, xin, g, key_gu, key_d):
        h = rms_fwd(tag + "_norm", xin, g)
        gate, up, a = _ffn_up(tag + "_up", h, w[key_gu])
        xo = _mm(tag + "_down", a, w[key_d], "nn", F32, res=xin, scale=0.5)
        return xo, (h, gate, up, a)

    def ffn_bwd(tag, xin, g, wgut, wd, saved, dxo, key_gu, key_d):
        h, gate, up, a = saved
        gw_[key_d] = _mm(tag + "_dwd", a, dxo, "tn", gdt, scale=0.5)
        dgu = _ffn_dact(tag + "_dact", dxo, wd, gate, up, 0.5)
        gw_[key_gu] = _mm(tag + "_dwgu", dgu, h, "tn", gdt)
        dh = _mm(tag + "_dh", dgu, wgut, "nn", F32)
        return rms_bwd(tag + "_dnorm", xin, g, dh, dxo)

    x1, ffn1_saved = ffn_fwd("ffn1", x, p["ffn1_norm"], "gu1t", "d1")
    if hasattr(w, "after_first_ffn"):
        w.after_first_ffn()
    h2 = rms_fwd("mix_norm", x1, p["mix_norm"])
    proj = _mm("in_proj", h2, w["maint"], "nt", ACT_DTYPE, cap_m=512, cap_n=2944)
    dtraw = _mm("dt_proj", h2, w["dtt"], "nt", F32)

    coefs = _alibi_coefs(hp)
    qg2 = jnp.concatenate([p["q_norm"], p["q_norm"]], axis=1)
    kg2 = jnp.concatenate([p["k_norm"], p["k_norm"]], axis=1)
    pw = 2 * HD
    attn_bases = [[(off + gi * aw) // pw for off in (0, qkv, 2 * qkv)]
                  for gi in range(len(PATTERNS))]
    attn_o, attn_l = [], []
    for gi, (_, dil) in enumerate(PATTERNS):
        o, l = _attn_fwd(f"attn_fwd{gi}", proj, attn_bases[gi], qg2, kg2, coefs[gi], dil)
        attn_o.append(o)
        attn_l.append(l)
    ao = _rw("attn_mix", lambda *v: (_mix(*v),), [row(a, aw) for a in attn_o + attn_l],
             [((aw,), ACT_DTYPE)])[0]

    xbc = _conv_fwd("conv_fwd", proj, xbc_off // LANE, p["conv_w"], p["conv_b"])
    pad = lambda v: jnp.pad(v, ((0, 0), (0, LANE - nh)))
    bias_p, alog_p = pad(p["dt_bias"]), pad(p["a_log"])
    yssd, states = _ssd_fwd("ssd_fwd", xbc, dtraw, bias_p, alog_p, inner)
    dexp = jnp.repeat(p["d_skip"], SSD_P, axis=1)
    gate_ins = [row(yssd, gw), row(xbc, gw), row(proj, gw, z_off // gw),
                const(dexp, gw), const(p["ssd_norm"], gw)]
    yn = _rw("ssd_gate", lambda *v: (_gate(*v),), gate_ins, [((gw,), ACT_DTYPE)], ncb=SSD_G)[0]

    ap = _mm("attn_out", ao, w["abt"], "nt", F32)
    sp = _mm("ssd_out", yn, w["sb"], "nn", F32)
    merge_ins = [row(proj, hw, ga_off // hw), row(proj, hw, gs_off // hw), row(ap, hw), row(sp, hw)]
    mg = _rw("merge", lambda *v: (_merge(*v),), merge_ins, [((hw,), ACT_DTYPE)], ncb=2)[0]
    x2 = _mm("mix_out", mg, w["out"], "nn", F32, res=x1)
    x3, ffn2_saved = ffn_fwd("ffn2", x2, p["ffn2_norm"], "gu2t", "d2")

    def loss_fn(yv, tv):
        e = yv - tv
        return e * (1.0 / d), _colsum(e * e)
    dy, loss_vec = _rw("loss", loss_fn, [row(x3, d), row(tgt, d)], [((d,), F32)], accs=[(1, d)])

    gp = {}
    dx2, gp["ffn2_norm"] = ffn_bwd(
        "ffn2", x2, p["ffn2_norm"], w["gu2t"], w["d2"], ffn2_saved, dy, "gu2t", "d2")
    dmg = _mm("d_merge", dx2, w["out"], "nt", ACT_DTYPE)
    gw_["out"] = _mm("dw_out", mg, dx2, "tn", gdt)

    def merge_bwd(gav, gsv, apv, spv, dv):
        _, vjp = jax.vjp(_merge, gav, gsv, apv, spv)
        return vjp(dv.astype(F32))
    dga, dgs, dap, dsp = _rw("d_merge_gate", merge_bwd, merge_ins + [row(dmg, hw)],
                             [((hw,), ACT_DTYPE)] * 4, ncb=2)
    gw_["abt"] = _mm("dw_ab", dap, ao, "tn", gdt)
    dao = _mm("d_attn_o", dap, w["abt"], "nn", F32)
    gw_["sb"] = _mm("dw_sb", yn, dsp, "tn", gdt)
    dyn = _mm("d_ssd_y", dsp, w["sb"], "nt", F32)

    def gate_bwd(yv, xv, zv, dev, gv, dv):
        _, vjp = jax.vjp(_gate, yv, xv, zv, dev, gv)
        return vjp(dv)
    dyssd, dxs_gate, dz, ddexp, gp["ssd_norm"] = _rw(
        "d_ssd_gate", gate_bwd, gate_ins + [row(dyn, gw)],
        [((gw,), F32), ((gw,), F32), ((gw,), ACT_DTYPE)], accs=[(1, gw), (1, gw)], ncb=SSD_G)
    gp["d_skip"] = ddexp.reshape(nh, SSD_P).sum(axis=1).reshape(1, nh)

    dxbc, ddtraw, dbias, dalog = _ssd_bwd("ssd_bwd", xbc, dtraw, bias_p, alog_p, states,
                                          dyssd, dxs_gate)
    gp["dt_bias"], gp["a_log"] = dbias[:, :nh], dalog[:, :nh]
    du, gp["conv_w"], gp["conv_b"] = _conv_bwd("conv_bwd", proj, xbc_off // LANE,
                                               p["conv_w"], p["conv_b"], dxbc)

    def mix_bwd(*v):
        _, vjp = jax.vjp(_mix, *v[:6])
        return vjp(v[6])
    dmix = _rw("d_attn_mix", mix_bwd, [row(a, aw) for a in attn_o + attn_l] + [row(dao, aw)],
               [((aw,), F32)] * 6)
    dq, dk, dv = [], [], []
    dqg = dkg = None
    for gi, (_, dil) in enumerate(PATTERNS):
        r = _attn_bwd(f"attn_bwd{gi}", proj, attn_bases[gi], qg2, kg2, coefs[gi], dil,
                      dmix[gi], dmix[3 + gi])
        dq.append(r[0])
        dk.append(r[1])
        dv.append(r[2])
        dqg = r[3] if dqg is None else dqg + r[3]
        dkg = r[4] if dkg is None else dkg + r[4]
    gp["q_norm"] = dqg[:, :HD] + dqg[:, HD:]
    gp["k_norm"] = dkg[:, :HD] + dkg[:, HD:]

    segs = dq + dk + dv + [dz, du, dga, dgs]
    gw_["maint"] = _mm("dw_in", segs, h2, "tn", gdt)
    gw_["dtt"] = _mm("dw_dt", ddtraw, h2, "tn", gdt)
    dh2 = _mm("d_h2_main", segs, w["maint"], "nn", F32)
    dh2 = _mm("d_h2_dt", ddtraw, w["dtt"], "nn", F32, res=dh2)
    dx1, gp["mix_norm"] = rms_bwd("d_mix_norm", x1, p["mix_norm"], dh2, dx2)
    dx0, gp["ffn1_norm"] = ffn_bwd(
        "ffn1", x, p["ffn1_norm"], w["gu1t"], w["d1"], ffn1_saved, dx1, "gu1t", "d1")
    return loss_vec, dx0, gw_, gp


MESH = pl.DeviceIdType.MESH
HBM_SPEC = pl.BlockSpec(memory_space=pltpu.HBM)


def _mesh_pos():
    return lax.axis_index("x"), lax.axis_index("y"), lax.axis_index("c")


def _flip(pos, k):
    x, y, c = pos
    return (1 - x if k & 4 else x, 1 - y if k & 2 else y, 1 - c if k & 1 else c)


def _dev_index(pos):
    return 4 * pos[0] + 2 * pos[1] + pos[2]


def _rows_of(ref, base, stride, rows, pos):
    start = pl.multiple_of(base + stride * _dev_index(pos), ROW_ALIGN)
    return ref.at[pl.ds(start, rows)]


def _gather(name, shards, dests, out_shapes):
    n = len(shards)
    n_out = len(out_shapes)

    def body(*refs):
        x_refs = refs[:n]
        o_refs = refs[n:n + n_out]
        send_sems, recv_sems, local_sems = refs[n + n_out:]
        me = _mesh_pos()
        sibling = _flip(me, 1)
        chips = [_flip(me, 4), _flip(me, 2), _flip(me, 6)]

        def slot(i, block):
            k_out, base, stride = dests[i]
            return _rows_of(o_refs[k_out], base, stride, shards[i].shape[0], block)

        def copy(i, k, block, to, src=None):
            dst = slot(i, block)
            return pltpu.make_async_remote_copy(
                src_ref=dst if src is None else src, dst_ref=dst,
                send_sem=send_sems.at[7 * i + k], recv_sem=recv_sems.at[7 * i + k],
                device_id=to, device_id_type=MESH)

        mine = [pltpu.make_async_copy(x_refs[i], slot(i, me), local_sems.at[i]) for i in range(n)]
        for cp in mine:
            cp.start()
        first = []
        for i in range(n):
            first.append(copy(i, 0, me, sibling, src=x_refs[i]))
            first += [copy(i, 1 + j, me, chip, src=x_refs[i]) for j, chip in enumerate(chips)]
        for cp in first:
            cp.start()
        passed = []
        for j, chip in enumerate(chips):
            for i in range(n):
                copy(i, 1 + j, chip, me).wait_recv()
                fwd = copy(i, 4 + j, chip, sibling)
                fwd.start()
                passed.append(fwd)
        for i in range(n):
            copy(i, 0, sibling, me).wait_recv()
            for j, chip in enumerate(chips):
                copy(i, 4 + j, _flip(chip, 1), me).wait_recv()
        for cp in first + passed:
            cp.wait_send()
        for cp in mine:
            cp.wait()

    return pl.pallas_call(
        body, name=name,
        out_shape=[jax.ShapeDtypeStruct(s, dt) for s, dt in out_shapes],
        in_specs=[HBM_SPEC] * n, out_specs=[HBM_SPEC] * n_out,
        scratch_shapes=[pltpu.SemaphoreType.DMA((7 * n,)), pltpu.SemaphoreType.DMA((7 * n,)),
                        pltpu.SemaphoreType.DMA((n,))],
    )(*shards)


def _exchange(name, grads, srcs, small):
    n = len(srcs)
    ng = len(grads)

    def body(*refs):
        g_refs = refs[:ng]
        m_ref = refs[ng]
        r_refs = refs[ng + 1:ng + 1 + n]
        s_ref = refs[ng + 1 + n]
        send_sems, recv_sems, local_sems = refs[ng + 2 + n:]
        me = _mesh_pos()
        my = _dev_index(me)

        def slab(i, pos):
            gi, base, stride, rows = srcs[i]
            return _rows_of(g_refs[gi], base, stride, rows, pos)

        own = [pltpu.make_async_copy(slab(i, me), r_refs[i].at[my], local_sems.at[i])
               for i in range(n)]
        own.append(pltpu.make_async_copy(m_ref, s_ref.at[my], local_sems.at[n]))
        for cp in own:
            cp.start()

        def copies(k, src_pos, slot_pos):
            peer = _flip(me, k)
            si = _dev_index(slot_pos)
            out = [pltpu.make_async_remote_copy(
                src_ref=slab(i, src_pos), dst_ref=r_refs[i].at[si],
                send_sem=send_sems.at[7 * i + k - 1], recv_sem=recv_sems.at[7 * i + k - 1],
                device_id=peer, device_id_type=MESH) for i in range(n)]
            out.append(pltpu.make_async_remote_copy(
                src_ref=m_ref, dst_ref=s_ref.at[si],
                send_sem=send_sems.at[7 * n + k - 1], recv_sem=recv_sems.at[7 * n + k - 1],
                device_id=peer, device_id_type=MESH))
            return out

        sent = [cp for k in range(1, NDEV) for cp in copies(k, _flip(me, k), me)]
        for cp in sent:
            cp.start()
        for k in range(1, NDEV):
            for cp in copies(k, me, _flip(me, k)):
                cp.wait_recv()
        for cp in sent:
            cp.wait_send()
        for cp in own:
            cp.wait()

    out_shape = [jax.ShapeDtypeStruct((NDEV, rows, grads[gi].shape[1]), grads[gi].dtype)
                 for gi, _, _, rows in srcs]
    out_shape.append(jax.ShapeDtypeStruct((NDEV,) + small.shape, small.dtype))
    return pl.pallas_call(
        body, name=name, out_shape=out_shape,
        in_specs=[HBM_SPEC] * (ng + 1), out_specs=[HBM_SPEC] * (n + 1),
        scratch_shapes=[pltpu.SemaphoreType.DMA((7 * (n + 1),)),
                        pltpu.SemaphoreType.DMA((7 * (n + 1),)),
                        pltpu.SemaphoreType.DMA((n + 1,))],
    )(*grads, small)


SEM_SPEC = pl.BlockSpec(memory_space=pltpu.SEMAPHORE)
SIDE_EFFECT = pltpu.SideEffectType.DATAFLOW_SIDE_EFFECTING


def _split_refs(plan, srcs, lands, i, src_for, land_from):
    si, sbase, sstride, li, lbase, lstride, rows = plan[i]
    return (_rows_of(srcs[si], sbase, sstride, rows, src_for),
            _rows_of(lands[li], lbase, lstride, rows, land_from))


ALL_PEERS = tuple(range(1, NDEV))
SAME_CORE_AND_SIBLING = (1, 4, 2, 6)
OTHER_CHIPS = (4, 2, 6)


def _split_start(name, srcs, lands, plan, after=(), relations=ALL_PEERS):
    ns, nl, n = len(srcs), len(lands), len(plan)

    def body(*refs):
        s_refs = refs[:ns]
        l_refs = refs[ns:ns + nl]
        send_sems, recv_sems = refs[ns + nl + len(after):ns + nl + len(after) + 2]
        local_sems = refs[ns + nl + len(after) + 2]
        token = refs[ns + nl + len(after) + 3 + ns + nl]
        me = _mesh_pos()
        for i in range(n):
            src, dst = _split_refs(plan, s_refs, l_refs, i, me, me)
            pltpu.make_async_copy(src, dst, local_sems.at[i]).start()
        for k in relations:
            peer = _flip(me, k)
            for i in range(n):
                src, dst = _split_refs(plan, s_refs, l_refs, i, peer, me)
                pltpu.make_async_remote_copy(
                    src_ref=src, dst_ref=dst,
                    send_sem=send_sems.at[7 * i + k - 1], recv_sem=recv_sems.at[7 * i + k - 1],
                    device_id=peer, device_id_type=MESH).start()
        token[...] = jnp.zeros_like(token)

    hbm = lambda a: pltpu.HBM(a.shape, a.dtype)
    out_shape = ((pltpu.SemaphoreType.DMA((7 * n,)), pltpu.SemaphoreType.DMA((7 * n,)),
                  pltpu.SemaphoreType.DMA((n,)))
                 + tuple(hbm(a) for a in srcs) + tuple(hbm(a) for a in lands)
                 + (jax.ShapeDtypeStruct((8, LANE), F32),))
    out = pl.pallas_call(
        body, name=name, out_shape=out_shape,
        in_specs=[HBM_SPEC] * (ns + nl) + [ANY_SPEC] * len(after),
        out_specs=(SEM_SPEC, SEM_SPEC, SEM_SPEC) + (HBM_SPEC,) * (ns + nl)
        + (pl.BlockSpec(memory_space=pltpu.VMEM),),
        input_output_aliases={i: 3 + i for i in range(ns + nl)},
        compiler_params=pltpu.CompilerParams(has_side_effects=SIDE_EFFECT),
    )(*[pltpu.with_memory_space_constraint(a, pltpu.HBM) for a in tuple(srcs) + tuple(lands)],
      *after)
    _Order.tokens.append(out[-1])
    return out[0], out[1], out[2], out[3:3 + ns], out[3 + ns:3 + ns + nl]


def _split_wait(name, started, plan, relations=ALL_PEERS):
    send_sems, recv_sems, local_sems, srcs, lands = started
    ns, nl, n = len(srcs), len(lands), len(plan)
    after = [_Order.last] if _Order.last is not None else []

    def body(*refs):
        s_refs = refs[:ns]
        l_refs = refs[ns:ns + nl]
        send_sems, recv_sems, local_sems = refs[ns + nl:ns + nl + 3]
        me = _mesh_pos()
        for i in range(n):
            src, dst = _split_refs(plan, s_refs, l_refs, i, me, me)
            pltpu.make_async_copy(src, dst, local_sems.at[i]).wait()
        for k in relations:
            peer = _flip(me, k)
            for i in range(n):
                src, dst = _split_refs(plan, s_refs, l_refs, i, peer, peer)
                cp = pltpu.make_async_remote_copy(
                    src_ref=src, dst_ref=dst,
                    send_sem=send_sems.at[7 * i + k - 1], recv_sem=recv_sems.at[7 * i + k - 1],
                    device_id=peer, device_id_type=MESH)
                cp.wait_send()
                cp.wait_recv()

    hbm = lambda a: pltpu.HBM(a.shape, a.dtype)
    out = pl.pallas_call(
        body, name=name,
        out_shape=tuple(hbm(a) for a in srcs) + tuple(hbm(a) for a in lands),
        in_specs=[HBM_SPEC] * (ns + nl) + [SEM_SPEC] * 3 + [ANY_SPEC] * len(after),
        out_specs=(HBM_SPEC,) * (ns + nl),
        input_output_aliases={i: i for i in range(ns + nl)},
        compiler_params=pltpu.CompilerParams(has_side_effects=SIDE_EFFECT),
    )(*srcs, *lands, send_sems, recv_sems, local_sems, *after)
    return list(out[ns:])


def _forward_refs(plan, lands, i, block):
    _, _, _, li, lbase, lstride, rows = plan[i]
    return _rows_of(lands[li], lbase, lstride, rows, block)


def _forward_start(name, lands, plan):
    nl, n = len(lands), len(plan)

    def body(*refs):
        l_refs = refs[:nl]
        send_sems, recv_sems = refs[nl:nl + 2]
        token = refs[nl + 2 + nl]
        me = _mesh_pos()
        for j, kc in enumerate(OTHER_CHIPS):
            for i in range(n):
                rows = _forward_refs(plan, l_refs, i, _flip(me, kc))
                pltpu.make_async_remote_copy(
                    src_ref=rows, dst_ref=rows,
                    send_sem=send_sems.at[3 * i + j], recv_sem=recv_sems.at[3 * i + j],
                    device_id=_flip(me, 1), device_id_type=MESH).start()
        token[...] = jnp.zeros_like(token)

    hbm = lambda a: pltpu.HBM(a.shape, a.dtype)
    out = pl.pallas_call(
        body, name=name,
        out_shape=((pltpu.SemaphoreType.DMA((3 * n,)), pltpu.SemaphoreType.DMA((3 * n,)))
                   + tuple(hbm(a) for a in lands) + (jax.ShapeDtypeStruct((8, LANE), F32),)),
        in_specs=[HBM_SPEC] * nl,
        out_specs=(SEM_SPEC, SEM_SPEC) + (HBM_SPEC,) * nl
        + (pl.BlockSpec(memory_space=pltpu.VMEM),),
        input_output_aliases={i: 2 + i for i in range(nl)},
        compiler_params=pltpu.CompilerParams(has_side_effects=SIDE_EFFECT),
    )(*[pltpu.with_memory_space_constraint(a, pltpu.HBM) for a in lands])
    _Order.tokens.append(out[-1])
    return out[0], out[1], out[2:2 + nl]


def _forward_wait(name, started, plan):
    send_sems, recv_sems, lands = started
    nl, n = len(lands), len(plan)
    after = [_Order.last] if _Order.last is not None else []

    def body(*refs):
        l_refs = refs[:nl]
        send_sems, recv_sems = refs[nl:nl + 2]
        me = _mesh_pos()
        for j, kc in enumerate(OTHER_CHIPS):
            for i in range(n):
                sent = _forward_refs(plan, l_refs, i, _flip(me, kc))
                came = _forward_refs(plan, l_refs, i, _flip(_flip(me, 1), kc))
                cp = pltpu.make_async_remote_copy(
                    src_ref=sent, dst_ref=came,
                    send_sem=send_sems.at[3 * i + j], recv_sem=recv_sems.at[3 * i + j],
                    device_id=_flip(me, 1), device_id_type=MESH)
                cp.wait_send()
                cp.wait_recv()

    hbm = lambda a: pltpu.HBM(a.shape, a.dtype)
    out = pl.pallas_call(
        body, name=name, out_shape=tuple(hbm(a) for a in lands),
        in_specs=[HBM_SPEC] * nl + [SEM_SPEC] * 2 + [ANY_SPEC] * len(after),
        out_specs=(HBM_SPEC,) * nl,
        input_output_aliases={i: i for i in range(nl)},
        compiler_params=pltpu.CompilerParams(has_side_effects=SIDE_EFFECT),
    )(*lands, send_sems, recv_sems, *after)
    return list(out)


def _regroup_rows(name, padded, r, rp, lo, hi):
    d = padded.shape[1]
    pack = 4 // padded.dtype.itemsize
    assert r % pack == 0 and rp % ROW_ALIGN == 0 and lo % (8 * pack) == 0 and hi % (8 * pack) == 0
    r2, rp2, lo2, hi2 = r // pack, rp // pack, lo // pack, hi // pack
    u32 = jnp.uint32

    def body(x_ref, main_ref, cut_ref):
        x = pltpu.bitcast(x_ref[...], u32)
        joined = jnp.concatenate([x[rp2 * j:rp2 * j + r2] for j in range(NDEV)], axis=0)
        main = jnp.concatenate([joined[:lo2], joined[hi2:]], axis=0)
        cut = jnp.concatenate([joined[lo2:hi2], jnp.zeros((LANE // pack - (hi2 - lo2), LANE), u32)],
                              axis=0)
        main_ref[...] = pltpu.bitcast(main, padded.dtype)
        cut_ref[...] = pltpu.bitcast(cut, padded.dtype)

    return pl.pallas_call(
        body, name=name,
        out_shape=[jax.ShapeDtypeStruct((NDEV * r - (hi - lo), d), padded.dtype),
                   jax.ShapeDtypeStruct((LANE, d), padded.dtype)],
        grid=(d // LANE,),
        in_specs=[pl.BlockSpec((NDEV * rp, LANE), lambda i: (0, i))],
        out_specs=[pl.BlockSpec((NDEV * r - (hi - lo), LANE), lambda i: (0, i)),
                   pl.BlockSpec((LANE, LANE), lambda i: (0, i))],
        compiler_params=_params(("parallel",)),
    )(padded)


def _ungroup_rows(name, main, cut, r, rp, lo, hi):
    d = main.shape[1]
    pack = 4 // main.dtype.itemsize
    r2, rp2, lo2, hi2 = r // pack, rp // pack, lo // pack, hi // pack
    u32 = jnp.uint32

    def body(main_ref, cut_ref, o_ref):
        m = pltpu.bitcast(main_ref[...], u32)
        c = pltpu.bitcast(cut_ref[...], u32)
        joined = jnp.concatenate([m[:lo2], c[:hi2 - lo2], m[lo2:]], axis=0)
        zeros = jnp.zeros((rp2 - r2, LANE), u32)
        parts = []
        for j in range(NDEV):
            parts += [joined[r2 * j:r2 * (j + 1)], zeros]
        o_ref[...] = pltpu.bitcast(jnp.concatenate(parts, axis=0), main.dtype)

    return pl.pallas_call(
        body, name=name,
        out_shape=jax.ShapeDtypeStruct((NDEV * rp, d), main.dtype),
        grid=(d // LANE,),
        in_specs=[pl.BlockSpec((main.shape[0], LANE), lambda i: (0, i)),
                  pl.BlockSpec((LANE, LANE), lambda i: (0, i))],
        out_specs=pl.BlockSpec((NDEV * rp, LANE), lambda i: (0, i)),
        compiler_params=_params(("parallel",)),
    )(main, cut)


def _sum_slabs(name, a):
    s, r, c = a.shape

    def body(a_ref, o_ref):
        acc = a_ref[0].astype(F32)
        for i in range(1, s):
            acc = acc + a_ref[i].astype(F32)
        o_ref[...] = acc

    return pl.pallas_call(body, name=name, out_shape=jax.ShapeDtypeStruct((r, c), F32))(a)


def _adamw_update(g, w, m, v):
    mn = ADAM_B1 * m + (1.0 - ADAM_B1) * g
    vn = ADAM_B2 * v + (1.0 - ADAM_B2) * (g * g)
    m_hat = mn / (1.0 - ADAM_B1 ** ADAM_STEP)
    v_hat = vn / (1.0 - ADAM_B2 ** ADAM_STEP)
    delta = -ADAM_LR * (m_hat / (jnp.sqrt(v_hat) + ADAM_EPS) + ADAM_WD * w)
    return delta, mn, vn


def _adamw(name, gsrc, w, m, v, transposed=False, tr=256):
    s = gsrc.shape[0]
    lead = w.ndim == 3
    r, c = w.shape[-2:]
    step = LANE if transposed else 8
    tr = max(t for t in range(step, min(tr, r) + 1, step) if r % t == 0)

    def body(g_ref, w_ref, m_ref, v_ref, go_ref, d_ref, mo_ref, vo_ref):
        g = g_ref[0].astype(F32)
        for i in range(1, s):
            g = g + g_ref[i].astype(F32)
        if transposed:
            g = g.T[:, :c]
        delta, mn, vn = _adamw_update(g, w_ref[...], m_ref[...], v_ref[...])
        go_ref[...] = g
        d_ref[...] = delta
        mo_ref[...] = mn
        vo_ref[...] = vn

    if lead:
        blk = pl.BlockSpec((None, tr, c), lambda i: (0, i, 0))
    else:
        blk = pl.BlockSpec((tr, c), lambda i: (i, 0))
    if transposed:
        g_spec = pl.BlockSpec((s, gsrc.shape[1], tr), lambda i: (0, 0, i))
    else:
        g_spec = pl.BlockSpec((s, tr, c), lambda i: (0, i, 0))
    return pl.pallas_call(
        body, name=name, out_shape=[jax.ShapeDtypeStruct(w.shape, F32)] * 4,
        grid=(r // tr,),
        in_specs=[g_spec, blk, blk, blk], out_specs=[blk] * 4,
        compiler_params=_params(("parallel",)),
    )(gsrc, w, m, v)


REPLICATED = ("ffn1_norm", "mix_norm", "q_norm", "k_norm", "conv_b", "dt_bias", "a_log",
              "d_skip", "ssd_norm", "ffn2_norm")
ALL_WEIGHTS = ("ffn1_norm", "ffn1_w_gate", "ffn1_w_up", "ffn1_w_down", "mix_norm", "w_in",
               "q_norm", "k_norm", "conv_w", "conv_b", "dt_bias", "a_log", "d_skip", "ssd_norm",
               "w_attn_branch", "w_ssd_branch", "w_out", "ffn2_norm", "ffn2_w_gate", "ffn2_w_up",
               "ffn2_w_down")
BIG = (("ffn1_w_gate", True, "gu1t", 0), ("ffn1_w_up", True, "gu1t", 1),
       ("ffn1_w_down", False, "d1", 0), ("w_in", True, "wint", 0),
       ("w_attn_branch", True, "abt", 0), ("w_ssd_branch", False, "sb", 0),
       ("w_out", False, "out", 0),
       ("ffn2_w_gate", True, "gu2t", 0), ("ffn2_w_up", True, "gu2t", 1),
       ("ffn2_w_down", False, "d2", 0))


def _nrows(shape, cols):
    return -(-math.prod(shape) // cols)


def _pack_rows(arrs, cols, row_tile):
    parts = []
    for a in arrs:
        flat = a.reshape(-1)
        nr = -(-flat.shape[0] // cols)
        parts.append(jnp.pad(flat, (0, nr * cols - flat.shape[0])).reshape(nr, cols))
    out = jnp.concatenate(parts, axis=0)
    return jnp.pad(out, ((0, _round_up(out.shape[0], row_tile) - out.shape[0]), (0, 0)))


def _unpack_rows(packed, shapes):
    cols = packed.shape[-1]
    out, r0 = [], 0
    for sh in shapes:
        nr = _nrows(sh, cols)
        out.append(packed[r0:r0 + nr].reshape(-1)[:math.prod(sh)].reshape(tuple(sh)))
        r0 += nr
    return out


def kernel(x, ffn1_norm, ffn1_w_gate, ffn1_w_up, ffn1_w_down, mix_norm, w_in, q_norm, k_norm, conv_w, conv_b, dt_bias, a_log, d_skip, ssd_norm, w_attn_branch, w_ssd_branch, w_out, ffn2_norm, ffn2_w_gate, ffn2_w_up, ffn2_w_down, loss_target, m_ffn1_norm, m_ffn1_w_gate, m_ffn1_w_up, m_ffn1_w_down, m_mix_norm, m_w_in, m_q_norm, m_k_norm, m_conv_w, m_conv_b, m_dt_bias, m_a_log, m_d_skip, m_ssd_norm, m_w_attn_branch, m_w_ssd_branch, m_w_out, m_ffn2_norm, m_ffn2_w_gate, m_ffn2_w_up, m_ffn2_w_down, v_ffn1_norm, v_ffn1_w_gate, v_ffn1_w_up, v_ffn1_w_down, v_mix_norm, v_w_in, v_q_norm, v_k_norm, v_conv_w, v_conv_b, v_dt_bias, v_a_log, v_d_skip, v_ssd_norm, v_w_attn_branch, v_w_ssd_branch, v_w_out, v_ffn2_norm, v_ffn2_w_gate, v_ffn2_w_up, v_ffn2_w_down):
    given = dict(locals())
    wts = {n: given[n] for n in ALL_WEIGHTS}
    mom = {n: given["m_" + n] for n in ALL_WEIGHTS}
    var = {n: given["v_" + n] for n in ALL_WEIGHTS}
    d = x.shape[-1]
    nh = dt_bias.shape[1]
    my = _dev_index(_mesh_pos())

    def row_form(n, col_sharded):
        a = wts[n][0].T if col_sharded else wts[n][0]
        a = jnp.pad(a, ((0, _round_up(a.shape[0], ROW_ALIGN) - a.shape[0]), (0, 0)))
        return a.astype(MXU_DTYPE)

    _Order.tokens, _Order.last = [], None
    shard = {n: row_form(n, cs) for n, cs, _, _ in BIG}
    entries = {buf: [e for e in BIG if e[2] == buf] for buf in dict.fromkeys(e[2] for e in BIG)}

    def buf_shape(buf):
        r, c = shard[entries[buf][0][0]].shape
        return (len(entries[buf]) * NDEV * r, c)

    def gather_plan(bufs):
        srcs, lands, plan = [], [], []
        for li, buf in enumerate(bufs):
            lands.append(lax.empty(buf_shape(buf), MXU_DTYPE))
            for n, _, _, pos in entries[buf]:
                r = shard[n].shape[0]
                plan.append((len(srcs), 0, 0, li, pos * NDEV * r, r, r))
                srcs.append(shard[n])
        return srcs, lands, plan

    def scatter_plan(bufs, grads):
        srcs, lands, plan, names = [], [], [], []
        for si, buf in enumerate(bufs):
            srcs.append(grads[buf])
            for n, _, _, pos in entries[buf]:
                r, c = shard[n].shape
                plan.append((si, pos * NDEV * r, r, len(lands), 0, r, r))
                lands.append(lax.empty((NDEV * r, c), MXU_DTYPE))
                names.append(n)
        return srcs, lands, plan, names

    first_bufs = ("gu1t", "d1")
    shards, dests, out_shapes = [], [], []
    for bi, buf in enumerate(first_bufs):
        out_shapes.append((buf_shape(buf), MXU_DTYPE))
        for n, _, _, pos in entries[buf]:
            r = shard[n].shape[0]
            shards.append(shard[n])
            dests.append((bi, pos * NDEV * r, r))
    conv_rows = _pack_rows([conv_w[0]], LANE, ROW_ALIGN)
    shards.append(conv_rows)
    dests.append((len(first_bufs), 0, conv_rows.shape[0]))
    out_shapes.append(((NDEV * conv_rows.shape[0], LANE), F32))
    gathered = _gather("gather_first", shards, dests, out_shapes)

    in_cols = w_in.shape[2]
    in_pad = _round_up(in_cols, ROW_ALIGN)
    dt_off = NDEV * in_cols - 2 * d - nh
    second_bufs = ("wint",)
    third_bufs = ("abt", "sb", "out", "gu2t", "d2")
    plan2 = gather_plan(second_bufs)
    started2 = _split_start("gather_in_start", *plan2, after=[gathered[0]],
                            relations=SAME_CORE_AND_SIBLING)
    forwarded, started3 = [], []

    class Weights(dict):
        def after_first_ffn(self):
            lands = _split_wait("gather_in_wait", started2, plan2[2],
                                relations=SAME_CORE_AND_SIBLING)
            forwarded.append(_forward_start("gather_in_forward", lands, plan2[2]))

        def __missing__(self, key):
            if key in ("maint", "dtt"):
                wint = _forward_wait("gather_in_arrive", forwarded[0], plan2[2])[0]
                plan3 = gather_plan(third_bufs)
                started3.append((_split_start("gather_rest_start", *plan3, after=[wint]), plan3[2]))
                self["maint"], self["dtt"] = _regroup_rows(
                    "regroup_w_in", wint, in_cols, in_pad, dt_off, dt_off + nh)
            else:
                st, plan = started3[0]
                for buf, a in zip(third_bufs, _split_wait("gather_rest_wait", st, plan)):
                    self[buf] = a
            return self[key]

    w = Weights(gu1t=gathered[0], d1=gathered[1])
    p = {n: wts[n] for n in REPLICATED}
    conv_all = gathered[-1].reshape(NDEV, conv_rows.shape[0] * LANE)[:, :math.prod(conv_w.shape[1:])]
    p["conv_w"] = (conv_all.reshape((NDEV,) + conv_w.shape[1:]).transpose(1, 0, 2)
                   .reshape(conv_w.shape[1], NDEV * conv_w.shape[2]))

    groups = (("scatter_late", ("gu2t", "d2", "out", "abt", "sb")),
              ("scatter_in", ("maint", "dtt")),
              ("scatter_first", ("gu1t", "d1")))
    in_flight = []

    class Grads(dict):
        def __setitem__(self, key, value):
            dict.__setitem__(self, key, value)
            for tag, need in groups:
                if key in need and all(k in self for k in need):
                    if tag == "scatter_in":
                        gwin = _ungroup_rows("ungroup_w_in", self["maint"], self["dtt"],
                                             in_cols, in_pad, dt_off, dt_off + nh)
                        bufs, grads = ("wint",), {"wint": gwin}
                    else:
                        bufs, grads = need, self
                    srcs, lands, plan, names = scatter_plan(bufs, grads)
                    in_flight.append((tag, _split_start(tag + "_start", srcs, lands, plan),
                                      plan, names))

    loss_vec, dx, gw, gp = _local_step(x[0], loss_target[0], w, p, Grads(),
                                       aw=w_attn_branch.shape[1])

    small_names = REPLICATED + ("conv_w",)
    small_shapes = [gp[n].shape for n in small_names]
    small = _pack_rows([gp[n] for n in small_names], LANE, 8)
    small_all = _exchange("exchange_small", [], [], small)[0]

    outs = [{}, {}, {}, {}]
    col_sharded_of = {n: cs for n, cs, _, _ in BIG}
    for tag, started, plan, names in in_flight:
        for n, rv in zip(names, _split_wait(tag + "_wait", started, plan)):
            rv = rv.reshape(NDEV, shard[n].shape[0], shard[n].shape[1])
            if col_sharded_of[n] and rv.shape[1] == wts[n].shape[2]:
                res = _adamw("adamw_" + n, rv, wts[n][0].T, mom[n][0].T, var[n][0].T)
                res = [a.T for a in res]
            else:
                res = _adamw("adamw_" + n, rv, wts[n][0], mom[n][0], var[n][0],
                             transposed=col_sharded_of[n])
            _Order.done(res)
            for k in range(4):
                outs[k][n] = res[k][None]

    small_g = _unpack_rows(_sum_slabs("sum_small_grads", small_all), small_shapes)
    small_g = dict(zip(small_names, small_g))
    cs = conv_w.shape[2]
    small_g["conv_w"] = lax.dynamic_slice_in_dim(small_g["conv_w"], my * cs, cs, axis=1)
    small_shard_shapes = [wts[n].shape[-2:] for n in small_names]
    sg = _pack_rows([small_g[n] for n in small_names], LANE, 8)
    sw = _pack_rows([wts[n] for n in small_names], LANE, 8)
    sm = _pack_rows([mom[n] for n in small_names], LANE, 8)
    sv = _pack_rows([var[n] for n in small_names], LANE, 8)
    res_small = _adamw("adamw_small", sg[None], sw, sm, sv, tr=sg.shape[0])
    for k in range(4):
        for n, a in zip(small_names, _unpack_rows(res_small[k], small_shard_shapes)):
            outs[k][n] = a.reshape(wts[n].shape)

    loss = lax.psum(0.5 * jnp.sum(loss_vec) / d, ("x", "y", "c"))
    result = [loss, dx[None]]
    for k in range(4):
        result += [outs[k][n] for n in ALL_WEIGHTS]
    return tuple(result)
```

```python
import functools
import math

import numpy as np
import jax
import jax.numpy as jnp
from jax import lax
from jax.experimental import pallas as pl
from jax.experimental.pallas import tpu as pltpu

F32 = jnp.float32
BF16 = jnp.bfloat16
MXU_DTYPE = BF16
ACT_DTYPE = BF16

NDEV = 8
EPS = 1e-6
HD = 64
QB = 128
PATTERNS = ((128, 1), (512, 4), (2048, 16))
ALIBI_MAX_EXP = 8.0
SSD_P = 64
SSD_N = 128
SSD_G = 4
SSD_Q = 128
SSD_K = 4
NEG = -1e30
LANE = 128
ROW_ALIGN = 16
VMEM_LIMIT = 56 * 1024 * 1024

ADAM_LR, ADAM_B1, ADAM_B2, ADAM_EPS, ADAM_WD, ADAM_STEP = 0.001, 0.9, 0.999, 1e-8, 0.01, 10

NN = (((1,), (0,)), ((), ()))
NT = (((1,), (1,)), ((), ()))
TN = (((0,), (0,)), ((), ()))


BNN = (((2,), (1,)), ((0,), (0,)))
BNT = (((2,), (2,)), ((0,), (0,)))
BTN = (((1,), (1,)), ((0,), (0,)))
_DOT_GRADS = {
    NN: (("g", "b", NT), ("a", "g", TN)),
    NT: (("g", "b", NN), ("g", "a", TN)),
    TN: (("b", "g", NT), ("a", "g", NN)),
    BNT: (("g", "b", BNN), ("g", "a", BTN)),
    BTN: (("b", "g", BNT), ("a", "g", BNN)),
}


def _mxu(a, b, dims):
    return lax.dot_general(a.astype(MXU_DTYPE), b.astype(MXU_DTYPE), dims,
                           preferred_element_type=F32)


@functools.partial(jax.custom_vjp, nondiff_argnums=(2,))
def _dot_vjp(a, b, dims):
    return _mxu(a, b, dims)


def _dot_vjp_fwd(a, b, dims):
    return _mxu(a, b, dims), (a.astype(MXU_DTYPE), b.astype(MXU_DTYPE))


def _dot_vjp_bwd(dims, res, g):
    ops = {"a": res[0], "b": res[1], "g": g}
    (x1, y1, d1), (x2, y2, d2) = _DOT_GRADS[dims]
    return _mxu(ops[x1], ops[y1], d1), _mxu(ops[x2], ops[y2], d2)


_dot_vjp.defvjp(_dot_vjp_fwd, _dot_vjp_bwd)


def _dot(a, b, dims=NN):
    return _dot_vjp(a, b, dims)


def _split3(a):
    hi = a.astype(BF16)
    r = a - hi.astype(F32)
    mid = r.astype(BF16)
    lo = (r - mid.astype(F32)).astype(BF16)
    return hi, mid, lo


def _dot3(a, b, dims=NN, split=0):
    if split == 0:
        bb = b.astype(BF16)
        parts = [lax.dot_general(s, bb, dims, preferred_element_type=F32) for s in _split3(a)]
    else:
        aa = a.astype(BF16)
        parts = [lax.dot_general(aa, s, dims, preferred_element_type=F32) for s in _split3(b)]
    return parts[0] + parts[1] + parts[2]


@jax.custom_vjp
def _spread(v, e):
    return _dot3(v, e)


def _spread_fwd(v, e):
    return _dot3(v, e), e


def _spread_bwd(e, g):
    return _dot3(g, e, NT), jnp.zeros_like(e)


_spread.defvjp(_spread_fwd, _spread_bwd)


@jax.custom_vjp
def _running_sum(a, lower):
    return _dot3(lower, a, NN, split=1)


def _running_sum_fwd(a, lower):
    return _dot3(lower, a, NN, split=1), lower


def _running_sum_bwd(lower, g):
    return _dot3(lower, g, TN, split=1), jnp.zeros_like(lower)


_running_sum.defvjp(_running_sum_fwd, _running_sum_bwd)


def _tile(n, cap):
    if n <= cap:
        return n
    best = None
    for t in range(LANE, cap + 1, LANE):
        if n % t == 0:
            best = t
    assert best is not None, (n, cap)
    return best


def _params(sem):
    return pltpu.CompilerParams(dimension_semantics=sem, vmem_limit_bytes=VMEM_LIMIT)


def _round_up(n, m):
    return -(-n // m) * m


class _Order:
    tokens = []
    last = None

    @classmethod
    def take(cls):
        out, cls.tokens = cls.tokens, []
        return out

    @classmethod
    def done(cls, result):
        cls.last = result[0] if isinstance(result, (list, tuple)) else result
        return result


ANY_SPEC = pl.BlockSpec(memory_space=pl.ANY)


def _mm(name, a, b, mode, out_dtype=F32, res=None, scale=1.0,
        cap_m=1408, cap_n=1408, cap_k=1408):
    segs = list(a) if isinstance(a, (list, tuple)) else [a]
    nseg = len(segs)
    if mode == "tn":
        k = segs[0].shape[0]
        widths = [s.shape[1] for s in segs]
        m = sum(widths)
        k2, n = b.shape
        tm = _tile(math.gcd(*widths), cap_m)
        tk = _tile(k, cap_k)
        counts = [wd // tm for wd in widths]
    else:
        m = segs[0].shape[0]
        widths = [s.shape[1] for s in segs]
        k = sum(widths)
        (k2, n) = b.shape if mode == "nn" else b.shape[::-1]
        tm = _tile(m, cap_m)
        tk = _tile(math.gcd(*widths), cap_k)
        counts = [wd // tk for wd in widths]
    assert k == k2, (name, [s.shape for s in segs], b.shape, mode)
    tn = _tile(n, cap_n)
    nk = k // tk
    starts = [sum(counts[:s]) for s in range(nseg)]
    dims = {"nn": NN, "nt": NT, "tn": TN}[mode]

    def a_spec(s):
        lo, cnt = starts[s], counts[s]
        if mode == "tn":
            if nseg == 1:
                return pl.BlockSpec((tk, tm), lambda i, j, kk: (kk, i))
            return pl.BlockSpec(
                (tk, tm), lambda i, j, kk: (jnp.where((i >= lo) & (i < lo + cnt), kk, 0),
                                            jnp.clip(i - lo, 0, cnt - 1)))
        if nseg == 1:
            return pl.BlockSpec((tm, tk), lambda i, j, kk: (i, kk))
        return pl.BlockSpec((tm, tk), lambda i, j, kk: (i, jnp.clip(kk - lo, 0, cnt - 1)))

    b_spec = (pl.BlockSpec((tn, tk), lambda i, j, kk: (j, kk)) if mode == "nt"
              else pl.BlockSpec((tk, tn), lambda i, j, kk: (kk, j)))
    o_spec = pl.BlockSpec((tm, tn), lambda i, j, kk: (i, j))
    has_res = res is not None
    use_acc = nk > 1 or nseg > 1
    ties = _Order.take()
    nt_ = len(ties)

    def body(*refs):
        a_refs = refs[:nseg]
        b_ref = refs[nseg]
        r_ref = refs[nseg + 1] if has_res else None
        o_ref = refs[nseg + 1 + has_res + nt_]
        scr = refs[nseg + 2 + has_res + nt_:]

        def finish(acc):
            if scale != 1.0:
                acc = acc * scale
            if has_res:
                acc = r_ref[...].astype(F32) + acc
            o_ref[...] = acc.astype(o_ref.dtype)

        if not use_acc:
            finish(_dot(a_refs[0][...], b_ref[...], dims))
            return
        acc_ref = scr[0]
        kk = pl.program_id(2)
        sel = pl.program_id(0) if mode == "tn" else kk

        @pl.when(kk == 0)
        def _():
            acc_ref[...] = jnp.zeros_like(acc_ref)

        for s in range(nseg):
            def add(s=s):
                acc_ref[...] += _dot(a_refs[s][...], b_ref[...], dims)
            if nseg == 1:
                add()
            else:
                pl.when((sel >= starts[s]) & (sel < starts[s] + counts[s]))(add)

        @pl.when(kk == nk - 1)
        def _():
            finish(acc_ref[...])

    in_specs = ([a_spec(s) for s in range(nseg)] + [b_spec] + ([o_spec] if has_res else [])
                + [ANY_SPEC] * nt_)
    args = tuple(segs) + (b,) + ((res,) if has_res else ()) + tuple(ties)
    return _Order.done(pl.pallas_call(
        body, name=name,
        out_shape=jax.ShapeDtypeStruct((m, n), out_dtype),
        grid=(m // tm, n // tn, nk),
        in_specs=in_specs, out_specs=o_spec,
        scratch_shapes=[pltpu.VMEM((tm, tn), F32)] if use_acc else [],
        compiler_params=_params(("parallel", "parallel", "arbitrary")),
    )(*args))


def _act(g, u):
    return _silu(g.astype(F32)) * u.astype(F32)


def _ffn_up(name, h, wgut, cap_m=512, cap_n=1408):
    m, k = h.shape
    dff = wgut.shape[0] // 2
    tm, tn = _tile(m, cap_m), _tile(dff, cap_n)
    nj = dff // tn
    ties = _Order.take()

    def body(h_ref, wg_ref, wu_ref, *rest):
        g_ref, u_ref, a_ref = rest[len(ties):]
        hv = h_ref[...]
        g = _dot(hv, wg_ref[...], NT)
        u = _dot(hv, wu_ref[...], NT)
        g_ref[...] = g.astype(g_ref.dtype)
        u_ref[...] = u.astype(u_ref.dtype)
        a_ref[...] = _act(g, u).astype(a_ref.dtype)

    o_spec = pl.BlockSpec((tm, tn), lambda i, j: (i, j))
    return _Order.done(pl.pallas_call(
        body, name=name, out_shape=[jax.ShapeDtypeStruct((m, dff), ACT_DTYPE)] * 3,
        grid=(m // tm, nj),
        in_specs=[pl.BlockSpec((tm, k), lambda i, j: (i, 0)),
                  pl.BlockSpec((tn, k), lambda i, j: (j, 0)),
                  pl.BlockSpec((tn, k), lambda i, j: (nj + j, 0))] + [ANY_SPEC] * len(ties),
        out_specs=[o_spec] * 3,
        compiler_params=_params(("parallel", "parallel")),
    )(h, wgut, wgut, *ties))


def _ffn_dact(name, dxo, wd, g, u, scale, cap_m=512, cap_n=1408):
    m, k = dxo.shape
    dff = wd.shape[0]
    tm, tn = _tile(m, cap_m), _tile(dff, cap_n)
    ties = _Order.take()

    def body(d_ref, w_ref, g_ref, u_ref, *rest):
        dg_ref, du_ref = rest[len(ties):]
        da = _dot(d_ref[...], w_ref[...], NT) * scale
        _, vjp = jax.vjp(_act, g_ref[...], u_ref[...])
        dg, du = vjp(da)
        dg_ref[...] = dg.astype(dg_ref.dtype)
        du_ref[...] = du.astype(du_ref.dtype)

    o_spec = pl.BlockSpec((tm, tn), lambda i, j: (i, j))
    return _Order.done(pl.pallas_call(
        body, name=name, out_shape=[jax.ShapeDtypeStruct((m, dff), ACT_DTYPE)] * 2,
        grid=(m // tm, dff // tn),
        in_specs=[pl.BlockSpec((tm, k), lambda i, j: (i, 0)),
                  pl.BlockSpec((tn, k), lambda i, j: (j, 0)), o_spec, o_spec]
        + [ANY_SPEC] * len(ties),
        out_specs=[o_spec] * 2,
        compiler_params=_params(("parallel", "parallel")),
    )(dxo, wd, g, u, *ties))


def _rw(name, fn, ins, outs, accs=(), tr=256, ncb=1):
    t = next(a.shape[0] for kind, a, _, _ in ins if kind == "row")
    assert t % tr == 0
    n_in = len(ins)
    n_pieces = sum(len(w) for w, _ in outs)

    def spec(kind, arr, width, base):
        if kind == "row":
            return pl.BlockSpec((tr, width), lambda j, i: (i, base + j))
        return pl.BlockSpec((arr.shape[0], width), lambda j, i: (0, base + j))

    in_specs = [spec(*s) for s in ins]
    out_shapes, out_specs = [], []
    for widths, dt in outs:
        w = sum(widths)
        out_shapes.append(jax.ShapeDtypeStruct((t, w * ncb), dt))
        out_specs.append(pl.BlockSpec((tr, w), lambda j, i: (i, j)))
    for rows, width in accs:
        out_shapes.append(jax.ShapeDtypeStruct((rows, width * ncb), F32))
        out_specs.append(pl.BlockSpec((rows, width), lambda j, i: (0, j)))

    ties = _Order.take()
    nt_ = len(ties)
    in_specs = in_specs + [ANY_SPEC] * nt_

    def body(*refs):
        vals = [r[...] for r in refs[:n_in]]
        res = fn(*vals)
        o_refs = refs[n_in + nt_:n_in + nt_ + len(outs)]
        a_refs = refs[n_in + nt_ + len(outs):]
        p = 0
        for (widths, _), o_ref in zip(outs, o_refs):
            off = 0
            for w in widths:
                if len(widths) == 1:
                    o_ref[...] = res[p].astype(o_ref.dtype)
                else:
                    o_ref[:, off:off + w] = res[p].astype(o_ref.dtype)
                off += w
                p += 1
        i = pl.program_id(1)
        for a_ref, v in zip(a_refs, res[n_pieces:]):
            @pl.when(i == 0)
            def _(a_ref=a_ref, v=v):
                a_ref[...] = v

            @pl.when(i > 0)
            def _(a_ref=a_ref, v=v):
                a_ref[...] += v

    return _Order.done(pl.pallas_call(
        body, name=name, out_shape=out_shapes,
        grid=(ncb, t // tr), in_specs=in_specs, out_specs=out_specs,
        compiler_params=_params(("parallel", "arbitrary")),
    )(*[a for _, a, _, _ in ins], *ties))


def _rms(x, g):
    x = x.astype(F32)
    return x * lax.rsqrt(jnp.mean(x * x, axis=-1, keepdims=True) + EPS) * g


def _silu(x):
    return x * jax.nn.sigmoid(x)


def _colsum(v):
    return jnp.sum(v, axis=0, keepdims=True)


def _pair_norm(x, g):
    w = 2 * HD
    ri = lax.broadcasted_iota(jnp.int32, (w, w), 0)
    ci = lax.broadcasted_iota(jnp.int32, (w, w), 1)
    same_head = ((ri < HD) == (ci < HD)).astype(F32)
    ms = _spread(x * x, same_head) * (1.0 / HD)
    return x * lax.rsqrt(ms + EPS) * g


ATTN_SCALE = 1.0 / math.sqrt(HD)


def _attn_bias(coef):
    key = lax.broadcasted_iota(jnp.int32, (QB, QB), 0)
    qry = lax.broadcasted_iota(jnp.int32, (QB, QB), 1)
    dist = (qry - key).astype(F32)
    own = jnp.where(qry >= key, -coef * dist, NEG)
    prev = jnp.where(qry <= key, -coef * (dist + float(QB)), NEG)
    return own, prev


def _attn_pair(qn, kcn, kpn, vc, vp, b_own, b_prev):
    nb = qn.shape[0]
    w = 2 * HD
    lane = lax.broadcasted_iota(jnp.int32, (1, 1, w), 2)
    eye = (lax.broadcasted_iota(jnp.int32, (QB, QB), 0)
           == lax.broadcasted_iota(jnp.int32, (QB, QB), 1)).astype(F32)
    out = jnp.zeros((nb, QB, w), F32)
    lb = jnp.zeros((nb * QB, w), F32)
    for hh in range(2):
        mask = ((lane < HD) if hh == 0 else (lane >= HD)).astype(F32)
        qm = qn * mask
        lc = _dot(kcn, qm, BNT) + b_own[hh]
        lp = _dot(kpn, qm, BNT) + b_prev[hh]
        m = lax.stop_gradient(jnp.maximum(jnp.max(lc, axis=1, keepdims=True),
                                          jnp.max(lp, axis=1, keepdims=True)))
        pc = jnp.exp(lc - m)
        pp = jnp.exp(lp - m)
        l = jnp.sum(pc, axis=1, keepdims=True) + jnp.sum(pp, axis=1, keepdims=True)
        inv = 1.0 / l
        out = out + (_dot(pc * inv, vc, BTN) + _dot(pp * inv, vp, BTN)) * mask
        diag = (eye * (m + jnp.log(l))).reshape(nb * QB, QB)
        lb = lb + _spread(diag, jnp.broadcast_to(mask[0], (QB, w)))
    return out, lb.reshape(nb, QB, w)


NORM_ROWS = 128
NORM_UNROLL = 4
EPILOGUE_ROWS = 512
ATTN_BATCH_FWD = 8
ATTN_BATCH_BWD = 4


def _unit_rows(u, d):
    r = u & (d - 1)
    n = u >> (d.bit_length() - 1)

    def rows(blk):
        start = pl.multiple_of(blk * (QB * d), QB * d)
        return pl.ds(start, QB) if d == 1 else pl.ds(start + r, QB, stride=d)

    return rows(n), rows(jnp.maximum(n - 1, 0)), n == 0


def _unit_batch(i, nbatch, d, bias, qf, kf, vf):
    units = [_unit_rows(i * nbatch + j, d) for j in range(nbatch)]
    cur = lambda ref: jnp.stack([ref[c, :] for c, _, _ in units])
    prv = lambda ref: jnp.stack([ref[p, :] for _, p, _ in units])
    b_own = [b[0] for b in bias]
    b_prev = [jnp.stack([jnp.where(first, NEG, b[1]) for _, _, first in units]) for b in bias]
    return units, (cur(qf), cur(kf), prv(kf), cur(vf), prv(vf), b_own, b_prev)


def _q_norm(x, g):
    return _pair_norm(x, g * ATTN_SCALE)


def _attn_prologue(t, q_ref, k_ref, v_ref, qg_ref, kg_ref, qf, kf, vf):
    def chunk(c, carry):
        rows = pl.ds(pl.multiple_of(c * NORM_ROWS, NORM_ROWS), NORM_ROWS)
        qf[rows, :] = _q_norm(q_ref[rows, :].astype(F32), qg_ref[...])
        kf[rows, :] = _pair_norm(k_ref[rows, :].astype(F32), kg_ref[...])
        vf[rows, :] = v_ref[rows, :].astype(F32)
        return carry
    lax.fori_loop(0, t // NORM_ROWS, chunk, 0, unroll=NORM_UNROLL)


def _attn_specs(t, bases):
    w = 2 * HD
    ins = [pl.BlockSpec((t, w), functools.partial(lambda p, c, b: (0, b + p), b=b)) for b in bases]
    gain = pl.BlockSpec((1, w), lambda p, c: (0, 0))
    blk = pl.BlockSpec((t, w), lambda p, c: (0, p))
    return ins, gain, blk


def _attn_fwd(name, proj, bases, qg, kg, coefs, d):
    t = proj.shape[0]
    npairs = coefs.shape[0] // 2
    w = 2 * HD
    ins, gain, blk = _attn_specs(t, bases)

    def body(coef_ref, q_ref, k_ref, v_ref, qg_ref, kg_ref, o_ref, l_ref, qf, kf, vf):
        p = pl.program_id(0)
        bias = (_attn_bias(coef_ref[2 * p]), _attn_bias(coef_ref[2 * p + 1]))
        _attn_prologue(t, q_ref, k_ref, v_ref, qg_ref, kg_ref, qf, kf, vf)

        def step(i, carry):
            units, ins = _unit_batch(i, ATTN_BATCH_FWD, d, bias, qf, kf, vf)
            o, lb = _attn_pair(*ins)
            for j, (cur, _, _) in enumerate(units):
                o_ref[cur, :] = o[j]
                l_ref[cur, :] = lb[j]
            return carry

        lax.fori_loop(0, t // QB // ATTN_BATCH_FWD, step, 0)

    return pl.pallas_call(
        body, name=name,
        out_shape=[jax.ShapeDtypeStruct((t, npairs * w), F32)] * 2,
        grid_spec=pltpu.PrefetchScalarGridSpec(
            num_scalar_prefetch=1, grid=(npairs,),
            in_specs=ins + [gain, gain], out_specs=[blk, blk],
            scratch_shapes=[pltpu.VMEM((t, w), F32)] * 3),
        compiler_params=_params(("arbitrary",)),
    )(coefs, proj, proj, proj, qg, kg)


def _attn_bwd(name, proj, bases, qg, kg, coefs, d, do, dl):
    t = proj.shape[0]
    npairs = coefs.shape[0] // 2
    w = 2 * HD
    ins, gain, blk = _attn_specs(t, bases)

    def body(coef_ref, q_ref, k_ref, v_ref, qg_ref, kg_ref, do_ref, dl_ref,
             dq_ref, dk_ref, dv_ref, dqg_ref, dkg_ref, qf, kf, vf, dqf, dkf, dvf):
        p = pl.program_id(0)
        bias = (_attn_bias(coef_ref[2 * p]), _attn_bias(coef_ref[2 * p + 1]))
        _attn_prologue(t, q_ref, k_ref, v_ref, qg_ref, kg_ref, qf, kf, vf)
        dkf[...] = jnp.zeros_like(dkf)
        dvf[...] = jnp.zeros_like(dvf)

        def step(i, carry):
            units, ins = _unit_batch(i, ATTN_BATCH_BWD, d, bias, qf, kf, vf)
            f = lambda a, b, c, e, g: _attn_pair(a, b, c, e, g, *ins[5:])
            _, vjp = jax.vjp(f, *ins[:5])
            cot = (jnp.stack([do_ref[cur, :] for cur, _, _ in units]),
                   jnp.stack([dl_ref[cur, :] for cur, _, _ in units]))
            dq, dkc, dkp, dvc, dvp = vjp(cot)
            for j, (cur, prv, _) in enumerate(units):
                dqf[cur, :] = dq[j]
                dkf[cur, :] += dkc[j]
                dkf[prv, :] += dkp[j]
                dvf[cur, :] += dvc[j]
                dvf[prv, :] += dvp[j]
            return carry

        lax.fori_loop(0, t // QB // ATTN_BATCH_BWD, step, 0)

        def chunk(c, carry):
            dqg_acc, dkg_acc = carry
            rows = pl.ds(pl.multiple_of(c * EPILOGUE_ROWS, EPILOGUE_ROWS), EPILOGUE_ROWS)
            _, vq = jax.vjp(_q_norm, q_ref[rows, :].astype(F32), qg_ref[...])
            dq, dqg = vq(dqf[rows, :])
            _, vk = jax.vjp(_pair_norm, k_ref[rows, :].astype(F32), kg_ref[...])
            dk, dkg = vk(dkf[rows, :])
            dq_ref[rows, :] = dq.astype(dq_ref.dtype)
            dk_ref[rows, :] = dk.astype(dk_ref.dtype)
            dv_ref[rows, :] = dvf[rows, :].astype(dv_ref.dtype)
            return dqg_acc + dqg, dkg_acc + dkg

        zero = jnp.zeros((1, w), F32)
        dqg, dkg = lax.fori_loop(0, t // EPILOGUE_ROWS, chunk, (zero, zero))

        @pl.when(p == 0)
        def _():
            dqg_ref[...] = dqg
            dkg_ref[...] = dkg

        @pl.when(p > 0)
        def _():
            dqg_ref[...] += dqg
            dkg_ref[...] += dkg

    big = jax.ShapeDtypeStruct((t, npairs * w), ACT_DTYPE)
    small = jax.ShapeDtypeStruct((1, w), F32)
    return pl.pallas_call(
        body, name=name,
        out_shape=[big, big, big, small, small],
        grid_spec=pltpu.PrefetchScalarGridSpec(
            num_scalar_prefetch=1, grid=(npairs,),
            in_specs=ins + [gain, gain, blk, blk],
            out_specs=[blk, blk, blk, gain, gain],
            scratch_shapes=[pltpu.VMEM((t, w), F32)] * 6),
        compiler_params=_params(("arbitrary",)),
    )(coefs, proj, proj, proj, qg, kg, do, dl)


def _shift_down(u, s):
    if s == 0:
        return u
    rows = lax.broadcasted_iota(jnp.int32, u.shape, 0)
    return jnp.where(rows >= s, pltpu.roll(u, s, 0), 0.0)


def _shift_up(u, s):
    if s == 0:
        return u
    t = u.shape[0]
    rows = lax.broadcasted_iota(jnp.int32, u.shape, 0)
    return jnp.where(rows < t - s, pltpu.roll(u, t - s, 0), 0.0)


def _conv_pre(u, w, b):
    y = b
    for kk in range(SSD_K):
        y = y + w[kk:kk + 1, :] * _shift_down(u, SSD_K - 1 - kk)
    return y


def _conv_fwd(name, src, base, w, b, cw=128):
    t = src.shape[0]
    c = w.shape[1]

    def body(u_ref, w_ref, b_ref, o_ref):
        y = _conv_pre(u_ref[...].astype(F32), w_ref[...], b_ref[...])
        o_ref[...] = _silu(y).astype(o_ref.dtype)

    return pl.pallas_call(
        body, name=name, out_shape=jax.ShapeDtypeStruct((t, c), ACT_DTYPE),
        grid=(c // cw,),
        in_specs=[pl.BlockSpec((t, cw), lambda j: (0, base + j)),
                  pl.BlockSpec((SSD_K, cw), lambda j: (0, j)),
                  pl.BlockSpec((1, cw), lambda j: (0, j))],
        out_specs=pl.BlockSpec((t, cw), lambda j: (0, j)),
        compiler_params=_params(("parallel",)),
    )(src, w, b)


def _conv_bwd(name, src, base, w, b, dout, cw=128):
    t = src.shape[0]
    c = w.shape[1]

    def body(u_ref, w_ref, b_ref, d_ref, du_ref, dw_ref, db_ref):
        u = u_ref[...].astype(F32)
        wv = w_ref[...]
        y = _conv_pre(u, wv, b_ref[...])
        sg = jax.nn.sigmoid(y)
        dy = d_ref[...].astype(F32) * (sg * (1.0 + y * (1.0 - sg)))
        du = jnp.zeros_like(u)
        for kk in range(SSD_K):
            s = SSD_K - 1 - kk
            du = du + wv[kk:kk + 1, :] * _shift_up(dy, s)
            dw_ref[kk:kk + 1, :] = _colsum(dy * _shift_down(u, s))
        du_ref[...] = du.astype(du_ref.dtype)
        db_ref[...] = _colsum(dy)

    return pl.pallas_call(
        body, name=name,
        out_shape=[jax.ShapeDtypeStruct((t, c), ACT_DTYPE),
                   jax.ShapeDtypeStruct((SSD_K, c), F32),
                   jax.ShapeDtypeStruct((1, c), F32)],
        grid=(c // cw,),
        in_specs=[pl.BlockSpec((t, cw), lambda j: (0, base + j)),
                  pl.BlockSpec((SSD_K, cw), lambda j: (0, j)),
                  pl.BlockSpec((1, cw), lambda j: (0, j)),
                  pl.BlockSpec((t, cw), lambda j: (0, j))],
        out_specs=[pl.BlockSpec((t, cw), lambda j: (0, j)),
                   pl.BlockSpec((SSD_K, cw), lambda j: (0, j)),
                   pl.BlockSpec((1, cw), lambda j: (0, j))],
        compiler_params=_params(("parallel",)),
    )(src, w, b, dout)


def _softplus(x):
    return jnp.maximum(x, 0.0) + jnp.log(1.0 + jnp.exp(-jnp.abs(x)))


def _ssd_chunk(xbc, dtraw, bias, alog, states):
    wd = states[0].shape[1]
    nj = wd // SSD_P
    inner = SSD_G * wd
    dt = _softplus(dtraw + bias)
    a = dt * (-jnp.exp(alog))
    li = lax.broadcasted_iota(jnp.int32, (SSD_Q, SSD_Q), 0)
    si = lax.broadcasted_iota(jnp.int32, (SSD_Q, SSD_Q), 1)
    causal = li >= si
    acs = _running_sum(a, causal.astype(F32))
    acs_t = acs.T
    a_last = acs[SSD_Q - 1:SSD_Q, :]
    grow = jnp.exp(acs)
    shrink = jnp.exp(a_last - acs)
    hrow = lax.broadcasted_iota(jnp.int32, (LANE, wd), 0)
    wcol = lax.broadcasted_iota(jnp.int32, (LANE, wd), 1)
    lane = lax.broadcasted_iota(jnp.int32, (1, LANE), 1)
    ys, snext = [], []
    for g in range(SSD_G):
        lo = (hrow - g * nj) * SSD_P
        head_lanes = jnp.logical_and(wcol >= lo, wcol < lo + SSD_P).astype(F32)
        xs = xbc[:, g * wd:(g + 1) * wd]
        bm = xbc[:, inner + g * SSD_N:inner + (g + 1) * SSD_N]
        cm = xbc[:, inner + (SSD_G + g) * SSD_N:inner + (SSD_G + g + 1) * SSD_N]
        xdt = xs * _dot(dt, head_lanes)
        grow_x = _spread(grow, head_lanes)
        y_off = _dot(cm, states[g]) * grow_x
        s_new = (states[g] * grow_x[SSD_Q - 1:SSD_Q, :]
                 + _dot(bm, xdt * _dot(shrink, head_lanes), TN))
        cb = _dot(cm, bm, NT)
        pieces = []
        for i in range(wd // LANE):
            xp = xdt[:, i * LANE:(i + 1) * LANE]
            acc = jnp.zeros((SSD_Q, LANE), F32)
            for hh in range(LANE // SSD_P):
                h = g * nj + i * (LANE // SSD_P) + hh
                decay = jnp.exp(jnp.where(causal, acs[:, h:h + 1] - acs_t[h:h + 1, :], NEG))
                keep = jnp.logical_and(lane >= hh * SSD_P, lane < (hh + 1) * SSD_P).astype(F32)
                acc = acc + _dot(cb * decay, xp * keep)
            pieces.append(acc)
        y_diag = pieces[0] if len(pieces) == 1 else jnp.concatenate(pieces, axis=1)
        ys.append(y_diag + y_off)
        snext.append(s_new)
    return ys, snext


def _ssd_specs(cdim, wd, rev, nc):
    ch = (lambda c: nc - 1 - c) if rev else (lambda c: c)
    full = lambda width: pl.BlockSpec((SSD_Q, width), lambda c: (ch(c), 0))
    vec = pl.BlockSpec((1, LANE), lambda c: (0, 0))
    st = pl.BlockSpec((1, SSD_G, SSD_N, wd), lambda c: (ch(c), 0, 0, 0))
    return full, vec, st


def _ssd_fwd(name, xbc, dtraw, bias, alog, inner):
    t, cdim = xbc.shape
    wd = inner // SSD_G
    nc = t // SSD_Q
    full, vec, st = _ssd_specs(cdim, wd, False, nc)

    def body(x_ref, r_ref, b_ref, a_ref, y_ref, st_ref, s_scr):
        @pl.when(pl.program_id(0) == 0)
        def _():
            s_scr[...] = jnp.zeros_like(s_scr)

        sprev = [s_scr[g] for g in range(SSD_G)]
        ys, snext = _ssd_chunk(x_ref[...].astype(F32), r_ref[...], b_ref[...], a_ref[...], sprev)
        for g in range(SSD_G):
            st_ref[0, g] = sprev[g]
            y_ref[:, g * wd:(g + 1) * wd] = ys[g]
            s_scr[g] = snext[g]

    return pl.pallas_call(
        body, name=name,
        out_shape=[jax.ShapeDtypeStruct((t, inner), F32),
                   jax.ShapeDtypeStruct((nc, SSD_G, SSD_N, wd), F32)],
        grid=(nc,),
        in_specs=[full(cdim), full(LANE), vec, vec],
        out_specs=[full(inner), st],
        scratch_shapes=[pltpu.VMEM((SSD_G, SSD_N, wd), F32)],
        compiler_params=_params(("arbitrary",)),
    )(xbc, dtraw, bias, alog)


def _ssd_bwd(name, xbc, dtraw, bias, alog, states, dy, dxs_extra):
    t, cdim = xbc.shape
    inner = dy.shape[1]
    wd = inner // SSD_G
    nc = t // SSD_Q
    full, vec, st = _ssd_specs(cdim, wd, True, nc)

    def body(x_ref, r_ref, b_ref, a_ref, st_ref, dy_ref, dx0_ref,
             dx_ref, dr_ref, db_ref, da_ref, ds_scr):
        first = pl.program_id(0) == 0

        @pl.when(first)
        def _():
            ds_scr[...] = jnp.zeros_like(ds_scr)

        sprev = [st_ref[0, g] for g in range(SSD_G)]
        _, vjp = jax.vjp(_ssd_chunk, x_ref[...].astype(F32), r_ref[...], b_ref[...], a_ref[...],
                         sprev)
        dyv = dy_ref[...]
        dys = [dyv[:, g * wd:(g + 1) * wd] for g in range(SSD_G)]
        dsn = [ds_scr[g] for g in range(SSD_G)]
        dx, dr, db, da, dsp = vjp((dys, dsn))
        dx_ref[:, :inner] = dx[:, :inner] + dx0_ref[...].astype(F32)
        dx_ref[:, inner:] = dx[:, inner:]
        dr_ref[...] = dr
        for g in range(SSD_G):
            ds_scr[g] = dsp[g]

        @pl.when(first)
        def _():
            db_ref[...] = db
            da_ref[...] = da

        @pl.when(jnp.logical_not(first))
        def _():
            db_ref[...] += db
            da_ref[...] += da

    return pl.pallas_call(
        body, name=name,
        out_shape=[jax.ShapeDtypeStruct((t, cdim), F32),
                   jax.ShapeDtypeStruct((t, LANE), F32),
                   jax.ShapeDtypeStruct((1, LANE), F32),
                   jax.ShapeDtypeStruct((1, LANE), F32)],
        grid=(nc,),
        in_specs=[full(cdim), full(LANE), vec, vec, st, full(inner), full(inner)],
        out_specs=[full(cdim), full(LANE), vec, vec],
        scratch_shapes=[pltpu.VMEM((SSD_G, SSD_N, wd), F32)],
        compiler_params=_params(("arbitrary",)),
    )(xbc, dtraw, bias, alog, states, dy, dxs_extra)


def _mix(o0, o1, o2, l0, l1, l2):
    m = lax.stop_gradient(jnp.maximum(jnp.maximum(l0, l1), l2))
    e0, e1, e2 = jnp.exp(l0 - m), jnp.exp(l1 - m), jnp.exp(l2 - m)
    return (e0 * o0 + e1 * o1 + e2 * o2) / (e0 + e1 + e2)


def _gate(y, xs, z, dexp, gain):
    v = (y + xs.astype(F32) * dexp) * _silu(z.astype(F32))
    return _rms(v, gain)


def _merge(ga, gs, ap, sp):
    return jax.nn.sigmoid(ga.astype(F32)) * ap + jax.nn.sigmoid(gs.astype(F32)) * sp


def _alibi_coefs(hp):
    n = hp * len(PATTERNS)
    slopes = np.exp2(-ALIBI_MAX_EXP * np.arange(1, n + 1, dtype=np.float32) / n).astype(np.float32)
    return [jnp.asarray(slopes[g * hp:(g + 1) * hp] * np.float32(d))
            for g, (_, d) in enumerate(PATTERNS)]


def _local_step(x, tgt, w, p, gw_=None, aw=None):
    t, d = x.shape
    dff = w["gu1t"].shape[0] // 2
    aw = w["abt"].shape[1] if aw is None else aw
    hp = aw // HD
    qkv = len(PATTERNS) * aw
    inner = p["ssd_norm"].shape[1]
    nh = p["dt_bias"].shape[1]
    gw_ = {} if gw_ is None else gw_
    gw = inner // SSD_G
    cdim = inner + 2 * SSD_G * SSD_N
    z_off, xbc_off = 3 * qkv, 3 * qkv + inner
    ga_off = xbc_off + cdim
    gs_off = ga_off + d
    hw = d // 2
    assert z_off % gw == 0 and xbc_off % LANE == 0 and ga_off % hw == 0 and gs_off % hw == 0
    assert (nh // SSD_G) * SSD_P == gw and hp % 2 == 0 and aw % LANE == 0 and nh <= LANE
    gdt = MXU_DTYPE

    row = lambda a, width, base=0: ("row", a, width, base)
    const = lambda a, width, base=0: ("const", a, width, base)

    def rms_fwd(name, xin, g):
        return _rw(name, lambda xv, gv: (_rms(xv, gv),), [row(xin, d), const(g, d)],
                   [((d,), ACT_DTYPE)])[0]

    def rms_bwd(name, xin, g, dh, dres):
        def fn(xv, gv, dhv, drv):
            _, vjp = jax.vjp(_rms, xv, gv)
            dx, dg = vjp(dhv.astype(F32))
            return drv + dx, dg
        return _rw(name, fn, [row(xin, d), const(g, d), row(dh, d), row(dres, d)],
                   [((d,), F32)], accs=[(1, d)])

    def ffn_fwd(tag, xin, g, key_gu, key_d):
        h = rms_fwd(tag + "_norm", xin, g)
        gate, up, a = _ffn_up(tag + "_up", h, w[key_gu])
        xo = _mm(tag + "_down", a, w[key_d], "nn", F32, res=xin, scale=0.5)
        return xo, (h, gate, up, a)

    def ffn_bwd(tag, xin, g, wgut, wd, saved, dxo, key_gu, key_d):
        h, gate, up, a = saved
        gw_[key_d] = _mm(tag + "_dwd", a, dxo, "tn", gdt, scale=0.5)
        dgu = _ffn_dact(tag + "_dact", dxo, wd, gate, up, 0.5)
        gw_[key_gu] = _mm(tag + "_dwgu", dgu, h, "tn", gdt)
        dh = _mm(tag + "_dh", dgu, wgut, "nn", F32)
        return rms_bwd(tag + "_dnorm", xin, g, dh, dxo)

    x1, ffn1_saved = ffn_fwd("ffn1", x, p["ffn1_norm"], "gu1t", "d1")
    if hasattr(w, "after_first_ffn"):
        w.after_first_ffn()
    h2 = rms_fwd("mix_norm", x1, p["mix_norm"])
    proj = _mm("in_proj", h2, w["maint"], "nt", ACT_DTYPE, cap_m=512, cap_n=2944)
    dtraw = _mm("dt_proj", h2, w["dtt"], "nt", F32)

    coefs = _alibi_coefs(hp)
    qg2 = jnp.concatenate([p["q_norm"], p["q_norm"]], axis=1)
    kg2 = jnp.concatenate([p["k_norm"], p["k_norm"]], axis=1)
    pw = 2 * HD
    attn_bases = [[(off + gi * aw) // pw for off in (0, qkv, 2 * qkv)]
                  for gi in range(len(PATTERNS))]
    attn_o, attn_l = [], []
    for gi, (_, dil) in enumerate(PATTERNS):
        o, l = _attn_fwd(f"attn_fwd{gi}", proj, attn_bases[gi], qg2, kg2, coefs[gi], dil)
        attn_o.append(o)
        attn_l.append(l)
    ao = _rw("attn_mix", lambda *v: (_mix(*v),), [row(a, aw) for a in attn_o + attn_l],
             [((aw,), ACT_DTYPE)])[0]

    xbc = _conv_fwd("conv_fwd", proj, xbc_off // LANE, p["conv_w"], p["conv_b"])
    pad = lambda v: jnp.pad(v, ((0, 0), (0, LANE - nh)))
    bias_p, alog_p = pad(p["dt_bias"]), pad(p["a_log"])
    yssd, states = _ssd_fwd("ssd_fwd", xbc, dtraw, bias_p, alog_p, inner)
    dexp = jnp.repeat(p["d_skip"], SSD_P, axis=1)
    gate_ins = [row(yssd, gw), row(xbc, gw), row(proj, gw, z_off // gw),
                const(dexp, gw), const(p["ssd_norm"], gw)]
    yn = _rw("ssd_gate", lambda *v: (_gate(*v),), gate_ins, [((gw,), ACT_DTYPE)], ncb=SSD_G)[0]

    ap = _mm("attn_out", ao, w["abt"], "nt", F32)
    sp = _mm("ssd_out", yn, w["sb"], "nn", F32)
    merge_ins = [row(proj, hw, ga_off // hw), row(proj, hw, gs_off // hw), row(ap, hw), row(sp, hw)]
    mg = _rw("merge", lambda *v: (_merge(*v),), merge_ins, [((hw,), ACT_DTYPE)], ncb=2)[0]
    x2 = _mm("mix_out", mg, w["out"], "nn", F32, res=x1)
    x3, ffn2_saved = ffn_fwd("ffn2", x2, p["ffn2_norm"], "gu2t", "d2")

    def loss_fn(yv, tv):
        e = yv - tv
        return e * (1.0 / d), _colsum(e * e)
    dy, loss_vec = _rw("loss", loss_fn, [row(x3, d), row(tgt, d)], [((d,), F32)], accs=[(1, d)])

    gp = {}
    dx2, gp["ffn2_norm"] = ffn_bwd(
        "ffn2", x2, p["ffn2_norm"], w["gu2t"], w["d2"], ffn2_saved, dy, "gu2t", "d2")
    dmg = _mm("d_merge", dx2, w["out"], "nt", ACT_DTYPE)
    gw_["out"] = _mm("dw_out", mg, dx2, "tn", gdt)

    def merge_bwd(gav, gsv, apv, spv, dv):
        _, vjp = jax.vjp(_merge, gav, gsv, apv, spv)
        return vjp(dv.astype(F32))
    dga, dgs, dap, dsp = _rw("d_merge_gate", merge_bwd, merge_ins + [row(dmg, hw)],
                             [((hw,), ACT_DTYPE)] * 4, ncb=2)
    gw_["abt"] = _mm("dw_ab", dap, ao, "tn", gdt)
    dao = _mm("d_attn_o", dap, w["abt"], "nn", F32)
    gw_["sb"] = _mm("dw_sb", yn, dsp, "tn", gdt)
    dyn = _mm("d_ssd_y", dsp, w["sb"], "nt", F32)

    def gate_bwd(yv, xv, zv, dev, gv, dv):
        _, vjp = jax.vjp(_gate, yv, xv, zv, dev, gv)
        return vjp(dv)
    dyssd, dxs_gate, dz, ddexp, gp["ssd_norm"] = _rw(
        "d_ssd_gate", gate_bwd, gate_ins + [row(dyn, gw)],
        [((gw,), F32), ((gw,), F32), ((gw,), ACT_DTYPE)], accs=[(1, gw), (1, gw)], ncb=SSD_G)
    gp["d_skip"] = ddexp.reshape(nh, SSD_P).sum(axis=1).reshape(1, nh)

    dxbc, ddtraw, dbias, dalog = _ssd_bwd("ssd_bwd", xbc, dtraw, bias_p, alog_p, states,
                                          dyssd, dxs_gate)
    gp["dt_bias"], gp["a_log"] = dbias[:, :nh], dalog[:, :nh]
    du, gp["conv_w"], gp["conv_b"] = _conv_bwd("conv_bwd", proj, xbc_off // LANE,
                                               p["conv_w"], p["conv_b"], dxbc)

    def mix_bwd(*v):
        _, vjp = jax.vjp(_mix, *v[:6])
        return vjp(v[6])
    dmix = _rw("d_attn_mix", mix_bwd, [row(a, aw) for a in attn_o + attn_l] + [row(dao, aw)],
               [((aw,), F32)] * 6)
    dq, dk, dv = [], [], []
    dqg = dkg = None
    for gi, (_, dil) in enumerate(PATTERNS):
        r = _attn_bwd(f"attn_bwd{gi}", proj, attn_bases[gi], qg2, kg2, coefs[gi], dil,
                      dmix[gi], dmix[3 + gi])
        dq.append(r[0])
        dk.append(r[1])
        dv.append(r[2])
        dqg = r[3] if dqg is None else dqg + r[3]
        dkg = r[4] if dkg is None else dkg + r[4]
    gp["q_norm"] = dqg[:, :HD] + dqg[:, HD:]
    gp["k_norm"] = dkg[:, :HD] + dkg[:, HD:]

    segs = dq + dk + dv + [dz, du, dga, dgs]
    gw_["maint"] = _mm("dw_in", segs, h2, "tn", gdt)
    gw_["dtt"] = _mm("dw_dt", ddtraw, h2, "tn", gdt)
    dh2 = _mm("d_h2_main", segs, w["maint"], "nn", F32)
    dh2 = _mm("d_h2_dt", ddtraw, w["dtt"], "nn", F32, res=dh2)
    dx1, gp["mix_norm"] = rms_bwd("d_mix_norm", x1, p["mix_norm"], dh2, dx2)
    dx0, gp["ffn1_norm"] = ffn_bwd(
        "ffn1", x, p["ffn1_norm"], w["gu1t"], w["d1"], ffn1_saved, dx1, "gu1t", "d1")
    return loss_vec, dx0, gw_, gp


MESH = pl.DeviceIdType.MESH
HBM_SPEC = pl.BlockSpec(memory_space=pltpu.HBM)


def _mesh_pos():
    return lax.axis_index("x"), lax.axis_index("y"), lax.axis_index("c")


def _flip(pos, k):
    x, y, c = pos
    return (1 - x if k & 4 else x, 1 - y if k & 2 else y, 1 - c if k & 1 else c)


def _dev_index(pos):
    return 4 * pos[0] + 2 * pos[1] + pos[2]


def _rows_of(ref, base, stride, rows, pos):
    start = pl.multiple_of(base + stride * _dev_index(pos), ROW_ALIGN)
    return ref.at[pl.ds(start, rows)]


def _gather(name, shards, dests, out_shapes):
    n = len(shards)
    n_out = len(out_shapes)

    def body(*refs):
        x_refs = refs[:n]
        o_refs = refs[n:n + n_out]
        send_sems, recv_sems, local_sems = refs[n + n_out:]
        me = _mesh_pos()
        sibling = _flip(me, 1)
        chips = [_flip(me, 4), _flip(me, 2), _flip(me, 6)]

        def slot(i, block):
            k_out, base, stride = dests[i]
            return _rows_of(o_refs[k_out], base, stride, shards[i].shape[0], block)

        def copy(i, k, block, to, src=None):
            dst = slot(i, block)
            return pltpu.make_async_remote_copy(
                src_ref=dst if src is None else src, dst_ref=dst,
                send_sem=send_sems.at[7 * i + k], recv_sem=recv_sems.at[7 * i + k],
                device_id=to, device_id_type=MESH)

        mine = [pltpu.make_async_copy(x_refs[i], slot(i, me), local_sems.at[i]) for i in range(n)]
        for cp in mine:
            cp.start()
        first = []
        for i in range(n):
            first.append(copy(i, 0, me, sibling, src=x_refs[i]))
            first += [copy(i, 1 + j, me, chip, src=x_refs[i]) for j, chip in enumerate(chips)]
        for cp in first:
            cp.start()
        passed = []
        for j, chip in enumerate(chips):
            for i in range(n):
                copy(i, 1 + j, chip, me).wait_recv()
                fwd = copy(i, 4 + j, chip, sibling)
                fwd.start()
                passed.append(fwd)
        for i in range(n):
            copy(i, 0, sibling, me).wait_recv()
            for j, chip in enumerate(chips):
                copy(i, 4 + j, _flip(chip, 1), me).wait_recv()
        for cp in first + passed:
            cp.wait_send()
        for cp in mine:
            cp.wait()

    return pl.pallas_call(
        body, name=name,
        out_shape=[jax.ShapeDtypeStruct(s, dt) for s, dt in out_shapes],
        in_specs=[HBM_SPEC] * n, out_specs=[HBM_SPEC] * n_out,
        scratch_shapes=[pltpu.SemaphoreType.DMA((7 * n,)), pltpu.SemaphoreType.DMA((7 * n,)),
                        pltpu.SemaphoreType.DMA((n,))],
    )(*shards)


def _exchange(name, grads, srcs, small):
    n = len(srcs)
    ng = len(grads)

    def body(*refs):
        g_refs = refs[:ng]
        m_ref = refs[ng]
        r_refs = refs[ng + 1:ng + 1 + n]
        s_ref = refs[ng + 1 + n]
        send_sems, recv_sems, local_sems = refs[ng + 2 + n:]
        me = _mesh_pos()
        my = _dev_index(me)

        def slab(i, pos):
            gi, base, stride, rows = srcs[i]
            return _rows_of(g_refs[gi], base, stride, rows, pos)

        own = [pltpu.make_async_copy(slab(i, me), r_refs[i].at[my], local_sems.at[i])
               for i in range(n)]
        own.append(pltpu.make_async_copy(m_ref, s_ref.at[my], local_sems.at[n]))
        for cp in own:
            cp.start()

        def copies(k, src_pos, slot_pos):
            peer = _flip(me, k)
            si = _dev_index(slot_pos)
            out = [pltpu.make_async_remote_copy(
                src_ref=slab(i, src_pos), dst_ref=r_refs[i].at[si],
                send_sem=send_sems.at[7 * i + k - 1], recv_sem=recv_sems.at[7 * i + k - 1],
                device_id=peer, device_id_type=MESH) for i in range(n)]
            out.append(pltpu.make_async_remote_copy(
                src_ref=m_ref, dst_ref=s_ref.at[si],
                send_sem=send_sems.at[7 * n + k - 1], recv_sem=recv_sems.at[7 * n + k - 1],
                device_id=peer, device_id_type=MESH))
            return out

        sent = [cp for k in range(1, NDEV) for cp in copies(k, _flip(me, k), me)]
        for cp in sent:
            cp.start()
        for k in range(1, NDEV):
            for cp in copies(k, me, _flip(me, k)):
                cp.wait_recv()
        for cp in sent:
            cp.wait_send()
        for cp in own:
            cp.wait()

    out_shape = [jax.ShapeDtypeStruct((NDEV, rows, grads[gi].shape[1]), grads[gi].dtype)
                 for gi, _, _, rows in srcs]
    out_shape.append(jax.ShapeDtypeStruct((NDEV,) + small.shape, small.dtype))
    return pl.pallas_call(
        body, name=name, out_shape=out_shape,
        in_specs=[HBM_SPEC] * (ng + 1), out_specs=[HBM_SPEC] * (n + 1),
        scratch_shapes=[pltpu.SemaphoreType.DMA((7 * (n + 1),)),
                        pltpu.SemaphoreType.DMA((7 * (n + 1),)),
                        pltpu.SemaphoreType.DMA((n + 1,))],
    )(*grads, small)


SEM_SPEC = pl.BlockSpec(memory_space=pltpu.SEMAPHORE)
SIDE_EFFECT = pltpu.SideEffectType.DATAFLOW_SIDE_EFFECTING


def _split_refs(plan, srcs, lands, i, src_for, land_from):
    si, sbase, sstride, li, lbase, lstride, rows = plan[i]
    return (_rows_of(srcs[si], sbase, sstride, rows, src_for),
            _rows_of(lands[li], lbase, lstride, rows, land_from))


ALL_PEERS = tuple(range(1, NDEV))
SAME_CORE_AND_SIBLING = (1, 4, 2, 6)
OTHER_CHIPS = (4, 2, 6)


def _split_start(name, srcs, lands, plan, after=(), relations=ALL_PEERS):
    ns, nl, n = len(srcs), len(lands), len(plan)

    def body(*refs):
        s_refs = refs[:ns]
        l_refs = refs[ns:ns + nl]
        send_sems, recv_sems = refs[ns + nl + len(after):ns + nl + len(after) + 2]
        local_sems = refs[ns + nl + len(after) + 2]
        token = refs[ns + nl + len(after) + 3 + ns + nl]
        me = _mesh_pos()
        for i in range(n):
            src, dst = _split_refs(plan, s_refs, l_refs, i, me, me)
            pltpu.make_async_copy(src, dst, local_sems.at[i]).start()
        for k in relations:
            peer = _flip(me, k)
            for i in range(n):
                src, dst = _split_refs(plan, s_refs, l_refs, i, peer, me)
                pltpu.make_async_remote_copy(
                    src_ref=src, dst_ref=dst,
                    send_sem=send_sems.at[7 * i + k - 1], recv_sem=recv_sems.at[7 * i + k - 1],
                    device_id=peer, device_id_type=MESH).start()
        token[...] = jnp.zeros_like(token)

    hbm = lambda a: pltpu.HBM(a.shape, a.dtype)
    out_shape = ((pltpu.SemaphoreType.DMA((7 * n,)), pltpu.SemaphoreType.DMA((7 * n,)),
                  pltpu.SemaphoreType.DMA((n,)))
                 + tuple(hbm(a) for a in srcs) + tuple(hbm(a) for a in lands)
                 + (jax.ShapeDtypeStruct((8, LANE), F32),))
    out = pl.pallas_call(
        body, name=name, out_shape=out_shape,
        in_specs=[HBM_SPEC] * (ns + nl) + [ANY_SPEC] * len(after),
        out_specs=(SEM_SPEC, SEM_SPEC, SEM_SPEC) + (HBM_SPEC,) * (ns + nl)
        + (pl.BlockSpec(memory_space=pltpu.VMEM),),
        input_output_aliases={i: 3 + i for i in range(ns + nl)},
        compiler_params=pltpu.CompilerParams(has_side_effects=SIDE_EFFECT),
    )(*[pltpu.with_memory_space_constraint(a, pltpu.HBM) for a in tuple(srcs) + tuple(lands)],
      *after)
    _Order.tokens.append(out[-1])
    return out[0], out[1], out[2], out[3:3 + ns], out[3 + ns:3 + ns + nl]


def _split_wait(name, started, plan, relations=ALL_PEERS):
    send_sems, recv_sems, local_sems, srcs, lands = started
    ns, nl, n = len(srcs), len(lands), len(plan)
    after = [_Order.last] if _Order.last is not None else []

    def body(*refs):
        s_refs = refs[:ns]
        l_refs = refs[ns:ns + nl]
        send_sems, recv_sems, local_sems = refs[ns + nl:ns + nl + 3]
        me = _mesh_pos()
        for i in range(n):
            src, dst = _split_refs(plan, s_refs, l_refs, i, me, me)
            pltpu.make_async_copy(src, dst, local_sems.at[i]).wait()
        for k in relations:
            peer = _flip(me, k)
            for i in range(n):
                src, dst = _split_refs(plan, s_refs, l_refs, i, peer, peer)
                cp = pltpu.make_async_remote_copy(
                    src_ref=src, dst_ref=dst,
                    send_sem=send_sems.at[7 * i + k - 1], recv_sem=recv_sems.at[7 * i + k - 1],
                    device_id=peer, device_id_type=MESH)
                cp.wait_send()
                cp.wait_recv()

    hbm = lambda a: pltpu.HBM(a.shape, a.dtype)
    out = pl.pallas_call(
        body, name=name,
        out_shape=tuple(hbm(a) for a in srcs) + tuple(hbm(a) for a in lands),
        in_specs=[HBM_SPEC] * (ns + nl) + [SEM_SPEC] * 3 + [ANY_SPEC] * len(after),
        out_specs=(HBM_SPEC,) * (ns + nl),
        input_output_aliases={i: i for i in range(ns + nl)},
        compiler_params=pltpu.CompilerParams(has_side_effects=SIDE_EFFECT),
    )(*srcs, *lands, send_sems, recv_sems, local_sems, *after)
    return list(out[ns:])


def _forward_refs(plan, lands, i, block):
    _, _, _, li, lbase, lstride, rows = plan[i]
    return _rows_of(lands[li], lbase, lstride, rows, block)


def _forward_start(name, lands, plan):
    nl, n = len(lands), len(plan)

    def body(*refs):
        l_refs = refs[:nl]
        send_sems, recv_sems = refs[nl:nl + 2]
        token = refs[nl + 2 + nl]
        me = _mesh_pos()
        for j, kc in enumerate(OTHER_CHIPS):
            for i in range(n):
                rows = _forward_refs(plan, l_refs, i, _flip(me, kc))
                pltpu.make_async_remote_copy(
                    src_ref=rows, dst_ref=rows,
                    send_sem=send_sems.at[3 * i + j], recv_sem=recv_sems.at[3 * i + j],
                    device_id=_flip(me, 1), device_id_type=MESH).start()
        token[...] = jnp.zeros_like(token)

    hbm = lambda a: pltpu.HBM(a.shape, a.dtype)
    out = pl.pallas_call(
        body, name=name,
        out_shape=((pltpu.SemaphoreType.DMA((3 * n,)), pltpu.SemaphoreType.DMA((3 * n,)))
                   + tuple(hbm(a) for a in lands) + (jax.ShapeDtypeStruct((8, LANE), F32),)),
        in_specs=[HBM_SPEC] * nl,
        out_specs=(SEM_SPEC, SEM_SPEC) + (HBM_SPEC,) * nl
        + (pl.BlockSpec(memory_space=pltpu.VMEM),),
        input_output_aliases={i: 2 + i for i in range(nl)},
        compiler_params=pltpu.CompilerParams(has_side_effects=SIDE_EFFECT),
    )(*[pltpu.with_memory_space_constraint(a, pltpu.HBM) for a in lands])
    _Order.tokens.append(out[-1])
    return out[0], out[1], out[2:2 + nl]


def _forward_wait(name, started, plan):
    send_sems, recv_sems, lands = started
    nl, n = len(lands), len(plan)
    after = [_Order.last] if _Order.last is not None else []

    def body(*refs):
        l_refs = refs[:nl]
        send_sems, recv_sems = refs[nl:nl + 2]
        me = _mesh_pos()
        for j, kc in enumerate(OTHER_CHIPS):
            for i in range(n):
                sent = _forward_refs(plan, l_refs, i, _flip(me, kc))
                came = _forward_refs(plan, l_refs, i, _flip(_flip(me, 1), kc))
                cp = pltpu.make_async_remote_copy(
                    src_ref=sent, dst_ref=came,
                    send_sem=send_sems.at[3 * i + j], recv_sem=recv_sems.at[3 * i + j],
                    device_id=_flip(me, 1), device_id_type=MESH)
                cp.wait_send()
                cp.wait_recv()

    hbm = lambda a: pltpu.HBM(a.shape, a.dtype)
    out = pl.pallas_call(
        body, name=name, out_shape=tuple(hbm(a) for a in lands),
        in_specs=[HBM_SPEC] * nl + [SEM_SPEC] * 2 + [ANY_SPEC] * len(after),
        out_specs=(HBM_SPEC,) * nl,
        input_output_aliases={i: i for i in range(nl)},
        compiler_params=pltpu.CompilerParams(has_side_effects=SIDE_EFFECT),
    )(*lands, send_sems, recv_sems, *after)
    return list(out)


def _regroup_rows(name, padded, r, rp, lo, hi):
    d = padded.shape[1]
    pack = 4 // padded.dtype.itemsize
    assert r % pack == 0 and rp % ROW_ALIGN == 0 and lo % (8 * pack) == 0 and hi % (8 * pack) == 0
    r2, rp2, lo2, hi2 = r // pack, rp // pack, lo // pack, hi // pack
    u32 = jnp.uint32

    def body(x_ref, main_ref, cut_ref):
        x = pltpu.bitcast(x_ref[...], u32)
        joined = jnp.concatenate([x[rp2 * j:rp2 * j + r2] for j in range(NDEV)], axis=0)
        main = jnp.concatenate([joined[:lo2], joined[hi2:]], axis=0)
        cut = jnp.concatenate([joined[lo2:hi2], jnp.zeros((LANE // pack - (hi2 - lo2), LANE), u32)],
                              axis=0)
        main_ref[...] = pltpu.bitcast(main, padded.dtype)
        cut_ref[...] = pltpu.bitcast(cut, padded.dtype)

    return pl.pallas_call(
        body, name=name,
        out_shape=[jax.ShapeDtypeStruct((NDEV * r - (hi - lo), d), padded.dtype),
                   jax.ShapeDtypeStruct((LANE, d), padded.dtype)],
        grid=(d // LANE,),
        in_specs=[pl.BlockSpec((NDEV * rp, LANE), lambda i: (0, i))],
        out_specs=[pl.BlockSpec((NDEV * r - (hi - lo), LANE), lambda i: (0, i)),
                   pl.BlockSpec((LANE, LANE), lambda i: (0, i))],
        compiler_params=_params(("parallel",)),
    )(padded)


def _ungroup_rows(name, main, cut, r, rp, lo, hi):
    d = main.shape[1]
    pack = 4 // main.dtype.itemsize
    r2, rp2, lo2, hi2 = r // pack, rp // pack, lo // pack, hi // pack
    u32 = jnp.uint32

    def body(main_ref, cut_ref, o_ref):
        m = pltpu.bitcast(main_ref[...], u32)
        c = pltpu.bitcast(cut_ref[...], u32)
        joined = jnp.concatenate([m[:lo2], c[:hi2 - lo2], m[lo2:]], axis=0)
        zeros = jnp.zeros((rp2 - r2, LANE), u32)
        parts = []
        for j in range(NDEV):
            parts += [joined[r2 * j:r2 * (j + 1)], zeros]
        o_ref[...] = pltpu.bitcast(jnp.concatenate(parts, axis=0), main.dtype)

    return pl.pallas_call(
        body, name=name,
        out_shape=jax.ShapeDtypeStruct((NDEV * rp, d), main.dtype),
        grid=(d // LANE,),
        in_specs=[pl.BlockSpec((main.shape[0], LANE), lambda i: (0, i)),
                  pl.BlockSpec((LANE, LANE), lambda i: (0, i))],
        out_specs=pl.BlockSpec((NDEV * rp, LANE), lambda i: (0, i)),
        compiler_params=_params(("parallel",)),
    )(main, cut)


def _sum_slabs(name, a):
    s, r, c = a.shape

    def body(a_ref, o_ref):
        acc = a_ref[0].astype(F32)
        for i in range(1, s):
            acc = acc + a_ref[i].astype(F32)
        o_ref[...] = acc

    return pl.pallas_call(body, name=name, out_shape=jax.ShapeDtypeStruct((r, c), F32))(a)


def _adamw_update(g, w, m, v):
    mn = ADAM_B1 * m + (1.0 - ADAM_B1) * g
    vn = ADAM_B2 * v + (1.0 - ADAM_B2) * (g * g)
    m_hat = mn / (1.0 - ADAM_B1 ** ADAM_STEP)
    v_hat = vn / (1.0 - ADAM_B2 ** ADAM_STEP)
    delta = -ADAM_LR * (m_hat / (jnp.sqrt(v_hat) + ADAM_EPS) + ADAM_WD * w)
    return delta, mn, vn


def _adamw(name, gsrc, w, m, v, transposed=False, tr=256):
    s = gsrc.shape[0]
    lead = w.ndim == 3
    r, c = w.shape[-2:]
    step = LANE if transposed else 8
    tr = max(t for t in range(step, min(tr, r) + 1, step) if r % t == 0)

    def body(g_ref, w_ref, m_ref, v_ref, go_ref, d_ref, mo_ref, vo_ref):
        g = g_ref[0].astype(F32)
        for i in range(1, s):
            g = g + g_ref[i].astype(F32)
        if transposed:
            g = g.T[:, :c]
        delta, mn, vn = _adamw_update(g, w_ref[...], m_ref[...], v_ref[...])
        go_ref[...] = g
        d_ref[...] = delta
        mo_ref[...] = mn
        vo_ref[...] = vn

    if lead:
        blk = pl.BlockSpec((None, tr, c), lambda i: (0, i, 0))
    else:
        blk = pl.BlockSpec((tr, c), lambda i: (i, 0))
    if transposed:
        g_spec = pl.BlockSpec((s, gsrc.shape[1], tr), lambda i: (0, 0, i))
    else:
        g_spec = pl.BlockSpec((s, tr, c), lambda i: (0, i, 0))
    return pl.pallas_call(
        body, name=name, out_shape=[jax.ShapeDtypeStruct(w.shape, F32)] * 4,
        grid=(r // tr,),
        in_specs=[g_spec, blk, blk, blk], out_specs=[blk] * 4,
        compiler_params=_params(("parallel",)),
    )(gsrc, w, m, v)


REPLICATED = ("ffn1_norm", "mix_norm", "q_norm", "k_norm", "conv_b", "dt_bias", "a_log",
              "d_skip", "ssd_norm", "ffn2_norm")
ALL_WEIGHTS = ("ffn1_norm", "ffn1_w_gate", "ffn1_w_up", "ffn1_w_down", "mix_norm", "w_in",
               "q_norm", "k_norm", "conv_w", "conv_b", "dt_bias", "a_log", "d_skip", "ssd_norm",
               "w_attn_branch", "w_ssd_branch", "w_out", "ffn2_norm", "ffn2_w_gate", "ffn2_w_up",
               "ffn2_w_down")
BIG = (("ffn1_w_gate", True, "gu1t", 0), ("ffn1_w_up", True, "gu1t", 1),
       ("ffn1_w_down", False, "d1", 0), ("w_in", True, "wint", 0),
       ("w_attn_branch", True, "abt", 0), ("w_ssd_branch", False, "sb", 0),
       ("w_out", False, "out", 0),
       ("ffn2_w_gate", True, "gu2t", 0), ("ffn2_w_up", True, "gu2t", 1),
       ("ffn2_w_down", False, "d2", 0))


def _nrows(shape, cols):
    return -(-math.prod(shape) // cols)


def _pack_rows(arrs, cols, row_tile):
    parts = []
    for a in arrs:
        flat = a.reshape(-1)
        nr = -(-flat.shape[0] // cols)
        parts.append(jnp.pad(flat, (0, nr * cols - flat.shape[0])).reshape(nr, cols))
    out = jnp.concatenate(parts, axis=0)
    return jnp.pad(out, ((0, _round_up(out.shape[0], row_tile) - out.shape[0]), (0, 0)))


def _unpack_rows(packed, shapes):
    cols = packed.shape[-1]
    out, r0 = [], 0
    for sh in shapes:
        nr = _nrows(sh, cols)
        out.append(packed[r0:r0 + nr].reshape(-1)[:math.prod(sh)].reshape(tuple(sh)))
        r0 += nr
    return out


def kernel(x, ffn1_norm, ffn1_w_gate, ffn1_w_up, ffn1_w_down, mix_norm, w_in, q_norm, k_norm, conv_w, conv_b, dt_bias, a_log, d_skip, ssd_norm, w_attn_branch, w_ssd_branch, w_out, ffn2_norm, ffn2_w_gate, ffn2_w_up, ffn2_w_down, loss_target, m_ffn1_norm, m_ffn1_w_gate, m_ffn1_w_up, m_ffn1_w_down, m_mix_norm, m_w_in, m_q_norm, m_k_norm, m_conv_w, m_conv_b, m_dt_bias, m_a_log, m_d_skip, m_ssd_norm, m_w_attn_branch, m_w_ssd_branch, m_w_out, m_ffn2_norm, m_ffn2_w_gate, m_ffn2_w_up, m_ffn2_w_down, v_ffn1_norm, v_ffn1_w_gate, v_ffn1_w_up, v_ffn1_w_down, v_mix_norm, v_w_in, v_q_norm, v_k_norm, v_conv_w, v_conv_b, v_dt_bias, v_a_log, v_d_skip, v_ssd_norm, v_w_attn_branch, v_w_ssd_branch, v_w_out, v_ffn2_norm, v_ffn2_w_gate, v_ffn2_w_up, v_ffn2_w_down):
    given = dict(locals())
    wts = {n: given[n] for n in ALL_WEIGHTS}
    mom = {n: given["m_" + n] for n in ALL_WEIGHTS}
    var = {n: given["v_" + n] for n in ALL_WEIGHTS}
    d = x.shape[-1]
    nh = dt_bias.shape[1]
    my = _dev_index(_mesh_pos())

    def row_form(n, col_sharded):
        a = wts[n][0].T if col_sharded else wts[n][0]
        a = jnp.pad(a, ((0, _round_up(a.shape[0], ROW_ALIGN) - a.shape[0]), (0, 0)))
        return a.astype(MXU_DTYPE)

    _Order.tokens, _Order.last = [], None
    shard = {n: row_form(n, cs) for n, cs, _, _ in BIG}
    entries = {buf: [e for e in BIG if e[2] == buf] for buf in dict.fromkeys(e[2] for e in BIG)}

    def buf_shape(buf):
        r, c = shard[entries[buf][0][0]].shape
        return (len(entries[buf]) * NDEV * r, c)

    def gather_plan(bufs):
        srcs, lands, plan = [], [], []
        for li, buf in enumerate(bufs):
            lands.append(lax.empty(buf_shape(buf), MXU_DTYPE))
            for n, _, _, pos in entries[buf]:
                r = shard[n].shape[0]
                plan.append((len(srcs), 0, 0, li, pos * NDEV * r, r, r))
                srcs.append(shard[n])
        return srcs, lands, plan

    def scatter_plan(bufs, grads):
        srcs, lands, plan, names = [], [], [], []
        for si, buf in enumerate(bufs):
            srcs.append(grads[buf])
            for n, _, _, pos in entries[buf]:
                r, c = shard[n].shape
                plan.append((si, pos * NDEV * r, r, len(lands), 0, r, r))
                lands.append(lax.empty((NDEV * r, c), MXU_DTYPE))
                names.append(n)
        return srcs, lands, plan, names

    first_bufs = ("gu1t", "d1")
    shards, dests, out_shapes = [], [], []
    for bi, buf in enumerate(first_bufs):
        out_shapes.append((buf_shape(buf), MXU_DTYPE))
        for n, _, _, pos in entries[buf]:
            r = shard[n].shape[0]
            shards.append(shard[n])
            dests.append((bi, pos * NDEV * r, r))
    conv_rows = _pack_rows([conv_w[0]], LANE, ROW_ALIGN)
    shards.append(conv_rows)
    dests.append((len(first_bufs), 0, conv_rows.shape[0]))
    out_shapes.append(((NDEV * conv_rows.shape[0], LANE), F32))
    gathered = _gather("gather_first", shards, dests, out_shapes)

    in_cols = w_in.shape[2]
    in_pad = _round_up(in_cols, ROW_ALIGN)
    dt_off = NDEV * in_cols - 2 * d - nh
    second_bufs = ("wint",)
    third_bufs = ("abt", "sb", "out", "gu2t", "d2")
    plan2 = gather_plan(second_bufs)
    started2 = _split_start("gather_in_start", *plan2, after=[gathered[0]],
                            relations=SAME_CORE_AND_SIBLING)
    forwarded, started3 = [], []

    class Weights(dict):
        def after_first_ffn(self):
            lands = _split_wait("gather_in_wait", started2, plan2[2],
                                relations=SAME_CORE_AND_SIBLING)
            forwarded.append(_forward_start("gather_in_forward", lands, plan2[2]))

        def __missing__(self, key):
            if key in ("maint", "dtt"):
                wint = _forward_wait("gather_in_arrive", forwarded[0], plan2[2])[0]
                plan3 = gather_plan(third_bufs)
                started3.append((_split_start("gather_rest_start", *plan3, after=[wint]), plan3[2]))
                self["maint"], self["dtt"] = _regroup_rows(
                    "regroup_w_in", wint, in_cols, in_pad, dt_off, dt_off + nh)
            else:
                st, plan = started3[0]
                for buf, a in zip(third_bufs, _split_wait("gather_rest_wait", st, plan)):
                    self[buf] = a
            return self[key]

    w = Weights(gu1t=gathered[0], d1=gathered[1])
    p = {n: wts[n] for n in REPLICATED}
    conv_all = gathered[-1].reshape(NDEV, conv_rows.shape[0] * LANE)[:, :math.prod(conv_w.shape[1:])]
    p["conv_w"] = (conv_all.reshape((NDEV,) + conv_w.shape[1:]).transpose(1, 0, 2)
                   .reshape(conv_w.shape[1], NDEV * conv_w.shape[2]))

    groups = (("scatter_late", ("gu2t", "d2", "out", "abt", "sb")),
              ("scatter_in", ("maint", "dtt")),
              ("scatter_first", ("gu1t", "d1")))
    in_flight = []

    class Grads(dict):
        def __setitem__(self, key, value):
            dict.__setitem__(self, key, value)
            for tag, need in groups:
                if key in need and all(k in self for k in need):
                    if tag == "scatter_in":
                        gwin = _ungroup_rows("ungroup_w_in", self["maint"], self["dtt"],
                                             in_cols, in_pad, dt_off, dt_off + nh)
                        bufs, grads = ("wint",), {"wint": gwin}
                    else:
                        bufs, grads = need, self
                    srcs, lands, plan, names = scatter_plan(bufs, grads)
                    in_flight.append((tag, _split_start(tag + "_start", srcs, lands, plan),
                                      plan, names))

    loss_vec, dx, gw, gp = _local_step(x[0], loss_target[0], w, p, Grads(),
                                       aw=w_attn_branch.shape[1])

    small_names = REPLICATED + ("conv_w",)
    small_shapes = [gp[n].shape for n in small_names]
    small = _pack_rows([gp[n] for n in small_names], LANE, 8)
    small_all = _exchange("exchange_small", [], [], small)[0]

    outs = [{}, {}, {}, {}]
    col_sharded_of = {n: cs for n, cs, _, _ in BIG}
    for tag, started, plan, names in in_flight:
        for n, rv in zip(names, _split_wait(tag + "_wait", started, plan)):
            rv = rv.reshape(NDEV, shard[n].shape[0], shard[n].shape[1])
            if col_sharded_of[n] and rv.shape[1] == wts[n].shape[2]:
                res = _adamw("adamw_" + n, rv, wts[n][0].T, mom[n][0].T, var[n][0].T)
                res = [a.T for a in res]
            else:
                res = _adamw("adamw_" + n, rv, wts[n][0], mom[n][0], var[n][0],
                             transposed=col_sharded_of[n])
            _Order.done(res)
            for k in range(4):
                outs[k][n] = res[k][None]

    small_g = _unpack_rows(_sum_slabs("sum_small_grads", small_all), small_shapes)
    small_g = dict(zip(small_names, small_g))
    cs = conv_w.shape[2]
    small_g["conv_w"] = lax.dynamic_slice_in_dim(small_g["conv_w"], my * cs, cs, axis=1)
    small_shard_shapes = [wts[n].shape[-2:] for n in small_names]
    sg = _pack_rows([small_g[n] for n in small_names], LANE, 8)
    sw = _pack_rows([wts[n] for n in small_names], LANE, 8)
    sm = _pack_rows([mom[n] for n in small_names], LANE, 8)
    sv = _pack_rows([var[n] for n in small_names], LANE, 8)
    res_small = _adamw("adamw_small", sg[None], sw, sm, sv, tr=sg.shape[0])
    for k in range(4):
        for n, a in zip(small_names, _unpack_rows(res_small[k], small_shard_shapes)):
            outs[k][n] = a.reshape(wts[n].shape)

    loss = lax.psum(0.5 * jnp.sum(loss_vec) / d, ("x", "y", "c"))
    result = [loss, dx[None]]
    for k in range(4):
        result += [outs[k][n] for n in ALL_WEIGHTS]
    return tuple(result)
```

```python
import functools
import math

import numpy as np
import jax
import jax.numpy as jnp
from jax import lax
from jax.experimental import pallas as pl
from jax.experimental.pallas import tpu as pltpu

F32 = jnp.float32
BF16 = jnp.bfloat16
MXU_DTYPE = BF16
ACT_DTYPE = BF16

NDEV = 8
EPS = 1e-6
HD = 64
QB = 128
PATTERNS = ((128, 1), (512, 4), (2048, 16))
ALIBI_MAX_EXP = 8.0
SSD_P = 64
SSD_N = 128
SSD_G = 4
SSD_Q = 128
SSD_K = 4
NEG = -1e30
LANE = 128
ROW_ALIGN = 16
VMEM_LIMIT = 56 * 1024 * 1024

ADAM_LR, ADAM_B1, ADAM_B2, ADAM_EPS, ADAM_WD, ADAM_STEP = 0.001, 0.9, 0.999, 1e-8, 0.01, 10

NN = (((1,), (0,)), ((), ()))
NT = (((1,), (1,)), ((), ()))
TN = (((0,), (0,)), ((), ()))


BNN = (((2,), (1,)), ((0,), (0,)))
BNT = (((2,), (2,)), ((0,), (0,)))
BTN = (((1,), (1,)), ((0,), (0,)))
_DOT_GRADS = {
    NN: (("g", "b", NT), ("a", "g", TN)),
    NT: (("g", "b", NN), ("g", "a", TN)),
    TN: (("b", "g", NT), ("a", "g", NN)),
    BNT: (("g", "b", BNN), ("g", "a", BTN)),
    BTN: (("b", "g", BNT), ("a", "g", BNN)),
}


def _mxu(a, b, dims):
    return lax.dot_general(a.astype(MXU_DTYPE), b.astype(MXU_DTYPE), dims,
                           preferred_element_type=F32)


@functools.partial(jax.custom_vjp, nondiff_argnums=(2,))
def _dot_vjp(a, b, dims):
    return _mxu(a, b, dims)


def _dot_vjp_fwd(a, b, dims):
    return _mxu(a, b, dims), (a.astype(MXU_DTYPE), b.astype(MXU_DTYPE))


def _dot_vjp_bwd(dims, res, g):
    ops = {"a": res[0], "b": res[1], "g": g}
    (x1, y1, d1), (x2, y2, d2) = _DOT_GRADS[dims]
    return _mxu(ops[x1], ops[y1], d1), _mxu(ops[x2], ops[y2], d2)


_dot_vjp.defvjp(_dot_vjp_fwd, _dot_vjp_bwd)


def _dot(a, b, dims=NN):
    return _dot_vjp(a, b, dims)


def _split3(a):
    hi = a.astype(BF16)
    r = a - hi.astype(F32)
    mid = r.astype(BF16)
    lo = (r - mid.astype(F32)).astype(BF16)
    return hi, mid, lo


def _dot3(a, b, dims=NN, split=0):
    if split == 0:
        bb = b.astype(BF16)
        parts = [lax.dot_general(s, bb, dims, preferred_element_type=F32) for s in _split3(a)]
    else:
        aa = a.astype(BF16)
        parts = [lax.dot_general(aa, s, dims, preferred_element_type=F32) for s in _split3(b)]
    return parts[0] + parts[1] + parts[2]


@jax.custom_vjp
def _spread(v, e):
    return _dot3(v, e)


def _spread_fwd(v, e):
    return _dot3(v, e), e


def _spread_bwd(e, g):
    return _dot3(g, e, NT), jnp.zeros_like(e)


_spread.defvjp(_spread_fwd, _spread_bwd)


@jax.custom_vjp
def _running_sum(a, lower):
    return _dot3(lower, a, NN, split=1)


def _running_sum_fwd(a, lower):
    return _dot3(lower, a, NN, split=1), lower


def _running_sum_bwd(lower, g):
    return _dot3(lower, g, TN, split=1), jnp.zeros_like(lower)


_running_sum.defvjp(_running_sum_fwd, _running_sum_bwd)


def _tile(n, cap):
    if n <= cap:
        return n
    best = None
    for t in range(LANE, cap + 1, LANE):
        if n % t == 0:
            best = t
    assert best is not None, (n, cap)
    return best


def _params(sem):
    return pltpu.CompilerParams(dimension_semantics=sem, vmem_limit_bytes=VMEM_LIMIT)


def _round_up(n, m):
    return -(-n // m) * m


class _Order:
    tokens = []
    last = None

    @classmethod
    def take(cls):
        out, cls.tokens = cls.tokens, []
        return out

    @classmethod
    def done(cls, result):
        cls.last = result[0] if isinstance(result, (list, tuple)) else result
        return result


ANY_SPEC = pl.BlockSpec(memory_space=pl.ANY)


def _mm(name, a, b, mode, out_dtype=F32, res=None, scale=1.0,
        cap_m=1408, cap_n=1408, cap_k=1408):
    segs = list(a) if isinstance(a, (list, tuple)) else [a]
    nseg = len(segs)
    if mode == "tn":
        k = segs[0].shape[0]
        widths = [s.shape[1] for s in segs]
        m = sum(widths)
        k2, n = b.shape
        tm = _tile(math.gcd(*widths), cap_m)
        tk = _tile(k, cap_k)
        counts = [wd // tm for wd in widths]
    else:
        m = segs[0].shape[0]
        widths = [s.shape[1] for s in segs]
        k = sum(widths)
        (k2, n) = b.shape if mode == "nn" else b.shape[::-1]
        tm = _tile(m, cap_m)
        tk = _tile(math.gcd(*widths), cap_k)
        counts = [wd // tk for wd in widths]
    assert k == k2, (name, [s.shape for s in segs], b.shape, mode)
    tn = _tile(n, cap_n)
    nk = k // tk
    starts = [sum(counts[:s]) for s in range(nseg)]
    dims = {"nn": NN, "nt": NT, "tn": TN}[mode]

    def a_spec(s):
        lo, cnt = starts[s], counts[s]
        if mode == "tn":
            if nseg == 1:
                return pl.BlockSpec((tk, tm), lambda i, j, kk: (kk, i))
            return pl.BlockSpec(
                (tk, tm), lambda i, j, kk: (jnp.where((i >= lo) & (i < lo + cnt), kk, 0),
                                            jnp.clip(i - lo, 0, cnt - 1)))
        if nseg == 1:
            return pl.BlockSpec((tm, tk), lambda i, j, kk: (i, kk))
        return pl.BlockSpec((tm, tk), lambda i, j, kk: (i, jnp.clip(kk - lo, 0, cnt - 1)))

    b_spec = (pl.BlockSpec((tn, tk), lambda i, j, kk: (j, kk)) if mode == "nt"
              else pl.BlockSpec((tk, tn), lambda i, j, kk: (kk, j)))
    o_spec = pl.BlockSpec((tm, tn), lambda i, j, kk: (i, j))
    has_res = res is not None
    use_acc = nk > 1 or nseg > 1
    ties = _Order.take()
    nt_ = len(ties)

    def body(*refs):
        a_refs = refs[:nseg]
        b_ref = refs[nseg]
        r_ref = refs[nseg + 1] if has_res else None
        o_ref = refs[nseg + 1 + has_res + nt_]
        scr = refs[nseg + 2 + has_res + nt_:]

        def finish(acc):
            if scale != 1.0:
                acc = acc * scale
            if has_res:
                acc = r_ref[...].astype(F32) + acc
            o_ref[...] = acc.astype(o_ref.dtype)

        if not use_acc:
            finish(_dot(a_refs[0][...], b_ref[...], dims))
            return
        acc_ref = scr[0]
        kk = pl.program_id(2)
        sel = pl.program_id(0) if mode == "tn" else kk

        @pl.when(kk == 0)
        def _():
            acc_ref[...] = jnp.zeros_like(acc_ref)

        for s in range(nseg):
            def add(s=s):
                acc_ref[...] += _dot(a_refs[s][...], b_ref[...], dims)
            if nseg == 1:
                add()
            else:
                pl.when((sel >= starts[s]) & (sel < starts[s] + counts[s]))(add)

        @pl.when(kk == nk - 1)
        def _():
            finish(acc_ref[...])

    in_specs = ([a_spec(s) for s in range(nseg)] + [b_spec] + ([o_spec] if has_res else [])
                + [ANY_SPEC] * nt_)
    args = tuple(segs) + (b,) + ((res,) if has_res else ()) + tuple(ties)
    return _Order.done(pl.pallas_call(
        body, name=name,
        out_shape=jax.ShapeDtypeStruct((m, n), out_dtype),
        grid=(m // tm, n // tn, nk),
        in_specs=in_specs, out_specs=o_spec,
        scratch_shapes=[pltpu.VMEM((tm, tn), F32)] if use_acc else [],
        compiler_params=_params(("parallel", "parallel", "arbitrary")),
    )(*args))


def _act(g, u):
    return _silu(g.astype(F32)) * u.astype(F32)


def _ffn_up(name, h, wgut, cap_m=512, cap_n=1408):
    m, k = h.shape
    dff = wgut.shape[0] // 2
    tm, tn = _tile(m, cap_m), _tile(dff, cap_n)
    nj = dff // tn
    ties = _Order.take()

    def body(h_ref, wg_ref, wu_ref, *rest):
        g_ref, u_ref, a_ref = rest[len(ties):]
        hv = h_ref[...]
        g = _dot(hv, wg_ref[...], NT)
        u = _dot(hv, wu_ref[...], NT)
        g_ref[...] = g.astype(g_ref.dtype)
        u_ref[...] = u.astype(u_ref.dtype)
        a_ref[...] = _act(g, u).astype(a_ref.dtype)

    o_spec = pl.BlockSpec((tm, tn), lambda i, j: (i, j))
    return _Order.done(pl.pallas_call(
        body, name=name, out_shape=[jax.ShapeDtypeStruct((m, dff), ACT_DTYPE)] * 3,
        grid=(m // tm, nj),
        in_specs=[pl.BlockSpec((tm, k), lambda i, j: (i, 0)),
                  pl.BlockSpec((tn, k), lambda i, j: (j, 0)),
                  pl.BlockSpec((tn, k), lambda i, j: (nj + j, 0))] + [ANY_SPEC] * len(ties),
        out_specs=[o_spec] * 3,
        compiler_params=_params(("parallel", "parallel")),
    )(h, wgut, wgut, *ties))


def _ffn_dact(name, dxo, wd, g, u, scale, cap_m=512, cap_n=1408):
    m, k = dxo.shape
    dff = wd.shape[0]
    tm, tn = _tile(m, cap_m), _tile(dff, cap_n)
    ties = _Order.take()

    def body(d_ref, w_ref, g_ref, u_ref, *rest):
        dg_ref, du_ref = rest[len(ties):]
        da = _dot(d_ref[...], w_ref[...], NT) * scale
        _, vjp = jax.vjp(_act, g_ref[...], u_ref[...])
        dg, du = vjp(da)
        dg_ref[...] = dg.astype(dg_ref.dtype)
        du_ref[...] = du.astype(du_ref.dtype)

    o_spec = pl.BlockSpec((tm, tn), lambda i, j: (i, j))
    return _Order.done(pl.pallas_call(
        body, name=name, out_shape=[jax.ShapeDtypeStruct((m, dff), ACT_DTYPE)] * 2,
        grid=(m // tm, dff // tn),
        in_specs=[pl.BlockSpec((tm, k), lambda i, j: (i, 0)),
                  pl.BlockSpec((tn, k), lambda i, j: (j, 0)), o_spec, o_spec]
        + [ANY_SPEC] * len(ties),
        out_specs=[o_spec] * 2,
        compiler_params=_params(("parallel", "parallel")),
    )(dxo, wd, g, u, *ties))


def _rw(name, fn, ins, outs, accs=(), tr=256, ncb=1):
    t = next(a.shape[0] for kind, a, _, _ in ins if kind == "row")
    assert t % tr == 0
    n_in = len(ins)
    n_pieces = sum(len(w) for w, _ in outs)

    def spec(kind, arr, width, base):
        if kind == "row":
            return pl.BlockSpec((tr, width), lambda j, i: (i, base + j))
        return pl.BlockSpec((arr.shape[0], width), lambda j, i: (0, base + j))

    in_specs = [spec(*s) for s in ins]
    out_shapes, out_specs = [], []
    for widths, dt in outs:
        w = sum(widths)
        out_shapes.append(jax.ShapeDtypeStruct((t, w * ncb), dt))
        out_specs.append(pl.BlockSpec((tr, w), lambda j, i: (i, j)))
    for rows, width in accs:
        out_shapes.append(jax.ShapeDtypeStruct((rows, width * ncb), F32))
        out_specs.append(pl.BlockSpec((rows, width), lambda j, i: (0, j)))

    ties = _Order.take()
    nt_ = len(ties)
    in_specs = in_specs + [ANY_SPEC] * nt_

    def body(*refs):
        vals = [r[...] for r in refs[:n_in]]
        res = fn(*vals)
        o_refs = refs[n_in + nt_:n_in + nt_ + len(outs)]
        a_refs = refs[n_in + nt_ + len(outs):]
        p = 0
        for (widths, _), o_ref in zip(outs, o_refs):
            off = 0
            for w in widths:
                if len(widths) == 1:
                    o_ref[...] = res[p].astype(o_ref.dtype)
                else:
                    o_ref[:, off:off + w] = res[p].astype(o_ref.dtype)
                off += w
                p += 1
        i = pl.program_id(1)
        for a_ref, v in zip(a_refs, res[n_pieces:]):
            @pl.when(i == 0)
            def _(a_ref=a_ref, v=v):
                a_ref[...] = v

            @pl.when(i > 0)
            def _(a_ref=a_ref, v=v):
                a_ref[...] += v

    return _Order.done(pl.pallas_call(
        body, name=name, out_shape=out_shapes,
        grid=(ncb, t // tr), in_specs=in_specs, out_specs=out_specs,
        compiler_params=_params(("parallel", "arbitrary")),
    )(*[a for _, a, _, _ in ins], *ties))


def _rms(x, g):
    x = x.astype(F32)
    return x * lax.rsqrt(jnp.mean(x * x, axis=-1, keepdims=True) + EPS) * g


def _silu(x):
    return x * jax.nn.sigmoid(x)


def _colsum(v):
    return jnp.sum(v, axis=0, keepdims=True)


def _pair_norm(x, g):
    w = 2 * HD
    ri = lax.broadcasted_iota(jnp.int32, (w, w), 0)
    ci = lax.broadcasted_iota(jnp.int32, (w, w), 1)
    same_head = ((ri < HD) == (ci < HD)).astype(F32)
    ms = _spread(x * x, same_head) * (1.0 / HD)
    return x * lax.rsqrt(ms + EPS) * g


ATTN_SCALE = 1.0 / math.sqrt(HD)


def _attn_bias(coef):
    key = lax.broadcasted_iota(jnp.int32, (QB, QB), 0)
    qry = lax.broadcasted_iota(jnp.int32, (QB, QB), 1)
    dist = (qry - key).astype(F32)
    own = jnp.where(qry >= key, -coef * dist, NEG)
    prev = jnp.where(qry <= key, -coef * (dist + float(QB)), NEG)
    return own, prev


def _attn_pair(qn, kcn, kpn, vc, vp, b_own, b_prev):
    nb = qn.shape[0]
    w = 2 * HD
    lane = lax.broadcasted_iota(jnp.int32, (1, 1, w), 2)
    eye = (lax.broadcasted_iota(jnp.int32, (QB, QB), 0)
           == lax.broadcasted_iota(jnp.int32, (QB, QB), 1)).astype(F32)
    out = jnp.zeros((nb, QB, w), F32)
    lb = jnp.zeros((nb * QB, w), F32)
    for hh in range(2):
        mask = ((lane < HD) if hh == 0 else (lane >= HD)).astype(F32)
        qm = qn * mask
        lc = _dot(kcn, qm, BNT) + b_own[hh]
        lp = _dot(kpn, qm, BNT) + b_prev[hh]
        m = lax.stop_gradient(jnp.maximum(jnp.max(lc, axis=1, keepdims=True),
                                          jnp.max(lp, axis=1, keepdims=True)))
        pc = jnp.exp(lc - m)
        pp = jnp.exp(lp - m)
        l = jnp.sum(pc, axis=1, keepdims=True) + jnp.sum(pp, axis=1, keepdims=True)
        inv = 1.0 / l
        out = out + (_dot(pc * inv, vc, BTN) + _dot(pp * inv, vp, BTN)) * mask
        diag = (eye * (m + jnp.log(l))).reshape(nb * QB, QB)
        lb = lb + _spread(diag, jnp.broadcast_to(mask[0], (QB, w)))
    return out, lb.reshape(nb, QB, w)


NORM_ROWS = 128
NORM_UNROLL = 4
EPILOGUE_ROWS = 512
ATTN_BATCH_FWD = 16
ATTN_BATCH_BWD = 8


def _unit_rows(u, d):
    r = u & (d - 1)
    n = u >> (d.bit_length() - 1)

    def rows(blk):
        start = pl.multiple_of(blk * (QB * d), QB * d)
        return pl.ds(start, QB) if d == 1 else pl.ds(start + r, QB, stride=d)

    return rows(n), rows(jnp.maximum(n - 1, 0)), n == 0


def _unit_batch(i, nbatch, d, bias, qf, kf, vf):
    units = [_unit_rows(i * nbatch + j, d) for j in range(nbatch)]
    cur = lambda ref: jnp.stack([ref[c, :] for c, _, _ in units])
    prv = lambda ref: jnp.stack([ref[p, :] for _, p, _ in units])
    b_own = [b[0] for b in bias]
    b_prev = [jnp.stack([jnp.where(first, NEG, b[1]) for _, _, first in units]) for b in bias]
    return units, (cur(qf), cur(kf), prv(kf), cur(vf), prv(vf), b_own, b_prev)


def _q_norm(x, g):
    return _pair_norm(x, g * ATTN_SCALE)


def _attn_prologue(t, q_ref, k_ref, v_ref, qg_ref, kg_ref, qf, kf, vf):
    def chunk(c, carry):
        rows = pl.ds(pl.multiple_of(c * NORM_ROWS, NORM_ROWS), NORM_ROWS)
        qf[rows, :] = _q_norm(q_ref[rows, :].astype(F32), qg_ref[...])
        kf[rows, :] = _pair_norm(k_ref[rows, :].astype(F32), kg_ref[...])
        vf[rows, :] = v_ref[rows, :].astype(F32)
        return carry
    lax.fori_loop(0, t // NORM_ROWS, chunk, 0, unroll=NORM_UNROLL)


def _attn_specs(t, bases):
    w = 2 * HD
    ins = [pl.BlockSpec((t, w), functools.partial(lambda p, c, b: (0, b + p), b=b)) for b in bases]
    gain = pl.BlockSpec((1, w), lambda p, c: (0, 0))
    blk = pl.BlockSpec((t, w), lambda p, c: (0, p))
    return ins, gain, blk


def _attn_fwd(name, proj, bases, qg, kg, coefs, d):
    t = proj.shape[0]
    npairs = coefs.shape[0] // 2
    w = 2 * HD
    ins, gain, blk = _attn_specs(t, bases)

    def body(coef_ref, q_ref, k_ref, v_ref, qg_ref, kg_ref, o_ref, l_ref, qf, kf, vf):
        p = pl.program_id(0)
        bias = (_attn_bias(coef_ref[2 * p]), _attn_bias(coef_ref[2 * p + 1]))
        _attn_prologue(t, q_ref, k_ref, v_ref, qg_ref, kg_ref, qf, kf, vf)

        def step(i, carry):
            units, ins = _unit_batch(i, ATTN_BATCH_FWD, d, bias, qf, kf, vf)
            o, lb = _attn_pair(*ins)
            for j, (cur, _, _) in enumerate(units):
                o_ref[cur, :] = o[j]
                l_ref[cur, :] = lb[j]
            return carry

        lax.fori_loop(0, t // QB // ATTN_BATCH_FWD, step, 0)

    return pl.pallas_call(
        body, name=name,
        out_shape=[jax.ShapeDtypeStruct((t, npairs * w), F32)] * 2,
        grid_spec=pltpu.PrefetchScalarGridSpec(
            num_scalar_prefetch=1, grid=(npairs,),
            in_specs=ins + [gain, gain], out_specs=[blk, blk],
            scratch_shapes=[pltpu.VMEM((t, w), F32)] * 3),
        compiler_params=_params(("arbitrary",)),
    )(coefs, proj, proj, proj, qg, kg)


def _attn_bwd(name, proj, bases, qg, kg, coefs, d, do, dl):
    t = proj.shape[0]
    npairs = coefs.shape[0] // 2
    w = 2 * HD
    ins, gain, blk = _attn_specs(t, bases)

    def body(coef_ref, q_ref, k_ref, v_ref, qg_ref, kg_ref, do_ref, dl_ref,
             dq_ref, dk_ref, dv_ref, dqg_ref, dkg_ref, qf, kf, vf, dqf, dkf, dvf):
        p = pl.program_id(0)
        bias = (_attn_bias(coef_ref[2 * p]), _attn_bias(coef_ref[2 * p + 1]))
        _attn_prologue(t, q_ref, k_ref, v_ref, qg_ref, kg_ref, qf, kf, vf)
        dkf[...] = jnp.zeros_like(dkf)
        dvf[...] = jnp.zeros_like(dvf)

        def step(i, carry):
            units, ins = _unit_batch(i, ATTN_BATCH_BWD, d, bias, qf, kf, vf)
            f = lambda a, b, c, e, g: _attn_pair(a, b, c, e, g, *ins[5:])
            _, vjp = jax.vjp(f, *ins[:5])
            cot = (jnp.stack([do_ref[cur, :] for cur, _, _ in units]),
                   jnp.stack([dl_ref[cur, :] for cur, _, _ in units]))
            dq, dkc, dkp, dvc, dvp = vjp(cot)
            for j, (cur, prv, _) in enumerate(units):
                dqf[cur, :] = dq[j]
                dkf[cur, :] += dkc[j]
                dkf[prv, :] += dkp[j]
                dvf[cur, :] += dvc[j]
                dvf[prv, :] += dvp[j]
            return carry

        lax.fori_loop(0, t // QB // ATTN_BATCH_BWD, step, 0)

        def chunk(c, carry):
            dqg_acc, dkg_acc = carry
            rows = pl.ds(pl.multiple_of(c * EPILOGUE_ROWS, EPILOGUE_ROWS), EPILOGUE_ROWS)
            _, vq = jax.vjp(_q_norm, q_ref[rows, :].astype(F32), qg_ref[...])
            dq, dqg = vq(dqf[rows, :])
            _, vk = jax.vjp(_pair_norm, k_ref[rows, :].astype(F32), kg_ref[...])
            dk, dkg = vk(dkf[rows, :])
            dq_ref[rows, :] = dq.astype(dq_ref.dtype)
            dk_ref[rows, :] = dk.astype(dk_ref.dtype)
            dv_ref[rows, :] = dvf[rows, :].astype(dv_ref.dtype)
            return dqg_acc + dqg, dkg_acc + dkg

        zero = jnp.zeros((1, w), F32)
        dqg, dkg = lax.fori_loop(0, t // EPILOGUE_ROWS, chunk, (zero, zero))

        @pl.when(p == 0)
        def _():
            dqg_ref[...] = dqg
            dkg_ref[...] = dkg

        @pl.when(p > 0)
        def _():
            dqg_ref[...] += dqg
            dkg_ref[...] += dkg

    big = jax.ShapeDtypeStruct((t, npairs * w), ACT_DTYPE)
    small = jax.ShapeDtypeStruct((1, w), F32)
    return pl.pallas_call(
        body, name=name,
        out_shape=[big, big, big, small, small],
        grid_spec=pltpu.PrefetchScalarGridSpec(
            num_scalar_prefetch=1, grid=(npairs,),
            in_specs=ins + [gain, gain, blk, blk],
            out_specs=[blk, blk, blk, gain, gain],
            scratch_shapes=[pltpu.VMEM((t, w), F32)] * 6),
        compiler_params=_params(("arbitrary",)),
    )(coefs, proj, proj, proj, qg, kg, do, dl)


CONV_ROWS = 256
CONV_HALO = 8


def _rows_back(x, s):
    return x if s == 0 else pltpu.roll(x, s, 0)


def _rows_ahead(x, s):
    return x if s == 0 else pltpu.roll(x, x.shape[0] - s, 0)


def _conv_pre(u, w, b):
    y = b
    for kk in range(SSD_K):
        y = y + w[kk:kk + 1, :] * _rows_back(u, SSD_K - 1 - kk)
    return y


def _stage_padded(dst, src_ref, t):
    zeros = jnp.zeros((CONV_HALO, dst.shape[1]), F32)
    dst[0:CONV_HALO, :] = zeros
    dst[t + CONV_HALO:t + 2 * CONV_HALO, :] = zeros
    dst[CONV_HALO:t + CONV_HALO, :] = src_ref[...].astype(F32)


def _chunk_rows(c):
    r0 = pl.multiple_of(c * CONV_ROWS, CONV_ROWS)
    return pl.ds(r0, CONV_ROWS + 2 * CONV_HALO), pl.ds(r0, CONV_ROWS)


def _conv_fwd(name, src, base, w, b, cw=128):
    t = src.shape[0]
    c = w.shape[1]
    centre = slice(CONV_HALO, CONV_HALO + CONV_ROWS)

    def body(u_ref, w_ref, b_ref, o_ref, up):
        _stage_padded(up, u_ref, t)
        wv, bv = w_ref[...], b_ref[...]

        def chunk(ci, carry):
            ext, rows = _chunk_rows(ci)
            y = _conv_pre(up[ext, :], wv, bv)
            o_ref[rows, :] = _silu(y)[centre].astype(o_ref.dtype)
            return carry

        lax.fori_loop(0, t // CONV_ROWS, chunk, 0)

    return pl.pallas_call(
        body, name=name, out_shape=jax.ShapeDtypeStruct((t, c), ACT_DTYPE),
        grid=(c // cw,),
        in_specs=[pl.BlockSpec((t, cw), lambda j: (0, base + j)),
                  pl.BlockSpec((SSD_K, cw), lambda j: (0, j)),
                  pl.BlockSpec((1, cw), lambda j: (0, j))],
        out_specs=pl.BlockSpec((t, cw), lambda j: (0, j)),
        scratch_shapes=[pltpu.VMEM((t + 2 * CONV_HALO, cw), F32)],
        compiler_params=_params(("parallel",)),
    )(src, w, b)


def _conv_bwd(name, src, base, w, b, dout, cw=128):
    t = src.shape[0]
    c = w.shape[1]

    centre = slice(CONV_HALO, CONV_HALO + CONV_ROWS)

    def body(u_ref, w_ref, b_ref, d_ref, du_ref, dw_ref, db_ref, up, dp):
        _stage_padded(up, u_ref, t)
        _stage_padded(dp, d_ref, t)
        wv, bv = w_ref[...], b_ref[...]

        def chunk(ci, carry):
            dws, db = carry
            ext, rows = _chunk_rows(ci)
            u = up[ext, :]
            y = _conv_pre(u, wv, bv)
            sg = jax.nn.sigmoid(y)
            dy = dp[ext, :] * (sg * (1.0 + y * (1.0 - sg)))
            du = jnp.zeros_like(u)
            new_dws = []
            for kk in range(SSD_K):
                s = SSD_K - 1 - kk
                du = du + wv[kk:kk + 1, :] * _rows_ahead(dy, s)
                new_dws.append(dws[kk] + _colsum((dy * _rows_back(u, s))[centre]))
            du_ref[rows, :] = du[centre].astype(du_ref.dtype)
            return tuple(new_dws), db + _colsum(dy[centre])

        zero = jnp.zeros((1, cw), F32)
        dws, db = lax.fori_loop(0, t // CONV_ROWS, chunk, ((zero,) * SSD_K, zero))
        for kk in range(SSD_K):
            dw_ref[kk:kk + 1, :] = dws[kk]
        db_ref[...] = db

    return pl.pallas_call(
        body, name=name,
        out_shape=[jax.ShapeDtypeStruct((t, c), ACT_DTYPE),
                   jax.ShapeDtypeStruct((SSD_K, c), F32),
                   jax.ShapeDtypeStruct((1, c), F32)],
        grid=(c // cw,),
        in_specs=[pl.BlockSpec((t, cw), lambda j: (0, base + j)),
                  pl.BlockSpec((SSD_K, cw), lambda j: (0, j)),
                  pl.BlockSpec((1, cw), lambda j: (0, j)),
                  pl.BlockSpec((t, cw), lambda j: (0, j))],
        out_specs=[pl.BlockSpec((t, cw), lambda j: (0, j)),
                   pl.BlockSpec((SSD_K, cw), lambda j: (0, j)),
                   pl.BlockSpec((1, cw), lambda j: (0, j))],
        scratch_shapes=[pltpu.VMEM((t + 2 * CONV_HALO, cw), F32)] * 2,
        compiler_params=_params(("parallel",)),
    )(src, w, b, dout)


def _softplus(x):
    return jnp.maximum(x, 0.0) + jnp.log(1.0 + jnp.exp(-jnp.abs(x)))


def _ssd_chunk(xbc, dtraw, bias, alog, states):
    wd = states[0].shape[1]
    nj = wd // SSD_P
    inner = SSD_G * wd
    dt = _softplus(dtraw + bias)
    a = dt * (-jnp.exp(alog))
    li = lax.broadcasted_iota(jnp.int32, (SSD_Q, SSD_Q), 0)
    si = lax.broadcasted_iota(jnp.int32, (SSD_Q, SSD_Q), 1)
    causal = li >= si
    acs = _running_sum(a, causal.astype(F32))
    acs_t = acs.T
    a_last = acs[SSD_Q - 1:SSD_Q, :]
    grow = jnp.exp(acs)
    shrink = jnp.exp(a_last - acs)
    hrow = lax.broadcasted_iota(jnp.int32, (LANE, wd), 0)
    wcol = lax.broadcasted_iota(jnp.int32, (LANE, wd), 1)
    lane = lax.broadcasted_iota(jnp.int32, (1, LANE), 1)
    ys, snext = [], []
    for g in range(SSD_G):
        lo = (hrow - g * nj) * SSD_P
        head_lanes = jnp.logical_and(wcol >= lo, wcol < lo + SSD_P).astype(F32)
        xs = xbc[:, g * wd:(g + 1) * wd]
        bm = xbc[:, inner + g * SSD_N:inner + (g + 1) * SSD_N]
        cm = xbc[:, inner + (SSD_G + g) * SSD_N:inner + (SSD_G + g + 1) * SSD_N]
        xdt = xs * _dot(dt, head_lanes)
        grow_x = _spread(grow, head_lanes)
        y_off = _dot(cm, states[g]) * grow_x
        s_new = (states[g] * grow_x[SSD_Q - 1:SSD_Q, :]
                 + _dot(bm, xdt * _dot(shrink, head_lanes), TN))
        cb = _dot(cm, bm, NT)
        pieces = []
        for i in range(wd // LANE):
            xp = xdt[:, i * LANE:(i + 1) * LANE]
            acc = jnp.zeros((SSD_Q, LANE), F32)
            for hh in range(LANE // SSD_P):
                h = g * nj + i * (LANE // SSD_P) + hh
                decay = jnp.exp(jnp.where(causal, acs[:, h:h + 1] - acs_t[h:h + 1, :], NEG))
                keep = jnp.logical_and(lane >= hh * SSD_P, lane < (hh + 1) * SSD_P).astype(F32)
                acc = acc + _dot(cb * decay, xp * keep)
            pieces.append(acc)
        y_diag = pieces[0] if len(pieces) == 1 else jnp.concatenate(pieces, axis=1)
        ys.append(y_diag + y_off)
        snext.append(s_new)
    return ys, snext


def _ssd_specs(cdim, wd, rev, nc):
    ch = (lambda c: nc - 1 - c) if rev else (lambda c: c)
    full = lambda width: pl.BlockSpec((SSD_Q, width), lambda c: (ch(c), 0))
    vec = pl.BlockSpec((1, LANE), lambda c: (0, 0))
    st = pl.BlockSpec((1, SSD_G, SSD_N, wd), lambda c: (ch(c), 0, 0, 0))
    return full, vec, st


def _ssd_fwd(name, xbc, dtraw, bias, alog, inner):
    t, cdim = xbc.shape
    wd = inner // SSD_G
    nc = t // SSD_Q
    full, vec, st = _ssd_specs(cdim, wd, False, nc)

    def body(x_ref, r_ref, b_ref, a_ref, y_ref, st_ref, s_scr):
        @pl.when(pl.program_id(0) == 0)
        def _():
            s_scr[...] = jnp.zeros_like(s_scr)

        sprev = [s_scr[g] for g in range(SSD_G)]
        ys, snext = _ssd_chunk(x_ref[...].astype(F32), r_ref[...], b_ref[...], a_ref[...], sprev)
        for g in range(SSD_G):
            st_ref[0, g] = sprev[g]
            y_ref[:, g * wd:(g + 1) * wd] = ys[g]
            s_scr[g] = snext[g]

    return pl.pallas_call(
        body, name=name,
        out_shape=[jax.ShapeDtypeStruct((t, inner), F32),
                   jax.ShapeDtypeStruct((nc, SSD_G, SSD_N, wd), F32)],
        grid=(nc,),
        in_specs=[full(cdim), full(LANE), vec, vec],
        out_specs=[full(inner), st],
        scratch_shapes=[pltpu.VMEM((SSD_G, SSD_N, wd), F32)],
        compiler_params=_params(("arbitrary",)),
    )(xbc, dtraw, bias, alog)


def _ssd_bwd(name, xbc, dtraw, bias, alog, states, dy, dxs_extra):
    t, cdim = xbc.shape
    inner = dy.shape[1]
    wd = inner // SSD_G
    nc = t // SSD_Q
    full, vec, st = _ssd_specs(cdim, wd, True, nc)

    def body(x_ref, r_ref, b_ref, a_ref, st_ref, dy_ref, dx0_ref,
             dx_ref, dr_ref, db_ref, da_ref, ds_scr):
        first = pl.program_id(0) == 0

        @pl.when(first)
        def _():
            ds_scr[...] = jnp.zeros_like(ds_scr)

        sprev = [st_ref[0, g] for g in range(SSD_G)]
        _, vjp = jax.vjp(_ssd_chunk, x_ref[...].astype(F32), r_ref[...], b_ref[...], a_ref[...],
                         sprev)
        dyv = dy_ref[...]
        dys = [dyv[:, g * wd:(g + 1) * wd] for g in range(SSD_G)]
        dsn = [ds_scr[g] for g in range(SSD_G)]
        dx, dr, db, da, dsp = vjp((dys, dsn))
        dx_ref[:, :inner] = dx[:, :inner] + dx0_ref[...].astype(F32)
        dx_ref[:, inner:] = dx[:, inner:]
        dr_ref[...] = dr
        for g in range(SSD_G):
            ds_scr[g] = dsp[g]

        @pl.when(first)
        def _():
            db_ref[...] = db
            da_ref[...] = da

        @pl.when(jnp.logical_not(first))
        def _():
            db_ref[...] += db
            da_ref[...] += da

    return pl.pallas_call(
        body, name=name,
        out_shape=[jax.ShapeDtypeStruct((t, cdim), F32),
                   jax.ShapeDtypeStruct((t, LANE), F32),
                   jax.ShapeDtypeStruct((1, LANE), F32),
                   jax.ShapeDtypeStruct((1, LANE), F32)],
        grid=(nc,),
        in_specs=[full(cdim), full(LANE), vec, vec, st, full(inner), full(inner)],
        out_specs=[full(cdim), full(LANE), vec, vec],
        scratch_shapes=[pltpu.VMEM((SSD_G, SSD_N, wd), F32)],
        compiler_params=_params(("arbitrary",)),
    )(xbc, dtraw, bias, alog, states, dy, dxs_extra)


def _mix(o0, o1, o2, l0, l1, l2):
    m = lax.stop_gradient(jnp.maximum(jnp.maximum(l0, l1), l2))
    e0, e1, e2 = jnp.exp(l0 - m), jnp.exp(l1 - m), jnp.exp(l2 - m)
    return (e0 * o0 + e1 * o1 + e2 * o2) / (e0 + e1 + e2)


def _gate(y, xs, z, dexp, gain):
    v = (y + xs.astype(F32) * dexp) * _silu(z.astype(F32))
    return _rms(v, gain)


def _merge(ga, gs, ap, sp):
    return jax.nn.sigmoid(ga.astype(F32)) * ap + jax.nn.sigmoid(gs.astype(F32)) * sp


def _alibi_coefs(hp):
    n = hp * len(PATTERNS)
    slopes = np.exp2(-ALIBI_MAX_EXP * np.arange(1, n + 1, dtype=np.float32) / n).astype(np.float32)
    return [jnp.asarray(slopes[g * hp:(g + 1) * hp] * np.float32(d))
            for g, (_, d) in enumerate(PATTERNS)]


def _local_step(x, tgt, w, p, gw_=None, aw=None):
    t, d = x.shape
    dff = w["gu1t"].shape[0] // 2
    aw = w["abt"].shape[1] if aw is None else aw
    hp = aw // HD
    qkv = len(PATTERNS) * aw
    inner = p["ssd_norm"].shape[1]
    nh = p["dt_bias"].shape[1]
    gw_ = {} if gw_ is None else gw_
    gw = inner // SSD_G
    cdim = inner + 2 * SSD_G * SSD_N
    z_off, xbc_off = 3 * qkv, 3 * qkv + inner
    ga_off = xbc_off + cdim
    gs_off = ga_off + d
    hw = d // 2
    assert z_off % gw == 0 and xbc_off % LANE == 0 and ga_off % hw == 0 and gs_off % hw == 0
    assert (nh // SSD_G) * SSD_P == gw and hp % 2 == 0 and aw % LANE == 0 and nh <= LANE
    gdt = MXU_DTYPE

    row = lambda a, width, base=0: ("row", a, width, base)
    const = lambda a, width, base=0: ("const", a, width, base)

    def rms_fwd(name, xin, g):
        return _rw(name, lambda xv, gv: (_rms(xv, gv),), [row(xin, d), const(g, d)],
                   [((d,), ACT_DTYPE)])[0]

    def rms_bwd(name, xin, g, dh, dres):
        def fn(xv, gv, dhv, drv):
            _, vjp = jax.vjp(_rms, xv, gv)
            dx, dg = vjp(dhv.astype(F32))
            return drv + dx, dg
        return _rw(name, fn, [row(xin, d), const(g, d), row(dh, d), row(dres, d)],
                   [((d,), F32)], accs=[(1, d)])

    def ffn_fwd(tag, xin, g, key_gu, key_d):
        h = rms_fwd(tag + "_norm", xin, g)
        gate, up, a = _ffn_up(tag + "_up", h, w[key_gu])
        xo = _mm(tag + "_down", a, w[key_d], "nn", F32, res=xin, scale=0.5)
        return xo, (h, gate, up, a)

    def ffn_bwd(tag, xin, g, wgut, wd, saved, dxo, key_gu, key_d):
        h, gate, up, a = saved
        gw_[key_d] = _mm(tag + "_dwd", a, dxo, "tn", gdt, scale=0.5)
        dgu = _ffn_dact(tag + "_dact", dxo, wd, gate, up, 0.5)
        gw_[key_gu] = _mm(tag + "_dwgu", dgu, h, "tn", gdt)
        dh = _mm(tag + "_dh", dgu, wgut, "nn", F32)
        return rms_bwd(tag + "_dnorm", xin, g, dh, dxo)

    x1, ffn1_saved = ffn_fwd("ffn1", x, p["ffn1_norm"], "gu1t", "d1")
    if hasattr(w, "after_first_ffn"):
        w.after_first_ffn()
    h2 = rms_fwd("mix_norm", x1, p["mix_norm"])
    proj = _mm("in_proj", h2, w["maint"], "nt", ACT_DTYPE, cap_m=512, cap_n=2944)
    dtraw = _mm("dt_proj", h2, w["dtt"], "nt", F32)

    coefs = _alibi_coefs(hp)
    qg2 = jnp.concatenate([p["q_norm"], p["q_norm"]], axis=1)
    kg2 = jnp.concatenate([p["k_norm"], p["k_norm"]], axis=1)
    pw = 2 * HD
    attn_bases = [[(off + gi * aw) // pw for off in (0, qkv, 2 * qkv)]
                  for gi in range(len(PATTERNS))]
    attn_o, attn_l = [], []
    for gi, (_, dil) in enumerate(PATTERNS):
        o, l = _attn_fwd(f"attn_fwd{gi}", proj, attn_bases[gi], qg2, kg2, coefs[gi], dil)
        attn_o.append(o)
        attn_l.append(l)
    ao = _rw("attn_mix", lambda *v: (_mix(*v),), [row(a, aw) for a in attn_o + attn_l],
             [((aw,), ACT_DTYPE)])[0]

    xbc = _conv_fwd("conv_fwd", proj, xbc_off // LANE, p["conv_w"], p["conv_b"])
    pad = lambda v: jnp.pad(v, ((0, 0), (0, LANE - nh)))
    bias_p, alog_p = pad(p["dt_bias"]), pad(p["a_log"])
    yssd, states = _ssd_fwd("ssd_fwd", xbc, dtraw, bias_p, alog_p, inner)
    dexp = jnp.repeat(p["d_skip"], SSD_P, axis=1)
    gate_ins = [row(yssd, gw), row(xbc, gw), row(proj, gw, z_off // gw),
                const(dexp, gw), const(p["ssd_norm"], gw)]
    yn = _rw("ssd_gate", lambda *v: (_gate(*v),), gate_ins, [((gw,), ACT_DTYPE)], ncb=SSD_G)[0]

    ap = _mm("attn_out", ao, w["abt"], "nt", F32)
    sp = _mm("ssd_out", yn, w["sb"], "nn", F32)
    merge_ins = [row(proj, hw, ga_off // hw), row(proj, hw, gs_off // hw), row(ap, hw), row(sp, hw)]
    mg = _rw("merge", lambda *v: (_merge(*v),), merge_ins, [((hw,), ACT_DTYPE)], ncb=2)[0]
    x2 = _mm("mix_out", mg, w["out"], "nn", F32, res=x1)
    x3, ffn2_saved = ffn_fwd("ffn2", x2, p["ffn2_norm"], "gu2t", "d2")

    def loss_fn(yv, tv):
        e = yv - tv
        return e * (1.0 / d), _colsum(e * e)
    dy, loss_vec = _rw("loss", loss_fn, [row(x3, d), row(tgt, d)], [((d,), F32)], accs=[(1, d)])

    gp = {}
    dx2, gp["ffn2_norm"] = ffn_bwd(
        "ffn2", x2, p["ffn2_norm"], w["gu2t"], w["d2"], ffn2_saved, dy, "gu2t", "d2")
    dmg = _mm("d_merge", dx2, w["out"], "nt", ACT_DTYPE)
    gw_["out"] = _mm("dw_out", mg, dx2, "tn", gdt)

    def merge_bwd(gav, gsv, apv, spv, dv):
        _, vjp = jax.vjp(_merge, gav, gsv, apv, spv)
        return vjp(dv.astype(F32))
    dga, dgs, dap, dsp = _rw("d_merge_gate", merge_bwd, merge_ins + [row(dmg, hw)],
                             [((hw,), ACT_DTYPE)] * 4, ncb=2)
    gw_["abt"] = _mm("dw_ab", dap, ao, "tn", gdt)
    dao = _mm("d_attn_o", dap, w["abt"], "nn", F32)
    gw_["sb"] = _mm("dw_sb", yn, dsp, "tn", gdt)
    dyn = _mm("d_ssd_y", dsp, w["sb"], "nt", F32)

    def gate_bwd(yv, xv, zv, dev, gv, dv):
        _, vjp = jax.vjp(_gate, yv, xv, zv, dev, gv)
        return vjp(dv)
    dyssd, dxs_gate, dz, ddexp, gp["ssd_norm"] = _rw(
        "d_ssd_gate", gate_bwd, gate_ins + [row(dyn, gw)],
        [((gw,), F32), ((gw,), F32), ((gw,), ACT_DTYPE)], accs=[(1, gw), (1, gw)], ncb=SSD_G)
    gp["d_skip"] = ddexp.reshape(nh, SSD_P).sum(axis=1).reshape(1, nh)

    dxbc, ddtraw, dbias, dalog = _ssd_bwd("ssd_bwd", xbc, dtraw, bias_p, alog_p, states,
                                          dyssd, dxs_gate)
    gp["dt_bias"], gp["a_log"] = dbias[:, :nh], dalog[:, :nh]
    du, gp["conv_w"], gp["conv_b"] = _conv_bwd("conv_bwd", proj, xbc_off // LANE,
                                               p["conv_w"], p["conv_b"], dxbc)

    def mix_bwd(*v):
        _, vjp = jax.vjp(_mix, *v[:6])
        return vjp(v[6])
    dmix = _rw("d_attn_mix", mix_bwd, [row(a, aw) for a in attn_o + attn_l] + [row(dao, aw)],
               [((aw,), F32)] * 6)
    dq, dk, dv = [], [], []
    dqg = dkg = None
    for gi, (_, dil) in enumerate(PATTERNS):
        r = _attn_bwd(f"attn_bwd{gi}", proj, attn_bases[gi], qg2, kg2, coefs[gi], dil,
                      dmix[gi], dmix[3 + gi])
        dq.append(r[0])
        dk.append(r[1])
        dv.append(r[2])
        dqg = r[3] if dqg is None else dqg + r[3]
        dkg = r[4] if dkg is None else dkg + r[4]
    gp["q_norm"] = dqg[:, :HD] + dqg[:, HD:]
    gp["k_norm"] = dkg[:, :HD] + dkg[:, HD:]

    segs = dq + dk + dv + [dz, du, dga, dgs]
    gw_["maint"] = _mm("dw_in", segs, h2, "tn", gdt)
    gw_["dtt"] = _mm("dw_dt", ddtraw, h2, "tn", gdt)
    dh2 = _mm("d_h2_main", segs, w["maint"], "nn", F32)
    dh2 = _mm("d_h2_dt", ddtraw, w["dtt"], "nn", F32, res=dh2)
    dx1, gp["mix_norm"] = rms_bwd("d_mix_norm", x1, p["mix_norm"], dh2, dx2)
    dx0, gp["ffn1_norm"] = ffn_bwd(
        "ffn1", x, p["ffn1_norm"], w["gu1t"], w["d1"], ffn1_saved, dx1, "gu1t", "d1")
    return loss_vec, dx0, gw_, gp


MESH = pl.DeviceIdType.MESH
HBM_SPEC = pl.BlockSpec(memory_space=pltpu.HBM)


def _mesh_pos():
    return lax.axis_index("x"), lax.axis_index("y"), lax.axis_index("c")


def _flip(pos, k):
    x, y, c = pos
    return (1 - x if k & 4 else x, 1 - y if k & 2 else y, 1 - c if k & 1 else c)


def _dev_index(pos):
    return 4 * pos[0] + 2 * pos[1] + pos[2]


def _rows_of(ref, base, stride, rows, pos):
    start = pl.multiple_of(base + stride * _dev_index(pos), ROW_ALIGN)
    return ref.at[pl.ds(start, rows)]


def _gather(name, shards, dests, out_shapes):
    n = len(shards)
    n_out = len(out_shapes)

    def body(*refs):
        x_refs = refs[:n]
        o_refs = refs[n:n + n_out]
        send_sems, recv_sems, local_sems = refs[n + n_out:]
        me = _mesh_pos()
        sibling = _flip(me, 1)
        chips = [_flip(me, 4), _flip(me, 2), _flip(me, 6)]

        def slot(i, block):
            k_out, base, stride = dests[i]
            return _rows_of(o_refs[k_out], base, stride, shards[i].shape[0], block)

        def copy(i, k, block, to, src=None):
            dst = slot(i, block)
            return pltpu.make_async_remote_copy(
                src_ref=dst if src is None else src, dst_ref=dst,
                send_sem=send_sems.at[7 * i + k], recv_sem=recv_sems.at[7 * i + k],
                device_id=to, device_id_type=MESH)

        mine = [pltpu.make_async_copy(x_refs[i], slot(i, me), local_sems.at[i]) for i in range(n)]
        for cp in mine:
            cp.start()
        first = []
        for i in range(n):
            first.append(copy(i, 0, me, sibling, src=x_refs[i]))
            first += [copy(i, 1 + j, me, chip, src=x_refs[i]) for j, chip in enumerate(chips)]
        for cp in first:
            cp.start()
        passed = []
        for j, chip in enumerate(chips):
            for i in range(n):
                copy(i, 1 + j, chip, me).wait_recv()
                fwd = copy(i, 4 + j, chip, sibling)
                fwd.start()
                passed.append(fwd)
        for i in range(n):
            copy(i, 0, sibling, me).wait_recv()
            for j, chip in enumerate(chips):
                copy(i, 4 + j, _flip(chip, 1), me).wait_recv()
        for cp in first + passed:
            cp.wait_send()
        for cp in mine:
            cp.wait()

    return pl.pallas_call(
        body, name=name,
        out_shape=[jax.ShapeDtypeStruct(s, dt) for s, dt in out_shapes],
        in_specs=[HBM_SPEC] * n, out_specs=[HBM_SPEC] * n_out,
        scratch_shapes=[pltpu.SemaphoreType.DMA((7 * n,)), pltpu.SemaphoreType.DMA((7 * n,)),
                        pltpu.SemaphoreType.DMA((n,))],
    )(*shards)


def _exchange(name, grads, srcs, small):
    n = len(srcs)
    ng = len(grads)

    def body(*refs):
        g_refs = refs[:ng]
        m_ref = refs[ng]
        r_refs = refs[ng + 1:ng + 1 + n]
        s_ref = refs[ng + 1 + n]
        send_sems, recv_sems, local_sems = refs[ng + 2 + n:]
        me = _mesh_pos()
        my = _dev_index(me)

        def slab(i, pos):
            gi, base, stride, rows = srcs[i]
            return _rows_of(g_refs[gi], base, stride, rows, pos)

        own = [pltpu.make_async_copy(slab(i, me), r_refs[i].at[my], local_sems.at[i])
               for i in range(n)]
        own.append(pltpu.make_async_copy(m_ref, s_ref.at[my], local_sems.at[n]))
        for cp in own:
            cp.start()

        def copies(k, src_pos, slot_pos):
            peer = _flip(me, k)
            si = _dev_index(slot_pos)
            out = [pltpu.make_async_remote_copy(
                src_ref=slab(i, src_pos), dst_ref=r_refs[i].at[si],
                send_sem=send_sems.at[7 * i + k - 1], recv_sem=recv_sems.at[7 * i + k - 1],
                device_id=peer, device_id_type=MESH) for i in range(n)]
            out.append(pltpu.make_async_remote_copy(
                src_ref=m_ref, dst_ref=s_ref.at[si],
                send_sem=send_sems.at[7 * n + k - 1], recv_sem=recv_sems.at[7 * n + k - 1],
                device_id=peer, device_id_type=MESH))
            return out

        sent = [cp for k in range(1, NDEV) for cp in copies(k, _flip(me, k), me)]
        for cp in sent:
            cp.start()
        for k in range(1, NDEV):
            for cp in copies(k, me, _flip(me, k)):
                cp.wait_recv()
        for cp in sent:
            cp.wait_send()
        for cp in own:
            cp.wait()

    out_shape = [jax.ShapeDtypeStruct((NDEV, rows, grads[gi].shape[1]), grads[gi].dtype)
                 for gi, _, _, rows in srcs]
    out_shape.append(jax.ShapeDtypeStruct((NDEV,) + small.shape, small.dtype))
    return pl.pallas_call(
        body, name=name, out_shape=out_shape,
        in_specs=[HBM_SPEC] * (ng + 1), out_specs=[HBM_SPEC] * (n + 1),
        scratch_shapes=[pltpu.SemaphoreType.DMA((7 * (n + 1),)),
                        pltpu.SemaphoreType.DMA((7 * (n + 1),)),
                        pltpu.SemaphoreType.DMA((n + 1,))],
    )(*grads, small)


SEM_SPEC = pl.BlockSpec(memory_space=pltpu.SEMAPHORE)
SIDE_EFFECT = pltpu.SideEffectType.DATAFLOW_SIDE_EFFECTING


def _split_refs(plan, srcs, lands, i, src_for, land_from):
    si, sbase, sstride, li, lbase, lstride, rows = plan[i]
    return (_rows_of(srcs[si], sbase, sstride, rows, src_for),
            _rows_of(lands[li], lbase, lstride, rows, land_from))


ALL_PEERS = tuple(range(1, NDEV))
SAME_CORE_AND_SIBLING = (1, 4, 2, 6)
OTHER_CHIPS = (4, 2, 6)


def _split_start(name, srcs, lands, plan, after=(), relations=ALL_PEERS):
    ns, nl, n = len(srcs), len(lands), len(plan)

    def body(*refs):
        s_refs = refs[:ns]
        l_refs = refs[ns:ns + nl]
        send_sems, recv_sems = refs[ns + nl + len(after):ns + nl + len(after) + 2]
        local_sems = refs[ns + nl + len(after) + 2]
        token = refs[ns + nl + len(after) + 3 + ns + nl]
        me = _mesh_pos()
        for i in range(n):
            src, dst = _split_refs(plan, s_refs, l_refs, i, me, me)
            pltpu.make_async_copy(src, dst, local_sems.at[i]).start()
        for k in relations:
            peer = _flip(me, k)
            for i in range(n):
                src, dst = _split_refs(plan, s_refs, l_refs, i, peer, me)
                pltpu.make_async_remote_copy(
                    src_ref=src, dst_ref=dst,
                    send_sem=send_sems.at[7 * i + k - 1], recv_sem=recv_sems.at[7 * i + k - 1],
                    device_id=peer, device_id_type=MESH).start()
        token[...] = jnp.zeros_like(token)

    hbm = lambda a: pltpu.HBM(a.shape, a.dtype)
    out_shape = ((pltpu.SemaphoreType.DMA((7 * n,)), pltpu.SemaphoreType.DMA((7 * n,)),
                  pltpu.SemaphoreType.DMA((n,)))
                 + tuple(hbm(a) for a in srcs) + tuple(hbm(a) for a in lands)
                 + (jax.ShapeDtypeStruct((8, LANE), F32),))
    out = pl.pallas_call(
        body, name=name, out_shape=out_shape,
        in_specs=[HBM_SPEC] * (ns + nl) + [ANY_SPEC] * len(after),
        out_specs=(SEM_SPEC, SEM_SPEC, SEM_SPEC) + (HBM_SPEC,) * (ns + nl)
        + (pl.BlockSpec(memory_space=pltpu.VMEM),),
        input_output_aliases={i: 3 + i for i in range(ns + nl)},
        compiler_params=pltpu.CompilerParams(has_side_effects=SIDE_EFFECT),
    )(*[pltpu.with_memory_space_constraint(a, pltpu.HBM) for a in tuple(srcs) + tuple(lands)],
      *after)
    _Order.tokens.append(out[-1])
    return out[0], out[1], out[2], out[3:3 + ns], out[3 + ns:3 + ns + nl]


def _split_wait(name, started, plan, relations=ALL_PEERS):
    send_sems, recv_sems, local_sems, srcs, lands = started
    ns, nl, n = len(srcs), len(lands), len(plan)
    after = [_Order.last] if _Order.last is not None else []

    def body(*refs):
        s_refs = refs[:ns]
        l_refs = refs[ns:ns + nl]
        send_sems, recv_sems, local_sems = refs[ns + nl:ns + nl + 3]
        me = _mesh_pos()
        for i in range(n):
            src, dst = _split_refs(plan, s_refs, l_refs, i, me, me)
            pltpu.make_async_copy(src, dst, local_sems.at[i]).wait()
        for k in relations:
            peer = _flip(me, k)
            for i in range(n):
                src, dst = _split_refs(plan, s_refs, l_refs, i, peer, peer)
                cp = pltpu.make_async_remote_copy(
                    src_ref=src, dst_ref=dst,
                    send_sem=send_sems.at[7 * i + k - 1], recv_sem=recv_sems.at[7 * i + k - 1],
                    device_id=peer, device_id_type=MESH)
                cp.wait_send()
                cp.wait_recv()

    hbm = lambda a: pltpu.HBM(a.shape, a.dtype)
    out = pl.pallas_call(
        body, name=name,
        out_shape=tuple(hbm(a) for a in srcs) + tuple(hbm(a) for a in lands),
        in_specs=[HBM_SPEC] * (ns + nl) + [SEM_SPEC] * 3 + [ANY_SPEC] * len(after),
        out_specs=(HBM_SPEC,) * (ns + nl),
        input_output_aliases={i: i for i in range(ns + nl)},
        compiler_params=pltpu.CompilerParams(has_side_effects=SIDE_EFFECT),
    )(*srcs, *lands, send_sems, recv_sems, local_sems, *after)
    return list(out[ns:])


def _forward_refs(plan, lands, i, block):
    _, _, _, li, lbase, lstride, rows = plan[i]
    return _rows_of(lands[li], lbase, lstride, rows, block)


def _forward_start(name, lands, plan):
    nl, n = len(lands), len(plan)

    def body(*refs):
        l_refs = refs[:nl]
        send_sems, recv_sems = refs[nl:nl + 2]
        token = refs[nl + 2 + nl]
        me = _mesh_pos()
        for j, kc in enumerate(OTHER_CHIPS):
            for i in range(n):
                rows = _forward_refs(plan, l_refs, i, _flip(me, kc))
                pltpu.make_async_remote_copy(
                    src_ref=rows, dst_ref=rows,
                    send_sem=send_sems.at[3 * i + j], recv_sem=recv_sems.at[3 * i + j],
                    device_id=_flip(me, 1), device_id_type=MESH).start()
        token[...] = jnp.zeros_like(token)

    hbm = lambda a: pltpu.HBM(a.shape, a.dtype)
    out = pl.pallas_call(
        body, name=name,
        out_shape=((pltpu.SemaphoreType.DMA((3 * n,)), pltpu.SemaphoreType.DMA((3 * n,)))
                   + tuple(hbm(a) for a in lands) + (jax.ShapeDtypeStruct((8, LANE), F32),)),
        in_specs=[HBM_SPEC] * nl,
        out_specs=(SEM_SPEC, SEM_SPEC) + (HBM_SPEC,) * nl
        + (pl.BlockSpec(memory_space=pltpu.VMEM),),
        input_output_aliases={i: 2 + i for i in range(nl)},
        compiler_params=pltpu.CompilerParams(has_side_effects=SIDE_EFFECT),
    )(*[pltpu.with_memory_space_constraint(a, pltpu.HBM) for a in lands])
    _Order.tokens.append(out[-1])
    return out[0], out[1], out[2:2 + nl]


def _forward_wait(name, started, plan):
    send_sems, recv_sems, lands = started
    nl, n = len(lands), len(plan)
    after = [_Order.last] if _Order.last is not None else []

    def body(*refs):
        l_refs = refs[:nl]
        send_sems, recv_sems = refs[nl:nl + 2]
        me = _mesh_pos()
        for j, kc in enumerate(OTHER_CHIPS):
            for i in range(n):
                sent = _forward_refs(plan, l_refs, i, _flip(me, kc))
                came = _forward_refs(plan, l_refs, i, _flip(_flip(me, 1), kc))
                cp = pltpu.make_async_remote_copy(
                    src_ref=sent, dst_ref=came,
                    send_sem=send_sems.at[3 * i + j], recv_sem=recv_sems.at[3 * i + j],
                    device_id=_flip(me, 1), device_id_type=MESH)
                cp.wait_send()
                cp.wait_recv()

    hbm = lambda a: pltpu.HBM(a.shape, a.dtype)
    out = pl.pallas_call(
        body, name=name, out_shape=tuple(hbm(a) for a in lands),
        in_specs=[HBM_SPEC] * nl + [SEM_SPEC] * 2 + [ANY_SPEC] * len(after),
        out_specs=(HBM_SPEC,) * nl,
        input_output_aliases={i: i for i in range(nl)},
        compiler_params=pltpu.CompilerParams(has_side_effects=SIDE_EFFECT),
    )(*lands, send_sems, recv_sems, *after)
    return list(out)


def _regroup_rows(name, padded, r, rp, lo, hi):
    d = padded.shape[1]
    pack = 4 // padded.dtype.itemsize
    assert r % pack == 0 and rp % ROW_ALIGN == 0 and lo % (8 * pack) == 0 and hi % (8 * pack) == 0
    r2, rp2, lo2, hi2 = r // pack, rp // pack, lo // pack, hi // pack
    u32 = jnp.uint32

    def body(x_ref, main_ref, cut_ref):
        x = pltpu.bitcast(x_ref[...], u32)
        joined = jnp.concatenate([x[rp2 * j:rp2 * j + r2] for j in range(NDEV)], axis=0)
        main = jnp.concatenate([joined[:lo2], joined[hi2:]], axis=0)
        cut = jnp.concatenate([joined[lo2:hi2], jnp.zeros((LANE // pack - (hi2 - lo2), LANE), u32)],
                              axis=0)
        main_ref[...] = pltpu.bitcast(main, padded.dtype)
        cut_ref[...] = pltpu.bitcast(cut, padded.dtype)

    return pl.pallas_call(
        body, name=name,
        out_shape=[jax.ShapeDtypeStruct((NDEV * r - (hi - lo), d), padded.dtype),
                   jax.ShapeDtypeStruct((LANE, d), padded.dtype)],
        grid=(d // LANE,),
        in_specs=[pl.BlockSpec((NDEV * rp, LANE), lambda i: (0, i))],
        out_specs=[pl.BlockSpec((NDEV * r - (hi - lo), LANE), lambda i: (0, i)),
                   pl.BlockSpec((LANE, LANE), lambda i: (0, i))],
        compiler_params=_params(("parallel",)),
    )(padded)


def _ungroup_rows(name, main, cut, r, rp, lo, hi):
    d = main.shape[1]
    pack = 4 // main.dtype.itemsize
    r2, rp2, lo2, hi2 = r // pack, rp // pack, lo // pack, hi // pack
    u32 = jnp.uint32

    def body(main_ref, cut_ref, o_ref):
        m = pltpu.bitcast(main_ref[...], u32)
        c = pltpu.bitcast(cut_ref[...], u32)
        joined = jnp.concatenate([m[:lo2], c[:hi2 - lo2], m[lo2:]], axis=0)
        zeros = jnp.zeros((rp2 - r2, LANE), u32)
        parts = []
        for j in range(NDEV):
            parts += [joined[r2 * j:r2 * (j + 1)], zeros]
        o_ref[...] = pltpu.bitcast(jnp.concatenate(parts, axis=0), main.dtype)

    return pl.pallas_call(
        body, name=name,
        out_shape=jax.ShapeDtypeStruct((NDEV * rp, d), main.dtype),
        grid=(d // LANE,),
        in_specs=[pl.BlockSpec((main.shape[0], LANE), lambda i: (0, i)),
                  pl.BlockSpec((LANE, LANE), lambda i: (0, i))],
        out_specs=pl.BlockSpec((NDEV * rp, LANE), lambda i: (0, i)),
        compiler_params=_params(("parallel",)),
    )(main, cut)


def _sum_slabs(name, a):
    s, r, c = a.shape

    def body(a_ref, o_ref):
        acc = a_ref[0].astype(F32)
        for i in range(1, s):
            acc = acc + a_ref[i].astype(F32)
        o_ref[...] = acc

    return pl.pallas_call(body, name=name, out_shape=jax.ShapeDtypeStruct((r, c), F32))(a)


def _adamw_update(g, w, m, v):
    mn = ADAM_B1 * m + (1.0 - ADAM_B1) * g
    vn = ADAM_B2 * v + (1.0 - ADAM_B2) * (g * g)
    m_hat = mn / (1.0 - ADAM_B1 ** ADAM_STEP)
    v_hat = vn / (1.0 - ADAM_B2 ** ADAM_STEP)
    delta = -ADAM_LR * (m_hat / (jnp.sqrt(v_hat) + ADAM_EPS) + ADAM_WD * w)
    return delta, mn, vn


def _adamw(name, gsrc, w, m, v, transposed=False, tr=256):
    s = gsrc.shape[0]
    lead = w.ndim == 3
    r, c = w.shape[-2:]
    step = LANE if transposed else 8
    tr = max(t for t in range(step, min(tr, r) + 1, step) if r % t == 0)

    def body(g_ref, w_ref, m_ref, v_ref, go_ref, d_ref, mo_ref, vo_ref):
        g = g_ref[0].astype(F32)
        for i in range(1, s):
            g = g + g_ref[i].astype(F32)
        if transposed:
            g = g.T[:, :c]
        delta, mn, vn = _adamw_update(g, w_ref[...], m_ref[...], v_ref[...])
        go_ref[...] = g
        d_ref[...] = delta
        mo_ref[...] = mn
        vo_ref[...] = vn

    if lead:
        blk = pl.BlockSpec((None, tr, c), lambda i: (0, i, 0))
    else:
        blk = pl.BlockSpec((tr, c), lambda i: (i, 0))
    if transposed:
        g_spec = pl.BlockSpec((s, gsrc.shape[1], tr), lambda i: (0, 0, i))
    else:
        g_spec = pl.BlockSpec((s, tr, c), lambda i: (0, i, 0))
    return pl.pallas_call(
        body, name=name, out_shape=[jax.ShapeDtypeStruct(w.shape, F32)] * 4,
        grid=(r // tr,),
        in_specs=[g_spec, blk, blk, blk], out_specs=[blk] * 4,
        compiler_params=_params(("parallel",)),
    )(gsrc, w, m, v)


REPLICATED = ("ffn1_norm", "mix_norm", "q_norm", "k_norm", "conv_b", "dt_bias", "a_log",
              "d_skip", "ssd_norm", "ffn2_norm")
ALL_WEIGHTS = ("ffn1_norm", "ffn1_w_gate", "ffn1_w_up", "ffn1_w_down", "mix_norm", "w_in",
               "q_norm", "k_norm", "conv_w", "conv_b", "dt_bias", "a_log", "d_skip", "ssd_norm",
               "w_attn_branch", "w_ssd_branch", "w_out", "ffn2_norm", "ffn2_w_gate", "ffn2_w_up",
               "ffn2_w_down")
BIG = (("ffn1_w_gate", True, "gu1t", 0), ("ffn1_w_up", True, "gu1t", 1),
       ("ffn1_w_down", False, "d1", 0), ("w_in", True, "wint", 0),
       ("w_attn_branch", True, "abt", 0), ("w_ssd_branch", False, "sb", 0),
       ("w_out", False, "out", 0),
       ("ffn2_w_gate", True, "gu2t", 0), ("ffn2_w_up", True, "gu2t", 1),
       ("ffn2_w_down", False, "d2", 0))


def _nrows(shape, cols):
    return -(-math.prod(shape) // cols)


def _pack_rows(arrs, cols, row_tile):
    parts = []
    for a in arrs:
        flat = a.reshape(-1)
        nr = -(-flat.shape[0] // cols)
        parts.append(jnp.pad(flat, (0, nr * cols - flat.shape[0])).reshape(nr, cols))
    out = jnp.concatenate(parts, axis=0)
    return jnp.pad(out, ((0, _round_up(out.shape[0], row_tile) - out.shape[0]), (0, 0)))


def _unpack_rows(packed, shapes):
    cols = packed.shape[-1]
    out, r0 = [], 0
    for sh in shapes:
        nr = _nrows(sh, cols)
        out.append(packed[r0:r0 + nr].reshape(-1)[:math.prod(sh)].reshape(tuple(sh)))
        r0 += nr
    return out


def kernel(x, ffn1_norm, ffn1_w_gate, ffn1_w_up, ffn1_w_down, mix_norm, w_in, q_norm, k_norm, conv_w, conv_b, dt_bias, a_log, d_skip, ssd_norm, w_attn_branch, w_ssd_branch, w_out, ffn2_norm, ffn2_w_gate, ffn2_w_up, ffn2_w_down, loss_target, m_ffn1_norm, m_ffn1_w_gate, m_ffn1_w_up, m_ffn1_w_down, m_mix_norm, m_w_in, m_q_norm, m_k_norm, m_conv_w, m_conv_b, m_dt_bias, m_a_log, m_d_skip, m_ssd_norm, m_w_attn_branch, m_w_ssd_branch, m_w_out, m_ffn2_norm, m_ffn2_w_gate, m_ffn2_w_up, m_ffn2_w_down, v_ffn1_norm, v_ffn1_w_gate, v_ffn1_w_up, v_ffn1_w_down, v_mix_norm, v_w_in, v_q_norm, v_k_norm, v_conv_w, v_conv_b, v_dt_bias, v_a_log, v_d_skip, v_ssd_norm, v_w_attn_branch, v_w_ssd_branch, v_w_out, v_ffn2_norm, v_ffn2_w_gate, v_ffn2_w_up, v_ffn2_w_down):
    given = dict(locals())
    wts = {n: given[n] for n in ALL_WEIGHTS}
    mom = {n: given["m_" + n] for n in ALL_WEIGHTS}
    var = {n: given["v_" + n] for n in ALL_WEIGHTS}
    d = x.shape[-1]
    nh = dt_bias.shape[1]
    my = _dev_index(_mesh_pos())

    def row_form(n, col_sharded):
        a = wts[n][0].T if col_sharded else wts[n][0]
        a = jnp.pad(a, ((0, _round_up(a.shape[0], ROW_ALIGN) - a.shape[0]), (0, 0)))
        return a.astype(MXU_DTYPE)

    _Order.tokens, _Order.last = [], None
    shard = {n: row_form(n, cs) for n, cs, _, _ in BIG}
    entries = {buf: [e for e in BIG if e[2] == buf] for buf in dict.fromkeys(e[2] for e in BIG)}

    def buf_shape(buf):
        r, c = shard[entries[buf][0][0]].shape
        return (len(entries[buf]) * NDEV * r, c)

    def gather_plan(bufs):
        srcs, lands, plan = [], [], []
        for li, buf in enumerate(bufs):
            lands.append(lax.empty(buf_shape(buf), MXU_DTYPE))
            for n, _, _, pos in entries[buf]:
                r = shard[n].shape[0]
                plan.append((len(srcs), 0, 0, li, pos * NDEV * r, r, r))
                srcs.append(shard[n])
        return srcs, lands, plan

    def scatter_plan(bufs, grads):
        srcs, lands, plan, names = [], [], [], []
        for si, buf in enumerate(bufs):
            srcs.append(grads[buf])
            for n, _, _, pos in entries[buf]:
                r, c = shard[n].shape
                plan.append((si, pos * NDEV * r, r, len(lands), 0, r, r))
                lands.append(lax.empty((NDEV * r, c), MXU_DTYPE))
                names.append(n)
        return srcs, lands, plan, names

    first_bufs = ("gu1t", "d1")
    shards, dests, out_shapes = [], [], []
    for bi, buf in enumerate(first_bufs):
        out_shapes.append((buf_shape(buf), MXU_DTYPE))
        for n, _, _, pos in entries[buf]:
            r = shard[n].shape[0]
            shards.append(shard[n])
            dests.append((bi, pos * NDEV * r, r))
    conv_rows = _pack_rows([conv_w[0]], LANE, ROW_ALIGN)
    shards.append(conv_rows)
    dests.append((len(first_bufs), 0, conv_rows.shape[0]))
    out_shapes.append(((NDEV * conv_rows.shape[0], LANE), F32))
    gathered = _gather("gather_first", shards, dests, out_shapes)

    in_cols = w_in.shape[2]
    in_pad = _round_up(in_cols, ROW_ALIGN)
    dt_off = NDEV * in_cols - 2 * d - nh
    second_bufs = ("wint",)
    third_bufs = ("abt", "sb", "out", "gu2t", "d2")
    plan2 = gather_plan(second_bufs)
    started2 = _split_start("gather_in_start", *plan2, after=[gathered[0]],
                            relations=SAME_CORE_AND_SIBLING)
    forwarded, started3 = [], []

    class Weights(dict):
        def after_first_ffn(self):
            lands = _split_wait("gather_in_wait", started2, plan2[2],
                                relations=SAME_CORE_AND_SIBLING)
            forwarded.append(_forward_start("gather_in_forward", lands, plan2[2]))

        def __missing__(self, key):
            if key in ("maint", "dtt"):
                wint = _forward_wait("gather_in_arrive", forwarded[0], plan2[2])[0]
                plan3 = gather_plan(third_bufs)
                started3.append((_split_start("gather_rest_start", *plan3, after=[wint]), plan3[2]))
                self["maint"], self["dtt"] = _regroup_rows(
                    "regroup_w_in", wint, in_cols, in_pad, dt_off, dt_off + nh)
            else:
                st, plan = started3[0]
                for buf, a in zip(third_bufs, _split_wait("gather_rest_wait", st, plan)):
                    self[buf] = a
            return self[key]

    w = Weights(gu1t=gathered[0], d1=gathered[1])
    p = {n: wts[n] for n in REPLICATED}
    conv_all = gathered[-1].reshape(NDEV, conv_rows.shape[0] * LANE)[:, :math.prod(conv_w.shape[1:])]
    p["conv_w"] = (conv_all.reshape((NDEV,) + conv_w.shape[1:]).transpose(1, 0, 2)
                   .reshape(conv_w.shape[1], NDEV * conv_w.shape[2]))

    groups = (("scatter_late", ("gu2t", "d2", "out", "abt", "sb")),
              ("scatter_in", ("maint", "dtt")),
              ("scatter_first", ("gu1t", "d1")))
    in_flight = []

    class Grads(dict):
        def __setitem__(self, key, value):
            dict.__setitem__(self, key, value)
            for tag, need in groups:
                if key in need and all(k in self for k in need):
                    if tag == "scatter_in":
                        gwin = _ungroup_rows("ungroup_w_in", self["maint"], self["dtt"],
                                             in_cols, in_pad, dt_off, dt_off + nh)
                        bufs, grads = ("wint",), {"wint": gwin}
                    else:
                        bufs, grads = need, self
                    srcs, lands, plan, names = scatter_plan(bufs, grads)
                    in_flight.append((tag, _split_start(tag + "_start", srcs, lands, plan),
                                      plan, names))

    loss_vec, dx, gw, gp = _local_step(x[0], loss_target[0], w, p, Grads(),
                                       aw=w_attn_branch.shape[1])

    small_names = REPLICATED + ("conv_w",)
    small_shapes = [gp[n].shape for n in small_names]
    small = _pack_rows([gp[n] for n in small_names], LANE, 8)
    small_all = _exchange("exchange_small", [], [], small)[0]

    outs = [{}, {}, {}, {}]
    col_sharded_of = {n: cs for n, cs, _, _ in BIG}
    for tag, started, plan, names in in_flight:
        for n, rv in zip(names, _split_wait(tag + "_wait", started, plan)):
            rv = rv.reshape(NDEV, shard[n].shape[0], shard[n].shape[1])
            if col_sharded_of[n] and rv.shape[1] == wts[n].shape[2]:
                res = _adamw("adamw_" + n, rv, wts[n][0].T, mom[n][0].T, var[n][0].T)
                res = [a.T for a in res]
            else:
                res = _adamw("adamw_" + n, rv, wts[n][0], mom[n][0], var[n][0],
                             transposed=col_sharded_of[n])
            _Order.done(res)
            for k in range(4):
                outs[k][n] = res[k][None]

    small_g = _unpack_rows(_sum_slabs("sum_small_grads", small_all), small_shapes)
    small_g = dict(zip(small_names, small_g))
    cs = conv_w.shape[2]
    small_g["conv_w"] = lax.dynamic_slice_in_dim(small_g["conv_w"], my * cs, cs, axis=1)
    small_shard_shapes = [wts[n].shape[-2:] for n in small_names]
    sg = _pack_rows([small_g[n] for n in small_names], LANE, 8)
    sw = _pack_rows([wts[n] for n in small_names], LANE, 8)
    sm = _pack_rows([mom[n] for n in small_names], LANE, 8)
    sv = _pack_rows([var[n] for n in small_names], LANE, 8)
    res_small = _adamw("adamw_small", sg[None], sw, sm, sv, tr=sg.shape[0])
    for k in range(4):
        for n, a in zip(small_names, _unpack_rows(res_small[k], small_shard_shapes)):
            outs[k][n] = a.reshape(wts[n].shape)

    loss = lax.psum(0.5 * jnp.sum(loss_vec) / d, ("x", "y", "c"))
    result = [loss, dx[None]]
    for k in range(4):
        result += [outs[k][n] for n in ALL_WEIGHTS]
    return tuple(result)
```

```python
import functools
import math

import numpy as np
import jax
import jax.numpy as jnp
from jax import lax
from jax.experimental import pallas as pl
from jax.experimental.pallas import tpu as pltpu

F32 = jnp.float32
BF16 = jnp.bfloat16
MXU_DTYPE = BF16
ACT_DTYPE = BF16

NDEV = 8
EPS = 1e-6
HD = 64
QB = 128
PATTERNS = ((128, 1), (512, 4), (2048, 16))
ALIBI_MAX_EXP = 8.0
SSD_P = 64
SSD_N = 128
SSD_G = 4
SSD_Q = 128
SSD_K = 4
NEG = -1e30
LANE = 128
ROW_ALIGN = 16
VMEM_LIMIT = 56 * 1024 * 1024

ADAM_LR, ADAM_B1, ADAM_B2, ADAM_EPS, ADAM_WD, ADAM_STEP = 0.001, 0.9, 0.999, 1e-8, 0.01, 10

NN = (((1,), (0,)), ((), ()))
NT = (((1,), (1,)), ((), ()))
TN = (((0,), (0,)), ((), ()))


BNN = (((2,), (1,)), ((0,), (0,)))
BNT = (((2,), (2,)), ((0,), (0,)))
BTN = (((1,), (1,)), ((0,), (0,)))
_DOT_GRADS = {
    NN: (("g", "b", NT), ("a", "g", TN)),
    NT: (("g", "b", NN), ("g", "a", TN)),
    TN: (("b", "g", NT), ("a", "g", NN)),
    BNT: (("g", "b", BNN), ("g", "a", BTN)),
    BTN: (("b", "g", BNT), ("a", "g", BNN)),
}


def _mxu(a, b, dims):
    return lax.dot_general(a.astype(MXU_DTYPE), b.astype(MXU_DTYPE), dims,
                           preferred_element_type=F32)


@functools.partial(jax.custom_vjp, nondiff_argnums=(2,))
def _dot_vjp(a, b, dims):
    return _mxu(a, b, dims)


def _dot_vjp_fwd(a, b, dims):
    return _mxu(a, b, dims), (a.astype(MXU_DTYPE), b.astype(MXU_DTYPE))


def _dot_vjp_bwd(dims, res, g):
    ops = {"a": res[0], "b": res[1], "g": g}
    (x1, y1, d1), (x2, y2, d2) = _DOT_GRADS[dims]
    return _mxu(ops[x1], ops[y1], d1), _mxu(ops[x2], ops[y2], d2)


_dot_vjp.defvjp(_dot_vjp_fwd, _dot_vjp_bwd)


def _dot(a, b, dims=NN):
    return _dot_vjp(a, b, dims)


def _split3(a):
    hi = a.astype(BF16)
    r = a - hi.astype(F32)
    mid = r.astype(BF16)
    lo = (r - mid.astype(F32)).astype(BF16)
    return hi, mid, lo


def _dot3(a, b, dims=NN, split=0):
    if split == 0:
        bb = b.astype(BF16)
        parts = [lax.dot_general(s, bb, dims, preferred_element_type=F32) for s in _split3(a)]
    else:
        aa = a.astype(BF16)
        parts = [lax.dot_general(aa, s, dims, preferred_element_type=F32) for s in _split3(b)]
    return parts[0] + parts[1] + parts[2]


@jax.custom_vjp
def _spread(v, e):
    return _dot3(v, e)


def _spread_fwd(v, e):
    return _dot3(v, e), e


def _spread_bwd(e, g):
    return _dot3(g, e, NT), jnp.zeros_like(e)


_spread.defvjp(_spread_fwd, _spread_bwd)


@jax.custom_vjp
def _running_sum(a, lower):
    return _dot3(lower, a, NN, split=1)


def _running_sum_fwd(a, lower):
    return _dot3(lower, a, NN, split=1), lower


def _running_sum_bwd(lower, g):
    return _dot3(lower, g, TN, split=1), jnp.zeros_like(lower)


_running_sum.defvjp(_running_sum_fwd, _running_sum_bwd)


def _tile(n, cap):
    if n <= cap:
        return n
    best = None
    for t in range(LANE, cap + 1, LANE):
        if n % t == 0:
            best = t
    assert best is not None, (n, cap)
    return best


def _params(sem):
    return pltpu.CompilerParams(dimension_semantics=sem, vmem_limit_bytes=VMEM_LIMIT)


def _round_up(n, m):
    return -(-n // m) * m


class _Order:
    tokens = []
    last = None

    @classmethod
    def take(cls):
        out, cls.tokens = cls.tokens, []
        return out

    @classmethod
    def done(cls, result):
        cls.last = result[0] if isinstance(result, (list, tuple)) else result
        return result


ANY_SPEC = pl.BlockSpec(memory_space=pl.ANY)


def _mm(name, a, b, mode, out_dtype=F32, res=None, scale=1.0,
        cap_m=1408, cap_n=1408, cap_k=1408):
    segs = list(a) if isinstance(a, (list, tuple)) else [a]
    nseg = len(segs)
    if mode == "tn":
        k = segs[0].shape[0]
        widths = [s.shape[1] for s in segs]
        m = sum(widths)
        k2, n = b.shape
        tm = _tile(math.gcd(*widths), cap_m)
        tk = _tile(k, cap_k)
        counts = [wd // tm for wd in widths]
    else:
        m = segs[0].shape[0]
        widths = [s.shape[1] for s in segs]
        k = sum(widths)
        (k2, n) = b.shape if mode == "nn" else b.shape[::-1]
        tm = _tile(m, cap_m)
        tk = _tile(math.gcd(*widths), cap_k)
        counts = [wd // tk for wd in widths]
    assert k == k2, (name, [s.shape for s in segs], b.shape, mode)
    tn = _tile(n, cap_n)
    nk = k // tk
    starts = [sum(counts[:s]) for s in range(nseg)]
    dims = {"nn": NN, "nt": NT, "tn": TN}[mode]

    def a_spec(s):
        lo, cnt = starts[s], counts[s]
        if mode == "tn":
            if nseg == 1:
                return pl.BlockSpec((tk, tm), lambda i, j, kk: (kk, i))
            return pl.BlockSpec(
                (tk, tm), lambda i, j, kk: (jnp.where((i >= lo) & (i < lo + cnt), kk, 0),
                                            jnp.clip(i - lo, 0, cnt - 1)))
        if nseg == 1:
            return pl.BlockSpec((tm, tk), lambda i, j, kk: (i, kk))
        return pl.BlockSpec((tm, tk), lambda i, j, kk: (i, jnp.clip(kk - lo, 0, cnt - 1)))

    b_spec = (pl.BlockSpec((tn, tk), lambda i, j, kk: (j, kk)) if mode == "nt"
              else pl.BlockSpec((tk, tn), lambda i, j, kk: (kk, j)))
    o_spec = pl.BlockSpec((tm, tn), lambda i, j, kk: (i, j))
    has_res = res is not None
    use_acc = nk > 1 or nseg > 1
    ties = _Order.take()
    nt_ = len(ties)

    def body(*refs):
        a_refs = refs[:nseg]
        b_ref = refs[nseg]
        r_ref = refs[nseg + 1] if has_res else None
        o_ref = refs[nseg + 1 + has_res + nt_]
        scr = refs[nseg + 2 + has_res + nt_:]

        def finish(acc):
            if scale != 1.0:
                acc = acc * scale
            if has_res:
                acc = r_ref[...].astype(F32) + acc
            o_ref[...] = acc.astype(o_ref.dtype)

        if not use_acc:
            finish(_dot(a_refs[0][...], b_ref[...], dims))
            return
        acc_ref = scr[0]
        kk = pl.program_id(2)
        sel = pl.program_id(0) if mode == "tn" else kk

        @pl.when(kk == 0)
        def _():
            acc_ref[...] = jnp.zeros_like(acc_ref)

        for s in range(nseg):
            def add(s=s):
                acc_ref[...] += _dot(a_refs[s][...], b_ref[...], dims)
            if nseg == 1:
                add()
            else:
                pl.when((sel >= starts[s]) & (sel < starts[s] + counts[s]))(add)

        @pl.when(kk == nk - 1)
        def _():
            finish(acc_ref[...])

    in_specs = ([a_spec(s) for s in range(nseg)] + [b_spec] + ([o_spec] if has_res else [])
                + [ANY_SPEC] * nt_)
    args = tuple(segs) + (b,) + ((res,) if has_res else ()) + tuple(ties)
    return _Order.done(pl.pallas_call(
        body, name=name,
        out_shape=jax.ShapeDtypeStruct((m, n), out_dtype),
        grid=(m // tm, n // tn, nk),
        in_specs=in_specs, out_specs=o_spec,
        scratch_shapes=[pltpu.VMEM((tm, tn), F32)] if use_acc else [],
        compiler_params=_params(("parallel", "parallel", "arbitrary")),
    )(*args))


def _act(g, u):
    return _silu(g.astype(F32)) * u.astype(F32)


def _ffn_up(name, h, wgut, cap_m=512, cap_n=1408):
    m, k = h.shape
    dff = wgut.shape[0] // 2
    tm, tn = _tile(m, cap_m), _tile(dff, cap_n)
    nj = dff // tn
    ties = _Order.take()

    def body(h_ref, wg_ref, wu_ref, *rest):
        g_ref, u_ref, a_ref = rest[len(ties):]
        hv = h_ref[...]
        g = _dot(hv, wg_ref[...], NT)
        u = _dot(hv, wu_ref[...], NT)
        g_ref[...] = g.astype(g_ref.dtype)
        u_ref[...] = u.astype(u_ref.dtype)
        a_ref[...] = _act(g, u).astype(a_ref.dtype)

    o_spec = pl.BlockSpec((tm, tn), lambda i, j: (i, j))
    return _Order.done(pl.pallas_call(
        body, name=name, out_shape=[jax.ShapeDtypeStruct((m, dff), ACT_DTYPE)] * 3,
        grid=(m // tm, nj),
        in_specs=[pl.BlockSpec((tm, k), lambda i, j: (i, 0)),
                  pl.BlockSpec((tn, k), lambda i, j: (j, 0)),
                  pl.BlockSpec((tn, k), lambda i, j: (nj + j, 0))] + [ANY_SPEC] * len(ties),
        out_specs=[o_spec] * 3,
        compiler_params=_params(("parallel", "parallel")),
    )(h, wgut, wgut, *ties))


def _ffn_dact(name, dxo, wd, g, u, scale, cap_m=512, cap_n=1408):
    m, k = dxo.shape
    dff = wd.shape[0]
    tm, tn = _tile(m, cap_m), _tile(dff, cap_n)
    ties = _Order.take()

    def body(d_ref, w_ref, g_ref, u_ref, *rest):
        dg_ref, du_ref = rest[len(ties):]
        da = _dot(d_ref[...], w_ref[...], NT) * scale
        _, vjp = jax.vjp(_act, g_ref[...], u_ref[...])
        dg, du = vjp(da)
        dg_ref[...] = dg.astype(dg_ref.dtype)
        du_ref[...] = du.astype(du_ref.dtype)

    o_spec = pl.BlockSpec((tm, tn), lambda i, j: (i, j))
    return _Order.done(pl.pallas_call(
        body, name=name, out_shape=[jax.ShapeDtypeStruct((m, dff), ACT_DTYPE)] * 2,
        grid=(m // tm, dff // tn),
        in_specs=[pl.BlockSpec((tm, k), lambda i, j: (i, 0)),
                  pl.BlockSpec((tn, k), lambda i, j: (j, 0)), o_spec, o_spec]
        + [ANY_SPEC] * len(ties),
        out_specs=[o_spec] * 2,
        compiler_params=_params(("parallel", "parallel")),
    )(dxo, wd, g, u, *ties))


def _rw(name, fn, ins, outs, accs=(), tr=512, ncb=1):
    t = next(a.shape[0] for kind, a, _, _ in ins if kind == "row")
    assert t % tr == 0
    n_in = len(ins)
    n_pieces = sum(len(w) for w, _ in outs)

    def spec(kind, arr, width, base):
        if kind == "row":
            return pl.BlockSpec((tr, width), lambda j, i: (i, base + j))
        return pl.BlockSpec((arr.shape[0], width), lambda j, i: (0, base + j))

    in_specs = [spec(*s) for s in ins]
    out_shapes, out_specs = [], []
    for widths, dt in outs:
        w = sum(widths)
        out_shapes.append(jax.ShapeDtypeStruct((t, w * ncb), dt))
        out_specs.append(pl.BlockSpec((tr, w), lambda j, i: (i, j)))
    for rows, width in accs:
        out_shapes.append(jax.ShapeDtypeStruct((rows, width * ncb), F32))
        out_specs.append(pl.BlockSpec((rows, width), lambda j, i: (0, j)))

    ties = _Order.take()
    nt_ = len(ties)
    in_specs = in_specs + [ANY_SPEC] * nt_

    def body(*refs):
        vals = [r[...] for r in refs[:n_in]]
        res = fn(*vals)
        o_refs = refs[n_in + nt_:n_in + nt_ + len(outs)]
        a_refs = refs[n_in + nt_ + len(outs):]
        p = 0
        for (widths, _), o_ref in zip(outs, o_refs):
            off = 0
            for w in widths:
                if len(widths) == 1:
                    o_ref[...] = res[p].astype(o_ref.dtype)
                else:
                    o_ref[:, off:off + w] = res[p].astype(o_ref.dtype)
                off += w
                p += 1
        i = pl.program_id(1)
        for a_ref, v in zip(a_refs, res[n_pieces:]):
            @pl.when(i == 0)
            def _(a_ref=a_ref, v=v):
                a_ref[...] = v

            @pl.when(i > 0)
            def _(a_ref=a_ref, v=v):
                a_ref[...] += v

    return _Order.done(pl.pallas_call(
        body, name=name, out_shape=out_shapes,
        grid=(ncb, t // tr), in_specs=in_specs, out_specs=out_specs,
        compiler_params=_params(("parallel", "arbitrary")),
    )(*[a for _, a, _, _ in ins], *ties))


def _rms(x, g):
    x = x.astype(F32)
    return x * lax.rsqrt(jnp.mean(x * x, axis=-1, keepdims=True) + EPS) * g


def _silu(x):
    return x * jax.nn.sigmoid(x)


def _colsum(v):
    return jnp.sum(v, axis=0, keepdims=True)


def _pair_norm(x, g):
    w = 2 * HD
    ri = lax.broadcasted_iota(jnp.int32, (w, w), 0)
    ci = lax.broadcasted_iota(jnp.int32, (w, w), 1)
    same_head = ((ri < HD) == (ci < HD)).astype(F32)
    ms = _spread(x * x, same_head) * (1.0 / HD)
    return x * lax.rsqrt(ms + EPS) * g


ATTN_SCALE = 1.0 / math.sqrt(HD)


def _attn_bias(coef):
    key = lax.broadcasted_iota(jnp.int32, (QB, QB), 0)
    qry = lax.broadcasted_iota(jnp.int32, (QB, QB), 1)
    dist = (qry - key).astype(F32)
    own = jnp.where(qry >= key, -coef * dist, NEG)
    prev = jnp.where(qry <= key, -coef * (dist + float(QB)), NEG)
    return own, prev


def _attn_pair(qn, kcn, kpn, vc, vp, b_own, b_prev):
    nb = qn.shape[0]
    w = 2 * HD
    lane = lax.broadcasted_iota(jnp.int32, (1, 1, w), 2)
    eye = (lax.broadcasted_iota(jnp.int32, (QB, QB), 0)
           == lax.broadcasted_iota(jnp.int32, (QB, QB), 1)).astype(F32)
    out = jnp.zeros((nb, QB, w), F32)
    lb = jnp.zeros((nb * QB, w), F32)
    for hh in range(2):
        mask = ((lane < HD) if hh == 0 else (lane >= HD)).astype(F32)
        qm = qn * mask
        lc = _dot(kcn, qm, BNT) + b_own[hh]
        lp = _dot(kpn, qm, BNT) + b_prev[hh]
        m = lax.stop_gradient(jnp.maximum(jnp.max(lc, axis=1, keepdims=True),
                                          jnp.max(lp, axis=1, keepdims=True)))
        pc = jnp.exp(lc - m)
        pp = jnp.exp(lp - m)
        l = jnp.sum(pc, axis=1, keepdims=True) + jnp.sum(pp, axis=1, keepdims=True)
        inv = 1.0 / l
        out = out + (_dot(pc * inv, vc, BTN) + _dot(pp * inv, vp, BTN)) * mask
        diag = (eye * (m + jnp.log(l))).reshape(nb * QB, QB)
        lb = lb + _spread(diag, jnp.broadcast_to(mask[0], (QB, w)))
    return out, lb.reshape(nb, QB, w)


NORM_ROWS = 128
NORM_UNROLL = 4
EPILOGUE_ROWS = 512
ATTN_BATCH_FWD = 16
ATTN_BATCH_BWD = 8


def _unit_rows(u, d):
    r = u & (d - 1)
    n = u >> (d.bit_length() - 1)

    def rows(blk):
        start = pl.multiple_of(blk * (QB * d), QB * d)
        return pl.ds(start, QB) if d == 1 else pl.ds(start + r, QB, stride=d)

    return rows(n), rows(jnp.maximum(n - 1, 0)), n == 0


def _unit_batch(i, nbatch, d, bias, qf, kf, vf):
    units = [_unit_rows(i * nbatch + j, d) for j in range(nbatch)]
    cur = lambda ref: jnp.stack([ref[c, :] for c, _, _ in units])
    prv = lambda ref: jnp.stack([ref[p, :] for _, p, _ in units])
    b_own = [b[0] for b in bias]
    b_prev = [jnp.stack([jnp.where(first, NEG, b[1]) for _, _, first in units]) for b in bias]
    return units, (cur(qf), cur(kf), prv(kf), cur(vf), prv(vf), b_own, b_prev)


def _q_norm(x, g):
    return _pair_norm(x, g * ATTN_SCALE)


def _attn_prologue(t, q_ref, k_ref, v_ref, qg_ref, kg_ref, qf, kf, vf):
    def chunk(c, carry):
        rows = pl.ds(pl.multiple_of(c * NORM_ROWS, NORM_ROWS), NORM_ROWS)
        qf[rows, :] = _q_norm(q_ref[rows, :].astype(F32), qg_ref[...])
        kf[rows, :] = _pair_norm(k_ref[rows, :].astype(F32), kg_ref[...])
        vf[rows, :] = v_ref[rows, :].astype(F32)
        return carry
    lax.fori_loop(0, t // NORM_ROWS, chunk, 0, unroll=NORM_UNROLL)


def _attn_specs(t, bases):
    w = 2 * HD
    ins = [pl.BlockSpec((t, w), functools.partial(lambda p, c, b: (0, b + p), b=b)) for b in bases]
    gain = pl.BlockSpec((1, w), lambda p, c: (0, 0))
    blk = pl.BlockSpec((t, w), lambda p, c: (0, p))
    return ins, gain, blk


def _attn_fwd(name, proj, bases, qg, kg, coefs, d):
    t = proj.shape[0]
    npairs = coefs.shape[0] // 2
    w = 2 * HD
    ins, gain, blk = _attn_specs(t, bases)

    def body(coef_ref, q_ref, k_ref, v_ref, qg_ref, kg_ref, o_ref, l_ref, qf, kf, vf):
        p = pl.program_id(0)
        bias = (_attn_bias(coef_ref[2 * p]), _attn_bias(coef_ref[2 * p + 1]))
        _attn_prologue(t, q_ref, k_ref, v_ref, qg_ref, kg_ref, qf, kf, vf)

        def step(i, carry):
            units, ins = _unit_batch(i, ATTN_BATCH_FWD, d, bias, qf, kf, vf)
            o, lb = _attn_pair(*ins)
            for j, (cur, _, _) in enumerate(units):
                o_ref[cur, :] = o[j]
                l_ref[cur, :] = lb[j]
            return carry

        lax.fori_loop(0, t // QB // ATTN_BATCH_FWD, step, 0)

    return pl.pallas_call(
        body, name=name,
        out_shape=[jax.ShapeDtypeStruct((t, npairs * w), F32)] * 2,
        grid_spec=pltpu.PrefetchScalarGridSpec(
            num_scalar_prefetch=1, grid=(npairs,),
            in_specs=ins + [gain, gain], out_specs=[blk, blk],
            scratch_shapes=[pltpu.VMEM((t, w), F32)] * 3),
        compiler_params=_params(("arbitrary",)),
    )(coefs, proj, proj, proj, qg, kg)


def _attn_bwd(name, proj, bases, qg, kg, coefs, d, do, dl):
    t = proj.shape[0]
    npairs = coefs.shape[0] // 2
    w = 2 * HD
    ins, gain, blk = _attn_specs(t, bases)

    def body(coef_ref, q_ref, k_ref, v_ref, qg_ref, kg_ref, do_ref, dl_ref,
             dq_ref, dk_ref, dv_ref, dqg_ref, dkg_ref, qf, kf, vf, dqf, dkf, dvf):
        p = pl.program_id(0)
        bias = (_attn_bias(coef_ref[2 * p]), _attn_bias(coef_ref[2 * p + 1]))
        _attn_prologue(t, q_ref, k_ref, v_ref, qg_ref, kg_ref, qf, kf, vf)
        dkf[...] = jnp.zeros_like(dkf)
        dvf[...] = jnp.zeros_like(dvf)

        def step(i, carry):
            units, ins = _unit_batch(i, ATTN_BATCH_BWD, d, bias, qf, kf, vf)
            f = lambda a, b, c, e, g: _attn_pair(a, b, c, e, g, *ins[5:])
            _, vjp = jax.vjp(f, *ins[:5])
            cot = (jnp.stack([do_ref[cur, :] for cur, _, _ in units]),
                   jnp.stack([dl_ref[cur, :] for cur, _, _ in units]))
            dq, dkc, dkp, dvc, dvp = vjp(cot)
            for j, (cur, prv, _) in enumerate(units):
                dqf[cur, :] = dq[j]
                dkf[cur, :] += dkc[j]
                dkf[prv, :] += dkp[j]
                dvf[cur, :] += dvc[j]
                dvf[prv, :] += dvp[j]
            return carry

        lax.fori_loop(0, t // QB // ATTN_BATCH_BWD, step, 0)

        def chunk(c, carry):
            dqg_acc, dkg_acc = carry
            rows = pl.ds(pl.multiple_of(c * EPILOGUE_ROWS, EPILOGUE_ROWS), EPILOGUE_ROWS)
            _, vq = jax.vjp(_q_norm, q_ref[rows, :].astype(F32), qg_ref[...])
            dq, dqg = vq(dqf[rows, :])
            _, vk = jax.vjp(_pair_norm, k_ref[rows, :].astype(F32), kg_ref[...])
            dk, dkg = vk(dkf[rows, :])
            dq_ref[rows, :] = dq.astype(dq_ref.dtype)
            dk_ref[rows, :] = dk.astype(dk_ref.dtype)
            dv_ref[rows, :] = dvf[rows, :].astype(dv_ref.dtype)
            return dqg_acc + dqg, dkg_acc + dkg

        zero = jnp.zeros((1, w), F32)
        dqg, dkg = lax.fori_loop(0, t // EPILOGUE_ROWS, chunk, (zero, zero))

        @pl.when(p == 0)
        def _():
            dqg_ref[...] = dqg
            dkg_ref[...] = dkg

        @pl.when(p > 0)
        def _():
            dqg_ref[...] += dqg
            dkg_ref[...] += dkg

    big = jax.ShapeDtypeStruct((t, npairs * w), ACT_DTYPE)
    small = jax.ShapeDtypeStruct((1, w), F32)
    return pl.pallas_call(
        body, name=name,
        out_shape=[big, big, big, small, small],
        grid_spec=pltpu.PrefetchScalarGridSpec(
            num_scalar_prefetch=1, grid=(npairs,),
            in_specs=ins + [gain, gain, blk, blk],
            out_specs=[blk, blk, blk, gain, gain],
            scratch_shapes=[pltpu.VMEM((t, w), F32)] * 6),
        compiler_params=_params(("arbitrary",)),
    )(coefs, proj, proj, proj, qg, kg, do, dl)


CONV_ROWS = 256
CONV_HALO = 8


def _rows_back(x, s):
    return x if s == 0 else pltpu.roll(x, s, 0)


def _rows_ahead(x, s):
    return x if s == 0 else pltpu.roll(x, x.shape[0] - s, 0)


def _conv_pre(u, w, b):
    y = b
    for kk in range(SSD_K):
        y = y + w[kk:kk + 1, :] * _rows_back(u, SSD_K - 1 - kk)
    return y


def _stage_padded(dst, src_ref, t):
    zeros = jnp.zeros((CONV_HALO, dst.shape[1]), F32)
    dst[0:CONV_HALO, :] = zeros
    dst[t + CONV_HALO:t + 2 * CONV_HALO, :] = zeros
    dst[CONV_HALO:t + CONV_HALO, :] = src_ref[...].astype(F32)


def _chunk_rows(c):
    r0 = pl.multiple_of(c * CONV_ROWS, CONV_ROWS)
    return pl.ds(r0, CONV_ROWS + 2 * CONV_HALO), pl.ds(r0, CONV_ROWS)


def _conv_fwd(name, src, base, w, b, cw=128):
    t = src.shape[0]
    c = w.shape[1]
    centre = slice(CONV_HALO, CONV_HALO + CONV_ROWS)

    def body(u_ref, w_ref, b_ref, o_ref, up):
        _stage_padded(up, u_ref, t)
        wv, bv = w_ref[...], b_ref[...]

        def chunk(ci, carry):
            ext, rows = _chunk_rows(ci)
            y = _conv_pre(up[ext, :], wv, bv)
            o_ref[rows, :] = _silu(y)[centre].astype(o_ref.dtype)
            return carry

        lax.fori_loop(0, t // CONV_ROWS, chunk, 0)

    return pl.pallas_call(
        body, name=name, out_shape=jax.ShapeDtypeStruct((t, c), ACT_DTYPE),
        grid=(c // cw,),
        in_specs=[pl.BlockSpec((t, cw), lambda j: (0, base + j)),
                  pl.BlockSpec((SSD_K, cw), lambda j: (0, j)),
                  pl.BlockSpec((1, cw), lambda j: (0, j))],
        out_specs=pl.BlockSpec((t, cw), lambda j: (0, j)),
        scratch_shapes=[pltpu.VMEM((t + 2 * CONV_HALO, cw), F32)],
        compiler_params=_params(("parallel",)),
    )(src, w, b)


def _conv_bwd(name, src, base, w, b, dout, cw=128):
    t = src.shape[0]
    c = w.shape[1]

    centre = slice(CONV_HALO, CONV_HALO + CONV_ROWS)

    def body(u_ref, w_ref, b_ref, d_ref, du_ref, dw_ref, db_ref, up, dp):
        _stage_padded(up, u_ref, t)
        _stage_padded(dp, d_ref, t)
        wv, bv = w_ref[...], b_ref[...]

        def chunk(ci, carry):
            dws, db = carry
            ext, rows = _chunk_rows(ci)
            u = up[ext, :]
            y = _conv_pre(u, wv, bv)
            sg = jax.nn.sigmoid(y)
            dy = dp[ext, :] * (sg * (1.0 + y * (1.0 - sg)))
            du = jnp.zeros_like(u)
            new_dws = []
            for kk in range(SSD_K):
                s = SSD_K - 1 - kk
                du = du + wv[kk:kk + 1, :] * _rows_ahead(dy, s)
                new_dws.append(dws[kk] + _colsum((dy * _rows_back(u, s))[centre]))
            du_ref[rows, :] = du[centre].astype(du_ref.dtype)
            return tuple(new_dws), db + _colsum(dy[centre])

        zero = jnp.zeros((1, cw), F32)
        dws, db = lax.fori_loop(0, t // CONV_ROWS, chunk, ((zero,) * SSD_K, zero))
        for kk in range(SSD_K):
            dw_ref[kk:kk + 1, :] = dws[kk]
        db_ref[...] = db

    return pl.pallas_call(
        body, name=name,
        out_shape=[jax.ShapeDtypeStruct((t, c), ACT_DTYPE),
                   jax.ShapeDtypeStruct((SSD_K, c), F32),
                   jax.ShapeDtypeStruct((1, c), F32)],
        grid=(c // cw,),
        in_specs=[pl.BlockSpec((t, cw), lambda j: (0, base + j)),
                  pl.BlockSpec((SSD_K, cw), lambda j: (0, j)),
                  pl.BlockSpec((1, cw), lambda j: (0, j)),
                  pl.BlockSpec((t, cw), lambda j: (0, j))],
        out_specs=[pl.BlockSpec((t, cw), lambda j: (0, j)),
                   pl.BlockSpec((SSD_K, cw), lambda j: (0, j)),
                   pl.BlockSpec((1, cw), lambda j: (0, j))],
        scratch_shapes=[pltpu.VMEM((t + 2 * CONV_HALO, cw), F32)] * 2,
        compiler_params=_params(("parallel",)),
    )(src, w, b, dout)


def _softplus(x):
    return jnp.maximum(x, 0.0) + jnp.log(1.0 + jnp.exp(-jnp.abs(x)))


def _ssd_chunk(xbc, dtraw, bias, alog, states):
    wd = states[0].shape[1]
    nj = wd // SSD_P
    inner = SSD_G * wd
    dt = _softplus(dtraw + bias)
    a = dt * (-jnp.exp(alog))
    li = lax.broadcasted_iota(jnp.int32, (SSD_Q, SSD_Q), 0)
    si = lax.broadcasted_iota(jnp.int32, (SSD_Q, SSD_Q), 1)
    causal = li >= si
    acs = _running_sum(a, causal.astype(F32))
    acs_t = acs.T
    a_last = acs[SSD_Q - 1:SSD_Q, :]
    grow = jnp.exp(acs)
    shrink = jnp.exp(a_last - acs)
    hrow = lax.broadcasted_iota(jnp.int32, (LANE, wd), 0)
    wcol = lax.broadcasted_iota(jnp.int32, (LANE, wd), 1)
    lane = lax.broadcasted_iota(jnp.int32, (1, LANE), 1)
    ys, snext = [], []
    for g in range(SSD_G):
        lo = (hrow - g * nj) * SSD_P
        head_lanes = jnp.logical_and(wcol >= lo, wcol < lo + SSD_P).astype(F32)
        xs = xbc[:, g * wd:(g + 1) * wd]
        bm = xbc[:, inner + g * SSD_N:inner + (g + 1) * SSD_N]
        cm = xbc[:, inner + (SSD_G + g) * SSD_N:inner + (SSD_G + g + 1) * SSD_N]
        xdt = xs * _dot(dt, head_lanes)
        grow_x = _spread(grow, head_lanes)
        y_off = _dot(cm, states[g]) * grow_x
        s_new = (states[g] * grow_x[SSD_Q - 1:SSD_Q, :]
                 + _dot(bm, xdt * _dot(shrink, head_lanes), TN))
        cb = _dot(cm, bm, NT)
        pieces = []
        for i in range(wd // LANE):
            xp = xdt[:, i * LANE:(i + 1) * LANE]
            acc = jnp.zeros((SSD_Q, LANE), F32)
            for hh in range(LANE // SSD_P):
                h = g * nj + i * (LANE // SSD_P) + hh
                decay = jnp.exp(jnp.where(causal, acs[:, h:h + 1] - acs_t[h:h + 1, :], NEG))
                keep = jnp.logical_and(lane >= hh * SSD_P, lane < (hh + 1) * SSD_P).astype(F32)
                acc = acc + _dot(cb * decay, xp * keep)
            pieces.append(acc)
        y_diag = pieces[0] if len(pieces) == 1 else jnp.concatenate(pieces, axis=1)
        ys.append(y_diag + y_off)
        snext.append(s_new)
    return ys, snext


def _ssd_specs(cdim, wd, rev, nc):
    ch = (lambda c: nc - 1 - c) if rev else (lambda c: c)
    full = lambda width: pl.BlockSpec((SSD_Q, width), lambda c: (ch(c), 0))
    vec = pl.BlockSpec((1, LANE), lambda c: (0, 0))
    st = pl.BlockSpec((1, SSD_G, SSD_N, wd), lambda c: (ch(c), 0, 0, 0))
    return full, vec, st


def _ssd_fwd(name, xbc, dtraw, bias, alog, inner):
    t, cdim = xbc.shape
    wd = inner // SSD_G
    nc = t // SSD_Q
    full, vec, st = _ssd_specs(cdim, wd, False, nc)

    def body(x_ref, r_ref, b_ref, a_ref, y_ref, st_ref, s_scr):
        @pl.when(pl.program_id(0) == 0)
        def _():
            s_scr[...] = jnp.zeros_like(s_scr)

        sprev = [s_scr[g] for g in range(SSD_G)]
        ys, snext = _ssd_chunk(x_ref[...].astype(F32), r_ref[...], b_ref[...], a_ref[...], sprev)
        for g in range(SSD_G):
            st_ref[0, g] = sprev[g]
            y_ref[:, g * wd:(g + 1) * wd] = ys[g]
            s_scr[g] = snext[g]

    return pl.pallas_call(
        body, name=name,
        out_shape=[jax.ShapeDtypeStruct((t, inner), F32),
                   jax.ShapeDtypeStruct((nc, SSD_G, SSD_N, wd), F32)],
        grid=(nc,),
        in_specs=[full(cdim), full(LANE), vec, vec],
        out_specs=[full(inner), st],
        scratch_shapes=[pltpu.VMEM((SSD_G, SSD_N, wd), F32)],
        compiler_params=_params(("arbitrary",)),
    )(xbc, dtraw, bias, alog)


def _ssd_bwd(name, xbc, dtraw, bias, alog, states, dy, dxs_extra):
    t, cdim = xbc.shape
    inner = dy.shape[1]
    wd = inner // SSD_G
    nc = t // SSD_Q
    full, vec, st = _ssd_specs(cdim, wd, True, nc)

    def body(x_ref, r_ref, b_ref, a_ref, st_ref, dy_ref, dx0_ref,
             dx_ref, dr_ref, db_ref, da_ref, ds_scr):
        first = pl.program_id(0) == 0

        @pl.when(first)
        def _():
            ds_scr[...] = jnp.zeros_like(ds_scr)

        sprev = [st_ref[0, g] for g in range(SSD_G)]
        _, vjp = jax.vjp(_ssd_chunk, x_ref[...].astype(F32), r_ref[...], b_ref[...], a_ref[...],
                         sprev)
        dyv = dy_ref[...]
        dys = [dyv[:, g * wd:(g + 1) * wd] for g in range(SSD_G)]
        dsn = [ds_scr[g] for g in range(SSD_G)]
        dx, dr, db, da, dsp = vjp((dys, dsn))
        dx_ref[:, :inner] = dx[:, :inner] + dx0_ref[...].astype(F32)
        dx_ref[:, inner:] = dx[:, inner:]
        dr_ref[...] = dr
        for g in range(SSD_G):
            ds_scr[g] = dsp[g]

        @pl.when(first)
        def _():
            db_ref[...] = db
            da_ref[...] = da

        @pl.when(jnp.logical_not(first))
        def _():
            db_ref[...] += db
            da_ref[...] += da

    return pl.pallas_call(
        body, name=name,
        out_shape=[jax.ShapeDtypeStruct((t, cdim), F32),
                   jax.ShapeDtypeStruct((t, LANE), F32),
                   jax.ShapeDtypeStruct((1, LANE), F32),
                   jax.ShapeDtypeStruct((1, LANE), F32)],
        grid=(nc,),
        in_specs=[full(cdim), full(LANE), vec, vec, st, full(inner), full(inner)],
        out_specs=[full(cdim), full(LANE), vec, vec],
        scratch_shapes=[pltpu.VMEM((SSD_G, SSD_N, wd), F32)],
        compiler_params=_params(("arbitrary",)),
    )(xbc, dtraw, bias, alog, states, dy, dxs_extra)


def _mix(o0, o1, o2, l0, l1, l2):
    m = lax.stop_gradient(jnp.maximum(jnp.maximum(l0, l1), l2))
    e0, e1, e2 = jnp.exp(l0 - m), jnp.exp(l1 - m), jnp.exp(l2 - m)
    return (e0 * o0 + e1 * o1 + e2 * o2) / (e0 + e1 + e2)


def _gate(y, xs, z, dexp, gain):
    v = (y + xs.astype(F32) * dexp) * _silu(z.astype(F32))
    return _rms(v, gain)


def _merge(ga, gs, ap, sp):
    return jax.nn.sigmoid(ga.astype(F32)) * ap + jax.nn.sigmoid(gs.astype(F32)) * sp


def _alibi_coefs(hp):
    n = hp * len(PATTERNS)
    slopes = np.exp2(-ALIBI_MAX_EXP * np.arange(1, n + 1, dtype=np.float32) / n).astype(np.float32)
    return [jnp.asarray(slopes[g * hp:(g + 1) * hp] * np.float32(d))
            for g, (_, d) in enumerate(PATTERNS)]


def _local_step(x, tgt, w, p, gw_=None, aw=None):
    t, d = x.shape
    dff = w["gu1t"].shape[0] // 2
    aw = w["abt"].shape[1] if aw is None else aw
    hp = aw // HD
    qkv = len(PATTERNS) * aw
    inner = p["ssd_norm"].shape[1]
    nh = p["dt_bias"].shape[1]
    gw_ = {} if gw_ is None else gw_
    gw = inner // SSD_G
    cdim = inner + 2 * SSD_G * SSD_N
    z_off, xbc_off = 3 * qkv, 3 * qkv + inner
    ga_off = xbc_off + cdim
    gs_off = ga_off + d
    hw = d // 2
    assert z_off % gw == 0 and xbc_off % LANE == 0 and ga_off % hw == 0 and gs_off % hw == 0
    assert (nh // SSD_G) * SSD_P == gw and hp % 2 == 0 and aw % LANE == 0 and nh <= LANE
    gdt = MXU_DTYPE

    row = lambda a, width, base=0: ("row", a, width, base)
    const = lambda a, width, base=0: ("const", a, width, base)

    def rms_fwd(name, xin, g):
        return _rw(name, lambda xv, gv: (_rms(xv, gv),), [row(xin, d), const(g, d)],
                   [((d,), ACT_DTYPE)])[0]

    def rms_bwd(name, xin, g, dh, dres):
        def fn(xv, gv, dhv, drv):
            _, vjp = jax.vjp(_rms, xv, gv)
            dx, dg = vjp(dhv.astype(F32))
            return drv + dx, dg
        return _rw(name, fn, [row(xin, d), const(g, d), row(dh, d), row(dres, d)],
                   [((d,), F32)], accs=[(1, d)])

    def ffn_fwd(tag, xin, g, key_gu, key_d):
        h = rms_fwd(tag + "_norm", xin, g)
        gate, up, a = _ffn_up(tag + "_up", h, w[key_gu])
        xo = _mm(tag + "_down", a, w[key_d], "nn", F32, res=xin, scale=0.5)
        return xo, (h, gate, up, a)

    def ffn_bwd(tag, xin, g, wgut, wd, saved, dxo, key_gu, key_d):
        h, gate, up, a = saved
        gw_[key_d] = _mm(tag + "_dwd", a, dxo, "tn", gdt, scale=0.5)
        dgu = _ffn_dact(tag + "_dact", dxo, wd, gate, up, 0.5)
        gw_[key_gu] = _mm(tag + "_dwgu", dgu, h, "tn", gdt)
        dh = _mm(tag + "_dh", dgu, wgut, "nn", F32)
        return rms_bwd(tag + "_dnorm", xin, g, dh, dxo)

    x1, ffn1_saved = ffn_fwd("ffn1", x, p["ffn1_norm"], "gu1t", "d1")
    if hasattr(w, "after_first_ffn"):
        w.after_first_ffn()
    h2 = rms_fwd("mix_norm", x1, p["mix_norm"])
    proj = _mm("in_proj", h2, w["maint"], "nt", ACT_DTYPE, cap_m=512, cap_n=2944)
    dtraw = _mm("dt_proj", h2, w["dtt"], "nt", F32)

    coefs = _alibi_coefs(hp)
    qg2 = jnp.concatenate([p["q_norm"], p["q_norm"]], axis=1)
    kg2 = jnp.concatenate([p["k_norm"], p["k_norm"]], axis=1)
    pw = 2 * HD
    attn_bases = [[(off + gi * aw) // pw for off in (0, qkv, 2 * qkv)]
                  for gi in range(len(PATTERNS))]
    attn_o, attn_l = [], []
    for gi, (_, dil) in enumerate(PATTERNS):
        o, l = _attn_fwd(f"attn_fwd{gi}", proj, attn_bases[gi], qg2, kg2, coefs[gi], dil)
        attn_o.append(o)
        attn_l.append(l)
    ao = _rw("attn_mix", lambda *v: (_mix(*v),), [row(a, aw) for a in attn_o + attn_l],
             [((aw,), ACT_DTYPE)])[0]

    xbc = _conv_fwd("conv_fwd", proj, xbc_off // LANE, p["conv_w"], p["conv_b"])
    pad = lambda v: jnp.pad(v, ((0, 0), (0, LANE - nh)))
    bias_p, alog_p = pad(p["dt_bias"]), pad(p["a_log"])
    yssd, states = _ssd_fwd("ssd_fwd", xbc, dtraw, bias_p, alog_p, inner)
    dexp = jnp.repeat(p["d_skip"], SSD_P, axis=1)
    gate_ins = [row(yssd, gw), row(xbc, gw), row(proj, gw, z_off // gw),
                const(dexp, gw), const(p["ssd_norm"], gw)]
    yn = _rw("ssd_gate", lambda *v: (_gate(*v),), gate_ins, [((gw,), ACT_DTYPE)], ncb=SSD_G)[0]

    ap = _mm("attn_out", ao, w["abt"], "nt", F32)
    sp = _mm("ssd_out", yn, w["sb"], "nn", F32)
    merge_ins = [row(proj, hw, ga_off // hw), row(proj, hw, gs_off // hw), row(ap, hw), row(sp, hw)]
    mg = _rw("merge", lambda *v: (_merge(*v),), merge_ins, [((hw,), ACT_DTYPE)], ncb=2)[0]
    x2 = _mm("mix_out", mg, w["out"], "nn", F32, res=x1)
    x3, ffn2_saved = ffn_fwd("ffn2", x2, p["ffn2_norm"], "gu2t", "d2")

    def loss_fn(yv, tv):
        e = yv - tv
        return e * (1.0 / d), _colsum(e * e)
    dy, loss_vec = _rw("loss", loss_fn, [row(x3, d), row(tgt, d)], [((d,), F32)], accs=[(1, d)])

    gp = {}
    dx2, gp["ffn2_norm"] = ffn_bwd(
        "ffn2", x2, p["ffn2_norm"], w["gu2t"], w["d2"], ffn2_saved, dy, "gu2t", "d2")
    dmg = _mm("d_merge", dx2, w["out"], "nt", ACT_DTYPE)
    gw_["out"] = _mm("dw_out", mg, dx2, "tn", gdt)

    def merge_bwd(gav, gsv, apv, spv, dv):
        _, vjp = jax.vjp(_merge, gav, gsv, apv, spv)
        return vjp(dv.astype(F32))
    dga, dgs, dap, dsp = _rw("d_merge_gate", merge_bwd, merge_ins + [row(dmg, hw)],
                             [((hw,), ACT_DTYPE)] * 4, ncb=2)
    gw_["abt"] = _mm("dw_ab", dap, ao, "tn", gdt)
    dao = _mm("d_attn_o", dap, w["abt"], "nn", F32)
    gw_["sb"] = _mm("dw_sb", yn, dsp, "tn", gdt)
    dyn = _mm("d_ssd_y", dsp, w["sb"], "nt", F32)

    def gate_bwd(yv, xv, zv, dev, gv, dv):
        _, vjp = jax.vjp(_gate, yv, xv, zv, dev, gv)
        return vjp(dv)
    dyssd, dxs_gate, dz, ddexp, gp["ssd_norm"] = _rw(
        "d_ssd_gate", gate_bwd, gate_ins + [row(dyn, gw)],
        [((gw,), F32), ((gw,), F32), ((gw,), ACT_DTYPE)], accs=[(1, gw), (1, gw)], ncb=SSD_G)
    gp["d_skip"] = ddexp.reshape(nh, SSD_P).sum(axis=1).reshape(1, nh)

    dxbc, ddtraw, dbias, dalog = _ssd_bwd("ssd_bwd", xbc, dtraw, bias_p, alog_p, states,
                                          dyssd, dxs_gate)
    gp["dt_bias"], gp["a_log"] = dbias[:, :nh], dalog[:, :nh]
    du, gp["conv_w"], gp["conv_b"] = _conv_bwd("conv_bwd", proj, xbc_off // LANE,
                                               p["conv_w"], p["conv_b"], dxbc)

    def mix_bwd(*v):
        _, vjp = jax.vjp(_mix, *v[:6])
        return vjp(v[6])
    dmix = _rw("d_attn_mix", mix_bwd, [row(a, aw) for a in attn_o + attn_l] + [row(dao, aw)],
               [((aw,), F32)] * 6)
    dq, dk, dv = [], [], []
    dqg = dkg = None
    for gi, (_, dil) in enumerate(PATTERNS):
        r = _attn_bwd(f"attn_bwd{gi}", proj, attn_bases[gi], qg2, kg2, coefs[gi], dil,
                      dmix[gi], dmix[3 + gi])
        dq.append(r[0])
        dk.append(r[1])
        dv.append(r[2])
        dqg = r[3] if dqg is None else dqg + r[3]
        dkg = r[4] if dkg is None else dkg + r[4]
    gp["q_norm"] = dqg[:, :HD] + dqg[:, HD:]
    gp["k_norm"] = dkg[:, :HD] + dkg[:, HD:]

    segs = dq + dk + dv + [dz, du, dga, dgs]
    gw_["maint"] = _mm("dw_in", segs, h2, "tn", gdt)
    gw_["dtt"] = _mm("dw_dt", ddtraw, h2, "tn", gdt)
    dh2 = _mm("d_h2_main", segs, w["maint"], "nn", F32)
    dh2 = _mm("d_h2_dt", ddtraw, w["dtt"], "nn", F32, res=dh2)
    dx1, gp["mix_norm"] = rms_bwd("d_mix_norm", x1, p["mix_norm"], dh2, dx2)
    dx0, gp["ffn1_norm"] = ffn_bwd(
        "ffn1", x, p["ffn1_norm"], w["gu1t"], w["d1"], ffn1_saved, dx1, "gu1t", "d1")
    return loss_vec, dx0, gw_, gp


MESH = pl.DeviceIdType.MESH
HBM_SPEC = pl.BlockSpec(memory_space=pltpu.HBM)


def _mesh_pos():
    return lax.axis_index("x"), lax.axis_index("y"), lax.axis_index("c")


def _flip(pos, k):
    x, y, c = pos
    return (1 - x if k & 4 else x, 1 - y if k & 2 else y, 1 - c if k & 1 else c)


def _dev_index(pos):
    return 4 * pos[0] + 2 * pos[1] + pos[2]


def _rows_of(ref, base, stride, rows, pos):
    start = pl.multiple_of(base + stride * _dev_index(pos), ROW_ALIGN)
    return ref.at[pl.ds(start, rows)]


def _gather(name, shards, dests, out_shapes):
    n = len(shards)
    n_out = len(out_shapes)

    def body(*refs):
        x_refs = refs[:n]
        o_refs = refs[n:n + n_out]
        send_sems, recv_sems, local_sems = refs[n + n_out:]
        me = _mesh_pos()
        sibling = _flip(me, 1)
        chips = [_flip(me, 4), _flip(me, 2), _flip(me, 6)]

        def slot(i, block):
            k_out, base, stride = dests[i]
            return _rows_of(o_refs[k_out], base, stride, shards[i].shape[0], block)

        def copy(i, k, block, to, src=None):
            dst = slot(i, block)
            return pltpu.make_async_remote_copy(
                src_ref=dst if src is None else src, dst_ref=dst,
                send_sem=send_sems.at[7 * i + k], recv_sem=recv_sems.at[7 * i + k],
                device_id=to, device_id_type=MESH)

        mine = [pltpu.make_async_copy(x_refs[i], slot(i, me), local_sems.at[i]) for i in range(n)]
        for cp in mine:
            cp.start()
        first = []
        for i in range(n):
            first.append(copy(i, 0, me, sibling, src=x_refs[i]))
            first += [copy(i, 1 + j, me, chip, src=x_refs[i]) for j, chip in enumerate(chips)]
        for cp in first:
            cp.start()
        passed = []
        for j, chip in enumerate(chips):
            for i in range(n):
                copy(i, 1 + j, chip, me).wait_recv()
                fwd = copy(i, 4 + j, chip, sibling)
                fwd.start()
                passed.append(fwd)
        for i in range(n):
            copy(i, 0, sibling, me).wait_recv()
            for j, chip in enumerate(chips):
                copy(i, 4 + j, _flip(chip, 1), me).wait_recv()
        for cp in first + passed:
            cp.wait_send()
        for cp in mine:
            cp.wait()

    return pl.pallas_call(
        body, name=name,
        out_shape=[jax.ShapeDtypeStruct(s, dt) for s, dt in out_shapes],
        in_specs=[HBM_SPEC] * n, out_specs=[HBM_SPEC] * n_out,
        scratch_shapes=[pltpu.SemaphoreType.DMA((7 * n,)), pltpu.SemaphoreType.DMA((7 * n,)),
                        pltpu.SemaphoreType.DMA((n,))],
    )(*shards)


def _exchange(name, grads, srcs, small):
    n = len(srcs)
    ng = len(grads)

    def body(*refs):
        g_refs = refs[:ng]
        m_ref = refs[ng]
        r_refs = refs[ng + 1:ng + 1 + n]
        s_ref = refs[ng + 1 + n]
        send_sems, recv_sems, local_sems = refs[ng + 2 + n:]
        me = _mesh_pos()
        my = _dev_index(me)

        def slab(i, pos):
            gi, base, stride, rows = srcs[i]
            return _rows_of(g_refs[gi], base, stride, rows, pos)

        own = [pltpu.make_async_copy(slab(i, me), r_refs[i].at[my], local_sems.at[i])
               for i in range(n)]
        own.append(pltpu.make_async_copy(m_ref, s_ref.at[my], local_sems.at[n]))
        for cp in own:
            cp.start()

        def copies(k, src_pos, slot_pos):
            peer = _flip(me, k)
            si = _dev_index(slot_pos)
            out = [pltpu.make_async_remote_copy(
                src_ref=slab(i, src_pos), dst_ref=r_refs[i].at[si],
                send_sem=send_sems.at[7 * i + k - 1], recv_sem=recv_sems.at[7 * i + k - 1],
                device_id=peer, device_id_type=MESH) for i in range(n)]
            out.append(pltpu.make_async_remote_copy(
                src_ref=m_ref, dst_ref=s_ref.at[si],
                send_sem=send_sems.at[7 * n + k - 1], recv_sem=recv_sems.at[7 * n + k - 1],
                device_id=peer, device_id_type=MESH))
            return out

        sent = [cp for k in range(1, NDEV) for cp in copies(k, _flip(me, k), me)]
        for cp in sent:
            cp.start()
        for k in range(1, NDEV):
            for cp in copies(k, me, _flip(me, k)):
                cp.wait_recv()
        for cp in sent:
            cp.wait_send()
        for cp in own:
            cp.wait()

    out_shape = [jax.ShapeDtypeStruct((NDEV, rows, grads[gi].shape[1]), grads[gi].dtype)
                 for gi, _, _, rows in srcs]
    out_shape.append(jax.ShapeDtypeStruct((NDEV,) + small.shape, small.dtype))
    return pl.pallas_call(
        body, name=name, out_shape=out_shape,
        in_specs=[HBM_SPEC] * (ng + 1), out_specs=[HBM_SPEC] * (n + 1),
        scratch_shapes=[pltpu.SemaphoreType.DMA((7 * (n + 1),)),
                        pltpu.SemaphoreType.DMA((7 * (n + 1),)),
                        pltpu.SemaphoreType.DMA((n + 1,))],
    )(*grads, small)


SEM_SPEC = pl.BlockSpec(memory_space=pltpu.SEMAPHORE)
SIDE_EFFECT = pltpu.SideEffectType.DATAFLOW_SIDE_EFFECTING


def _split_refs(plan, srcs, lands, i, src_for, land_from):
    si, sbase, sstride, li, lbase, lstride, rows = plan[i]
    return (_rows_of(srcs[si], sbase, sstride, rows, src_for),
            _rows_of(lands[li], lbase, lstride, rows, land_from))


ALL_PEERS = tuple(range(1, NDEV))
SAME_CORE_AND_SIBLING = (1, 4, 2, 6)
OTHER_CHIPS = (4, 2, 6)


def _split_start(name, srcs, lands, plan, after=(), relations=ALL_PEERS):
    ns, nl, n = len(srcs), len(lands), len(plan)

    def body(*refs):
        s_refs = refs[:ns]
        l_refs = refs[ns:ns + nl]
        send_sems, recv_sems = refs[ns + nl + len(after):ns + nl + len(after) + 2]
        local_sems = refs[ns + nl + len(after) + 2]
        token = refs[ns + nl + len(after) + 3 + ns + nl]
        me = _mesh_pos()
        for i in range(n):
            src, dst = _split_refs(plan, s_refs, l_refs, i, me, me)
            pltpu.make_async_copy(src, dst, local_sems.at[i]).start()
        for k in relations:
            peer = _flip(me, k)
            for i in range(n):
                src, dst = _split_refs(plan, s_refs, l_refs, i, peer, me)
                pltpu.make_async_remote_copy(
                    src_ref=src, dst_ref=dst,
                    send_sem=send_sems.at[7 * i + k - 1], recv_sem=recv_sems.at[7 * i + k - 1],
                    device_id=peer, device_id_type=MESH).start()
        token[...] = jnp.zeros_like(token)

    hbm = lambda a: pltpu.HBM(a.shape, a.dtype)
    out_shape = ((pltpu.SemaphoreType.DMA((7 * n,)), pltpu.SemaphoreType.DMA((7 * n,)),
                  pltpu.SemaphoreType.DMA((n,)))
                 + tuple(hbm(a) for a in srcs) + tuple(hbm(a) for a in lands)
                 + (jax.ShapeDtypeStruct((8, LANE), F32),))
    out = pl.pallas_call(
        body, name=name, out_shape=out_shape,
        in_specs=[HBM_SPEC] * (ns + nl) + [ANY_SPEC] * len(after),
        out_specs=(SEM_SPEC, SEM_SPEC, SEM_SPEC) + (HBM_SPEC,) * (ns + nl)
        + (pl.BlockSpec(memory_space=pltpu.VMEM),),
        input_output_aliases={i: 3 + i for i in range(ns + nl)},
        compiler_params=pltpu.CompilerParams(has_side_effects=SIDE_EFFECT),
    )(*[pltpu.with_memory_space_constraint(a, pltpu.HBM) for a in tuple(srcs) + tuple(lands)],
      *after)
    _Order.tokens.append(out[-1])
    return out[0], out[1], out[2], out[3:3 + ns], out[3 + ns:3 + ns + nl]


def _split_wait(name, started, plan, relations=ALL_PEERS):
    send_sems, recv_sems, local_sems, srcs, lands = started
    ns, nl, n = len(srcs), len(lands), len(plan)
    after = [_Order.last] if _Order.last is not None else []

    def body(*refs):
        s_refs = refs[:ns]
        l_refs = refs[ns:ns + nl]
        send_sems, recv_sems, local_sems = refs[ns + nl:ns + nl + 3]
        me = _mesh_pos()
        for i in range(n):
            src, dst = _split_refs(plan, s_refs, l_refs, i, me, me)
            pltpu.make_async_copy(src, dst, local_sems.at[i]).wait()
        for k in relations:
            peer = _flip(me, k)
            for i in range(n):
                src, dst = _split_refs(plan, s_refs, l_refs, i, peer, peer)
                cp = pltpu.make_async_remote_copy(
                    src_ref=src, dst_ref=dst,
                    send_sem=send_sems.at[7 * i + k - 1], recv_sem=recv_sems.at[7 * i + k - 1],
                    device_id=peer, device_id_type=MESH)
                cp.wait_send()
                cp.wait_recv()

    hbm = lambda a: pltpu.HBM(a.shape, a.dtype)
    out = pl.pallas_call(
        body, name=name,
        out_shape=tuple(hbm(a) for a in srcs) + tuple(hbm(a) for a in lands),
        in_specs=[HBM_SPEC] * (ns + nl) + [SEM_SPEC] * 3 + [ANY_SPEC] * len(after),
        out_specs=(HBM_SPEC,) * (ns + nl),
        input_output_aliases={i: i for i in range(ns + nl)},
        compiler_params=pltpu.CompilerParams(has_side_effects=SIDE_EFFECT),
    )(*srcs, *lands, send_sems, recv_sems, local_sems, *after)
    return list(out[ns:])


def _forward_refs(plan, lands, i, block):
    _, _, _, li, lbase, lstride, rows = plan[i]
    return _rows_of(lands[li], lbase, lstride, rows, block)


def _forward_start(name, lands, plan):
    nl, n = len(lands), len(plan)

    def body(*refs):
        l_refs = refs[:nl]
        send_sems, recv_sems = refs[nl:nl + 2]
        token = refs[nl + 2 + nl]
        me = _mesh_pos()
        for j, kc in enumerate(OTHER_CHIPS):
            for i in range(n):
                rows = _forward_refs(plan, l_refs, i, _flip(me, kc))
                pltpu.make_async_remote_copy(
                    src_ref=rows, dst_ref=rows,
                    send_sem=send_sems.at[3 * i + j], recv_sem=recv_sems.at[3 * i + j],
                    device_id=_flip(me, 1), device_id_type=MESH).start()
        token[...] = jnp.zeros_like(token)

    hbm = lambda a: pltpu.HBM(a.shape, a.dtype)
    out = pl.pallas_call(
        body, name=name,
        out_shape=((pltpu.SemaphoreType.DMA((3 * n,)), pltpu.SemaphoreType.DMA((3 * n,)))
                   + tuple(hbm(a) for a in lands) + (jax.ShapeDtypeStruct((8, LANE), F32),)),
        in_specs=[HBM_SPEC] * nl,
        out_specs=(SEM_SPEC, SEM_SPEC) + (HBM_SPEC,) * nl
        + (pl.BlockSpec(memory_space=pltpu.VMEM),),
        input_output_aliases={i: 2 + i for i in range(nl)},
        compiler_params=pltpu.CompilerParams(has_side_effects=SIDE_EFFECT),
    )(*[pltpu.with_memory_space_constraint(a, pltpu.HBM) for a in lands])
    _Order.tokens.append(out[-1])
    return out[0], out[1], out[2:2 + nl]


def _forward_wait(name, started, plan):
    send_sems, recv_sems, lands = started
    nl, n = len(lands), len(plan)
    after = [_Order.last] if _Order.last is not None else []

    def body(*refs):
        l_refs = refs[:nl]
        send_sems, recv_sems = refs[nl:nl + 2]
        me = _mesh_pos()
        for j, kc in enumerate(OTHER_CHIPS):
            for i in range(n):
                sent = _forward_refs(plan, l_refs, i, _flip(me, kc))
                came = _forward_refs(plan, l_refs, i, _flip(_flip(me, 1), kc))
                cp = pltpu.make_async_remote_copy(
                    src_ref=sent, dst_ref=came,
                    send_sem=send_sems.at[3 * i + j], recv_sem=recv_sems.at[3 * i + j],
                    device_id=_flip(me, 1), device_id_type=MESH)
                cp.wait_send()
                cp.wait_recv()

    hbm = lambda a: pltpu.HBM(a.shape, a.dtype)
    out = pl.pallas_call(
        body, name=name, out_shape=tuple(hbm(a) for a in lands),
        in_specs=[HBM_SPEC] * nl + [SEM_SPEC] * 2 + [ANY_SPEC] * len(after),
        out_specs=(HBM_SPEC,) * nl,
        input_output_aliases={i: i for i in range(nl)},
        compiler_params=pltpu.CompilerParams(has_side_effects=SIDE_EFFECT),
    )(*lands, send_sems, recv_sems, *after)
    return list(out)


def _regroup_rows(name, padded, r, rp, lo, hi):
    d = padded.shape[1]
    pack = 4 // padded.dtype.itemsize
    assert r % pack == 0 and rp % ROW_ALIGN == 0 and lo % (8 * pack) == 0 and hi % (8 * pack) == 0
    r2, rp2, lo2, hi2 = r // pack, rp // pack, lo // pack, hi // pack
    u32 = jnp.uint32

    def body(x_ref, main_ref, cut_ref):
        x = pltpu.bitcast(x_ref[...], u32)
        joined = jnp.concatenate([x[rp2 * j:rp2 * j + r2] for j in range(NDEV)], axis=0)
        main = jnp.concatenate([joined[:lo2], joined[hi2:]], axis=0)
        cut = jnp.concatenate([joined[lo2:hi2], jnp.zeros((LANE // pack - (hi2 - lo2), LANE), u32)],
                              axis=0)
        main_ref[...] = pltpu.bitcast(main, padded.dtype)
        cut_ref[...] = pltpu.bitcast(cut, padded.dtype)

    return pl.pallas_call(
        body, name=name,
        out_shape=[jax.ShapeDtypeStruct((NDEV * r - (hi - lo), d), padded.dtype),
                   jax.ShapeDtypeStruct((LANE, d), padded.dtype)],
        grid=(d // LANE,),
        in_specs=[pl.BlockSpec((NDEV * rp, LANE), lambda i: (0, i))],
        out_specs=[pl.BlockSpec((NDEV * r - (hi - lo), LANE), lambda i: (0, i)),
                   pl.BlockSpec((LANE, LANE), lambda i: (0, i))],
        compiler_params=_params(("parallel",)),
    )(padded)


def _ungroup_rows(name, main, cut, r, rp, lo, hi):
    d = main.shape[1]
    pack = 4 // main.dtype.itemsize
    r2, rp2, lo2, hi2 = r // pack, rp // pack, lo // pack, hi // pack
    u32 = jnp.uint32

    def body(main_ref, cut_ref, o_ref):
        m = pltpu.bitcast(main_ref[...], u32)
        c = pltpu.bitcast(cut_ref[...], u32)
        joined = jnp.concatenate([m[:lo2], c[:hi2 - lo2], m[lo2:]], axis=0)
        zeros = jnp.zeros((rp2 - r2, LANE), u32)
        parts = []
        for j in range(NDEV):
            parts += [joined[r2 * j:r2 * (j + 1)], zeros]
        o_ref[...] = pltpu.bitcast(jnp.concatenate(parts, axis=0), main.dtype)

    return pl.pallas_call(
        body, name=name,
        out_shape=jax.ShapeDtypeStruct((NDEV * rp, d), main.dtype),
        grid=(d // LANE,),
        in_specs=[pl.BlockSpec((main.shape[0], LANE), lambda i: (0, i)),
                  pl.BlockSpec((LANE, LANE), lambda i: (0, i))],
        out_specs=pl.BlockSpec((NDEV * rp, LANE), lambda i: (0, i)),
        compiler_params=_params(("parallel",)),
    )(main, cut)


def _sum_slabs(name, a):
    s, r, c = a.shape

    def body(a_ref, o_ref):
        acc = a_ref[0].astype(F32)
        for i in range(1, s):
            acc = acc + a_ref[i].astype(F32)
        o_ref[...] = acc

    return pl.pallas_call(body, name=name, out_shape=jax.ShapeDtypeStruct((r, c), F32))(a)


def _adamw_update(g, w, m, v):
    mn = ADAM_B1 * m + (1.0 - ADAM_B1) * g
    vn = ADAM_B2 * v + (1.0 - ADAM_B2) * (g * g)
    m_hat = mn / (1.0 - ADAM_B1 ** ADAM_STEP)
    v_hat = vn / (1.0 - ADAM_B2 ** ADAM_STEP)
    delta = -ADAM_LR * (m_hat / (jnp.sqrt(v_hat) + ADAM_EPS) + ADAM_WD * w)
    return delta, mn, vn


def _adamw(name, gsrc, w, m, v, transposed=False, tr=256):
    s = gsrc.shape[0]
    lead = w.ndim == 3
    r, c = w.shape[-2:]
    step = LANE if transposed else 8
    tr = max(t for t in range(step, min(tr, r) + 1, step) if r % t == 0)

    def body(g_ref, w_ref, m_ref, v_ref, go_ref, d_ref, mo_ref, vo_ref):
        g = g_ref[0].astype(F32)
        for i in range(1, s):
            g = g + g_ref[i].astype(F32)
        if transposed:
            g = g.T[:, :c]
        delta, mn, vn = _adamw_update(g, w_ref[...], m_ref[...], v_ref[...])
        go_ref[...] = g
        d_ref[...] = delta
        mo_ref[...] = mn
        vo_ref[...] = vn

    if lead:
        blk = pl.BlockSpec((None, tr, c), lambda i: (0, i, 0))
    else:
        blk = pl.BlockSpec((tr, c), lambda i: (i, 0))
    if transposed:
        g_spec = pl.BlockSpec((s, gsrc.shape[1], tr), lambda i: (0, 0, i))
    else:
        g_spec = pl.BlockSpec((s, tr, c), lambda i: (0, i, 0))
    return pl.pallas_call(
        body, name=name, out_shape=[jax.ShapeDtypeStruct(w.shape, F32)] * 4,
        grid=(r // tr,),
        in_specs=[g_spec, blk, blk, blk], out_specs=[blk] * 4,
        compiler_params=_params(("parallel",)),
    )(gsrc, w, m, v)


REPLICATED = ("ffn1_norm", "mix_norm", "q_norm", "k_norm", "conv_b", "dt_bias", "a_log",
              "d_skip", "ssd_norm", "ffn2_norm")
ALL_WEIGHTS = ("ffn1_norm", "ffn1_w_gate", "ffn1_w_up", "ffn1_w_down", "mix_norm", "w_in",
               "q_norm", "k_norm", "conv_w", "conv_b", "dt_bias", "a_log", "d_skip", "ssd_norm",
               "w_attn_branch", "w_ssd_branch", "w_out", "ffn2_norm", "ffn2_w_gate", "ffn2_w_up",
               "ffn2_w_down")
BIG = (("ffn1_w_gate", True, "gu1t", 0), ("ffn1_w_up", True, "gu1t", 1),
       ("ffn1_w_down", False, "d1", 0), ("w_in", True, "wint", 0),
       ("w_attn_branch", True, "abt", 0), ("w_ssd_branch", False, "sb", 0),
       ("w_out", False, "out", 0),
       ("ffn2_w_gate", True, "gu2t", 0), ("ffn2_w_up", True, "gu2t", 1),
       ("ffn2_w_down", False, "d2", 0))


def _nrows(shape, cols):
    return -(-math.prod(shape) // cols)


def _pack_rows(arrs, cols, row_tile):
    parts = []
    for a in arrs:
        flat = a.reshape(-1)
        nr = -(-flat.shape[0] // cols)
        parts.append(jnp.pad(flat, (0, nr * cols - flat.shape[0])).reshape(nr, cols))
    out = jnp.concatenate(parts, axis=0)
    return jnp.pad(out, ((0, _round_up(out.shape[0], row_tile) - out.shape[0]), (0, 0)))


def _unpack_rows(packed, shapes):
    cols = packed.shape[-1]
    out, r0 = [], 0
    for sh in shapes:
        nr = _nrows(sh, cols)
        out.append(packed[r0:r0 + nr].reshape(-1)[:math.prod(sh)].reshape(tuple(sh)))
        r0 += nr
    return out


def kernel(x, ffn1_norm, ffn1_w_gate, ffn1_w_up, ffn1_w_down, mix_norm, w_in, q_norm, k_norm, conv_w, conv_b, dt_bias, a_log, d_skip, ssd_norm, w_attn_branch, w_ssd_branch, w_out, ffn2_norm, ffn2_w_gate, ffn2_w_up, ffn2_w_down, loss_target, m_ffn1_norm, m_ffn1_w_gate, m_ffn1_w_up, m_ffn1_w_down, m_mix_norm, m_w_in, m_q_norm, m_k_norm, m_conv_w, m_conv_b, m_dt_bias, m_a_log, m_d_skip, m_ssd_norm, m_w_attn_branch, m_w_ssd_branch, m_w_out, m_ffn2_norm, m_ffn2_w_gate, m_ffn2_w_up, m_ffn2_w_down, v_ffn1_norm, v_ffn1_w_gate, v_ffn1_w_up, v_ffn1_w_down, v_mix_norm, v_w_in, v_q_norm, v_k_norm, v_conv_w, v_conv_b, v_dt_bias, v_a_log, v_d_skip, v_ssd_norm, v_w_attn_branch, v_w_ssd_branch, v_w_out, v_ffn2_norm, v_ffn2_w_gate, v_ffn2_w_up, v_ffn2_w_down):
    given = dict(locals())
    wts = {n: given[n] for n in ALL_WEIGHTS}
    mom = {n: given["m_" + n] for n in ALL_WEIGHTS}
    var = {n: given["v_" + n] for n in ALL_WEIGHTS}
    d = x.shape[-1]
    nh = dt_bias.shape[1]
    my = _dev_index(_mesh_pos())

    def row_form(n, col_sharded):
        a = wts[n][0].T if col_sharded else wts[n][0]
        a = jnp.pad(a, ((0, _round_up(a.shape[0], ROW_ALIGN) - a.shape[0]), (0, 0)))
        return a.astype(MXU_DTYPE)

    _Order.tokens, _Order.last = [], None
    shard = {n: row_form(n, cs) for n, cs, _, _ in BIG}
    entries = {buf: [e for e in BIG if e[2] == buf] for buf in dict.fromkeys(e[2] for e in BIG)}

    def buf_shape(buf):
        r, c = shard[entries[buf][0][0]].shape
        return (len(entries[buf]) * NDEV * r, c)

    def gather_plan(bufs):
        srcs, lands, plan = [], [], []
        for li, buf in enumerate(bufs):
            lands.append(lax.empty(buf_shape(buf), MXU_DTYPE))
            for n, _, _, pos in entries[buf]:
                r = shard[n].shape[0]
                plan.append((len(srcs), 0, 0, li, pos * NDEV * r, r, r))
                srcs.append(shard[n])
        return srcs, lands, plan

    def scatter_plan(bufs, grads):
        srcs, lands, plan, names = [], [], [], []
        for si, buf in enumerate(bufs):
            srcs.append(grads[buf])
            for n, _, _, pos in entries[buf]:
                r, c = shard[n].shape
                plan.append((si, pos * NDEV * r, r, len(lands), 0, r, r))
                lands.append(lax.empty((NDEV * r, c), MXU_DTYPE))
                names.append(n)
        return srcs, lands, plan, names

    first_bufs = ("gu1t",)
    shards, dests, out_shapes = [], [], []
    for bi, buf in enumerate(first_bufs):
        out_shapes.append((buf_shape(buf), MXU_DTYPE))
        for n, _, _, pos in entries[buf]:
            r = shard[n].shape[0]
            shards.append(shard[n])
            dests.append((bi, pos * NDEV * r, r))
    conv_rows = _pack_rows([conv_w[0]], LANE, ROW_ALIGN)
    shards.append(conv_rows)
    dests.append((len(first_bufs), 0, conv_rows.shape[0]))
    out_shapes.append(((NDEV * conv_rows.shape[0], LANE), F32))
    gathered = _gather("gather_first", shards, dests, out_shapes)

    in_cols = w_in.shape[2]
    in_pad = _round_up(in_cols, ROW_ALIGN)
    dt_off = NDEV * in_cols - 2 * d - nh
    second_bufs = ("wint",)
    third_bufs = ("abt", "sb", "out", "gu2t", "d2")
    plan_d1 = gather_plan(("d1",))
    started_d1 = _split_start("gather_down_start", *plan_d1, after=[gathered[0]])
    plan2 = gather_plan(second_bufs)
    started2 = _split_start("gather_in_start", *plan2, after=[started_d1[3][0]],
                            relations=SAME_CORE_AND_SIBLING)
    forwarded, started3 = [], []

    class Weights(dict):
        def after_first_ffn(self):
            lands = _split_wait("gather_in_wait", started2, plan2[2],
                                relations=SAME_CORE_AND_SIBLING)
            forwarded.append(_forward_start("gather_in_forward", lands, plan2[2]))

        def __missing__(self, key):
            if key == "d1":
                self["d1"] = _split_wait("gather_down_wait", started_d1, plan_d1[2])[0]
            elif key in ("maint", "dtt"):
                wint = _forward_wait("gather_in_arrive", forwarded[0], plan2[2])[0]
                plan3 = gather_plan(third_bufs)
                started3.append((_split_start("gather_rest_start", *plan3, after=[wint]), plan3[2]))
                self["maint"], self["dtt"] = _regroup_rows(
                    "regroup_w_in", wint, in_cols, in_pad, dt_off, dt_off + nh)
            else:
                st, plan = started3[0]
                for buf, a in zip(third_bufs, _split_wait("gather_rest_wait", st, plan)):
                    self[buf] = a
            return self[key]

    w = Weights(gu1t=gathered[0])
    p = {n: wts[n] for n in REPLICATED}
    conv_all = gathered[-1].reshape(NDEV, conv_rows.shape[0] * LANE)[:, :math.prod(conv_w.shape[1:])]
    p["conv_w"] = (conv_all.reshape((NDEV,) + conv_w.shape[1:]).transpose(1, 0, 2)
                   .reshape(conv_w.shape[1], NDEV * conv_w.shape[2]))

    groups = (("scatter_late", ("gu2t", "d2", "out", "abt", "sb")),
              ("scatter_in", ("maint", "dtt")),
              ("scatter_first", ("gu1t", "d1")))
    in_flight = []

    class Grads(dict):
        def __setitem__(self, key, value):
            dict.__setitem__(self, key, value)
            for tag, need in groups:
                if key in need and all(k in self for k in need):
                    if tag == "scatter_in":
                        gwin = _ungroup_rows("ungroup_w_in", self["maint"], self["dtt"],
                                             in_cols, in_pad, dt_off, dt_off + nh)
                        bufs, grads = ("wint",), {"wint": gwin}
                    else:
                        bufs, grads = need, self
                    srcs, lands, plan, names = scatter_plan(bufs, grads)
                    in_flight.append((tag, _split_start(tag + "_start", srcs, lands, plan),
                                      plan, names))

    loss_vec, dx, gw, gp = _local_step(x[0], loss_target[0], w, p, Grads(),
                                       aw=w_attn_branch.shape[1])

    small_names = REPLICATED + ("conv_w",)
    small_shapes = [gp[n].shape for n in small_names]
    small = _pack_rows([gp[n] for n in small_names], LANE, 8)
    small_all = _exchange("exchange_small", [], [], small)[0]

    outs = [{}, {}, {}, {}]
    col_sharded_of = {n: cs for n, cs, _, _ in BIG}
    for tag, started, plan, names in in_flight:
        for n, rv in zip(names, _split_wait(tag + "_wait", started, plan)):
            rv = rv.reshape(NDEV, shard[n].shape[0], shard[n].shape[1])
            if col_sharded_of[n] and rv.shape[1] == wts[n].shape[2]:
                res = _adamw("adamw_" + n, rv, wts[n][0].T, mom[n][0].T, var[n][0].T)
                res = [a.T for a in res]
            else:
                res = _adamw("adamw_" + n, rv, wts[n][0], mom[n][0], var[n][0],
                             transposed=col_sharded_of[n])
            _Order.done(res)
            for k in range(4):
                outs[k][n] = res[k][None]

    small_g = _unpack_rows(_sum_slabs("sum_small_grads", small_all), small_shapes)
    small_g = dict(zip(small_names, small_g))
    cs = conv_w.shape[2]
    small_g["conv_w"] = lax.dynamic_slice_in_dim(small_g["conv_w"], my * cs, cs, axis=1)
    small_shard_shapes = [wts[n].shape[-2:] for n in small_names]
    sg = _pack_rows([small_g[n] for n in small_names], LANE, 8)
    sw = _pack_rows([wts[n] for n in small_names], LANE, 8)
    sm = _pack_rows([mom[n] for n in small_names], LANE, 8)
    sv = _pack_rows([var[n] for n in small_names], LANE, 8)
    res_small = _adamw("adamw_small", sg[None], sw, sm, sv, tr=sg.shape[0])
    for k in range(4):
        for n, a in zip(small_names, _unpack_rows(res_small[k], small_shard_shapes)):
            outs[k][n] = a.reshape(wts[n].shape)

    loss = lax.psum(0.5 * jnp.sum(loss_vec) / d, ("x", "y", "c"))
    result = [loss, dx[None]]
    for k in range(4):
        result += [outs[k][n] for n in ALL_WEIGHTS]
    return tuple(result)
```

```python
import functools
import math

import numpy as np
import jax
import jax.numpy as jnp
from jax import lax
from jax.experimental import pallas as pl
from jax.experimental.pallas import tpu as pltpu

F32 = jnp.float32
BF16 = jnp.bfloat16
MXU_DTYPE = BF16
ACT_DTYPE = BF16

NDEV = 8
EPS = 1e-6
HD = 64
QB = 128
PATTERNS = ((128, 1), (512, 4), (2048, 16))
ALIBI_MAX_EXP = 8.0
SSD_P = 64
SSD_N = 128
SSD_G = 4
SSD_Q = 128
SSD_K = 4
NEG = -1e30
LANE = 128
ROW_ALIGN = 16
VMEM_LIMIT = 56 * 1024 * 1024

ADAM_LR, ADAM_B1, ADAM_B2, ADAM_EPS, ADAM_WD, ADAM_STEP = 0.001, 0.9, 0.999, 1e-8, 0.01, 10

NN = (((1,), (0,)), ((), ()))
NT = (((1,), (1,)), ((), ()))
TN = (((0,), (0,)), ((), ()))


BNN = (((2,), (1,)), ((0,), (0,)))
BNT = (((2,), (2,)), ((0,), (0,)))
BTN = (((1,), (1,)), ((0,), (0,)))
_DOT_GRADS = {
    NN: (("g", "b", NT), ("a", "g", TN)),
    NT: (("g", "b", NN), ("g", "a", TN)),
    TN: (("b", "g", NT), ("a", "g", NN)),
    BNT: (("g", "b", BNN), ("g", "a", BTN)),
    BTN: (("b", "g", BNT), ("a", "g", BNN)),
}


def _mxu(a, b, dims):
    return lax.dot_general(a.astype(MXU_DTYPE), b.astype(MXU_DTYPE), dims,
                           preferred_element_type=F32)


@functools.partial(jax.custom_vjp, nondiff_argnums=(2,))
def _dot_vjp(a, b, dims):
    return _mxu(a, b, dims)


def _dot_vjp_fwd(a, b, dims):
    return _mxu(a, b, dims), (a.astype(MXU_DTYPE), b.astype(MXU_DTYPE))


def _dot_vjp_bwd(dims, res, g):
    ops = {"a": res[0], "b": res[1], "g": g}
    (x1, y1, d1), (x2, y2, d2) = _DOT_GRADS[dims]
    return _mxu(ops[x1], ops[y1], d1), _mxu(ops[x2], ops[y2], d2)


_dot_vjp.defvjp(_dot_vjp_fwd, _dot_vjp_bwd)


def _dot(a, b, dims=NN):
    return _dot_vjp(a, b, dims)


def _split3(a):
    hi = a.astype(BF16)
    r = a - hi.astype(F32)
    mid = r.astype(BF16)
    lo = (r - mid.astype(F32)).astype(BF16)
    return hi, mid, lo


def _dot3(a, b, dims=NN, split=0):
    if split == 0:
        bb = b.astype(BF16)
        parts = [lax.dot_general(s, bb, dims, preferred_element_type=F32) for s in _split3(a)]
    else:
        aa = a.astype(BF16)
        parts = [lax.dot_general(aa, s, dims, preferred_element_type=F32) for s in _split3(b)]
    return parts[0] + parts[1] + parts[2]


@jax.custom_vjp
def _spread(v, e):
    return _dot3(v, e)


def _spread_fwd(v, e):
    return _dot3(v, e), e


def _spread_bwd(e, g):
    return _dot3(g, e, NT), jnp.zeros_like(e)


_spread.defvjp(_spread_fwd, _spread_bwd)


@jax.custom_vjp
def _running_sum(a, lower):
    return _dot3(lower, a, NN, split=1)


def _running_sum_fwd(a, lower):
    return _dot3(lower, a, NN, split=1), lower


def _running_sum_bwd(lower, g):
    return _dot3(lower, g, TN, split=1), jnp.zeros_like(lower)


_running_sum.defvjp(_running_sum_fwd, _running_sum_bwd)


def _tile(n, cap):
    if n <= cap:
        return n
    best = None
    for t in range(LANE, cap + 1, LANE):
        if n % t == 0:
            best = t
    assert best is not None, (n, cap)
    return best


def _params(sem):
    return pltpu.CompilerParams(dimension_semantics=sem, vmem_limit_bytes=VMEM_LIMIT)


def _round_up(n, m):
    return -(-n // m) * m


class _Order:
    tokens = []
    last = None

    @classmethod
    def take(cls):
        out, cls.tokens = cls.tokens, []
        return out

    @classmethod
    def done(cls, result):
        cls.last = result[0] if isinstance(result, (list, tuple)) else result
        return result


ANY_SPEC = pl.BlockSpec(memory_space=pl.ANY)


def _mm(name, a, b, mode, out_dtype=F32, res=None, scale=1.0,
        cap_m=1408, cap_n=1408, cap_k=1408):
    segs = list(a) if isinstance(a, (list, tuple)) else [a]
    nseg = len(segs)
    if mode == "tn":
        k = segs[0].shape[0]
        widths = [s.shape[1] for s in segs]
        m = sum(widths)
        k2, n = b.shape
        tm = _tile(math.gcd(*widths), cap_m)
        tk = _tile(k, cap_k)
        counts = [wd // tm for wd in widths]
    else:
        m = segs[0].shape[0]
        widths = [s.shape[1] for s in segs]
        k = sum(widths)
        (k2, n) = b.shape if mode == "nn" else b.shape[::-1]
        tm = _tile(m, cap_m)
        tk = _tile(math.gcd(*widths), cap_k)
        counts = [wd // tk for wd in widths]
    assert k == k2, (name, [s.shape for s in segs], b.shape, mode)
    tn = _tile(n, cap_n)
    nk = k // tk
    starts = [sum(counts[:s]) for s in range(nseg)]
    dims = {"nn": NN, "nt": NT, "tn": TN}[mode]

    def a_spec(s):
        lo, cnt = starts[s], counts[s]
        if mode == "tn":
            if nseg == 1:
                return pl.BlockSpec((tk, tm), lambda i, j, kk: (kk, i))
            return pl.BlockSpec(
                (tk, tm), lambda i, j, kk: (jnp.where((i >= lo) & (i < lo + cnt), kk, 0),
                                            jnp.clip(i - lo, 0, cnt - 1)))
        if nseg == 1:
            return pl.BlockSpec((tm, tk), lambda i, j, kk: (i, kk))
        return pl.BlockSpec((tm, tk), lambda i, j, kk: (i, jnp.clip(kk - lo, 0, cnt - 1)))

    b_spec = (pl.BlockSpec((tn, tk), lambda i, j, kk: (j, kk)) if mode == "nt"
              else pl.BlockSpec((tk, tn), lambda i, j, kk: (kk, j)))
    o_spec = pl.BlockSpec((tm, tn), lambda i, j, kk: (i, j))
    has_res = res is not None
    use_acc = nk > 1 or nseg > 1
    ties = _Order.take()
    nt_ = len(ties)

    def body(*refs):
        a_refs = refs[:nseg]
        b_ref = refs[nseg]
        r_ref = refs[nseg + 1] if has_res else None
        o_ref = refs[nseg + 1 + has_res + nt_]
        scr = refs[nseg + 2 + has_res + nt_:]

        def finish(acc):
            if scale != 1.0:
                acc = acc * scale
            if has_res:
                acc = r_ref[...].astype(F32) + acc
            o_ref[...] = acc.astype(o_ref.dtype)

        if not use_acc:
            finish(_dot(a_refs[0][...], b_ref[...], dims))
            return
        acc_ref = scr[0]
        kk = pl.program_id(2)
        sel = pl.program_id(0) if mode == "tn" else kk

        @pl.when(kk == 0)
        def _():
            acc_ref[...] = jnp.zeros_like(acc_ref)

        for s in range(nseg):
            def add(s=s):
                acc_ref[...] += _dot(a_refs[s][...], b_ref[...], dims)
            if nseg == 1:
                add()
            else:
                pl.when((sel >= starts[s]) & (sel < starts[s] + counts[s]))(add)

        @pl.when(kk == nk - 1)
        def _():
            finish(acc_ref[...])

    in_specs = ([a_spec(s) for s in range(nseg)] + [b_spec] + ([o_spec] if has_res else [])
                + [ANY_SPEC] * nt_)
    args = tuple(segs) + (b,) + ((res,) if has_res else ()) + tuple(ties)
    return _Order.done(pl.pallas_call(
        body, name=name,
        out_shape=jax.ShapeDtypeStruct((m, n), out_dtype),
        grid=(m // tm, n // tn, nk),
        in_specs=in_specs, out_specs=o_spec,
        scratch_shapes=[pltpu.VMEM((tm, tn), F32)] if use_acc else [],
        compiler_params=_params(("parallel", "parallel", "arbitrary")),
    )(*args))


def _act(g, u):
    return _silu(g.astype(F32)) * u.astype(F32)


def _ffn_up(name, h, wgut, cap_m=512, cap_n=1408):
    m, k = h.shape
    dff = wgut.shape[0] // 2
    tm, tn = _tile(m, cap_m), _tile(dff, cap_n)
    nj = dff // tn
    ties = _Order.take()

    def body(h_ref, wg_ref, wu_ref, *rest):
        g_ref, u_ref, a_ref = rest[len(ties):]
        hv = h_ref[...]
        g = _dot(hv, wg_ref[...], NT)
        u = _dot(hv, wu_ref[...], NT)
        g_ref[...] = g.astype(g_ref.dtype)
        u_ref[...] = u.astype(u_ref.dtype)
        a_ref[...] = _act(g, u).astype(a_ref.dtype)

    o_spec = pl.BlockSpec((tm, tn), lambda i, j: (i, j))
    return _Order.done(pl.pallas_call(
        body, name=name, out_shape=[jax.ShapeDtypeStruct((m, dff), ACT_DTYPE)] * 3,
        grid=(m // tm, nj),
        in_specs=[pl.BlockSpec((tm, k), lambda i, j: (i, 0)),
                  pl.BlockSpec((tn, k), lambda i, j: (j, 0)),
                  pl.BlockSpec((tn, k), lambda i, j: (nj + j, 0))] + [ANY_SPEC] * len(ties),
        out_specs=[o_spec] * 3,
        compiler_params=_params(("parallel", "parallel")),
    )(h, wgut, wgut, *ties))


def _ffn_dact(name, dxo, wd, g, u, scale, cap_m=512, cap_n=1408):
    m, k = dxo.shape
    dff = wd.shape[0]
    tm, tn = _tile(m, cap_m), _tile(dff, cap_n)
    ties = _Order.take()

    def body(d_ref, w_ref, g_ref, u_ref, *rest):
        dg_ref, du_ref = rest[len(ties):]
        da = _dot(d_ref[...], w_ref[...], NT) * scale
        _, vjp = jax.vjp(_act, g_ref[...], u_ref[...])
        dg, du = vjp(da)
        dg_ref[...] = dg.astype(dg_ref.dtype)
        du_ref[...] = du.astype(du_ref.dtype)

    o_spec = pl.BlockSpec((tm, tn), lambda i, j: (i, j))
    return _Order.done(pl.pallas_call(
        body, name=name, out_shape=[jax.ShapeDtypeStruct((m, dff), ACT_DTYPE)] * 2,
        grid=(m // tm, dff // tn),
        in_specs=[pl.BlockSpec((tm, k), lambda i, j: (i, 0)),
                  pl.BlockSpec((tn, k), lambda i, j: (j, 0)), o_spec, o_spec]
        + [ANY_SPEC] * len(ties),
        out_specs=[o_spec] * 2,
        compiler_params=_params(("parallel", "parallel")),
    )(dxo, wd, g, u, *ties))


def _rw(name, fn, ins, outs, accs=(), tr=512, ncb=1):
    t = next(a.shape[0] for kind, a, _, _ in ins if kind == "row")
    assert t % tr == 0
    n_in = len(ins)
    n_pieces = sum(len(w) for w, _ in outs)

    def spec(kind, arr, width, base):
        if kind == "row":
            return pl.BlockSpec((tr, width), lambda j, i: (i, base + j))
        return pl.BlockSpec((arr.shape[0], width), lambda j, i: (0, base + j))

    in_specs = [spec(*s) for s in ins]
    out_shapes, out_specs = [], []
    for widths, dt in outs:
        w = sum(widths)
        out_shapes.append(jax.ShapeDtypeStruct((t, w * ncb), dt))
        out_specs.append(pl.BlockSpec((tr, w), lambda j, i: (i, j)))
    for rows, width in accs:
        out_shapes.append(jax.ShapeDtypeStruct((rows, width * ncb), F32))
        out_specs.append(pl.BlockSpec((rows, width), lambda j, i: (0, j)))

    ties = _Order.take()
    nt_ = len(ties)
    in_specs = in_specs + [ANY_SPEC] * nt_

    def body(*refs):
        vals = [r[...] for r in refs[:n_in]]
        res = fn(*vals)
        o_refs = refs[n_in + nt_:n_in + nt_ + len(outs)]
        a_refs = refs[n_in + nt_ + len(outs):]
        p = 0
        for (widths, _), o_ref in zip(outs, o_refs):
            off = 0
            for w in widths:
                if len(widths) == 1:
                    o_ref[...] = res[p].astype(o_ref.dtype)
                else:
                    o_ref[:, off:off + w] = res[p].astype(o_ref.dtype)
                off += w
                p += 1
        i = pl.program_id(1)
        for a_ref, v in zip(a_refs, res[n_pieces:]):
            @pl.when(i == 0)
            def _(a_ref=a_ref, v=v):
                a_ref[...] = v

            @pl.when(i > 0)
            def _(a_ref=a_ref, v=v):
                a_ref[...] += v

    return _Order.done(pl.pallas_call(
        body, name=name, out_shape=out_shapes,
        grid=(ncb, t // tr), in_specs=in_specs, out_specs=out_specs,
        compiler_params=_params(("parallel", "arbitrary")),
    )(*[a for _, a, _, _ in ins], *ties))


def _rms(x, g):
    x = x.astype(F32)
    return x * lax.rsqrt(jnp.mean(x * x, axis=-1, keepdims=True) + EPS) * g


def _silu(x):
    return x * jax.nn.sigmoid(x)


def _colsum(v):
    return jnp.sum(v, axis=0, keepdims=True)


def _pair_norm(x, g):
    w = 2 * HD
    ri = lax.broadcasted_iota(jnp.int32, (w, w), 0)
    ci = lax.broadcasted_iota(jnp.int32, (w, w), 1)
    same_head = ((ri < HD) == (ci < HD)).astype(F32)
    ms = _spread(x * x, same_head) * (1.0 / HD)
    return x * lax.rsqrt(ms + EPS) * g


ATTN_SCALE = 1.0 / math.sqrt(HD)


def _attn_bias(coef):
    key = lax.broadcasted_iota(jnp.int32, (QB, QB), 0)
    qry = lax.broadcasted_iota(jnp.int32, (QB, QB), 1)
    dist = (qry - key).astype(F32)
    own = jnp.where(qry >= key, -coef * dist, NEG)
    prev = jnp.where(qry <= key, -coef * (dist + float(QB)), NEG)
    return own, prev


def _attn_pair(qn, kcn, kpn, vc, vp, b_own, b_prev):
    nb = qn.shape[0]
    w = 2 * HD
    lane = lax.broadcasted_iota(jnp.int32, (1, 1, w), 2)
    eye = (lax.broadcasted_iota(jnp.int32, (QB, QB), 0)
           == lax.broadcasted_iota(jnp.int32, (QB, QB), 1)).astype(F32)
    out = jnp.zeros((nb, QB, w), F32)
    lb = jnp.zeros((nb * QB, w), F32)
    for hh in range(2):
        mask = ((lane < HD) if hh == 0 else (lane >= HD)).astype(F32)
        qm = qn * mask
        lc = _dot(kcn, qm, BNT) + b_own[hh]
        lp = _dot(kpn, qm, BNT) + b_prev[hh]
        m = lax.stop_gradient(jnp.maximum(jnp.max(lc, axis=1, keepdims=True),
                                          jnp.max(lp, axis=1, keepdims=True)))
        pc = jnp.exp(lc - m)
        pp = jnp.exp(lp - m)
        l = jnp.sum(pc, axis=1, keepdims=True) + jnp.sum(pp, axis=1, keepdims=True)
        inv = 1.0 / l
        out = out + (_dot(pc * inv, vc, BTN) + _dot(pp * inv, vp, BTN)) * mask
        diag = (eye * (m + jnp.log(l))).reshape(nb * QB, QB)
        lb = lb + _spread(diag, jnp.broadcast_to(mask[0], (QB, w)))
    return out, lb.reshape(nb, QB, w)


NORM_ROWS = 128
NORM_UNROLL = 4
EPILOGUE_ROWS = 512
ATTN_BATCH_FWD = 16
ATTN_BATCH_BWD = 8


def _unit_rows(u, d):
    r = u & (d - 1)
    n = u >> (d.bit_length() - 1)

    def rows(blk):
        start = pl.multiple_of(blk * (QB * d), QB * d)
        return pl.ds(start, QB) if d == 1 else pl.ds(start + r, QB, stride=d)

    return rows(n), rows(jnp.maximum(n - 1, 0)), n == 0


def _unit_batch(i, nbatch, d, bias, qf, kf, vf):
    units = [_unit_rows(i * nbatch + j, d) for j in range(nbatch)]
    cur = lambda ref: jnp.stack([ref[c, :] for c, _, _ in units])
    prv = lambda ref: jnp.stack([ref[p, :] for _, p, _ in units])
    b_own = [b[0] for b in bias]
    b_prev = [jnp.stack([jnp.where(first, NEG, b[1]) for _, _, first in units]) for b in bias]
    return units, (cur(qf), cur(kf), prv(kf), cur(vf), prv(vf), b_own, b_prev)


def _q_norm(x, g):
    return _pair_norm(x, g * ATTN_SCALE)


def _attn_prologue(t, q_ref, k_ref, v_ref, qg_ref, kg_ref, qf, kf, vf):
    def chunk(c, carry):
        rows = pl.ds(pl.multiple_of(c * NORM_ROWS, NORM_ROWS), NORM_ROWS)
        qf[rows, :] = _q_norm(q_ref[rows, :].astype(F32), qg_ref[...])
        kf[rows, :] = _pair_norm(k_ref[rows, :].astype(F32), kg_ref[...])
        vf[rows, :] = v_ref[rows, :].astype(F32)
        return carry
    lax.fori_loop(0, t // NORM_ROWS, chunk, 0, unroll=NORM_UNROLL)


def _attn_specs(t, bases):
    w = 2 * HD
    ins = [pl.BlockSpec((t, w), functools.partial(lambda p, c, b: (0, b + p), b=b)) for b in bases]
    gain = pl.BlockSpec((1, w), lambda p, c: (0, 0))
    blk = pl.BlockSpec((t, w), lambda p, c: (0, p))
    return ins, gain, blk


def _attn_fwd(name, proj, bases, qg, kg, coefs, d):
    t = proj.shape[0]
    npairs = coefs.shape[0] // 2
    w = 2 * HD
    ins, gain, blk = _attn_specs(t, bases)

    def body(coef_ref, q_ref, k_ref, v_ref, qg_ref, kg_ref, o_ref, l_ref, qf, kf, vf):
        p = pl.program_id(0)
        bias = (_attn_bias(coef_ref[2 * p]), _attn_bias(coef_ref[2 * p + 1]))
        _attn_prologue(t, q_ref, k_ref, v_ref, qg_ref, kg_ref, qf, kf, vf)

        def step(i, carry):
            units, ins = _unit_batch(i, ATTN_BATCH_FWD, d, bias, qf, kf, vf)
            o, lb = _attn_pair(*ins)
            for j, (cur, _, _) in enumerate(units):
                o_ref[cur, :] = o[j]
                l_ref[cur, :] = lb[j]
            return carry

        lax.fori_loop(0, t // QB // ATTN_BATCH_FWD, step, 0)

    return pl.pallas_call(
        body, name=name,
        out_shape=[jax.ShapeDtypeStruct((t, npairs * w), F32)] * 2,
        grid_spec=pltpu.PrefetchScalarGridSpec(
            num_scalar_prefetch=1, grid=(npairs,),
            in_specs=ins + [gain, gain], out_specs=[blk, blk],
            scratch_shapes=[pltpu.VMEM((t, w), F32)] * 3),
        compiler_params=_params(("arbitrary",)),
    )(coefs, proj, proj, proj, qg, kg)


def _attn_bwd(name, proj, bases, qg, kg, coefs, d, do, dl):
    t = proj.shape[0]
    npairs = coefs.shape[0] // 2
    w = 2 * HD
    ins, gain, blk = _attn_specs(t, bases)

    def body(coef_ref, q_ref, k_ref, v_ref, qg_ref, kg_ref, do_ref, dl_ref,
             dq_ref, dk_ref, dv_ref, dqg_ref, dkg_ref, qf, kf, vf, dqf, dkf, dvf):
        p = pl.program_id(0)
        bias = (_attn_bias(coef_ref[2 * p]), _attn_bias(coef_ref[2 * p + 1]))
        _attn_prologue(t, q_ref, k_ref, v_ref, qg_ref, kg_ref, qf, kf, vf)
        dkf[...] = jnp.zeros_like(dkf)
        dvf[...] = jnp.zeros_like(dvf)

        def step(i, carry):
            units, ins = _unit_batch(i, ATTN_BATCH_BWD, d, bias, qf, kf, vf)
            f = lambda a, b, c, e, g: _attn_pair(a, b, c, e, g, *ins[5:])
            _, vjp = jax.vjp(f, *ins[:5])
            cot = (jnp.stack([do_ref[cur, :] for cur, _, _ in units]),
                   jnp.stack([dl_ref[cur, :] for cur, _, _ in units]))
            dq, dkc, dkp, dvc, dvp = vjp(cot)
            for j, (cur, prv, _) in enumerate(units):
                dqf[cur, :] = dq[j]
                dkf[cur, :] += dkc[j]
                dkf[prv, :] += dkp[j]
                dvf[cur, :] += dvc[j]
                dvf[prv, :] += dvp[j]
            return carry

        lax.fori_loop(0, t // QB // ATTN_BATCH_BWD, step, 0)

        def chunk(c, carry):
            dqg_acc, dkg_acc = carry
            rows = pl.ds(pl.multiple_of(c * EPILOGUE_ROWS, EPILOGUE_ROWS), EPILOGUE_ROWS)
            _, vq = jax.vjp(_q_norm, q_ref[rows, :].astype(F32), qg_ref[...])
            dq, dqg = vq(dqf[rows, :])
            _, vk = jax.vjp(_pair_norm, k_ref[rows, :].astype(F32), kg_ref[...])
            dk, dkg = vk(dkf[rows, :])
            dq_ref[rows, :] = dq.astype(dq_ref.dtype)
            dk_ref[rows, :] = dk.astype(dk_ref.dtype)
            dv_ref[rows, :] = dvf[rows, :].astype(dv_ref.dtype)
            return dqg_acc + dqg, dkg_acc + dkg

        zero = jnp.zeros((1, w), F32)
        dqg, dkg = lax.fori_loop(0, t // EPILOGUE_ROWS, chunk, (zero, zero))

        @pl.when(p == 0)
        def _():
            dqg_ref[...] = dqg
            dkg_ref[...] = dkg

        @pl.when(p > 0)
        def _():
            dqg_ref[...] += dqg
            dkg_ref[...] += dkg

    big = jax.ShapeDtypeStruct((t, npairs * w), ACT_DTYPE)
    small = jax.ShapeDtypeStruct((1, w), F32)
    return pl.pallas_call(
        body, name=name,
        out_shape=[big, big, big, small, small],
        grid_spec=pltpu.PrefetchScalarGridSpec(
            num_scalar_prefetch=1, grid=(npairs,),
            in_specs=ins + [gain, gain, blk, blk],
            out_specs=[blk, blk, blk, gain, gain],
            scratch_shapes=[pltpu.VMEM((t, w), F32)] * 6),
        compiler_params=_params(("arbitrary",)),
    )(coefs, proj, proj, proj, qg, kg, do, dl)


CONV_ROWS = 256
CONV_HALO = 8


def _rows_back(x, s):
    return x if s == 0 else pltpu.roll(x, s, 0)


def _rows_ahead(x, s):
    return x if s == 0 else pltpu.roll(x, x.shape[0] - s, 0)


def _conv_pre(u, w, b):
    y = b
    for kk in range(SSD_K):
        y = y + w[kk:kk + 1, :] * _rows_back(u, SSD_K - 1 - kk)
    return y


def _stage_padded(dst, src_ref, t):
    zeros = jnp.zeros((CONV_HALO, dst.shape[1]), F32)
    dst[0:CONV_HALO, :] = zeros
    dst[t + CONV_HALO:t + 2 * CONV_HALO, :] = zeros
    dst[CONV_HALO:t + CONV_HALO, :] = src_ref[...].astype(F32)


def _chunk_rows(c):
    r0 = pl.multiple_of(c * CONV_ROWS, CONV_ROWS)
    return pl.ds(r0, CONV_ROWS + 2 * CONV_HALO), pl.ds(r0, CONV_ROWS)


def _conv_fwd(name, src, base, w, b, cw=128):
    t = src.shape[0]
    c = w.shape[1]
    centre = slice(CONV_HALO, CONV_HALO + CONV_ROWS)

    def body(u_ref, w_ref, b_ref, o_ref, up):
        _stage_padded(up, u_ref, t)
        wv, bv = w_ref[...], b_ref[...]

        def chunk(ci, carry):
            ext, rows = _chunk_rows(ci)
            y = _conv_pre(up[ext, :], wv, bv)
            o_ref[rows, :] = _silu(y)[centre].astype(o_ref.dtype)
            return carry

        lax.fori_loop(0, t // CONV_ROWS, chunk, 0)

    return pl.pallas_call(
        body, name=name, out_shape=jax.ShapeDtypeStruct((t, c), ACT_DTYPE),
        grid=(c // cw,),
        in_specs=[pl.BlockSpec((t, cw), lambda j: (0, base + j)),
                  pl.BlockSpec((SSD_K, cw), lambda j: (0, j)),
                  pl.BlockSpec((1, cw), lambda j: (0, j))],
        out_specs=pl.BlockSpec((t, cw), lambda j: (0, j)),
        scratch_shapes=[pltpu.VMEM((t + 2 * CONV_HALO, cw), F32)],
        compiler_params=_params(("parallel",)),
    )(src, w, b)


def _conv_bwd(name, src, base, w, b, dout, cw=128):
    t = src.shape[0]
    c = w.shape[1]

    centre = slice(CONV_HALO, CONV_HALO + CONV_ROWS)

    def body(u_ref, w_ref, b_ref, d_ref, du_ref, dw_ref, db_ref, up, dp):
        _stage_padded(up, u_ref, t)
        _stage_padded(dp, d_ref, t)
        wv, bv = w_ref[...], b_ref[...]

        def chunk(ci, carry):
            dws, db = carry
            ext, rows = _chunk_rows(ci)
            u = up[ext, :]
            y = _conv_pre(u, wv, bv)
            sg = jax.nn.sigmoid(y)
            dy = dp[ext, :] * (sg * (1.0 + y * (1.0 - sg)))
            du = jnp.zeros_like(u)
            new_dws = []
            for kk in range(SSD_K):
                s = SSD_K - 1 - kk
                du = du + wv[kk:kk + 1, :] * _rows_ahead(dy, s)
                new_dws.append(dws[kk] + _colsum((dy * _rows_back(u, s))[centre]))
            du_ref[rows, :] = du[centre].astype(du_ref.dtype)
            return tuple(new_dws), db + _colsum(dy[centre])

        zero = jnp.zeros((1, cw), F32)
        dws, db = lax.fori_loop(0, t // CONV_ROWS, chunk, ((zero,) * SSD_K, zero))
        for kk in range(SSD_K):
            dw_ref[kk:kk + 1, :] = dws[kk]
        db_ref[...] = db

    return pl.pallas_call(
        body, name=name,
        out_shape=[jax.ShapeDtypeStruct((t, c), ACT_DTYPE),
                   jax.ShapeDtypeStruct((SSD_K, c), F32),
                   jax.ShapeDtypeStruct((1, c), F32)],
        grid=(c // cw,),
        in_specs=[pl.BlockSpec((t, cw), lambda j: (0, base + j)),
                  pl.BlockSpec((SSD_K, cw), lambda j: (0, j)),
                  pl.BlockSpec((1, cw), lambda j: (0, j)),
                  pl.BlockSpec((t, cw), lambda j: (0, j))],
        out_specs=[pl.BlockSpec((t, cw), lambda j: (0, j)),
                   pl.BlockSpec((SSD_K, cw), lambda j: (0, j)),
                   pl.BlockSpec((1, cw), lambda j: (0, j))],
        scratch_shapes=[pltpu.VMEM((t + 2 * CONV_HALO, cw), F32)] * 2,
        compiler_params=_params(("parallel",)),
    )(src, w, b, dout)


def _softplus(x):
    return jnp.maximum(x, 0.0) + jnp.log(1.0 + jnp.exp(-jnp.abs(x)))


def _ssd_chunk(xbc, dtraw, bias, alog, states):
    wd = states[0].shape[1]
    nj = wd // SSD_P
    inner = SSD_G * wd
    dt = _softplus(dtraw + bias)
    a = dt * (-jnp.exp(alog))
    li = lax.broadcasted_iota(jnp.int32, (SSD_Q, SSD_Q), 0)
    si = lax.broadcasted_iota(jnp.int32, (SSD_Q, SSD_Q), 1)
    causal = li >= si
    acs = _running_sum(a, causal.astype(F32))
    acs_t = acs.T
    a_last = acs[SSD_Q - 1:SSD_Q, :]
    grow = jnp.exp(acs)
    shrink = jnp.exp(a_last - acs)
    hrow = lax.broadcasted_iota(jnp.int32, (LANE, wd), 0)
    wcol = lax.broadcasted_iota(jnp.int32, (LANE, wd), 1)
    lane = lax.broadcasted_iota(jnp.int32, (1, LANE), 1)
    ys, snext = [], []
    for g in range(SSD_G):
        lo = (hrow - g * nj) * SSD_P
        head_lanes = jnp.logical_and(wcol >= lo, wcol < lo + SSD_P).astype(F32)
        xs = xbc[:, g * wd:(g + 1) * wd]
        bm = xbc[:, inner + g * SSD_N:inner + (g + 1) * SSD_N]
        cm = xbc[:, inner + (SSD_G + g) * SSD_N:inner + (SSD_G + g + 1) * SSD_N]
        xdt = xs * _dot(dt, head_lanes)
        grow_x = _spread(grow, head_lanes)
        y_off = _dot(cm, states[g]) * grow_x
        s_new = (states[g] * grow_x[SSD_Q - 1:SSD_Q, :]
                 + _dot(bm, xdt * _dot(shrink, head_lanes), TN))
        cb = _dot(cm, bm, NT)
        pieces = []
        for i in range(wd // LANE):
            xp = xdt[:, i * LANE:(i + 1) * LANE]
            acc = jnp.zeros((SSD_Q, LANE), F32)
            for hh in range(LANE // SSD_P):
                h = g * nj + i * (LANE // SSD_P) + hh
                decay = jnp.exp(jnp.where(causal, acs[:, h:h + 1] - acs_t[h:h + 1, :], NEG))
                keep = jnp.logical_and(lane >= hh * SSD_P, lane < (hh + 1) * SSD_P).astype(F32)
                acc = acc + _dot(cb * decay, xp * keep)
            pieces.append(acc)
        y_diag = pieces[0] if len(pieces) == 1 else jnp.concatenate(pieces, axis=1)
        ys.append(y_diag + y_off)
        snext.append(s_new)
    return ys, snext


def _ssd_specs(cdim, wd, rev, nc):
    ch = (lambda c: nc - 1 - c) if rev else (lambda c: c)
    full = lambda width: pl.BlockSpec((SSD_Q, width), lambda c: (ch(c), 0))
    vec = pl.BlockSpec((1, LANE), lambda c: (0, 0))
    st = pl.BlockSpec((1, SSD_G, SSD_N, wd), lambda c: (ch(c), 0, 0, 0))
    return full, vec, st


def _ssd_fwd(name, xbc, dtraw, bias, alog, inner):
    t, cdim = xbc.shape
    wd = inner // SSD_G
    nc = t // SSD_Q
    full, vec, st = _ssd_specs(cdim, wd, False, nc)

    def body(x_ref, r_ref, b_ref, a_ref, y_ref, st_ref, s_scr):
        @pl.when(pl.program_id(0) == 0)
        def _():
            s_scr[...] = jnp.zeros_like(s_scr)

        sprev = [s_scr[g] for g in range(SSD_G)]
        ys, snext = _ssd_chunk(x_ref[...].astype(F32), r_ref[...], b_ref[...], a_ref[...], sprev)
        for g in range(SSD_G):
            st_ref[0, g] = sprev[g]
            y_ref[:, g * wd:(g + 1) * wd] = ys[g]
            s_scr[g] = snext[g]

    return pl.pallas_call(
        body, name=name,
        out_shape=[jax.ShapeDtypeStruct((t, inner), F32),
                   jax.ShapeDtypeStruct((nc, SSD_G, SSD_N, wd), F32)],
        grid=(nc,),
        in_specs=[full(cdim), full(LANE), vec, vec],
        out_specs=[full(inner), st],
        scratch_shapes=[pltpu.VMEM((SSD_G, SSD_N, wd), F32)],
        compiler_params=_params(("arbitrary",)),
    )(xbc, dtraw, bias, alog)


def _ssd_bwd(name, xbc, dtraw, bias, alog, states, dy, dxs_extra):
    t, cdim = xbc.shape
    inner = dy.shape[1]
    wd = inner // SSD_G
    nc = t // SSD_Q
    full, vec, st = _ssd_specs(cdim, wd, True, nc)

    def body(x_ref, r_ref, b_ref, a_ref, st_ref, dy_ref, dx0_ref,
             dx_ref, dr_ref, db_ref, da_ref, ds_scr):
        first = pl.program_id(0) == 0

        @pl.when(first)
        def _():
            ds_scr[...] = jnp.zeros_like(ds_scr)

        sprev = [st_ref[0, g] for g in range(SSD_G)]
        _, vjp = jax.vjp(_ssd_chunk, x_ref[...].astype(F32), r_ref[...], b_ref[...], a_ref[...],
                         sprev)
        dyv = dy_ref[...]
        dys = [dyv[:, g * wd:(g + 1) * wd] for g in range(SSD_G)]
        dsn = [ds_scr[g] for g in range(SSD_G)]
        dx, dr, db, da, dsp = vjp((dys, dsn))
        dx_ref[:, :inner] = dx[:, :inner] + dx0_ref[...].astype(F32)
        dx_ref[:, inner:] = dx[:, inner:]
        dr_ref[...] = dr
        for g in range(SSD_G):
            ds_scr[g] = dsp[g]

        @pl.when(first)
        def _():
            db_ref[...] = db
            da_ref[...] = da

        @pl.when(jnp.logical_not(first))
        def _():
            db_ref[...] += db
            da_ref[...] += da

    return pl.pallas_call(
        body, name=name,
        out_shape=[jax.ShapeDtypeStruct((t, cdim), F32),
                   jax.ShapeDtypeStruct((t, LANE), F32),
                   jax.ShapeDtypeStruct((1, LANE), F32),
                   jax.ShapeDtypeStruct((1, LANE), F32)],
        grid=(nc,),
        in_specs=[full(cdim), full(LANE), vec, vec, st, full(inner), full(inner)],
        out_specs=[full(cdim), full(LANE), vec, vec],
        scratch_shapes=[pltpu.VMEM((SSD_G, SSD_N, wd), F32)],
        compiler_params=_params(("arbitrary",)),
    )(xbc, dtraw, bias, alog, states, dy, dxs_extra)


def _mix(o0, o1, o2, l0, l1, l2):
    m = lax.stop_gradient(jnp.maximum(jnp.maximum(l0, l1), l2))
    e0, e1, e2 = jnp.exp(l0 - m), jnp.exp(l1 - m), jnp.exp(l2 - m)
    return (e0 * o0 + e1 * o1 + e2 * o2) / (e0 + e1 + e2)


def _gate(y, xs, z, dexp, gain):
    v = (y + xs.astype(F32) * dexp) * _silu(z.astype(F32))
    return _rms(v, gain)


def _merge(ga, gs, ap, sp):
    return jax.nn.sigmoid(ga.astype(F32)) * ap + jax.nn.sigmoid(gs.astype(F32)) * sp


def _alibi_coefs(hp):
    n = hp * len(PATTERNS)
    slopes = np.exp2(-ALIBI_MAX_EXP * np.arange(1, n + 1, dtype=np.float32) / n).astype(np.float32)
    return [jnp.asarray(slopes[g * hp:(g + 1) * hp] * np.float32(d))
            for g, (_, d) in enumerate(PATTERNS)]


def _local_step(x, tgt, w, p, gw_=None, aw=None):
    t, d = x.shape
    dff = w["gu1t"].shape[0] // 2
    aw = w["abt"].shape[1] if aw is None else aw
    hp = aw // HD
    qkv = len(PATTERNS) * aw
    inner = p["ssd_norm"].shape[1]
    nh = p["dt_bias"].shape[1]
    gw_ = {} if gw_ is None else gw_
    gw = inner // SSD_G
    cdim = inner + 2 * SSD_G * SSD_N
    z_off, xbc_off = 3 * qkv, 3 * qkv + inner
    ga_off = xbc_off + cdim
    gs_off = ga_off + d
    hw = d // 2
    assert z_off % gw == 0 and xbc_off % LANE == 0 and ga_off % hw == 0 and gs_off % hw == 0
    assert (nh // SSD_G) * SSD_P == gw and hp % 2 == 0 and aw % LANE == 0 and nh <= LANE
    gdt = MXU_DTYPE

    row = lambda a, width, base=0: ("row", a, width, base)
    const = lambda a, width, base=0: ("const", a, width, base)

    def rms_fwd(name, xin, g):
        return _rw(name, lambda xv, gv: (_rms(xv, gv),), [row(xin, d), const(g, d)],
                   [((d,), ACT_DTYPE)])[0]

    def rms_bwd(name, xin, g, dh, dres):
        def fn(xv, gv, dhv, drv):
            _, vjp = jax.vjp(_rms, xv, gv)
            dx, dg = vjp(dhv.astype(F32))
            return drv + dx, dg
        return _rw(name, fn, [row(xin, d), const(g, d), row(dh, d), row(dres, d)],
                   [((d,), F32)], accs=[(1, d)])

    def ffn_fwd(tag, xin, g, key_gu, key_d):
        h = rms_fwd(tag + "_norm", xin, g)
        gate, up, a = _ffn_up(tag + "_up", h, w[key_gu])
        xo = _mm(tag + "_down", a, w[key_d], "nn", F32, res=xin, scale=0.5)
        return xo, (h, gate, up, a)

    def ffn_bwd(tag, xin, g, wgut, wd, saved, dxo, key_gu, key_d):
        h, gate, up, a = saved
        gw_[key_d] = _mm(tag + "_dwd", a, dxo, "tn", gdt, scale=0.5)
        dgu = _ffn_dact(tag + "_dact", dxo, wd, gate, up, 0.5)
        gw_[key_gu] = _mm(tag + "_dwgu", dgu, h, "tn", gdt)
        dh = _mm(tag + "_dh", dgu, wgut, "nn", F32)
        return rms_bwd(tag + "_dnorm", xin, g, dh, dxo)

    x1, ffn1_saved = ffn_fwd("ffn1", x, p["ffn1_norm"], "gu1t", "d1")
    if hasattr(w, "after_first_ffn"):
        w.after_first_ffn()
    h2 = rms_fwd("mix_norm", x1, p["mix_norm"])
    proj = _mm("in_proj", h2, w["maint"], "nt", ACT_DTYPE, cap_m=512, cap_n=2944)
    dtraw = _mm("dt_proj", h2, w["dtt"], "nt", F32)

    coefs = _alibi_coefs(hp)
    qg2 = jnp.concatenate([p["q_norm"], p["q_norm"]], axis=1)
    kg2 = jnp.concatenate([p["k_norm"], p["k_norm"]], axis=1)
    pw = 2 * HD
    attn_bases = [[(off + gi * aw) // pw for off in (0, qkv, 2 * qkv)]
                  for gi in range(len(PATTERNS))]
    attn_o, attn_l = [], []
    for gi, (_, dil) in enumerate(PATTERNS):
        o, l = _attn_fwd(f"attn_fwd{gi}", proj, attn_bases[gi], qg2, kg2, coefs[gi], dil)
        attn_o.append(o)
        attn_l.append(l)
    ao = _rw("attn_mix", lambda *v: (_mix(*v),), [row(a, aw) for a in attn_o + attn_l],
             [((aw,), ACT_DTYPE)])[0]

    xbc = _conv_fwd("conv_fwd", proj, xbc_off // LANE, p["conv_w"], p["conv_b"])
    pad = lambda v: jnp.pad(v, ((0, 0), (0, LANE - nh)))
    bias_p, alog_p = pad(p["dt_bias"]), pad(p["a_log"])
    yssd, states = _ssd_fwd("ssd_fwd", xbc, dtraw, bias_p, alog_p, inner)
    dexp = jnp.repeat(p["d_skip"], SSD_P, axis=1)
    gate_ins = [row(yssd, gw), row(xbc, gw), row(proj, gw, z_off // gw),
                const(dexp, gw), const(p["ssd_norm"], gw)]
    yn = _rw("ssd_gate", lambda *v: (_gate(*v),), gate_ins, [((gw,), ACT_DTYPE)], ncb=SSD_G)[0]

    ap = _mm("attn_out", ao, w["abt"], "nt", F32)
    sp = _mm("ssd_out", yn, w["sb"], "nn", F32)
    merge_ins = [row(proj, hw, ga_off // hw), row(proj, hw, gs_off // hw), row(ap, hw), row(sp, hw)]
    mg = _rw("merge", lambda *v: (_merge(*v),), merge_ins, [((hw,), ACT_DTYPE)], ncb=2)[0]
    x2 = _mm("mix_out", mg, w["out"], "nn", F32, res=x1)
    x3, ffn2_saved = ffn_fwd("ffn2", x2, p["ffn2_norm"], "gu2t", "d2")

    def loss_fn(yv, tv):
        e = yv - tv
        return e * (1.0 / d), _colsum(e * e)
    dy, loss_vec = _rw("loss", loss_fn, [row(x3, d), row(tgt, d)], [((d,), F32)], accs=[(1, d)])

    gp = {}
    dx2, gp["ffn2_norm"] = ffn_bwd(
        "ffn2", x2, p["ffn2_norm"], w["gu2t"], w["d2"], ffn2_saved, dy, "gu2t", "d2")
    dmg = _mm("d_merge", dx2, w["out"], "nt", ACT_DTYPE)
    gw_["out"] = _mm("dw_out", mg, dx2, "tn", gdt)

    def merge_bwd(gav, gsv, apv, spv, dv):
        _, vjp = jax.vjp(_merge, gav, gsv, apv, spv)
        return vjp(dv.astype(F32))
    dga, dgs, dap, dsp = _rw("d_merge_gate", merge_bwd, merge_ins + [row(dmg, hw)],
                             [((hw,), ACT_DTYPE)] * 4, ncb=2)
    gw_["abt"] = _mm("dw_ab", dap, ao, "tn", gdt)
    dao = _mm("d_attn_o", dap, w["abt"], "nn", F32)
    gw_["sb"] = _mm("dw_sb", yn, dsp, "tn", gdt)
    dyn = _mm("d_ssd_y", dsp, w["sb"], "nt", F32)

    def gate_bwd(yv, xv, zv, dev, gv, dv):
        _, vjp = jax.vjp(_gate, yv, xv, zv, dev, gv)
        return vjp(dv)
    dyssd, dxs_gate, dz, ddexp, gp["ssd_norm"] = _rw(
        "d_ssd_gate", gate_bwd, gate_ins + [row(dyn, gw)],
        [((gw,), F32), ((gw,), F32), ((gw,), ACT_DTYPE)], accs=[(1, gw), (1, gw)], ncb=SSD_G)
    gp["d_skip"] = ddexp.reshape(nh, SSD_P).sum(axis=1).reshape(1, nh)

    dxbc, ddtraw, dbias, dalog = _ssd_bwd("ssd_bwd", xbc, dtraw, bias_p, alog_p, states,
                                          dyssd, dxs_gate)
    gp["dt_bias"], gp["a_log"] = dbias[:, :nh], dalog[:, :nh]
    du, gp["conv_w"], gp["conv_b"] = _conv_bwd("conv_bwd", proj, xbc_off // LANE,
                                               p["conv_w"], p["conv_b"], dxbc)

    def mix_bwd(*v):
        _, vjp = jax.vjp(_mix, *v[:6])
        return vjp(v[6])
    dmix = _rw("d_attn_mix", mix_bwd, [row(a, aw) for a in attn_o + attn_l] + [row(dao, aw)],
               [((aw,), F32)] * 6)
    dq, dk, dv = [], [], []
    dqg = dkg = None
    for gi, (_, dil) in enumerate(PATTERNS):
        r = _attn_bwd(f"attn_bwd{gi}", proj, attn_bases[gi], qg2, kg2, coefs[gi], dil,
                      dmix[gi], dmix[3 + gi])
        dq.append(r[0])
        dk.append(r[1])
        dv.append(r[2])
        dqg = r[3] if dqg is None else dqg + r[3]
        dkg = r[4] if dkg is None else dkg + r[4]
    gp["q_norm"] = dqg[:, :HD] + dqg[:, HD:]
    gp["k_norm"] = dkg[:, :HD] + dkg[:, HD:]

    segs = dq + dk + dv + [dz, du, dga, dgs]
    gw_["maint"] = _mm("dw_in", segs, h2, "tn", gdt)
    gw_["dtt"] = _mm("dw_dt", ddtraw, h2, "tn", gdt)
    dh2 = _mm("d_h2_main", segs, w["maint"], "nn", F32)
    dh2 = _mm("d_h2_dt", ddtraw, w["dtt"], "nn", F32, res=dh2)
    dx1, gp["mix_norm"] = rms_bwd("d_mix_norm", x1, p["mix_norm"], dh2, dx2)
    dx0, gp["ffn1_norm"] = ffn_bwd(
        "ffn1", x, p["ffn1_norm"], w["gu1t"], w["d1"], ffn1_saved, dx1, "gu1t", "d1")
    return loss_vec, dx0, gw_, gp


MESH = pl.DeviceIdType.MESH
HBM_SPEC = pl.BlockSpec(memory_space=pltpu.HBM)


def _mesh_pos():
    return lax.axis_index("x"), lax.axis_index("y"), lax.axis_index("c")


def _flip(pos, k):
    x, y, c = pos
    return (1 - x if k & 4 else x, 1 - y if k & 2 else y, 1 - c if k & 1 else c)


def _dev_index(pos):
    return 4 * pos[0] + 2 * pos[1] + pos[2]


def _rows_of(ref, base, stride, rows, pos):
    start = pl.multiple_of(base + stride * _dev_index(pos), ROW_ALIGN)
    return ref.at[pl.ds(start, rows)]


def _gather(name, shards, dests, out_shapes):
    n = len(shards)
    n_out = len(out_shapes)

    def body(*refs):
        x_refs = refs[:n]
        o_refs = refs[n:n + n_out]
        send_sems, recv_sems, local_sems = refs[n + n_out:]
        me = _mesh_pos()
        sibling = _flip(me, 1)
        chips = [_flip(me, 4), _flip(me, 2), _flip(me, 6)]

        def slot(i, block):
            k_out, base, stride = dests[i]
            return _rows_of(o_refs[k_out], base, stride, shards[i].shape[0], block)

        def copy(i, k, block, to, src=None):
            dst = slot(i, block)
            return pltpu.make_async_remote_copy(
                src_ref=dst if src is None else src, dst_ref=dst,
                send_sem=send_sems.at[7 * i + k], recv_sem=recv_sems.at[7 * i + k],
                device_id=to, device_id_type=MESH)

        mine = [pltpu.make_async_copy(x_refs[i], slot(i, me), local_sems.at[i]) for i in range(n)]
        for cp in mine:
            cp.start()
        first = []
        for i in range(n):
            first.append(copy(i, 0, me, sibling, src=x_refs[i]))
            first += [copy(i, 1 + j, me, chip, src=x_refs[i]) for j, chip in enumerate(chips)]
        for cp in first:
            cp.start()
        passed = []
        for j, chip in enumerate(chips):
            for i in range(n):
                copy(i, 1 + j, chip, me).wait_recv()
                fwd = copy(i, 4 + j, chip, sibling)
                fwd.start()
                passed.append(fwd)
        for i in range(n):
            copy(i, 0, sibling, me).wait_recv()
            for j, chip in enumerate(chips):
                copy(i, 4 + j, _flip(chip, 1), me).wait_recv()
        for cp in first + passed:
            cp.wait_send()
        for cp in mine:
            cp.wait()

    return pl.pallas_call(
        body, name=name,
        out_shape=[jax.ShapeDtypeStruct(s, dt) for s, dt in out_shapes],
        in_specs=[HBM_SPEC] * n, out_specs=[HBM_SPEC] * n_out,
        scratch_shapes=[pltpu.SemaphoreType.DMA((7 * n,)), pltpu.SemaphoreType.DMA((7 * n,)),
                        pltpu.SemaphoreType.DMA((n,))],
    )(*shards)


SEM_SPEC =pl.BlockSpec(memory_space=pltpu.SEMAPHORE)
SIDE_EFFECT = pltpu.SideEffectType.DATAFLOW_SIDE_EFFECTING


def _split_refs(plan, srcs, lands, i, src_for, land_from):
    si, sbase, sstride, li, lbase, lstride, rows = plan[i]
    return (_rows_of(srcs[si], sbase, sstride, rows, src_for),
            _rows_of(lands[li], lbase, lstride, rows, land_from))


ALL_PEERS = tuple(range(1, NDEV))
SAME_CORE_AND_SIBLING = (1, 4, 2, 6)
OTHER_CHIPS = (4, 2, 6)


def _split_start(name, srcs, lands, plan, after=(), relations=ALL_PEERS):
    ns, nl, n = len(srcs), len(lands), len(plan)

    def body(*refs):
        s_refs = refs[:ns]
        l_refs = refs[ns:ns + nl]
        send_sems, recv_sems = refs[ns + nl + len(after):ns + nl + len(after) + 2]
        local_sems = refs[ns + nl + len(after) + 2]
        token = refs[ns + nl + len(after) + 3 + ns + nl]
        me = _mesh_pos()
        for i in range(n):
            src, dst = _split_refs(plan, s_refs, l_refs, i, me, me)
            pltpu.make_async_copy(src, dst, local_sems.at[i]).start()
        for k in relations:
            peer = _flip(me, k)
            for i in range(n):
                src, dst = _split_refs(plan, s_refs, l_refs, i, peer, me)
                pltpu.make_async_remote_copy(
                    src_ref=src, dst_ref=dst,
                    send_sem=send_sems.at[7 * i + k - 1], recv_sem=recv_sems.at[7 * i + k - 1],
                    device_id=peer, device_id_type=MESH).start()
        token[...] = jnp.zeros_like(token)

    hbm = lambda a: pltpu.HBM(a.shape, a.dtype)
    out_shape = ((pltpu.SemaphoreType.DMA((7 * n,)), pltpu.SemaphoreType.DMA((7 * n,)),
                  pltpu.SemaphoreType.DMA((n,)))
                 + tuple(hbm(a) for a in srcs) + tuple(hbm(a) for a in lands)
                 + (jax.ShapeDtypeStruct((8, LANE), F32),))
    out = pl.pallas_call(
        body, name=name, out_shape=out_shape,
        in_specs=[HBM_SPEC] * (ns + nl) + [ANY_SPEC] * len(after),
        out_specs=(SEM_SPEC, SEM_SPEC, SEM_SPEC) + (HBM_SPEC,) * (ns + nl)
        + (pl.BlockSpec(memory_space=pltpu.VMEM),),
        input_output_aliases={i: 3 + i for i in range(ns + nl)},
        compiler_params=pltpu.CompilerParams(has_side_effects=SIDE_EFFECT),
    )(*[pltpu.with_memory_space_constraint(a, pltpu.HBM) for a in tuple(srcs) + tuple(lands)],
      *after)
    _Order.tokens.append(out[-1])
    return out[0], out[1], out[2], out[3:3 + ns], out[3 + ns:3 + ns + nl]


def _split_wait(name, started, plan, relations=ALL_PEERS):
    send_sems, recv_sems, local_sems, srcs, lands = started
    ns, nl, n = len(srcs), len(lands), len(plan)
    after = [_Order.last] if _Order.last is not None else []

    def body(*refs):
        s_refs = refs[:ns]
        l_refs = refs[ns:ns + nl]
        send_sems, recv_sems, local_sems = refs[ns + nl:ns + nl + 3]
        me = _mesh_pos()
        for i in range(n):
            src, dst = _split_refs(plan, s_refs, l_refs, i, me, me)
            pltpu.make_async_copy(src, dst, local_sems.at[i]).wait()
        for k in relations:
            peer = _flip(me, k)
            for i in range(n):
                src, dst = _split_refs(plan, s_refs, l_refs, i, peer, peer)
                cp = pltpu.make_async_remote_copy(
                    src_ref=src, dst_ref=dst,
                    send_sem=send_sems.at[7 * i + k - 1], recv_sem=recv_sems.at[7 * i + k - 1],
                    device_id=peer, device_id_type=MESH)
                cp.wait_send()
                cp.wait_recv()

    hbm = lambda a: pltpu.HBM(a.shape, a.dtype)
    out = pl.pallas_call(
        body, name=name,
        out_shape=tuple(hbm(a) for a in srcs) + tuple(hbm(a) for a in lands),
        in_specs=[HBM_SPEC] * (ns + nl) + [SEM_SPEC] * 3 + [ANY_SPEC] * len(after),
        out_specs=(HBM_SPEC,) * (ns + nl),
        input_output_aliases={i: i for i in range(ns + nl)},
        compiler_params=pltpu.CompilerParams(has_side_effects=SIDE_EFFECT),
    )(*srcs, *lands, send_sems, recv_sems, local_sems, *after)
    return list(out[ns:])


def _forward_refs(plan, lands, i, block):
    _, _, _, li, lbase, lstride, rows = plan[i]
    return _rows_of(lands[li], lbase, lstride, rows, block)


def _forward_start(name, lands, plan):
    nl, n = len(lands), len(plan)

    def body(*refs):
        l_refs = refs[:nl]
        send_sems, recv_sems = refs[nl:nl + 2]
        token = refs[nl + 2 + nl]
        me = _mesh_pos()
        for j, kc in enumerate(OTHER_CHIPS):
            for i in range(n):
                rows = _forward_refs(plan, l_refs, i, _flip(me, kc))
                pltpu.make_async_remote_copy(
                    src_ref=rows, dst_ref=rows,
                    send_sem=send_sems.at[3 * i + j], recv_sem=recv_sems.at[3 * i + j],
                    device_id=_flip(me, 1), device_id_type=MESH).start()
        token[...] = jnp.zeros_like(token)

    hbm = lambda a: pltpu.HBM(a.shape, a.dtype)
    out = pl.pallas_call(
        body, name=name,
        out_shape=((pltpu.SemaphoreType.DMA((3 * n,)), pltpu.SemaphoreType.DMA((3 * n,)))
                   + tuple(hbm(a) for a in lands) + (jax.ShapeDtypeStruct((8, LANE), F32),)),
        in_specs=[HBM_SPEC] * nl,
        out_specs=(SEM_SPEC, SEM_SPEC) + (HBM_SPEC,) * nl
        + (pl.BlockSpec(memory_space=pltpu.VMEM),),
        input_output_aliases={i: 2 + i for i in range(nl)},
        compiler_params=pltpu.CompilerParams(has_side_effects=SIDE_EFFECT),
    )(*[pltpu.with_memory_space_constraint(a, pltpu.HBM) for a in lands])
    _Order.tokens.append(out[-1])
    return out[0], out[1], out[2:2 + nl]


def _forward_wait(name, started, plan):
    send_sems, recv_sems, lands = started
    nl, n = len(lands), len(plan)
    after = [_Order.last] if _Order.last is not None else []

    def body(*refs):
        l_refs = refs[:nl]
        send_sems, recv_sems = refs[nl:nl + 2]
        me = _mesh_pos()
        for j, kc in enumerate(OTHER_CHIPS):
            for i in range(n):
                sent = _forward_refs(plan, l_refs, i, _flip(me, kc))
                came = _forward_refs(plan, l_refs, i, _flip(_flip(me, 1), kc))
                cp = pltpu.make_async_remote_copy(
                    src_ref=sent, dst_ref=came,
                    send_sem=send_sems.at[3 * i + j], recv_sem=recv_sems.at[3 * i + j],
                    device_id=_flip(me, 1), device_id_type=MESH)
                cp.wait_send()
                cp.wait_recv()

    hbm = lambda a: pltpu.HBM(a.shape, a.dtype)
    out = pl.pallas_call(
        body, name=name, out_shape=tuple(hbm(a) for a in lands),
        in_specs=[HBM_SPEC] * nl + [SEM_SPEC] * 2 + [ANY_SPEC] * len(after),
        out_specs=(HBM_SPEC,) * nl,
        input_output_aliases={i: i for i in range(nl)},
        compiler_params=pltpu.CompilerParams(has_side_effects=SIDE_EFFECT),
    )(*lands, send_sems, recv_sems, *after)
    return list(out)


def _regroup_rows(name, padded, r, rp, lo, hi):
    d = padded.shape[1]
    pack = 4 // padded.dtype.itemsize
    assert r % pack == 0 and rp % ROW_ALIGN == 0 and lo % (8 * pack) == 0 and hi % (8 * pack) == 0
    r2, rp2, lo2, hi2 = r // pack, rp // pack, lo // pack, hi // pack
    u32 = jnp.uint32

    def body(x_ref, main_ref, cut_ref):
        x = pltpu.bitcast(x_ref[...], u32)
        joined = jnp.concatenate([x[rp2 * j:rp2 * j + r2] for j in range(NDEV)], axis=0)
        main = jnp.concatenate([joined[:lo2], joined[hi2:]], axis=0)
        cut = jnp.concatenate([joined[lo2:hi2], jnp.zeros((LANE // pack - (hi2 - lo2), LANE), u32)],
                              axis=0)
        main_ref[...] = pltpu.bitcast(main, padded.dtype)
        cut_ref[...] = pltpu.bitcast(cut, padded.dtype)

    return pl.pallas_call(
        body, name=name,
        out_shape=[jax.ShapeDtypeStruct((NDEV * r - (hi - lo), d), padded.dtype),
                   jax.ShapeDtypeStruct((LANE, d), padded.dtype)],
        grid=(d // LANE,),
        in_specs=[pl.BlockSpec((NDEV * rp, LANE), lambda i: (0, i))],
        out_specs=[pl.BlockSpec((NDEV * r - (hi - lo), LANE), lambda i: (0, i)),
                   pl.BlockSpec((LANE, LANE), lambda i: (0, i))],
        compiler_params=_params(("parallel",)),
    )(padded)


def _ungroup_rows(name, main, cut, r, rp, lo, hi):
    d = main.shape[1]
    pack = 4 // main.dtype.itemsize
    r2, rp2, lo2, hi2 = r // pack, rp // pack, lo // pack, hi // pack
    u32 = jnp.uint32

    def body(main_ref, cut_ref, o_ref):
        m = pltpu.bitcast(main_ref[...], u32)
        c = pltpu.bitcast(cut_ref[...], u32)
        joined = jnp.concatenate([m[:lo2], c[:hi2 - lo2], m[lo2:]], axis=0)
        zeros = jnp.zeros((rp2 - r2, LANE), u32)
        parts = []
        for j in range(NDEV):
            parts += [joined[r2 * j:r2 * (j + 1)], zeros]
        o_ref[...] = pltpu.bitcast(jnp.concatenate(parts, axis=0), main.dtype)

    return pl.pallas_call(
        body, name=name,
        out_shape=jax.ShapeDtypeStruct((NDEV * rp, d), main.dtype),
        grid=(d // LANE,),
        in_specs=[pl.BlockSpec((main.shape[0], LANE), lambda i: (0, i)),
                  pl.BlockSpec((LANE, LANE), lambda i: (0, i))],
        out_specs=pl.BlockSpec((NDEV * rp, LANE), lambda i: (0, i)),
        compiler_params=_params(("parallel",)),
    )(main, cut)


def _sum_slabs(name, a):
    s, r, c = a.shape

    def body(a_ref, o_ref):
        acc = a_ref[0].astype(F32)
        for i in range(1, s):
            acc = acc + a_ref[i].astype(F32)
        o_ref[...] = acc

    return pl.pallas_call(body, name=name, out_shape=jax.ShapeDtypeStruct((r, c), F32))(a)


def _adamw_update(g, w, m, v):
    mn = ADAM_B1 * m + (1.0 - ADAM_B1) * g
    vn = ADAM_B2 * v + (1.0 - ADAM_B2) * (g * g)
    m_hat = mn / (1.0 - ADAM_B1 ** ADAM_STEP)
    v_hat = vn / (1.0 - ADAM_B2 ** ADAM_STEP)
    delta = -ADAM_LR * (m_hat / (jnp.sqrt(v_hat) + ADAM_EPS) + ADAM_WD * w)
    return delta, mn, vn


def _adamw(name, gsrc, w, m, v, transposed=False, tr=256):
    s = gsrc.shape[0]
    lead = w.ndim == 3
    r, c = w.shape[-2:]
    step = LANE if transposed else 8
    tr = max(t for t in range(step, min(tr, r) + 1, step) if r % t == 0)

    def body(g_ref, w_ref, m_ref, v_ref, go_ref, d_ref, mo_ref, vo_ref):
        g = g_ref[0].astype(F32)
        for i in range(1, s):
            g = g + g_ref[i].astype(F32)
        if transposed:
            g = g.T[:, :c]
        delta, mn, vn = _adamw_update(g, w_ref[...], m_ref[...], v_ref[...])
        go_ref[...] = g
        d_ref[...] = delta
        mo_ref[...] = mn
        vo_ref[...] = vn

    if lead:
        blk = pl.BlockSpec((None, tr, c), lambda i: (0, i, 0))
    else:
        blk = pl.BlockSpec((tr, c), lambda i: (i, 0))
    if transposed:
        g_spec = pl.BlockSpec((s, gsrc.shape[1], tr), lambda i: (0, 0, i))
    else:
        g_spec = pl.BlockSpec((s, tr, c), lambda i: (0, i, 0))
    return pl.pallas_call(
        body, name=name, out_shape=[jax.ShapeDtypeStruct(w.shape, F32)] * 4,
        grid=(r // tr,),
        in_specs=[g_spec, blk, blk, blk], out_specs=[blk] * 4,
        compiler_params=_params(("parallel",)),
    )(gsrc, w, m, v)


REPLICATED = ("ffn1_norm", "mix_norm", "q_norm", "k_norm", "conv_b", "dt_bias", "a_log",
              "d_skip", "ssd_norm", "ffn2_norm")
ALL_WEIGHTS = ("ffn1_norm", "ffn1_w_gate", "ffn1_w_up", "ffn1_w_down", "mix_norm", "w_in",
               "q_norm", "k_norm", "conv_w", "conv_b", "dt_bias", "a_log", "d_skip", "ssd_norm",
               "w_attn_branch", "w_ssd_branch", "w_out", "ffn2_norm", "ffn2_w_gate", "ffn2_w_up",
               "ffn2_w_down")
BIG = (("ffn1_w_gate", True, "gu1t", 0), ("ffn1_w_up", True, "gu1t", 1),
       ("ffn1_w_down", False, "d1", 0), ("w_in", True, "wint", 0),
       ("w_attn_branch", True, "abt", 0), ("w_ssd_branch", False, "sb", 0),
       ("w_out", False, "out", 0),
       ("ffn2_w_gate", True, "gu2t", 0), ("ffn2_w_up", True, "gu2t", 1),
       ("ffn2_w_down", False, "d2", 0))


def _nrows(shape, cols):
    return -(-math.prod(shape) // cols)


def _pack_rows(arrs, cols, row_tile):
    parts = []
    for a in arrs:
        flat = a.reshape(-1)
        nr = -(-flat.shape[0] // cols)
        parts.append(jnp.pad(flat, (0, nr * cols - flat.shape[0])).reshape(nr, cols))
    out = jnp.concatenate(parts, axis=0)
    return jnp.pad(out, ((0, _round_up(out.shape[0], row_tile) - out.shape[0]), (0, 0)))


def _unpack_rows(packed, shapes):
    cols = packed.shape[-1]
    out, r0 = [], 0
    for sh in shapes:
        nr = _nrows(sh, cols)
        out.append(packed[r0:r0 + nr].reshape(-1)[:math.prod(sh)].reshape(tuple(sh)))
        r0 += nr
    return out


def kernel(x, ffn1_norm, ffn1_w_gate, ffn1_w_up, ffn1_w_down, mix_norm, w_in, q_norm, k_norm, conv_w, conv_b, dt_bias, a_log, d_skip, ssd_norm, w_attn_branch, w_ssd_branch, w_out, ffn2_norm, ffn2_w_gate, ffn2_w_up, ffn2_w_down, loss_target, m_ffn1_norm, m_ffn1_w_gate, m_ffn1_w_up, m_ffn1_w_down, m_mix_norm, m_w_in, m_q_norm, m_k_norm, m_conv_w, m_conv_b, m_dt_bias, m_a_log, m_d_skip, m_ssd_norm, m_w_attn_branch, m_w_ssd_branch, m_w_out, m_ffn2_norm, m_ffn2_w_gate, m_ffn2_w_up, m_ffn2_w_down, v_ffn1_norm, v_ffn1_w_gate, v_ffn1_w_up, v_ffn1_w_down, v_mix_norm, v_w_in, v_q_norm, v_k_norm, v_conv_w, v_conv_b, v_dt_bias, v_a_log, v_d_skip, v_ssd_norm, v_w_attn_branch, v_w_ssd_branch, v_w_out, v_ffn2_norm, v_ffn2_w_gate, v_ffn2_w_up, v_ffn2_w_down):
    given = dict(locals())
    wts = {n: given[n] for n in ALL_WEIGHTS}
    mom = {n: given["m_" + n] for n in ALL_WEIGHTS}
    var = {n: given["v_" + n] for n in ALL_WEIGHTS}
    d = x.shape[-1]
    nh = dt_bias.shape[1]
    my = _dev_index(_mesh_pos())

    def row_form(n, col_sharded):
        a = wts[n][0].T if col_sharded else wts[n][0]
        a = jnp.pad(a, ((0, _round_up(a.shape[0], ROW_ALIGN) - a.shape[0]), (0, 0)))
        return a.astype(MXU_DTYPE)

    _Order.tokens, _Order.last = [], None
    shard = {n: row_form(n, cs) for n, cs, _, _ in BIG}
    entries = {buf: [e for e in BIG if e[2] == buf] for buf in dict.fromkeys(e[2] for e in BIG)}

    def buf_shape(buf):
        r, c = shard[entries[buf][0][0]].shape
        return (len(entries[buf]) * NDEV * r, c)

    def gather_plan(bufs):
        srcs, lands, plan = [], [], []
        for li, buf in enumerate(bufs):
            lands.append(lax.empty(buf_shape(buf), MXU_DTYPE))
            for n, _, _, pos in entries[buf]:
                r = shard[n].shape[0]
                plan.append((len(srcs), 0, 0, li, pos * NDEV * r, r, r))
                srcs.append(shard[n])
        return srcs, lands, plan

    def scatter_plan(bufs, grads):
        srcs, lands, plan, names = [], [], [], []
        for si, buf in enumerate(bufs):
            srcs.append(grads[buf])
            for n, _, _, pos in entries[buf]:
                r, c = shard[n].shape
                plan.append((si, pos * NDEV * r, r, len(lands), 0, r, r))
                lands.append(lax.empty((NDEV * r, c), MXU_DTYPE))
                names.append(n)
        return srcs, lands, plan, names

    first_bufs = ("gu1t", "d1")
    shards, dests, out_shapes = [], [], []
    for bi, buf in enumerate(first_bufs):
        out_shapes.append((buf_shape(buf), MXU_DTYPE))
        for n, _, _, pos in entries[buf]:
            r = shard[n].shape[0]
            shards.append(shard[n])
            dests.append((bi, pos * NDEV * r, r))
    conv_rows = _pack_rows([conv_w[0]], LANE, ROW_ALIGN)
    shards.append(conv_rows)
    dests.append((len(first_bufs), 0, conv_rows.shape[0]))
    out_shapes.append(((NDEV * conv_rows.shape[0], LANE), F32))
    gathered = _gather("gather_first", shards, dests, out_shapes)

    in_cols = w_in.shape[2]
    in_pad = _round_up(in_cols, ROW_ALIGN)
    dt_off = NDEV * in_cols - 2 * d - nh
    second_bufs = ("wint",)
    third_bufs = ("abt", "sb", "out", "gu2t", "d2")
    plan2 = gather_plan(second_bufs)
    started2 = _split_start("gather_in_start", *plan2, after=[gathered[0]],
                            relations=SAME_CORE_AND_SIBLING)
    forwarded, started3 = [], []

    class Weights(dict):
        def after_first_ffn(self):
            lands = _split_wait("gather_in_wait", started2, plan2[2],
                                relations=SAME_CORE_AND_SIBLING)
            forwarded.append(_forward_start("gather_in_forward", lands, plan2[2]))

        def __missing__(self, key):
            if key in ("maint", "dtt"):
                wint = _forward_wait("gather_in_arrive", forwarded[0], plan2[2])[0]
                plan3 = gather_plan(third_bufs)
                started3.append((_split_start("gather_rest_start", *plan3, after=[wint]), plan3[2]))
                self["maint"], self["dtt"] = _regroup_rows(
                    "regroup_w_in", wint, in_cols, in_pad, dt_off, dt_off + nh)
            else:
                st, plan = started3[0]
                for buf, a in zip(third_bufs, _split_wait("gather_rest_wait", st, plan)):
                    self[buf] = a
            return self[key]

    w = Weights(gu1t=gathered[0], d1=gathered[1])
    p = {n: wts[n] for n in REPLICATED}
    conv_all = gathered[-1].reshape(NDEV, conv_rows.shape[0] * LANE)[:, :math.prod(conv_w.shape[1:])]
    p["conv_w"] = (conv_all.reshape((NDEV,) + conv_w.shape[1:]).transpose(1, 0, 2)
                   .reshape(conv_w.shape[1], NDEV * conv_w.shape[2]))

    groups = (("scatter_late", ("gu2t", "d2", "out", "abt", "sb")),
              ("scatter_in", ("maint", "dtt")),
              ("scatter_first", ("gu1t", "d1")))
    in_flight = []

    class Grads(dict):
        def __setitem__(self, key, value):
            dict.__setitem__(self, key, value)
            for tag, need in groups:
                if key in need and all(k in self for k in need):
                    if tag == "scatter_in":
                        gwin = _ungroup_rows("ungroup_w_in", self["maint"], self["dtt"],
                                             in_cols, in_pad, dt_off, dt_off + nh)
                        bufs, grads = ("wint",), {"wint": gwin}
                    else:
                        bufs, grads = need, self
                    srcs, lands, plan, names = scatter_plan(bufs, grads)
                    in_flight.append((tag, _split_start(tag + "_start", srcs, lands, plan),
                                      plan, names))

    loss_vec, dx, gw, gp = _local_step(x[0], loss_target[0], w, p, Grads(),
                                       aw=w_attn_branch.shape[1])

    small_names = REPLICATED + ("conv_w",)
    small_shapes = [gp[n].shape for n in small_names]
    small = _pack_rows([gp[n] for n in small_names], LANE, ROW_ALIGN)
    small_plan = [(0, 0, 0, 0, 0, small.shape[0], small.shape[0])]
    small_started = _split_start("gather_small_start", [small],
                                 [lax.empty((NDEV * small.shape[0], LANE), F32)], small_plan)

    outs = [{}, {}, {}, {}]
    col_sharded_of = {n: cs for n, cs, _, _ in BIG}
    for tag, started, plan, names in in_flight:
        for n, rv in zip(names, _split_wait(tag + "_wait", started, plan)):
            rv = rv.reshape(NDEV, shard[n].shape[0], shard[n].shape[1])
            if col_sharded_of[n] and rv.shape[1] == wts[n].shape[2]:
                res = _adamw("adamw_" + n, rv, wts[n][0].T, mom[n][0].T, var[n][0].T)
                res = [a.T for a in res]
            else:
                res = _adamw("adamw_" + n, rv, wts[n][0], mom[n][0], var[n][0],
                             transposed=col_sharded_of[n])
            _Order.done(res)
            for k in range(4):
                outs[k][n] = res[k][None]

    small_all = _split_wait("gather_small_wait", small_started, small_plan)[0]
    small_all = small_all.reshape(NDEV, small.shape[0], LANE)
    small_g = _unpack_rows(_sum_slabs("sum_small_grads", small_all), small_shapes)
    small_g = dict(zip(small_names, small_g))
    cs = conv_w.shape[2]
    small_g["conv_w"] = lax.dynamic_slice_in_dim(small_g["conv_w"], my * cs, cs, axis=1)
    small_shard_shapes = [wts[n].shape[-2:] for n in small_names]
    sg = _pack_rows([small_g[n] for n in small_names], LANE, 8)
    sw = _pack_rows([wts[n] for n in small_names], LANE, 8)
    sm = _pack_rows([mom[n] for n in small_names], LANE, 8)
    sv = _pack_rows([var[n] for n in small_names], LANE, 8)
    res_small = _adamw("adamw_small", sg[None], sw, sm, sv, tr=sg.shape[0])
    for k in range(4):
        for n, a in zip(small_names, _unpack_rows(res_small[k], small_shard_shapes)):
            outs[k][n] = a.reshape(wts[n].shape)

    loss = lax.psum(0.5 * jnp.sum(loss_vec) / d, ("x", "y", "c"))
    result = [loss, dx[None]]
    for k in range(4):
        result += [outs[k][n] for n in ALL_WEIGHTS]
    return tuple(result)
```

```python
import functools
import math

import numpy as np
import jax
import jax.numpy as jnp
from jax import lax
from jax.experimental import pallas as pl
from jax.experimental.pallas import tpu as pltpu

F32 = jnp.float32
BF16 = jnp.bfloat16
MXU_DTYPE = BF16
ACT_DTYPE = BF16

NDEV = 8
EPS = 1e-6
HD = 64
QB = 128
PATTERNS = ((128, 1), (512, 4), (2048, 16))
ALIBI_MAX_EXP = 8.0
SSD_P = 64
SSD_N = 128
SSD_G = 4
SSD_Q = 128
SSD_K = 4
NEG = -1e30
LANE = 128
ROW_ALIGN = 16
VMEM_LIMIT = 56 * 1024 * 1024

ADAM_LR, ADAM_B1, ADAM_B2, ADAM_EPS, ADAM_WD, ADAM_STEP = 0.001, 0.9, 0.999, 1e-8, 0.01, 10

NN = (((1,), (0,)), ((), ()))
NT = (((1,), (1,)), ((), ()))
TN = (((0,), (0,)), ((), ()))


BNN = (((2,), (1,)), ((0,), (0,)))
BNT = (((2,), (2,)), ((0,), (0,)))
BTN = (((1,), (1,)), ((0,), (0,)))
_DOT_GRADS = {
    NN: (("g", "b", NT), ("a", "g", TN)),
    NT: (("g", "b", NN), ("g", "a", TN)),
    TN: (("b", "g", NT), ("a", "g", NN)),
    BNT: (("g", "b", BNN), ("g", "a", BTN)),
    BTN: (("b", "g", BNT), ("a", "g", BNN)),
}


def _mxu(a, b, dims):
    return lax.dot_general(a.astype(MXU_DTYPE), b.astype(MXU_DTYPE), dims,
                           preferred_element_type=F32)


@functools.partial(jax.custom_vjp, nondiff_argnums=(2,))
def _dot_vjp(a, b, dims):
    return _mxu(a, b, dims)


def _dot_vjp_fwd(a, b, dims):
    return _mxu(a, b, dims), (a.astype(MXU_DTYPE), b.astype(MXU_DTYPE))


def _dot_vjp_bwd(dims, res, g):
    ops = {"a": res[0], "b": res[1], "g": g}
    (x1, y1, d1), (x2, y2, d2) = _DOT_GRADS[dims]
    return _mxu(ops[x1], ops[y1], d1), _mxu(ops[x2], ops[y2], d2)


_dot_vjp.defvjp(_dot_vjp_fwd, _dot_vjp_bwd)


def _dot(a, b, dims=NN):
    return _dot_vjp(a, b, dims)


def _split3(a):
    hi = a.astype(BF16)
    r = a - hi.astype(F32)
    mid = r.astype(BF16)
    lo = (r - mid.astype(F32)).astype(BF16)
    return hi, mid, lo


def _dot3(a, b, dims=NN, split=0):
    if split == 0:
        bb = b.astype(BF16)
        parts = [lax.dot_general(s, bb, dims, preferred_element_type=F32) for s in _split3(a)]
    else:
        aa = a.astype(BF16)
        parts = [lax.dot_general(aa, s, dims, preferred_element_type=F32) for s in _split3(b)]
    return parts[0] + parts[1] + parts[2]


@jax.custom_vjp
def _spread(v, e):
    return _dot3(v, e)


def _spread_fwd(v, e):
    return _dot3(v, e), e


def _spread_bwd(e, g):
    return _dot3(g, e, NT), jnp.zeros_like(e)


_spread.defvjp(_spread_fwd, _spread_bwd)


@jax.custom_vjp
def _running_sum(a, lower):
    return _dot3(lower, a, NN, split=1)


def _running_sum_fwd(a, lower):
    return _dot3(lower, a, NN, split=1), lower


def _running_sum_bwd(lower, g):
    return _dot3(lower, g, TN, split=1), jnp.zeros_like(lower)


_running_sum.defvjp(_running_sum_fwd, _running_sum_bwd)


def _tile(n, cap):
    if n <= cap:
        return n
    best = None
    for t in range(LANE, cap + 1, LANE):
        if n % t == 0:
            best = t
    assert best is not None, (n, cap)
    return best


def _params(sem):
    return pltpu.CompilerParams(dimension_semantics=sem, vmem_limit_bytes=VMEM_LIMIT)


def _round_up(n, m):
    return -(-n // m) * m


class _Order:
    tokens = []
    last = None

    @classmethod
    def take(cls):
        out, cls.tokens = cls.tokens, []
        return out

    @classmethod
    def done(cls, result):
        cls.last = result[0] if isinstance(result, (list, tuple)) else result
        return result


ANY_SPEC = pl.BlockSpec(memory_space=pl.ANY)


def _mm(name, a, b, mode, out_dtype=F32, res=None, scale=1.0,
        cap_m=1408, cap_n=1408, cap_k=1408):
    segs = list(a) if isinstance(a, (list, tuple)) else [a]
    nseg = len(segs)
    if mode == "tn":
        k = segs[0].shape[0]
        widths = [s.shape[1] for s in segs]
        m = sum(widths)
        k2, n = b.shape
        tm = _tile(math.gcd(*widths), cap_m)
        tk = _tile(k, cap_k)
        counts = [wd // tm for wd in widths]
    else:
        m = segs[0].shape[0]
        widths = [s.shape[1] for s in segs]
        k = sum(widths)
        (k2, n) = b.shape if mode == "nn" else b.shape[::-1]
        tm = _tile(m, cap_m)
        tk = _tile(math.gcd(*widths), cap_k)
        counts = [wd // tk for wd in widths]
    assert k == k2, (name, [s.shape for s in segs], b.shape, mode)
    tn = _tile(n, cap_n)
    nk = k // tk
    starts = [sum(counts[:s]) for s in range(nseg)]
    dims = {"nn": NN, "nt": NT, "tn": TN}[mode]

    def a_spec(s):
        lo, cnt = starts[s], counts[s]
        if mode == "tn":
            if nseg == 1:
                return pl.BlockSpec((tk, tm), lambda i, j, kk: (kk, i))
            return pl.BlockSpec(
                (tk, tm), lambda i, j, kk: (jnp.where((i >= lo) & (i < lo + cnt), kk, 0),
                                            jnp.clip(i - lo, 0, cnt - 1)))
        if nseg == 1:
            return pl.BlockSpec((tm, tk), lambda i, j, kk: (i, kk))
        return pl.BlockSpec((tm, tk), lambda i, j, kk: (i, jnp.clip(kk - lo, 0, cnt - 1)))

    b_spec = (pl.BlockSpec((tn, tk), lambda i, j, kk: (j, kk)) if mode == "nt"
              else pl.BlockSpec((tk, tn), lambda i, j, kk: (kk, j)))
    o_spec = pl.BlockSpec((tm, tn), lambda i, j, kk: (i, j))
    has_res = res is not None
    use_acc = nk > 1 or nseg > 1
    ties = _Order.take()
    nt_ = len(ties)

    def body(*refs):
        a_refs = refs[:nseg]
        b_ref = refs[nseg]
        r_ref = refs[nseg + 1] if has_res else None
        o_ref = refs[nseg + 1 + has_res + nt_]
        scr = refs[nseg + 2 + has_res + nt_:]

        def finish(acc):
            if scale != 1.0:
                acc = acc * scale
            if has_res:
                acc = r_ref[...].astype(F32) + acc
            o_ref[...] = acc.astype(o_ref.dtype)

        if not use_acc:
            finish(_dot(a_refs[0][...], b_ref[...], dims))
            return
        acc_ref = scr[0]
        kk = pl.program_id(2)
        sel = pl.program_id(0) if mode == "tn" else kk

        @pl.when(kk == 0)
        def _():
            acc_ref[...] = jnp.zeros_like(acc_ref)

        for s in range(nseg):
            def add(s=s):
                acc_ref[...] += _dot(a_refs[s][...], b_ref[...], dims)
            if nseg == 1:
                add()
            else:
                pl.when((sel >= starts[s]) & (sel < starts[s] + counts[s]))(add)

        @pl.when(kk == nk - 1)
        def _():
            finish(acc_ref[...])

    in_specs = ([a_spec(s) for s in range(nseg)] + [b_spec] + ([o_spec] if has_res else [])
                + [ANY_SPEC] * nt_)
    args = tuple(segs) + (b,) + ((res,) if has_res else ()) + tuple(ties)
    return _Order.done(pl.pallas_call(
        body, name=name,
        out_shape=jax.ShapeDtypeStruct((m, n), out_dtype),
        grid=(m // tm, n // tn, nk),
        in_specs=in_specs, out_specs=o_spec,
        scratch_shapes=[pltpu.VMEM((tm, tn), F32)] if use_acc else [],
        compiler_params=_params(("parallel", "parallel", "arbitrary")),
    )(*args))


def _act(g, u):
    return _silu(g.astype(F32)) * u.astype(F32)


def _ffn_up(name, h, wgut, cap_m=1024, cap_n=1408):
    m, k = h.shape
    dff = wgut.shape[0] // 2
    tm, tn = _tile(m, cap_m), _tile(dff, cap_n)
    nj = dff // tn
    ties = _Order.take()

    def body(h_ref, wg_ref, wu_ref, *rest):
        g_ref, u_ref, a_ref = rest[len(ties):]
        hv = h_ref[...]
        g = _dot(hv, wg_ref[...], NT)
        u = _dot(hv, wu_ref[...], NT)
        g_ref[...] = g.astype(g_ref.dtype)
        u_ref[...] = u.astype(u_ref.dtype)
        a_ref[...] = _act(g, u).astype(a_ref.dtype)

    o_spec = pl.BlockSpec((tm, tn), lambda i, j: (i, j))
    return _Order.done(pl.pallas_call(
        body, name=name, out_shape=[jax.ShapeDtypeStruct((m, dff), ACT_DTYPE)] * 3,
        grid=(m // tm, nj),
        in_specs=[pl.BlockSpec((tm, k), lambda i, j: (i, 0)),
                  pl.BlockSpec((tn, k), lambda i, j: (j, 0)),
                  pl.BlockSpec((tn, k), lambda i, j: (nj + j, 0))] + [ANY_SPEC] * len(ties),
        out_specs=[o_spec] * 3,
        compiler_params=_params(("parallel", "parallel")),
    )(h, wgut, wgut, *ties))


def _ffn_dact(name, dxo, wd, g, u, scale, cap_m=1024, cap_n=1408):
    m, k = dxo.shape
    dff = wd.shape[0]
    tm, tn = _tile(m, cap_m), _tile(dff, cap_n)
    ties = _Order.take()

    def body(d_ref, w_ref, g_ref, u_ref, *rest):
        dg_ref, du_ref = rest[len(ties):]
        da = _dot(d_ref[...], w_ref[...], NT) * scale
        _, vjp = jax.vjp(_act, g_ref[...], u_ref[...])
        dg, du = vjp(da)
        dg_ref[...] = dg.astype(dg_ref.dtype)
        du_ref[...] = du.astype(du_ref.dtype)

    o_spec = pl.BlockSpec((tm, tn), lambda i, j: (i, j))
    return _Order.done(pl.pallas_call(
        body, name=name, out_shape=[jax.ShapeDtypeStruct((m, dff), ACT_DTYPE)] * 2,
        grid=(m // tm, dff // tn),
        in_specs=[pl.BlockSpec((tm, k), lambda i, j: (i, 0)),
                  pl.BlockSpec((tn, k), lambda i, j: (j, 0)), o_spec, o_spec]
        + [ANY_SPEC] * len(ties),
        out_specs=[o_spec] * 2,
        compiler_params=_params(("parallel", "parallel")),
    )(dxo, wd, g, u, *ties))


def _rw(name, fn, ins, outs, accs=(), tr=512, ncb=1):
    t = next(a.shape[0] for kind, a, _, _ in ins if kind == "row")
    assert t % tr == 0
    n_in = len(ins)
    n_pieces = sum(len(w) for w, _ in outs)

    def spec(kind, arr, width, base):
        if kind == "row":
            return pl.BlockSpec((tr, width), lambda j, i: (i, base + j))
        return pl.BlockSpec((arr.shape[0], width), lambda j, i: (0, base + j))

    in_specs = [spec(*s) for s in ins]
    out_shapes, out_specs = [], []
    for widths, dt in outs:
        w = sum(widths)
        out_shapes.append(jax.ShapeDtypeStruct((t, w * ncb), dt))
        out_specs.append(pl.BlockSpec((tr, w), lambda j, i: (i, j)))
    for rows, width in accs:
        out_shapes.append(jax.ShapeDtypeStruct((rows, width * ncb), F32))
        out_specs.append(pl.BlockSpec((rows, width), lambda j, i: (0, j)))

    ties = _Order.take()
    nt_ = len(ties)
    in_specs = in_specs + [ANY_SPEC] * nt_

    def body(*refs):
        vals = [r[...] for r in refs[:n_in]]
        res = fn(*vals)
        o_refs = refs[n_in + nt_:n_in + nt_ + len(outs)]
        a_refs = refs[n_in + nt_ + len(outs):]
        p = 0
        for (widths, _), o_ref in zip(outs, o_refs):
            off = 0
            for w in widths:
                if len(widths) == 1:
                    o_ref[...] = res[p].astype(o_ref.dtype)
                else:
                    o_ref[:, off:off + w] = res[p].astype(o_ref.dtype)
                off += w
                p += 1
        i = pl.program_id(1)
        for a_ref, v in zip(a_refs, res[n_pieces:]):
            @pl.when(i == 0)
            def _(a_ref=a_ref, v=v):
                a_ref[...] = v

            @pl.when(i > 0)
            def _(a_ref=a_ref, v=v):
                a_ref[...] += v

    return _Order.done(pl.pallas_call(
        body, name=name, out_shape=out_shapes,
        grid=(ncb, t // tr), in_specs=in_specs, out_specs=out_specs,
        compiler_params=_params(("parallel", "arbitrary")),
    )(*[a for _, a, _, _ in ins], *ties))


def _rms(x, g):
    x = x.astype(F32)
    return x * lax.rsqrt(jnp.mean(x * x, axis=-1, keepdims=True) + EPS) * g


def _silu(x):
    return x * jax.nn.sigmoid(x)


def _colsum(v):
    return jnp.sum(v, axis=0, keepdims=True)


def _pair_norm(x, g):
    w = 2 * HD
    ri = lax.broadcasted_iota(jnp.int32, (w, w), 0)
    ci = lax.broadcasted_iota(jnp.int32, (w, w), 1)
    same_head = ((ri < HD) == (ci < HD)).astype(F32)
    ms = _spread(x * x, same_head) * (1.0 / HD)
    return x * lax.rsqrt(ms + EPS) * g


ATTN_SCALE = 1.0 / math.sqrt(HD)


def _attn_bias(coef):
    key = lax.broadcasted_iota(jnp.int32, (QB, QB), 0)
    qry = lax.broadcasted_iota(jnp.int32, (QB, QB), 1)
    dist = (qry - key).astype(F32)
    own = jnp.where(qry >= key, -coef * dist, NEG)
    prev = jnp.where(qry <= key, -coef * (dist + float(QB)), NEG)
    return own, prev


def _attn_pair(qn, kcn, kpn, vc, vp, b_own, b_prev):
    nb = qn.shape[0]
    w = 2 * HD
    lane = lax.broadcasted_iota(jnp.int32, (1, 1, w), 2)
    eye = (lax.broadcasted_iota(jnp.int32, (QB, QB), 0)
           == lax.broadcasted_iota(jnp.int32, (QB, QB), 1)).astype(F32)
    out = jnp.zeros((nb, QB, w), F32)
    lb = jnp.zeros((nb * QB, w), F32)
    for hh in range(2):
        mask = ((lane < HD) if hh == 0 else (lane >= HD)).astype(F32)
        qm = qn * mask
        lc = _dot(kcn, qm, BNT) + b_own[hh]
        lp = _dot(kpn, qm, BNT) + b_prev[hh]
        m = lax.stop_gradient(jnp.maximum(jnp.max(lc, axis=1, keepdims=True),
                                          jnp.max(lp, axis=1, keepdims=True)))
        pc = jnp.exp(lc - m)
        pp = jnp.exp(lp - m)
        l = jnp.sum(pc, axis=1, keepdims=True) + jnp.sum(pp, axis=1, keepdims=True)
        inv = 1.0 / l
        out = out + (_dot(pc * inv, vc, BTN) + _dot(pp * inv, vp, BTN)) * mask
        diag = (eye * (m + jnp.log(l))).reshape(nb * QB, QB)
        lb = lb + _spread(diag, jnp.broadcast_to(mask[0], (QB, w)))
    return out, lb.reshape(nb, QB, w)


NORM_ROWS = 128
NORM_UNROLL = 4
EPILOGUE_ROWS = 512
ATTN_BATCH_FWD = 16
ATTN_BATCH_BWD = 8


def _unit_rows(u, d):
    r = u & (d - 1)
    n = u >> (d.bit_length() - 1)

    def rows(blk):
        start = pl.multiple_of(blk * (QB * d), QB * d)
        return pl.ds(start, QB) if d == 1 else pl.ds(start + r, QB, stride=d)

    return rows(n), rows(jnp.maximum(n - 1, 0)), n == 0


def _unit_batch(i, nbatch, d, bias, qf, kf, vf):
    units = [_unit_rows(i * nbatch + j, d) for j in range(nbatch)]
    cur = lambda ref: jnp.stack([ref[c, :] for c, _, _ in units])
    prv = lambda ref: jnp.stack([ref[p, :] for _, p, _ in units])
    b_own = [b[0] for b in bias]
    b_prev = [jnp.stack([jnp.where(first, NEG, b[1]) for _, _, first in units]) for b in bias]
    return units, (cur(qf), cur(kf), prv(kf), cur(vf), prv(vf), b_own, b_prev)


def _q_norm(x, g):
    return _pair_norm(x, g * ATTN_SCALE)


def _attn_prologue(t, q_ref, k_ref, v_ref, qg_ref, kg_ref, qf, kf, vf):
    def chunk(c, carry):
        rows = pl.ds(pl.multiple_of(c * NORM_ROWS, NORM_ROWS), NORM_ROWS)
        qf[rows, :] = _q_norm(q_ref[rows, :].astype(F32), qg_ref[...])
        kf[rows, :] = _pair_norm(k_ref[rows, :].astype(F32), kg_ref[...])
        vf[rows, :] = v_ref[rows, :].astype(F32)
        return carry
    lax.fori_loop(0, t // NORM_ROWS, chunk, 0, unroll=NORM_UNROLL)


def _attn_specs(t, bases):
    w = 2 * HD
    ins = [pl.BlockSpec((t, w), functools.partial(lambda p, c, b: (0, b + p), b=b)) for b in bases]
    gain = pl.BlockSpec((1, w), lambda p, c: (0, 0))
    blk = pl.BlockSpec((t, w), lambda p, c: (0, p))
    return ins, gain, blk


def _attn_fwd(name, proj, bases, qg, kg, coefs, d):
    t = proj.shape[0]
    npairs = coefs.shape[0] // 2
    w = 2 * HD
    ins, gain, blk = _attn_specs(t, bases)

    def body(coef_ref, q_ref, k_ref, v_ref, qg_ref, kg_ref, o_ref, l_ref, qf, kf, vf):
        p = pl.program_id(0)
        bias = (_attn_bias(coef_ref[2 * p]), _attn_bias(coef_ref[2 * p + 1]))
        _attn_prologue(t, q_ref, k_ref, v_ref, qg_ref, kg_ref, qf, kf, vf)

        def step(i, carry):
            units, ins = _unit_batch(i, ATTN_BATCH_FWD, d, bias, qf, kf, vf)
            o, lb = _attn_pair(*ins)
            for j, (cur, _, _) in enumerate(units):
                o_ref[cur, :] = o[j]
                l_ref[cur, :] = lb[j]
            return carry

        lax.fori_loop(0, t // QB // ATTN_BATCH_FWD, step, 0)

    return pl.pallas_call(
        body, name=name,
        out_shape=[jax.ShapeDtypeStruct((t, npairs * w), F32)] * 2,
        grid_spec=pltpu.PrefetchScalarGridSpec(
            num_scalar_prefetch=1, grid=(npairs,),
            in_specs=ins + [gain, gain], out_specs=[blk, blk],
            scratch_shapes=[pltpu.VMEM((t, w), F32)] * 3),
        compiler_params=_params(("arbitrary",)),
    )(coefs, proj, proj, proj, qg, kg)


def _attn_bwd(name, proj, bases, qg, kg, coefs, d, do, dl):
    t = proj.shape[0]
    npairs = coefs.shape[0] // 2
    w = 2 * HD
    ins, gain, blk = _attn_specs(t, bases)

    def body(coef_ref, q_ref, k_ref, v_ref, qg_ref, kg_ref, do_ref, dl_ref,
             dq_ref, dk_ref, dv_ref, dqg_ref, dkg_ref, qf, kf, vf, dqf, dkf, dvf):
        p = pl.program_id(0)
        bias = (_attn_bias(coef_ref[2 * p]), _attn_bias(coef_ref[2 * p + 1]))
        _attn_prologue(t, q_ref, k_ref, v_ref, qg_ref, kg_ref, qf, kf, vf)
        dkf[...] = jnp.zeros_like(dkf)
        dvf[...] = jnp.zeros_like(dvf)

        def step(i, carry):
            units, ins = _unit_batch(i, ATTN_BATCH_BWD, d, bias, qf, kf, vf)
            f = lambda a, b, c, e, g: _attn_pair(a, b, c, e, g, *ins[5:])
            _, vjp = jax.vjp(f, *ins[:5])
            cot = (jnp.stack([do_ref[cur, :] for cur, _, _ in units]),
                   jnp.stack([dl_ref[cur, :] for cur, _, _ in units]))
            dq, dkc, dkp, dvc, dvp = vjp(cot)
            for j, (cur, prv, _) in enumerate(units):
                dqf[cur, :] = dq[j]
                dkf[cur, :] += dkc[j]
                dkf[prv, :] += dkp[j]
                dvf[cur, :] += dvc[j]
                dvf[prv, :] += dvp[j]
            return carry

        lax.fori_loop(0, t // QB // ATTN_BATCH_BWD, step, 0)

        def chunk(c, carry):
            dqg_acc, dkg_acc = carry
            rows = pl.ds(pl.multiple_of(c * EPILOGUE_ROWS, EPILOGUE_ROWS), EPILOGUE_ROWS)
            _, vq = jax.vjp(_q_norm, q_ref[rows, :].astype(F32), qg_ref[...])
            dq, dqg = vq(dqf[rows, :])
            _, vk = jax.vjp(_pair_norm, k_ref[rows, :].astype(F32), kg_ref[...])
            dk, dkg = vk(dkf[rows, :])
            dq_ref[rows, :] = dq.astype(dq_ref.dtype)
            dk_ref[rows, :] = dk.astype(dk_ref.dtype)
            dv_ref[rows, :] = dvf[rows, :].astype(dv_ref.dtype)
            return dqg_acc + dqg, dkg_acc + dkg

        zero = jnp.zeros((1, w), F32)
        dqg, dkg = lax.fori_loop(0, t // EPILOGUE_ROWS, chunk, (zero, zero))

        @pl.when(p == 0)
        def _():
            dqg_ref[...] = dqg
            dkg_ref[...] = dkg

        @pl.when(p > 0)
        def _():
            dqg_ref[...] += dqg
            dkg_ref[...] += dkg

    big = jax.ShapeDtypeStruct((t, npairs * w), ACT_DTYPE)
    small = jax.ShapeDtypeStruct((1, w), F32)
    return pl.pallas_call(
        body, name=name,
        out_shape=[big, big, big, small, small],
        grid_spec=pltpu.PrefetchScalarGridSpec(
            num_scalar_prefetch=1, grid=(npairs,),
            in_specs=ins + [gain, gain, blk, blk],
            out_specs=[blk, blk, blk, gain, gain],
            scratch_shapes=[pltpu.VMEM((t, w), F32)] * 6),
        compiler_params=_params(("arbitrary",)),
    )(coefs, proj, proj, proj, qg, kg, do, dl)


CONV_ROWS = 256
CONV_HALO = 8


def _rows_back(x, s):
    return x if s == 0 else pltpu.roll(x, s, 0)


def _rows_ahead(x, s):
    return x if s == 0 else pltpu.roll(x, x.shape[0] - s, 0)


def _conv_pre(u, w, b):
    y = b
    for kk in range(SSD_K):
        y = y + w[kk:kk + 1, :] * _rows_back(u, SSD_K - 1 - kk)
    return y


def _stage_padded(dst, src_ref, t):
    zeros = jnp.zeros((CONV_HALO, dst.shape[1]), F32)
    dst[0:CONV_HALO, :] = zeros
    dst[t + CONV_HALO:t + 2 * CONV_HALO, :] = zeros
    dst[CONV_HALO:t + CONV_HALO, :] = src_ref[...].astype(F32)


def _chunk_rows(c):
    r0 = pl.multiple_of(c * CONV_ROWS, CONV_ROWS)
    return pl.ds(r0, CONV_ROWS + 2 * CONV_HALO), pl.ds(r0, CONV_ROWS)


def _conv_fwd(name, src, base, w, b, cw=128):
    t = src.shape[0]
    c = w.shape[1]
    centre = slice(CONV_HALO, CONV_HALO + CONV_ROWS)

    def body(u_ref, w_ref, b_ref, o_ref, up):
        _stage_padded(up, u_ref, t)
        wv, bv = w_ref[...], b_ref[...]

        def chunk(ci, carry):
            ext, rows = _chunk_rows(ci)
            y = _conv_pre(up[ext, :], wv, bv)
            o_ref[rows, :] = _silu(y)[centre].astype(o_ref.dtype)
            return carry

        lax.fori_loop(0, t // CONV_ROWS, chunk, 0)

    return pl.pallas_call(
        body, name=name, out_shape=jax.ShapeDtypeStruct((t, c), ACT_DTYPE),
        grid=(c // cw,),
        in_specs=[pl.BlockSpec((t, cw), lambda j: (0, base + j)),
                  pl.BlockSpec((SSD_K, cw), lambda j: (0, j)),
                  pl.BlockSpec((1, cw), lambda j: (0, j))],
        out_specs=pl.BlockSpec((t, cw), lambda j: (0, j)),
        scratch_shapes=[pltpu.VMEM((t + 2 * CONV_HALO, cw), F32)],
        compiler_params=_params(("parallel",)),
    )(src, w, b)


def _conv_bwd(name, src, base, w, b, dout, cw=128):
    t = src.shape[0]
    c = w.shape[1]

    centre = slice(CONV_HALO, CONV_HALO + CONV_ROWS)

    def body(u_ref, w_ref, b_ref, d_ref, du_ref, dw_ref, db_ref, up, dp):
        _stage_padded(up, u_ref, t)
        _stage_padded(dp, d_ref, t)
        wv, bv = w_ref[...], b_ref[...]

        def chunk(ci, carry):
            dws, db = carry
            ext, rows = _chunk_rows(ci)
            u = up[ext, :]
            y = _conv_pre(u, wv, bv)
            sg = jax.nn.sigmoid(y)
            dy = dp[ext, :] * (sg * (1.0 + y * (1.0 - sg)))
            du = jnp.zeros_like(u)
            new_dws = []
            for kk in range(SSD_K):
                s = SSD_K - 1 - kk
                du = du + wv[kk:kk + 1, :] * _rows_ahead(dy, s)
                new_dws.append(dws[kk] + _colsum((dy * _rows_back(u, s))[centre]))
            du_ref[rows, :] = du[centre].astype(du_ref.dtype)
            return tuple(new_dws), db + _colsum(dy[centre])

        zero = jnp.zeros((1, cw), F32)
        dws, db = lax.fori_loop(0, t // CONV_ROWS, chunk, ((zero,) * SSD_K, zero))
        for kk in range(SSD_K):
            dw_ref[kk:kk + 1, :] = dws[kk]
        db_ref[...] = db

    return pl.pallas_call(
        body, name=name,
        out_shape=[jax.ShapeDtypeStruct((t, c), ACT_DTYPE),
                   jax.ShapeDtypeStruct((SSD_K, c), F32),
                   jax.ShapeDtypeStruct((1, c), F32)],
        grid=(c // cw,),
        in_specs=[pl.BlockSpec((t, cw), lambda j: (0, base + j)),
                  pl.BlockSpec((SSD_K, cw), lambda j: (0, j)),
                  pl.BlockSpec((1, cw), lambda j: (0, j)),
                  pl.BlockSpec((t, cw), lambda j: (0, j))],
        out_specs=[pl.BlockSpec((t, cw), lambda j: (0, j)),
                   pl.BlockSpec((SSD_K, cw), lambda j: (0, j)),
                   pl.BlockSpec((1, cw), lambda j: (0, j))],
        scratch_shapes=[pltpu.VMEM((t + 2 * CONV_HALO, cw), F32)] * 2,
        compiler_params=_params(("parallel",)),
    )(src, w, b, dout)


def _softplus(x):
    return jnp.maximum(x, 0.0) + jnp.log(1.0 + jnp.exp(-jnp.abs(x)))


def _ssd_chunk(xbc, dtraw, bias, alog, states):
    wd = states[0].shape[1]
    nj = wd // SSD_P
    inner = SSD_G * wd
    dt = _softplus(dtraw + bias)
    a = dt * (-jnp.exp(alog))
    li = lax.broadcasted_iota(jnp.int32, (SSD_Q, SSD_Q), 0)
    si = lax.broadcasted_iota(jnp.int32, (SSD_Q, SSD_Q), 1)
    causal = li >= si
    acs = _running_sum(a, causal.astype(F32))
    acs_t = acs.T
    a_last = acs[SSD_Q - 1:SSD_Q, :]
    grow = jnp.exp(acs)
    shrink = jnp.exp(a_last - acs)
    hrow = lax.broadcasted_iota(jnp.int32, (LANE, wd), 0)
    wcol = lax.broadcasted_iota(jnp.int32, (LANE, wd), 1)
    lane = lax.broadcasted_iota(jnp.int32, (1, LANE), 1)
    ys, snext = [], []
    for g in range(SSD_G):
        lo = (hrow - g * nj) * SSD_P
        head_lanes = jnp.logical_and(wcol >= lo, wcol < lo + SSD_P).astype(F32)
        xs = xbc[:, g * wd:(g + 1) * wd]
        bm = xbc[:, inner + g * SSD_N:inner + (g + 1) * SSD_N]
        cm = xbc[:, inner + (SSD_G + g) * SSD_N:inner + (SSD_G + g + 1) * SSD_N]
        xdt = xs * _dot(dt, head_lanes)
        grow_x = _spread(grow, head_lanes)
        y_off = _dot(cm, states[g]) * grow_x
        s_new = (states[g] * grow_x[SSD_Q - 1:SSD_Q, :]
                 + _dot(bm, xdt * _dot(shrink, head_lanes), TN))
        cb = _dot(cm, bm, NT)
        pieces = []
        for i in range(wd // LANE):
            xp = xdt[:, i * LANE:(i + 1) * LANE]
            acc = jnp.zeros((SSD_Q, LANE), F32)
            for hh in range(LANE // SSD_P):
                h = g * nj + i * (LANE // SSD_P) + hh
                decay = jnp.exp(jnp.where(causal, acs[:, h:h + 1] - acs_t[h:h + 1, :], NEG))
                keep = jnp.logical_and(lane >= hh * SSD_P, lane < (hh + 1) * SSD_P).astype(F32)
                acc = acc + _dot(cb * decay, xp * keep)
            pieces.append(acc)
        y_diag = pieces[0] if len(pieces) == 1 else jnp.concatenate(pieces, axis=1)
        ys.append(y_diag + y_off)
        snext.append(s_new)
    return ys, snext


def _ssd_specs(cdim, wd, rev, nc):
    ch = (lambda c: nc - 1 - c) if rev else (lambda c: c)
    full = lambda width: pl.BlockSpec((SSD_Q, width), lambda c: (ch(c), 0))
    vec = pl.BlockSpec((1, LANE), lambda c: (0, 0))
    st = pl.BlockSpec((1, SSD_G, SSD_N, wd), lambda c: (ch(c), 0, 0, 0))
    return full, vec, st


def _ssd_fwd(name, xbc, dtraw, bias, alog, inner):
    t, cdim = xbc.shape
    wd = inner // SSD_G
    nc = t // SSD_Q
    full, vec, st = _ssd_specs(cdim, wd, False, nc)

    def body(x_ref, r_ref, b_ref, a_ref, y_ref, st_ref, s_scr):
        @pl.when(pl.program_id(0) == 0)
        def _():
            s_scr[...] = jnp.zeros_like(s_scr)

        sprev = [s_scr[g] for g in range(SSD_G)]
        ys, snext = _ssd_chunk(x_ref[...].astype(F32), r_ref[...], b_ref[...], a_ref[...], sprev)
        for g in range(SSD_G):
            st_ref[0, g] = sprev[g]
            y_ref[:, g * wd:(g + 1) * wd] = ys[g]
            s_scr[g] = snext[g]

    return pl.pallas_call(
        body, name=name,
        out_shape=[jax.ShapeDtypeStruct((t, inner), F32),
                   jax.ShapeDtypeStruct((nc, SSD_G, SSD_N, wd), F32)],
        grid=(nc,),
        in_specs=[full(cdim), full(LANE), vec, vec],
        out_specs=[full(inner), st],
        scratch_shapes=[pltpu.VMEM((SSD_G, SSD_N, wd), F32)],
        compiler_params=_params(("arbitrary",)),
    )(xbc, dtraw, bias, alog)


def _ssd_bwd(name, xbc, dtraw, bias, alog, states, dy, dxs_extra):
    t, cdim = xbc.shape
    inner = dy.shape[1]
    wd = inner // SSD_G
    nc = t // SSD_Q
    full, vec, st = _ssd_specs(cdim, wd, True, nc)

    def body(x_ref, r_ref, b_ref, a_ref, st_ref, dy_ref, dx0_ref,
             dx_ref, dr_ref, db_ref, da_ref, ds_scr):
        first = pl.program_id(0) == 0

        @pl.when(first)
        def _():
            ds_scr[...] = jnp.zeros_like(ds_scr)

        sprev = [st_ref[0, g] for g in range(SSD_G)]
        _, vjp = jax.vjp(_ssd_chunk, x_ref[...].astype(F32), r_ref[...], b_ref[...], a_ref[...],
                         sprev)
        dyv = dy_ref[...]
        dys = [dyv[:, g * wd:(g + 1) * wd] for g in range(SSD_G)]
        dsn = [ds_scr[g] for g in range(SSD_G)]
        dx, dr, db, da, dsp = vjp((dys, dsn))
        dx_ref[:, :inner] = dx[:, :inner] + dx0_ref[...].astype(F32)
        dx_ref[:, inner:] = dx[:, inner:]
        dr_ref[...] = dr
        for g in range(SSD_G):
            ds_scr[g] = dsp[g]

        @pl.when(first)
        def _():
            db_ref[...] = db
            da_ref[...] = da

        @pl.when(jnp.logical_not(first))
        def _():
            db_ref[...] += db
            da_ref[...] += da

    return pl.pallas_call(
        body, name=name,
        out_shape=[jax.ShapeDtypeStruct((t, cdim), F32),
                   jax.ShapeDtypeStruct((t, LANE), F32),
                   jax.ShapeDtypeStruct((1, LANE), F32),
                   jax.ShapeDtypeStruct((1, LANE), F32)],
        grid=(nc,),
        in_specs=[full(cdim), full(LANE), vec, vec, st, full(inner), full(inner)],
        out_specs=[full(cdim), full(LANE), vec, vec],
        scratch_shapes=[pltpu.VMEM((SSD_G, SSD_N, wd), F32)],
        compiler_params=_params(("arbitrary",)),
    )(xbc, dtraw, bias, alog, states, dy, dxs_extra)


def _mix(o0, o1, o2, l0, l1, l2):
    m = lax.stop_gradient(jnp.maximum(jnp.maximum(l0, l1), l2))
    e0, e1, e2 = jnp.exp(l0 - m), jnp.exp(l1 - m), jnp.exp(l2 - m)
    return (e0 * o0 + e1 * o1 + e2 * o2) / (e0 + e1 + e2)


def _gate(y, xs, z, dexp, gain):
    v = (y + xs.astype(F32) * dexp) * _silu(z.astype(F32))
    return _rms(v, gain)


def _merge(ga, gs, ap, sp):
    return jax.nn.sigmoid(ga.astype(F32)) * ap + jax.nn.sigmoid(gs.astype(F32)) * sp


def _alibi_coefs(hp):
    n = hp * len(PATTERNS)
    slopes = np.exp2(-ALIBI_MAX_EXP * np.arange(1, n + 1, dtype=np.float32) / n).astype(np.float32)
    return [jnp.asarray(slopes[g * hp:(g + 1) * hp] * np.float32(d))
            for g, (_, d) in enumerate(PATTERNS)]


def _local_step(x, tgt, w, p, gw_=None, aw=None):
    t, d = x.shape
    dff = w["gu1t"].shape[0] // 2
    aw = w["abt"].shape[1] if aw is None else aw
    hp = aw // HD
    qkv = len(PATTERNS) * aw
    inner = p["ssd_norm"].shape[1]
    nh = p["dt_bias"].shape[1]
    gw_ = {} if gw_ is None else gw_
    gw = inner // SSD_G
    cdim = inner + 2 * SSD_G * SSD_N
    z_off, xbc_off = 3 * qkv, 3 * qkv + inner
    ga_off = xbc_off + cdim
    gs_off = ga_off + d
    hw = d // 2
    assert z_off % gw == 0 and xbc_off % LANE == 0 and ga_off % hw == 0 and gs_off % hw == 0
    assert (nh // SSD_G) * SSD_P == gw and hp % 2 == 0 and aw % LANE == 0 and nh <= LANE
    gdt = MXU_DTYPE

    row = lambda a, width, base=0: ("row", a, width, base)
    const = lambda a, width, base=0: ("const", a, width, base)

    def rms_fwd(name, xin, g):
        return _rw(name, lambda xv, gv: (_rms(xv, gv),), [row(xin, d), const(g, d)],
                   [((d,), ACT_DTYPE)])[0]

    def rms_bwd(name, xin, g, dh, dres):
        def fn(xv, gv, dhv, drv):
            _, vjp = jax.vjp(_rms, xv, gv)
            dx, dg = vjp(dhv.astype(F32))
            return drv + dx, dg
        return _rw(name, fn, [row(xin, d), const(g, d), row(dh, d), row(dres, d)],
                   [((d,), F32)], accs=[(1, d)])

    def ffn_fwd(tag, xin, g, key_gu, key_d):
        h = rms_fwd(tag + "_norm", xin, g)
        gate, up, a = _ffn_up(tag + "_up", h, w[key_gu])
        xo = _mm(tag + "_down", a, w[key_d], "nn", F32, res=xin, scale=0.5)
        return xo, (h, gate, up, a)

    def ffn_bwd(tag, xin, g, wgut, wd, saved, dxo, key_gu, key_d):
        h, gate, up, a = saved
        gw_[key_d] = _mm(tag + "_dwd", a, dxo, "tn", gdt, scale=0.5)
        dgu = _ffn_dact(tag + "_dact", dxo, wd, gate, up, 0.5)
        gw_[key_gu] = _mm(tag + "_dwgu", dgu, h, "tn", gdt)
        dh = _mm(tag + "_dh", dgu, wgut, "nn", F32)
        return rms_bwd(tag + "_dnorm", xin, g, dh, dxo)

    x1, ffn1_saved = ffn_fwd("ffn1", x, p["ffn1_norm"], "gu1t", "d1")
    if hasattr(w, "after_first_ffn"):
        w.after_first_ffn()
    h2 = rms_fwd("mix_norm", x1, p["mix_norm"])
    proj = _mm("in_proj", h2, w["maint"], "nt", ACT_DTYPE, cap_m=1024, cap_n=2944)
    dtraw = _mm("dt_proj", h2, w["dtt"], "nt", F32)

    coefs = _alibi_coefs(hp)
    qg2 = jnp.concatenate([p["q_norm"], p["q_norm"]], axis=1)
    kg2 = jnp.concatenate([p["k_norm"], p["k_norm"]], axis=1)
    pw = 2 * HD
    attn_bases = [[(off + gi * aw) // pw for off in (0, qkv, 2 * qkv)]
                  for gi in range(len(PATTERNS))]
    attn_o, attn_l = [], []
    for gi, (_, dil) in enumerate(PATTERNS):
        o, l = _attn_fwd(f"attn_fwd{gi}", proj, attn_bases[gi], qg2, kg2, coefs[gi], dil)
        attn_o.append(o)
        attn_l.append(l)
    ao = _rw("attn_mix", lambda *v: (_mix(*v),), [row(a, aw) for a in attn_o + attn_l],
             [((aw,), ACT_DTYPE)])[0]

    xbc = _conv_fwd("conv_fwd", proj, xbc_off // LANE, p["conv_w"], p["conv_b"])
    pad = lambda v: jnp.pad(v, ((0, 0), (0, LANE - nh)))
    bias_p, alog_p = pad(p["dt_bias"]), pad(p["a_log"])
    yssd, states = _ssd_fwd("ssd_fwd", xbc, dtraw, bias_p, alog_p, inner)
    dexp = jnp.repeat(p["d_skip"], SSD_P, axis=1)
    gate_ins = [row(yssd, gw), row(xbc, gw), row(proj, gw, z_off // gw),
                const(dexp, gw), const(p["ssd_norm"], gw)]
    yn = _rw("ssd_gate", lambda *v: (_gate(*v),), gate_ins, [((gw,), ACT_DTYPE)], ncb=SSD_G)[0]

    ap = _mm("attn_out", ao, w["abt"], "nt", F32)
    sp = _mm("ssd_out", yn, w["sb"], "nn", F32)
    merge_ins = [row(proj, hw, ga_off // hw), row(proj, hw, gs_off // hw), row(ap, hw), row(sp, hw)]
    mg = _rw("merge", lambda *v: (_merge(*v),), merge_ins, [((hw,), ACT_DTYPE)], ncb=2)[0]
    x2 = _mm("mix_out", mg, w["out"], "nn", F32, res=x1)
    x3, ffn2_saved = ffn_fwd("ffn2", x2, p["ffn2_norm"], "gu2t", "d2")

    def loss_fn(yv, tv):
        e = yv - tv
        return e * (1.0 / d), _colsum(e * e)
    dy, loss_vec = _rw("loss", loss_fn, [row(x3, d), row(tgt, d)], [((d,), F32)], accs=[(1, d)])

    gp = {}
    dx2, gp["ffn2_norm"] = ffn_bwd(
        "ffn2", x2, p["ffn2_norm"], w["gu2t"], w["d2"], ffn2_saved, dy, "gu2t", "d2")
    dmg = _mm("d_merge", dx2, w["out"], "nt", ACT_DTYPE)
    gw_["out"] = _mm("dw_out", mg, dx2, "tn", gdt)

    def merge_bwd(gav, gsv, apv, spv, dv):
        _, vjp = jax.vjp(_merge, gav, gsv, apv, spv)
        return vjp(dv.astype(F32))
    dga, dgs, dap, dsp = _rw("d_merge_gate", merge_bwd, merge_ins + [row(dmg, hw)],
                             [((hw,), ACT_DTYPE)] * 4, ncb=2)
    gw_["abt"] = _mm("dw_ab", dap, ao, "tn", gdt)
    dao = _mm("d_attn_o", dap, w["abt"], "nn", F32)
    gw_["sb"] = _mm("dw_sb", yn, dsp, "tn", gdt)
    dyn = _mm("d_ssd_y", dsp, w["sb"], "nt", F32)

    def gate_bwd(yv, xv, zv, dev, gv, dv):
        _, vjp = jax.vjp(_gate, yv, xv, zv, dev, gv)
        return vjp(dv)
    dyssd, dxs_gate, dz, ddexp, gp["ssd_norm"] = _rw(
        "d_ssd_gate", gate_bwd, gate_ins + [row(dyn, gw)],
        [((gw,), F32), ((gw,), F32), ((gw,), ACT_DTYPE)], accs=[(1, gw), (1, gw)], ncb=SSD_G)
    gp["d_skip"] = ddexp.reshape(nh, SSD_P).sum(axis=1).reshape(1, nh)

    dxbc, ddtraw, dbias, dalog = _ssd_bwd("ssd_bwd", xbc, dtraw, bias_p, alog_p, states,
                                          dyssd, dxs_gate)
    gp["dt_bias"], gp["a_log"] = dbias[:, :nh], dalog[:, :nh]
    du, gp["conv_w"], gp["conv_b"] = _conv_bwd("conv_bwd", proj, xbc_off // LANE,
                                               p["conv_w"], p["conv_b"], dxbc)

    def mix_bwd(*v):
        _, vjp = jax.vjp(_mix, *v[:6])
        return vjp(v[6])
    dmix = _rw("d_attn_mix", mix_bwd, [row(a, aw) for a in attn_o + attn_l] + [row(dao, aw)],
               [((aw,), F32)] * 6)
    dq, dk, dv = [], [], []
    dqg = dkg = None
    for gi, (_, dil) in enumerate(PATTERNS):
        r = _attn_bwd(f"attn_bwd{gi}", proj, attn_bases[gi], qg2, kg2, coefs[gi], dil,
                      dmix[gi], dmix[3 + gi])
        dq.append(r[0])
        dk.append(r[1])
        dv.append(r[2])
        dqg = r[3] if dqg is None else dqg + r[3]
        dkg = r[4] if dkg is None else dkg + r[4]
    gp["q_norm"] = dqg[:, :HD] + dqg[:, HD:]
    gp["k_norm"] = dkg[:, :HD] + dkg[:, HD:]

    segs = dq + dk + dv + [dz, du, dga, dgs]
    gw_["maint"] = _mm("dw_in", segs, h2, "tn", gdt)
    gw_["dtt"] = _mm("dw_dt", ddtraw, h2, "tn", gdt)
    dh2 = _mm("d_h2_main", segs, w["maint"], "nn", F32)
    dh2 = _mm("d_h2_dt", ddtraw, w["dtt"], "nn", F32, res=dh2)
    dx1, gp["mix_norm"] = rms_bwd("d_mix_norm", x1, p["mix_norm"], dh2, dx2)
    dx0, gp["ffn1_norm"] = ffn_bwd(
        "ffn1", x, p["ffn1_norm"], w["gu1t"], w["d1"], ffn1_saved, dx1, "gu1t", "d1")
    return loss_vec, dx0, gw_, gp


MESH = pl.DeviceIdType.MESH
HBM_SPEC = pl.BlockSpec(memory_space=pltpu.HBM)


def _mesh_pos():
    return lax.axis_index("x"), lax.axis_index("y"), lax.axis_index("c")


def _flip(pos, k):
    x, y, c = pos
    return (1 - x if k & 4 else x, 1 - y if k & 2 else y, 1 - c if k & 1 else c)


def _dev_index(pos):
    return 4 * pos[0] + 2 * pos[1] + pos[2]


def _rows_of(ref, base, stride, rows, pos):
    start = pl.multiple_of(base + stride * _dev_index(pos), ROW_ALIGN)
    return ref.at[pl.ds(start, rows)]


def _gather(name, shards, dests, out_shapes):
    n = len(shards)
    n_out = len(out_shapes)

    def body(*refs):
        x_refs = refs[:n]
        o_refs = refs[n:n + n_out]
        send_sems, recv_sems, local_sems = refs[n + n_out:]
        me = _mesh_pos()
        sibling = _flip(me, 1)
        chips = [_flip(me, 4), _flip(me, 2), _flip(me, 6)]

        def slot(i, block):
            k_out, base, stride = dests[i]
            return _rows_of(o_refs[k_out], base, stride, shards[i].shape[0], block)

        def copy(i, k, block, to, src=None):
            dst = slot(i, block)
            return pltpu.make_async_remote_copy(
                src_ref=dst if src is None else src, dst_ref=dst,
                send_sem=send_sems.at[7 * i + k], recv_sem=recv_sems.at[7 * i + k],
                device_id=to, device_id_type=MESH)

        mine = [pltpu.make_async_copy(x_refs[i], slot(i, me), local_sems.at[i]) for i in range(n)]
        for cp in mine:
            cp.start()
        first = []
        for i in range(n):
            first.append(copy(i, 0, me, sibling, src=x_refs[i]))
            first += [copy(i, 1 + j, me, chip, src=x_refs[i]) for j, chip in enumerate(chips)]
        for cp in first:
            cp.start()
        passed = []
        for j, chip in enumerate(chips):
            for i in range(n):
                copy(i, 1 + j, chip, me).wait_recv()
                fwd = copy(i, 4 + j, chip, sibling)
                fwd.start()
                passed.append(fwd)
        for i in range(n):
            copy(i, 0, sibling, me).wait_recv()
            for j, chip in enumerate(chips):
                copy(i, 4 + j, _flip(chip, 1), me).wait_recv()
        for cp in first + passed:
            cp.wait_send()
        for cp in mine:
            cp.wait()

    return pl.pallas_call(
        body, name=name,
        out_shape=[jax.ShapeDtypeStruct(s, dt) for s, dt in out_shapes],
        in_specs=[HBM_SPEC] * n, out_specs=[HBM_SPEC] * n_out,
        scratch_shapes=[pltpu.SemaphoreType.DMA((7 * n,)), pltpu.SemaphoreType.DMA((7 * n,)),
                        pltpu.SemaphoreType.DMA((n,))],
    )(*shards)


SEM_SPEC =pl.BlockSpec(memory_space=pltpu.SEMAPHORE)
SIDE_EFFECT = pltpu.SideEffectType.DATAFLOW_SIDE_EFFECTING


def _split_refs(plan, srcs, lands, i, src_for, land_from):
    si, sbase, sstride, li, lbase, lstride, rows = plan[i]
    return (_rows_of(srcs[si], sbase, sstride, rows, src_for),
            _rows_of(lands[li], lbase, lstride, rows, land_from))


ALL_PEERS = tuple(range(1, NDEV))
SAME_CORE_AND_SIBLING = (1, 4, 2, 6)
OTHER_CHIPS = (4, 2, 6)


def _split_start(name, srcs, lands, plan, after=(), relations=ALL_PEERS):
    ns, nl, n = len(srcs), len(lands), len(plan)

    def body(*refs):
        s_refs = refs[:ns]
        l_refs = refs[ns:ns + nl]
        send_sems, recv_sems = refs[ns + nl + len(after):ns + nl + len(after) + 2]
        local_sems = refs[ns + nl + len(after) + 2]
        token = refs[ns + nl + len(after) + 3 + ns + nl]
        me = _mesh_pos()
        for i in range(n):
            src, dst = _split_refs(plan, s_refs, l_refs, i, me, me)
            pltpu.make_async_copy(src, dst, local_sems.at[i]).start()
        for k in relations:
            peer = _flip(me, k)
            for i in range(n):
                src, dst = _split_refs(plan, s_refs, l_refs, i, peer, me)
                pltpu.make_async_remote_copy(
                    src_ref=src, dst_ref=dst,
                    send_sem=send_sems.at[7 * i + k - 1], recv_sem=recv_sems.at[7 * i + k - 1],
                    device_id=peer, device_id_type=MESH).start()
        token[...] = jnp.zeros_like(token)

    hbm = lambda a: pltpu.HBM(a.shape, a.dtype)
    out_shape = ((pltpu.SemaphoreType.DMA((7 * n,)), pltpu.SemaphoreType.DMA((7 * n,)),
                  pltpu.SemaphoreType.DMA((n,)))
                 + tuple(hbm(a) for a in srcs) + tuple(hbm(a) for a in lands)
                 + (jax.ShapeDtypeStruct((8, LANE), F32),))
    out = pl.pallas_call(
        body, name=name, out_shape=out_shape,
        in_specs=[HBM_SPEC] * (ns + nl) + [ANY_SPEC] * len(after),
        out_specs=(SEM_SPEC, SEM_SPEC, SEM_SPEC) + (HBM_SPEC,) * (ns + nl)
        + (pl.BlockSpec(memory_space=pltpu.VMEM),),
        input_output_aliases={i: 3 + i for i in range(ns + nl)},
        compiler_params=pltpu.CompilerParams(has_side_effects=SIDE_EFFECT),
    )(*[pltpu.with_memory_space_constraint(a, pltpu.HBM) for a in tuple(srcs) + tuple(lands)],
      *after)
    _Order.tokens.append(out[-1])
    return out[0], out[1], out[2], out[3:3 + ns], out[3 + ns:3 + ns + nl]


def _split_wait(name, started, plan, relations=ALL_PEERS):
    send_sems, recv_sems, local_sems, srcs, lands = started
    ns, nl, n = len(srcs), len(lands), len(plan)
    after = [_Order.last] if _Order.last is not None else []

    def body(*refs):
        s_refs = refs[:ns]
        l_refs = refs[ns:ns + nl]
        send_sems, recv_sems, local_sems = refs[ns + nl:ns + nl + 3]
        me = _mesh_pos()
        for i in range(n):
            src, dst = _split_refs(plan, s_refs, l_refs, i, me, me)
            pltpu.make_async_copy(src, dst, local_sems.at[i]).wait()
        for k in relations:
            peer = _flip(me, k)
            for i in range(n):
                src, dst = _split_refs(plan, s_refs, l_refs, i, peer, peer)
                cp = pltpu.make_async_remote_copy(
                    src_ref=src, dst_ref=dst,
                    send_sem=send_sems.at[7 * i + k - 1], recv_sem=recv_sems.at[7 * i + k - 1],
                    device_id=peer, device_id_type=MESH)
                cp.wait_send()
                cp.wait_recv()

    hbm = lambda a: pltpu.HBM(a.shape, a.dtype)
    out = pl.pallas_call(
        body, name=name,
        out_shape=tuple(hbm(a) for a in srcs) + tuple(hbm(a) for a in lands),
        in_specs=[HBM_SPEC] * (ns + nl) + [SEM_SPEC] * 3 + [ANY_SPEC] * len(after),
        out_specs=(HBM_SPEC,) * (ns + nl),
        input_output_aliases={i: i for i in range(ns + nl)},
        compiler_params=pltpu.CompilerParams(has_side_effects=SIDE_EFFECT),
    )(*srcs, *lands, send_sems, recv_sems, local_sems, *after)
    return list(out[ns:])


def _forward_refs(plan, lands, i, block):
    _, _, _, li, lbase, lstride, rows = plan[i]
    return _rows_of(lands[li], lbase, lstride, rows, block)


def _forward_start(name, lands, plan):
    nl, n = len(lands), len(plan)

    def body(*refs):
        l_refs = refs[:nl]
        send_sems, recv_sems = refs[nl:nl + 2]
        token = refs[nl + 2 + nl]
        me = _mesh_pos()
        for j, kc in enumerate(OTHER_CHIPS):
            for i in range(n):
                rows = _forward_refs(plan, l_refs, i, _flip(me, kc))
                pltpu.make_async_remote_copy(
                    src_ref=rows, dst_ref=rows,
                    send_sem=send_sems.at[3 * i + j], recv_sem=recv_sems.at[3 * i + j],
                    device_id=_flip(me, 1), device_id_type=MESH).start()
        token[...] = jnp.zeros_like(token)

    hbm = lambda a: pltpu.HBM(a.shape, a.dtype)
    out = pl.pallas_call(
        body, name=name,
        out_shape=((pltpu.SemaphoreType.DMA((3 * n,)), pltpu.SemaphoreType.DMA((3 * n,)))
                   + tuple(hbm(a) for a in lands) + (jax.ShapeDtypeStruct((8, LANE), F32),)),
        in_specs=[HBM_SPEC] * nl,
        out_specs=(SEM_SPEC, SEM_SPEC) + (HBM_SPEC,) * nl
        + (pl.BlockSpec(memory_space=pltpu.VMEM),),
        input_output_aliases={i: 2 + i for i in range(nl)},
        compiler_params=pltpu.CompilerParams(has_side_effects=SIDE_EFFECT),
    )(*[pltpu.with_memory_space_constraint(a, pltpu.HBM) for a in lands])
    _Order.tokens.append(out[-1])
    return out[0], out[1], out[2:2 + nl]


def _forward_wait(name, started, plan):
    send_sems, recv_sems, lands = started
    nl, n = len(lands), len(plan)
    after = [_Order.last] if _Order.last is not None else []

    def body(*refs):
        l_refs = refs[:nl]
        send_sems, recv_sems = refs[nl:nl + 2]
        me = _mesh_pos()
        for j, kc in enumerate(OTHER_CHIPS):
            for i in range(n):
                sent = _forward_refs(plan, l_refs, i, _flip(me, kc))
                came = _forward_refs(plan, l_refs, i, _flip(_flip(me, 1), kc))
                cp = pltpu.make_async_remote_copy(
                    src_ref=sent, dst_ref=came,
                    send_sem=send_sems.at[3 * i + j], recv_sem=recv_sems.at[3 * i + j],
                    device_id=_flip(me, 1), device_id_type=MESH)
                cp.wait_send()
                cp.wait_recv()

    hbm = lambda a: pltpu.HBM(a.shape, a.dtype)
    out = pl.pallas_call(
        body, name=name, out_shape=tuple(hbm(a) for a in lands),
        in_specs=[HBM_SPEC] * nl + [SEM_SPEC] * 2 + [ANY_SPEC] * len(after),
        out_specs=(HBM_SPEC,) * nl,
        input_output_aliases={i: i for i in range(nl)},
        compiler_params=pltpu.CompilerParams(has_side_effects=SIDE_EFFECT),
    )(*lands, send_sems, recv_sems, *after)
    return list(out)


def _regroup_rows(name, padded, r, rp, lo, hi):
    d = padded.shape[1]
    pack = 4 // padded.dtype.itemsize
    assert r % pack == 0 and rp % ROW_ALIGN == 0 and lo % (8 * pack) == 0 and hi % (8 * pack) == 0
    r2, rp2, lo2, hi2 = r // pack, rp // pack, lo // pack, hi // pack
    u32 = jnp.uint32

    def body(x_ref, main_ref, cut_ref):
        x = pltpu.bitcast(x_ref[...], u32)
        joined = jnp.concatenate([x[rp2 * j:rp2 * j + r2] for j in range(NDEV)], axis=0)
        main = jnp.concatenate([joined[:lo2], joined[hi2:]], axis=0)
        cut = jnp.concatenate([joined[lo2:hi2], jnp.zeros((LANE // pack - (hi2 - lo2), LANE), u32)],
                              axis=0)
        main_ref[...] = pltpu.bitcast(main, padded.dtype)
        cut_ref[...] = pltpu.bitcast(cut, padded.dtype)

    return pl.pallas_call(
        body, name=name,
        out_shape=[jax.ShapeDtypeStruct((NDEV * r - (hi - lo), d), padded.dtype),
                   jax.ShapeDtypeStruct((LANE, d), padded.dtype)],
        grid=(d // LANE,),
        in_specs=[pl.BlockSpec((NDEV * rp, LANE), lambda i: (0, i))],
        out_specs=[pl.BlockSpec((NDEV * r - (hi - lo), LANE), lambda i: (0, i)),
                   pl.BlockSpec((LANE, LANE), lambda i: (0, i))],
        compiler_params=_params(("parallel",)),
    )(padded)


def _ungroup_rows(name, main, cut, r, rp, lo, hi):
    d = main.shape[1]
    pack = 4 // main.dtype.itemsize
    r2, rp2, lo2, hi2 = r // pack, rp // pack, lo // pack, hi // pack
    u32 = jnp.uint32

    def body(main_ref, cut_ref, o_ref):
        m = pltpu.bitcast(main_ref[...], u32)
        c = pltpu.bitcast(cut_ref[...], u32)
        joined = jnp.concatenate([m[:lo2], c[:hi2 - lo2], m[lo2:]], axis=0)
        zeros = jnp.zeros((rp2 - r2, LANE), u32)
        parts = []
        for j in range(NDEV):
            parts += [joined[r2 * j:r2 * (j + 1)], zeros]
        o_ref[...] = pltpu.bitcast(jnp.concatenate(parts, axis=0), main.dtype)

    return pl.pallas_call(
        body, name=name,
        out_shape=jax.ShapeDtypeStruct((NDEV * rp, d), main.dtype),
        grid=(d // LANE,),
        in_specs=[pl.BlockSpec((main.shape[0], LANE), lambda i: (0, i)),
                  pl.BlockSpec((LANE, LANE), lambda i: (0, i))],
        out_specs=pl.BlockSpec((NDEV * rp, LANE), lambda i: (0, i)),
        compiler_params=_params(("parallel",)),
    )(main, cut)


def _sum_slabs(name, a):
    s, r, c = a.shape

    def body(a_ref, o_ref):
        acc = a_ref[0].astype(F32)
        for i in range(1, s):
            acc = acc + a_ref[i].astype(F32)
        o_ref[...] = acc

    return pl.pallas_call(body, name=name, out_shape=jax.ShapeDtypeStruct((r, c), F32))(a)


def _adamw_update(g, w, m, v):
    mn = ADAM_B1 * m + (1.0 - ADAM_B1) * g
    vn = ADAM_B2 * v + (1.0 - ADAM_B2) * (g * g)
    m_hat = mn / (1.0 - ADAM_B1 ** ADAM_STEP)
    v_hat = vn / (1.0 - ADAM_B2 ** ADAM_STEP)
    delta = -ADAM_LR * (m_hat / (jnp.sqrt(v_hat) + ADAM_EPS) + ADAM_WD * w)
    return delta, mn, vn


def _adamw(name, gsrc, w, m, v, transposed=False, tr=256):
    s = gsrc.shape[0]
    lead = w.ndim == 3
    r, c = w.shape[-2:]
    step = LANE if transposed else 8
    tr = max(t for t in range(step, min(tr, r) + 1, step) if r % t == 0)

    def body(g_ref, w_ref, m_ref, v_ref, go_ref, d_ref, mo_ref, vo_ref):
        g = g_ref[0].astype(F32)
        for i in range(1, s):
            g = g + g_ref[i].astype(F32)
        if transposed:
            g = g.T[:, :c]
        delta, mn, vn = _adamw_update(g, w_ref[...], m_ref[...], v_ref[...])
        go_ref[...] = g
        d_ref[...] = delta
        mo_ref[...] = mn
        vo_ref[...] = vn

    if lead:
        blk = pl.BlockSpec((None, tr, c), lambda i: (0, i, 0))
    else:
        blk = pl.BlockSpec((tr, c), lambda i: (i, 0))
    if transposed:
        g_spec = pl.BlockSpec((s, gsrc.shape[1], tr), lambda i: (0, 0, i))
    else:
        g_spec = pl.BlockSpec((s, tr, c), lambda i: (0, i, 0))
    return pl.pallas_call(
        body, name=name, out_shape=[jax.ShapeDtypeStruct(w.shape, F32)] * 4,
        grid=(r // tr,),
        in_specs=[g_spec, blk, blk, blk], out_specs=[blk] * 4,
        compiler_params=_params(("parallel",)),
    )(gsrc, w, m, v)


REPLICATED = ("ffn1_norm", "mix_norm", "q_norm", "k_norm", "conv_b", "dt_bias", "a_log",
              "d_skip", "ssd_norm", "ffn2_norm")
ALL_WEIGHTS = ("ffn1_norm", "ffn1_w_gate", "ffn1_w_up", "ffn1_w_down", "mix_norm", "w_in",
               "q_norm", "k_norm", "conv_w", "conv_b", "dt_bias", "a_log", "d_skip", "ssd_norm",
               "w_attn_branch", "w_ssd_branch", "w_out", "ffn2_norm", "ffn2_w_gate", "ffn2_w_up",
               "ffn2_w_down")
BIG = (("ffn1_w_gate", True, "gu1t", 0), ("ffn1_w_up", True, "gu1t", 1),
       ("ffn1_w_down", False, "d1", 0), ("w_in", True, "wint", 0),
       ("w_attn_branch", True, "abt", 0), ("w_ssd_branch", False, "sb", 0),
       ("w_out", False, "out", 0),
       ("ffn2_w_gate", True, "gu2t", 0), ("ffn2_w_up", True, "gu2t", 1),
       ("ffn2_w_down", False, "d2", 0))


def _nrows(shape, cols):
    return -(-math.prod(shape) // cols)


def _pack_rows(arrs, cols, row_tile):
    parts = []
    for a in arrs:
        flat = a.reshape(-1)
        nr = -(-flat.shape[0] // cols)
        parts.append(jnp.pad(flat, (0, nr * cols - flat.shape[0])).reshape(nr, cols))
    out = jnp.concatenate(parts, axis=0)
    return jnp.pad(out, ((0, _round_up(out.shape[0], row_tile) - out.shape[0]), (0, 0)))


def _unpack_rows(packed, shapes):
    cols = packed.shape[-1]
    out, r0 = [], 0
    for sh in shapes:
        nr = _nrows(sh, cols)
        out.append(packed[r0:r0 + nr].reshape(-1)[:math.prod(sh)].reshape(tuple(sh)))
        r0 += nr
    return out


def kernel(x, ffn1_norm, ffn1_w_gate, ffn1_w_up, ffn1_w_down, mix_norm, w_in, q_norm, k_norm, conv_w, conv_b, dt_bias, a_log, d_skip, ssd_norm, w_attn_branch, w_ssd_branch, w_out, ffn2_norm, ffn2_w_gate, ffn2_w_up, ffn2_w_down, loss_target, m_ffn1_norm, m_ffn1_w_gate, m_ffn1_w_up, m_ffn1_w_down, m_mix_norm, m_w_in, m_q_norm, m_k_norm, m_conv_w, m_conv_b, m_dt_bias, m_a_log, m_d_skip, m_ssd_norm, m_w_attn_branch, m_w_ssd_branch, m_w_out, m_ffn2_norm, m_ffn2_w_gate, m_ffn2_w_up, m_ffn2_w_down, v_ffn1_norm, v_ffn1_w_gate, v_ffn1_w_up, v_ffn1_w_down, v_mix_norm, v_w_in, v_q_norm, v_k_norm, v_conv_w, v_conv_b, v_dt_bias, v_a_log, v_d_skip, v_ssd_norm, v_w_attn_branch, v_w_ssd_branch, v_w_out, v_ffn2_norm, v_ffn2_w_gate, v_ffn2_w_up, v_ffn2_w_down):
    given = dict(locals())
    wts = {n: given[n] for n in ALL_WEIGHTS}
    mom = {n: given["m_" + n] for n in ALL_WEIGHTS}
    var = {n: given["v_" + n] for n in ALL_WEIGHTS}
    d = x.shape[-1]
    nh = dt_bias.shape[1]
    my = _dev_index(_mesh_pos())

    def row_form(n, col_sharded):
        a = wts[n][0].T if col_sharded else wts[n][0]
        a = jnp.pad(a, ((0, _round_up(a.shape[0], ROW_ALIGN) - a.shape[0]), (0, 0)))
        return a.astype(MXU_DTYPE)

    _Order.tokens, _Order.last = [], None
    shard = {n: row_form(n, cs) for n, cs, _, _ in BIG}
    entries = {buf: [e for e in BIG if e[2] == buf] for buf in dict.fromkeys(e[2] for e in BIG)}

    def buf_shape(buf):
        r, c = shard[entries[buf][0][0]].shape
        return (len(entries[buf]) * NDEV * r, c)

    def gather_plan(bufs):
        srcs, lands, plan = [], [], []
        for li, buf in enumerate(bufs):
            lands.append(lax.empty(buf_shape(buf), MXU_DTYPE))
            for n, _, _, pos in entries[buf]:
                r = shard[n].shape[0]
                plan.append((len(srcs), 0, 0, li, pos * NDEV * r, r, r))
                srcs.append(shard[n])
        return srcs, lands, plan

    def scatter_plan(bufs, grads):
        srcs, lands, plan, names = [], [], [], []
        for si, buf in enumerate(bufs):
            srcs.append(grads[buf])
            for n, _, _, pos in entries[buf]:
                r, c = shard[n].shape
                plan.append((si, pos * NDEV * r, r, len(lands), 0, r, r))
                lands.append(lax.empty((NDEV * r, c), MXU_DTYPE))
                names.append(n)
        return srcs, lands, plan, names

    first_bufs = ("gu1t", "d1")
    shards, dests, out_shapes = [], [], []
    for bi, buf in enumerate(first_bufs):
        out_shapes.append((buf_shape(buf), MXU_DTYPE))
        for n, _, _, pos in entries[buf]:
            r = shard[n].shape[0]
            shards.append(shard[n])
            dests.append((bi, pos * NDEV * r, r))
    conv_rows = _pack_rows([conv_w[0]], LANE, ROW_ALIGN)
    shards.append(conv_rows)
    dests.append((len(first_bufs), 0, conv_rows.shape[0]))
    out_shapes.append(((NDEV * conv_rows.shape[0], LANE), F32))
    gathered = _gather("gather_first", shards, dests, out_shapes)

    in_cols = w_in.shape[2]
    in_pad = _round_up(in_cols, ROW_ALIGN)
    dt_off = NDEV * in_cols - 2 * d - nh
    second_bufs = ("wint",)
    third_bufs = ("abt", "sb", "out", "gu2t", "d2")
    plan2 = gather_plan(second_bufs)
    started2 = _split_start("gather_in_start", *plan2, after=[gathered[0]],
                            relations=SAME_CORE_AND_SIBLING)
    forwarded, started3 = [], []

    class Weights(dict):
        def after_first_ffn(self):
            lands = _split_wait("gather_in_wait", started2, plan2[2],
                                relations=SAME_CORE_AND_SIBLING)
            forwarded.append(_forward_start("gather_in_forward", lands, plan2[2]))

        def __missing__(self, key):
            if key in ("maint", "dtt"):
                wint = _forward_wait("gather_in_arrive", forwarded[0], plan2[2])[0]
                plan3 = gather_plan(third_bufs)
                started3.append((_split_start("gather_rest_start", *plan3, after=[wint]), plan3[2]))
                self["maint"], self["dtt"] = _regroup_rows(
                    "regroup_w_in", wint, in_cols, in_pad, dt_off, dt_off + nh)
            else:
                st, plan = started3[0]
                for buf, a in zip(third_bufs, _split_wait("gather_rest_wait", st, plan)):
                    self[buf] = a
            return self[key]

    w = Weights(gu1t=gathered[0], d1=gathered[1])
    p = {n: wts[n] for n in REPLICATED}
    conv_all = gathered[-1].reshape(NDEV, conv_rows.shape[0] * LANE)[:, :math.prod(conv_w.shape[1:])]
    p["conv_w"] = (conv_all.reshape((NDEV,) + conv_w.shape[1:]).transpose(1, 0, 2)
                   .reshape(conv_w.shape[1], NDEV * conv_w.shape[2]))

    groups = (("scatter_late", ("gu2t", "d2", "out", "abt", "sb")),
              ("scatter_in", ("maint", "dtt")),
              ("scatter_first", ("gu1t", "d1")))
    in_flight = []

    class Grads(dict):
        def __setitem__(self, key, value):
            dict.__setitem__(self, key, value)
            for tag, need in groups:
                if key in need and all(k in self for k in need):
                    if tag == "scatter_in":
                        gwin = _ungroup_rows("ungroup_w_in", self["maint"], self["dtt"],
                                             in_cols, in_pad, dt_off, dt_off + nh)
                        bufs, grads = ("wint",), {"wint": gwin}
                    else:
                        bufs, grads = need, self
                    srcs, lands, plan, names = scatter_plan(bufs, grads)
                    in_flight.append((tag, _split_start(tag + "_start", srcs, lands, plan),
                                      plan, names))

    loss_vec, dx, gw, gp = _local_step(x[0], loss_target[0], w, p, Grads(),
                                       aw=w_attn_branch.shape[1])

    small_names = REPLICATED + ("conv_w",)
    small_shapes = [gp[n].shape for n in small_names]
    small = _pack_rows([gp[n] for n in small_names], LANE, ROW_ALIGN)
    small_plan = [(0, 0, 0, 0, 0, small.shape[0], small.shape[0])]
    small_started = _split_start("gather_small_start", [small],
                                 [lax.empty((NDEV * small.shape[0], LANE), F32)], small_plan)

    outs = [{}, {}, {}, {}]
    col_sharded_of = {n: cs for n, cs, _, _ in BIG}
    for tag, started, plan, names in in_flight:
        for n, rv in zip(names, _split_wait(tag + "_wait", started, plan)):
            rv = rv.reshape(NDEV, shard[n].shape[0], shard[n].shape[1])
            if col_sharded_of[n] and rv.shape[1] == wts[n].shape[2]:
                res = _adamw("adamw_" + n, rv, wts[n][0].T, mom[n][0].T, var[n][0].T)
                res = [a.T for a in res]
            else:
                res = _adamw("adamw_" + n, rv, wts[n][0], mom[n][0], var[n][0],
                             transposed=col_sharded_of[n])
            _Order.done(res)
            for k in range(4):
                outs[k][n] = res[k][None]

    small_all = _split_wait("gather_small_wait", small_started, small_plan)[0]
    small_all = small_all.reshape(NDEV, small.shape[0], LANE)
    small_g = _unpack_rows(_sum_slabs("sum_small_grads", small_all), small_shapes)
    small_g = dict(zip(small_names, small_g))
    cs = conv_w.shape[2]
    small_g["conv_w"] = lax.dynamic_slice_in_dim(small_g["conv_w"], my * cs, cs, axis=1)
    small_shard_shapes = [wts[n].shape[-2:] for n in small_names]
    sg = _pack_rows([small_g[n] for n in small_names], LANE, 8)
    sw = _pack_rows([wts[n] for n in small_names], LANE, 8)
    sm = _pack_rows([mom[n] for n in small_names], LANE, 8)
    sv = _pack_rows([var[n] for n in small_names], LANE, 8)
    res_small = _adamw("adamw_small", sg[None], sw, sm, sv, tr=sg.shape[0])
    for k in range(4):
        for n, a in zip(small_names, _unpack_rows(res_small[k], small_shard_shapes)):
            outs[k][n] = a.reshape(wts[n].shape)

    loss = lax.psum(0.5 * jnp.sum(loss_vec) / d, ("x", "y", "c"))
    result = [loss, dx[None]]
    for k in range(4):
        result += [outs[k][n] for n in ALL_WEIGHTS]
    return tuple(result)
```

```python
import functools
import math

import numpy as np
import jax
import jax.numpy as jnp
from jax import lax
from jax.experimental import pallas as pl
from jax.experimental.pallas import tpu as pltpu

F32 = jnp.float32
BF16 = jnp.bfloat16
MXU_DTYPE = BF16
ACT_DTYPE = BF16

NDEV = 8
EPS = 1e-6
HD = 64
QB = 128
PATTERNS = ((128, 1), (512, 4), (2048, 16))
ALIBI_MAX_EXP = 8.0
SSD_P = 64
SSD_N = 128
SSD_G = 4
SSD_Q = 128
SSD_K = 4
NEG = -1e30
LANE = 128
ROW_ALIGN = 16
VMEM_LIMIT = 56 * 1024 * 1024
TILE_CAP = 11 * LANE
ROW_TILE_CAP = 8 * LANE
WIDE_N_CAP = 23 * LANE

ADAM_LR, ADAM_B1, ADAM_B2, ADAM_EPS, ADAM_WD, ADAM_STEP = 0.001, 0.9, 0.999, 1e-8, 0.01, 10

NN = (((1,), (0,)), ((), ()))
NT = (((1,), (1,)), ((), ()))
TN = (((0,), (0,)), ((), ()))


BNN = (((2,), (1,)), ((0,), (0,)))
BNT = (((2,), (2,)), ((0,), (0,)))
BTN = (((1,), (1,)), ((0,), (0,)))
_DOT_GRADS = {
    NN: (("g", "b", NT), ("a", "g", TN)),
    NT: (("g", "b", NN), ("g", "a", TN)),
    TN: (("b", "g", NT), ("a", "g", NN)),
    BNT: (("g", "b", BNN), ("g", "a", BTN)),
    BTN: (("b", "g", BNT), ("a", "g", BNN)),
}


def _mxu(a, b, dims):
    return lax.dot_general(a.astype(MXU_DTYPE), b.astype(MXU_DTYPE), dims,
                           preferred_element_type=F32)


@functools.partial(jax.custom_vjp, nondiff_argnums=(2,))
def _dot_vjp(a, b, dims):
    return _mxu(a, b, dims)


def _dot_vjp_fwd(a, b, dims):
    return _mxu(a, b, dims), (a.astype(MXU_DTYPE), b.astype(MXU_DTYPE))


def _dot_vjp_bwd(dims, res, g):
    ops = {"a": res[0], "b": res[1], "g": g}
    (x1, y1, d1), (x2, y2, d2) = _DOT_GRADS[dims]
    return _mxu(ops[x1], ops[y1], d1), _mxu(ops[x2], ops[y2], d2)


_dot_vjp.defvjp(_dot_vjp_fwd, _dot_vjp_bwd)


def _dot(a, b, dims=NN):
    return _dot_vjp(a, b, dims)


def _split3(a):
    hi = a.astype(BF16)
    r = a - hi.astype(F32)
    mid = r.astype(BF16)
    lo = (r - mid.astype(F32)).astype(BF16)
    return hi, mid, lo


def _dot3(a, b, dims=NN, split=0):
    if split == 0:
        bb = b.astype(BF16)
        parts = [lax.dot_general(s, bb, dims, preferred_element_type=F32) for s in _split3(a)]
    else:
        aa = a.astype(BF16)
        parts = [lax.dot_general(aa, s, dims, preferred_element_type=F32) for s in _split3(b)]
    return parts[0] + parts[1] + parts[2]


@jax.custom_vjp
def _spread(v, e):
    return _dot3(v, e)


def _spread_fwd(v, e):
    return _dot3(v, e), e


def _spread_bwd(e, g):
    return _dot3(g, e, NT), jnp.zeros_like(e)


_spread.defvjp(_spread_fwd, _spread_bwd)


@jax.custom_vjp
def _running_sum(a, lower):
    return _dot3(lower, a, NN, split=1)


def _running_sum_fwd(a, lower):
    return _dot3(lower, a, NN, split=1), lower


def _running_sum_bwd(lower, g):
    return _dot3(lower, g, TN, split=1), jnp.zeros_like(lower)


_running_sum.defvjp(_running_sum_fwd, _running_sum_bwd)


def _tile(n, cap):
    if n <= cap:
        return n
    best = None
    for t in range(LANE, cap + 1, LANE):
        if n % t == 0:
            best = t
    assert best is not None, (n, cap)
    return best


def _params(sem):
    return pltpu.CompilerParams(dimension_semantics=sem, vmem_limit_bytes=VMEM_LIMIT)


def _round_up(n, m):
    return -(-n // m) * m


class _Order:
    tokens = []
    last = None

    @classmethod
    def take(cls):
        out, cls.tokens = cls.tokens, []
        return out

    @classmethod
    def done(cls, result):
        cls.last = result[0] if isinstance(result, (list, tuple)) else result
        return result


ANY_SPEC = pl.BlockSpec(memory_space=pl.ANY)


def _mm(name, a, b, mode, out_dtype=F32, res=None, scale=1.0,
        cap_m=TILE_CAP, cap_n=TILE_CAP, cap_k=TILE_CAP):
    segs = list(a) if isinstance(a, (list, tuple)) else [a]
    nseg = len(segs)
    if mode == "tn":
        k = segs[0].shape[0]
        widths = [s.shape[1] for s in segs]
        m = sum(widths)
        k2, n = b.shape
        tm = _tile(math.gcd(*widths), cap_m)
        tk = _tile(k, cap_k)
        counts = [wd // tm for wd in widths]
    else:
        m = segs[0].shape[0]
        widths = [s.shape[1] for s in segs]
        k = sum(widths)
        (k2, n) = b.shape if mode == "nn" else b.shape[::-1]
        tm = _tile(m, cap_m)
        tk = _tile(math.gcd(*widths), cap_k)
        counts = [wd // tk for wd in widths]
    assert k == k2, (name, [s.shape for s in segs], b.shape, mode)
    tn = _tile(n, cap_n)
    nk = k // tk
    starts = [sum(counts[:s]) for s in range(nseg)]
    dims = {"nn": NN, "nt": NT, "tn": TN}[mode]

    def a_spec(s):
        lo, cnt = starts[s], counts[s]
        if mode == "tn":
            if nseg == 1:
                return pl.BlockSpec((tk, tm), lambda i, j, kk: (kk, i))
            return pl.BlockSpec(
                (tk, tm), lambda i, j, kk: (jnp.where((i >= lo) & (i < lo + cnt), kk, 0),
                                            jnp.clip(i - lo, 0, cnt - 1)))
        if nseg == 1:
            return pl.BlockSpec((tm, tk), lambda i, j, kk: (i, kk))
        return pl.BlockSpec((tm, tk), lambda i, j, kk: (i, jnp.clip(kk - lo, 0, cnt - 1)))

    b_spec = (pl.BlockSpec((tn, tk), lambda i, j, kk: (j, kk)) if mode == "nt"
              else pl.BlockSpec((tk, tn), lambda i, j, kk: (kk, j)))
    o_spec = pl.BlockSpec((tm, tn), lambda i, j, kk: (i, j))
    has_res = res is not None
    use_acc = nk > 1 or nseg > 1
    ties = _Order.take()
    nt_ = len(ties)

    def body(*refs):
        a_refs = refs[:nseg]
        b_ref = refs[nseg]
        r_ref = refs[nseg + 1] if has_res else None
        o_ref = refs[nseg + 1 + has_res + nt_]
        scr = refs[nseg + 2 + has_res + nt_:]

        def finish(acc):
            if scale != 1.0:
                acc = acc * scale
            if has_res:
                acc = r_ref[...].astype(F32) + acc
            o_ref[...] = acc.astype(o_ref.dtype)

        if not use_acc:
            finish(_dot(a_refs[0][...], b_ref[...], dims))
            return
        acc_ref = scr[0]
        kk = pl.program_id(2)
        sel = pl.program_id(0) if mode == "tn" else kk

        @pl.when(kk == 0)
        def _():
            acc_ref[...] = jnp.zeros_like(acc_ref)

        for s in range(nseg):
            def add(s=s):
                acc_ref[...] += _dot(a_refs[s][...], b_ref[...], dims)
            if nseg == 1:
                add()
            else:
                pl.when((sel >= starts[s]) & (sel < starts[s] + counts[s]))(add)

        @pl.when(kk == nk - 1)
        def _():
            finish(acc_ref[...])

    in_specs = ([a_spec(s) for s in range(nseg)] + [b_spec] + ([o_spec] if has_res else [])
                + [ANY_SPEC] * nt_)
    args = tuple(segs) + (b,) + ((res,) if has_res else ()) + tuple(ties)
    return _Order.done(pl.pallas_call(
        body, name=name,
        out_shape=jax.ShapeDtypeStruct((m, n), out_dtype),
        grid=(m // tm, n // tn, nk),
        in_specs=in_specs, out_specs=o_spec,
        scratch_shapes=[pltpu.VMEM((tm, tn), F32)] if use_acc else [],
        compiler_params=_params(("parallel", "parallel", "arbitrary")),
    )(*args))


def _act(g, u):
    return _silu(g.astype(F32)) * u.astype(F32)


def _ffn_up(name, h, wgut, cap_m=ROW_TILE_CAP, cap_n=TILE_CAP):
    m, k = h.shape
    dff = wgut.shape[0] // 2
    tm, tn = _tile(m, cap_m), _tile(dff, cap_n)
    nj = dff // tn
    ties = _Order.take()

    def body(h_ref, wg_ref, wu_ref, *rest):
        g_ref, u_ref, a_ref = rest[len(ties):]
        hv = h_ref[...]
        g = _dot(hv, wg_ref[...], NT)
        u = _dot(hv, wu_ref[...], NT)
        g_ref[...] = g.astype(g_ref.dtype)
        u_ref[...] = u.astype(u_ref.dtype)
        a_ref[...] = _act(g, u).astype(a_ref.dtype)

    o_spec = pl.BlockSpec((tm, tn), lambda i, j: (i, j))
    return _Order.done(pl.pallas_call(
        body, name=name, out_shape=[jax.ShapeDtypeStruct((m, dff), ACT_DTYPE)] * 3,
        grid=(m // tm, nj),
        in_specs=[pl.BlockSpec((tm, k), lambda i, j: (i, 0)),
                  pl.BlockSpec((tn, k), lambda i, j: (j, 0)),
                  pl.BlockSpec((tn, k), lambda i, j: (nj + j, 0))] + [ANY_SPEC] * len(ties),
        out_specs=[o_spec] * 3,
        compiler_params=_params(("parallel", "parallel")),
    )(h, wgut, wgut, *ties))


def _ffn_dact(name, dxo, wd, g, u, scale, cap_m=ROW_TILE_CAP, cap_n=TILE_CAP):
    m, k = dxo.shape
    dff = wd.shape[0]
    tm, tn = _tile(m, cap_m), _tile(dff, cap_n)
    ties = _Order.take()

    def body(d_ref, w_ref, g_ref, u_ref, *rest):
        dg_ref, du_ref = rest[len(ties):]
        da = _dot(d_ref[...], w_ref[...], NT) * scale
        _, vjp = jax.vjp(_act, g_ref[...], u_ref[...])
        dg, du = vjp(da)
        dg_ref[...] = dg.astype(dg_ref.dtype)
        du_ref[...] = du.astype(du_ref.dtype)

    o_spec = pl.BlockSpec((tm, tn), lambda i, j: (i, j))
    return _Order.done(pl.pallas_call(
        body, name=name, out_shape=[jax.ShapeDtypeStruct((m, dff), ACT_DTYPE)] * 2,
        grid=(m // tm, dff // tn),
        in_specs=[pl.BlockSpec((tm, k), lambda i, j: (i, 0)),
                  pl.BlockSpec((tn, k), lambda i, j: (j, 0)), o_spec, o_spec]
        + [ANY_SPEC] * len(ties),
        out_specs=[o_spec] * 2,
        compiler_params=_params(("parallel", "parallel")),
    )(dxo, wd, g, u, *ties))


def _rw(name, fn, ins, outs, accs=(), tr=512, ncb=1):
    t = next(a.shape[0] for kind, a, _, _ in ins if kind == "row")
    assert t % tr == 0
    n_in = len(ins)
    n_pieces = sum(len(w) for w, _ in outs)

    def spec(kind, arr, width, base):
        if kind == "row":
            return pl.BlockSpec((tr, width), lambda j, i: (i, base + j))
        return pl.BlockSpec((arr.shape[0], width), lambda j, i: (0, base + j))

    in_specs = [spec(*s) for s in ins]
    out_shapes, out_specs = [], []
    for widths, dt in outs:
        w = sum(widths)
        out_shapes.append(jax.ShapeDtypeStruct((t, w * ncb), dt))
        out_specs.append(pl.BlockSpec((tr, w), lambda j, i: (i, j)))
    for rows, width in accs:
        out_shapes.append(jax.ShapeDtypeStruct((rows, width * ncb), F32))
        out_specs.append(pl.BlockSpec((rows, width), lambda j, i: (0, j)))

    ties = _Order.take()
    nt_ = len(ties)
    in_specs = in_specs + [ANY_SPEC] * nt_

    def body(*refs):
        vals = [r[...] for r in refs[:n_in]]
        res = fn(*vals)
        o_refs = refs[n_in + nt_:n_in + nt_ + len(outs)]
        a_refs = refs[n_in + nt_ + len(outs):]
        p = 0
        for (widths, _), o_ref in zip(outs, o_refs):
            off = 0
            for w in widths:
                if len(widths) == 1:
                    o_ref[...] = res[p].astype(o_ref.dtype)
                else:
                    o_ref[:, off:off + w] = res[p].astype(o_ref.dtype)
                off += w
                p += 1
        i = pl.program_id(1)
        for a_ref, v in zip(a_refs, res[n_pieces:]):
            @pl.when(i == 0)
            def _(a_ref=a_ref, v=v):
                a_ref[...] = v

            @pl.when(i > 0)
            def _(a_ref=a_ref, v=v):
                a_ref[...] += v

    return _Order.done(pl.pallas_call(
        body, name=name, out_shape=out_shapes,
        grid=(ncb, t // tr), in_specs=in_specs, out_specs=out_specs,
        compiler_params=_params(("parallel", "arbitrary")),
    )(*[a for _, a, _, _ in ins], *ties))


def _rms(x, g):
    x = x.astype(F32)
    return x * lax.rsqrt(jnp.mean(x * x, axis=-1, keepdims=True) + EPS) * g


def _silu(x):
    return x * jax.nn.sigmoid(x)


def _colsum(v):
    return jnp.sum(v, axis=0, keepdims=True)


def _pair_norm(x, g):
    w = 2 * HD
    ri = lax.broadcasted_iota(jnp.int32, (w, w), 0)
    ci = lax.broadcasted_iota(jnp.int32, (w, w), 1)
    same_head = ((ri < HD) == (ci < HD)).astype(F32)
    ms = _spread(x * x, same_head) * (1.0 / HD)
    return x * lax.rsqrt(ms + EPS) * g


ATTN_SCALE = 1.0 / math.sqrt(HD)


def _attn_bias(coef):
    key = lax.broadcasted_iota(jnp.int32, (QB, QB), 0)
    qry = lax.broadcasted_iota(jnp.int32, (QB, QB), 1)
    dist = (qry - key).astype(F32)
    own = jnp.where(qry >= key, -coef * dist, NEG)
    prev = jnp.where(qry <= key, -coef * (dist + float(QB)), NEG)
    return own, prev


def _attn_pair(qn, kcn, kpn, vc, vp, b_own, b_prev):
    nb = qn.shape[0]
    w = 2 * HD
    lane = lax.broadcasted_iota(jnp.int32, (1, 1, w), 2)
    eye = (lax.broadcasted_iota(jnp.int32, (QB, QB), 0)
           == lax.broadcasted_iota(jnp.int32, (QB, QB), 1)).astype(F32)
    out = jnp.zeros((nb, QB, w), F32)
    lb = jnp.zeros((nb * QB, w), F32)
    for hh in range(2):
        mask = ((lane < HD) if hh == 0 else (lane >= HD)).astype(F32)
        qm = qn * mask
        lc = _dot(kcn, qm, BNT) + b_own[hh]
        lp = _dot(kpn, qm, BNT) + b_prev[hh]
        m = lax.stop_gradient(jnp.maximum(jnp.max(lc, axis=1, keepdims=True),
                                          jnp.max(lp, axis=1, keepdims=True)))
        pc = jnp.exp(lc - m)
        pp = jnp.exp(lp - m)
        l = jnp.sum(pc, axis=1, keepdims=True) + jnp.sum(pp, axis=1, keepdims=True)
        inv = 1.0 / l
        out = out + (_dot(pc * inv, vc, BTN) + _dot(pp * inv, vp, BTN)) * mask
        diag = (eye * (m + jnp.log(l))).reshape(nb * QB, QB)
        lb = lb + _spread(diag, jnp.broadcast_to(mask[0], (QB, w)))
    return out, lb.reshape(nb, QB, w)


NORM_ROWS = 128
NORM_UNROLL = 4
EPILOGUE_ROWS = 512
ATTN_BATCH_FWD = 16
ATTN_BATCH_BWD = 8


def _unit_rows(u, d):
    r = u & (d - 1)
    n = u >> (d.bit_length() - 1)

    def rows(blk):
        start = pl.multiple_of(blk * (QB * d), QB * d)
        return pl.ds(start, QB) if d == 1 else pl.ds(start + r, QB, stride=d)

    return rows(n), rows(jnp.maximum(n - 1, 0)), n == 0


def _unit_batch(i, nbatch, d, bias, qf, kf, vf):
    units = [_unit_rows(i * nbatch + j, d) for j in range(nbatch)]
    cur = lambda ref: jnp.stack([ref[c, :] for c, _, _ in units])
    prv = lambda ref: jnp.stack([ref[p, :] for _, p, _ in units])
    b_own = [b[0] for b in bias]
    b_prev = [jnp.stack([jnp.where(first, NEG, b[1]) for _, _, first in units]) for b in bias]
    return units, (cur(qf), cur(kf), prv(kf), cur(vf), prv(vf), b_own, b_prev)


def _q_norm(x, g):
    return _pair_norm(x, g * ATTN_SCALE)


def _attn_prologue(t, q_ref, k_ref, v_ref, qg_ref, kg_ref, qf, kf, vf):
    def chunk(c, carry):
        rows = pl.ds(pl.multiple_of(c * NORM_ROWS, NORM_ROWS), NORM_ROWS)
        qf[rows, :] = _q_norm(q_ref[rows, :].astype(F32), qg_ref[...])
        kf[rows, :] = _pair_norm(k_ref[rows, :].astype(F32), kg_ref[...])
        vf[rows, :] = v_ref[rows, :].astype(F32)
        return carry
    lax.fori_loop(0, t // NORM_ROWS, chunk, 0, unroll=NORM_UNROLL)


def _attn_specs(t, bases):
    w = 2 * HD
    ins = [pl.BlockSpec((t, w), functools.partial(lambda p, c, b: (0, b + p), b=b)) for b in bases]
    gain = pl.BlockSpec((1, w), lambda p, c: (0, 0))
    blk = pl.BlockSpec((t, w), lambda p, c: (0, p))
    return ins, gain, blk


def _attn_fwd(name, proj, bases, qg, kg, coefs, d):
    t = proj.shape[0]
    npairs = coefs.shape[0] // 2
    w = 2 * HD
    ins, gain, blk = _attn_specs(t, bases)

    def body(coef_ref, q_ref, k_ref, v_ref, qg_ref, kg_ref, o_ref, l_ref, qf, kf, vf):
        p = pl.program_id(0)
        bias = (_attn_bias(coef_ref[2 * p]), _attn_bias(coef_ref[2 * p + 1]))
        _attn_prologue(t, q_ref, k_ref, v_ref, qg_ref, kg_ref, qf, kf, vf)

        def step(i, carry):
            units, ins = _unit_batch(i, ATTN_BATCH_FWD, d, bias, qf, kf, vf)
            o, lb = _attn_pair(*ins)
            for j, (cur, _, _) in enumerate(units):
                o_ref[cur, :] = o[j]
                l_ref[cur, :] = lb[j]
            return carry

        lax.fori_loop(0, t // QB // ATTN_BATCH_FWD, step, 0)

    return pl.pallas_call(
        body, name=name,
        out_shape=[jax.ShapeDtypeStruct((t, npairs * w), F32)] * 2,
        grid_spec=pltpu.PrefetchScalarGridSpec(
            num_scalar_prefetch=1, grid=(npairs,),
            in_specs=ins + [gain, gain], out_specs=[blk, blk],
            scratch_shapes=[pltpu.VMEM((t, w), F32)] * 3),
        compiler_params=_params(("arbitrary",)),
    )(coefs, proj, proj, proj, qg, kg)


def _attn_bwd(name, proj, bases, qg, kg, coefs, d, do, dl):
    t = proj.shape[0]
    npairs = coefs.shape[0] // 2
    w = 2 * HD
    ins, gain, blk = _attn_specs(t, bases)

    def body(coef_ref, q_ref, k_ref, v_ref, qg_ref, kg_ref, do_ref, dl_ref,
             dq_ref, dk_ref, dv_ref, dqg_ref, dkg_ref, qf, kf, vf, dqf, dkf, dvf):
        p = pl.program_id(0)
        bias = (_attn_bias(coef_ref[2 * p]), _attn_bias(coef_ref[2 * p + 1]))
        _attn_prologue(t, q_ref, k_ref, v_ref, qg_ref, kg_ref, qf, kf, vf)
        dkf[...] = jnp.zeros_like(dkf)
        dvf[...] = jnp.zeros_like(dvf)

        def step(i, carry):
            units, ins = _unit_batch(i, ATTN_BATCH_BWD, d, bias, qf, kf, vf)
            f = lambda a, b, c, e, g: _attn_pair(a, b, c, e, g, *ins[5:])
            _, vjp = jax.vjp(f, *ins[:5])
            cot = (jnp.stack([do_ref[cur, :] for cur, _, _ in units]),
                   jnp.stack([dl_ref[cur, :] for cur, _, _ in units]))
            dq, dkc, dkp, dvc, dvp = vjp(cot)
            for j, (cur, prv, _) in enumerate(units):
                dqf[cur, :] = dq[j]
                dkf[cur, :] += dkc[j]
                dkf[prv, :] += dkp[j]
                dvf[cur, :] += dvc[j]
                dvf[prv, :] += dvp[j]
            return carry

        lax.fori_loop(0, t // QB // ATTN_BATCH_BWD, step, 0)

        def chunk(c, carry):
            dqg_acc, dkg_acc = carry
            rows = pl.ds(pl.multiple_of(c * EPILOGUE_ROWS, EPILOGUE_ROWS), EPILOGUE_ROWS)
            _, vq = jax.vjp(_q_norm, q_ref[rows, :].astype(F32), qg_ref[...])
            dq, dqg = vq(dqf[rows, :])
            _, vk = jax.vjp(_pair_norm, k_ref[rows, :].astype(F32), kg_ref[...])
            dk, dkg = vk(dkf[rows, :])
            dq_ref[rows, :] = dq.astype(dq_ref.dtype)
            dk_ref[rows, :] = dk.astype(dk_ref.dtype)
            dv_ref[rows, :] = dvf[rows, :].astype(dv_ref.dtype)
            return dqg_acc + dqg, dkg_acc + dkg

        zero = jnp.zeros((1, w), F32)
        dqg, dkg = lax.fori_loop(0, t // EPILOGUE_ROWS, chunk, (zero, zero))

        @pl.when(p == 0)
        def _():
            dqg_ref[...] = dqg
            dkg_ref[...] = dkg

        @pl.when(p > 0)
        def _():
            dqg_ref[...] += dqg
            dkg_ref[...] += dkg

    big = jax.ShapeDtypeStruct((t, npairs * w), ACT_DTYPE)
    small = jax.ShapeDtypeStruct((1, w), F32)
    return pl.pallas_call(
        body, name=name,
        out_shape=[big, big, big, small, small],
        grid_spec=pltpu.PrefetchScalarGridSpec(
            num_scalar_prefetch=1, grid=(npairs,),
            in_specs=ins + [gain, gain, blk, blk],
            out_specs=[blk, blk, blk, gain, gain],
            scratch_shapes=[pltpu.VMEM((t, w), F32)] * 6),
        compiler_params=_params(("arbitrary",)),
    )(coefs, proj, proj, proj, qg, kg, do, dl)


CONV_ROWS = 256
CONV_HALO = 8


def _rows_back(x, s):
    return x if s == 0 else pltpu.roll(x, s, 0)


def _rows_ahead(x, s):
    return x if s == 0 else pltpu.roll(x, x.shape[0] - s, 0)


def _conv_pre(u, w, b):
    y = b
    for kk in range(SSD_K):
        y = y + w[kk:kk + 1, :] * _rows_back(u, SSD_K - 1 - kk)
    return y


def _stage_padded(dst, src_ref, t):
    zeros = jnp.zeros((CONV_HALO, dst.shape[1]), F32)
    dst[0:CONV_HALO, :] = zeros
    dst[t + CONV_HALO:t + 2 * CONV_HALO, :] = zeros
    dst[CONV_HALO:t + CONV_HALO, :] = src_ref[...].astype(F32)


def _chunk_rows(c):
    r0 = pl.multiple_of(c * CONV_ROWS, CONV_ROWS)
    return pl.ds(r0, CONV_ROWS + 2 * CONV_HALO), pl.ds(r0, CONV_ROWS)


def _conv_fwd(name, src, base, w, b, cw=128):
    t = src.shape[0]
    c = w.shape[1]
    centre = slice(CONV_HALO, CONV_HALO + CONV_ROWS)

    def body(u_ref, w_ref, b_ref, o_ref, up):
        _stage_padded(up, u_ref, t)
        wv, bv = w_ref[...], b_ref[...]

        def chunk(ci, carry):
            ext, rows = _chunk_rows(ci)
            y = _conv_pre(up[ext, :], wv, bv)
            o_ref[rows, :] = _silu(y)[centre].astype(o_ref.dtype)
            return carry

        lax.fori_loop(0, t // CONV_ROWS, chunk, 0)

    return pl.pallas_call(
        body, name=name, out_shape=jax.ShapeDtypeStruct((t, c), ACT_DTYPE),
        grid=(c // cw,),
        in_specs=[pl.BlockSpec((t, cw), lambda j: (0, base + j)),
                  pl.BlockSpec((SSD_K, cw), lambda j: (0, j)),
                  pl.BlockSpec((1, cw), lambda j: (0, j))],
        out_specs=pl.BlockSpec((t, cw), lambda j: (0, j)),
        scratch_shapes=[pltpu.VMEM((t + 2 * CONV_HALO, cw), F32)],
        compiler_params=_params(("parallel",)),
    )(src, w, b)


def _conv_bwd(name, src, base, w, b, dout, cw=128):
    t = src.shape[0]
    c = w.shape[1]

    centre = slice(CONV_HALO, CONV_HALO + CONV_ROWS)

    def body(u_ref, w_ref, b_ref, d_ref, du_ref, dw_ref, db_ref, up, dp):
        _stage_padded(up, u_ref, t)
        _stage_padded(dp, d_ref, t)
        wv, bv = w_ref[...], b_ref[...]

        def chunk(ci, carry):
            dws, db = carry
            ext, rows = _chunk_rows(ci)
            u = up[ext, :]
            y = _conv_pre(u, wv, bv)
            sg = jax.nn.sigmoid(y)
            dy = dp[ext, :] * (sg * (1.0 + y * (1.0 - sg)))
            du = jnp.zeros_like(u)
            new_dws = []
            for kk in range(SSD_K):
                s = SSD_K - 1 - kk
                du = du + wv[kk:kk + 1, :] * _rows_ahead(dy, s)
                new_dws.append(dws[kk] + _colsum((dy * _rows_back(u, s))[centre]))
            du_ref[rows, :] = du[centre].astype(du_ref.dtype)
            return tuple(new_dws), db + _colsum(dy[centre])

        zero = jnp.zeros((1, cw), F32)
        dws, db = lax.fori_loop(0, t // CONV_ROWS, chunk, ((zero,) * SSD_K, zero))
        for kk in range(SSD_K):
            dw_ref[kk:kk + 1, :] = dws[kk]
        db_ref[...] = db

    return pl.pallas_call(
        body, name=name,
        out_shape=[jax.ShapeDtypeStruct((t, c), ACT_DTYPE),
                   jax.ShapeDtypeStruct((SSD_K, c), F32),
                   jax.ShapeDtypeStruct((1, c), F32)],
        grid=(c // cw,),
        in_specs=[pl.BlockSpec((t, cw), lambda j: (0, base + j)),
                  pl.BlockSpec((SSD_K, cw), lambda j: (0, j)),
                  pl.BlockSpec((1, cw), lambda j: (0, j)),
                  pl.BlockSpec((t, cw), lambda j: (0, j))],
        out_specs=[pl.BlockSpec((t, cw), lambda j: (0, j)),
                   pl.BlockSpec((SSD_K, cw), lambda j: (0, j)),
                   pl.BlockSpec((1, cw), lambda j: (0, j))],
        scratch_shapes=[pltpu.VMEM((t + 2 * CONV_HALO, cw), F32)] * 2,
        compiler_params=_params(("parallel",)),
    )(src, w, b, dout)


def _softplus(x):
    return jnp.maximum(x, 0.0) + jnp.log(1.0 + jnp.exp(-jnp.abs(x)))


def _ssd_chunk(xbc, dtraw, bias, alog, states):
    wd = states[0].shape[1]
    nj = wd // SSD_P
    inner = SSD_G * wd
    dt = _softplus(dtraw + bias)
    a = dt * (-jnp.exp(alog))
    li = lax.broadcasted_iota(jnp.int32, (SSD_Q, SSD_Q), 0)
    si = lax.broadcasted_iota(jnp.int32, (SSD_Q, SSD_Q), 1)
    causal = li >= si
    acs = _running_sum(a, causal.astype(F32))
    acs_t = acs.T
    a_last = acs[SSD_Q - 1:SSD_Q, :]
    grow = jnp.exp(acs)
    shrink = jnp.exp(a_last - acs)
    hrow = lax.broadcasted_iota(jnp.int32, (LANE, wd), 0)
    wcol = lax.broadcasted_iota(jnp.int32, (LANE, wd), 1)
    lane = lax.broadcasted_iota(jnp.int32, (1, LANE), 1)
    ys, snext = [], []
    for g in range(SSD_G):
        lo = (hrow - g * nj) * SSD_P
        head_lanes = jnp.logical_and(wcol >= lo, wcol < lo + SSD_P).astype(F32)
        xs = xbc[:, g * wd:(g + 1) * wd]
        bm = xbc[:, inner + g * SSD_N:inner + (g + 1) * SSD_N]
        cm = xbc[:, inner + (SSD_G + g) * SSD_N:inner + (SSD_G + g + 1) * SSD_N]
        xdt = xs * _dot(dt, head_lanes)
        y_off = _dot(cm, states[g]) * _dot(grow, head_lanes)
        keep_all = _spread(jnp.broadcast_to(grow[SSD_Q - 1:SSD_Q, :], (8, LANE)), head_lanes)
        s_new = (states[g] * keep_all[0:1, :]
                 + _dot(bm, xdt * _dot(shrink, head_lanes), TN))
        cb = _dot(cm, bm, NT)
        pieces = []
        for i in range(wd // LANE):
            xp = xdt[:, i * LANE:(i + 1) * LANE]
            acc = jnp.zeros((SSD_Q, LANE), F32)
            for hh in range(LANE // SSD_P):
                h = g * nj + i * (LANE // SSD_P) + hh
                decay = jnp.exp(jnp.where(causal, acs[:, h:h + 1] - acs_t[h:h + 1, :], NEG))
                keep = jnp.logical_and(lane >= hh * SSD_P, lane < (hh + 1) * SSD_P).astype(F32)
                acc = acc + _dot(cb * decay, xp * keep)
            pieces.append(acc)
        y_diag = pieces[0] if len(pieces) == 1 else jnp.concatenate(pieces, axis=1)
        ys.append(y_diag + y_off)
        snext.append(s_new)
    return ys, snext


def _ssd_specs(cdim, wd, rev, nc):
    ch = (lambda c: nc - 1 - c) if rev else (lambda c: c)
    full = lambda width: pl.BlockSpec((SSD_Q, width), lambda c: (ch(c), 0))
    vec = pl.BlockSpec((1, LANE), lambda c: (0, 0))
    st = pl.BlockSpec((1, SSD_G, SSD_N, wd), lambda c: (ch(c), 0, 0, 0))
    return full, vec, st


def _ssd_fwd(name, xbc, dtraw, bias, alog, inner):
    t, cdim = xbc.shape
    wd = inner // SSD_G
    nc = t // SSD_Q
    full, vec, st = _ssd_specs(cdim, wd, False, nc)

    def body(x_ref, r_ref, b_ref, a_ref, y_ref, st_ref, s_scr):
        @pl.when(pl.program_id(0) == 0)
        def _():
            s_scr[...] = jnp.zeros_like(s_scr)

        sprev = [s_scr[g] for g in range(SSD_G)]
        ys, snext = _ssd_chunk(x_ref[...].astype(F32), r_ref[...], b_ref[...], a_ref[...], sprev)
        for g in range(SSD_G):
            st_ref[0, g] = sprev[g]
            y_ref[:, g * wd:(g + 1) * wd] = ys[g]
            s_scr[g] = snext[g]

    return pl.pallas_call(
        body, name=name,
        out_shape=[jax.ShapeDtypeStruct((t, inner), F32),
                   jax.ShapeDtypeStruct((nc, SSD_G, SSD_N, wd), F32)],
        grid=(nc,),
        in_specs=[full(cdim), full(LANE), vec, vec],
        out_specs=[full(inner), st],
        scratch_shapes=[pltpu.VMEM((SSD_G, SSD_N, wd), F32)],
        compiler_params=_params(("arbitrary",)),
    )(xbc, dtraw, bias, alog)


def _ssd_bwd(name, xbc, dtraw, bias, alog, states, dy, dxs_extra):
    t, cdim = xbc.shape
    inner = dy.shape[1]
    wd = inner // SSD_G
    nc = t // SSD_Q
    full, vec, st = _ssd_specs(cdim, wd, True, nc)

    def body(x_ref, r_ref, b_ref, a_ref, st_ref, dy_ref, dx0_ref,
             dx_ref, dr_ref, db_ref, da_ref, ds_scr):
        first = pl.program_id(0) == 0

        @pl.when(first)
        def _():
            ds_scr[...] = jnp.zeros_like(ds_scr)

        sprev = [st_ref[0, g] for g in range(SSD_G)]
        _, vjp = jax.vjp(_ssd_chunk, x_ref[...].astype(F32), r_ref[...], b_ref[...], a_ref[...],
                         sprev)
        dyv = dy_ref[...]
        dys = [dyv[:, g * wd:(g + 1) * wd] for g in range(SSD_G)]
        dsn = [ds_scr[g] for g in range(SSD_G)]
        dx, dr, db, da, dsp = vjp((dys, dsn))
        dx_ref[:, :inner] = dx[:, :inner] + dx0_ref[...].astype(F32)
        dx_ref[:, inner:] = dx[:, inner:]
        dr_ref[...] = dr
        for g in range(SSD_G):
            ds_scr[g] = dsp[g]

        @pl.when(first)
        def _():
            db_ref[...] = db
            da_ref[...] = da

        @pl.when(jnp.logical_not(first))
        def _():
            db_ref[...] += db
            da_ref[...] += da

    return pl.pallas_call(
        body, name=name,
        out_shape=[jax.ShapeDtypeStruct((t, cdim), F32),
                   jax.ShapeDtypeStruct((t, LANE), F32),
                   jax.ShapeDtypeStruct((1, LANE), F32),
                   jax.ShapeDtypeStruct((1, LANE), F32)],
        grid=(nc,),
        in_specs=[full(cdim), full(LANE), vec, vec, st, full(inner), full(inner)],
        out_specs=[full(cdim), full(LANE), vec, vec],
        scratch_shapes=[pltpu.VMEM((SSD_G, SSD_N, wd), F32)],
        compiler_params=_params(("arbitrary",)),
    )(xbc, dtraw, bias, alog, states, dy, dxs_extra)


def _mix(o0, o1, o2, l0, l1, l2):
    m = lax.stop_gradient(jnp.maximum(jnp.maximum(l0, l1), l2))
    e0, e1, e2 = jnp.exp(l0 - m), jnp.exp(l1 - m), jnp.exp(l2 - m)
    return (e0 * o0 + e1 * o1 + e2 * o2) / (e0 + e1 + e2)


def _gate(y, xs, z, dexp, gain):
    v = (y + xs.astype(F32) * dexp) * _silu(z.astype(F32))
    return _rms(v, gain)


def _merge(ga, gs, ap, sp):
    return jax.nn.sigmoid(ga.astype(F32)) * ap + jax.nn.sigmoid(gs.astype(F32)) * sp


def _alibi_coefs(hp):
    n = hp * len(PATTERNS)
    slopes = np.exp2(-ALIBI_MAX_EXP * np.arange(1, n + 1, dtype=np.float32) / n).astype(np.float32)
    return [jnp.asarray(slopes[g * hp:(g + 1) * hp] * np.float32(d))
            for g, (_, d) in enumerate(PATTERNS)]


def _local_step(x, tgt, w, p, gw_=None, aw=None):
    t, d = x.shape
    dff = w["gu1t"].shape[0] // 2
    aw = w["abt"].shape[1] if aw is None else aw
    hp = aw // HD
    qkv = len(PATTERNS) * aw
    inner = p["ssd_norm"].shape[1]
    nh = p["dt_bias"].shape[1]
    gw_ = {} if gw_ is None else gw_
    gw = inner // SSD_G
    cdim = inner + 2 * SSD_G * SSD_N
    z_off, xbc_off = 3 * qkv, 3 * qkv + inner
    ga_off = xbc_off + cdim
    gs_off = ga_off + d
    hw = d // 2
    assert z_off % gw == 0 and xbc_off % LANE == 0 and ga_off % hw == 0 and gs_off % hw == 0
    assert (nh // SSD_G) * SSD_P == gw and hp % 2 == 0 and aw % LANE == 0 and nh <= LANE
    gdt = MXU_DTYPE

    row = lambda a, width, base=0: ("row", a, width, base)
    const = lambda a, width, base=0: ("const", a, width, base)

    def rms_fwd(name, xin, g):
        return _rw(name, lambda xv, gv: (_rms(xv, gv),), [row(xin, d), const(g, d)],
                   [((d,), ACT_DTYPE)])[0]

    def rms_bwd(name, xin, g, dh, dres):
        def fn(xv, gv, dhv, drv):
            _, vjp = jax.vjp(_rms, xv, gv)
            dx, dg = vjp(dhv.astype(F32))
            return drv + dx, dg
        return _rw(name, fn, [row(xin, d), const(g, d), row(dh, d), row(dres, d)],
                   [((d,), F32)], accs=[(1, d)])

    def ffn_fwd(tag, xin, g, key_gu, key_d):
        h = rms_fwd(tag + "_norm", xin, g)
        gate, up, a = _ffn_up(tag + "_up", h, w[key_gu])
        xo = _mm(tag + "_down", a, w[key_d], "nn", F32, res=xin, scale=0.5)
        return xo, (h, gate, up, a)

    def ffn_bwd(tag, xin, g, wgut, wd, saved, dxo, key_gu, key_d):
        h, gate, up, a = saved
        gw_[key_d] = _mm(tag + "_dwd", a, dxo, "tn", gdt, scale=0.5)
        dgu = _ffn_dact(tag + "_dact", dxo, wd, gate, up, 0.5)
        gw_[key_gu] = _mm(tag + "_dwgu", dgu, h, "tn", gdt)
        dh = _mm(tag + "_dh", dgu, wgut, "nn", F32)
        return rms_bwd(tag + "_dnorm", xin, g, dh, dxo)

    x1, ffn1_saved = ffn_fwd("ffn1", x, p["ffn1_norm"], "gu1t", "d1")
    if hasattr(w, "after_first_ffn"):
        w.after_first_ffn()
    h2 = rms_fwd("mix_norm", x1, p["mix_norm"])
    proj = _mm("in_proj", h2, w["maint"], "nt", ACT_DTYPE, cap_m=ROW_TILE_CAP,
               cap_n=WIDE_N_CAP)
    dtraw = _mm("dt_proj", h2, w["dtt"], "nt", F32)

    coefs = _alibi_coefs(hp)
    qg2 = jnp.concatenate([p["q_norm"], p["q_norm"]], axis=1)
    kg2 = jnp.concatenate([p["k_norm"], p["k_norm"]], axis=1)
    pw = 2 * HD
    attn_bases = [[(off + gi * aw) // pw for off in (0, qkv, 2 * qkv)]
                  for gi in range(len(PATTERNS))]
    attn_o, attn_l = [], []
    for gi, (_, dil) in enumerate(PATTERNS):
        o, l = _attn_fwd(f"attn_fwd{gi}", proj, attn_bases[gi], qg2, kg2, coefs[gi], dil)
        attn_o.append(o)
        attn_l.append(l)
    ao = _rw("attn_mix", lambda *v: (_mix(*v),), [row(a, aw) for a in attn_o + attn_l],
             [((aw,), ACT_DTYPE)])[0]

    xbc = _conv_fwd("conv_fwd", proj, xbc_off // LANE, p["conv_w"], p["conv_b"])
    pad = lambda v: jnp.pad(v, ((0, 0), (0, LANE - nh)))
    bias_p, alog_p = pad(p["dt_bias"]), pad(p["a_log"])
    yssd, states = _ssd_fwd("ssd_fwd", xbc, dtraw, bias_p, alog_p, inner)
    dexp = jnp.repeat(p["d_skip"], SSD_P, axis=1)
    gate_ins = [row(yssd, gw), row(xbc, gw), row(proj, gw, z_off // gw),
                const(dexp, gw), const(p["ssd_norm"], gw)]
    yn = _rw("ssd_gate", lambda *v: (_gate(*v),), gate_ins, [((gw,), ACT_DTYPE)], ncb=SSD_G)[0]

    ap = _mm("attn_out", ao, w["abt"], "nt", F32)
    sp = _mm("ssd_out", yn, w["sb"], "nn", F32)
    merge_ins = [row(proj, hw, ga_off // hw), row(proj, hw, gs_off // hw), row(ap, hw), row(sp, hw)]
    mg = _rw("merge", lambda *v: (_merge(*v),), merge_ins, [((hw,), ACT_DTYPE)], ncb=2)[0]
    x2 = _mm("mix_out", mg, w["out"], "nn", F32, res=x1)
    x3, ffn2_saved = ffn_fwd("ffn2", x2, p["ffn2_norm"], "gu2t", "d2")

    def loss_fn(yv, tv):
        e = yv - tv
        return e * (1.0 / d), _colsum(e * e)
    dy, loss_vec = _rw("loss", loss_fn, [row(x3, d), row(tgt, d)], [((d,), F32)], accs=[(1, d)])

    gp = {}
    dx2, gp["ffn2_norm"] = ffn_bwd(
        "ffn2", x2, p["ffn2_norm"], w["gu2t"], w["d2"], ffn2_saved, dy, "gu2t", "d2")
    dmg = _mm("d_merge", dx2, w["out"], "nt", ACT_DTYPE)
    gw_["out"] = _mm("dw_out", mg, dx2, "tn", gdt)

    def merge_bwd(gav, gsv, apv, spv, dv):
        _, vjp = jax.vjp(_merge, gav, gsv, apv, spv)
        return vjp(dv.astype(F32))
    dga, dgs, dap, dsp = _rw("d_merge_gate", merge_bwd, merge_ins + [row(dmg, hw)],
                             [((hw,), ACT_DTYPE)] * 4, ncb=2)
    gw_["abt"] = _mm("dw_ab", dap, ao, "tn", gdt)
    dao = _mm("d_attn_o", dap, w["abt"], "nn", F32)
    gw_["sb"] = _mm("dw_sb", yn, dsp, "tn", gdt)
    dyn = _mm("d_ssd_y", dsp, w["sb"], "nt", F32)

    def gate_bwd(yv, xv, zv, dev, gv, dv):
        _, vjp = jax.vjp(_gate, yv, xv, zv, dev, gv)
        return vjp(dv)
    dyssd, dxs_gate, dz, ddexp, gp["ssd_norm"] = _rw(
        "d_ssd_gate", gate_bwd, gate_ins + [row(dyn, gw)],
        [((gw,), F32), ((gw,), F32), ((gw,), ACT_DTYPE)], accs=[(1, gw), (1, gw)], ncb=SSD_G)
    gp["d_skip"] = ddexp.reshape(nh, SSD_P).sum(axis=1).reshape(1, nh)

    dxbc, ddtraw, dbias, dalog = _ssd_bwd("ssd_bwd", xbc, dtraw, bias_p, alog_p, states,
                                          dyssd, dxs_gate)
    gp["dt_bias"], gp["a_log"] = dbias[:, :nh], dalog[:, :nh]
    du, gp["conv_w"], gp["conv_b"] = _conv_bwd("conv_bwd", proj, xbc_off // LANE,
                                               p["conv_w"], p["conv_b"], dxbc)

    def mix_bwd(*v):
        _, vjp = jax.vjp(_mix, *v[:6])
        return vjp(v[6])
    dmix = _rw("d_attn_mix", mix_bwd, [row(a, aw) for a in attn_o + attn_l] + [row(dao, aw)],
               [((aw,), F32)] * 6)
    dq, dk, dv = [], [], []
    dqg = dkg = None
    for gi, (_, dil) in enumerate(PATTERNS):
        r = _attn_bwd(f"attn_bwd{gi}", proj, attn_bases[gi], qg2, kg2, coefs[gi], dil,
                      dmix[gi], dmix[3 + gi])
        dq.append(r[0])
        dk.append(r[1])
        dv.append(r[2])
        dqg = r[3] if dqg is None else dqg + r[3]
        dkg = r[4] if dkg is None else dkg + r[4]
    gp["q_norm"] = dqg[:, :HD] + dqg[:, HD:]
    gp["k_norm"] = dkg[:, :HD] + dkg[:, HD:]

    segs = dq + dk + dv + [dz, du, dga, dgs]
    gw_["maint"] = _mm("dw_in", segs, h2, "tn", gdt)
    gw_["dtt"] = _mm("dw_dt", ddtraw, h2, "tn", gdt)
    dh2 = _mm("d_h2_main", segs, w["maint"], "nn", F32)
    dh2 = _mm("d_h2_dt", ddtraw, w["dtt"], "nn", F32, res=dh2)
    dx1, gp["mix_norm"] = rms_bwd("d_mix_norm", x1, p["mix_norm"], dh2, dx2)
    dx0, gp["ffn1_norm"] = ffn_bwd(
        "ffn1", x, p["ffn1_norm"], w["gu1t"], w["d1"], ffn1_saved, dx1, "gu1t", "d1")
    return loss_vec, dx0, gw_, gp


MESH = pl.DeviceIdType.MESH
HBM_SPEC = pl.BlockSpec(memory_space=pltpu.HBM)


def _mesh_pos():
    return lax.axis_index("x"), lax.axis_index("y"), lax.axis_index("c")


def _flip(pos, k):
    x, y, c = pos
    return (1 - x if k & 4 else x, 1 - y if k & 2 else y, 1 - c if k & 1 else c)


def _dev_index(pos):
    return 4 * pos[0] + 2 * pos[1] + pos[2]


def _rows_of(ref, base, stride, rows, pos):
    start = pl.multiple_of(base + stride * _dev_index(pos), ROW_ALIGN)
    return ref.at[pl.ds(start, rows)]


def _gather(name, shards, dests, out_shapes):
    n = len(shards)
    n_out = len(out_shapes)

    def body(*refs):
        x_refs = refs[:n]
        o_refs = refs[n:n + n_out]
        send_sems, recv_sems, local_sems = refs[n + n_out:]
        me = _mesh_pos()
        sibling = _flip(me, 1)
        chips = [_flip(me, 4), _flip(me, 2), _flip(me, 6)]

        def slot(i, block):
            k_out, base, stride = dests[i]
            return _rows_of(o_refs[k_out], base, stride, shards[i].shape[0], block)

        def copy(i, k, block, to, src=None):
            dst = slot(i, block)
            return pltpu.make_async_remote_copy(
                src_ref=dst if src is None else src, dst_ref=dst,
                send_sem=send_sems.at[7 * i + k], recv_sem=recv_sems.at[7 * i + k],
                device_id=to, device_id_type=MESH)

        mine = [pltpu.make_async_copy(x_refs[i], slot(i, me), local_sems.at[i]) for i in range(n)]
        for cp in mine:
            cp.start()
        first = []
        for i in range(n):
            first.append(copy(i, 0, me, sibling, src=x_refs[i]))
            first += [copy(i, 1 + j, me, chip, src=x_refs[i]) for j, chip in enumerate(chips)]
        for cp in first:
            cp.start()
        passed = []
        for j, chip in enumerate(chips):
            for i in range(n):
                copy(i, 1 + j, chip, me).wait_recv()
                fwd = copy(i, 4 + j, chip, sibling)
                fwd.start()
                passed.append(fwd)
        for i in range(n):
            copy(i, 0, sibling, me).wait_recv()
            for j, chip in enumerate(chips):
                copy(i, 4 + j, _flip(chip, 1), me).wait_recv()
        for cp in first + passed:
            cp.wait_send()
        for cp in mine:
            cp.wait()

    return pl.pallas_call(
        body, name=name,
        out_shape=[jax.ShapeDtypeStruct(s, dt) for s, dt in out_shapes],
        in_specs=[HBM_SPEC] * n, out_specs=[HBM_SPEC] * n_out,
        scratch_shapes=[pltpu.SemaphoreType.DMA((7 * n,)), pltpu.SemaphoreType.DMA((7 * n,)),
                        pltpu.SemaphoreType.DMA((n,))],
    )(*shards)


SEM_SPEC =pl.BlockSpec(memory_space=pltpu.SEMAPHORE)
SIDE_EFFECT = pltpu.SideEffectType.DATAFLOW_SIDE_EFFECTING


def _split_refs(plan, srcs, lands, i, src_for, land_from):
    si, sbase, sstride, li, lbase, lstride, rows = plan[i]
    return (_rows_of(srcs[si], sbase, sstride, rows, src_for),
            _rows_of(lands[li], lbase, lstride, rows, land_from))


ALL_PEERS = tuple(range(1, NDEV))
SAME_CORE_AND_SIBLING = (1, 4, 2, 6)
OTHER_CHIPS = (4, 2, 6)


def _split_start(name, srcs, lands, plan, after=(), relations=ALL_PEERS):
    ns, nl, n = len(srcs), len(lands), len(plan)

    def body(*refs):
        s_refs = refs[:ns]
        l_refs = refs[ns:ns + nl]
        send_sems, recv_sems = refs[ns + nl + len(after):ns + nl + len(after) + 2]
        local_sems = refs[ns + nl + len(after) + 2]
        token = refs[ns + nl + len(after) + 3 + ns + nl]
        me = _mesh_pos()
        for i in range(n):
            src, dst = _split_refs(plan, s_refs, l_refs, i, me, me)
            pltpu.make_async_copy(src, dst, local_sems.at[i]).start()
        for k in relations:
            peer = _flip(me, k)
            for i in range(n):
                src, dst = _split_refs(plan, s_refs, l_refs, i, peer, me)
                pltpu.make_async_remote_copy(
                    src_ref=src, dst_ref=dst,
                    send_sem=send_sems.at[7 * i + k - 1], recv_sem=recv_sems.at[7 * i + k - 1],
                    device_id=peer, device_id_type=MESH).start()
        token[...] = jnp.zeros_like(token)

    hbm = lambda a: pltpu.HBM(a.shape, a.dtype)
    out_shape = ((pltpu.SemaphoreType.DMA((7 * n,)), pltpu.SemaphoreType.DMA((7 * n,)),
                  pltpu.SemaphoreType.DMA((n,)))
                 + tuple(hbm(a) for a in srcs) + tuple(hbm(a) for a in lands)
                 + (jax.ShapeDtypeStruct((8, LANE), F32),))
    out = pl.pallas_call(
        body, name=name, out_shape=out_shape,
        in_specs=[HBM_SPEC] * (ns + nl) + [ANY_SPEC] * len(after),
        out_specs=(SEM_SPEC, SEM_SPEC, SEM_SPEC) + (HBM_SPEC,) * (ns + nl)
        + (pl.BlockSpec(memory_space=pltpu.VMEM),),
        input_output_aliases={i: 3 + i for i in range(ns + nl)},
        compiler_params=pltpu.CompilerParams(has_side_effects=SIDE_EFFECT),
    )(*[pltpu.with_memory_space_constraint(a, pltpu.HBM) for a in tuple(srcs) + tuple(lands)],
      *after)
    _Order.tokens.append(out[-1])
    return out[0], out[1], out[2], out[3:3 + ns], out[3 + ns:3 + ns + nl]


def _split_wait(name, started, plan, relations=ALL_PEERS):
    send_sems, recv_sems, local_sems, srcs, lands = started
    ns, nl, n = len(srcs), len(lands), len(plan)
    after = [_Order.last] if _Order.last is not None else []

    def body(*refs):
        s_refs = refs[:ns]
        l_refs = refs[ns:ns + nl]
        send_sems, recv_sems, local_sems = refs[ns + nl:ns + nl + 3]
        me = _mesh_pos()
        for i in range(n):
            src, dst = _split_refs(plan, s_refs, l_refs, i, me, me)
            pltpu.make_async_copy(src, dst, local_sems.at[i]).wait()
        for k in relations:
            peer = _flip(me, k)
            for i in range(n):
                src, dst = _split_refs(plan, s_refs, l_refs, i, peer, peer)
                cp = pltpu.make_async_remote_copy(
                    src_ref=src, dst_ref=dst,
                    send_sem=send_sems.at[7 * i + k - 1], recv_sem=recv_sems.at[7 * i + k - 1],
                    device_id=peer, device_id_type=MESH)
                cp.wait_send()
                cp.wait_recv()

    hbm = lambda a: pltpu.HBM(a.shape, a.dtype)
    out = pl.pallas_call(
        body, name=name,
        out_shape=tuple(hbm(a) for a in srcs) + tuple(hbm(a) for a in lands),
        in_specs=[HBM_SPEC] * (ns + nl) + [SEM_SPEC] * 3 + [ANY_SPEC] * len(after),
        out_specs=(HBM_SPEC,) * (ns + nl),
        input_output_aliases={i: i for i in range(ns + nl)},
        compiler_params=pltpu.CompilerParams(has_side_effects=SIDE_EFFECT),
    )(*srcs, *lands, send_sems, recv_sems, local_sems, *after)
    return list(out[ns:])


def _forward_refs(plan, lands, i, block):
    _, _, _, li, lbase, lstride, rows = plan[i]
    return _rows_of(lands[li], lbase, lstride, rows, block)


def _forward_start(name, lands, plan):
    nl, n = len(lands), len(plan)

    def body(*refs):
        l_refs = refs[:nl]
        send_sems, recv_sems = refs[nl:nl + 2]
        token = refs[nl + 2 + nl]
        me = _mesh_pos()
        for j, kc in enumerate(OTHER_CHIPS):
            for i in range(n):
                rows = _forward_refs(plan, l_refs, i, _flip(me, kc))
                pltpu.make_async_remote_copy(
                    src_ref=rows, dst_ref=rows,
                    send_sem=send_sems.at[3 * i + j], recv_sem=recv_sems.at[3 * i + j],
                    device_id=_flip(me, 1), device_id_type=MESH).start()
        token[...] = jnp.zeros_like(token)

    hbm = lambda a: pltpu.HBM(a.shape, a.dtype)
    out = pl.pallas_call(
        body, name=name,
        out_shape=((pltpu.SemaphoreType.DMA((3 * n,)), pltpu.SemaphoreType.DMA((3 * n,)))
                   + tuple(hbm(a) for a in lands) + (jax.ShapeDtypeStruct((8, LANE), F32),)),
        in_specs=[HBM_SPEC] * nl,
        out_specs=(SEM_SPEC, SEM_SPEC) + (HBM_SPEC,) * nl
        + (pl.BlockSpec(memory_space=pltpu.VMEM),),
        input_output_aliases={i: 2 + i for i in range(nl)},
        compiler_params=pltpu.CompilerParams(has_side_effects=SIDE_EFFECT),
    )(*[pltpu.with_memory_space_constraint(a, pltpu.HBM) for a in lands])
    _Order.tokens.append(out[-1])
    return out[0], out[1], out[2:2 + nl]


def _forward_wait(name, started, plan):
    send_sems, recv_sems, lands = started
    nl, n = len(lands), len(plan)
    after = [_Order.last] if _Order.last is not None else []

    def body(*refs):
        l_refs = refs[:nl]
        send_sems, recv_sems = refs[nl:nl + 2]
        me = _mesh_pos()
        for j, kc in enumerate(OTHER_CHIPS):
            for i in range(n):
                sent = _forward_refs(plan, l_refs, i, _flip(me, kc))
                came = _forward_refs(plan, l_refs, i, _flip(_flip(me, 1), kc))
                cp = pltpu.make_async_remote_copy(
                    src_ref=sent, dst_ref=came,
                    send_sem=send_sems.at[3 * i + j], recv_sem=recv_sems.at[3 * i + j],
                    device_id=_flip(me, 1), device_id_type=MESH)
                cp.wait_send()
                cp.wait_recv()

    hbm = lambda a: pltpu.HBM(a.shape, a.dtype)
    out = pl.pallas_call(
        body, name=name, out_shape=tuple(hbm(a) for a in lands),
        in_specs=[HBM_SPEC] * nl + [SEM_SPEC] * 2 + [ANY_SPEC] * len(after),
        out_specs=(HBM_SPEC,) * nl,
        input_output_aliases={i: i for i in range(nl)},
        compiler_params=pltpu.CompilerParams(has_side_effects=SIDE_EFFECT),
    )(*lands, send_sems, recv_sems, *after)
    return list(out)


def _regroup_rows(name, padded, r, rp, lo, hi):
    d = padded.shape[1]
    pack = 4 // padded.dtype.itemsize
    assert r % pack == 0 and rp % ROW_ALIGN == 0 and lo % (8 * pack) == 0 and hi % (8 * pack) == 0
    r2, rp2, lo2, hi2 = r // pack, rp // pack, lo // pack, hi // pack
    u32 = jnp.uint32

    def body(x_ref, main_ref, cut_ref):
        x = pltpu.bitcast(x_ref[...], u32)
        joined = jnp.concatenate([x[rp2 * j:rp2 * j + r2] for j in range(NDEV)], axis=0)
        main = jnp.concatenate([joined[:lo2], joined[hi2:]], axis=0)
        cut = jnp.concatenate([joined[lo2:hi2], jnp.zeros((LANE // pack - (hi2 - lo2), LANE), u32)],
                              axis=0)
        main_ref[...] = pltpu.bitcast(main, padded.dtype)
        cut_ref[...] = pltpu.bitcast(cut, padded.dtype)

    return pl.pallas_call(
        body, name=name,
        out_shape=[jax.ShapeDtypeStruct((NDEV * r - (hi - lo), d), padded.dtype),
                   jax.ShapeDtypeStruct((LANE, d), padded.dtype)],
        grid=(d // LANE,),
        in_specs=[pl.BlockSpec((NDEV * rp, LANE), lambda i: (0, i))],
        out_specs=[pl.BlockSpec((NDEV * r - (hi - lo), LANE), lambda i: (0, i)),
                   pl.BlockSpec((LANE, LANE), lambda i: (0, i))],
        compiler_params=_params(("parallel",)),
    )(padded)


def _ungroup_rows(name, main, cut, r, rp, lo, hi):
    d = main.shape[1]
    pack = 4 // main.dtype.itemsize
    r2, rp2, lo2, hi2 = r // pack, rp // pack, lo // pack, hi // pack
    u32 = jnp.uint32

    def body(main_ref, cut_ref, o_ref):
        m = pltpu.bitcast(main_ref[...], u32)
        c = pltpu.bitcast(cut_ref[...], u32)
        joined = jnp.concatenate([m[:lo2], c[:hi2 - lo2], m[lo2:]], axis=0)
        zeros = jnp.zeros((rp2 - r2, LANE), u32)
        parts = []
        for j in range(NDEV):
            parts += [joined[r2 * j:r2 * (j + 1)], zeros]
        o_ref[...] = pltpu.bitcast(jnp.concatenate(parts, axis=0), main.dtype)

    return pl.pallas_call(
        body, name=name,
        out_shape=jax.ShapeDtypeStruct((NDEV * rp, d), main.dtype),
        grid=(d // LANE,),
        in_specs=[pl.BlockSpec((main.shape[0], LANE), lambda i: (0, i)),
                  pl.BlockSpec((LANE, LANE), lambda i: (0, i))],
        out_specs=pl.BlockSpec((NDEV * rp, LANE), lambda i: (0, i)),
        compiler_params=_params(("parallel",)),
    )(main, cut)


def _sum_slabs(name, a):
    s, r, c = a.shape

    def body(a_ref, o_ref):
        acc = a_ref[0].astype(F32)
        for i in range(1, s):
            acc = acc + a_ref[i].astype(F32)
        o_ref[...] = acc

    return pl.pallas_call(body, name=name, out_shape=jax.ShapeDtypeStruct((r, c), F32))(a)


def _adamw_update(g, w, m, v):
    mn = ADAM_B1 * m + (1.0 - ADAM_B1) * g
    vn = ADAM_B2 * v + (1.0 - ADAM_B2) * (g * g)
    m_hat = mn / (1.0 - ADAM_B1 ** ADAM_STEP)
    v_hat = vn / (1.0 - ADAM_B2 ** ADAM_STEP)
    delta = -ADAM_LR * (m_hat / (jnp.sqrt(v_hat) + ADAM_EPS) + ADAM_WD * w)
    return delta, mn, vn


def _adamw(name, gsrc, w, m, v, transposed=False, tr=256):
    s = gsrc.shape[0]
    r, c = w.shape
    step = LANE if transposed else 8
    tr = max(t for t in range(step, min(tr, r) + 1, step) if r % t == 0)

    def body(g_ref, w_ref, m_ref, v_ref, go_ref, d_ref, mo_ref, vo_ref):
        g = g_ref[0].astype(F32)
        for i in range(1, s):
            g = g + g_ref[i].astype(F32)
        if transposed:
            g = g.T[:, :c]
        delta, mn, vn = _adamw_update(g, w_ref[...], m_ref[...], v_ref[...])
        go_ref[...] = g
        d_ref[...] = delta
        mo_ref[...] = mn
        vo_ref[...] = vn

    blk = pl.BlockSpec((tr, c), lambda i: (i, 0))
    if transposed:
        g_spec = pl.BlockSpec((s, gsrc.shape[1], tr), lambda i: (0, 0, i))
    else:
        g_spec = pl.BlockSpec((s, tr, c), lambda i: (0, i, 0))
    return pl.pallas_call(
        body, name=name, out_shape=[jax.ShapeDtypeStruct(w.shape, F32)] * 4,
        grid=(r // tr,),
        in_specs=[g_spec, blk, blk, blk], out_specs=[blk] * 4,
        compiler_params=_params(("parallel",)),
    )(gsrc, w, m, v)


REPLICATED = ("ffn1_norm", "mix_norm", "q_norm", "k_norm", "conv_b", "dt_bias", "a_log",
              "d_skip", "ssd_norm", "ffn2_norm")
ALL_WEIGHTS = ("ffn1_norm", "ffn1_w_gate", "ffn1_w_up", "ffn1_w_down", "mix_norm", "w_in",
               "q_norm", "k_norm", "conv_w", "conv_b", "dt_bias", "a_log", "d_skip", "ssd_norm",
               "w_attn_branch", "w_ssd_branch", "w_out", "ffn2_norm", "ffn2_w_gate", "ffn2_w_up",
               "ffn2_w_down")
BIG = (("ffn1_w_gate", True, "gu1t", 0), ("ffn1_w_up", True, "gu1t", 1),
       ("ffn1_w_down", False, "d1", 0), ("w_in", True, "wint", 0),
       ("w_attn_branch", True, "abt", 0), ("w_ssd_branch", False, "sb", 0),
       ("w_out", False, "out", 0),
       ("ffn2_w_gate", True, "gu2t", 0), ("ffn2_w_up", True, "gu2t", 1),
       ("ffn2_w_down", False, "d2", 0))


def _nrows(shape, cols):
    return -(-math.prod(shape) // cols)


def _pack_rows(arrs, cols, row_tile):
    parts = []
    for a in arrs:
        flat = a.reshape(-1)
        nr = -(-flat.shape[0] // cols)
        parts.append(jnp.pad(flat, (0, nr * cols - flat.shape[0])).reshape(nr, cols))
    out = jnp.concatenate(parts, axis=0)
    return jnp.pad(out, ((0, _round_up(out.shape[0], row_tile) - out.shape[0]), (0, 0)))


def _unpack_rows(packed, shapes):
    cols = packed.shape[-1]
    out, r0 = [], 0
    for sh in shapes:
        nr = _nrows(sh, cols)
        out.append(packed[r0:r0 + nr].reshape(-1)[:math.prod(sh)].reshape(tuple(sh)))
        r0 += nr
    return out


def kernel(x, ffn1_norm, ffn1_w_gate, ffn1_w_up, ffn1_w_down, mix_norm, w_in, q_norm, k_norm, conv_w, conv_b, dt_bias, a_log, d_skip, ssd_norm, w_attn_branch, w_ssd_branch, w_out, ffn2_norm, ffn2_w_gate, ffn2_w_up, ffn2_w_down, loss_target, m_ffn1_norm, m_ffn1_w_gate, m_ffn1_w_up, m_ffn1_w_down, m_mix_norm, m_w_in, m_q_norm, m_k_norm, m_conv_w, m_conv_b, m_dt_bias, m_a_log, m_d_skip, m_ssd_norm, m_w_attn_branch, m_w_ssd_branch, m_w_out, m_ffn2_norm, m_ffn2_w_gate, m_ffn2_w_up, m_ffn2_w_down, v_ffn1_norm, v_ffn1_w_gate, v_ffn1_w_up, v_ffn1_w_down, v_mix_norm, v_w_in, v_q_norm, v_k_norm, v_conv_w, v_conv_b, v_dt_bias, v_a_log, v_d_skip, v_ssd_norm, v_w_attn_branch, v_w_ssd_branch, v_w_out, v_ffn2_norm, v_ffn2_w_gate, v_ffn2_w_up, v_ffn2_w_down):
    given = dict(locals())
    wts = {n: given[n] for n in ALL_WEIGHTS}
    mom = {n: given["m_" + n] for n in ALL_WEIGHTS}
    var = {n: given["v_" + n] for n in ALL_WEIGHTS}
    d = x.shape[-1]
    nh = dt_bias.shape[1]
    my = _dev_index(_mesh_pos())

    def row_form(n, col_sharded):
        a = wts[n][0].T if col_sharded else wts[n][0]
        a = jnp.pad(a, ((0, _round_up(a.shape[0], ROW_ALIGN) - a.shape[0]), (0, 0)))
        return a.astype(MXU_DTYPE)

    _Order.tokens, _Order.last = [], None
    shard = {n: row_form(n, cs) for n, cs, _, _ in BIG}
    entries = {buf: [e for e in BIG if e[2] == buf] for buf in dict.fromkeys(e[2] for e in BIG)}

    def buf_shape(buf):
        r, c = shard[entries[buf][0][0]].shape
        return (len(entries[buf]) * NDEV * r, c)

    def gather_plan(bufs):
        srcs, lands, plan = [], [], []
        for li, buf in enumerate(bufs):
            lands.append(lax.empty(buf_shape(buf), MXU_DTYPE))
            for n, _, _, pos in entries[buf]:
                r = shard[n].shape[0]
                plan.append((len(srcs), 0, 0, li, pos * NDEV * r, r, r))
                srcs.append(shard[n])
        return srcs, lands, plan

    def scatter_plan(bufs, grads):
        srcs, lands, plan, names = [], [], [], []
        for si, buf in enumerate(bufs):
            srcs.append(grads[buf])
            for n, _, _, pos in entries[buf]:
                r, c = shard[n].shape
                plan.append((si, pos * NDEV * r, r, len(lands), 0, r, r))
                lands.append(lax.empty((NDEV * r, c), MXU_DTYPE))
                names.append(n)
        return srcs, lands, plan, names

    first_bufs = ("gu1t", "d1")
    shards, dests, out_shapes = [], [], []
    for bi, buf in enumerate(first_bufs):
        out_shapes.append((buf_shape(buf), MXU_DTYPE))
        for n, _, _, pos in entries[buf]:
            r = shard[n].shape[0]
            shards.append(shard[n])
            dests.append((bi, pos * NDEV * r, r))
    conv_rows = _pack_rows([conv_w[0]], LANE, ROW_ALIGN)
    shards.append(conv_rows)
    dests.append((len(first_bufs), 0, conv_rows.shape[0]))
    out_shapes.append(((NDEV * conv_rows.shape[0], LANE), F32))
    gathered = _gather("gather_first", shards, dests, out_shapes)

    in_cols = w_in.shape[2]
    in_pad = _round_up(in_cols, ROW_ALIGN)
    dt_off = NDEV * in_cols - 2 * d - nh
    second_bufs = ("wint",)
    third_bufs = ("abt", "sb", "out", "gu2t", "d2")
    plan2 = gather_plan(second_bufs)
    started2 = _split_start("gather_in_start", *plan2, after=[gathered[0]],
                            relations=SAME_CORE_AND_SIBLING)
    forwarded, started3 = [], []

    class Weights(dict):
        def after_first_ffn(self):
            lands = _split_wait("gather_in_wait", started2, plan2[2],
                                relations=SAME_CORE_AND_SIBLING)
            forwarded.append(_forward_start("gather_in_forward", lands, plan2[2]))

        def __missing__(self, key):
            if key in ("maint", "dtt"):
                wint = _forward_wait("gather_in_arrive", forwarded[0], plan2[2])[0]
                plan3 = gather_plan(third_bufs)
                started3.append((_split_start("gather_rest_start", *plan3, after=[wint]), plan3[2]))
                self["maint"], self["dtt"] = _regroup_rows(
                    "regroup_w_in", wint, in_cols, in_pad, dt_off, dt_off + nh)
            else:
                st, plan = started3[0]
                for buf, a in zip(third_bufs, _split_wait("gather_rest_wait", st, plan)):
                    self[buf] = a
            return self[key]

    w = Weights(gu1t=gathered[0], d1=gathered[1])
    p = {n: wts[n] for n in REPLICATED}
    conv_all = gathered[-1].reshape(NDEV, conv_rows.shape[0] * LANE)[:, :math.prod(conv_w.shape[1:])]
    p["conv_w"] = (conv_all.reshape((NDEV,) + conv_w.shape[1:]).transpose(1, 0, 2)
                   .reshape(conv_w.shape[1], NDEV * conv_w.shape[2]))

    groups = (("scatter_late", ("gu2t", "d2", "out", "abt", "sb")),
              ("scatter_in", ("maint", "dtt")),
              ("scatter_first", ("gu1t", "d1")))
    in_flight = []

    class Grads(dict):
        def __setitem__(self, key, value):
            dict.__setitem__(self, key, value)
            for tag, need in groups:
                if key in need and all(k in self for k in need):
                    if tag == "scatter_in":
                        gwin = _ungroup_rows("ungroup_w_in", self["maint"], self["dtt"],
                                             in_cols, in_pad, dt_off, dt_off + nh)
                        bufs, grads = ("wint",), {"wint": gwin}
                    else:
                        bufs, grads = need, self
                    srcs, lands, plan, names = scatter_plan(bufs, grads)
                    in_flight.append((tag, _split_start(tag + "_start", srcs, lands, plan),
                                      plan, names))

    loss_vec, dx, gw, gp = _local_step(x[0], loss_target[0], w, p, Grads(),
                                       aw=w_attn_branch.shape[1])

    small_names = REPLICATED + ("conv_w",)
    small_shapes = [gp[n].shape for n in small_names]
    small = _pack_rows([gp[n] for n in small_names], LANE, ROW_ALIGN)
    small_plan = [(0, 0, 0, 0, 0, small.shape[0], small.shape[0])]
    small_started = _split_start("gather_small_start", [small],
                                 [lax.empty((NDEV * small.shape[0], LANE), F32)], small_plan)

    outs = [{}, {}, {}, {}]
    col_sharded_of = {n: cs for n, cs, _, _ in BIG}
    for tag, started, plan, names in in_flight:
        for n, rv in zip(names, _split_wait(tag + "_wait", started, plan)):
            rv = rv.reshape(NDEV, shard[n].shape[0], shard[n].shape[1])
            if col_sharded_of[n] and rv.shape[1] == wts[n].shape[2]:
                res = _adamw("adamw_" + n, rv, wts[n][0].T, mom[n][0].T, var[n][0].T)
                res = [a.T for a in res]
            else:
                res = _adamw("adamw_" + n, rv, wts[n][0], mom[n][0], var[n][0],
                             transposed=col_sharded_of[n])
            _Order.done(res)
            for k in range(4):
                outs[k][n] = res[k][None]

    small_all = _split_wait("gather_small_wait", small_started, small_plan)[0]
    small_all = small_all.reshape(NDEV, small.shape[0], LANE)
    small_g = _unpack_rows(_sum_slabs("sum_small_grads", small_all), small_shapes)
    small_g = dict(zip(small_names, small_g))
    cs = conv_w.shape[2]
    small_g["conv_w"] = lax.dynamic_slice_in_dim(small_g["conv_w"], my * cs, cs, axis=1)
    small_shard_shapes = [wts[n].shape[-2:] for n in small_names]
    sg = _pack_rows([small_g[n] for n in small_names], LANE, 8)
    sw = _pack_rows([wts[n] for n in small_names], LANE, 8)
    sm = _pack_rows([mom[n] for n in small_names], LANE, 8)
    sv = _pack_rows([var[n] for n in small_names], LANE, 8)
    res_small = _adamw("adamw_small", sg[None], sw, sm, sv, tr=sg.shape[0])
    for k in range(4):
        for n, a in zip(small_names, _unpack_rows(res_small[k], small_shard_shapes)):
            outs[k][n] = a.reshape(wts[n].shape)

    loss = lax.psum(0.5 * jnp.sum(loss_vec) / d, ("x", "y", "c"))
    result = [loss, dx[None]]
    for k in range(4):
        result += [outs[k][n] for n in ALL_WEIGHTS]
    return tuple(result)
```

```python
import functools
import math

import numpy as np
import jax
import jax.numpy as jnp
from jax import lax
from jax.experimental import pallas as pl
from jax.experimental.pallas import tpu as pltpu

F32 = jnp.float32
BF16 = jnp.bfloat16
MXU_DTYPE = BF16
ACT_DTYPE = BF16

NDEV = 8
EPS = 1e-6
HD = 64
QB = 128
PATTERNS = ((128, 1), (512, 4), (2048, 16))
ALIBI_MAX_EXP = 8.0
SSD_P = 64
SSD_N = 128
SSD_G = 4
SSD_Q = 128
SSD_K = 4
NEG = -1e30
LANE = 128
ROW_ALIGN = 16
VMEM_LIMIT = 56 * 1024 * 1024
TILE_CAP = 11 * LANE
ROW_TILE_CAP = 8 * LANE
WIDE_N_CAP = 23 * LANE

ADAM_LR, ADAM_B1, ADAM_B2, ADAM_EPS, ADAM_WD, ADAM_STEP = 0.001, 0.9, 0.999, 1e-8, 0.01, 10

NN = (((1,), (0,)), ((), ()))
NT = (((1,), (1,)), ((), ()))
TN = (((0,), (0,)), ((), ()))


BNN = (((2,), (1,)), ((0,), (0,)))
BNT = (((2,), (2,)), ((0,), (0,)))
BTN = (((1,), (1,)), ((0,), (0,)))
_DOT_GRADS = {
    NN: (("g", "b", NT), ("a", "g", TN)),
    NT: (("g", "b", NN), ("g", "a", TN)),
    TN: (("b", "g", NT), ("a", "g", NN)),
    BNT: (("g", "b", BNN), ("g", "a", BTN)),
    BTN: (("b", "g", BNT), ("a", "g", BNN)),
}


def _mxu(a, b, dims):
    return lax.dot_general(a.astype(MXU_DTYPE), b.astype(MXU_DTYPE), dims,
                           preferred_element_type=F32)


@functools.partial(jax.custom_vjp, nondiff_argnums=(2,))
def _dot_vjp(a, b, dims):
    return _mxu(a, b, dims)


def _dot_vjp_fwd(a, b, dims):
    return _mxu(a, b, dims), (a.astype(MXU_DTYPE), b.astype(MXU_DTYPE))


def _dot_vjp_bwd(dims, res, g):
    ops = {"a": res[0], "b": res[1], "g": g}
    (x1, y1, d1), (x2, y2, d2) = _DOT_GRADS[dims]
    return _mxu(ops[x1], ops[y1], d1), _mxu(ops[x2], ops[y2], d2)


_dot_vjp.defvjp(_dot_vjp_fwd, _dot_vjp_bwd)


def _dot(a, b, dims=NN):
    return _dot_vjp(a, b, dims)


def _split3(a):
    hi = a.astype(BF16)
    r = a - hi.astype(F32)
    mid = r.astype(BF16)
    lo = (r - mid.astype(F32)).astype(BF16)
    return hi, mid, lo


def _dot3(a, b, dims=NN, split=0):
    if split == 0:
        bb = b.astype(BF16)
        parts = [lax.dot_general(s, bb, dims, preferred_element_type=F32) for s in _split3(a)]
    else:
        aa = a.astype(BF16)
        parts = [lax.dot_general(aa, s, dims, preferred_element_type=F32) for s in _split3(b)]
    return parts[0] + parts[1] + parts[2]


@jax.custom_vjp
def _spread(v, e):
    return _dot3(v, e)


def _spread_fwd(v, e):
    return _dot3(v, e), e


def _spread_bwd(e, g):
    return _dot3(g, e, NT), jnp.zeros_like(e)


_spread.defvjp(_spread_fwd, _spread_bwd)


@jax.custom_vjp
def _running_sum(a, lower):
    return _dot3(lower, a, NN, split=1)


def _running_sum_fwd(a, lower):
    return _dot3(lower, a, NN, split=1), lower


def _running_sum_bwd(lower, g):
    return _dot3(lower, g, TN, split=1), jnp.zeros_like(lower)


_running_sum.defvjp(_running_sum_fwd, _running_sum_bwd)


def _tile(n, cap):
    if n <= cap:
        return n
    best = None
    for t in range(LANE, cap + 1, LANE):
        if n % t == 0:
            best = t
    assert best is not None, (n, cap)
    return best


def _params(sem):
    return pltpu.CompilerParams(dimension_semantics=sem, vmem_limit_bytes=VMEM_LIMIT)


def _round_up(n, m):
    return -(-n // m) * m


class _Order:
    tokens = []
    last = None

    @classmethod
    def take(cls):
        out, cls.tokens = cls.tokens, []
        return out

    @classmethod
    def done(cls, result):
        cls.last = result[0] if isinstance(result, (list, tuple)) else result
        return result


ANY_SPEC = pl.BlockSpec(memory_space=pl.ANY)


def _mm(name, a, b, mode, out_dtype=F32, res=None, scale=1.0,
        cap_m=TILE_CAP, cap_n=TILE_CAP, cap_k=TILE_CAP):
    segs = list(a) if isinstance(a, (list, tuple)) else [a]
    nseg = len(segs)
    if mode == "tn":
        k = segs[0].shape[0]
        widths = [s.shape[1] for s in segs]
        m = sum(widths)
        k2, n = b.shape
        tm = _tile(math.gcd(*widths), cap_m)
        tk = _tile(k, cap_k)
        counts = [wd // tm for wd in widths]
    else:
        m = segs[0].shape[0]
        widths = [s.shape[1] for s in segs]
        k = sum(widths)
        (k2, n) = b.shape if mode == "nn" else b.shape[::-1]
        tm = _tile(m, cap_m)
        tk = _tile(math.gcd(*widths), cap_k)
        counts = [wd // tk for wd in widths]
    assert k == k2, (name, [s.shape for s in segs], b.shape, mode)
    tn = _tile(n, cap_n)
    nk = k // tk
    starts = [sum(counts[:s]) for s in range(nseg)]
    dims = {"nn": NN, "nt": NT, "tn": TN}[mode]

    def a_spec(s):
        lo, cnt = starts[s], counts[s]
        if mode == "tn":
            if nseg == 1:
                return pl.BlockSpec((tk, tm), lambda i, j, kk: (kk, i))
            return pl.BlockSpec(
                (tk, tm), lambda i, j, kk: (jnp.where((i >= lo) & (i < lo + cnt), kk, 0),
                                            jnp.clip(i - lo, 0, cnt - 1)))
        if nseg == 1:
            return pl.BlockSpec((tm, tk), lambda i, j, kk: (i, kk))
        return pl.BlockSpec((tm, tk), lambda i, j, kk: (i, jnp.clip(kk - lo, 0, cnt - 1)))

    b_spec = (pl.BlockSpec((tn, tk), lambda i, j, kk: (j, kk)) if mode == "nt"
              else pl.BlockSpec((tk, tn), lambda i, j, kk: (kk, j)))
    o_spec = pl.BlockSpec((tm, tn), lambda i, j, kk: (i, j))
    has_res = res is not None
    use_acc = nk > 1 or nseg > 1
    ties = _Order.take()
    nt_ = len(ties)

    def body(*refs):
        a_refs = refs[:nseg]
        b_ref = refs[nseg]
        r_ref = refs[nseg + 1] if has_res else None
        o_ref = refs[nseg + 1 + has_res + nt_]
        scr = refs[nseg + 2 + has_res + nt_:]

        def finish(acc):
            if scale != 1.0:
                acc = acc * scale
            if has_res:
                acc = r_ref[...].astype(F32) + acc
            o_ref[...] = acc.astype(o_ref.dtype)

        if not use_acc:
            finish(_dot(a_refs[0][...], b_ref[...], dims))
            return
        acc_ref = scr[0]
        kk = pl.program_id(2)
        sel = pl.program_id(0) if mode == "tn" else kk

        @pl.when(kk == 0)
        def _():
            acc_ref[...] = jnp.zeros_like(acc_ref)

        for s in range(nseg):
            def add(s=s):
                acc_ref[...] += _dot(a_refs[s][...], b_ref[...], dims)
            if nseg == 1:
                add()
            else:
                pl.when((sel >= starts[s]) & (sel < starts[s] + counts[s]))(add)

        @pl.when(kk == nk - 1)
        def _():
            finish(acc_ref[...])

    in_specs = ([a_spec(s) for s in range(nseg)] + [b_spec] + ([o_spec] if has_res else [])
                + [ANY_SPEC] * nt_)
    args = tuple(segs) + (b,) + ((res,) if has_res else ()) + tuple(ties)
    return _Order.done(pl.pallas_call(
        body, name=name,
        out_shape=jax.ShapeDtypeStruct((m, n), out_dtype),
        grid=(m // tm, n // tn, nk),
        in_specs=in_specs, out_specs=o_spec,
        scratch_shapes=[pltpu.VMEM((tm, tn), F32)] if use_acc else [],
        compiler_params=_params(("parallel", "parallel", "arbitrary")),
    )(*args))


def _act(g, u):
    return _silu(g.astype(F32)) * u.astype(F32)


def _ffn_up(name, h, wgut, cap_m=ROW_TILE_CAP, cap_n=TILE_CAP):
    m, k = h.shape
    dff = wgut.shape[0] // 2
    tm, tn = _tile(m, cap_m), _tile(dff, cap_n)
    nj = dff // tn
    ties = _Order.take()

    def body(h_ref, wg_ref, wu_ref, *rest):
        g_ref, u_ref, a_ref = rest[len(ties):]
        hv = h_ref[...]
        g = _dot(hv, wg_ref[...], NT)
        u = _dot(hv, wu_ref[...], NT)
        g_ref[...] = g.astype(g_ref.dtype)
        u_ref[...] = u.astype(u_ref.dtype)
        a_ref[...] = _act(g, u).astype(a_ref.dtype)

    o_spec = pl.BlockSpec((tm, tn), lambda i, j: (i, j))
    return _Order.done(pl.pallas_call(
        body, name=name, out_shape=[jax.ShapeDtypeStruct((m, dff), ACT_DTYPE)] * 3,
        grid=(m // tm, nj),
        in_specs=[pl.BlockSpec((tm, k), lambda i, j: (i, 0)),
                  pl.BlockSpec((tn, k), lambda i, j: (j, 0)),
                  pl.BlockSpec((tn, k), lambda i, j: (nj + j, 0))] + [ANY_SPEC] * len(ties),
        out_specs=[o_spec] * 3,
        compiler_params=_params(("parallel", "parallel")),
    )(h, wgut, wgut, *ties))


def _ffn_dact(name, dxo, wd, g, u, scale, cap_m=ROW_TILE_CAP, cap_n=TILE_CAP):
    m, k = dxo.shape
    dff = wd.shape[0]
    tm, tn = _tile(m, cap_m), _tile(dff, cap_n)
    ties = _Order.take()

    def body(d_ref, w_ref, g_ref, u_ref, *rest):
        dg_ref, du_ref = rest[len(ties):]
        da = _dot(d_ref[...], w_ref[...], NT) * scale
        _, vjp = jax.vjp(_act, g_ref[...], u_ref[...])
        dg, du = vjp(da)
        dg_ref[...] = dg.astype(dg_ref.dtype)
        du_ref[...] = du.astype(du_ref.dtype)

    o_spec = pl.BlockSpec((tm, tn), lambda i, j: (i, j))
    return _Order.done(pl.pallas_call(
        body, name=name, out_shape=[jax.ShapeDtypeStruct((m, dff), ACT_DTYPE)] * 2,
        grid=(m // tm, dff // tn),
        in_specs=[pl.BlockSpec((tm, k), lambda i, j: (i, 0)),
                  pl.BlockSpec((tn, k), lambda i, j: (j, 0)), o_spec, o_spec]
        + [ANY_SPEC] * len(ties),
        out_specs=[o_spec] * 2,
        compiler_params=_params(("parallel", "parallel")),
    )(dxo, wd, g, u, *ties))


def _rw(name, fn, ins, outs, accs=(), tr=512, ncb=1):
    t = next(a.shape[0] for kind, a, _, _ in ins if kind == "row")
    assert t % tr == 0
    n_in = len(ins)
    n_pieces = sum(len(w) for w, _ in outs)

    def spec(kind, arr, width, base):
        if kind == "row":
            return pl.BlockSpec((tr, width), lambda j, i: (i, base + j))
        return pl.BlockSpec((arr.shape[0], width), lambda j, i: (0, base + j))

    in_specs = [spec(*s) for s in ins]
    out_shapes, out_specs = [], []
    for widths, dt in outs:
        w = sum(widths)
        out_shapes.append(jax.ShapeDtypeStruct((t, w * ncb), dt))
        out_specs.append(pl.BlockSpec((tr, w), lambda j, i: (i, j)))
    for rows, width in accs:
        out_shapes.append(jax.ShapeDtypeStruct((rows, width * ncb), F32))
        out_specs.append(pl.BlockSpec((rows, width), lambda j, i: (0, j)))

    ties = _Order.take()
    nt_ = len(ties)
    in_specs = in_specs + [ANY_SPEC] * nt_

    def body(*refs):
        vals = [r[...] for r in refs[:n_in]]
        res = fn(*vals)
        o_refs = refs[n_in + nt_:n_in + nt_ + len(outs)]
        a_refs = refs[n_in + nt_ + len(outs):]
        p = 0
        for (widths, _), o_ref in zip(outs, o_refs):
            off = 0
            for w in widths:
                if len(widths) == 1:
                    o_ref[...] = res[p].astype(o_ref.dtype)
                else:
                    o_ref[:, off:off + w] = res[p].astype(o_ref.dtype)
                off += w
                p += 1
        i = pl.program_id(1)
        for a_ref, v in zip(a_refs, res[n_pieces:]):
            @pl.when(i == 0)
            def _(a_ref=a_ref, v=v):
                a_ref[...] = v

            @pl.when(i > 0)
            def _(a_ref=a_ref, v=v):
                a_ref[...] += v

    return _Order.done(pl.pallas_call(
        body, name=name, out_shape=out_shapes,
        grid=(ncb, t // tr), in_specs=in_specs, out_specs=out_specs,
        compiler_params=_params(("parallel", "arbitrary")),
    )(*[a for _, a, _, _ in ins], *ties))


def _rms(x, g):
    x = x.astype(F32)
    return x * lax.rsqrt(jnp.mean(x * x, axis=-1, keepdims=True) + EPS) * g


def _silu(x):
    return x * jax.nn.sigmoid(x)


def _colsum(v):
    return jnp.sum(v, axis=0, keepdims=True)


def _pair_norm(x, g):
    w = 2 * HD
    ri = lax.broadcasted_iota(jnp.int32, (w, w), 0)
    ci = lax.broadcasted_iota(jnp.int32, (w, w), 1)
    same_head = ((ri < HD) == (ci < HD)).astype(F32)
    ms = _spread(x * x, same_head) * (1.0 / HD)
    return x * lax.rsqrt(ms + EPS) * g


ATTN_SCALE = 1.0 / math.sqrt(HD)


def _attn_bias(coef):
    key = lax.broadcasted_iota(jnp.int32, (QB, QB), 0)
    qry = lax.broadcasted_iota(jnp.int32, (QB, QB), 1)
    dist = (qry - key).astype(F32)
    own = jnp.where(qry >= key, -coef * dist, NEG)
    prev = jnp.where(qry <= key, -coef * (dist + float(QB)), NEG)
    return own, prev


def _attn_pair(qn, kcn, kpn, vc, vp, b_own, b_prev):
    nb = qn.shape[0]
    w = 2 * HD
    lane = lax.broadcasted_iota(jnp.int32, (1, 1, w), 2)
    eye = (lax.broadcasted_iota(jnp.int32, (QB, QB), 0)
           == lax.broadcasted_iota(jnp.int32, (QB, QB), 1)).astype(F32)
    out = jnp.zeros((nb, QB, w), F32)
    lb = jnp.zeros((nb * QB, w), F32)
    for hh in range(2):
        mask = ((lane < HD) if hh == 0 else (lane >= HD)).astype(F32)
        qm = qn * mask
        lc = _dot(kcn, qm, BNT) + b_own[hh]
        lp = _dot(kpn, qm, BNT) + b_prev[hh]
        m = lax.stop_gradient(jnp.maximum(jnp.max(lc, axis=1, keepdims=True),
                                          jnp.max(lp, axis=1, keepdims=True)))
        pc = jnp.exp(lc - m)
        pp = jnp.exp(lp - m)
        l = jnp.sum(pc, axis=1, keepdims=True) + jnp.sum(pp, axis=1, keepdims=True)
        inv = 1.0 / l
        out = out + (_dot(pc * inv, vc, BTN) + _dot(pp * inv, vp, BTN)) * mask
        diag = (eye * (m + jnp.log(l))).reshape(nb * QB, QB)
        lb = lb + _spread(diag, jnp.broadcast_to(mask[0], (QB, w)))
    return out, lb.reshape(nb, QB, w)


NORM_ROWS = 128
NORM_UNROLL = 4
EPILOGUE_ROWS = 512
ATTN_BATCH_FWD = 16
ATTN_BATCH_BWD = 8


def _unit_rows(u, d):
    r = u & (d - 1)
    n = u >> (d.bit_length() - 1)

    def rows(blk):
        start = pl.multiple_of(blk * (QB * d), QB * d)
        return pl.ds(start, QB) if d == 1 else pl.ds(start + r, QB, stride=d)

    return rows(n), rows(jnp.maximum(n - 1, 0)), n == 0


def _unit_batch(i, nbatch, d, bias, qf, kf, vf):
    units = [_unit_rows(i * nbatch + j, d) for j in range(nbatch)]
    cur = lambda ref: jnp.stack([ref[c, :] for c, _, _ in units])
    prv = lambda ref: jnp.stack([ref[p, :] for _, p, _ in units])
    b_own = [b[0] for b in bias]
    b_prev = [jnp.stack([jnp.where(first, NEG, b[1]) for _, _, first in units]) for b in bias]
    return units, (cur(qf), cur(kf), prv(kf), cur(vf), prv(vf), b_own, b_prev)


def _q_norm(x, g):
    return _pair_norm(x, g * ATTN_SCALE)


def _attn_prologue(t, q_ref, k_ref, v_ref, qg_ref, kg_ref, qf, kf, vf):
    def chunk(c, carry):
        rows = pl.ds(pl.multiple_of(c * NORM_ROWS, NORM_ROWS), NORM_ROWS)
        qf[rows, :] = _q_norm(q_ref[rows, :].astype(F32), qg_ref[...])
        kf[rows, :] = _pair_norm(k_ref[rows, :].astype(F32), kg_ref[...])
        vf[rows, :] = v_ref[rows, :].astype(F32)
        return carry
    lax.fori_loop(0, t // NORM_ROWS, chunk, 0, unroll=NORM_UNROLL)


def _attn_specs(t, bases):
    w = 2 * HD
    ins = [pl.BlockSpec((t, w), functools.partial(lambda p, c, b: (0, b + p), b=b)) for b in bases]
    gain = pl.BlockSpec((1, w), lambda p, c: (0, 0))
    blk = pl.BlockSpec((t, w), lambda p, c: (0, p))
    return ins, gain, blk


def _attn_fwd(name, proj, bases, qg, kg, coefs, d):
    t = proj.shape[0]
    npairs = coefs.shape[0] // 2
    w = 2 * HD
    ins, gain, blk = _attn_specs(t, bases)

    def body(coef_ref, q_ref, k_ref, v_ref, qg_ref, kg_ref, o_ref, l_ref, qf, kf, vf):
        p = pl.program_id(0)
        bias = (_attn_bias(coef_ref[2 * p]), _attn_bias(coef_ref[2 * p + 1]))
        _attn_prologue(t, q_ref, k_ref, v_ref, qg_ref, kg_ref, qf, kf, vf)

        def step(i, carry):
            units, ins = _unit_batch(i, ATTN_BATCH_FWD, d, bias, qf, kf, vf)
            o, lb = _attn_pair(*ins)
            for j, (cur, _, _) in enumerate(units):
                o_ref[cur, :] = o[j]
                l_ref[cur, :] = lb[j]
            return carry

        lax.fori_loop(0, t // QB // ATTN_BATCH_FWD, step, 0)

    return pl.pallas_call(
        body, name=name,
        out_shape=[jax.ShapeDtypeStruct((t, npairs * w), F32)] * 2,
        grid_spec=pltpu.PrefetchScalarGridSpec(
            num_scalar_prefetch=1, grid=(npairs,),
            in_specs=ins + [gain, gain], out_specs=[blk, blk],
            scratch_shapes=[pltpu.VMEM((t, w), F32)] * 3),
        compiler_params=_params(("arbitrary",)),
    )(coefs, proj, proj, proj, qg, kg)


def _attn_bwd(name, proj, bases, qg, kg, coefs, d, do, dl):
    t = proj.shape[0]
    npairs = coefs.shape[0] // 2
    w = 2 * HD
    ins, gain, blk = _attn_specs(t, bases)

    def body(coef_ref, q_ref, k_ref, v_ref, qg_ref, kg_ref, do_ref, dl_ref,
             dq_ref, dk_ref, dv_ref, dqg_ref, dkg_ref, qf, kf, vf, dqf, dkf, dvf):
        p = pl.program_id(0)
        bias = (_attn_bias(coef_ref[2 * p]), _attn_bias(coef_ref[2 * p + 1]))
        _attn_prologue(t, q_ref, k_ref, v_ref, qg_ref, kg_ref, qf, kf, vf)
        dkf[...] = jnp.zeros_like(dkf)
        dvf[...] = jnp.zeros_like(dvf)

        def step(i, carry):
            units, ins = _unit_batch(i, ATTN_BATCH_BWD, d, bias, qf, kf, vf)
            f = lambda a, b, c, e, g: _attn_pair(a, b, c, e, g, *ins[5:])
            _, vjp = jax.vjp(f, *ins[:5])
            cot = (jnp.stack([do_ref[cur, :] for cur, _, _ in units]),
                   jnp.stack([dl_ref[cur, :] for cur, _, _ in units]))
            dq, dkc, dkp, dvc, dvp = vjp(cot)
            for j, (cur, prv, _) in enumerate(units):
                dqf[cur, :] = dq[j]
                dkf[cur, :] += dkc[j]
                dkf[prv, :] += dkp[j]
                dvf[cur, :] += dvc[j]
                dvf[prv, :] += dvp[j]
            return carry

        lax.fori_loop(0, t // QB // ATTN_BATCH_BWD, step, 0)

        def chunk(c, carry):
            dqg_acc, dkg_acc = carry
            rows = pl.ds(pl.multiple_of(c * EPILOGUE_ROWS, EPILOGUE_ROWS), EPILOGUE_ROWS)
            _, vq = jax.vjp(_q_norm, q_ref[rows, :].astype(F32), qg_ref[...])
            dq, dqg = vq(dqf[rows, :])
            _, vk = jax.vjp(_pair_norm, k_ref[rows, :].astype(F32), kg_ref[...])
            dk, dkg = vk(dkf[rows, :])
            dq_ref[rows, :] = dq.astype(dq_ref.dtype)
            dk_ref[rows, :] = dk.astype(dk_ref.dtype)
            dv_ref[rows, :] = dvf[rows, :].astype(dv_ref.dtype)
            return dqg_acc + dqg, dkg_acc + dkg

        zero = jnp.zeros((1, w), F32)
        dqg, dkg = lax.fori_loop(0, t // EPILOGUE_ROWS, chunk, (zero, zero))

        @pl.when(p == 0)
        def _():
            dqg_ref[...] = dqg
            dkg_ref[...] = dkg

        @pl.when(p > 0)
        def _():
            dqg_ref[...] += dqg
            dkg_ref[...] += dkg

    big = jax.ShapeDtypeStruct((t, npairs * w), ACT_DTYPE)
    small = jax.ShapeDtypeStruct((1, w), F32)
    return pl.pallas_call(
        body, name=name,
        out_shape=[big, big, big, small, small],
        grid_spec=pltpu.PrefetchScalarGridSpec(
            num_scalar_prefetch=1, grid=(npairs,),
            in_specs=ins + [gain, gain, blk, blk],
            out_specs=[blk, blk, blk, gain, gain],
            scratch_shapes=[pltpu.VMEM((t, w), F32)] * 6),
        compiler_params=_params(("arbitrary",)),
    )(coefs, proj, proj, proj, qg, kg, do, dl)


CONV_ROWS = 256
CONV_HALO = 8


def _rows_back(x, s):
    return x if s == 0 else pltpu.roll(x, s, 0)


def _rows_ahead(x, s):
    return x if s == 0 else pltpu.roll(x, x.shape[0] - s, 0)


def _conv_pre(u, w, b):
    y = b
    for kk in range(SSD_K):
        y = y + w[kk:kk + 1, :] * _rows_back(u, SSD_K - 1 - kk)
    return y


def _stage_padded(dst, src_ref, t):
    zeros = jnp.zeros((CONV_HALO, dst.shape[1]), F32)
    dst[0:CONV_HALO, :] = zeros
    dst[t + CONV_HALO:t + 2 * CONV_HALO, :] = zeros
    dst[CONV_HALO:t + CONV_HALO, :] = src_ref[...].astype(F32)


def _chunk_rows(c):
    r0 = pl.multiple_of(c * CONV_ROWS, CONV_ROWS)
    return pl.ds(r0, CONV_ROWS + 2 * CONV_HALO), pl.ds(r0, CONV_ROWS)


def _conv_fwd(name, src, base, w, b, cw=128):
    t = src.shape[0]
    c = w.shape[1]
    centre = slice(CONV_HALO, CONV_HALO + CONV_ROWS)

    def body(u_ref, w_ref, b_ref, o_ref, up):
        _stage_padded(up, u_ref, t)
        wv, bv = w_ref[...], b_ref[...]

        def chunk(ci, carry):
            ext, rows = _chunk_rows(ci)
            y = _conv_pre(up[ext, :], wv, bv)
            o_ref[rows, :] = _silu(y)[centre].astype(o_ref.dtype)
            return carry

        lax.fori_loop(0, t // CONV_ROWS, chunk, 0)

    return pl.pallas_call(
        body, name=name, out_shape=jax.ShapeDtypeStruct((t, c), ACT_DTYPE),
        grid=(c // cw,),
        in_specs=[pl.BlockSpec((t, cw), lambda j: (0, base + j)),
                  pl.BlockSpec((SSD_K, cw), lambda j: (0, j)),
                  pl.BlockSpec((1, cw), lambda j: (0, j))],
        out_specs=pl.BlockSpec((t, cw), lambda j: (0, j)),
        scratch_shapes=[pltpu.VMEM((t + 2 * CONV_HALO, cw), F32)],
        compiler_params=_params(("parallel",)),
    )(src, w, b)


def _conv_bwd(name, src, base, w, b, dout, cw=128):
    t = src.shape[0]
    c = w.shape[1]

    centre = slice(CONV_HALO, CONV_HALO + CONV_ROWS)

    def body(u_ref, w_ref, b_ref, d_ref, du_ref, dw_ref, db_ref, up, dp):
        _stage_padded(up, u_ref, t)
        _stage_padded(dp, d_ref, t)
        wv, bv = w_ref[...], b_ref[...]

        def chunk(ci, carry):
            dws, db = carry
            ext, rows = _chunk_rows(ci)
            u = up[ext, :]
            y = _conv_pre(u, wv, bv)
            sg = jax.nn.sigmoid(y)
            dy = dp[ext, :] * (sg * (1.0 + y * (1.0 - sg)))
            du = jnp.zeros_like(u)
            new_dws = []
            for kk in range(SSD_K):
                s = SSD_K - 1 - kk
                du = du + wv[kk:kk + 1, :] * _rows_ahead(dy, s)
                new_dws.append(dws[kk] + _colsum((dy * _rows_back(u, s))[centre]))
            du_ref[rows, :] = du[centre].astype(du_ref.dtype)
            return tuple(new_dws), db + _colsum(dy[centre])

        zero = jnp.zeros((1, cw), F32)
        dws, db = lax.fori_loop(0, t // CONV_ROWS, chunk, ((zero,) * SSD_K, zero))
        for kk in range(SSD_K):
            dw_ref[kk:kk + 1, :] = dws[kk]
        db_ref[...] = db

    return pl.pallas_call(
        body, name=name,
        out_shape=[jax.ShapeDtypeStruct((t, c), ACT_DTYPE),
                   jax.ShapeDtypeStruct((SSD_K, c), F32),
                   jax.ShapeDtypeStruct((1, c), F32)],
        grid=(c // cw,),
        in_specs=[pl.BlockSpec((t, cw), lambda j: (0, base + j)),
                  pl.BlockSpec((SSD_K, cw), lambda j: (0, j)),
                  pl.BlockSpec((1, cw), lambda j: (0, j)),
                  pl.BlockSpec((t, cw), lambda j: (0, j))],
        out_specs=[pl.BlockSpec((t, cw), lambda j: (0, j)),
                   pl.BlockSpec((SSD_K, cw), lambda j: (0, j)),
                   pl.BlockSpec((1, cw), lambda j: (0, j))],
        scratch_shapes=[pltpu.VMEM((t + 2 * CONV_HALO, cw), F32)] * 2,
        compiler_params=_params(("parallel",)),
    )(src, w, b, dout)


def _softplus(x):
    return jnp.maximum(x, 0.0) + jnp.log(1.0 + jnp.exp(-jnp.abs(x)))


def _ssd_chunk(xbc, dtraw, bias, alog, states):
    wd = states[0].shape[1]
    nj = wd // SSD_P
    inner = SSD_G * wd
    dt = _softplus(dtraw + bias)
    a = dt * (-jnp.exp(alog))
    li = lax.broadcasted_iota(jnp.int32, (SSD_Q, SSD_Q), 0)
    si = lax.broadcasted_iota(jnp.int32, (SSD_Q, SSD_Q), 1)
    causal = li >= si
    acs = _running_sum(a, causal.astype(F32))
    acs_t = acs.T
    a_last = acs[SSD_Q - 1:SSD_Q, :]
    grow = jnp.exp(acs)
    shrink = jnp.exp(a_last - acs)
    hrow = lax.broadcasted_iota(jnp.int32, (LANE, wd), 0)
    wcol = lax.broadcasted_iota(jnp.int32, (LANE, wd), 1)
    lane = lax.broadcasted_iota(jnp.int32, (1, LANE), 1)
    ys, snext = [], []
    for g in range(SSD_G):
        lo = (hrow - g * nj) * SSD_P
        head_lanes = jnp.logical_and(wcol >= lo, wcol < lo + SSD_P).astype(F32)
        xs = xbc[:, g * wd:(g + 1) * wd]
        bm = xbc[:, inner + g * SSD_N:inner + (g + 1) * SSD_N]
        cm = xbc[:, inner + (SSD_G + g) * SSD_N:inner + (SSD_G + g + 1) * SSD_N]
        xdt = xs * _dot(dt, head_lanes)
        grow_x = _spread(grow, head_lanes)
        y_off = _dot(cm, states[g]) * grow_x
        s_new = (states[g] * grow_x[SSD_Q - 1:SSD_Q, :]
                 + _dot(bm, xdt * _dot(shrink, head_lanes), TN))
        cb = _dot(cm, bm, NT)
        pieces = []
        for i in range(wd // LANE):
            xp = xdt[:, i * LANE:(i + 1) * LANE]
            acc = jnp.zeros((SSD_Q, LANE), F32)
            for hh in range(LANE // SSD_P):
                h = g * nj + i * (LANE // SSD_P) + hh
                decay = jnp.exp(jnp.where(causal, acs[:, h:h + 1] - acs_t[h:h + 1, :], NEG))
                keep = jnp.logical_and(lane >= hh * SSD_P, lane < (hh + 1) * SSD_P).astype(F32)
                acc = acc + _dot(cb * decay, xp * keep)
            pieces.append(acc)
        y_diag = pieces[0] if len(pieces) == 1 else jnp.concatenate(pieces, axis=1)
        ys.append(y_diag + y_off)
        snext.append(s_new)
    return ys, snext


def _ssd_specs(cdim, wd, rev, nc):
    ch = (lambda c: nc - 1 - c) if rev else (lambda c: c)
    full = lambda width: pl.BlockSpec((SSD_Q, width), lambda c: (ch(c), 0))
    vec = pl.BlockSpec((1, LANE), lambda c: (0, 0))
    st = pl.BlockSpec((1, SSD_G, SSD_N, wd), lambda c: (ch(c), 0, 0, 0))
    return full, vec, st


def _ssd_fwd(name, xbc, dtraw, bias, alog, inner):
    t, cdim = xbc.shape
    wd = inner // SSD_G
    nc = t // SSD_Q
    full, vec, st = _ssd_specs(cdim, wd, False, nc)

    def body(x_ref, r_ref, b_ref, a_ref, y_ref, st_ref, s_scr):
        @pl.when(pl.program_id(0) == 0)
        def _():
            s_scr[...] = jnp.zeros_like(s_scr)

        sprev = [s_scr[g] for g in range(SSD_G)]
        ys, snext = _ssd_chunk(x_ref[...].astype(F32), r_ref[...], b_ref[...], a_ref[...], sprev)
        for g in range(SSD_G):
            st_ref[0, g] = sprev[g]
            y_ref[:, g * wd:(g + 1) * wd] = ys[g]
            s_scr[g] = snext[g]

    return pl.pallas_call(
        body, name=name,
        out_shape=[jax.ShapeDtypeStruct((t, inner), F32),
                   jax.ShapeDtypeStruct((nc, SSD_G, SSD_N, wd), F32)],
        grid=(nc,),
        in_specs=[full(cdim), full(LANE), vec, vec],
        out_specs=[full(inner), st],
        scratch_shapes=[pltpu.VMEM((SSD_G, SSD_N, wd), F32)],
        compiler_params=_params(("arbitrary",)),
    )(xbc, dtraw, bias, alog)


def _ssd_bwd(name, xbc, dtraw, bias, alog, states, dy, dxs_extra):
    t, cdim = xbc.shape
    inner = dy.shape[1]
    wd = inner // SSD_G
    nc = t // SSD_Q
    full, vec, st = _ssd_specs(cdim, wd, True, nc)

    def body(x_ref, r_ref, b_ref, a_ref, st_ref, dy_ref, dx0_ref,
             dx_ref, dr_ref, db_ref, da_ref, ds_scr):
        first = pl.program_id(0) == 0

        @pl.when(first)
        def _():
            ds_scr[...] = jnp.zeros_like(ds_scr)

        sprev = [st_ref[0, g] for g in range(SSD_G)]
        _, vjp = jax.vjp(_ssd_chunk, x_ref[...].astype(F32), r_ref[...], b_ref[...], a_ref[...],
                         sprev)
        dyv = dy_ref[...]
        dys = [dyv[:, g * wd:(g + 1) * wd] for g in range(SSD_G)]
        dsn = [ds_scr[g] for g in range(SSD_G)]
        dx, dr, db, da, dsp = vjp((dys, dsn))
        dx_ref[:, :inner] = dx[:, :inner] + dx0_ref[...].astype(F32)
        dx_ref[:, inner:] = dx[:, inner:]
        dr_ref[...] = dr
        for g in range(SSD_G):
            ds_scr[g] = dsp[g]

        @pl.when(first)
        def _():
            db_ref[...] = db
            da_ref[...] = da

        @pl.when(jnp.logical_not(first))
        def _():
            db_ref[...] += db
            da_ref[...] += da

    return pl.pallas_call(
        body, name=name,
        out_shape=[jax.ShapeDtypeStruct((t, cdim), F32),
                   jax.ShapeDtypeStruct((t, LANE), F32),
                   jax.ShapeDtypeStruct((1, LANE), F32),
                   jax.ShapeDtypeStruct((1, LANE), F32)],
        grid=(nc,),
        in_specs=[full(cdim), full(LANE), vec, vec, st, full(inner), full(inner)],
        out_specs=[full(cdim), full(LANE), vec, vec],
        scratch_shapes=[pltpu.VMEM((SSD_G, SSD_N, wd), F32)],
        compiler_params=_params(("arbitrary",)),
    )(xbc, dtraw, bias, alog, states, dy, dxs_extra)


def _mix(o0, o1, o2, l0, l1, l2):
    m = lax.stop_gradient(jnp.maximum(jnp.maximum(l0, l1), l2))
    e0, e1, e2 = jnp.exp(l0 - m), jnp.exp(l1 - m), jnp.exp(l2 - m)
    return (e0 * o0 + e1 * o1 + e2 * o2) / (e0 + e1 + e2)


def _gate(y, xs, z, dexp, gain):
    v = (y + xs.astype(F32) * dexp) * _silu(z.astype(F32))
    return _rms(v, gain)


def _merge(ga, gs, ap, sp):
    return jax.nn.sigmoid(ga.astype(F32)) * ap + jax.nn.sigmoid(gs.astype(F32)) * sp


def _alibi_coefs(hp):
    n = hp * len(PATTERNS)
    slopes = np.exp2(-ALIBI_MAX_EXP * np.arange(1, n + 1, dtype=np.float32) / n).astype(np.float32)
    return [jnp.asarray(slopes[g * hp:(g + 1) * hp] * np.float32(d))
            for g, (_, d) in enumerate(PATTERNS)]


def _local_step(x, tgt, w, p, gw_=None, aw=None, dff=None):
    t, d = x.shape
    dff = w["gu1t"].shape[0] // 2 if dff is None else dff
    aw = w["abt"].shape[1] if aw is None else aw
    hp = aw // HD
    qkv = len(PATTERNS) * aw
    inner = p["ssd_norm"].shape[1]
    nh = p["dt_bias"].shape[1]
    gw_ = {} if gw_ is None else gw_
    gw = inner // SSD_G
    cdim = inner + 2 * SSD_G * SSD_N
    z_off, xbc_off = 3 * qkv, 3 * qkv + inner
    ga_off = xbc_off + cdim
    gs_off = ga_off + d
    hw = d // 2
    assert z_off % gw == 0 and xbc_off % LANE == 0 and ga_off % hw == 0 and gs_off % hw == 0
    assert (nh // SSD_G) * SSD_P == gw and hp % 2 == 0 and aw % LANE == 0 and nh <= LANE
    gdt = MXU_DTYPE

    row = lambda a, width, base=0: ("row", a, width, base)
    const = lambda a, width, base=0: ("const", a, width, base)

    def rms_fwd(name, xin, g):
        return _rw(name, lambda xv, gv: (_rms(xv, gv),), [row(xin, d), const(g, d)],
                   [((d,), ACT_DTYPE)])[0]

    def rms_bwd(name, xin, g, dh, dres):
        def fn(xv, gv, dhv, drv):
            _, vjp = jax.vjp(_rms, xv, gv)
            dx, dg = vjp(dhv.astype(F32))
            return drv + dx, dg
        return _rw(name, fn, [row(xin, d), const(g, d), row(dh, d), row(dres, d)],
                   [((d,), F32)], accs=[(1, d)])

    def ffn_fwd(tag, xin, g, key_gu, key_d):
        h = rms_fwd(tag + "_norm", xin, g)
        gate, up, a = _ffn_up(tag + "_up", h, w[key_gu])
        xo = _mm(tag + "_down", a, w[key_d], "nn", F32, res=xin, scale=0.5)
        return xo, (h, gate, up, a)

    def ffn_bwd(tag, xin, g, wgut, wd, saved, dxo, key_gu, key_d):
        h, gate, up, a = saved
        gw_[key_d] = _mm(tag + "_dwd", a, dxo, "tn", gdt, scale=0.5)
        dgu = _ffn_dact(tag + "_dact", dxo, wd, gate, up, 0.5)
        gw_[key_gu] = _mm(tag + "_dwgu", dgu, h, "tn", gdt)
        dh = _mm(tag + "_dh", dgu, wgut, "nn", F32)
        return rms_bwd(tag + "_dnorm", xin, g, dh, dxo)

    x1, ffn1_saved = ffn_fwd("ffn1", x, p["ffn1_norm"], "gu1t", "d1")
    if hasattr(w, "after_first_ffn"):
        w.after_first_ffn()
    h2 = rms_fwd("mix_norm", x1, p["mix_norm"])
    proj = _mm("in_proj", h2, w["maint"], "nt", ACT_DTYPE, cap_m=ROW_TILE_CAP,
               cap_n=WIDE_N_CAP)
    dtraw = _mm("dt_proj", h2, w["dtt"], "nt", F32)

    coefs = _alibi_coefs(hp)
    qg2 = jnp.concatenate([p["q_norm"], p["q_norm"]], axis=1)
    kg2 = jnp.concatenate([p["k_norm"], p["k_norm"]], axis=1)
    pw = 2 * HD
    attn_bases = [[(off + gi * aw) // pw for off in (0, qkv, 2 * qkv)]
                  for gi in range(len(PATTERNS))]
    attn_o, attn_l = [], []
    for gi, (_, dil) in enumerate(PATTERNS):
        o, l = _attn_fwd(f"attn_fwd{gi}", proj, attn_bases[gi], qg2, kg2, coefs[gi], dil)
        attn_o.append(o)
        attn_l.append(l)
    ao = _rw("attn_mix", lambda *v: (_mix(*v),), [row(a, aw) for a in attn_o + attn_l],
             [((aw,), ACT_DTYPE)])[0]

    xbc = _conv_fwd("conv_fwd", proj, xbc_off // LANE, p["conv_w"], p["conv_b"])
    pad = lambda v: jnp.pad(v, ((0, 0), (0, LANE - nh)))
    bias_p, alog_p = pad(p["dt_bias"]), pad(p["a_log"])
    yssd, states = _ssd_fwd("ssd_fwd", xbc, dtraw, bias_p, alog_p, inner)
    dexp = jnp.repeat(p["d_skip"], SSD_P, axis=1)
    gate_ins = [row(yssd, gw), row(xbc, gw), row(proj, gw, z_off // gw),
                const(dexp, gw), const(p["ssd_norm"], gw)]
    yn = _rw("ssd_gate", lambda *v: (_gate(*v),), gate_ins, [((gw,), ACT_DTYPE)], ncb=SSD_G)[0]

    ap = _mm("attn_out", ao, w["abt"], "nt", F32)
    sp = _mm("ssd_out", yn, w["sb"], "nn", F32)
    merge_ins = [row(proj, hw, ga_off // hw), row(proj, hw, gs_off // hw), row(ap, hw), row(sp, hw)]
    mg = _rw("merge", lambda *v: (_merge(*v),), merge_ins, [((hw,), ACT_DTYPE)], ncb=2)[0]
    x2 = _mm("mix_out", mg, w["out"], "nn", F32, res=x1)
    x3, ffn2_saved = ffn_fwd("ffn2", x2, p["ffn2_norm"], "gu2t", "d2")

    def loss_fn(yv, tv):
        e = yv - tv
        return e * (1.0 / d), _colsum(e * e)
    dy, loss_vec = _rw("loss", loss_fn, [row(x3, d), row(tgt, d)], [((d,), F32)], accs=[(1, d)])

    gp = {}
    dx2, gp["ffn2_norm"] = ffn_bwd(
        "ffn2", x2, p["ffn2_norm"], w["gu2t"], w["d2"], ffn2_saved, dy, "gu2t", "d2")
    dmg = _mm("d_merge", dx2, w["out"], "nt", ACT_DTYPE)
    gw_["out"] = _mm("dw_out", mg, dx2, "tn", gdt)

    def merge_bwd(gav, gsv, apv, spv, dv):
        _, vjp = jax.vjp(_merge, gav, gsv, apv, spv)
        return vjp(dv.astype(F32))
    dga, dgs, dap, dsp = _rw("d_merge_gate", merge_bwd, merge_ins + [row(dmg, hw)],
                             [((hw,), ACT_DTYPE)] * 4, ncb=2)
    gw_["abt"] = _mm("dw_ab", dap, ao, "tn", gdt)
    dao = _mm("d_attn_o", dap, w["abt"], "nn", F32)
    gw_["sb"] = _mm("dw_sb", yn, dsp, "tn", gdt)
    dyn = _mm("d_ssd_y", dsp, w["sb"], "nt", F32)

    def gate_bwd(yv, xv, zv, dev, gv, dv):
        _, vjp = jax.vjp(_gate, yv, xv, zv, dev, gv)
        return vjp(dv)
    dyssd, dxs_gate, dz, ddexp, gp["ssd_norm"] = _rw(
        "d_ssd_gate", gate_bwd, gate_ins + [row(dyn, gw)],
        [((gw,), F32), ((gw,), F32), ((gw,), ACT_DTYPE)], accs=[(1, gw), (1, gw)], ncb=SSD_G)
    gp["d_skip"] = ddexp.reshape(nh, SSD_P).sum(axis=1).reshape(1, nh)

    dxbc, ddtraw, dbias, dalog = _ssd_bwd("ssd_bwd", xbc, dtraw, bias_p, alog_p, states,
                                          dyssd, dxs_gate)
    gp["dt_bias"], gp["a_log"] = dbias[:, :nh], dalog[:, :nh]
    du, gp["conv_w"], gp["conv_b"] = _conv_bwd("conv_bwd", proj, xbc_off // LANE,
                                               p["conv_w"], p["conv_b"], dxbc)

    def mix_bwd(*v):
        _, vjp = jax.vjp(_mix, *v[:6])
        return vjp(v[6])
    dmix = _rw("d_attn_mix", mix_bwd, [row(a, aw) for a in attn_o + attn_l] + [row(dao, aw)],
               [((aw,), F32)] * 6)
    dq, dk, dv = [], [], []
    dqg = dkg = None
    for gi, (_, dil) in enumerate(PATTERNS):
        r = _attn_bwd(f"attn_bwd{gi}", proj, attn_bases[gi], qg2, kg2, coefs[gi], dil,
                      dmix[gi], dmix[3 + gi])
        dq.append(r[0])
        dk.append(r[1])
        dv.append(r[2])
        dqg = r[3] if dqg is None else dqg + r[3]
        dkg = r[4] if dkg is None else dkg + r[4]
    gp["q_norm"] = dqg[:, :HD] + dqg[:, HD:]
    gp["k_norm"] = dkg[:, :HD] + dkg[:, HD:]

    segs = dq + dk + dv + [dz, du, dga, dgs]
    gw_["maint"] = _mm("dw_in", segs, h2, "tn", gdt)
    gw_["dtt"] = _mm("dw_dt", ddtraw, h2, "tn", gdt)
    dh2 = _mm("d_h2_main", segs, w["maint"], "nn", F32)
    dh2 = _mm("d_h2_dt", ddtraw, w["dtt"], "nn", F32, res=dh2)
    dx1, gp["mix_norm"] = rms_bwd("d_mix_norm", x1, p["mix_norm"], dh2, dx2)
    dx0, gp["ffn1_norm"] = ffn_bwd(
        "ffn1", x, p["ffn1_norm"], w["gu1t"], w["d1"], ffn1_saved, dx1, "gu1t", "d1")
    return loss_vec, dx0, gw_, gp


MESH = pl.DeviceIdType.MESH
HBM_SPEC = pl.BlockSpec(memory_space=pltpu.HBM)


def _mesh_pos():
    return lax.axis_index("x"), lax.axis_index("y"), lax.axis_index("c")


def _flip(pos, k):
    x, y, c = pos
    return (1 - x if k & 4 else x, 1 - y if k & 2 else y, 1 - c if k & 1 else c)


def _dev_index(pos):
    return 4 * pos[0] + 2 * pos[1] + pos[2]


def _rows_of(ref, base, stride, rows, pos):
    start = pl.multiple_of(base + stride * _dev_index(pos), ROW_ALIGN)
    return ref.at[pl.ds(start, rows)]


def _gather(name, shards, dests, out_shapes):
    n = len(shards)
    n_out = len(out_shapes)

    def body(*refs):
        x_refs = refs[:n]
        o_refs = refs[n:n + n_out]
        send_sems, recv_sems, local_sems = refs[n + n_out:]
        me = _mesh_pos()
        sibling = _flip(me, 1)
        chips = [_flip(me, 4), _flip(me, 2), _flip(me, 6)]

        def slot(i, block):
            k_out, base, stride = dests[i]
            return _rows_of(o_refs[k_out], base, stride, shards[i].shape[0], block)

        def copy(i, k, block, to, src=None):
            dst = slot(i, block)
            return pltpu.make_async_remote_copy(
                src_ref=dst if src is None else src, dst_ref=dst,
                send_sem=send_sems.at[7 * i + k], recv_sem=recv_sems.at[7 * i + k],
                device_id=to, device_id_type=MESH)

        mine = [pltpu.make_async_copy(x_refs[i], slot(i, me), local_sems.at[i]) for i in range(n)]
        for cp in mine:
            cp.start()
        first = []
        for i in range(n):
            first.append(copy(i, 0, me, sibling, src=x_refs[i]))
            first += [copy(i, 1 + j, me, chip, src=x_refs[i]) for j, chip in enumerate(chips)]
        for cp in first:
            cp.start()
        passed = []
        for j, chip in enumerate(chips):
            for i in range(n):
                copy(i, 1 + j, chip, me).wait_recv()
                fwd = copy(i, 4 + j, chip, sibling)
                fwd.start()
                passed.append(fwd)
        for i in range(n):
            copy(i, 0, sibling, me).wait_recv()
            for j, chip in enumerate(chips):
                copy(i, 4 + j, _flip(chip, 1), me).wait_recv()
        for cp in first + passed:
            cp.wait_send()
        for cp in mine:
            cp.wait()

    return pl.pallas_call(
        body, name=name,
        out_shape=[jax.ShapeDtypeStruct(s, dt) for s, dt in out_shapes],
        in_specs=[HBM_SPEC] * n, out_specs=[HBM_SPEC] * n_out,
        scratch_shapes=[pltpu.SemaphoreType.DMA((7 * n,)), pltpu.SemaphoreType.DMA((7 * n,)),
                        pltpu.SemaphoreType.DMA((n,))],
    )(*shards)


SEM_SPEC =pl.BlockSpec(memory_space=pltpu.SEMAPHORE)
SIDE_EFFECT = pltpu.SideEffectType.DATAFLOW_SIDE_EFFECTING


def _split_refs(plan, srcs, lands, i, src_for, land_from):
    si, sbase, sstride, li, lbase, lstride, rows = plan[i]
    return (_rows_of(srcs[si], sbase, sstride, rows, src_for),
            _rows_of(lands[li], lbase, lstride, rows, land_from))


ALL_PEERS = tuple(range(1, NDEV))
SAME_CORE_AND_SIBLING = (1, 4, 2, 6)
OTHER_CHIPS = (4, 2, 6)


def _split_start(name, srcs, lands, plan, after=(), relations=ALL_PEERS):
    ns, nl, n = len(srcs), len(lands), len(plan)

    def body(*refs):
        s_refs = refs[:ns]
        l_refs = refs[ns:ns + nl]
        send_sems, recv_sems = refs[ns + nl + len(after):ns + nl + len(after) + 2]
        local_sems = refs[ns + nl + len(after) + 2]
        token = refs[ns + nl + len(after) + 3 + ns + nl]
        me = _mesh_pos()
        for i in range(n):
            src, dst = _split_refs(plan, s_refs, l_refs, i, me, me)
            pltpu.make_async_copy(src, dst, local_sems.at[i]).start()
        for k in relations:
            peer = _flip(me, k)
            for i in range(n):
                src, dst = _split_refs(plan, s_refs, l_refs, i, peer, me)
                pltpu.make_async_remote_copy(
                    src_ref=src, dst_ref=dst,
                    send_sem=send_sems.at[7 * i + k - 1], recv_sem=recv_sems.at[7 * i + k - 1],
                    device_id=peer, device_id_type=MESH).start()
        token[...] = jnp.zeros_like(token)

    hbm = lambda a: pltpu.HBM(a.shape, a.dtype)
    out_shape = ((pltpu.SemaphoreType.DMA((7 * n,)), pltpu.SemaphoreType.DMA((7 * n,)),
                  pltpu.SemaphoreType.DMA((n,)))
                 + tuple(hbm(a) for a in srcs) + tuple(hbm(a) for a in lands)
                 + (jax.ShapeDtypeStruct((8, LANE), F32),))
    out = pl.pallas_call(
        body, name=name, out_shape=out_shape,
        in_specs=[HBM_SPEC] * (ns + nl) + [ANY_SPEC] * len(after),
        out_specs=(SEM_SPEC, SEM_SPEC, SEM_SPEC) + (HBM_SPEC,) * (ns + nl)
        + (pl.BlockSpec(memory_space=pltpu.VMEM),),
        input_output_aliases={i: 3 + i for i in range(ns + nl)},
        compiler_params=pltpu.CompilerParams(has_side_effects=SIDE_EFFECT),
    )(*[pltpu.with_memory_space_constraint(a, pltpu.HBM) for a in tuple(srcs) + tuple(lands)],
      *after)
    _Order.tokens.append(out[-1])
    return out[0], out[1], out[2], out[3:3 + ns], out[3 + ns:3 + ns + nl]


def _split_wait(name, started, plan, relations=ALL_PEERS):
    send_sems, recv_sems, local_sems, srcs, lands = started
    ns, nl, n = len(srcs), len(lands), len(plan)
    after = [_Order.last] if _Order.last is not None else []

    def body(*refs):
        s_refs = refs[:ns]
        l_refs = refs[ns:ns + nl]
        send_sems, recv_sems, local_sems = refs[ns + nl:ns + nl + 3]
        me = _mesh_pos()
        for i in range(n):
            src, dst = _split_refs(plan, s_refs, l_refs, i, me, me)
            pltpu.make_async_copy(src, dst, local_sems.at[i]).wait()
        for k in relations:
            peer = _flip(me, k)
            for i in range(n):
                src, dst = _split_refs(plan, s_refs, l_refs, i, peer, peer)
                cp = pltpu.make_async_remote_copy(
                    src_ref=src, dst_ref=dst,
                    send_sem=send_sems.at[7 * i + k - 1], recv_sem=recv_sems.at[7 * i + k - 1],
                    device_id=peer, device_id_type=MESH)
                cp.wait_send()
                cp.wait_recv()

    hbm = lambda a: pltpu.HBM(a.shape, a.dtype)
    out = pl.pallas_call(
        body, name=name,
        out_shape=tuple(hbm(a) for a in srcs) + tuple(hbm(a) for a in lands),
        in_specs=[HBM_SPEC] * (ns + nl) + [SEM_SPEC] * 3 + [ANY_SPEC] * len(after),
        out_specs=(HBM_SPEC,) * (ns + nl),
        input_output_aliases={i: i for i in range(ns + nl)},
        compiler_params=pltpu.CompilerParams(has_side_effects=SIDE_EFFECT),
    )(*srcs, *lands, send_sems, recv_sems, local_sems, *after)
    return list(out[ns:])


def _forward_refs(plan, lands, i, block):
    _, _, _, li, lbase, lstride, rows = plan[i]
    return _rows_of(lands[li], lbase, lstride, rows, block)


def _forward_start(name, lands, plan):
    nl, n = len(lands), len(plan)

    def body(*refs):
        l_refs = refs[:nl]
        send_sems, recv_sems = refs[nl:nl + 2]
        token = refs[nl + 2 + nl]
        me = _mesh_pos()
        for j, kc in enumerate(OTHER_CHIPS):
            for i in range(n):
                rows = _forward_refs(plan, l_refs, i, _flip(me, kc))
                pltpu.make_async_remote_copy(
                    src_ref=rows, dst_ref=rows,
                    send_sem=send_sems.at[3 * i + j], recv_sem=recv_sems.at[3 * i + j],
                    device_id=_flip(me, 1), device_id_type=MESH).start()
        token[...] = jnp.zeros_like(token)

    hbm = lambda a: pltpu.HBM(a.shape, a.dtype)
    out = pl.pallas_call(
        body, name=name,
        out_shape=((pltpu.SemaphoreType.DMA((3 * n,)), pltpu.SemaphoreType.DMA((3 * n,)))
                   + tuple(hbm(a) for a in lands) + (jax.ShapeDtypeStruct((8, LANE), F32),)),
        in_specs=[HBM_SPEC] * nl,
        out_specs=(SEM_SPEC, SEM_SPEC) + (HBM_SPEC,) * nl
        + (pl.BlockSpec(memory_space=pltpu.VMEM),),
        input_output_aliases={i: 2 + i for i in range(nl)},
        compiler_params=pltpu.CompilerParams(has_side_effects=SIDE_EFFECT),
    )(*[pltpu.with_memory_space_constraint(a, pltpu.HBM) for a in lands])
    _Order.tokens.append(out[-1])
    return out[0], out[1], out[2:2 + nl]


def _forward_wait(name, started, plan):
    send_sems, recv_sems, lands = started
    nl, n = len(lands), len(plan)
    after = [_Order.last] if _Order.last is not None else []

    def body(*refs):
        l_refs = refs[:nl]
        send_sems, recv_sems = refs[nl:nl + 2]
        me = _mesh_pos()
        for j, kc in enumerate(OTHER_CHIPS):
            for i in range(n):
                sent = _forward_refs(plan, l_refs, i, _flip(me, kc))
                came = _forward_refs(plan, l_refs, i, _flip(_flip(me, 1), kc))
                cp = pltpu.make_async_remote_copy(
                    src_ref=sent, dst_ref=came,
                    send_sem=send_sems.at[3 * i + j], recv_sem=recv_sems.at[3 * i + j],
                    device_id=_flip(me, 1), device_id_type=MESH)
                cp.wait_send()
                cp.wait_recv()

    hbm = lambda a: pltpu.HBM(a.shape, a.dtype)
    out = pl.pallas_call(
        body, name=name, out_shape=tuple(hbm(a) for a in lands),
        in_specs=[HBM_SPEC] * nl + [SEM_SPEC] * 2 + [ANY_SPEC] * len(after),
        out_specs=(HBM_SPEC,) * nl,
        input_output_aliases={i: i for i in range(nl)},
        compiler_params=pltpu.CompilerParams(has_side_effects=SIDE_EFFECT),
    )(*lands, send_sems, recv_sems, *after)
    return list(out)


def _regroup_rows(name, padded, r, rp, lo, hi):
    d = padded.shape[1]
    pack = 4 // padded.dtype.itemsize
    assert r % pack == 0 and rp % ROW_ALIGN == 0 and lo % (8 * pack) == 0 and hi % (8 * pack) == 0
    r2, rp2, lo2, hi2 = r // pack, rp // pack, lo // pack, hi // pack
    u32 = jnp.uint32

    def body(x_ref, main_ref, cut_ref):
        x = pltpu.bitcast(x_ref[...], u32)
        joined = jnp.concatenate([x[rp2 * j:rp2 * j + r2] for j in range(NDEV)], axis=0)
        main = jnp.concatenate([joined[:lo2], joined[hi2:]], axis=0)
        cut = jnp.concatenate([joined[lo2:hi2], jnp.zeros((LANE // pack - (hi2 - lo2), LANE), u32)],
                              axis=0)
        main_ref[...] = pltpu.bitcast(main, padded.dtype)
        cut_ref[...] = pltpu.bitcast(cut, padded.dtype)

    return pl.pallas_call(
        body, name=name,
        out_shape=[jax.ShapeDtypeStruct((NDEV * r - (hi - lo), d), padded.dtype),
                   jax.ShapeDtypeStruct((LANE, d), padded.dtype)],
        grid=(d // LANE,),
        in_specs=[pl.BlockSpec((NDEV * rp, LANE), lambda i: (0, i))],
        out_specs=[pl.BlockSpec((NDEV * r - (hi - lo), LANE), lambda i: (0, i)),
                   pl.BlockSpec((LANE, LANE), lambda i: (0, i))],
        compiler_params=_params(("parallel",)),
    )(padded)


def _ungroup_rows(name, main, cut, r, rp, lo, hi):
    d = main.shape[1]
    pack = 4 // main.dtype.itemsize
    r2, rp2, lo2, hi2 = r // pack, rp // pack, lo // pack, hi // pack
    u32 = jnp.uint32

    def body(main_ref, cut_ref, o_ref):
        m = pltpu.bitcast(main_ref[...], u32)
        c = pltpu.bitcast(cut_ref[...], u32)
        joined = jnp.concatenate([m[:lo2], c[:hi2 - lo2], m[lo2:]], axis=0)
        zeros = jnp.zeros((rp2 - r2, LANE), u32)
        parts = []
        for j in range(NDEV):
            parts += [joined[r2 * j:r2 * (j + 1)], zeros]
        o_ref[...] = pltpu.bitcast(jnp.concatenate(parts, axis=0), main.dtype)

    return pl.pallas_call(
        body, name=name,
        out_shape=jax.ShapeDtypeStruct((NDEV * rp, d), main.dtype),
        grid=(d // LANE,),
        in_specs=[pl.BlockSpec((main.shape[0], LANE), lambda i: (0, i)),
                  pl.BlockSpec((LANE, LANE), lambda i: (0, i))],
        out_specs=pl.BlockSpec((NDEV * rp, LANE), lambda i: (0, i)),
        compiler_params=_params(("parallel",)),
    )(main, cut)


def _sum_slabs(name, a):
    s, r, c = a.shape

    def body(a_ref, o_ref):
        acc = a_ref[0].astype(F32)
        for i in range(1, s):
            acc = acc + a_ref[i].astype(F32)
        o_ref[...] = acc

    return pl.pallas_call(body, name=name, out_shape=jax.ShapeDtypeStruct((r, c), F32))(a)


def _adamw_update(g, w, m, v):
    mn = ADAM_B1 * m + (1.0 - ADAM_B1) * g
    vn = ADAM_B2 * v + (1.0 - ADAM_B2) * (g * g)
    m_hat = mn / (1.0 - ADAM_B1 ** ADAM_STEP)
    v_hat = vn / (1.0 - ADAM_B2 ** ADAM_STEP)
    delta = -ADAM_LR * (m_hat / (jnp.sqrt(v_hat) + ADAM_EPS) + ADAM_WD * w)
    return delta, mn, vn


def _adamw(name, gsrc, w, m, v, transposed=False, tr=256):
    s = gsrc.shape[0]
    r, c = w.shape
    step = LANE if transposed else 8
    tr = max(t for t in range(step, min(tr, r) + 1, step) if r % t == 0)

    def body(g_ref, w_ref, m_ref, v_ref, go_ref, d_ref, mo_ref, vo_ref):
        g = g_ref[0].astype(F32)
        for i in range(1, s):
            g = g + g_ref[i].astype(F32)
        if transposed:
            g = g.T[:, :c]
        delta, mn, vn = _adamw_update(g, w_ref[...], m_ref[...], v_ref[...])
        go_ref[...] = g
        d_ref[...] = delta
        mo_ref[...] = mn
        vo_ref[...] = vn

    blk = pl.BlockSpec((tr, c), lambda i: (i, 0))
    if transposed:
        g_spec = pl.BlockSpec((s, gsrc.shape[1], tr), lambda i: (0, 0, i))
    else:
        g_spec = pl.BlockSpec((s, tr, c), lambda i: (0, i, 0))
    return pl.pallas_call(
        body, name=name, out_shape=[jax.ShapeDtypeStruct(w.shape, F32)] * 4,
        grid=(r // tr,),
        in_specs=[g_spec, blk, blk, blk], out_specs=[blk] * 4,
        compiler_params=_params(("parallel",)),
    )(gsrc, w, m, v)


REPLICATED = ("ffn1_norm", "mix_norm", "q_norm", "k_norm", "conv_b", "dt_bias", "a_log",
              "d_skip", "ssd_norm", "ffn2_norm")
ALL_WEIGHTS = ("ffn1_norm", "ffn1_w_gate", "ffn1_w_up", "ffn1_w_down", "mix_norm", "w_in",
               "q_norm", "k_norm", "conv_w", "conv_b", "dt_bias", "a_log", "d_skip", "ssd_norm",
               "w_attn_branch", "w_ssd_branch", "w_out", "ffn2_norm", "ffn2_w_gate", "ffn2_w_up",
               "ffn2_w_down")
BIG = (("ffn1_w_gate", True, "gu1t", 0), ("ffn1_w_up", True, "gu1t", 1),
       ("ffn1_w_down", False, "d1", 0), ("w_in", True, "wint", 0),
       ("w_attn_branch", True, "abt", 0), ("w_ssd_branch", False, "sb", 0),
       ("w_out", False, "out", 0),
       ("ffn2_w_gate", True, "gu2t", 0), ("ffn2_w_up", True, "gu2t", 1),
       ("ffn2_w_down", False, "d2", 0))


def _nrows(shape, cols):
    return -(-math.prod(shape) // cols)


def _pack_rows(arrs, cols, row_tile):
    parts = []
    for a in arrs:
        flat = a.reshape(-1)
        nr = -(-flat.shape[0] // cols)
        parts.append(jnp.pad(flat, (0, nr * cols - flat.shape[0])).reshape(nr, cols))
    out = jnp.concatenate(parts, axis=0)
    return jnp.pad(out, ((0, _round_up(out.shape[0], row_tile) - out.shape[0]), (0, 0)))


def _unpack_rows(packed, shapes):
    cols = packed.shape[-1]
    out, r0 = [], 0
    for sh in shapes:
        nr = _nrows(sh, cols)
        out.append(packed[r0:r0 + nr].reshape(-1)[:math.prod(sh)].reshape(tuple(sh)))
        r0 += nr
    return out


def kernel(x, ffn1_norm, ffn1_w_gate, ffn1_w_up, ffn1_w_down, mix_norm, w_in, q_norm, k_norm, conv_w, conv_b, dt_bias, a_log, d_skip, ssd_norm, w_attn_branch, w_ssd_branch, w_out, ffn2_norm, ffn2_w_gate, ffn2_w_up, ffn2_w_down, loss_target, m_ffn1_norm, m_ffn1_w_gate, m_ffn1_w_up, m_ffn1_w_down, m_mix_norm, m_w_in, m_q_norm, m_k_norm, m_conv_w, m_conv_b, m_dt_bias, m_a_log, m_d_skip, m_ssd_norm, m_w_attn_branch, m_w_ssd_branch, m_w_out, m_ffn2_norm, m_ffn2_w_gate, m_ffn2_w_up, m_ffn2_w_down, v_ffn1_norm, v_ffn1_w_gate, v_ffn1_w_up, v_ffn1_w_down, v_mix_norm, v_w_in, v_q_norm, v_k_norm, v_conv_w, v_conv_b, v_dt_bias, v_a_log, v_d_skip, v_ssd_norm, v_w_attn_branch, v_w_ssd_branch, v_w_out, v_ffn2_norm, v_ffn2_w_gate, v_ffn2_w_up, v_ffn2_w_down):
    given = dict(locals())
    wts = {n: given[n] for n in ALL_WEIGHTS}
    mom = {n: given["m_" + n] for n in ALL_WEIGHTS}
    var = {n: given["v_" + n] for n in ALL_WEIGHTS}
    d = x.shape[-1]
    nh = dt_bias.shape[1]
    my = _dev_index(_mesh_pos())

    def row_form(n, col_sharded):
        a = wts[n][0].T if col_sharded else wts[n][0]
        a = jnp.pad(a, ((0, _round_up(a.shape[0], ROW_ALIGN) - a.shape[0]), (0, 0)))
        return a.astype(MXU_DTYPE)

    _Order.tokens, _Order.last = [], None
    shard = {n: row_form(n, cs) for n, cs, _, _ in BIG}
    entries = {buf: [e for e in BIG if e[2] == buf] for buf in dict.fromkeys(e[2] for e in BIG)}

    def buf_shape(buf):
        r, c = shard[entries[buf][0][0]].shape
        return (len(entries[buf]) * NDEV * r, c)

    def gather_plan(bufs):
        srcs, lands, plan = [], [], []
        for li, buf in enumerate(bufs):
            lands.append(lax.empty(buf_shape(buf), MXU_DTYPE))
            for n, _, _, pos in entries[buf]:
                r = shard[n].shape[0]
                plan.append((len(srcs), 0, 0, li, pos * NDEV * r, r, r))
                srcs.append(shard[n])
        return srcs, lands, plan

    def scatter_plan(bufs, grads):
        srcs, lands, plan, names = [], [], [], []
        for si, buf in enumerate(bufs):
            srcs.append(grads[buf])
            for n, _, _, pos in entries[buf]:
                r, c = shard[n].shape
                plan.append((si, pos * NDEV * r, r, len(lands), 0, r, r))
                lands.append(lax.empty((NDEV * r, c), MXU_DTYPE))
                names.append(n)
        return srcs, lands, plan, names

    conv_rows = _pack_rows([conv_w[0]], LANE, ROW_ALIGN)
    srcs1, lands1, plan1 = gather_plan(("gu1t", "d1"))
    plan1.append((len(srcs1), 0, 0, len(lands1), 0, conv_rows.shape[0], conv_rows.shape[0]))
    srcs1.append(conv_rows)
    lands1.append(lax.empty((NDEV * conv_rows.shape[0], LANE), F32))
    started1 = _split_start("gather_first_start", srcs1, lands1, plan1,
                            relations=SAME_CORE_AND_SIBLING)

    in_cols = w_in.shape[2]
    in_pad = _round_up(in_cols, ROW_ALIGN)
    dt_off = NDEV * in_cols - 2 * d - nh
    second_bufs = ("wint",)
    third_bufs = ("abt", "sb", "out", "gu2t", "d2")
    plan2 = gather_plan(second_bufs)
    started2, forwarded, started3, conv_landed = [], [], [], []

    class Weights(dict):
        def first_group(self):
            lands = _split_wait("gather_first_wait", started1, plan1,
                                relations=SAME_CORE_AND_SIBLING)
            fwd = _forward_start("gather_first_forward", lands, plan1)
            started2.append(_split_start("gather_in_start", *plan2, after=[lands[0]],
                                         relations=SAME_CORE_AND_SIBLING))
            lands = _forward_wait("gather_first_arrive", fwd, plan1)
            self["gu1t"], self["d1"] = lands[0], lands[1]
            conv_landed.append(lands[2])

        def after_first_ffn(self):
            lands = _split_wait("gather_in_wait", started2[0], plan2[2],
                                relations=SAME_CORE_AND_SIBLING)
            forwarded.append(_forward_start("gather_in_forward", lands, plan2[2]))

        def __missing__(self, key):
            if key in ("gu1t", "d1"):
                self.first_group()
            elif key in ("maint", "dtt"):
                wint = _forward_wait("gather_in_arrive", forwarded[0], plan2[2])[0]
                plan3 = gather_plan(third_bufs)
                started3.append((_split_start("gather_rest_start", *plan3, after=[wint]), plan3[2]))
                self["maint"], self["dtt"] = _regroup_rows(
                    "regroup_w_in", wint, in_cols, in_pad, dt_off, dt_off + nh)
            else:
                st, plan = started3[0]
                for buf, a in zip(third_bufs, _split_wait("gather_rest_wait", st, plan)):
                    self[buf] = a
            return self[key]

    w = Weights()

    class Params(dict):
        def __missing__(self, key):
            assert key == "conv_w" and conv_landed
            conv_all = conv_landed[0].reshape(NDEV, conv_rows.shape[0] * LANE)
            conv_all = conv_all[:, :math.prod(conv_w.shape[1:])]
            self[key] = (conv_all.reshape((NDEV,) + conv_w.shape[1:]).transpose(1, 0, 2)
                         .reshape(conv_w.shape[1], NDEV * conv_w.shape[2]))
            return self[key]

    p = Params({n: wts[n] for n in REPLICATED})

    groups = (("scatter_late", ("gu2t", "d2", "out", "abt", "sb")),
              ("scatter_in", ("maint", "dtt")),
              ("scatter_first", ("gu1t", "d1")))
    in_flight = []

    class Grads(dict):
        def __setitem__(self, key, value):
            dict.__setitem__(self, key, value)
            for tag, need in groups:
                if key in need and all(k in self for k in need):
                    if tag == "scatter_in":
                        gwin = _ungroup_rows("ungroup_w_in", self["maint"], self["dtt"],
                                             in_cols, in_pad, dt_off, dt_off + nh)
                        bufs, grads = ("wint",), {"wint": gwin}
                    else:
                        bufs, grads = need, self
                    srcs, lands, plan, names = scatter_plan(bufs, grads)
                    in_flight.append((tag, _split_start(tag + "_start", srcs, lands, plan),
                                      plan, names))

    loss_vec, dx, gw, gp = _local_step(x[0], loss_target[0], w, p, Grads(),
                                       aw=w_attn_branch.shape[1], dff=ffn1_w_down.shape[1] * NDEV)

    small_names = REPLICATED + ("conv_w",)
    small_shapes = [gp[n].shape for n in small_names]
    small = _pack_rows([gp[n] for n in small_names], LANE, ROW_ALIGN)
    small_plan = [(0, 0, 0, 0, 0, small.shape[0], small.shape[0])]
    small_started = _split_start("gather_small_start", [small],
                                 [lax.empty((NDEV * small.shape[0], LANE), F32)], small_plan)

    outs = [{}, {}, {}, {}]
    col_sharded_of = {n: cs for n, cs, _, _ in BIG}
    for tag, started, plan, names in in_flight:
        for n, rv in zip(names, _split_wait(tag + "_wait", started, plan)):
            rv = rv.reshape(NDEV, shard[n].shape[0], shard[n].shape[1])
            if col_sharded_of[n] and rv.shape[1] == wts[n].shape[2]:
                res = _adamw("adamw_" + n, rv, wts[n][0].T, mom[n][0].T, var[n][0].T)
                res = [a.T for a in res]
            else:
                res = _adamw("adamw_" + n, rv, wts[n][0], mom[n][0], var[n][0],
                             transposed=col_sharded_of[n])
            _Order.done(res)
            for k in range(4):
                outs[k][n] = res[k][None]

    small_all = _split_wait("gather_small_wait", small_started, small_plan)[0]
    small_all = small_all.reshape(NDEV, small.shape[0], LANE)
    small_g = _unpack_rows(_sum_slabs("sum_small_grads", small_all), small_shapes)
    small_g = dict(zip(small_names, small_g))
    cs = conv_w.shape[2]
    small_g["conv_w"] = lax.dynamic_slice_in_dim(small_g["conv_w"], my * cs, cs, axis=1)
    small_shard_shapes = [wts[n].shape[-2:] for n in small_names]
    sg = _pack_rows([small_g[n] for n in small_names], LANE, 8)
    sw = _pack_rows([wts[n] for n in small_names], LANE, 8)
    sm = _pack_rows([mom[n] for n in small_names], LANE, 8)
    sv = _pack_rows([var[n] for n in small_names], LANE, 8)
    res_small = _adamw("adamw_small", sg[None], sw, sm, sv, tr=sg.shape[0])
    for k in range(4):
        for n, a in zip(small_names, _unpack_rows(res_small[k], small_shard_shapes)):
            outs[k][n] = a.reshape(wts[n].shape)

    loss = lax.psum(0.5 * jnp.sum(loss_vec) / d, ("x", "y", "c"))
    result = [loss, dx[None]]
    for k in range(4):
        result += [outs[k][n] for n in ALL_WEIGHTS]
    return tuple(result)
```

```python
import functools
import math

import numpy as np
import jax
import jax.numpy as jnp
from jax import lax
from jax.experimental import pallas as pl
from jax.experimental.pallas import tpu as pltpu

F32 = jnp.float32
BF16 = jnp.bfloat16
MXU_DTYPE = BF16
ACT_DTYPE = BF16

NDEV = 8
EPS = 1e-6
HD = 64
QB = 128
PATTERNS = ((128, 1), (512, 4), (2048, 16))
ALIBI_MAX_EXP = 8.0
SSD_P = 64
SSD_N = 128
SSD_G = 4
SSD_Q = 128
SSD_K = 4
NEG = -1e30
LANE = 128
ROW_ALIGN = 16
VMEM_LIMIT = 56 * 1024 * 1024
TILE_CAP = 11 * LANE
ROW_TILE_CAP = 8 * LANE
WIDE_N_CAP = 23 * LANE

ADAM_LR, ADAM_B1, ADAM_B2, ADAM_EPS, ADAM_WD, ADAM_STEP = 0.001, 0.9, 0.999, 1e-8, 0.01, 10

NN = (((1,), (0,)), ((), ()))
NT = (((1,), (1,)), ((), ()))
TN = (((0,), (0,)), ((), ()))


BNN = (((2,), (1,)), ((0,), (0,)))
BNT = (((2,), (2,)), ((0,), (0,)))
BTN = (((1,), (1,)), ((0,), (0,)))
_DOT_GRADS = {
    NN: (("g", "b", NT), ("a", "g", TN)),
    NT: (("g", "b", NN), ("g", "a", TN)),
    TN: (("b", "g", NT), ("a", "g", NN)),
    BNT: (("g", "b", BNN), ("g", "a", BTN)),
    BTN: (("b", "g", BNT), ("a", "g", BNN)),
}


def _mxu(a, b, dims):
    return lax.dot_general(a.astype(MXU_DTYPE), b.astype(MXU_DTYPE), dims,
                           preferred_element_type=F32)


@functools.partial(jax.custom_vjp, nondiff_argnums=(2,))
def _dot_vjp(a, b, dims):
    return _mxu(a, b, dims)


def _dot_vjp_fwd(a, b, dims):
    return _mxu(a, b, dims), (a.astype(MXU_DTYPE), b.astype(MXU_DTYPE))


def _dot_vjp_bwd(dims, res, g):
    ops = {"a": res[0], "b": res[1], "g": g}
    (x1, y1, d1), (x2, y2, d2) = _DOT_GRADS[dims]
    return _mxu(ops[x1], ops[y1], d1), _mxu(ops[x2], ops[y2], d2)


_dot_vjp.defvjp(_dot_vjp_fwd, _dot_vjp_bwd)


def _dot(a, b, dims=NN):
    return _dot_vjp(a, b, dims)


def _split3(a):
    hi = a.astype(BF16)
    r = a - hi.astype(F32)
    mid = r.astype(BF16)
    lo = (r - mid.astype(F32)).astype(BF16)
    return hi, mid, lo


def _dot3(a, b, dims=NN, split=0):
    if split == 0:
        bb = b.astype(BF16)
        parts = [lax.dot_general(s, bb, dims, preferred_element_type=F32) for s in _split3(a)]
    else:
        aa = a.astype(BF16)
        parts = [lax.dot_general(aa, s, dims, preferred_element_type=F32) for s in _split3(b)]
    return parts[0] + parts[1] + parts[2]


@jax.custom_vjp
def _spread(v, e):
    return _dot3(v, e)


def _spread_fwd(v, e):
    return _dot3(v, e), e


def _spread_bwd(e, g):
    return _dot3(g, e, NT), jnp.zeros_like(e)


_spread.defvjp(_spread_fwd, _spread_bwd)


@jax.custom_vjp
def _running_sum(a, lower):
    return _dot3(lower, a, NN, split=1)


def _running_sum_fwd(a, lower):
    return _dot3(lower, a, NN, split=1), lower


def _running_sum_bwd(lower, g):
    return _dot3(lower, g, TN, split=1), jnp.zeros_like(lower)


_running_sum.defvjp(_running_sum_fwd, _running_sum_bwd)


def _tile(n, cap):
    if n <= cap:
        return n
    best = None
    for t in range(LANE, cap + 1, LANE):
        if n % t == 0:
            best = t
    assert best is not None, (n, cap)
    return best


def _params(sem):
    return pltpu.CompilerParams(dimension_semantics=sem, vmem_limit_bytes=VMEM_LIMIT)


def _round_up(n, m):
    return -(-n // m) * m


class _Order:
    tokens = []
    last = None

    @classmethod
    def take(cls):
        out, cls.tokens = cls.tokens, []
        return out

    @classmethod
    def done(cls, result):
        cls.last = result[0] if isinstance(result, (list, tuple)) else result
        return result


ANY_SPEC = pl.BlockSpec(memory_space=pl.ANY)


def _mm(name, a, b, mode, out_dtype=F32, res=None, scale=1.0,
        cap_m=TILE_CAP, cap_n=TILE_CAP, cap_k=TILE_CAP):
    segs = list(a) if isinstance(a, (list, tuple)) else [a]
    nseg = len(segs)
    if mode == "tn":
        k = segs[0].shape[0]
        widths = [s.shape[1] for s in segs]
        m = sum(widths)
        k2, n = b.shape
        tm = _tile(math.gcd(*widths), cap_m)
        tk = _tile(k, cap_k)
        counts = [wd // tm for wd in widths]
    else:
        m = segs[0].shape[0]
        widths = [s.shape[1] for s in segs]
        k = sum(widths)
        (k2, n) = b.shape if mode == "nn" else b.shape[::-1]
        tm = _tile(m, cap_m)
        tk = _tile(math.gcd(*widths), cap_k)
        counts = [wd // tk for wd in widths]
    assert k == k2, (name, [s.shape for s in segs], b.shape, mode)
    tn = _tile(n, cap_n)
    nk = k // tk
    starts = [sum(counts[:s]) for s in range(nseg)]
    dims = {"nn": NN, "nt": NT, "tn": TN}[mode]

    def a_spec(s):
        lo, cnt = starts[s], counts[s]
        if mode == "tn":
            if nseg == 1:
                return pl.BlockSpec((tk, tm), lambda i, j, kk: (kk, i))
            return pl.BlockSpec(
                (tk, tm), lambda i, j, kk: (jnp.where((i >= lo) & (i < lo + cnt), kk, 0),
                                            jnp.clip(i - lo, 0, cnt - 1)))
        if nseg == 1:
            return pl.BlockSpec((tm, tk), lambda i, j, kk: (i, kk))
        return pl.BlockSpec((tm, tk), lambda i, j, kk: (i, jnp.clip(kk - lo, 0, cnt - 1)))

    b_spec = (pl.BlockSpec((tn, tk), lambda i, j, kk: (j, kk)) if mode == "nt"
              else pl.BlockSpec((tk, tn), lambda i, j, kk: (kk, j)))
    o_spec = pl.BlockSpec((tm, tn), lambda i, j, kk: (i, j))
    has_res = res is not None
    use_acc = nk > 1 or nseg > 1
    ties = _Order.take()
    nt_ = len(ties)

    def body(*refs):
        a_refs = refs[:nseg]
        b_ref = refs[nseg]
        r_ref = refs[nseg + 1] if has_res else None
        o_ref = refs[nseg + 1 + has_res + nt_]
        scr = refs[nseg + 2 + has_res + nt_:]

        def finish(acc):
            if scale != 1.0:
                acc = acc * scale
            if has_res:
                acc = r_ref[...].astype(F32) + acc
            o_ref[...] = acc.astype(o_ref.dtype)

        if not use_acc:
            finish(_dot(a_refs[0][...], b_ref[...], dims))
            return
        acc_ref = scr[0]
        kk = pl.program_id(2)
        sel = pl.program_id(0) if mode == "tn" else kk

        @pl.when(kk == 0)
        def _():
            acc_ref[...] = jnp.zeros_like(acc_ref)

        for s in range(nseg):
            def add(s=s):
                acc_ref[...] += _dot(a_refs[s][...], b_ref[...], dims)
            if nseg == 1:
                add()
            else:
                pl.when((sel >= starts[s]) & (sel < starts[s] + counts[s]))(add)

        @pl.when(kk == nk - 1)
        def _():
            finish(acc_ref[...])

    in_specs = ([a_spec(s) for s in range(nseg)] + [b_spec] + ([o_spec] if has_res else [])
                + [ANY_SPEC] * nt_)
    args = tuple(segs) + (b,) + ((res,) if has_res else ()) + tuple(ties)
    return _Order.done(pl.pallas_call(
        body, name=name,
        out_shape=jax.ShapeDtypeStruct((m, n), out_dtype),
        grid=(m // tm, n // tn, nk),
        in_specs=in_specs, out_specs=o_spec,
        scratch_shapes=[pltpu.VMEM((tm, tn), F32)] if use_acc else [],
        compiler_params=_params(("parallel", "parallel", "arbitrary")),
    )(*args))


def _act(g, u):
    return _silu(g.astype(F32)) * u.astype(F32)


def _ffn_up(name, h, wgut, cap_m=ROW_TILE_CAP, cap_n=TILE_CAP):
    m, k = h.shape
    dff = wgut.shape[0] // 2
    tm, tn = _tile(m, cap_m), _tile(dff, cap_n)
    nj = dff // tn
    ties = _Order.take()

    def body(h_ref, wg_ref, wu_ref, *rest):
        g_ref, u_ref, a_ref = rest[len(ties):]
        hv = h_ref[...]
        g = _dot(hv, wg_ref[...], NT)
        u = _dot(hv, wu_ref[...], NT)
        g_ref[...] = g.astype(g_ref.dtype)
        u_ref[...] = u.astype(u_ref.dtype)
        a_ref[...] = _act(g, u).astype(a_ref.dtype)

    o_spec = pl.BlockSpec((tm, tn), lambda i, j: (i, j))
    return _Order.done(pl.pallas_call(
        body, name=name, out_shape=[jax.ShapeDtypeStruct((m, dff), ACT_DTYPE)] * 3,
        grid=(m // tm, nj),
        in_specs=[pl.BlockSpec((tm, k), lambda i, j: (i, 0)),
                  pl.BlockSpec((tn, k), lambda i, j: (j, 0)),
                  pl.BlockSpec((tn, k), lambda i, j: (nj + j, 0))] + [ANY_SPEC] * len(ties),
        out_specs=[o_spec] * 3,
        compiler_params=_params(("parallel", "parallel")),
    )(h, wgut, wgut, *ties))


def _ffn_dact(name, dxo, wd, g, u, scale, cap_m=ROW_TILE_CAP, cap_n=TILE_CAP):
    m, k = dxo.shape
    dff = wd.shape[0]
    tm, tn = _tile(m, cap_m), _tile(dff, cap_n)
    ties = _Order.take()

    def body(d_ref, w_ref, g_ref, u_ref, *rest):
        dg_ref, du_ref = rest[len(ties):]
        da = _dot(d_ref[...], w_ref[...], NT) * scale
        _, vjp = jax.vjp(_act, g_ref[...], u_ref[...])
        dg, du = vjp(da)
        dg_ref[...] = dg.astype(dg_ref.dtype)
        du_ref[...] = du.astype(du_ref.dtype)

    o_spec = pl.BlockSpec((tm, tn), lambda i, j: (i, j))
    return _Order.done(pl.pallas_call(
        body, name=name, out_shape=[jax.ShapeDtypeStruct((m, dff), ACT_DTYPE)] * 2,
        grid=(m // tm, dff // tn),
        in_specs=[pl.BlockSpec((tm, k), lambda i, j: (i, 0)),
                  pl.BlockSpec((tn, k), lambda i, j: (j, 0)), o_spec, o_spec]
        + [ANY_SPEC] * len(ties),
        out_specs=[o_spec] * 2,
        compiler_params=_params(("parallel", "parallel")),
    )(dxo, wd, g, u, *ties))


def _rw(name, fn, ins, outs, accs=(), tr=512, ncb=1):
    t = next(a.shape[0] for kind, a, _, _ in ins if kind == "row")
    assert t % tr == 0
    n_in = len(ins)
    n_pieces = sum(len(w) for w, _ in outs)

    def spec(kind, arr, width, base):
        if kind == "row":
            return pl.BlockSpec((tr, width), lambda j, i: (i, base + j))
        return pl.BlockSpec((arr.shape[0], width), lambda j, i: (0, base + j))

    in_specs = [spec(*s) for s in ins]
    out_shapes, out_specs = [], []
    for widths, dt in outs:
        w = sum(widths)
        out_shapes.append(jax.ShapeDtypeStruct((t, w * ncb), dt))
        out_specs.append(pl.BlockSpec((tr, w), lambda j, i: (i, j)))
    for rows, width in accs:
        out_shapes.append(jax.ShapeDtypeStruct((rows, width * ncb), F32))
        out_specs.append(pl.BlockSpec((rows, width), lambda j, i: (0, j)))

    ties = _Order.take()
    nt_ = len(ties)
    in_specs = in_specs + [ANY_SPEC] * nt_

    def body(*refs):
        vals = [r[...] for r in refs[:n_in]]
        res = fn(*vals)
        o_refs = refs[n_in + nt_:n_in + nt_ + len(outs)]
        a_refs = refs[n_in + nt_ + len(outs):]
        p = 0
        for (widths, _), o_ref in zip(outs, o_refs):
            off = 0
            for w in widths:
                if len(widths) == 1:
                    o_ref[...] = res[p].astype(o_ref.dtype)
                else:
                    o_ref[:, off:off + w] = res[p].astype(o_ref.dtype)
                off += w
                p += 1
        i = pl.program_id(1)
        for a_ref, v in zip(a_refs, res[n_pieces:]):
            @pl.when(i == 0)
            def _(a_ref=a_ref, v=v):
                a_ref[...] = v

            @pl.when(i > 0)
            def _(a_ref=a_ref, v=v):
                a_ref[...] += v

    return _Order.done(pl.pallas_call(
        body, name=name, out_shape=out_shapes,
        grid=(ncb, t // tr), in_specs=in_specs, out_specs=out_specs,
        compiler_params=_params(("parallel", "arbitrary")),
    )(*[a for _, a, _, _ in ins], *ties))


def _rms(x, g):
    x = x.astype(F32)
    return x * lax.rsqrt(jnp.mean(x * x, axis=-1, keepdims=True) + EPS) * g


def _silu(x):
    return x * jax.nn.sigmoid(x)


def _colsum(v):
    return jnp.sum(v, axis=0, keepdims=True)


def _pair_norm(x, g):
    w = 2 * HD
    ri = lax.broadcasted_iota(jnp.int32, (w, w), 0)
    ci = lax.broadcasted_iota(jnp.int32, (w, w), 1)
    same_head = ((ri < HD) == (ci < HD)).astype(F32)
    ms = _spread(x * x, same_head) * (1.0 / HD)
    return x * lax.rsqrt(ms + EPS) * g


ATTN_SCALE = 1.0 / math.sqrt(HD)


def _attn_bias(coef):
    key = lax.broadcasted_iota(jnp.int32, (QB, QB), 0)
    qry = lax.broadcasted_iota(jnp.int32, (QB, QB), 1)
    dist = (qry - key).astype(F32)
    own = jnp.where(qry >= key, -coef * dist, NEG)
    prev = jnp.where(qry <= key, -coef * (dist + float(QB)), NEG)
    return own, prev


def _attn_pair(qn, kcn, kpn, vc, vp, b_own, b_prev):
    nb = qn.shape[0]
    w = 2 * HD
    lane = lax.broadcasted_iota(jnp.int32, (1, 1, w), 2)
    eye = (lax.broadcasted_iota(jnp.int32, (QB, QB), 0)
           == lax.broadcasted_iota(jnp.int32, (QB, QB), 1)).astype(F32)
    out = jnp.zeros((nb, QB, w), F32)
    lb = jnp.zeros((nb * QB, w), F32)
    for hh in range(2):
        mask = ((lane < HD) if hh == 0 else (lane >= HD)).astype(F32)
        qm = qn * mask
        lc = _dot(kcn, qm, BNT) + b_own[hh]
        lp = _dot(kpn, qm, BNT) + b_prev[hh]
        m = lax.stop_gradient(jnp.maximum(jnp.max(lc, axis=1, keepdims=True),
                                          jnp.max(lp, axis=1, keepdims=True)))
        pc = jnp.exp(lc - m)
        pp = jnp.exp(lp - m)
        l = jnp.sum(pc, axis=1, keepdims=True) + jnp.sum(pp, axis=1, keepdims=True)
        inv = 1.0 / l
        out = out + (_dot(pc * inv, vc, BTN) + _dot(pp * inv, vp, BTN)) * mask
        diag = (eye * (m + jnp.log(l))).reshape(nb * QB, QB)
        lb = lb + _spread(diag, jnp.broadcast_to(mask[0], (QB, w)))
    return out, lb.reshape(nb, QB, w)


NORM_ROWS = 128
NORM_UNROLL = 4
EPILOGUE_ROWS = 512
ATTN_BATCH_FWD = 16
ATTN_BATCH_BWD = 16


def _unit_rows(u, d):
    r = u & (d - 1)
    n = u >> (d.bit_length() - 1)

    def rows(blk):
        start = pl.multiple_of(blk * (QB * d), QB * d)
        return pl.ds(start, QB) if d == 1 else pl.ds(start + r, QB, stride=d)

    return rows(n), rows(jnp.maximum(n - 1, 0)), n == 0


def _unit_batch(i, nbatch, d, bias, qf, kf, vf):
    units = [_unit_rows(i * nbatch + j, d) for j in range(nbatch)]
    cur = lambda ref: jnp.stack([ref[c, :] for c, _, _ in units])
    prv = lambda ref: jnp.stack([ref[p, :] for _, p, _ in units])
    b_own = [b[0] for b in bias]
    b_prev = [jnp.stack([jnp.where(first, NEG, b[1]) for _, _, first in units]) for b in bias]
    return units, (cur(qf), cur(kf), prv(kf), cur(vf), prv(vf), b_own, b_prev)


def _q_norm(x, g):
    return _pair_norm(x, g * ATTN_SCALE)


def _attn_prologue(t, q_ref, k_ref, v_ref, qg_ref, kg_ref, qf, kf, vf):
    def chunk(c, carry):
        rows = pl.ds(pl.multiple_of(c * NORM_ROWS, NORM_ROWS), NORM_ROWS)
        qf[rows, :] = _q_norm(q_ref[rows, :].astype(F32), qg_ref[...])
        kf[rows, :] = _pair_norm(k_ref[rows, :].astype(F32), kg_ref[...])
        vf[rows, :] = v_ref[rows, :].astype(F32)
        return carry
    lax.fori_loop(0, t // NORM_ROWS, chunk, 0, unroll=NORM_UNROLL)


def _attn_specs(t, bases):
    w = 2 * HD
    ins = [pl.BlockSpec((t, w), functools.partial(lambda p, c, b: (0, b + p), b=b)) for b in bases]
    gain = pl.BlockSpec((1, w), lambda p, c: (0, 0))
    blk = pl.BlockSpec((t, w), lambda p, c: (0, p))
    return ins, gain, blk


def _attn_fwd(name, proj, bases, qg, kg, coefs, d):
    t = proj.shape[0]
    npairs = coefs.shape[0] // 2
    w = 2 * HD
    ins, gain, blk = _attn_specs(t, bases)

    def body(coef_ref, q_ref, k_ref, v_ref, qg_ref, kg_ref, o_ref, l_ref, qf, kf, vf):
        p = pl.program_id(0)
        bias = (_attn_bias(coef_ref[2 * p]), _attn_bias(coef_ref[2 * p + 1]))
        _attn_prologue(t, q_ref, k_ref, v_ref, qg_ref, kg_ref, qf, kf, vf)

        def step(i, carry):
            units, ins = _unit_batch(i, ATTN_BATCH_FWD, d, bias, qf, kf, vf)
            o, lb = _attn_pair(*ins)
            for j, (cur, _, _) in enumerate(units):
                o_ref[cur, :] = o[j]
                l_ref[cur, :] = lb[j]
            return carry

        lax.fori_loop(0, t // QB // ATTN_BATCH_FWD, step, 0)

    return pl.pallas_call(
        body, name=name,
        out_shape=[jax.ShapeDtypeStruct((t, npairs * w), F32)] * 2,
        grid_spec=pltpu.PrefetchScalarGridSpec(
            num_scalar_prefetch=1, grid=(npairs,),
            in_specs=ins + [gain, gain], out_specs=[blk, blk],
            scratch_shapes=[pltpu.VMEM((t, w), F32)] * 3),
        compiler_params=_params(("arbitrary",)),
    )(coefs, proj, proj, proj, qg, kg)


def _attn_bwd(name, proj, bases, qg, kg, coefs, d, do, dl):
    t = proj.shape[0]
    npairs = coefs.shape[0] // 2
    w = 2 * HD
    ins, gain, blk = _attn_specs(t, bases)

    def body(coef_ref, q_ref, k_ref, v_ref, qg_ref, kg_ref, do_ref, dl_ref,
             dq_ref, dk_ref, dv_ref, dqg_ref, dkg_ref, qf, kf, vf, dqf, dkf, dvf):
        p = pl.program_id(0)
        bias = (_attn_bias(coef_ref[2 * p]), _attn_bias(coef_ref[2 * p + 1]))
        _attn_prologue(t, q_ref, k_ref, v_ref, qg_ref, kg_ref, qf, kf, vf)
        dkf[...] = jnp.zeros_like(dkf)
        dvf[...] = jnp.zeros_like(dvf)

        def step(i, carry):
            units, ins = _unit_batch(i, ATTN_BATCH_BWD, d, bias, qf, kf, vf)
            f = lambda a, b, c, e, g: _attn_pair(a, b, c, e, g, *ins[5:])
            _, vjp = jax.vjp(f, *ins[:5])
            cot = (jnp.stack([do_ref[cur, :] for cur, _, _ in units]),
                   jnp.stack([dl_ref[cur, :] for cur, _, _ in units]))
            dq, dkc, dkp, dvc, dvp = vjp(cot)
            for j, (cur, prv, _) in enumerate(units):
                dqf[cur, :] = dq[j]
                dkf[cur, :] += dkc[j]
                dkf[prv, :] += dkp[j]
                dvf[cur, :] += dvc[j]
                dvf[prv, :] += dvp[j]
            return carry

        lax.fori_loop(0, t // QB // ATTN_BATCH_BWD, step, 0)

        def chunk(c, carry):
            dqg_acc, dkg_acc = carry
            rows = pl.ds(pl.multiple_of(c * EPILOGUE_ROWS, EPILOGUE_ROWS), EPILOGUE_ROWS)
            _, vq = jax.vjp(_q_norm, q_ref[rows, :].astype(F32), qg_ref[...])
            dq, dqg = vq(dqf[rows, :])
            _, vk = jax.vjp(_pair_norm, k_ref[rows, :].astype(F32), kg_ref[...])
            dk, dkg = vk(dkf[rows, :])
            dq_ref[rows, :] = dq.astype(dq_ref.dtype)
            dk_ref[rows, :] = dk.astype(dk_ref.dtype)
            dv_ref[rows, :] = dvf[rows, :].astype(dv_ref.dtype)
            return dqg_acc + dqg, dkg_acc + dkg

        zero = jnp.zeros((1, w), F32)
        dqg, dkg = lax.fori_loop(0, t // EPILOGUE_ROWS, chunk, (zero, zero))

        @pl.when(p == 0)
        def _():
            dqg_ref[...] = dqg
            dkg_ref[...] = dkg

        @pl.when(p > 0)
        def _():
            dqg_ref[...] += dqg
            dkg_ref[...] += dkg

    big = jax.ShapeDtypeStruct((t, npairs * w), ACT_DTYPE)
    small = jax.ShapeDtypeStruct((1, w), F32)
    return pl.pallas_call(
        body, name=name,
        out_shape=[big, big, big, small, small],
        grid_spec=pltpu.PrefetchScalarGridSpec(
            num_scalar_prefetch=1, grid=(npairs,),
            in_specs=ins + [gain, gain, blk, blk],
            out_specs=[blk, blk, blk, gain, gain],
            scratch_shapes=[pltpu.VMEM((t, w), F32)] * 6),
        compiler_params=_params(("arbitrary",)),
    )(coefs, proj, proj, proj, qg, kg, do, dl)


CONV_ROWS = 256
CONV_HALO = 8


def _rows_back(x, s):
    return x if s == 0 else pltpu.roll(x, s, 0)


def _rows_ahead(x, s):
    return x if s == 0 else pltpu.roll(x, x.shape[0] - s, 0)


def _conv_pre(u, w, b):
    y = b
    for kk in range(SSD_K):
        y = y + w[kk:kk + 1, :] * _rows_back(u, SSD_K - 1 - kk)
    return y


def _stage_padded(dst, src_ref, t):
    zeros = jnp.zeros((CONV_HALO, dst.shape[1]), F32)
    dst[0:CONV_HALO, :] = zeros
    dst[t + CONV_HALO:t + 2 * CONV_HALO, :] = zeros
    dst[CONV_HALO:t + CONV_HALO, :] = src_ref[...].astype(F32)


def _chunk_rows(c):
    r0 = pl.multiple_of(c * CONV_ROWS, CONV_ROWS)
    return pl.ds(r0, CONV_ROWS + 2 * CONV_HALO), pl.ds(r0, CONV_ROWS)


def _conv_fwd(name, src, base, w, b, cw=128):
    t = src.shape[0]
    c = w.shape[1]
    centre = slice(CONV_HALO, CONV_HALO + CONV_ROWS)

    def body(u_ref, w_ref, b_ref, o_ref, up):
        _stage_padded(up, u_ref, t)
        wv, bv = w_ref[...], b_ref[...]

        def chunk(ci, carry):
            ext, rows = _chunk_rows(ci)
            y = _conv_pre(up[ext, :], wv, bv)
            o_ref[rows, :] = _silu(y)[centre].astype(o_ref.dtype)
            return carry

        lax.fori_loop(0, t // CONV_ROWS, chunk, 0)

    return pl.pallas_call(
        body, name=name, out_shape=jax.ShapeDtypeStruct((t, c), ACT_DTYPE),
        grid=(c // cw,),
        in_specs=[pl.BlockSpec((t, cw), lambda j: (0, base + j)),
                  pl.BlockSpec((SSD_K, cw), lambda j: (0, j)),
                  pl.BlockSpec((1, cw), lambda j: (0, j))],
        out_specs=pl.BlockSpec((t, cw), lambda j: (0, j)),
        scratch_shapes=[pltpu.VMEM((t + 2 * CONV_HALO, cw), F32)],
        compiler_params=_params(("parallel",)),
    )(src, w, b)


def _conv_bwd(name, src, base, w, b, dout, cw=128):
    t = src.shape[0]
    c = w.shape[1]

    centre = slice(CONV_HALO, CONV_HALO + CONV_ROWS)

    def body(u_ref, w_ref, b_ref, d_ref, du_ref, dw_ref, db_ref, up, dp):
        _stage_padded(up, u_ref, t)
        _stage_padded(dp, d_ref, t)
        wv, bv = w_ref[...], b_ref[...]

        def chunk(ci, carry):
            dws, db = carry
            ext, rows = _chunk_rows(ci)
            u = up[ext, :]
            y = _conv_pre(u, wv, bv)
            sg = jax.nn.sigmoid(y)
            dy = dp[ext, :] * (sg * (1.0 + y * (1.0 - sg)))
            du = jnp.zeros_like(u)
            new_dws = []
            for kk in range(SSD_K):
                s = SSD_K - 1 - kk
                du = du + wv[kk:kk + 1, :] * _rows_ahead(dy, s)
                new_dws.append(dws[kk] + _colsum((dy * _rows_back(u, s))[centre]))
            du_ref[rows, :] = du[centre].astype(du_ref.dtype)
            return tuple(new_dws), db + _colsum(dy[centre])

        zero = jnp.zeros((1, cw), F32)
        dws, db = lax.fori_loop(0, t // CONV_ROWS, chunk, ((zero,) * SSD_K, zero))
        for kk in range(SSD_K):
            dw_ref[kk:kk + 1, :] = dws[kk]
        db_ref[...] = db

    return pl.pallas_call(
        body, name=name,
        out_shape=[jax.ShapeDtypeStruct((t, c), ACT_DTYPE),
                   jax.ShapeDtypeStruct((SSD_K, c), F32),
                   jax.ShapeDtypeStruct((1, c), F32)],
        grid=(c // cw,),
        in_specs=[pl.BlockSpec((t, cw), lambda j: (0, base + j)),
                  pl.BlockSpec((SSD_K, cw), lambda j: (0, j)),
                  pl.BlockSpec((1, cw), lambda j: (0, j)),
                  pl.BlockSpec((t, cw), lambda j: (0, j))],
        out_specs=[pl.BlockSpec((t, cw), lambda j: (0, j)),
                   pl.BlockSpec((SSD_K, cw), lambda j: (0, j)),
                   pl.BlockSpec((1, cw), lambda j: (0, j))],
        scratch_shapes=[pltpu.VMEM((t + 2 * CONV_HALO, cw), F32)] * 2,
        compiler_params=_params(("parallel",)),
    )(src, w, b, dout)


def _softplus(x):
    return jnp.maximum(x, 0.0) + jnp.log(1.0 + jnp.exp(-jnp.abs(x)))


def _ssd_chunk(xbc, dtraw, bias, alog, states):
    wd = states[0].shape[1]
    nj = wd // SSD_P
    inner = SSD_G * wd
    dt = _softplus(dtraw + bias)
    a = dt * (-jnp.exp(alog))
    li = lax.broadcasted_iota(jnp.int32, (SSD_Q, SSD_Q), 0)
    si = lax.broadcasted_iota(jnp.int32, (SSD_Q, SSD_Q), 1)
    causal = li >= si
    acs = _running_sum(a, causal.astype(F32))
    acs_t = acs.T
    a_last = acs[SSD_Q - 1:SSD_Q, :]
    grow = jnp.exp(acs)
    shrink = jnp.exp(a_last - acs)
    hrow = lax.broadcasted_iota(jnp.int32, (LANE, wd), 0)
    wcol = lax.broadcasted_iota(jnp.int32, (LANE, wd), 1)
    lane = lax.broadcasted_iota(jnp.int32, (1, LANE), 1)
    ys, snext = [], []
    for g in range(SSD_G):
        lo = (hrow - g * nj) * SSD_P
        head_lanes = jnp.logical_and(wcol >= lo, wcol < lo + SSD_P).astype(F32)
        xs = xbc[:, g * wd:(g + 1) * wd]
        bm = xbc[:, inner + g * SSD_N:inner + (g + 1) * SSD_N]
        cm = xbc[:, inner + (SSD_G + g) * SSD_N:inner + (SSD_G + g + 1) * SSD_N]
        xdt = xs * _dot(dt, head_lanes)
        grow_x = _spread(grow, head_lanes)
        y_off = _dot(cm, states[g]) * grow_x
        s_new = (states[g] * grow_x[SSD_Q - 1:SSD_Q, :]
                 + _dot(bm, xdt * _dot(shrink, head_lanes), TN))
        cb = _dot(cm, bm, NT)
        pieces = []
        for i in range(wd // LANE):
            xp = xdt[:, i * LANE:(i + 1) * LANE]
            acc = jnp.zeros((SSD_Q, LANE), F32)
            for hh in range(LANE // SSD_P):
                h = g * nj + i * (LANE // SSD_P) + hh
                decay = jnp.exp(jnp.where(causal, acs[:, h:h + 1] - acs_t[h:h + 1, :], NEG))
                keep = jnp.logical_and(lane >= hh * SSD_P, lane < (hh + 1) * SSD_P).astype(F32)
                acc = acc + _dot(cb * decay, xp * keep)
            pieces.append(acc)
        y_diag = pieces[0] if len(pieces) == 1 else jnp.concatenate(pieces, axis=1)
        ys.append(y_diag + y_off)
        snext.append(s_new)
    return ys, snext


def _ssd_specs(cdim, wd, rev, nc):
    ch = (lambda c: nc - 1 - c) if rev else (lambda c: c)
    full = lambda width: pl.BlockSpec((SSD_Q, width), lambda c: (ch(c), 0))
    vec = pl.BlockSpec((1, LANE), lambda c: (0, 0))
    st = pl.BlockSpec((1, SSD_G, SSD_N, wd), lambda c: (ch(c), 0, 0, 0))
    return full, vec, st


def _ssd_fwd(name, xbc, dtraw, bias, alog, inner):
    t, cdim = xbc.shape
    wd = inner // SSD_G
    nc = t // SSD_Q
    full, vec, st = _ssd_specs(cdim, wd, False, nc)

    def body(x_ref, r_ref, b_ref, a_ref, y_ref, st_ref, s_scr):
        @pl.when(pl.program_id(0) == 0)
        def _():
            s_scr[...] = jnp.zeros_like(s_scr)

        sprev = [s_scr[g] for g in range(SSD_G)]
        ys, snext = _ssd_chunk(x_ref[...].astype(F32), r_ref[...], b_ref[...], a_ref[...], sprev)
        for g in range(SSD_G):
            st_ref[0, g] = sprev[g]
            y_ref[:, g * wd:(g + 1) * wd] = ys[g]
            s_scr[g] = snext[g]

    return pl.pallas_call(
        body, name=name,
        out_shape=[jax.ShapeDtypeStruct((t, inner), F32),
                   jax.ShapeDtypeStruct((nc, SSD_G, SSD_N, wd), F32)],
        grid=(nc,),
        in_specs=[full(cdim), full(LANE), vec, vec],
        out_specs=[full(inner), st],
        scratch_shapes=[pltpu.VMEM((SSD_G, SSD_N, wd), F32)],
        compiler_params=_params(("arbitrary",)),
    )(xbc, dtraw, bias, alog)


def _ssd_bwd(name, xbc, dtraw, bias, alog, states, dy, dxs_extra):
    t, cdim = xbc.shape
    inner = dy.shape[1]
    wd = inner // SSD_G
    nc = t // SSD_Q
    full, vec, st = _ssd_specs(cdim, wd, True, nc)

    def body(x_ref, r_ref, b_ref, a_ref, st_ref, dy_ref, dx0_ref,
             dx_ref, dr_ref, db_ref, da_ref, ds_scr):
        first = pl.program_id(0) == 0

        @pl.when(first)
        def _():
            ds_scr[...] = jnp.zeros_like(ds_scr)

        sprev = [st_ref[0, g] for g in range(SSD_G)]
        _, vjp = jax.vjp(_ssd_chunk, x_ref[...].astype(F32), r_ref[...], b_ref[...], a_ref[...],
                         sprev)
        dyv = dy_ref[...]
        dys = [dyv[:, g * wd:(g + 1) * wd] for g in range(SSD_G)]
        dsn = [ds_scr[g] for g in range(SSD_G)]
        dx, dr, db, da, dsp = vjp((dys, dsn))
        dx_ref[:, :inner] = dx[:, :inner] + dx0_ref[...].astype(F32)
        dx_ref[:, inner:] = dx[:, inner:]
        dr_ref[...] = dr
        for g in range(SSD_G):
            ds_scr[g] = dsp[g]

        @pl.when(first)
        def _():
            db_ref[...] = db
            da_ref[...] = da

        @pl.when(jnp.logical_not(first))
        def _():
            db_ref[...] += db
            da_ref[...] += da

    return pl.pallas_call(
        body, name=name,
        out_shape=[jax.ShapeDtypeStruct((t, cdim), F32),
                   jax.ShapeDtypeStruct((t, LANE), F32),
                   jax.ShapeDtypeStruct((1, LANE), F32),
                   jax.ShapeDtypeStruct((1, LANE), F32)],
        grid=(nc,),
        in_specs=[full(cdim), full(LANE), vec, vec, st, full(inner), full(inner)],
        out_specs=[full(cdim), full(LANE), vec, vec],
        scratch_shapes=[pltpu.VMEM((SSD_G, SSD_N, wd), F32)],
        compiler_params=_params(("arbitrary",)),
    )(xbc, dtraw, bias, alog, states, dy, dxs_extra)


def _mix(o0, o1, o2, l0, l1, l2):
    m = lax.stop_gradient(jnp.maximum(jnp.maximum(l0, l1), l2))
    e0, e1, e2 = jnp.exp(l0 - m), jnp.exp(l1 - m), jnp.exp(l2 - m)
    return (e0 * o0 + e1 * o1 + e2 * o2) / (e0 + e1 + e2)


def _gate(y, xs, z, dexp, gain):
    v = (y + xs.astype(F32) * dexp) * _silu(z.astype(F32))
    return _rms(v, gain)


def _merge(ga, gs, ap, sp):
    return jax.nn.sigmoid(ga.astype(F32)) * ap + jax.nn.sigmoid(gs.astype(F32)) * sp


def _alibi_coefs(hp):
    n = hp * len(PATTERNS)
    slopes = np.exp2(-ALIBI_MAX_EXP * np.arange(1, n + 1, dtype=np.float32) / n).astype(np.float32)
    return [jnp.asarray(slopes[g * hp:(g + 1) * hp] * np.float32(d))
            for g, (_, d) in enumerate(PATTERNS)]


def _local_step(x, tgt, w, p, gw_=None, aw=None, dff=None):
    t, d = x.shape
    dff = w["gu1t"].shape[0] // 2 if dff is None else dff
    aw = w["abt"].shape[1] if aw is None else aw
    hp = aw // HD
    qkv = len(PATTERNS) * aw
    inner = p["ssd_norm"].shape[1]
    nh = p["dt_bias"].shape[1]
    gw_ = {} if gw_ is None else gw_
    gw = inner // SSD_G
    cdim = inner + 2 * SSD_G * SSD_N
    z_off, xbc_off = 3 * qkv, 3 * qkv + inner
    ga_off = xbc_off + cdim
    gs_off = ga_off + d
    hw = d // 2
    assert z_off % gw == 0 and xbc_off % LANE == 0 and ga_off % hw == 0 and gs_off % hw == 0
    assert (nh // SSD_G) * SSD_P == gw and hp % 2 == 0 and aw % LANE == 0 and nh <= LANE
    gdt = MXU_DTYPE

    row = lambda a, width, base=0: ("row", a, width, base)
    const = lambda a, width, base=0: ("const", a, width, base)

    def rms_fwd(name, xin, g):
        return _rw(name, lambda xv, gv: (_rms(xv, gv),), [row(xin, d), const(g, d)],
                   [((d,), ACT_DTYPE)])[0]

    def rms_bwd(name, xin, g, dh, dres):
        def fn(xv, gv, dhv, drv):
            _, vjp = jax.vjp(_rms, xv, gv)
            dx, dg = vjp(dhv.astype(F32))
            return drv + dx, dg
        return _rw(name, fn, [row(xin, d), const(g, d), row(dh, d), row(dres, d)],
                   [((d,), F32)], accs=[(1, d)])

    def ffn_fwd(tag, xin, g, key_gu, key_d):
        h = rms_fwd(tag + "_norm", xin, g)
        gate, up, a = _ffn_up(tag + "_up", h, w[key_gu])
        xo = _mm(tag + "_down", a, w[key_d], "nn", F32, res=xin, scale=0.5)
        return xo, (h, gate, up, a)

    def ffn_bwd(tag, xin, g, wgut, wd, saved, dxo, key_gu, key_d):
        h, gate, up, a = saved
        gw_[key_d] = _mm(tag + "_dwd", a, dxo, "tn", gdt, scale=0.5)
        dgu = _ffn_dact(tag + "_dact", dxo, wd, gate, up, 0.5)
        gw_[key_gu] = _mm(tag + "_dwgu", dgu, h, "tn", gdt)
        dh = _mm(tag + "_dh", dgu, wgut, "nn", F32)
        return rms_bwd(tag + "_dnorm", xin, g, dh, dxo)

    x1, ffn1_saved = ffn_fwd("ffn1", x, p["ffn1_norm"], "gu1t", "d1")
    if hasattr(w, "after_first_ffn"):
        w.after_first_ffn()
    h2 = rms_fwd("mix_norm", x1, p["mix_norm"])
    proj = _mm("in_proj", h2, w["maint"], "nt", ACT_DTYPE, cap_m=ROW_TILE_CAP,
               cap_n=WIDE_N_CAP)
    dtraw = _mm("dt_proj", h2, w["dtt"], "nt", F32)

    coefs = _alibi_coefs(hp)
    qg2 = jnp.concatenate([p["q_norm"], p["q_norm"]], axis=1)
    kg2 = jnp.concatenate([p["k_norm"], p["k_norm"]], axis=1)
    pw = 2 * HD
    attn_bases = [[(off + gi * aw) // pw for off in (0, qkv, 2 * qkv)]
                  for gi in range(len(PATTERNS))]
    attn_o, attn_l = [], []
    for gi, (_, dil) in enumerate(PATTERNS):
        o, l = _attn_fwd(f"attn_fwd{gi}", proj, attn_bases[gi], qg2, kg2, coefs[gi], dil)
        attn_o.append(o)
        attn_l.append(l)
    ao = _rw("attn_mix", lambda *v: (_mix(*v),), [row(a, aw) for a in attn_o + attn_l],
             [((aw,), ACT_DTYPE)])[0]

    xbc = _conv_fwd("conv_fwd", proj, xbc_off // LANE, p["conv_w"], p["conv_b"])
    pad = lambda v: jnp.pad(v, ((0, 0), (0, LANE - nh)))
    bias_p, alog_p = pad(p["dt_bias"]), pad(p["a_log"])
    yssd, states = _ssd_fwd("ssd_fwd", xbc, dtraw, bias_p, alog_p, inner)
    dexp = jnp.repeat(p["d_skip"], SSD_P, axis=1)
    gate_ins = [row(yssd, gw), row(xbc, gw), row(proj, gw, z_off // gw),
                const(dexp, gw), const(p["ssd_norm"], gw)]
    yn = _rw("ssd_gate", lambda *v: (_gate(*v),), gate_ins, [((gw,), ACT_DTYPE)], ncb=SSD_G)[0]

    ap = _mm("attn_out", ao, w["abt"], "nt", F32)
    sp = _mm("ssd_out", yn, w["sb"], "nn", F32)
    merge_ins = [row(proj, hw, ga_off // hw), row(proj, hw, gs_off // hw), row(ap, hw), row(sp, hw)]
    mg = _rw("merge", lambda *v: (_merge(*v),), merge_ins, [((hw,), ACT_DTYPE)], ncb=2)[0]
    x2 = _mm("mix_out", mg, w["out"], "nn", F32, res=x1)
    x3, ffn2_saved = ffn_fwd("ffn2", x2, p["ffn2_norm"], "gu2t", "d2")

    def loss_fn(yv, tv):
        e = yv - tv
        return e * (1.0 / d), _colsum(e * e)
    dy, loss_vec = _rw("loss", loss_fn, [row(x3, d), row(tgt, d)], [((d,), F32)], accs=[(1, d)])

    gp = {}
    dx2, gp["ffn2_norm"] = ffn_bwd(
        "ffn2", x2, p["ffn2_norm"], w["gu2t"], w["d2"], ffn2_saved, dy, "gu2t", "d2")
    dmg = _mm("d_merge", dx2, w["out"], "nt", ACT_DTYPE)
    gw_["out"] = _mm("dw_out", mg, dx2, "tn", gdt)

    def merge_bwd(gav, gsv, apv, spv, dv):
        _, vjp = jax.vjp(_merge, gav, gsv, apv, spv)
        return vjp(dv.astype(F32))
    dga, dgs, dap, dsp = _rw("d_merge_gate", merge_bwd, merge_ins + [row(dmg, hw)],
                             [((hw,), ACT_DTYPE)] * 4, ncb=2)
    gw_["abt"] = _mm("dw_ab", dap, ao, "tn", gdt)
    dao = _mm("d_attn_o", dap, w["abt"], "nn", F32)
    gw_["sb"] = _mm("dw_sb", yn, dsp, "tn", gdt)
    dyn = _mm("d_ssd_y", dsp, w["sb"], "nt", F32)

    def gate_bwd(yv, xv, zv, dev, gv, dv):
        _, vjp = jax.vjp(_gate, yv, xv, zv, dev, gv)
        return vjp(dv)
    dyssd, dxs_gate, dz, ddexp, gp["ssd_norm"] = _rw(
        "d_ssd_gate", gate_bwd, gate_ins + [row(dyn, gw)],
        [((gw,), F32), ((gw,), F32), ((gw,), ACT_DTYPE)], accs=[(1, gw), (1, gw)], ncb=SSD_G)
    gp["d_skip"] = ddexp.reshape(nh, SSD_P).sum(axis=1).reshape(1, nh)

    dxbc, ddtraw, dbias, dalog = _ssd_bwd("ssd_bwd", xbc, dtraw, bias_p, alog_p, states,
                                          dyssd, dxs_gate)
    gp["dt_bias"], gp["a_log"] = dbias[:, :nh], dalog[:, :nh]
    du, gp["conv_w"], gp["conv_b"] = _conv_bwd("conv_bwd", proj, xbc_off // LANE,
                                               p["conv_w"], p["conv_b"], dxbc)

    def mix_bwd(*v):
        _, vjp = jax.vjp(_mix, *v[:6])
        return vjp(v[6])
    dmix = _rw("d_attn_mix", mix_bwd, [row(a, aw) for a in attn_o + attn_l] + [row(dao, aw)],
               [((aw,), F32)] * 6)
    dq, dk, dv = [], [], []
    dqg = dkg = None
    for gi, (_, dil) in enumerate(PATTERNS):
        r = _attn_bwd(f"attn_bwd{gi}", proj, attn_bases[gi], qg2, kg2, coefs[gi], dil,
                      dmix[gi], dmix[3 + gi])
        dq.append(r[0])
        dk.append(r[1])
        dv.append(r[2])
        dqg = r[3] if dqg is None else dqg + r[3]
        dkg = r[4] if dkg is None else dkg + r[4]
    gp["q_norm"] = dqg[:, :HD] + dqg[:, HD:]
    gp["k_norm"] = dkg[:, :HD] + dkg[:, HD:]

    segs = dq + dk + dv + [dz, du, dga, dgs]
    gw_["maint"] = _mm("dw_in", segs, h2, "tn", gdt)
    gw_["dtt"] = _mm("dw_dt", ddtraw, h2, "tn", gdt)
    dh2 = _mm("d_h2_main", segs, w["maint"], "nn", F32)
    dh2 = _mm("d_h2_dt", ddtraw, w["dtt"], "nn", F32, res=dh2)
    dx1, gp["mix_norm"] = rms_bwd("d_mix_norm", x1, p["mix_norm"], dh2, dx2)
    dx0, gp["ffn1_norm"] = ffn_bwd(
        "ffn1", x, p["ffn1_norm"], w["gu1t"], w["d1"], ffn1_saved, dx1, "gu1t", "d1")
    return loss_vec, dx0, gw_, gp


MESH = pl.DeviceIdType.MESH
HBM_SPEC = pl.BlockSpec(memory_space=pltpu.HBM)


def _mesh_pos():
    return lax.axis_index("x"), lax.axis_index("y"), lax.axis_index("c")


def _flip(pos, k):
    x, y, c = pos
    return (1 - x if k & 4 else x, 1 - y if k & 2 else y, 1 - c if k & 1 else c)


def _dev_index(pos):
    return 4 * pos[0] + 2 * pos[1] + pos[2]


def _rows_of(ref, base, stride, rows, pos):
    start = pl.multiple_of(base + stride * _dev_index(pos), ROW_ALIGN)
    return ref.at[pl.ds(start, rows)]


SEM_SPEC =pl.BlockSpec(memory_space=pltpu.SEMAPHORE)
SIDE_EFFECT = pltpu.SideEffectType.DATAFLOW_SIDE_EFFECTING


def _split_refs(plan, srcs, lands, i, src_for, land_from):
    si, sbase, sstride, li, lbase, lstride, rows = plan[i]
    return (_rows_of(srcs[si], sbase, sstride, rows, src_for),
            _rows_of(lands[li], lbase, lstride, rows, land_from))


ALL_PEERS = tuple(range(1, NDEV))
SAME_CORE_AND_SIBLING = (1, 4, 2, 6)
OTHER_CHIPS = (4, 2, 6)


def _split_start(name, srcs, lands, plan, after=(), relations=ALL_PEERS):
    ns, nl, n = len(srcs), len(lands), len(plan)

    def body(*refs):
        s_refs = refs[:ns]
        l_refs = refs[ns:ns + nl]
        send_sems, recv_sems = refs[ns + nl + len(after):ns + nl + len(after) + 2]
        local_sems = refs[ns + nl + len(after) + 2]
        token = refs[ns + nl + len(after) + 3 + ns + nl]
        me = _mesh_pos()
        for i in range(n):
            src, dst = _split_refs(plan, s_refs, l_refs, i, me, me)
            pltpu.make_async_copy(src, dst, local_sems.at[i]).start()
        for k in relations:
            peer = _flip(me, k)
            for i in range(n):
                src, dst = _split_refs(plan, s_refs, l_refs, i, peer, me)
                pltpu.make_async_remote_copy(
                    src_ref=src, dst_ref=dst,
                    send_sem=send_sems.at[7 * i + k - 1], recv_sem=recv_sems.at[7 * i + k - 1],
                    device_id=peer, device_id_type=MESH).start()
        token[...] = jnp.zeros_like(token)

    hbm = lambda a: pltpu.HBM(a.shape, a.dtype)
    out_shape = ((pltpu.SemaphoreType.DMA((7 * n,)), pltpu.SemaphoreType.DMA((7 * n,)),
                  pltpu.SemaphoreType.DMA((n,)))
                 + tuple(hbm(a) for a in srcs) + tuple(hbm(a) for a in lands)
                 + (jax.ShapeDtypeStruct((8, LANE), F32),))
    out = pl.pallas_call(
        body, name=name, out_shape=out_shape,
        in_specs=[HBM_SPEC] * (ns + nl) + [ANY_SPEC] * len(after),
        out_specs=(SEM_SPEC, SEM_SPEC, SEM_SPEC) + (HBM_SPEC,) * (ns + nl)
        + (pl.BlockSpec(memory_space=pltpu.VMEM),),
        input_output_aliases={i: 3 + i for i in range(ns + nl)},
        compiler_params=pltpu.CompilerParams(has_side_effects=SIDE_EFFECT),
    )(*[pltpu.with_memory_space_constraint(a, pltpu.HBM) for a in tuple(srcs) + tuple(lands)],
      *after)
    _Order.tokens.append(out[-1])
    return out[0], out[1], out[2], out[3:3 + ns], out[3 + ns:3 + ns + nl]


def _split_wait(name, started, plan, relations=ALL_PEERS):
    send_sems, recv_sems, local_sems, srcs, lands = started
    ns, nl, n = len(srcs), len(lands), len(plan)
    after = [_Order.last] if _Order.last is not None else []

    def body(*refs):
        s_refs = refs[:ns]
        l_refs = refs[ns:ns + nl]
        send_sems, recv_sems, local_sems = refs[ns + nl:ns + nl + 3]
        me = _mesh_pos()
        for i in range(n):
            src, dst = _split_refs(plan, s_refs, l_refs, i, me, me)
            pltpu.make_async_copy(src, dst, local_sems.at[i]).wait()
        for k in relations:
            peer = _flip(me, k)
            for i in range(n):
                src, dst = _split_refs(plan, s_refs, l_refs, i, peer, peer)
                cp = pltpu.make_async_remote_copy(
                    src_ref=src, dst_ref=dst,
                    send_sem=send_sems.at[7 * i + k - 1], recv_sem=recv_sems.at[7 * i + k - 1],
                    device_id=peer, device_id_type=MESH)
                cp.wait_send()
                cp.wait_recv()

    hbm = lambda a: pltpu.HBM(a.shape, a.dtype)
    out = pl.pallas_call(
        body, name=name,
        out_shape=tuple(hbm(a) for a in srcs) + tuple(hbm(a) for a in lands),
        in_specs=[HBM_SPEC] * (ns + nl) + [SEM_SPEC] * 3 + [ANY_SPEC] * len(after),
        out_specs=(HBM_SPEC,) * (ns + nl),
        input_output_aliases={i: i for i in range(ns + nl)},
        compiler_params=pltpu.CompilerParams(has_side_effects=SIDE_EFFECT),
    )(*srcs, *lands, send_sems, recv_sems, local_sems, *after)
    return list(out[ns:])


def _forward_refs(plan, lands, i, block):
    _, _, _, li, lbase, lstride, rows = plan[i]
    return _rows_of(lands[li], lbase, lstride, rows, block)


def _forward_start(name, lands, plan):
    nl, n = len(lands), len(plan)

    def body(*refs):
        l_refs = refs[:nl]
        send_sems, recv_sems = refs[nl:nl + 2]
        token = refs[nl + 2 + nl]
        me = _mesh_pos()
        for j, kc in enumerate(OTHER_CHIPS):
            for i in range(n):
                rows = _forward_refs(plan, l_refs, i, _flip(me, kc))
                pltpu.make_async_remote_copy(
                    src_ref=rows, dst_ref=rows,
                    send_sem=send_sems.at[3 * i + j], recv_sem=recv_sems.at[3 * i + j],
                    device_id=_flip(me, 1), device_id_type=MESH).start()
        token[...] = jnp.zeros_like(token)

    hbm = lambda a: pltpu.HBM(a.shape, a.dtype)
    out = pl.pallas_call(
        body, name=name,
        out_shape=((pltpu.SemaphoreType.DMA((3 * n,)), pltpu.SemaphoreType.DMA((3 * n,)))
                   + tuple(hbm(a) for a in lands) + (jax.ShapeDtypeStruct((8, LANE), F32),)),
        in_specs=[HBM_SPEC] * nl,
        out_specs=(SEM_SPEC, SEM_SPEC) + (HBM_SPEC,) * nl
        + (pl.BlockSpec(memory_space=pltpu.VMEM),),
        input_output_aliases={i: 2 + i for i in range(nl)},
        compiler_params=pltpu.CompilerParams(has_side_effects=SIDE_EFFECT),
    )(*[pltpu.with_memory_space_constraint(a, pltpu.HBM) for a in lands])
    _Order.tokens.append(out[-1])
    return out[0], out[1], out[2:2 + nl]


def _forward_wait(name, started, plan):
    send_sems, recv_sems, lands = started
    nl, n = len(lands), len(plan)
    after = [_Order.last] if _Order.last is not None else []

    def body(*refs):
        l_refs = refs[:nl]
        send_sems, recv_sems = refs[nl:nl + 2]
        me = _mesh_pos()
        for j, kc in enumerate(OTHER_CHIPS):
            for i in range(n):
                sent = _forward_refs(plan, l_refs, i, _flip(me, kc))
                came = _forward_refs(plan, l_refs, i, _flip(_flip(me, 1), kc))
                cp = pltpu.make_async_remote_copy(
                    src_ref=sent, dst_ref=came,
                    send_sem=send_sems.at[3 * i + j], recv_sem=recv_sems.at[3 * i + j],
                    device_id=_flip(me, 1), device_id_type=MESH)
                cp.wait_send()
                cp.wait_recv()

    hbm = lambda a: pltpu.HBM(a.shape, a.dtype)
    out = pl.pallas_call(
        body, name=name, out_shape=tuple(hbm(a) for a in lands),
        in_specs=[HBM_SPEC] * nl + [SEM_SPEC] * 2 + [ANY_SPEC] * len(after),
        out_specs=(HBM_SPEC,) * nl,
        input_output_aliases={i: i for i in range(nl)},
        compiler_params=pltpu.CompilerParams(has_side_effects=SIDE_EFFECT),
    )(*lands, send_sems, recv_sems, *after)
    return list(out)


def _regroup_rows(name, padded, r, rp, lo, hi):
    d = padded.shape[1]
    pack = 4 // padded.dtype.itemsize
    assert r % pack == 0 and rp % ROW_ALIGN == 0 and lo % (8 * pack) == 0 and hi % (8 * pack) == 0
    r2, rp2, lo2, hi2 = r // pack, rp // pack, lo // pack, hi // pack
    u32 = jnp.uint32

    def body(x_ref, main_ref, cut_ref):
        x = pltpu.bitcast(x_ref[...], u32)
        joined = jnp.concatenate([x[rp2 * j:rp2 * j + r2] for j in range(NDEV)], axis=0)
        main = jnp.concatenate([joined[:lo2], joined[hi2:]], axis=0)
        cut = jnp.concatenate([joined[lo2:hi2], jnp.zeros((LANE // pack - (hi2 - lo2), LANE), u32)],
                              axis=0)
        main_ref[...] = pltpu.bitcast(main, padded.dtype)
        cut_ref[...] = pltpu.bitcast(cut, padded.dtype)

    return pl.pallas_call(
        body, name=name,
        out_shape=[jax.ShapeDtypeStruct((NDEV * r - (hi - lo), d), padded.dtype),
                   jax.ShapeDtypeStruct((LANE, d), padded.dtype)],
        grid=(d // LANE,),
        in_specs=[pl.BlockSpec((NDEV * rp, LANE), lambda i: (0, i))],
        out_specs=[pl.BlockSpec((NDEV * r - (hi - lo), LANE), lambda i: (0, i)),
                   pl.BlockSpec((LANE, LANE), lambda i: (0, i))],
        compiler_params=_params(("parallel",)),
    )(padded)


def _ungroup_rows(name, main, cut, r, rp, lo, hi):
    d = main.shape[1]
    pack = 4 // main.dtype.itemsize
    r2, rp2, lo2, hi2 = r // pack, rp // pack, lo // pack, hi // pack
    u32 = jnp.uint32

    def body(main_ref, cut_ref, o_ref):
        m = pltpu.bitcast(main_ref[...], u32)
        c = pltpu.bitcast(cut_ref[...], u32)
        joined = jnp.concatenate([m[:lo2], c[:hi2 - lo2], m[lo2:]], axis=0)
        zeros = jnp.zeros((rp2 - r2, LANE), u32)
        parts = []
        for j in range(NDEV):
            parts += [joined[r2 * j:r2 * (j + 1)], zeros]
        o_ref[...] = pltpu.bitcast(jnp.concatenate(parts, axis=0), main.dtype)

    return pl.pallas_call(
        body, name=name,
        out_shape=jax.ShapeDtypeStruct((NDEV * rp, d), main.dtype),
        grid=(d // LANE,),
        in_specs=[pl.BlockSpec((main.shape[0], LANE), lambda i: (0, i)),
                  pl.BlockSpec((LANE, LANE), lambda i: (0, i))],
        out_specs=pl.BlockSpec((NDEV * rp, LANE), lambda i: (0, i)),
        compiler_params=_params(("parallel",)),
    )(main, cut)


def _sum_slabs(name, a):
    s, r, c = a.shape

    def body(a_ref, o_ref):
        acc = a_ref[0].astype(F32)
        for i in range(1, s):
            acc = acc + a_ref[i].astype(F32)
        o_ref[...] = acc

    return pl.pallas_call(body, name=name, out_shape=jax.ShapeDtypeStruct((r, c), F32))(a)


def _adamw_update(g, w, m, v):
    mn = ADAM_B1 * m + (1.0 - ADAM_B1) * g
    vn = ADAM_B2 * v + (1.0 - ADAM_B2) * (g * g)
    m_hat = mn / (1.0 - ADAM_B1 ** ADAM_STEP)
    v_hat = vn / (1.0 - ADAM_B2 ** ADAM_STEP)
    delta = -ADAM_LR * (m_hat / (jnp.sqrt(v_hat) + ADAM_EPS) + ADAM_WD * w)
    return delta, mn, vn


def _adamw(name, gsrc, w, m, v, transposed=False, tr=256):
    s = gsrc.shape[0]
    r, c = w.shape
    step = LANE if transposed else 8
    tr = max(t for t in range(step, min(tr, r) + 1, step) if r % t == 0)

    def body(g_ref, w_ref, m_ref, v_ref, go_ref, d_ref, mo_ref, vo_ref):
        g = g_ref[0].astype(F32)
        for i in range(1, s):
            g = g + g_ref[i].astype(F32)
        if transposed:
            g = g.T[:, :c]
        delta, mn, vn = _adamw_update(g, w_ref[...], m_ref[...], v_ref[...])
        go_ref[...] = g
        d_ref[...] = delta
        mo_ref[...] = mn
        vo_ref[...] = vn

    blk = pl.BlockSpec((tr, c), lambda i: (i, 0))
    if transposed:
        g_spec = pl.BlockSpec((s, gsrc.shape[1], tr), lambda i: (0, 0, i))
    else:
        g_spec = pl.BlockSpec((s, tr, c), lambda i: (0, i, 0))
    return pl.pallas_call(
        body, name=name, out_shape=[jax.ShapeDtypeStruct(w.shape, F32)] * 4,
        grid=(r // tr,),
        in_specs=[g_spec, blk, blk, blk], out_specs=[blk] * 4,
        compiler_params=_params(("parallel",)),
    )(gsrc, w, m, v)


REPLICATED = ("ffn1_norm", "mix_norm", "q_norm", "k_norm", "conv_b", "dt_bias", "a_log",
              "d_skip", "ssd_norm", "ffn2_norm")
ALL_WEIGHTS = ("ffn1_norm", "ffn1_w_gate", "ffn1_w_up", "ffn1_w_down", "mix_norm", "w_in",
               "q_norm", "k_norm", "conv_w", "conv_b", "dt_bias", "a_log", "d_skip", "ssd_norm",
               "w_attn_branch", "w_ssd_branch", "w_out", "ffn2_norm", "ffn2_w_gate", "ffn2_w_up",
               "ffn2_w_down")
BIG = (("ffn1_w_gate", True, "gu1t", 0), ("ffn1_w_up", True, "gu1t", 1),
       ("ffn1_w_down", False, "d1", 0), ("w_in", True, "wint", 0),
       ("w_attn_branch", True, "abt", 0), ("w_ssd_branch", False, "sb", 0),
       ("w_out", False, "out", 0),
       ("ffn2_w_gate", True, "gu2t", 0), ("ffn2_w_up", True, "gu2t", 1),
       ("ffn2_w_down", False, "d2", 0))


def _nrows(shape, cols):
    return -(-math.prod(shape) // cols)


def _pack_rows(arrs, cols, row_tile):
    parts = []
    for a in arrs:
        flat = a.reshape(-1)
        nr = -(-flat.shape[0] // cols)
        parts.append(jnp.pad(flat, (0, nr * cols - flat.shape[0])).reshape(nr, cols))
    out = jnp.concatenate(parts, axis=0)
    return jnp.pad(out, ((0, _round_up(out.shape[0], row_tile) - out.shape[0]), (0, 0)))


def _unpack_rows(packed, shapes):
    cols = packed.shape[-1]
    out, r0 = [], 0
    for sh in shapes:
        nr = _nrows(sh, cols)
        out.append(packed[r0:r0 + nr].reshape(-1)[:math.prod(sh)].reshape(tuple(sh)))
        r0 += nr
    return out


def kernel(x, ffn1_norm, ffn1_w_gate, ffn1_w_up, ffn1_w_down, mix_norm, w_in, q_norm, k_norm, conv_w, conv_b, dt_bias, a_log, d_skip, ssd_norm, w_attn_branch, w_ssd_branch, w_out, ffn2_norm, ffn2_w_gate, ffn2_w_up, ffn2_w_down, loss_target, m_ffn1_norm, m_ffn1_w_gate, m_ffn1_w_up, m_ffn1_w_down, m_mix_norm, m_w_in, m_q_norm, m_k_norm, m_conv_w, m_conv_b, m_dt_bias, m_a_log, m_d_skip, m_ssd_norm, m_w_attn_branch, m_w_ssd_branch, m_w_out, m_ffn2_norm, m_ffn2_w_gate, m_ffn2_w_up, m_ffn2_w_down, v_ffn1_norm, v_ffn1_w_gate, v_ffn1_w_up, v_ffn1_w_down, v_mix_norm, v_w_in, v_q_norm, v_k_norm, v_conv_w, v_conv_b, v_dt_bias, v_a_log, v_d_skip, v_ssd_norm, v_w_attn_branch, v_w_ssd_branch, v_w_out, v_ffn2_norm, v_ffn2_w_gate, v_ffn2_w_up, v_ffn2_w_down):
    given = dict(locals())
    wts = {n: given[n] for n in ALL_WEIGHTS}
    mom = {n: given["m_" + n] for n in ALL_WEIGHTS}
    var = {n: given["v_" + n] for n in ALL_WEIGHTS}
    d = x.shape[-1]
    nh = dt_bias.shape[1]
    my = _dev_index(_mesh_pos())

    def row_form(n, col_sharded):
        a = wts[n][0].T if col_sharded else wts[n][0]
        a = jnp.pad(a, ((0, _round_up(a.shape[0], ROW_ALIGN) - a.shape[0]), (0, 0)))
        return a.astype(MXU_DTYPE)

    _Order.tokens, _Order.last = [], None
    shard = {n: row_form(n, cs) for n, cs, _, _ in BIG}
    entries = {buf: [e for e in BIG if e[2] == buf] for buf in dict.fromkeys(e[2] for e in BIG)}

    def buf_shape(buf):
        r, c = shard[entries[buf][0][0]].shape
        return (len(entries[buf]) * NDEV * r, c)

    def gather_plan(bufs):
        srcs, lands, plan = [], [], []
        for li, buf in enumerate(bufs):
            lands.append(lax.empty(buf_shape(buf), MXU_DTYPE))
            for n, _, _, pos in entries[buf]:
                r = shard[n].shape[0]
                plan.append((len(srcs), 0, 0, li, pos * NDEV * r, r, r))
                srcs.append(shard[n])
        return srcs, lands, plan

    def scatter_plan(bufs, grads):
        srcs, lands, plan, names = [], [], [], []
        for si, buf in enumerate(bufs):
            srcs.append(grads[buf])
            for n, _, _, pos in entries[buf]:
                r, c = shard[n].shape
                plan.append((si, pos * NDEV * r, r, len(lands), 0, r, r))
                lands.append(lax.empty((NDEV * r, c), MXU_DTYPE))
                names.append(n)
        return srcs, lands, plan, names

    conv_rows = _pack_rows([conv_w[0]], LANE, ROW_ALIGN)
    srcs1, lands1, plan1 = gather_plan(("gu1t", "d1"))
    plan1.append((len(srcs1), 0, 0, len(lands1), 0, conv_rows.shape[0], conv_rows.shape[0]))
    srcs1.append(conv_rows)
    lands1.append(lax.empty((NDEV * conv_rows.shape[0], LANE), F32))
    started1 = _split_start("gather_first_start", srcs1, lands1, plan1,
                            relations=SAME_CORE_AND_SIBLING)

    in_cols = w_in.shape[2]
    in_pad = _round_up(in_cols, ROW_ALIGN)
    dt_off = NDEV * in_cols - 2 * d - nh
    second_bufs = ("wint",)
    third_bufs = ("abt", "sb", "out", "gu2t", "d2")
    plan2 = gather_plan(second_bufs)
    started2, forwarded, started3, conv_landed = [], [], [], []

    class Weights(dict):
        def first_group(self):
            lands = _split_wait("gather_first_wait", started1, plan1,
                                relations=SAME_CORE_AND_SIBLING)
            fwd = _forward_start("gather_first_forward", lands, plan1)
            started2.append(_split_start("gather_in_start", *plan2, after=[lands[0]],
                                         relations=SAME_CORE_AND_SIBLING))
            lands = _forward_wait("gather_first_arrive", fwd, plan1)
            self["gu1t"], self["d1"] = lands[0], lands[1]
            conv_landed.append(lands[2])

        def after_first_ffn(self):
            lands = _split_wait("gather_in_wait", started2[0], plan2[2],
                                relations=SAME_CORE_AND_SIBLING)
            forwarded.append(_forward_start("gather_in_forward", lands, plan2[2]))

        def __missing__(self, key):
            if key in ("gu1t", "d1"):
                self.first_group()
            elif key in ("maint", "dtt"):
                wint = _forward_wait("gather_in_arrive", forwarded[0], plan2[2])[0]
                plan3 = gather_plan(third_bufs)
                started3.append((_split_start("gather_rest_start", *plan3, after=[wint]), plan3[2]))
                self["maint"], self["dtt"] = _regroup_rows(
                    "regroup_w_in", wint, in_cols, in_pad, dt_off, dt_off + nh)
            else:
                st, plan = started3[0]
                for buf, a in zip(third_bufs, _split_wait("gather_rest_wait", st, plan)):
                    self[buf] = a
            return self[key]

    w = Weights()

    class Params(dict):
        def __missing__(self, key):
            assert key == "conv_w" and conv_landed
            conv_all = conv_landed[0].reshape(NDEV, conv_rows.shape[0] * LANE)
            conv_all = conv_all[:, :math.prod(conv_w.shape[1:])]
            self[key] = (conv_all.reshape((NDEV,) + conv_w.shape[1:]).transpose(1, 0, 2)
                         .reshape(conv_w.shape[1], NDEV * conv_w.shape[2]))
            return self[key]

    p = Params({n: wts[n] for n in REPLICATED})

    groups = (("scatter_late", ("gu2t", "d2", "out", "abt", "sb")),
              ("scatter_in", ("maint", "dtt")),
              ("scatter_first", ("gu1t", "d1")))
    in_flight = []

    class Grads(dict):
        def __setitem__(self, key, value):
            dict.__setitem__(self, key, value)
            for tag, need in groups:
                if key in need and all(k in self for k in need):
                    if tag == "scatter_in":
                        gwin = _ungroup_rows("ungroup_w_in", self["maint"], self["dtt"],
                                             in_cols, in_pad, dt_off, dt_off + nh)
                        bufs, grads = ("wint",), {"wint": gwin}
                    else:
                        bufs, grads = need, self
                    srcs, lands, plan, names = scatter_plan(bufs, grads)
                    in_flight.append((tag, _split_start(tag + "_start", srcs, lands, plan),
                                      plan, names))

    loss_vec, dx, gw, gp = _local_step(x[0], loss_target[0], w, p, Grads(),
                                       aw=w_attn_branch.shape[1], dff=ffn1_w_down.shape[1] * NDEV)

    small_names = REPLICATED + ("conv_w",)
    small_shapes = [gp[n].shape for n in small_names]
    small = _pack_rows([gp[n] for n in small_names], LANE, ROW_ALIGN)
    small_plan = [(0, 0, 0, 0, 0, small.shape[0], small.shape[0])]
    small_started = _split_start("gather_small_start", [small],
                                 [lax.empty((NDEV * small.shape[0], LANE), F32)], small_plan)

    outs = [{}, {}, {}, {}]
    col_sharded_of = {n: cs for n, cs, _, _ in BIG}
    for tag, started, plan, names in in_flight:
        for n, rv in zip(names, _split_wait(tag + "_wait", started, plan)):
            rv = rv.reshape(NDEV, shard[n].shape[0], shard[n].shape[1])
            if col_sharded_of[n] and rv.shape[1] == wts[n].shape[2]:
                res = _adamw("adamw_" + n, rv, wts[n][0].T, mom[n][0].T, var[n][0].T)
                res = [a.T for a in res]
            else:
                res = _adamw("adamw_" + n, rv, wts[n][0], mom[n][0], var[n][0],
                             transposed=col_sharded_of[n])
            _Order.done(res)
            for k in range(4):
                outs[k][n] = res[k][None]

    small_all = _split_wait("gather_small_wait", small_started, small_plan)[0]
    small_all = small_all.reshape(NDEV, small.shape[0], LANE)
    small_g = _unpack_rows(_sum_slabs("sum_small_grads", small_all), small_shapes)
    small_g = dict(zip(small_names, small_g))
    cs = conv_w.shape[2]
    small_g["conv_w"] = lax.dynamic_slice_in_dim(small_g["conv_w"], my * cs, cs, axis=1)
    small_shard_shapes = [wts[n].shape[-2:] for n in small_names]
    sg = _pack_rows([small_g[n] for n in small_names], LANE, 8)
    sw = _pack_rows([wts[n] for n in small_names], LANE, 8)
    sm = _pack_rows([mom[n] for n in small_names], LANE, 8)
    sv = _pack_rows([var[n] for n in small_names], LANE, 8)
    res_small = _adamw("adamw_small", sg[None], sw, sm, sv, tr=sg.shape[0])
    for k in range(4):
        for n, a in zip(small_names, _unpack_rows(res_small[k], small_shard_shapes)):
            outs[k][n] = a.reshape(wts[n].shape)

    loss = lax.psum(0.5 * jnp.sum(loss_vec) / d, ("x", "y", "c"))
    result = [loss, dx[None]]
    for k in range(4):
        result += [outs[k][n] for n in ALL_WEIGHTS]
    return tuple(result)
```
